```python
import jax, jax.numpy as jnp
from jax import lax
import numpy as np

D_MODEL = 2048
BATCH = 8
SEQ = 8192
DEPTH = 1

N_MLA_HEADS = D_MODEL // 256
MLA_NOPE_DIM = 128
ROPE_DIM = 64
MLA_QK_DIM = MLA_NOPE_DIM + ROPE_DIM
MLA_V_DIM = 128
Q_LORA = D_MODEL // 4
KV_LORA = D_MODEL // 8
ROPE_THETA = 10000.0
Q_BLOCK = 128
MLA_WIDTH = N_MLA_HEADS * MLA_V_DIM

N_MLSTM_HEADS = 4
MLSTM_HEAD_DIM = D_MODEL // 8
MLSTM_WIDTH = N_MLSTM_HEADS * MLSTM_HEAD_DIM
MLSTM_CHUNK = 128
CONV_WIDTH = 5
N_GATE_COLS = 4 * N_MLSTM_HEADS

D_MIX = MLA_WIDTH + MLSTM_WIDTH
D_FF = 4 * D_MODEL
IN_SIZES = (Q_LORA, KV_LORA, ROPE_DIM, MLSTM_WIDTH, MLSTM_WIDTH, MLSTM_WIDTH, MLSTM_WIDTH, N_GATE_COLS)
D_IN = sum(IN_SIZES)
IN_SPLIT_POINTS = tuple(int(v) for v in np.cumsum(IN_SIZES)[:-1])
EPS = 1e-6
M_INIT = -1e30

kernel_name = "hybrid_mla_mlstm_adaln_block"


def rmsnorm(x, g):
    xf = x.astype(jnp.float32)
    y = xf * lax.rsqrt(jnp.mean(xf * xf, axis=-1, keepdims=True) + EPS)
    return (y * g.astype(jnp.float32)).astype(x.dtype)


def modulate(h, shift, scale):
    return h * (1 + scale) + shift


def rope(xp, positions):
    half = ROPE_DIM // 2
    freqs = ROPE_THETA ** (-jnp.arange(half, dtype=jnp.float32) / half)
    ang = positions.astype(jnp.float32)[..., None] * freqs
    cos = jnp.cos(ang)[:, :, None, :]
    sin = jnp.sin(ang)[:, :, None, :]
    x1 = xp[..., :half].astype(jnp.float32)
    x2 = xp[..., half:].astype(jnp.float32)
    out = jnp.concatenate([x1 * cos - x2 * sin, x1 * sin + x2 * cos], axis=-1)
    return out.astype(xp.dtype)


def dense_attention_blocks(q, k, v):
    B, S, H, Dq = q.shape
    nb = S // Q_BLOCK
    qb = q.reshape(B, nb, Q_BLOCK, H, Dq).transpose(1, 0, 2, 3, 4)
    scale = Dq ** -0.5

    def one_block(qi):
        s = jnp.einsum('bqhd,bkhd->bhqk', qi, k).astype(jnp.float32) * scale
        p = jax.nn.softmax(s, axis=-1).astype(v.dtype)
        return jnp.einsum('bhqk,bkhd->bqhd', p, v)

    o = lax.map(one_block, qb)
    return o.transpose(1, 0, 2, 3, 4).reshape(B, S, H * v.shape[-1])


def mlstm_chunkwise(q, k, v, log_i, log_f):
    B, H, S, d = q.shape
    L = MLSTM_CHUNK
    nc = S // L
    chunk = lambda t: t.reshape(B, H, nc, L, *t.shape[3:]).swapaxes(0, 2).swapaxes(1, 2)
    qc, kc, vc = chunk(q), chunk(k), chunk(v)
    ic, fc = chunk(log_i), chunk(log_f)
    tril = jnp.tril(jnp.ones((L, L), dtype=bool))

    def step(carry, inp):
        C, n, m = carry
        qt, kt, vt, it, ft = inp
        b = jnp.cumsum(ft, axis=-1)
        log_inter = b + m[..., None]
        logD = jnp.where(tril, b[..., :, None] - b[..., None, :] + it[..., None, :], -jnp.inf)
        m_t = jnp.maximum(log_inter, jnp.max(logD, axis=-1))
        Dm = jnp.exp(logD - m_t[..., None])
        w_inter = jnp.exp(log_inter - m_t)
        scores = jnp.einsum('bhtd,bhsd->bhts', qt, kt) * Dm
        num = jnp.einsum('bhts,bhsd->bhtd', scores, vt) \
            + w_inter[..., None] * jnp.einsum('bhvk,bhtk->bhtv', C, qt)
        den = jnp.sum(scores, axis=-1) + w_inter * jnp.einsum('bhtk,bhk->bht', qt, n)
        h = num / jnp.maximum(jnp.abs(den), jnp.exp(-m_t))[..., None]
        bL = b[..., -1]
        log_w = bL[..., None] - b + it
        m_new = jnp.maximum(bL + m, jnp.max(log_w, axis=-1))
        decay = jnp.exp(bL + m - m_new)
        w = jnp.exp(log_w - m_new[..., None])
        C_new = decay[..., None, None] * C + jnp.einsum('bhs,bhsv,bhsk->bhvk', w, vt, kt)
        n_new = decay[..., None] * n + jnp.einsum('bhs,bhsk->bhk', w, kt)
        return (C_new, n_new, m_new), h

    init = (jnp.zeros((B, H, d, d), jnp.float32),
            jnp.zeros((B, H, d), jnp.float32),
            jnp.full((B, H), M_INIT, jnp.float32))
    _, hs = lax.scan(step, init, (qc, kc, vc, ic, fc))
    return hs.transpose(1, 2, 0, 3, 4).reshape(B, H, S, d)


def _fwd_setup_inputs(seed: int = 0) -> dict:
    key = jax.random.key(seed)
    ks = jax.random.split(key, 24)
    f32 = jnp.float32
    nrm = lambda k, shape, fan_in, s=1.0: jax.random.normal(k, shape, f32) * (s * fan_in ** -0.5)
    gain = lambda k, shape: 1.0 + 0.05 * jax.random.normal(k, shape, f32)
    gate_base = jnp.concatenate([jnp.zeros((N_MLSTM_HEADS,), f32), jnp.linspace(3.0, 6.0, N_MLSTM_HEADS, dtype=f32),
                                 jnp.zeros((N_MLSTM_HEADS,), f32), jnp.linspace(3.0, 6.0, N_MLSTM_HEADS, dtype=f32)])
    return {
        "x": jax.random.normal(ks[0], (BATCH, SEQ, D_MODEL), f32),
        "c": jax.random.normal(ks[1], (BATCH, D_MODEL), f32),
        "positions": jnp.broadcast_to(jnp.arange(SEQ, dtype=jnp.int32), (BATCH, SEQ)),
        "w_ada": nrm(ks[2], (DEPTH, D_MODEL, 6 * D_MODEL), D_MODEL, 0.5),
        "b_ada": 0.02 * jax.random.normal(ks[3], (DEPTH, 6 * D_MODEL), f32),
        "norm_mix_g": gain(ks[4], (DEPTH, D_MODEL)),
        "w_in": nrm(ks[5], (DEPTH, D_MODEL, D_IN), D_MODEL),
        "b_gates": gate_base + 0.1 * jax.random.normal(ks[6], (DEPTH, N_GATE_COLS), f32),
        "conv_w": nrm(ks[7], (DEPTH, CONV_WIDTH, 2 * MLSTM_WIDTH), CONV_WIDTH),
        "conv_b": 0.02 * jax.random.normal(ks[8], (DEPTH, 2 * MLSTM_WIDTH), f32),
        "q_lora_g": gain(ks[9], (DEPTH, Q_LORA)),
        "w_uq": nrm(ks[10], (DEPTH, Q_LORA, N_MLA_HEADS * MLA_QK_DIM), Q_LORA),
        "kv_lora_g": gain(ks[11], (DEPTH, KV_LORA)),
        "w_ukv": nrm(ks[12], (DEPTH, KV_LORA, N_MLA_HEADS * (MLA_NOPE_DIM + MLA_V_DIM)), KV_LORA),
        "q_norm_g": gain(ks[13], (DEPTH, MLA_QK_DIM)),
        "k_norm_g": gain(ks[14], (DEPTH, MLA_QK_DIM)),
        "mlstm_norm_g": gain(ks[15], (DEPTH, N_MLSTM_HEADS, MLSTM_HEAD_DIM)),
        "w_out": nrm(ks[16], (DEPTH, D_MIX, D_MODEL), D_MIX),
        "norm_mlp_g": gain(ks[17], (DEPTH, D_MODEL)),
        "w_ff1": nrm(ks[18], (DEPTH, D_MODEL, D_FF), D_MODEL),
        "w_ff2": nrm(ks[19], (DEPTH, D_FF, D_MODEL), D_FF),
    }


def _fwd_reference(x, c, positions, w_ada, b_ada, norm_mix_g, w_in, b_gates, conv_w, conv_b,
              q_lora_g, w_uq, kv_lora_g, w_ukv, q_norm_g, k_norm_g, mlstm_norm_g,
              w_out, norm_mlp_g, w_ff1, w_ff2):
    B, S, D = x.shape
    H, HM, DM = N_MLA_HEADS, N_MLSTM_HEADS, MLSTM_HEAD_DIM
    for l in range(DEPTH):
        mod = jax.nn.silu(c) @ w_ada[l] + b_ada[l]
        shift1, scale1, gate1, shift2, scale2, gate2 = jnp.split(mod[:, None, :], 6, axis=-1)

        h = modulate(rmsnorm(x, norm_mix_g[l]), shift1, scale1)
        proj = h @ w_in[l]
        c_q, c_kv, k_pe, q_m, k_m, v_m, o_m, g_m = jnp.split(proj, IN_SPLIT_POINTS, axis=-1)

        q = (rmsnorm(c_q, q_lora_g[l]) @ w_uq[l]).reshape(B, S, H, MLA_QK_DIM)
        kv = (rmsnorm(c_kv, kv_lora_g[l]) @ w_ukv[l]).reshape(B, S, H, MLA_NOPE_DIM + MLA_V_DIM)
        k_nope, v = kv[..., :MLA_NOPE_DIM], kv[..., MLA_NOPE_DIM:]
        k_full = jnp.concatenate([k_nope, jnp.broadcast_to(k_pe[:, :, None, :], (B, S, H, ROPE_DIM))], axis=-1)
        q = rmsnorm(q, q_norm_g[l])
        k_full = rmsnorm(k_full, k_norm_g[l])
        q = jnp.concatenate([q[..., :MLA_NOPE_DIM], rope(q[..., MLA_NOPE_DIM:], positions)], axis=-1)
        k_full = jnp.concatenate([k_full[..., :MLA_NOPE_DIM], rope(k_full[..., MLA_NOPE_DIM:], positions)], axis=-1)
        attn_out = dense_attention_blocks(q, k_full, v)

        qk = jnp.concatenate([q_m, k_m], axis=-1)
        qk = lax.conv_general_dilated(qk, conv_w[l][:, None, :].astype(qk.dtype), window_strides=(1,),
                                      padding='SAME', dimension_numbers=('NWC', 'WIO', 'NWC'),
                                      feature_group_count=2 * MLSTM_WIDTH)
        qk = jax.nn.silu(qk + conv_b[l])
        to_heads = lambda t: t.reshape(B, S, HM, DM).transpose(0, 2, 1, 3).astype(jnp.float32)
        qh = to_heads(qk[..., :MLSTM_WIDTH])
        kh = to_heads(qk[..., MLSTM_WIDTH:]) * (DM ** -0.5)
        vh = to_heads(v_m)
        gates = (g_m.astype(jnp.float32) + b_gates[l].astype(jnp.float32)).reshape(B, S, 4, HM)
        gates = gates.transpose(2, 0, 3, 1)
        i_fwd, f_fwd, i_bwd, f_bwd = gates[0], gates[1], gates[2], gates[3]
        h_fwd = mlstm_chunkwise(qh, kh, vh, i_fwd, jax.nn.log_sigmoid(f_fwd))
        flip = lambda t: jnp.flip(t, axis=2)
        h_bwd = flip(mlstm_chunkwise(flip(qh), flip(kh), flip(vh), flip(i_bwd), flip(jax.nn.log_sigmoid(f_bwd))))
        hm = (h_fwd + h_bwd).transpose(0, 2, 1, 3)
        hm = rmsnorm(hm, mlstm_norm_g[l]).astype(x.dtype)
        mlstm_out = (jax.nn.sigmoid(o_m).reshape(B, S, HM, DM) * hm).reshape(B, S, MLSTM_WIDTH)

        mixed = jnp.concatenate([attn_out.astype(x.dtype), mlstm_out], axis=-1) @ w_out[l]
        x = x + gate1 * mixed

        h2 = modulate(rmsnorm(x, norm_mlp_g[l]), shift2, scale2)
        y = jnp.square(jax.nn.relu(h2 @ w_ff1[l])) @ w_ff2[l]
        x = x + gate2 * y
    return x


import jax as _jax
import jax.numpy as _jnp

TWIN_FORMAT = 'train_step'
FWD_PARAMS = ['x', 'c', 'positions', 'w_ada', 'b_ada', 'norm_mix_g', 'w_in', 'b_gates', 'conv_w', 'conv_b', 'q_lora_g', 'w_uq', 'kv_lora_g', 'w_ukv', 'q_norm_g', 'k_norm_g', 'mlstm_norm_g', 'w_out', 'norm_mlp_g', 'w_ff1', 'w_ff2']
TWIN_WEIGHTS = ['w_ada', 'b_ada', 'norm_mix_g', 'w_in', 'b_gates', 'conv_w', 'conv_b', 'q_lora_g', 'w_uq', 'kv_lora_g', 'w_ukv', 'q_norm_g', 'k_norm_g', 'mlstm_norm_g', 'w_out', 'norm_mlp_g', 'w_ff1', 'w_ff2']
TWIN_DIFF_INPUT = 'x'
TWIN_INPUTS = ['x', 'c', 'positions', 'w_ada', 'b_ada', 'norm_mix_g', 'w_in', 'b_gates', 'conv_w', 'conv_b', 'q_lora_g', 'w_uq', 'kv_lora_g', 'w_ukv', 'q_norm_g', 'k_norm_g', 'mlstm_norm_g', 'w_out', 'norm_mlp_g', 'w_ff1', 'w_ff2', 'loss_target', 'm_w_ada', 'm_b_ada', 'm_norm_mix_g', 'm_w_in', 'm_b_gates', 'm_conv_w', 'm_conv_b', 'm_q_lora_g', 'm_w_uq', 'm_kv_lora_g', 'm_w_ukv', 'm_q_norm_g', 'm_k_norm_g', 'm_mlstm_norm_g', 'm_w_out', 'm_norm_mlp_g', 'm_w_ff1', 'm_w_ff2', 'v_w_ada', 'v_b_ada', 'v_norm_mix_g', 'v_w_in', 'v_b_gates', 'v_conv_w', 'v_conv_b', 'v_q_lora_g', 'v_w_uq', 'v_kv_lora_g', 'v_w_ukv', 'v_q_norm_g', 'v_k_norm_g', 'v_mlstm_norm_g', 'v_w_out', 'v_norm_mlp_g', 'v_w_ff1', 'v_w_ff2']
TWIN_OUTPUTS = ['loss', 'grad_x', 'grad_w_ada', 'grad_b_ada', 'grad_norm_mix_g', 'grad_w_in', 'grad_b_gates', 'grad_conv_w', 'grad_conv_b', 'grad_q_lora_g', 'grad_w_uq', 'grad_kv_lora_g', 'grad_w_ukv', 'grad_q_norm_g', 'grad_k_norm_g', 'grad_mlstm_norm_g', 'grad_w_out', 'grad_norm_mlp_g', 'grad_w_ff1', 'grad_w_ff2', 'delta_w_ada', 'delta_b_ada', 'delta_norm_mix_g', 'delta_w_in', 'delta_b_gates', 'delta_conv_w', 'delta_conv_b', 'delta_q_lora_g', 'delta_w_uq', 'delta_kv_lora_g', 'delta_w_ukv', 'delta_q_norm_g', 'delta_k_norm_g', 'delta_mlstm_norm_g', 'delta_w_out', 'delta_norm_mlp_g', 'delta_w_ff1', 'delta_w_ff2', 'new_m_w_ada', 'new_m_b_ada', 'new_m_norm_mix_g', 'new_m_w_in', 'new_m_b_gates', 'new_m_conv_w', 'new_m_conv_b', 'new_m_q_lora_g', 'new_m_w_uq', 'new_m_kv_lora_g', 'new_m_w_ukv', 'new_m_q_norm_g', 'new_m_k_norm_g', 'new_m_mlstm_norm_g', 'new_m_w_out', 'new_m_norm_mlp_g', 'new_m_w_ff1', 'new_m_w_ff2', 'new_v_w_ada', 'new_v_b_ada', 'new_v_norm_mix_g', 'new_v_w_in', 'new_v_b_gates', 'new_v_conv_w', 'new_v_conv_b', 'new_v_q_lora_g', 'new_v_w_uq', 'new_v_kv_lora_g', 'new_v_w_ukv', 'new_v_q_norm_g', 'new_v_k_norm_g', 'new_v_mlstm_norm_g', 'new_v_w_out', 'new_v_norm_mlp_g', 'new_v_w_ff1', 'new_v_w_ff2']
TWIN_LEAF_KINDS = {'loss': 'loss', 'grad_x': 'grad_x', 'grad_w_ada': 'grad_w', 'grad_b_ada': 'grad_w', 'grad_norm_mix_g': 'grad_w', 'grad_w_in': 'grad_w', 'grad_b_gates': 'grad_w', 'grad_conv_w': 'grad_w', 'grad_conv_b': 'grad_w', 'grad_q_lora_g': 'grad_w', 'grad_w_uq': 'grad_w', 'grad_kv_lora_g': 'grad_w', 'grad_w_ukv': 'grad_w', 'grad_q_norm_g': 'grad_w', 'grad_k_norm_g': 'grad_w', 'grad_mlstm_norm_g': 'grad_w', 'grad_w_out': 'grad_w', 'grad_norm_mlp_g': 'grad_w', 'grad_w_ff1': 'grad_w', 'grad_w_ff2': 'grad_w', 'delta_w_ada': 'delta_w', 'delta_b_ada': 'delta_w', 'delta_norm_mix_g': 'delta_w', 'delta_w_in': 'delta_w', 'delta_b_gates': 'delta_w', 'delta_conv_w': 'delta_w', 'delta_conv_b': 'delta_w', 'delta_q_lora_g': 'delta_w', 'delta_w_uq': 'delta_w', 'delta_kv_lora_g': 'delta_w', 'delta_w_ukv': 'delta_w', 'delta_q_norm_g': 'delta_w', 'delta_k_norm_g': 'delta_w', 'delta_mlstm_norm_g': 'delta_w', 'delta_w_out': 'delta_w', 'delta_norm_mlp_g': 'delta_w', 'delta_w_ff1': 'delta_w', 'delta_w_ff2': 'delta_w', 'new_m_w_ada': 'new_m', 'new_m_b_ada': 'new_m', 'new_m_norm_mix_g': 'new_m', 'new_m_w_in': 'new_m', 'new_m_b_gates': 'new_m', 'new_m_conv_w': 'new_m', 'new_m_conv_b': 'new_m', 'new_m_q_lora_g': 'new_m', 'new_m_w_uq': 'new_m', 'new_m_kv_lora_g': 'new_m', 'new_m_w_ukv': 'new_m', 'new_m_q_norm_g': 'new_m', 'new_m_k_norm_g': 'new_m', 'new_m_mlstm_norm_g': 'new_m', 'new_m_w_out': 'new_m', 'new_m_norm_mlp_g': 'new_m', 'new_m_w_ff1': 'new_m', 'new_m_w_ff2': 'new_m', 'new_v_w_ada': 'new_v', 'new_v_b_ada': 'new_v', 'new_v_norm_mix_g': 'new_v', 'new_v_w_in': 'new_v', 'new_v_b_gates': 'new_v', 'new_v_conv_w': 'new_v', 'new_v_conv_b': 'new_v', 'new_v_q_lora_g': 'new_v', 'new_v_w_uq': 'new_v', 'new_v_kv_lora_g': 'new_v', 'new_v_w_ukv': 'new_v', 'new_v_q_norm_g': 'new_v', 'new_v_k_norm_g': 'new_v', 'new_v_mlstm_norm_g': 'new_v', 'new_v_w_out': 'new_v', 'new_v_norm_mlp_g': 'new_v', 'new_v_w_ff1': 'new_v', 'new_v_w_ff2': 'new_v'}


def _forward(args):
    return _fwd_reference(*[args[k] for k in FWD_PARAMS])


def _output_shape():
    def fwd():
        inp = _fwd_setup_inputs(0)
        return _fwd_reference(*[inp[k] for k in FWD_PARAMS])
    out = _jax.eval_shape(fwd)
    return out.shape, out.dtype

N_MICROBATCH = 1
ADAM_LR = 0.001
ADAM_B1 = 0.9
ADAM_B2 = 0.999
ADAM_EPS = 1e-08
ADAM_WD = 0.01
ADAM_STEP = 10
PER_EXAMPLE_BATCH_AXIS = {'x': 0, 'c': 0, 'positions': 0, 'loss_target': 0}
SHARED_INPUTS = []
_WEIGHT_DTYPES = {'w_ada': _jnp.float32, 'b_ada': _jnp.float32, 'norm_mix_g': _jnp.float32, 'w_in': _jnp.float32, 'b_gates': _jnp.float32, 'conv_w': _jnp.float32, 'conv_b': _jnp.float32, 'q_lora_g': _jnp.float32, 'w_uq': _jnp.float32, 'kv_lora_g': _jnp.float32, 'w_ukv': _jnp.float32, 'q_norm_g': _jnp.float32, 'k_norm_g': _jnp.float32, 'mlstm_norm_g': _jnp.float32, 'w_out': _jnp.float32, 'norm_mlp_g': _jnp.float32, 'w_ff1': _jnp.float32, 'w_ff2': _jnp.float32}
MOMENT_SCALE = {'w_ada': 2.386689e+00, 'b_ada': 6.171003e+00, 'norm_mix_g': 6.197965e-02, 'w_in': 2.713153e-01, 'b_gates': 3.162231e-01, 'conv_w': 1.107445e-02, 'conv_b': 1.100331e-02, 'q_lora_g': 1.279612e-02, 'w_uq': 6.868003e-03, 'kv_lora_g': 1.143085e+00, 'w_ukv': 2.223830e-01, 'q_norm_g': 3.475634e-02, 'k_norm_g': 3.475941e-02, 'mlstm_norm_g': 1.051317e+00, 'w_out': 4.271250e-01, 'norm_mlp_g': 1.169840e+01, 'w_ff1': 2.715707e-01, 'w_ff2': 1.188704e+00}


def _to_microbatches(a, axis):
    t = _jnp.moveaxis(a, axis, 0)
    t = t.reshape((N_MICROBATCH, t.shape[0] // N_MICROBATCH) + t.shape[1:])
    return _jnp.moveaxis(t, 1, axis + 1)


def setup_inputs(seed: int = 0) -> dict:
    inp = _fwd_setup_inputs(seed)
    key = _jax.random.fold_in(_jax.random.key(seed), 7919)
    shape, _ = _output_shape()
    out = dict(inp)
    out["loss_target"] = _jax.random.normal(_jax.random.fold_in(key, 0), shape, _jnp.float32)
    for i, name in enumerate(TWIN_WEIGHTS):
        w = inp[name].astype(_jnp.float32)
        if MOMENT_SCALE is None:
            s = _jnp.sqrt(_jnp.mean(_jnp.square(w)) + 1e-30)
        else:
            s = MOMENT_SCALE[name]
        km, kv = _jax.random.split(_jax.random.fold_in(key, i + 1))
        out[name] = w
        out["m_" + name] = s * _jax.random.normal(km, w.shape, _jnp.float32)
        out["v_" + name] = (s * s) * _jax.random.uniform(kv, w.shape, _jnp.float32, 0.5, 1.5)
    if N_MICROBATCH > 1:
        for name, axis in PER_EXAMPLE_BATCH_AXIS.items():
            out[name] = _to_microbatches(out[name], axis)
    return {'x': out['x'], 'c': out['c'], 'positions': out['positions'], 'w_ada': out['w_ada'], 'b_ada': out['b_ada'], 'norm_mix_g': out['norm_mix_g'], 'w_in': out['w_in'], 'b_gates': out['b_gates'], 'conv_w': out['conv_w'], 'conv_b': out['conv_b'], 'q_lora_g': out['q_lora_g'], 'w_uq': out['w_uq'], 'kv_lora_g': out['kv_lora_g'], 'w_ukv': out['w_ukv'], 'q_norm_g': out['q_norm_g'], 'k_norm_g': out['k_norm_g'], 'mlstm_norm_g': out['mlstm_norm_g'], 'w_out': out['w_out'], 'norm_mlp_g': out['norm_mlp_g'], 'w_ff1': out['w_ff1'], 'w_ff2': out['w_ff2'], 'loss_target': out['loss_target'], 'm_w_ada': out['m_w_ada'], 'm_b_ada': out['m_b_ada'], 'm_norm_mix_g': out['m_norm_mix_g'], 'm_w_in': out['m_w_in'], 'm_b_gates': out['m_b_gates'], 'm_conv_w': out['m_conv_w'], 'm_conv_b': out['m_conv_b'], 'm_q_lora_g': out['m_q_lora_g'], 'm_w_uq': out['m_w_uq'], 'm_kv_lora_g': out['m_kv_lora_g'], 'm_w_ukv': out['m_w_ukv'], 'm_q_norm_g': out['m_q_norm_g'], 'm_k_norm_g': out['m_k_norm_g'], 'm_mlstm_norm_g': out['m_mlstm_norm_g'], 'm_w_out': out['m_w_out'], 'm_norm_mlp_g': out['m_norm_mlp_g'], 'm_w_ff1': out['m_w_ff1'], 'm_w_ff2': out['m_w_ff2'], 'v_w_ada': out['v_w_ada'], 'v_b_ada': out['v_b_ada'], 'v_norm_mix_g': out['v_norm_mix_g'], 'v_w_in': out['v_w_in'], 'v_b_gates': out['v_b_gates'], 'v_conv_w': out['v_conv_w'], 'v_conv_b': out['v_conv_b'], 'v_q_lora_g': out['v_q_lora_g'], 'v_w_uq': out['v_w_uq'], 'v_kv_lora_g': out['v_kv_lora_g'], 'v_w_ukv': out['v_w_ukv'], 'v_q_norm_g': out['v_q_norm_g'], 'v_k_norm_g': out['v_k_norm_g'], 'v_mlstm_norm_g': out['v_mlstm_norm_g'], 'v_w_out': out['v_w_out'], 'v_norm_mlp_g': out['v_norm_mlp_g'], 'v_w_ff1': out['v_w_ff1'], 'v_w_ff2': out['v_w_ff2']}


def _loss(weights, diff, rest, loss_target):
    with _jax.named_scope("forward"):
        args = {**rest, TWIN_DIFF_INPUT: diff, **{k: w.astype(_WEIGHT_DTYPES[k]) for k, w in weights.items()}}
        y = _forward(args)
    with _jax.named_scope("loss_head"):
        err = _jnp.square(y.astype(_jnp.float32) - loss_target)
        return 0.5 * _jnp.sum(_jnp.mean(err, axis=-1)) if err.ndim else 0.5 * err


def _adamw(w, g, m, v):
    m = ADAM_B1 * m + (1.0 - ADAM_B1) * g
    v = ADAM_B2 * v + (1.0 - ADAM_B2) * _jnp.square(g)
    m_hat = m / (1.0 - ADAM_B1 ** ADAM_STEP)
    v_hat = v / (1.0 - ADAM_B2 ** ADAM_STEP)
    delta = -ADAM_LR * (m_hat / (_jnp.sqrt(v_hat) + ADAM_EPS) + ADAM_WD * w)
    return delta, m, v


def reference(x, c, positions, w_ada, b_ada, norm_mix_g, w_in, b_gates, conv_w, conv_b, q_lora_g, w_uq, kv_lora_g, w_ukv, q_norm_g, k_norm_g, mlstm_norm_g, w_out, norm_mlp_g, w_ff1, w_ff2, loss_target, m_w_ada, m_b_ada, m_norm_mix_g, m_w_in, m_b_gates, m_conv_w, m_conv_b, m_q_lora_g, m_w_uq, m_kv_lora_g, m_w_ukv, m_q_norm_g, m_k_norm_g, m_mlstm_norm_g, m_w_out, m_norm_mlp_g, m_w_ff1, m_w_ff2, v_w_ada, v_b_ada, v_norm_mix_g, v_w_in, v_b_gates, v_conv_w, v_conv_b, v_q_lora_g, v_w_uq, v_kv_lora_g, v_w_ukv, v_q_norm_g, v_k_norm_g, v_mlstm_norm_g, v_w_out, v_norm_mlp_g, v_w_ff1, v_w_ff2):
    given = dict(x=x, c=c, positions=positions, w_ada=w_ada, b_ada=b_ada, norm_mix_g=norm_mix_g, w_in=w_in, b_gates=b_gates, conv_w=conv_w, conv_b=conv_b, q_lora_g=q_lora_g, w_uq=w_uq, kv_lora_g=kv_lora_g, w_ukv=w_ukv, q_norm_g=q_norm_g, k_norm_g=k_norm_g, mlstm_norm_g=mlstm_norm_g, w_out=w_out, norm_mlp_g=norm_mlp_g, w_ff1=w_ff1, w_ff2=w_ff2, loss_target=loss_target, m_w_ada=m_w_ada, m_b_ada=m_b_ada, m_norm_mix_g=m_norm_mix_g, m_w_in=m_w_in, m_b_gates=m_b_gates, m_conv_w=m_conv_w, m_conv_b=m_conv_b, m_q_lora_g=m_q_lora_g, m_w_uq=m_w_uq, m_kv_lora_g=m_kv_lora_g, m_w_ukv=m_w_ukv, m_q_norm_g=m_q_norm_g, m_k_norm_g=m_k_norm_g, m_mlstm_norm_g=m_mlstm_norm_g, m_w_out=m_w_out, m_norm_mlp_g=m_norm_mlp_g, m_w_ff1=m_w_ff1, m_w_ff2=m_w_ff2, v_w_ada=v_w_ada, v_b_ada=v_b_ada, v_norm_mix_g=v_norm_mix_g, v_w_in=v_w_in, v_b_gates=v_b_gates, v_conv_w=v_conv_w, v_conv_b=v_conv_b, v_q_lora_g=v_q_lora_g, v_w_uq=v_w_uq, v_kv_lora_g=v_kv_lora_g, v_w_ukv=v_w_ukv, v_q_norm_g=v_q_norm_g, v_k_norm_g=v_k_norm_g, v_mlstm_norm_g=v_mlstm_norm_g, v_w_out=v_w_out, v_norm_mlp_g=v_norm_mlp_g, v_w_ff1=v_w_ff1, v_w_ff2=v_w_ff2)
    weights = {n: given[n] for n in TWIN_WEIGHTS}
    shared = {n: given[n] for n in SHARED_INPUTS}
    per_example = {n: given[n] for n in ['x', 'c', 'positions']}
    grad_fn = _jax.value_and_grad(_loss, argnums=(0, 1))

    def one_microbatch(ex, loss_target):
        ex = dict(ex)
        diff = ex.pop(TWIN_DIFF_INPUT)
        return grad_fn(weights, diff, {**shared, **ex}, loss_target)

    if N_MICROBATCH == 1:
        loss, (grad_w, grad_x) = one_microbatch(per_example, given["loss_target"])
    else:
        def body(carry, xs):
            loss_sum, grad_sum = carry
            l_k, (gw_k, gx_k) = one_microbatch(xs[0], xs[1])
            with _jax.named_scope("update"):
                return (loss_sum + l_k, _jax.tree.map(_jnp.add, grad_sum, gw_k)), gx_k

        init = (_jnp.zeros((), _jnp.float32), _jax.tree.map(_jnp.zeros_like, weights))
        (loss, grad_w), grad_x = _jax.lax.scan(body, init, (per_example, given["loss_target"]))
    with _jax.named_scope("update"):
        delta_w, new_m, new_v = {}, {}, {}
        for n in TWIN_WEIGHTS:
            delta_w[n], new_m[n], new_v[n] = _adamw(weights[n], grad_w[n], given["m_" + n], given["v_" + n])
    return (loss, grad_x, *[grad_w[n] for n in TWIN_WEIGHTS], *[delta_w[n] for n in TWIN_WEIGHTS],
            *[new_m[n] for n in TWIN_WEIGHTS], *[new_v[n] for n in TWIN_WEIGHTS])
```

```python
import functools
import math

import jax
import jax.numpy as jnp
from jax import lax
from jax.experimental import pallas as pl
from jax.experimental.pallas import tpu as pltpu

F32 = jnp.float32
BF16 = jnp.bfloat16
_MXU_DTYPE = jnp.bfloat16
_INTERPRET = False

D = 2048
H_MLA = 8
NOPE = 128
ROPE = 64
QK = NOPE + ROPE
HP = 256
VD = 128
Q_LORA = 512
KV_LORA = 256
HM = 4
DM = 256
MW = HM * DM
LCH = 128
CONVW = 5
NG = 16
DFF = 4 * D
EPS = 1e-6
M_INIT = -1e30
ROPE_THETA = 10000.0
IN_SIZES = (Q_LORA, KV_LORA, ROPE, MW, MW, MW, MW, NG)
D_IN = sum(IN_SIZES)
P_QM, P_KM, P_VM, P_OM, P_CQ, P_CKV, P_KPE, P_G = 0, 1024, 2048, 3072, 4096, 4608, 4864, 4992
D_INP = 5120

ADAM_LR, ADAM_B1, ADAM_B2, ADAM_EPS, ADAM_WD, ADAM_STEP = 0.001, 0.9, 0.999, 1e-08, 0.01, 10

V7X_VMEM_LIMIT = 56 * 1024 * 1024


def _cparams(sem):
    return pltpu.CompilerParams(dimension_semantics=sem, vmem_limit_bytes=V7X_VMEM_LIMIT)


def _call(body, **kw):
    if _INTERPRET:
        kw.pop("compiler_params", None)
        kw["interpret"] = pltpu.InterpretParams()
    return pl.pallas_call(body, **kw)


def _dot(a, b, form):
    dims = {"nn": ((1,), (0,)), "nt": ((1,), (1,)), "tn": ((0,), (0,))}[form]
    return lax.dot_general(a.astype(_MXU_DTYPE), b.astype(_MXU_DTYPE), (dims, ((), ())),
                           preferred_element_type=F32)


def _mm(a, b, form, *, name, out_dtypes=(F32,), epilogue=None, extras=(), tm=1024, tn=1024, tk=512):
    if form == "nn":
        (M, K), (K2, N) = a.shape, b.shape
    elif form == "nt":
        (M, K), (N, K2) = a.shape, b.shape
    else:
        (K, M), (K2, N) = a.shape, b.shape
    assert K == K2, (a.shape, b.shape, form)
    tm, tn, tk = min(tm, M), min(tn, N), min(tk, K)
    assert M % tm == 0 and N % tn == 0 and K % tk == 0, (M, N, K, tm, tn, tk)
    nk = K // tk
    ne, no = len(extras), len(out_dtypes)
    if form == "tn":
        a_spec = pl.BlockSpec((tk, tm), lambda i, j, k: (k, i))
    else:
        a_spec = pl.BlockSpec((tm, tk), lambda i, j, k: (i, k))
    if form == "nt":
        b_spec = pl.BlockSpec((tn, tk), lambda i, j, k: (j, k))
    else:
        b_spec = pl.BlockSpec((tk, tn), lambda i, j, k: (k, j))
    mn_spec = pl.BlockSpec((tm, tn), lambda i, j, k: (i, j))

    def body(a_ref, b_ref, *rest):
        ex, outs, acc = rest[:ne], rest[ne:ne + no], rest[ne + no]
        k = pl.program_id(2)
        prod = _dot(a_ref[...], b_ref[...], form)

        @pl.when(k == 0)
        def _():
            acc[...] = prod

        @pl.when(k > 0)
        def _():
            acc[...] += prod

        @pl.when(k == nk - 1)
        def _():
            r = acc[...]
            vals = (r,) if epilogue is None else epilogue(r, *[e[...] for e in ex])
            for o, v in zip(outs, vals):
                o[...] = v.astype(o.dtype)

    res = _call(
        body, name=name, grid=(M // tm, N // tn, nk),
        in_specs=[a_spec, b_spec] + [mn_spec] * ne,
        out_specs=[mn_spec] * no,
        out_shape=[jax.ShapeDtypeStruct((M, N), dt) for dt in out_dtypes],
        scratch_shapes=[pltpu.VMEM((tm, tn), F32)],
        compiler_params=_cparams(("parallel", "parallel", "arbitrary")),
    )(a, b, *extras)
    return res[0] if no == 1 else res


def _rowmap(fn, rows, bcasts, outs, accs=(), *, tile, name):
    rows = [r if len(r) == 4 else (*r, None) for r in rows]
    S = rows[0][0].shape[-2]
    tile = min(tile, S)
    assert S % tile == 0
    nr, nb, no, na = len(rows), len(bcasts), len(outs), len(accs)

    def body(*refs):
        vals = [r[...] for r in refs[:nr + nb]]
        o_refs, a_refs = refs[nr + nb:nr + nb + no], refs[nr + nb + no:]
        o_vals, a_vals = fn(*vals)
        for r, v in zip(o_refs, o_vals):
            r[...] = v.astype(r.dtype)
        if na:
            @pl.when(pl.program_id(0) == 0)
            def _():
                for r in a_refs:
                    r[...] = jnp.zeros(r.shape, r.dtype)
            for r, v in zip(a_refs, a_vals):
                r[...] += v

    in_specs = []
    for (arr, w, cb, lead) in rows:
        if lead is None:
            in_specs.append(pl.BlockSpec((tile, w), lambda i, cb=cb: (i, cb)))
        else:
            in_specs.append(pl.BlockSpec((None, tile, w), lambda i, cb=cb, lead=lead: (lead, i, cb)))
    in_specs += [pl.BlockSpec(b.shape, lambda i: (0, 0)) for b in bcasts]
    out_specs = [pl.BlockSpec((tile, w), lambda i: (i, 0)) for (w, _) in outs]
    out_specs += [pl.BlockSpec(s, lambda i: (0, 0)) for s in accs]
    out_shape = [jax.ShapeDtypeStruct((S, w), dt) for (w, dt) in outs]
    out_shape += [jax.ShapeDtypeStruct(s, F32) for s in accs]
    return _call(
        body, name=name, grid=(S // tile,), in_specs=in_specs, out_specs=out_specs, out_shape=out_shape,
        compiler_params=_cparams(("arbitrary",)),
    )(*[r[0] for r in rows], *bcasts)


def _colsum(v):
    return jnp.sum(v, axis=0, keepdims=True)


def _rms(x, n):
    r = lax.rsqrt(jnp.sum(x * x, axis=-1, keepdims=True) * (1.0 / n) + EPS)
    return x * r, r


def _rms_bwd(dxhat, xhat, r, n):
    return r * (dxhat - xhat * (jnp.sum(dxhat * xhat, axis=-1, keepdims=True) * (1.0 / n)))


def _rope_fwd(r, cosp, s1, s2):
    return r * cosp + pltpu.roll(r, 32, 1) * s1 + pltpu.roll(r, 96, 1) * s2


def _rope_bwd(d, cosp, s1, s2):
    return d * cosp + pltpu.roll(d * s1, 96, 1) + pltpu.roll(d * s2, 32, 1)


def _sigmoid(x):
    return 1.0 / (1.0 + jnp.exp(-x))


def _halo_specs(tile, halo, width, cb, S, lead=None):
    nh = tile // halo
    last = S // halo - 1
    if lead is None:
        return [
            pl.BlockSpec((tile, width), lambda i: (i, cb)),
            pl.BlockSpec((halo, width), lambda i: (jnp.maximum(i * nh - 1, 0), cb)),
            pl.BlockSpec((halo, width), lambda i: (jnp.minimum((i + 1) * nh, last), cb)),
        ]
    return [
        pl.BlockSpec((None, tile, width), lambda i: (lead, i, cb)),
        pl.BlockSpec((None, halo, width), lambda i: (lead, jnp.maximum(i * nh - 1, 0), cb)),
        pl.BlockSpec((None, halo, width), lambda i: (lead, jnp.minimum((i + 1) * nh, last), cb)),
    ]


def _conv_fwd(proj, conv_w8, conv_b, *, tile=256):
    S = proj.shape[0]
    T = min(tile, S)
    n = S // T
    W = 2 * MW

    def body(x_ref, xp_ref, xn_ref, w_ref, b_ref, q_ref, k_ref, ext):
        i = pl.program_id(0)
        ext[pl.ds(0, 8), :] = xp_ref[...] * (i > 0).astype(F32)
        ext[pl.ds(8, T), :] = x_ref[...]
        ext[pl.ds(8 + T, 8), :] = xn_ref[...] * (i < n - 1).astype(F32)
        w = w_ref[...]
        y = b_ref[...] + w[0:1, :] * ext[pl.ds(6, T), :]
        for o in range(1, CONVW):
            y = y + w[o:o + 1, :] * ext[pl.ds(6 + o, T), :]
        y = y * _sigmoid(y)
        q_ref[...] = y[:, :MW].astype(q_ref.dtype)
        k_ref[...] = (y[:, MW:] * (DM ** -0.5)).astype(k_ref.dtype)

    return _call(
        body, name="conv_fwd", grid=(n,),
        in_specs=_halo_specs(T, 8, W, 0, S) + [pl.BlockSpec((8, W), lambda i: (0, 0)),
                                                 pl.BlockSpec((1, W), lambda i: (0, 0))],
        out_specs=[pl.BlockSpec((T, MW), lambda i: (i, 0))] * 2,
        out_shape=[jax.ShapeDtypeStruct((S, MW), _MXU_DTYPE)] * 2,
        scratch_shapes=[pltpu.VMEM((T + 16, W), F32)],
        compiler_params=_cparams(("arbitrary",)),
    )(proj, proj, proj, conv_w8, conv_b)


def _conv_bwd(proj, dqd, dkd, conv_w8, conv_b, *, tile=256):
    S = proj.shape[0]
    T = min(tile, S)
    n = S // T
    W = 2 * MW

    def body(x_ref, xp_ref, xn_ref, *rest):
        g = rest[:12]
        w_ref, b_ref, dx_ref, dw_ref, db_ref, ext, edp = rest[12:]
        i = pl.program_id(0)
        mp = (i > 0).astype(F32)
        mn = (i < n - 1).astype(F32)
        ext[pl.ds(0, 16), :] = xp_ref[...] * mp
        ext[pl.ds(16, T), :] = x_ref[...]
        ext[pl.ds(16 + T, 16), :] = xn_ref[...] * mn
        w = w_ref[...]
        pre = b_ref[...] + w[0:1, :] * ext[pl.ds(6, T + 16), :]
        for o in range(1, CONVW):
            pre = pre + w[o:o + 1, :] * ext[pl.ds(6 + o, T + 16), :]
        sg = _sigmoid(pre)
        dsilu = sg * (1.0 + pre * (1.0 - sg))
        for half, (a0, a1) in enumerate(((g[0:3], g[3:6]), (g[6:9], g[9:12]))):
            sc = 1.0 if half == 0 else DM ** -0.5
            cols = pl.ds(half * MW, MW)
            edp[pl.ds(0, 8), cols] = (a0[1][...] + a1[1][...]) * (mp * sc)
            edp[pl.ds(8, T), cols] = (a0[0][...] + a1[0][...]) * sc
            edp[pl.ds(8 + T, 8), cols] = (a0[2][...] + a1[2][...]) * (mn * sc)
        edp[...] = edp[...] * dsilu
        dpm = edp[pl.ds(8, T), :]
        dx = w[0:1, :] * edp[pl.ds(10, T), :]
        for o in range(1, CONVW):
            dx = dx + w[o:o + 1, :] * edp[pl.ds(10 - o, T), :]
        dx_ref[...] = dx.astype(dx_ref.dtype)

        @pl.when(i == 0)
        def _():
            dw_ref[...] = jnp.zeros(dw_ref.shape, F32)
            db_ref[...] = jnp.zeros(db_ref.shape, F32)

        for o in range(CONVW):
            dw_ref[pl.ds(o, 1), :] += _colsum(ext[pl.ds(14 + o, T), :] * dpm)
        db_ref[...] += _colsum(dpm)

    gspecs = []
    for arr in (dqd, dkd):
        for d in (0, 1):
            gspecs += _halo_specs(T, 8, MW, 0, S, lead=d)
    return _call(
        body, name="conv_bwd", grid=(n,),
        in_specs=_halo_specs(T, 16, W, 0, S) + gspecs + [pl.BlockSpec((8, W), lambda i: (0, 0)),
                                                          pl.BlockSpec((1, W), lambda i: (0, 0))],
        out_specs=[pl.BlockSpec((T, W), lambda i: (i, 0)), pl.BlockSpec((8, W), lambda i: (0, 0)),
                   pl.BlockSpec((1, W), lambda i: (0, 0))],
        out_shape=[jax.ShapeDtypeStruct((S, W), _MXU_DTYPE), jax.ShapeDtypeStruct((8, W), F32),
                   jax.ShapeDtypeStruct((1, W), F32)],
        scratch_shapes=[pltpu.VMEM((T + 32, W), F32), pltpu.VMEM((T + 16, W), F32)],
        compiler_params=_cparams(("arbitrary",)),
    )(proj, proj, proj, *([dqd] * 6), *([dkd] * 6), conv_w8, conv_b)


_ATT_SCALE = QK ** -0.5


def _attn_fwd(q, k, v, *, tq=512, tk=512):
    S = q.shape[0]
    tq, tk = min(tq, S), min(tk, S)
    nkv = S // tk

    def body(q_ref, k_ref, v_ref, o_ref, lse_ref, m_s, l_s, acc_s):
        qv = q_ref[...]
        m_s[...] = jnp.full(m_s.shape, -1e30, F32)
        l_s[...] = jnp.zeros(l_s.shape, F32)
        acc_s[...] = jnp.zeros(acc_s.shape, F32)

        def step(j, carry):
            rows = pl.ds(pl.multiple_of(j * tk, tk), tk)
            s = _dot(qv, k_ref[rows, :], "nt") * _ATT_SCALE
            m_old = m_s[...]
            m_new = jnp.maximum(m_old, jnp.max(s, axis=1, keepdims=True))
            p = jnp.exp(s - m_new)
            alpha = jnp.exp(m_old - m_new)
            l_s[...] = alpha * l_s[...] + jnp.sum(p, axis=1, keepdims=True)
            acc_s[...] = alpha * acc_s[...] + _dot(p, v_ref[rows, :], "nn")
            m_s[...] = m_new
            return carry

        lax.fori_loop(0, nkv, step, 0)
        o_ref[...] = (acc_s[...] / l_s[...]).astype(o_ref.dtype)
        lse_ref[...] = m_s[...] + jnp.log(l_s[...])

    return _call(
        body, name="attn_fwd", grid=(H_MLA, S // tq),
        in_specs=[pl.BlockSpec((tq, HP), lambda h, i: (i, h)),
                  pl.BlockSpec((S, HP), lambda h, i: (0, h)),
                  pl.BlockSpec((S, VD), lambda h, i: (0, h))],
        out_specs=[pl.BlockSpec((tq, VD), lambda h, i: (i, h)),
                   pl.BlockSpec((None, tq, 1), lambda h, i: (h, i, 0))],
        out_shape=[jax.ShapeDtypeStruct((S, H_MLA * VD), _MXU_DTYPE),
                   jax.ShapeDtypeStruct((H_MLA, S, 1), F32)],
        scratch_shapes=[pltpu.VMEM((tq, 1), F32), pltpu.VMEM((tq, 1), F32), pltpu.VMEM((tq, VD), F32)],
        compiler_params=_cparams(("parallel", "arbitrary")),
    )(q, k, v)


def _attn_bwd_dkv(q, k, v, dcat, lse_r, dl_r, *, tq=512, tk=512):
    S = q.shape[0]
    tq, tk = min(tq, S), min(tk, S)
    nq = S // tq

    def body(q_ref, k_ref, v_ref, do_ref, lse_ref, dl_ref, dk_ref, dv_ref):
        kb, vb = k_ref[...], v_ref[...]
        dk_ref[...] = jnp.zeros(dk_ref.shape, F32)
        dv_ref[...] = jnp.zeros(dv_ref.shape, F32)

        def step(i, carry):
            rows = pl.ds(pl.multiple_of(i * tq, tq), tq)
            qi, doi = q_ref[rows, :], do_ref[rows, :]
            st = _dot(kb, qi, "nt") * _ATT_SCALE
            pt = jnp.exp(st - lse_ref[i])
            dv_ref[...] += _dot(pt, doi, "nn")
            dpt = _dot(vb, doi, "nt")
            dst = pt * (dpt - dl_ref[i]) * _ATT_SCALE
            dk_ref[...] += _dot(dst, qi, "nn")
            return carry

        lax.fori_loop(0, nq, step, 0)

    return _call(
        body, name="attn_bwd_dkv", grid=(H_MLA, S // tk),
        in_specs=[pl.BlockSpec((S, HP), lambda h, j: (0, h)),
                  pl.BlockSpec((tk, HP), lambda h, j: (j, h)),
                  pl.BlockSpec((tk, VD), lambda h, j: (j, h)),
                  pl.BlockSpec((S, VD), lambda h, j: (0, h)),
                  pl.BlockSpec((None, nq, 1, tq), lambda h, j: (h, 0, 0, 0)),
                  pl.BlockSpec((None, nq, 1, tq), lambda h, j: (h, 0, 0, 0))],
        out_specs=[pl.BlockSpec((tk, HP), lambda h, j: (j, h)),
                   pl.BlockSpec((tk, VD), lambda h, j: (j, h))],
        out_shape=[jax.ShapeDtypeStruct((S, H_MLA * HP), F32), jax.ShapeDtypeStruct((S, H_MLA * VD), F32)],
        compiler_params=_cparams(("parallel", "arbitrary")),
    )(q, k, v, dcat, lse_r, dl_r)


def _attn_bwd_dq(q, k, v, dcat, lse_c, dl_c, *, tq=512, tk=512):
    S = q.shape[0]
    tq, tk = min(tq, S), min(tk, S)
    nkv = S // tk

    def body(q_ref, k_ref, v_ref, do_ref, lse_ref, dl_ref, dq_ref):
        qv, dov = q_ref[...], do_ref[...]
        lse, dl = lse_ref[...], dl_ref[...]
        dq_ref[...] = jnp.zeros(dq_ref.shape, F32)

        def step(j, carry):
            rows = pl.ds(pl.multiple_of(j * tk, tk), tk)
            kj, vj = k_ref[rows, :], v_ref[rows, :]
            s = _dot(qv, kj, "nt") * _ATT_SCALE
            p = jnp.exp(s - lse)
            dp = _dot(dov, vj, "nt")
            ds = p * (dp - dl) * _ATT_SCALE
            dq_ref[...] += _dot(ds, kj, "nn")
            return carry

        lax.fori_loop(0, nkv, step, 0)

    return _call(
        body, name="attn_bwd_dq", grid=(H_MLA, S // tq),
        in_specs=[pl.BlockSpec((tq, HP), lambda h, i: (i, h)),
                  pl.BlockSpec((S, HP), lambda h, i: (0, h)),
                  pl.BlockSpec((S, VD), lambda h, i: (0, h)),
                  pl.BlockSpec((tq, VD), lambda h, i: (i, h)),
                  pl.BlockSpec((None, tq, 1), lambda h, i: (h, i, 0)),
                  pl.BlockSpec((None, tq, 1), lambda h, i: (h, i, 0))],
        out_specs=pl.BlockSpec((tq, HP), lambda h, i: (i, h)),
        out_shape=jax.ShapeDtypeStruct((S, H_MLA * HP), F32),
        compiler_params=_cparams(("parallel", "arbitrary")),
    )(q, k, v, dcat, lse_c, dl_c)


def _mlstm_chunk_terms(g, q, k, v, gates, gates_t, bg_row, C, n_row, m):
    L = LCH
    d = g // HM
    h = g % HM
    i_idx = d * 8 + h
    f_idx = d * 8 + 4 + h
    rr = lax.broadcasted_iota(jnp.int32, (L, L), 0)
    cc = lax.broadcasted_iota(jnp.int32, (L, L), 1)
    order = (rr - cc) * (1 - 2 * d)
    tri = order >= 0
    eye = rr == cc
    lane = lax.broadcasted_iota(jnp.int32, gates.shape, 1)
    sub = lax.broadcasted_iota(jnp.int32, gates_t.shape, 0)
    lane_b = lax.broadcasted_iota(jnp.int32, bg_row.shape, 1)
    pick_c = lambda idx: jnp.sum(jnp.where(lane == idx, gates, 0.0), axis=1, keepdims=True)
    pick_r = lambda idx: jnp.sum(jnp.where(sub == idx, gates_t, 0.0), axis=0, keepdims=True)
    pick_b = lambda idx: jnp.sum(jnp.where(lane_b == idx, bg_row, 0.0), axis=1, keepdims=True)
    i_col, i_row = pick_c(i_idx) + pick_b(i_idx), pick_r(i_idx) + pick_b(i_idx)
    f_col, f_row = pick_c(f_idx) + pick_b(f_idx), pick_r(f_idx) + pick_b(f_idx)
    logsig = lambda x: jnp.minimum(x, 0.0) - jnp.log(1.0 + jnp.exp(-jnp.abs(x)))
    lf_col, lf_row = logsig(f_col), logsig(f_row)
    b_col = jnp.sum(jnp.where(tri, lf_row, 0.0), axis=1, keepdims=True)
    tri_t = order <= 0
    b_row = jnp.sum(jnp.where(tri_t, lf_col, 0.0), axis=0, keepdims=True)
    bL = jnp.sum(lf_row, axis=1, keepdims=True)
    log_inter = b_col + m
    logD = jnp.where(tri, b_col - b_row + i_row, -jnp.inf)
    m_t = jnp.maximum(log_inter, jnp.max(logD, axis=1, keepdims=True))
    Dm = jnp.exp(logD - m_t)
    w_inter = jnp.exp(log_inter - m_t)
    A = _dot(q, k, "nt")
    Sc = A * Dm
    numI = _dot(q, C, "nt")
    qf = q.astype(F32)
    kf = k.astype(F32)
    denI = jnp.sum(qf * n_row, axis=1, keepdims=True)
    num = _dot(Sc, v, "nn") + w_inter * numI
    den = jnp.sum(Sc, axis=1, keepdims=True) + w_inter * denI
    floor = jnp.exp(-m_t)
    Nst = jnp.maximum(jnp.abs(den), floor)
    log_w = bL - b_col + i_col
    m_new = jnp.maximum(bL + m, jnp.max(log_w, axis=0, keepdims=True))
    decay = jnp.exp(bL + m - m_new)
    w_col = jnp.exp(log_w - m_new)
    return dict(tri=tri, eye=eye, f_row=f_row, Dm=Dm, w_inter=w_inter, A=A, Sc=Sc, numI=numI, denI=denI,
                num=num, den=den, floor=floor, Nst=Nst, m_new=m_new, decay=decay, w_col=w_col, qf=qf, kf=kf)


def _mlstm_specs(S, nc):
    jeff = lambda g, j: j + (g // HM) * (nc - 1 - 2 * j)
    return jeff, [
        pl.BlockSpec((LCH, DM), lambda g, j: (jeff(g, j), g % HM)),
        pl.BlockSpec((LCH, DM), lambda g, j: (jeff(g, j), g % HM)),
        pl.BlockSpec((LCH, DM), lambda g, j: (jeff(g, j), P_VM // DM + g % HM)),
        pl.BlockSpec((LCH, 128), lambda g, j: (jeff(g, j), P_G // 128)),
        pl.BlockSpec((NG, LCH), lambda g, j: (0, jeff(g, j))),
        pl.BlockSpec((1, 128), lambda g, j: (0, 0)),
    ]


def _mlstm_fwd(qc, kc, proj, gates_t, bg_row):
    S = qc.shape[0]
    nc = S // LCH
    jeff, specs = _mlstm_specs(S, nc)

    def body(q_ref, k_ref, v_ref, g_ref, gt_ref, bg_ref, h_ref, cst_ref, nm_ref, C_s, n_s, m_s):
        g = pl.program_id(0)

        @pl.when(pl.program_id(1) == 0)
        def _():
            C_s[...] = jnp.zeros(C_s.shape, F32)
            n_s[...] = jnp.zeros(n_s.shape, F32)
            m_s[...] = jnp.full(m_s.shape, M_INIT, F32)

        C, n_row, m = C_s[...], n_s[0:1, :], m_s[0:1, 0:1]
        cst_ref[...] = C
        nm_ref[0:1, :] = n_row
        nm_ref[1:2, :] = jnp.broadcast_to(m, (1, DM))
        q, k, v = q_ref[...], k_ref[...], v_ref[...]
        t = _mlstm_chunk_terms(g, q, k, v, g_ref[...], gt_ref[...], bg_ref[...], C, n_row, m)
        h_ref[...] = t["num"] / t["Nst"]
        wv = t["w_col"] * v
        C_s[...] = t["decay"] * C + _dot(wv, k, "tn")
        n_s[0:1, :] = t["decay"] * n_row + _colsum(t["w_col"] * t["kf"])
        m_s[...] = jnp.broadcast_to(t["m_new"], m_s.shape)

    return _call(
        body, name="mlstm_fwd", grid=(2 * HM, nc), in_specs=specs,
        out_specs=[pl.BlockSpec((None, LCH, DM), lambda g, j: (g // HM, jeff(g, j), g % HM)),
                   pl.BlockSpec((None, None, DM, DM), lambda g, j: (g, jeff(g, j), 0, 0)),
                   pl.BlockSpec((None, None, 8, DM), lambda g, j: (g, jeff(g, j), 0, 0))],
        out_shape=[jax.ShapeDtypeStruct((2, S, MW), F32),
                   jax.ShapeDtypeStruct((2 * HM, nc, DM, DM), F32),
                   jax.ShapeDtypeStruct((2 * HM, nc, 8, DM), F32)],
        scratch_shapes=[pltpu.VMEM((DM, DM), F32), pltpu.VMEM((8, DM), F32), pltpu.VMEM((8, 128), F32)],
        compiler_params=_cparams(("parallel", "arbitrary")),
    )(qc, kc, proj, proj, gates_t, bg_row)


def _mlstm_bwd(qc, kc, proj, gates_t, bg_row, dh, cst, nm):
    S = qc.shape[0]
    nc = S // LCH
    _, specs = _mlstm_specs(S, nc)
    jb = lambda g, j: (nc - 1 - j) + (g // HM) * (2 * j - (nc - 1))
    respec = lambda bs: pl.BlockSpec(bs.block_shape, (lambda g, j, im=bs.index_map: im(g, nc - 1 - j)))
    specs = [respec(s) for s in specs]

    def body(q_ref, k_ref, v_ref, g_ref, gt_ref, bg_ref, dh_ref, cst_ref, nm_ref,
             dq_ref, dk_ref, dv_ref, dg_ref, dC_s, dn_s):
        g = pl.program_id(0)

        @pl.when(pl.program_id(1) == 0)
        def _():
            dC_s[...] = jnp.zeros(dC_s.shape, F32)
            dn_s[...] = jnp.zeros(dn_s.shape, F32)

        C, n_row, m = cst_ref[...], nm_ref[0:1, :], nm_ref[1:2, 0:1]
        q, k, v = q_ref[...], k_ref[...], v_ref[...]
        t = _mlstm_chunk_terms(g, q, k, v, g_ref[...], gt_ref[...], bg_ref[...], C, n_row, m)
        tri, eye, qf, kf = t["tri"], t["eye"], t["qf"], t["kf"]
        w_inter, w_col, decay, Nst = t["w_inter"], t["w_col"], t["decay"], t["Nst"]
        dC, dn = dC_s[...], dn_s[0:1, :]
        dhv = dh_ref[...]
        hval = t["num"] / Nst
        dnum = dhv / Nst
        dNst = -jnp.sum(dhv * hval, axis=1, keepdims=True) / Nst
        dden = jnp.where(jnp.abs(t["den"]) > t["floor"], jnp.sign(t["den"]) * dNst, 0.0)
        dSc = _dot(dnum, v, "nt") + dden
        dA = dSc * t["Dm"]
        G = dSc * t["Sc"]
        KdC = _dot(k, dC, "nt")
        dq = _dot(dA, k, "nn") + w_inter * _dot(dnum, C, "nn") + (w_inter * dden) * n_row
        dk = _dot(dA, q, "tn") + w_col * _dot(v, dC, "nn") + w_col * dn
        dv = _dot(t["Sc"], dnum, "tn") + w_col * KdC
        dq_ref[...] = dq
        dk_ref[...] = dk
        dv_ref[...] = dv
        dlog_inter = w_inter * (jnp.sum(dnum * t["numI"], axis=1, keepdims=True) + dden * t["denI"])
        rowG = jnp.sum(G, axis=1, keepdims=True)
        colG = jnp.sum(G, axis=0, keepdims=True)
        u_col = w_col * (jnp.sum(v * KdC, axis=1, keepdims=True) + jnp.sum(kf * dn, axis=1, keepdims=True))
        colG_c = jnp.sum(jnp.where(eye, colG, 0.0), axis=1, keepdims=True)
        u_row = jnp.sum(jnp.where(eye, u_col, 0.0), axis=0, keepdims=True)
        db_col = rowG + dlog_inter - u_col - colG_c
        dbL = jnp.sum(u_col, axis=0, keepdims=True) + decay * (
            jnp.sum(jnp.sum(dC * C, axis=1, keepdims=True), axis=0, keepdims=True)
            + jnp.sum(dn * n_row, axis=1, keepdims=True))
        dlf_row = jnp.sum(jnp.where(tri, db_col, 0.0), axis=0, keepdims=True) + dbL
        di_row = colG + u_row
        df_row = dlf_row * (1.0 - _sigmoid(t["f_row"]))
        dg_ref[...] = jnp.zeros(dg_ref.shape, F32)
        dg_ref[0:1, :] = di_row
        dg_ref[1:2, :] = df_row
        dC_s[...] = decay * dC + _dot(w_inter * dnum, q, "tn")
        dn_s[0:1, :] = decay * dn + _colsum((w_inter * dden) * qf)

    return _call(
        body, name="mlstm_bwd", grid=(2 * HM, nc),
        in_specs=specs + [pl.BlockSpec((LCH, DM), lambda g, j: (jb(g, j), g % HM)),
                          pl.BlockSpec((None, None, DM, DM), lambda g, j: (g, jb(g, j), 0, 0)),
                          pl.BlockSpec((None, None, 8, DM), lambda g, j: (g, jb(g, j), 0, 0))],
        out_specs=[pl.BlockSpec((None, LCH, DM), lambda g, j: (g // HM, jb(g, j), g % HM))] * 3
        + [pl.BlockSpec((None, None, 8, LCH), lambda g, j: (g, jb(g, j), 0, 0))],
        out_shape=[jax.ShapeDtypeStruct((2, S, MW), F32)] * 3
        + [jax.ShapeDtypeStruct((2 * HM, nc, 8, LCH), F32)],
        scratch_shapes=[pltpu.VMEM((DM, DM), F32), pltpu.VMEM((8, DM), F32)],
        compiler_params=_cparams(("parallel", "arbitrary")),
    )(qc, kc, proj, proj, gates_t, bg_row, dh, cst, nm)


def _pad_w_in(w):
    cq, ckv, kpe, qm, km, vm, om, gt = _split_in(w)
    z = lambda n: jnp.zeros((w.shape[0], n), w.dtype)
    return jnp.concatenate([qm, km, vm, om, cq, ckv, kpe, z(HP - QK), gt, z(128 - NG)], axis=1)


def _split_in(w):
    out, o = [], 0
    for n in IN_SIZES:
        out.append(w[:, o:o + n])
        o += n
    return out


def _unpad_w_in(g):
    return jnp.concatenate([g[:, P_CQ:P_CQ + Q_LORA], g[:, P_CKV:P_CKV + KV_LORA], g[:, P_KPE:P_KPE + ROPE],
                            g[:, 0:4 * MW], g[:, P_G:P_G + NG]], axis=1)


def _pad_w_uq(w):
    return jnp.pad(w.reshape(Q_LORA, H_MLA, QK), ((0, 0), (0, 0), (0, HP - QK))).reshape(Q_LORA, H_MLA * HP)


def _unpad_w_uq(g):
    return g.reshape(Q_LORA, H_MLA, HP)[:, :, :QK].reshape(Q_LORA, H_MLA * QK)


def _perm_w_ukv(w):
    return w.reshape(KV_LORA, H_MLA, 2, NOPE).transpose(0, 2, 1, 3).reshape(KV_LORA, 2 * H_MLA * NOPE)


def _unperm_w_ukv(g):
    return g.reshape(KV_LORA, 2, H_MLA, NOPE).transpose(0, 2, 1, 3).reshape(KV_LORA, 2 * H_MLA * NOPE)


def _rope_tables(positions):
    half = ROPE // 2
    freqs = ROPE_THETA ** (-jnp.arange(half, dtype=F32) / half)
    ang = positions.astype(F32)[:, None] * freqs
    cos, sin = jnp.cos(ang), jnp.sin(ang)
    z32, z64 = jnp.zeros_like(cos), jnp.zeros((cos.shape[0], 64), F32)
    return (jnp.concatenate([cos, cos, z64], axis=1), jnp.concatenate([z32, sin, z64], axis=1),
            jnp.concatenate([-sin, z32, z64], axis=1))


def _device_step(x, tgt, positions, modv, W):
    S = x.shape[0]
    MX = _MXU_DTYPE
    cosp, rs1, rs2 = _rope_tables(positions)
    tabs = [(cosp, 128, 0), (rs1, 128, 0), (rs2, 128, 0)]
    cat1 = lambda vs: jnp.concatenate(vs, axis=1)
    hsl = lambda hh, w: slice(hh * w, (hh + 1) * w)

    def ln1(xv, g, mv):
        xhat, _ = _rms(xv, D)
        return [xhat * g * (1.0 + mv[1:2]) + mv[0:1]], []

    (h,) = _rowmap(ln1, [(x, D, 0)], [W["g_mix"], modv], [(D, MX)], tile=128, name="ln1")
    proj = _mm(h, W["w_in"], "nn", name="proj")

    def lora(cq, ckv, gq, gkv):
        return [_rms(cq, Q_LORA)[0] * gq, _rms(ckv, KV_LORA)[0] * gkv], []

    cqn, ckvn = _rowmap(lora, [(proj, Q_LORA, P_CQ // Q_LORA), (proj, KV_LORA, P_CKV // KV_LORA)],
                        [W["g_qlora"], W["g_kvlora"]], [(Q_LORA, MX), (KV_LORA, MX)], tile=256, name="lora_norm")
    q_raw = _mm(cqn, W["w_uq"], "nn", name="q_up")
    kv_raw = _mm(ckvn, W["w_ukv"], "nn", name="kv_up")

    def mla_q(qr, cp, a1, a2, gq):
        outs = []
        for hh in range(H_MLA):
            y = _rms(qr[:, hsl(hh, HP)], QK)[0] * gq
            outs += [y[:, :NOPE], _rope_fwd(y[:, NOPE:], cp, a1, a2)]
        return [cat1(outs)], []

    (qh,) = _rowmap(mla_q, [(q_raw, H_MLA * HP, 0)] + tabs, [W["gq"]], [(H_MLA * HP, MX)], tile=128, name="mla_q")

    def mla_k(kvr, kpe, cp, a1, a2, gk):
        outs = []
        for hh in range(H_MLA):
            y = _rms(cat1([kvr[:, hsl(hh, NOPE)], kpe]), QK)[0] * gk
            outs += [y[:, :NOPE], _rope_fwd(y[:, NOPE:], cp, a1, a2)]
        return [cat1(outs), kvr[:, H_MLA * NOPE:]], []

    kh, vh = _rowmap(mla_k, [(kv_raw, 2 * H_MLA * NOPE, 0), (proj, 128, P_KPE // 128)] + tabs, [W["gk"]],
                     [(H_MLA * HP, MX), (H_MLA * VD, MX)], tile=128, name="mla_k")
    attn_o, lse = _attn_fwd(qh, kh, vh)

    qc, kc = _conv_fwd(proj, W["conv_w8"], W["conv_b"])
    gates_t = proj[:, P_G:P_G + NG].T
    h_dirs, cst, nm = _mlstm_fwd(qc, kc, proj, gates_t, W["bg_row"])
    hrows = [(h_dirs, MW, 0, 0), (h_dirs, MW, 0, 1), (proj, MW, P_OM // MW)]

    def ml_out(ao, hf, hb, om, gmn):
        outs = [ao.astype(F32)]
        hs = hf + hb
        for hh in range(HM):
            sl = hsl(hh, DM)
            outs.append(_sigmoid(om[:, sl]) * _rms(hs[:, sl], DM)[0] * gmn[:, sl])
        return [cat1(outs)], []

    (cat,) = _rowmap(ml_out, [(attn_o, MW, 0)] + hrows, [W["g_mn"]], [(D, MX)], tile=128, name="ml_out")
    mixed = _mm(cat, W["w_out"], "nn", name="out_proj")

    def res_ln2(xv, mx, g, mv):
        x1 = xv + mv[2:3] * mx
        return [x1, _rms(x1, D)[0] * g * (1.0 + mv[4:5]) + mv[3:4]], []

    x1, h2 = _rowmap(res_ln2, [(x, D, 0), (mixed, D, 0)], [W["g_mlp"], modv], [(D, F32), (D, MX)],
                     tile=128, name="res_ln2")
    a, u = _mm(h2, W["w_ff1"], "nn", name="ff1", out_dtypes=(MX, MX),
               epilogue=lambda r: (jnp.square(jnp.maximum(r, 0.0)), r))
    y = _mm(a, W["w_ff2"], "nn", name="ff2")

    def final(x1v, yv, tv, mv):
        err = x1v + mv[5:6] * yv - tv
        dout = err * (1.0 / D)
        loss = jnp.sum(jnp.sum(0.5 * err * dout, axis=1, keepdims=True), axis=0, keepdims=True)
        return [dout, mv[5:6] * dout], [loss, _colsum(dout * yv)]

    dout, dy, loss, dgate2 = _rowmap(final, [(x1, D, 0), (y, D, 0), (tgt, D, 0)], [modv], [(D, F32), (D, MX)],
                                     [(1, 1), (1, D)], tile=128, name="loss_head")

    du = _mm(dy, W["w_ff2"], "nt", name="ff2_dx", out_dtypes=(MX,), extras=(u,),
             epilogue=lambda r, uv: (r * (2.0 * jnp.maximum(uv.astype(F32), 0.0)),))
    g_ff2 = _mm(a, dy, "tn", name="ff2_dw")
    dh2 = _mm(du, W["w_ff1"], "nt", name="ff1_dx")
    g_ff1 = _mm(h2, du, "tn", name="ff1_dw")

    def ln2_bwd(dh2v, x1v, doutv, mxv, g, mv):
        xhat, r = _rms(x1v, D)
        dn2 = dh2v * (1.0 + mv[4:5])
        dx1 = doutv + _rms_bwd(dn2 * g, xhat, r, D)
        return [dx1, mv[2:3] * dx1], [_colsum(dh2v), _colsum(dh2v * xhat * g), _colsum(dn2 * xhat), _colsum(dx1 * mxv)]

    dx1, dmixed, dshift2, dscale2, dg_mlp, dgate1 = _rowmap(
        ln2_bwd, [(dh2, D, 0), (x1, D, 0), (dout, D, 0), (mixed, D, 0)], [W["g_mlp"], modv],
        [(D, F32), (D, MX)], [(1, D)] * 4, tile=128, name="ln2_bwd")
    dcat = _mm(dmixed, W["w_out"], "nt", name="out_dx")
    g_out = _mm(cat, dmixed, "tn", name="out_dw")

    def ml_out_bwd(dml, hf, hb, om, gmn):
        hs = hf + hb
        dhs, dos, dgs = [], [], []
        for hh in range(HM):
            sl = hsl(hh, DM)
            xhat, r = _rms(hs[:, sl], DM)
            g, sg, d = gmn[:, sl], _sigmoid(om[:, sl]), dml[:, sl]
            dos.append(d * xhat * g * sg * (1.0 - sg))
            dhn = d * sg
            dgs.append(_colsum(dhn * xhat))
            dhs.append(_rms_bwd(dhn * g, xhat, r, DM))
        return [cat1(dhs), cat1(dos)], [cat1(dgs)]

    dhs, do_m, dg_mn = _rowmap(ml_out_bwd, [(dcat, MW, 1)] + hrows, [W["g_mn"]], [(MW, F32), (MW, MX)],
                               [(1, MW)], tile=128, name="ml_out_bwd")
    dqd, dkd, dvd, dgates = _mlstm_bwd(qc, kc, proj, gates_t, W["bg_row"], dhs, cst, nm)
    dqk_m, dconv_w8, dconv_b = _conv_bwd(proj, dqd, dkd, W["conv_w8"], W["conv_b"])

    def delta_fn(ao, dov):
        lane = lax.broadcasted_iota(jnp.int32, (ao.shape[0], 128), 1)
        acc = jnp.zeros((ao.shape[0], 128), F32)
        for hh in range(H_MLA):
            sl = hsl(hh, VD)
            acc = acc + jnp.where(lane == hh, jnp.sum(ao[:, sl].astype(F32) * dov[:, sl], axis=1, keepdims=True), 0.0)
        return [acc], []

    (dl,) = _rowmap(delta_fn, [(attn_o, MW, 0), (dcat, MW, 0)], [], [(128, F32)], tile=256, name="attn_delta")
    tq = min(512, S)
    dl_hs = dl[:, :H_MLA].T
    dk_a, dv_a = _attn_bwd_dkv(qh, kh, vh, dcat, lse.reshape(H_MLA, S // tq, 1, tq),
                               dl_hs.reshape(H_MLA, S // tq, 1, tq))
    dq_a = _attn_bwd_dq(qh, kh, vh, dcat, lse, dl_hs.reshape(H_MLA, S, 1))

    def mla_q_bwd(dqv, qr, cp, a1, a2, gq):
        outs, dg = [], 0.0
        for hh in range(H_MLA):
            sl = hsl(hh, HP)
            xhat, r = _rms(qr[:, sl], QK)
            d = dqv[:, sl]
            dyv = cat1([d[:, :NOPE], _rope_bwd(d[:, NOPE:], cp, a1, a2)])
            dg = dg + _colsum(dyv * xhat)
            outs.append(_rms_bwd(dyv * gq, xhat, r, QK))
        return [cat1(outs)], [dg]

    dq_raw, dgq = _rowmap(mla_q_bwd, [(dq_a, H_MLA * HP, 0), (q_raw, H_MLA * HP, 0)] + tabs, [W["gq"]],
                          [(H_MLA * HP, MX)], [(1, HP)], tile=128, name="mla_q_bwd")
    dcqn = _mm(dq_raw, W["w_uq"], "nt", name="q_up_dx")
    g_uq = _mm(cqn, dq_raw, "tn", name="q_up_dw")

    def mla_k_bwd(dkv, dvv, kvr, kpe, cp, a1, a2, gk):
        dkn, dg, dkpe = [], 0.0, 0.0
        for hh in range(H_MLA):
            xhat, r = _rms(cat1([kvr[:, hsl(hh, NOPE)], kpe]), QK)
            d = dkv[:, hsl(hh, HP)]
            dyv = cat1([d[:, :NOPE], _rope_bwd(d[:, NOPE:], cp, a1, a2)])
            dg = dg + _colsum(dyv * xhat)
            dxv = _rms_bwd(dyv * gk, xhat, r, QK)
            dkn.append(dxv[:, :NOPE])
            dkpe = dkpe + dxv[:, NOPE:]
        return [cat1(dkn + [dvv]), dkpe], [dg]

    dkv_raw, dkpe, dgk = _rowmap(
        mla_k_bwd, [(dk_a, H_MLA * HP, 0), (dv_a, H_MLA * VD, 0), (kv_raw, 2 * H_MLA * NOPE, 0),
                    (proj, 128, P_KPE // 128)] + tabs, [W["gk"]],
        [(2 * H_MLA * NOPE, MX), (128, MX)], [(1, HP)], tile=128, name="mla_k_bwd")
    dckvn = _mm(dkv_raw, W["w_ukv"], "nt", name="kv_up_dx")
    g_ukv = _mm(ckvn, dkv_raw, "tn", name="kv_up_dw")

    def lora_bwd(dcq, dckv, cq, ckv, gq, gkv):
        xq, rq = _rms(cq, Q_LORA)
        xk, rk = _rms(ckv, KV_LORA)
        return ([_rms_bwd(dcq * gq, xq, rq, Q_LORA), _rms_bwd(dckv * gkv, xk, rk, KV_LORA)],
                [_colsum(dcq * xq), _colsum(dckv * xk)])

    dc_q, dc_kv, dg_qlora, dg_kvlora = _rowmap(
        lora_bwd, [(dcqn, Q_LORA, 0), (dckvn, KV_LORA, 0), (proj, Q_LORA, P_CQ // Q_LORA),
                   (proj, KV_LORA, P_CKV // KV_LORA)], [W["g_qlora"], W["g_kvlora"]],
        [(Q_LORA, MX), (KV_LORA, MX)], [(1, Q_LORA), (1, KV_LORA)], tile=256, name="lora_bwd")

    nc = S // LCH
    dg16 = dgates[:, :, 0:2, :].reshape(2, HM, nc, 2, LCH).transpose(2, 4, 0, 3, 1).reshape(S, NG)
    dg128 = jnp.pad(dg16, ((0, 0), (0, 128 - NG)))

    def assemble(dqk, dv0, dv1, dom, dcq, dckv, dkp, dgp):
        f = lambda t: t.astype(F32)
        return [cat1([f(dqk), dv0 + dv1, f(dom), f(dcq), f(dckv), f(dkp), dgp])], [_colsum(dgp)]

    dproj, dbg = _rowmap(
        assemble, [(dqk_m, 2 * MW, 0), (dvd, MW, 0, 0), (dvd, MW, 0, 1), (do_m, MW, 0), (dc_q, Q_LORA, 0),
                   (dc_kv, KV_LORA, 0), (dkpe, 128, 0), (dg128, 128, 0)], [], [(D_INP, MX)], [(1, 128)],
        tile=128, name="dproj")
    dh = _mm(dproj, W["w_in"], "nt", name="proj_dx")
    g_in = _mm(h, dproj, "tn", name="proj_dw")

    def ln1_bwd(dhv, xv, dx1v, g, mv):
        xhat, r = _rms(xv, D)
        dn = dhv * (1.0 + mv[1:2])
        return [dx1v + _rms_bwd(dn * g, xhat, r, D)], [_colsum(dhv), _colsum(dhv * xhat * g), _colsum(dn * xhat)]

    gx, dshift1, dscale1, dg_mix = _rowmap(ln1_bwd, [(dh, D, 0), (x, D, 0), (dx1, D, 0)], [W["g_mix"], modv],
                                           [(D, F32)], [(1, D)] * 3, tile=128, name="ln1_bwd")
    dmodv = jnp.concatenate([dshift1, dscale1, dgate1, dshift2, dscale2, dgate2], axis=0)
    grads = dict(w_in=g_in, w_uq=g_uq, w_ukv=g_ukv, w_out=g_out, w_ff1=g_ff1, w_ff2=g_ff2,
                 norm_mix_g=dg_mix, b_gates=dbg[:, :NG], conv_w=dconv_w8[:CONVW], conv_b=dconv_b,
                 q_lora_g=dg_qlora, kv_lora_g=dg_kvlora, q_norm_g=dgq[:, :QK], k_norm_g=dgk[:, :QK],
                 mlstm_norm_g=dg_mn, norm_mlp_g=dg_mlp)
    return loss, gx, dmodv, grads


def _prep_weights(w_in, w_uq, w_ukv, w_out, w_ff1, w_ff2, norm_mix_g, norm_mlp_g, q_lora_g, kv_lora_g,
                  q_norm_g, k_norm_g, mlstm_norm_g, conv_w, conv_b, b_gates):
    MX = _MXU_DTYPE
    padg = lambda g: jnp.pad(g.reshape(1, QK).astype(F32), ((0, 0), (0, HP - QK)))
    return dict(
        w_in=_pad_w_in(w_in).astype(MX), w_uq=_pad_w_uq(w_uq).astype(MX), w_ukv=_perm_w_ukv(w_ukv).astype(MX),
        w_out=w_out.astype(MX), w_ff1=w_ff1.astype(MX), w_ff2=w_ff2.astype(MX),
        g_mix=norm_mix_g.reshape(1, D), g_mlp=norm_mlp_g.reshape(1, D), g_qlora=q_lora_g.reshape(1, Q_LORA),
        g_kvlora=kv_lora_g.reshape(1, KV_LORA), gq=padg(q_norm_g), gk=padg(k_norm_g),
        g_mn=mlstm_norm_g.reshape(1, MW), conv_w8=jnp.pad(conv_w.reshape(CONVW, 2 * MW), ((0, 8 - CONVW), (0, 0))),
        conv_b=conv_b.reshape(1, 2 * MW), bg_row=jnp.pad(b_gates.reshape(1, NG), ((0, 0), (0, 128 - NG))))


MESH = pl.DeviceIdType.MESH
N_DEV = 8
N_CHIP = 4


def _comm_call(body, **kw):
    if _INTERPRET:
        kw["interpret"] = pltpu.InterpretParams()
    return pl.pallas_call(body, **kw)


def _allgather8(blk, *, name):
    m_per, n = blk.shape

    def body(x_ref, out_ref, send_sems, recv_sems, local_sem):
        x, y, c = lax.axis_index("x"), lax.axis_index("y"), lax.axis_index("c")
        me, sibling = (x, y, c), (x, y, 1 - c)
        chips = [(1 - x, y), (x, 1 - y), (1 - x, 1 - y)]

        def rows(px, py, pc):
            return out_ref.at[pl.ds((4 * px + 2 * py + pc) * m_per, m_per), :]

        def copy(k, block, to, src=None):
            return pltpu.make_async_remote_copy(
                src_ref=rows(*block) if src is None else src, dst_ref=rows(*block),
                send_sem=send_sems.at[k], recv_sem=recv_sems.at[k], device_id=to, device_id_type=MESH)

        mine = pltpu.make_async_copy(x_ref, rows(*me), local_sem)
        mine.start()
        first = [copy(0, me, sibling, src=x_ref)]
        first += [copy(1 + j, me, (*chip, c), src=x_ref) for j, chip in enumerate(chips)]
        for cp in first:
            cp.start()
        passed = [copy(4 + j, (*chip, c), sibling) for j, chip in enumerate(chips)]
        for j, chip in enumerate(chips):
            copy(1 + j, (*chip, c), me).wait_recv()
            passed[j].start()
        copy(0, sibling, me).wait_recv()
        for j, chip in enumerate(chips):
            copy(4 + j, (*chip, 1 - c), me).wait_recv()
        for cp in first + passed:
            cp.wait_send()
        mine.wait()

    return _comm_call(
        body, name=name, out_shape=jax.ShapeDtypeStruct((N_DEV * m_per, n), blk.dtype),
        in_specs=[pl.BlockSpec(memory_space=pltpu.VMEM)], out_specs=pl.BlockSpec(memory_space=pltpu.VMEM),
        scratch_shapes=[pltpu.SemaphoreType.DMA((7,)), pltpu.SemaphoreType.DMA((7,)), pltpu.SemaphoreType.DMA],
    )(blk)


def _chip_exchange(arrays, *, gather, name):
    n = len(arrays)
    shard = (lambda a: a.shape) if gather else (lambda a: a.shape[1:])

    def body(*refs):
        ins, outs = refs[:n], refs[n:2 * n]
        send_sems, recv_sems, local_sems = refs[2 * n:]
        x, y, c = lax.axis_index("x"), lax.axis_index("y"), lax.axis_index("c")
        k = 2 * x + y
        chips = [(1 - x, y), (x, 1 - y), (1 - x, 1 - y)]

        def remote(a, j):
            px, py = chips[j]
            src = ins[a] if gather else ins[a].at[2 * px + py]
            return pltpu.make_async_remote_copy(
                src_ref=src, dst_ref=outs[a].at[k], send_sem=send_sems.at[3 * a + j],
                recv_sem=recv_sems.at[3 * a + j], device_id=(px, py, c), device_id_type=MESH)

        def arrival(a, j):
            px, py = chips[j]
            src = ins[a] if gather else ins[a].at[k]
            return pltpu.make_async_remote_copy(
                src_ref=src, dst_ref=outs[a].at[2 * px + py], send_sem=send_sems.at[3 * a + j],
                recv_sem=recv_sems.at[3 * a + j], device_id=(px, py, c), device_id_type=MESH)

        local = [pltpu.make_async_copy(ins[a] if gather else ins[a].at[k], outs[a].at[k], local_sems.at[a])
                 for a in range(n)]
        for cp in local:
            cp.start()
        sent = [remote(a, j) for a in range(n) for j in range(3)]
        for cp in sent:
            cp.start()
        for a in range(n):
            for j in range(3):
                arrival(a, j).wait_recv()
        for cp in sent:
            cp.wait_send()
        for cp in local:
            cp.wait()

    hbm = pl.BlockSpec(memory_space=pltpu.HBM)
    return _comm_call(
        body, name=name,
        out_shape=[jax.ShapeDtypeStruct((N_CHIP, *shard(a)), a.dtype) for a in arrays],
        in_specs=[hbm] * n, out_specs=[hbm] * n,
        scratch_shapes=[pltpu.SemaphoreType.DMA((3 * n,)), pltpu.SemaphoreType.DMA((3 * n,)),
                        pltpu.SemaphoreType.DMA((n,))],
    )(*arrays)


def _sibling_exchange(arrays, *, name):
    n = len(arrays)

    def body(*refs):
        ins, outs = refs[:n], refs[n:2 * n]
        send_sems, recv_sems = refs[2 * n:]
        x, y, c = lax.axis_index("x"), lax.axis_index("y"), lax.axis_index("c")
        cps = [pltpu.make_async_remote_copy(
            src_ref=ins[a], dst_ref=outs[a], send_sem=send_sems.at[a], recv_sem=recv_sems.at[a],
            device_id=(x, y, 1 - c), device_id_type=MESH) for a in range(n)]
        for cp in cps:
            cp.start()
        for cp in cps:
            cp.wait()

    hbm = pl.BlockSpec(memory_space=pltpu.HBM)
    return _comm_call(
        body, name=name, out_shape=[jax.ShapeDtypeStruct(a.shape, a.dtype) for a in arrays],
        in_specs=[hbm] * n, out_specs=[hbm] * n,
        scratch_shapes=[pltpu.SemaphoreType.DMA((n,)), pltpu.SemaphoreType.DMA((n,))],
    )(*arrays)


def _sum_blocks(a, nblk, *, name):
    n = a.shape[1]

    def body(a_ref, o_ref):
        acc = a_ref[pl.ds(0, 8), :]
        for d in range(1, nblk):
            acc = acc + a_ref[pl.ds(8 * d, 8), :]
        o_ref[...] = acc

    return _call(body, name=name, out_shape=jax.ShapeDtypeStruct((8, n), F32))(a)


def _outer8(sct, dm, *, name, tm=256, tn=1024):
    R, N = sct.shape[0], dm.shape[1]
    tm, tn = min(tm, R), min(tn, N)

    def body(s_ref, d_ref, o_ref):
        s, dmv = s_ref[...], d_ref[...]
        acc = s[:, 0:1] * dmv[0:1, :]
        for b in range(1, 8):
            acc = acc + s[:, b:b + 1] * dmv[b:b + 1, :]
        o_ref[...] = acc

    return _call(
        body, name=name, grid=(R // tm, N // tn),
        in_specs=[pl.BlockSpec((tm, 8), lambda i, j: (i, 0)), pl.BlockSpec((8, tn), lambda i, j: (0, j))],
        out_specs=pl.BlockSpec((tm, tn), lambda i, j: (i, j)),
        out_shape=jax.ShapeDtypeStruct((R, N), F32),
        compiler_params=_cparams(("parallel", "parallel")),
    )(sct, dm)


_BC1 = 1.0 - ADAM_B1 ** ADAM_STEP
_BC2 = 1.0 - ADAM_B2 ** ADAM_STEP


def _adamw(w, g_parts, m, v, *, name, tile=128):
    R, C = w.shape
    tile = min(tile, R)
    assert R % tile == 0
    npart = len(g_parts)

    def body(*refs):
        w_ref, m_ref, v_ref = refs[npart:npart + 3]
        g_o, d_o, m_o, v_o = refs[npart + 3:]
        g = refs[0][...].astype(F32)
        for r in refs[1:npart]:
            g = g + r[...].astype(F32)
        mn = ADAM_B1 * m_ref[...] + (1.0 - ADAM_B1) * g
        vn = ADAM_B2 * v_ref[...] + (1.0 - ADAM_B2) * jnp.square(g)
        g_o[...] = g
        m_o[...] = mn
        v_o[...] = vn
        d_o[...] = -ADAM_LR * ((mn / _BC1) / (jnp.sqrt(vn / _BC2) + ADAM_EPS) + ADAM_WD * w_ref[...])

    spec = pl.BlockSpec((tile, C), lambda i: (i, 0))
    return _call(
        body, name=name, grid=(R // tile,), in_specs=[spec] * (npart + 3), out_specs=[spec] * 4,
        out_shape=[jax.ShapeDtypeStruct((R, C), F32)] * 4,
        compiler_params=_cparams(("parallel",)),
    )(*g_parts, w, m, v)


def _pack(vecs, rows8_cols):
    flat = jnp.concatenate([v.reshape(-1).astype(F32) for v in vecs])
    return jnp.pad(flat, (0, 8 * rows8_cols - flat.shape[0])).reshape(8, rows8_cols)


def _unpack(flat, shapes):
    out, o = [], 0
    for s in shapes:
        n = math.prod(s)
        out.append(flat[o:o + n].reshape(s))
        o += n
    return out


_BIG = ("w_in", "w_uq", "w_ukv", "w_out", "w_ff1", "w_ff2")
_SMALL = ("b_ada", "norm_mix_g", "b_gates", "conv_w", "conv_b", "q_lora_g", "kv_lora_g", "q_norm_g", "k_norm_g",
          "mlstm_norm_g", "norm_mlp_g")
_ORDER = ("w_ada", "b_ada", "norm_mix_g", "w_in", "b_gates", "conv_w", "conv_b", "q_lora_g", "w_uq", "kv_lora_g",
          "w_ukv", "q_norm_g", "k_norm_g", "mlstm_norm_g", "w_out", "norm_mlp_g", "w_ff1", "w_ff2")


def kernel(x, c, positions, w_ada, b_ada, norm_mix_g, w_in, b_gates, conv_w, conv_b, q_lora_g, w_uq, kv_lora_g, w_ukv, q_norm_g, k_norm_g, mlstm_norm_g, w_out, norm_mlp_g, w_ff1, w_ff2, loss_target, m_w_ada, m_b_ada, m_norm_mix_g, m_w_in, m_b_gates, m_conv_w, m_conv_b, m_q_lora_g, m_w_uq, m_kv_lora_g, m_w_ukv, m_q_norm_g, m_k_norm_g, m_mlstm_norm_g, m_w_out, m_norm_mlp_g, m_w_ff1, m_w_ff2, v_w_ada, v_b_ada, v_norm_mix_g, v_w_in, v_b_gates, v_conv_w, v_conv_b, v_q_lora_g, v_w_uq, v_kv_lora_g, v_w_ukv, v_q_norm_g, v_k_norm_g, v_mlstm_norm_g, v_w_out, v_norm_mlp_g, v_w_ff1, v_w_ff2):
    args = dict(locals())
    wts = {n: args[n] for n in _ORDER}
    mom = {n: args["m_" + n] for n in _ORDER}
    var = {n: args["v_" + n] for n in _ORDER}
    MX = _MXU_DTYPE
    xi, yi, ci = lax.axis_index("x"), lax.axis_index("y"), lax.axis_index("c")
    chip = 2 * xi + yi
    dev = 2 * chip + ci
    S = x.shape[1]
    CS = 2 * MW // N_CHIP
    GS = DM // N_CHIP

    pk = _pack([c, conv_w, mlstm_norm_g], 1024)
    allpk = _allgather8(pk, name="gather_small").reshape(N_DEV, 8 * 1024)
    c_all = allpk[:, :D]
    per_chip = allpk[0::2]
    conv_w_full = per_chip[:, D:D + CONVW * CS].reshape(N_CHIP, CONVW, CS).transpose(1, 0, 2).reshape(CONVW, 2 * MW)
    o = D + CONVW * CS
    mn_full = per_chip[:, o:o + HM * GS].reshape(N_CHIP, HM, GS).transpose(1, 0, 2).reshape(HM, DM)

    (sc,) = _rowmap(lambda cv: ([cv * _sigmoid(cv)], []), [(c_all, D, 0)], [], [(D, F32)], tile=8, name="silu_c")
    ncol = w_ada.shape[2]
    b_cols = lax.dynamic_slice(b_ada, (0, chip * ncol), (1, ncol))
    modp = _mm(sc, w_ada[0], "nn", name="ada_fwd", tm=8, tn=1024, tk=512, extras=(jnp.broadcast_to(b_cols, (8, ncol)),),
               epilogue=lambda r, b: (r + b,))
    modg = _allgather8(modp, name="gather_mod").reshape(N_CHIP, 2, 8, ncol)[:, 0]
    mod_all = modg.transpose(1, 0, 2).reshape(N_DEV, N_CHIP * ncol)
    modv = jnp.pad(lax.dynamic_slice(mod_all, (dev, 0), (1, 6 * D)).reshape(6, D), ((0, 2), (0, 0)))

    shards = [wts[n][0].astype(MX) for n in _BIG]
    gw_in, gw_uq, gw_ukv, gw_out, gw_ff1, gw_ff2 = _chip_exchange(shards, gather=True, name="gather_weights")
    cols = lambda g: g.transpose(1, 0, 2).reshape(g.shape[1], N_CHIP * g.shape[2])
    W = _prep_weights(cols(gw_in), cols(gw_uq), cols(gw_ukv), gw_out.reshape(D, D), cols(gw_ff1),
                      gw_ff2.reshape(DFF, D), norm_mix_g, norm_mlp_g, q_lora_g, kv_lora_g, q_norm_g, k_norm_g,
                      mn_full, conv_w_full, conv_b, b_gates)

    loss, gx, dmodv, g = _device_step(x[0], loss_target[0], positions[0], modv, W)

    small_shapes = [(6 * D,), (D,), (NG,), (CONVW, 2 * MW), (2 * MW,), (Q_LORA,), (KV_LORA,), (QK,), (QK,), (MW,), (D,), (1,)]
    pg = _pack([dmodv, g["norm_mix_g"], g["b_gates"], g["conv_w"], g["conv_b"], g["q_lora_g"], g["kv_lora_g"],
                g["q_norm_g"], g["k_norm_g"], g["mlstm_norm_g"], g["norm_mlp_g"], loss], 4096)
    allpg = _allgather8(pg, name="gather_small_grads")
    tot = _unpack(_sum_blocks(allpg, N_DEV, name="sum_small_grads").reshape(-1), small_shapes)
    dmod_all = allpg.reshape(N_DEV, 8 * 4096)[:, :6 * D]
    gsmall = dict(zip(_SMALL, [tot[0].reshape(1, 6 * D), tot[1].reshape(1, D), tot[2].reshape(1, NG),
                               lax.dynamic_slice(tot[3], (0, chip * CS), (CONVW, CS)).reshape(1, CONVW, CS),
                               tot[4].reshape(1, 2 * MW), tot[5].reshape(1, Q_LORA), tot[6].reshape(1, KV_LORA),
                               tot[7].reshape(1, QK), tot[8].reshape(1, QK),
                               lax.dynamic_slice(tot[9].reshape(HM, DM), (0, chip * GS), (HM, GS)).reshape(1, HM, GS),
                               tot[10].reshape(1, D)]))
    loss_tot = tot[11].reshape(())

    slabs = lambda gfull: gfull.reshape(gfull.shape[0], N_CHIP, -1).transpose(1, 0, 2).astype(MX)
    to_send = [slabs(_unpad_w_in(g["w_in"])), slabs(_unpad_w_uq(g["w_uq"])), slabs(_unperm_w_ukv(g["w_ukv"])),
               g["w_out"].reshape(N_CHIP, D // N_CHIP, D).astype(MX), slabs(g["w_ff1"]),
               g["w_ff2"].reshape(N_CHIP, DFF // N_CHIP, D).astype(MX)]
    got = _chip_exchange(to_send, gather=False, name="scatter_grads")
    part = []
    for nme, r in zip(_BIG, got):
        wd = r.shape[2]
        (p,) = _rowmap(lambda a0, a1, a2, a3: ([(a0.astype(F32) + a1.astype(F32)) + (a2.astype(F32) + a3.astype(F32))], []),
                       [(r, wd, 0, k) for k in range(N_CHIP)], [], [(wd, F32)], tile=256, name="sum_chips_" + nme)
        part.append(p)
    other = _sibling_exchange(part, name="exchange_cores")

    dm_cols = lax.dynamic_slice(dmod_all, (0, chip * ncol), (N_DEV, ncol))
    g_ada = _outer8(sc.T, dm_cols, name="ada_dw")

    res = {}
    for nme, p, q in zip(_BIG, part, other):
        res[nme] = _adamw(wts[nme][0], [p, q], mom[nme][0], var[nme][0], name="adamw_" + nme)
    res["w_ada"] = _adamw(w_ada[0], [g_ada], m_w_ada[0], v_w_ada[0], name="adamw_w_ada")
    sw = _pack([wts[n] for n in _SMALL], 3072)
    sg = _pack([gsmall[n] for n in _SMALL], 3072)
    sm = _pack([mom[n] for n in _SMALL], 3072)
    sv = _pack([var[n] for n in _SMALL], 3072)
    small_res = _adamw(sw, [sg], sm, sv, name="adamw_small", tile=8)
    shapes = [wts[n].shape for n in _SMALL]
    unp = [_unpack(r.reshape(-1), shapes) for r in small_res]
    for i, nme in enumerate(_SMALL):
        res[nme] = tuple(u[i] for u in unp)
    outs = [loss_tot, gx[None]]
    for kind in range(4):
        outs += [res[n][kind].reshape(wts[n].shape) for n in _ORDER]
    return tuple(outs)
```

```python
import functools
import math

import jax
import jax.numpy as jnp
from jax import lax
from jax.experimental import pallas as pl
from jax.experimental.pallas import tpu as pltpu

F32 = jnp.float32
BF16 = jnp.bfloat16
_MXU_DTYPE = jnp.bfloat16
_INTERPRET = False

D = 2048
H_MLA = 8
NOPE = 128
ROPE = 64
QK = NOPE + ROPE
HP = 256
VD = 128
Q_LORA = 512
KV_LORA = 256
HM = 4
DM = 256
MW = HM * DM
LCH = 128
CONVW = 5
NG = 16
DFF = 4 * D
EPS = 1e-6
M_INIT = -1e30
ROPE_THETA = 10000.0
IN_SIZES = (Q_LORA, KV_LORA, ROPE, MW, MW, MW, MW, NG)
D_IN = sum(IN_SIZES)
P_QM, P_KM, P_VM, P_OM, P_CQ, P_CKV, P_KPE, P_G = 0, 1024, 2048, 3072, 4096, 4608, 4864, 4992
D_INP = 5120

ADAM_LR, ADAM_B1, ADAM_B2, ADAM_EPS, ADAM_WD, ADAM_STEP = 0.001, 0.9, 0.999, 1e-08, 0.01, 10

V7X_VMEM_LIMIT = 56 * 1024 * 1024


def _cparams(sem):
    return pltpu.CompilerParams(dimension_semantics=sem, vmem_limit_bytes=V7X_VMEM_LIMIT)


def _call(body, **kw):
    if _INTERPRET:
        kw.pop("compiler_params", None)
        kw["interpret"] = pltpu.InterpretParams()
    return pl.pallas_call(body, **kw)


def _dot(a, b, form):
    dims = {"nn": ((1,), (0,)), "nt": ((1,), (1,)), "tn": ((0,), (0,))}[form]
    return lax.dot_general(a.astype(_MXU_DTYPE), b.astype(_MXU_DTYPE), (dims, ((), ())),
                           preferred_element_type=F32)


def _mm(a, b, form, *, name, out_dtypes=(F32,), epilogue=None, extras=(), tm=1024, tn=1024, tk=512):
    if form == "nn":
        (M, K), (K2, N) = a.shape, b.shape
    elif form == "nt":
        (M, K), (N, K2) = a.shape, b.shape
    else:
        (K, M), (K2, N) = a.shape, b.shape
    assert K == K2, (a.shape, b.shape, form)
    tm, tn, tk = min(tm, M), min(tn, N), min(tk, K)
    assert M % tm == 0 and N % tn == 0 and K % tk == 0, (M, N, K, tm, tn, tk)
    nk = K // tk
    ne, no = len(extras), len(out_dtypes)
    if form == "tn":
        a_spec = pl.BlockSpec((tk, tm), lambda i, j, k: (k, i))
    else:
        a_spec = pl.BlockSpec((tm, tk), lambda i, j, k: (i, k))
    if form == "nt":
        b_spec = pl.BlockSpec((tn, tk), lambda i, j, k: (j, k))
    else:
        b_spec = pl.BlockSpec((tk, tn), lambda i, j, k: (k, j))
    mn_spec = pl.BlockSpec((tm, tn), lambda i, j, k: (i, j))

    def body(a_ref, b_ref, *rest):
        ex, outs, acc = rest[:ne], rest[ne:ne + no], rest[ne + no]
        k = pl.program_id(2)
        prod = _dot(a_ref[...], b_ref[...], form)

        @pl.when(k == 0)
        def _():
            acc[...] = prod

        @pl.when(k > 0)
        def _():
            acc[...] += prod

        @pl.when(k == nk - 1)
        def _():
            r = acc[...]
            vals = (r,) if epilogue is None else epilogue(r, *[e[...] for e in ex])
            for o, v in zip(outs, vals):
                o[...] = v.astype(o.dtype)

    res = _call(
        body, name=name, grid=(M // tm, N // tn, nk),
        in_specs=[a_spec, b_spec] + [mn_spec] * ne,
        out_specs=[mn_spec] * no,
        out_shape=[jax.ShapeDtypeStruct((M, N), dt) for dt in out_dtypes],
        scratch_shapes=[pltpu.VMEM((tm, tn), F32)],
        compiler_params=_cparams(("parallel", "parallel", "arbitrary")),
    )(a, b, *extras)
    return res[0] if no == 1 else res


def _rowmap(fn, rows, bcasts, outs, accs=(), *, tile, name):
    rows = [r if len(r) == 4 else (*r, None) for r in rows]
    S = rows[0][0].shape[-2]
    tile = min(tile, S)
    assert S % tile == 0
    nr, nb, no, na = len(rows), len(bcasts), len(outs), len(accs)

    def body(*refs):
        vals = [r[...] for r in refs[:nr + nb]]
        o_refs, a_refs = refs[nr + nb:nr + nb + no], refs[nr + nb + no:]
        o_vals, a_vals = fn(*vals)
        for r, v in zip(o_refs, o_vals):
            r[...] = v.astype(r.dtype)
        if na:
            @pl.when(pl.program_id(0) == 0)
            def _():
                for r in a_refs:
                    r[...] = jnp.zeros(r.shape, r.dtype)
            for r, v in zip(a_refs, a_vals):
                r[...] += v

    in_specs = []
    for (arr, w, cb, lead) in rows:
        if lead is None:
            in_specs.append(pl.BlockSpec((tile, w), lambda i, cb=cb: (i, cb)))
        else:
            in_specs.append(pl.BlockSpec((None, tile, w), lambda i, cb=cb, lead=lead: (lead, i, cb)))
    in_specs += [pl.BlockSpec(b.shape, lambda i: (0, 0)) for b in bcasts]
    out_specs = [pl.BlockSpec((tile, w), lambda i: (i, 0)) for (w, _) in outs]
    out_specs += [pl.BlockSpec(s, lambda i: (0, 0)) for s in accs]
    out_shape = [jax.ShapeDtypeStruct((S, w), dt) for (w, dt) in outs]
    out_shape += [jax.ShapeDtypeStruct(s, F32) for s in accs]
    return _call(
        body, name=name, grid=(S // tile,), in_specs=in_specs, out_specs=out_specs, out_shape=out_shape,
        compiler_params=_cparams(("arbitrary",)),
    )(*[r[0] for r in rows], *bcasts)


def _colsum(v):
    return jnp.sum(v, axis=0, keepdims=True)


def _rms(x, n):
    r = lax.rsqrt(jnp.sum(x * x, axis=-1, keepdims=True) * (1.0 / n) + EPS)
    return x * r, r


def _rms_bwd(dxhat, xhat, r, n):
    return r * (dxhat - xhat * (jnp.sum(dxhat * xhat, axis=-1, keepdims=True) * (1.0 / n)))


def _rope_fwd(r, cosp, s1, s2):
    return r * cosp + pltpu.roll(r, 32, 1) * s1 + pltpu.roll(r, 96, 1) * s2


def _rope_bwd(d, cosp, s1, s2):
    return d * cosp + pltpu.roll(d * s1, 96, 1) + pltpu.roll(d * s2, 32, 1)


def _sigmoid(x):
    return 1.0 / (1.0 + jnp.exp(-x))


def _halo_specs(tile, halo, width, cb, S, lead=None):
    nh = tile // halo
    last = S // halo - 1
    if lead is None:
        return [
            pl.BlockSpec((tile, width), lambda i: (i, cb)),
            pl.BlockSpec((halo, width), lambda i: (jnp.maximum(i * nh - 1, 0), cb)),
            pl.BlockSpec((halo, width), lambda i: (jnp.minimum((i + 1) * nh, last), cb)),
        ]
    return [
        pl.BlockSpec((None, tile, width), lambda i: (lead, i, cb)),
        pl.BlockSpec((None, halo, width), lambda i: (lead, jnp.maximum(i * nh - 1, 0), cb)),
        pl.BlockSpec((None, halo, width), lambda i: (lead, jnp.minimum((i + 1) * nh, last), cb)),
    ]


def _conv_fwd(proj, conv_w8, conv_b, *, tile=256):
    S = proj.shape[0]
    T = min(tile, S)
    n = S // T
    W = 2 * MW

    def body(x_ref, xp_ref, xn_ref, w_ref, b_ref, q_ref, k_ref, ext):
        i = pl.program_id(0)
        ext[pl.ds(0, 8), :] = xp_ref[...] * (i > 0).astype(F32)
        ext[pl.ds(8, T), :] = x_ref[...]
        ext[pl.ds(8 + T, 8), :] = xn_ref[...] * (i < n - 1).astype(F32)
        w = w_ref[...]
        y = b_ref[...] + w[0:1, :] * ext[pl.ds(6, T), :]
        for o in range(1, CONVW):
            y = y + w[o:o + 1, :] * ext[pl.ds(6 + o, T), :]
        y = y * _sigmoid(y)
        q_ref[...] = y[:, :MW].astype(q_ref.dtype)
        k_ref[...] = (y[:, MW:] * (DM ** -0.5)).astype(k_ref.dtype)

    return _call(
        body, name="conv_fwd", grid=(n,),
        in_specs=_halo_specs(T, 8, W, 0, S) + [pl.BlockSpec((8, W), lambda i: (0, 0)),
                                                 pl.BlockSpec((1, W), lambda i: (0, 0))],
        out_specs=[pl.BlockSpec((T, MW), lambda i: (i, 0))] * 2,
        out_shape=[jax.ShapeDtypeStruct((S, MW), _MXU_DTYPE)] * 2,
        scratch_shapes=[pltpu.VMEM((T + 16, W), F32)],
        compiler_params=_cparams(("arbitrary",)),
    )(proj, proj, proj, conv_w8, conv_b)


def _conv_bwd(proj, dqd, dkd, conv_w8, conv_b, *, tile=256):
    S = proj.shape[0]
    T = min(tile, S)
    n = S // T
    W = 2 * MW

    def body(x_ref, xp_ref, xn_ref, *rest):
        g = rest[:12]
        w_ref, b_ref, dx_ref, dw_ref, db_ref, ext, edp = rest[12:]
        i = pl.program_id(0)
        mp = (i > 0).astype(F32)
        mn = (i < n - 1).astype(F32)
        ext[pl.ds(0, 16), :] = xp_ref[...] * mp
        ext[pl.ds(16, T), :] = x_ref[...]
        ext[pl.ds(16 + T, 16), :] = xn_ref[...] * mn
        w = w_ref[...]
        pre = b_ref[...] + w[0:1, :] * ext[pl.ds(6, T + 16), :]
        for o in range(1, CONVW):
            pre = pre + w[o:o + 1, :] * ext[pl.ds(6 + o, T + 16), :]
        sg = _sigmoid(pre)
        dsilu = sg * (1.0 + pre * (1.0 - sg))
        for half, (a0, a1) in enumerate(((g[0:3], g[3:6]), (g[6:9], g[9:12]))):
            sc = 1.0 if half == 0 else DM ** -0.5
            cols = pl.ds(half * MW, MW)
            edp[pl.ds(0, 8), cols] = (a0[1][...] + a1[1][...]) * (mp * sc)
            edp[pl.ds(8, T), cols] = (a0[0][...] + a1[0][...]) * sc
            edp[pl.ds(8 + T, 8), cols] = (a0[2][...] + a1[2][...]) * (mn * sc)
        edp[...] = edp[...] * dsilu
        dpm = edp[pl.ds(8, T), :]
        dx = w[0:1, :] * edp[pl.ds(10, T), :]
        for o in range(1, CONVW):
            dx = dx + w[o:o + 1, :] * edp[pl.ds(10 - o, T), :]
        dx_ref[...] = dx.astype(dx_ref.dtype)

        @pl.when(i == 0)
        def _():
            dw_ref[...] = jnp.zeros(dw_ref.shape, F32)
            db_ref[...] = jnp.zeros(db_ref.shape, F32)

        for o in range(CONVW):
            dw_ref[pl.ds(o, 1), :] += _colsum(ext[pl.ds(14 + o, T), :] * dpm)
        db_ref[...] += _colsum(dpm)

    gspecs = []
    for arr in (dqd, dkd):
        for d in (0, 1):
            gspecs += _halo_specs(T, 8, MW, 0, S, lead=d)
    return _call(
        body, name="conv_bwd", grid=(n,),
        in_specs=_halo_specs(T, 16, W, 0, S) + gspecs + [pl.BlockSpec((8, W), lambda i: (0, 0)),
                                                          pl.BlockSpec((1, W), lambda i: (0, 0))],
        out_specs=[pl.BlockSpec((T, W), lambda i: (i, 0)), pl.BlockSpec((8, W), lambda i: (0, 0)),
                   pl.BlockSpec((1, W), lambda i: (0, 0))],
        out_shape=[jax.ShapeDtypeStruct((S, W), _MXU_DTYPE), jax.ShapeDtypeStruct((8, W), F32),
                   jax.ShapeDtypeStruct((1, W), F32)],
        scratch_shapes=[pltpu.VMEM((T + 32, W), F32), pltpu.VMEM((T + 16, W), F32)],
        compiler_params=_cparams(("arbitrary",)),
    )(proj, proj, proj, *([dqd] * 6), *([dkd] * 6), conv_w8, conv_b)


_ATT_SCALE = QK ** -0.5
_LOG2E = math.log2(math.e)
_Q_PRESCALE = _ATT_SCALE * _LOG2E
_ATT_SPLIT = 2


def _attn_fwd(q, k, v, *, tq=512, tk=512):
    S = q.shape[0]
    tq, tk = min(tq, S), min(tk, S)
    nkv = S // tk
    hq = tq // _ATT_SPLIT

    def body(q_ref, k_ref, v_ref, o_ref, lse_ref, m_s, l_s, acc_s):
        m_s[...] = jnp.full(m_s.shape, -1e30, F32)
        l_s[...] = jnp.zeros(l_s.shape, F32)
        acc_s[...] = jnp.zeros(acc_s.shape, F32)

        def step(j, carry):
            rows = pl.ds(pl.multiple_of(j * tk, tk), tk)
            kj, vj = k_ref[rows, :], v_ref[rows, :]
            for a in range(_ATT_SPLIT):
                r = pl.ds(a * hq, hq)
                s = _dot(q_ref[r, :], kj, "nt")
                m_old = m_s[r, :]
                m_new = jnp.maximum(m_old, jnp.max(s, axis=1, keepdims=True))
                p = jnp.exp2(s - m_new)
                alpha = jnp.exp2(m_old - m_new)
                l_s[r, :] = alpha * l_s[r, :] + jnp.sum(p, axis=1, keepdims=True)
                acc_s[r, :] = alpha * acc_s[r, :] + _dot(p, vj, "nn")
                m_s[r, :] = m_new
            return carry

        lax.fori_loop(0, nkv, step, 0, unroll=2 if nkv % 2 == 0 else 1)
        o_ref[...] = (acc_s[...] / l_s[...]).astype(o_ref.dtype)
        lse_ref[...] = m_s[...] + jnp.log2(l_s[...])

    return _call(
        body, name="attn_fwd", grid=(H_MLA, S // tq),
        in_specs=[pl.BlockSpec((tq, HP), lambda h, i: (i, h)),
                  pl.BlockSpec((S, HP), lambda h, i: (0, h)),
                  pl.BlockSpec((S, VD), lambda h, i: (0, h))],
        out_specs=[pl.BlockSpec((tq, VD), lambda h, i: (i, h)),
                   pl.BlockSpec((None, tq, 1), lambda h, i: (h, i, 0))],
        out_shape=[jax.ShapeDtypeStruct((S, H_MLA * VD), _MXU_DTYPE),
                   jax.ShapeDtypeStruct((H_MLA, S, 1), F32)],
        scratch_shapes=[pltpu.VMEM((tq, 1), F32), pltpu.VMEM((tq, 1), F32), pltpu.VMEM((tq, VD), F32)],
        compiler_params=_cparams(("parallel", "arbitrary")),
    )(q, k, v)


def _attn_bwd_dkv(q, k, v, dcat, lse_r, dl_r, *, tq=512, tk=512):
    S = q.shape[0]
    tq, tk = min(tq, S), min(tk, S)
    nq = S // tq

    hk = tk // _ATT_SPLIT

    def body(q_ref, k_ref, v_ref, do_ref, lse_ref, dl_ref, dk_ref, dv_ref):
        dk_ref[...] = jnp.zeros(dk_ref.shape, F32)
        dv_ref[...] = jnp.zeros(dv_ref.shape, F32)

        def step(i, carry):
            rows = pl.ds(pl.multiple_of(i * tq, tq), tq)
            qi, doi = q_ref[rows, :], do_ref[rows, :].astype(_MXU_DTYPE)
            lse, dl = lse_ref[i], dl_ref[i]
            for a in range(_ATT_SPLIT):
                r = pl.ds(a * hk, hk)
                pt = jnp.exp2(_dot(k_ref[r, :], qi, "nt") - lse)
                dv_ref[r, :] += _dot(pt, doi, "nn")
                dst = pt * (_dot(v_ref[r, :], doi, "nt") - dl)
                dk_ref[r, :] += _dot(dst, qi, "nn")
            return carry

        lax.fori_loop(0, nq, step, 0, unroll=2 if nq % 2 == 0 else 1)
        dk_ref[...] = dk_ref[...] * (1.0 / _LOG2E)

    return _call(
        body, name="attn_bwd_dkv", grid=(H_MLA, S // tk),
        in_specs=[pl.BlockSpec((S, HP), lambda h, j: (0, h)),
                  pl.BlockSpec((tk, HP), lambda h, j: (j, h)),
                  pl.BlockSpec((tk, VD), lambda h, j: (j, h)),
                  pl.BlockSpec((S, VD), lambda h, j: (0, h)),
                  pl.BlockSpec((None, nq, 1, tq), lambda h, j: (h, 0, 0, 0)),
                  pl.BlockSpec((None, nq, 1, tq), lambda h, j: (h, 0, 0, 0))],
        out_specs=[pl.BlockSpec((tk, HP), lambda h, j: (j, h)),
                   pl.BlockSpec((tk, VD), lambda h, j: (j, h))],
        out_shape=[jax.ShapeDtypeStruct((S, H_MLA * HP), F32), jax.ShapeDtypeStruct((S, H_MLA * VD), F32)],
        compiler_params=_cparams(("parallel", "arbitrary")),
    )(q, k, v, dcat, lse_r, dl_r)


def _attn_bwd_dq(q, k, v, dcat, lse_c, dl_c, *, tq=512, tk=512):
    S = q.shape[0]
    tq, tk = min(tq, S), min(tk, S)
    nkv = S // tk

    hq = tq // _ATT_SPLIT

    def body(q_ref, k_ref, v_ref, do_ref, lse_ref, dl_ref, dq_ref, do_s):
        dq_ref[...] = jnp.zeros(dq_ref.shape, F32)
        do_s[...] = do_ref[...].astype(do_s.dtype)

        def step(j, carry):
            rows = pl.ds(pl.multiple_of(j * tk, tk), tk)
            kj, vj = k_ref[rows, :], v_ref[rows, :]
            for a in range(_ATT_SPLIT):
                r = pl.ds(a * hq, hq)
                p = jnp.exp2(_dot(q_ref[r, :], kj, "nt") - lse_ref[r, :])
                ds = p * (_dot(do_s[r, :], vj, "nt") - dl_ref[r, :])
                dq_ref[r, :] += _dot(ds, kj, "nn")
            return carry

        lax.fori_loop(0, nkv, step, 0, unroll=2 if nkv % 2 == 0 else 1)
        dq_ref[...] = dq_ref[...] * _ATT_SCALE

    return _call(
        body, name="attn_bwd_dq", grid=(H_MLA, S // tq),
        in_specs=[pl.BlockSpec((tq, HP), lambda h, i: (i, h)),
                  pl.BlockSpec((S, HP), lambda h, i: (0, h)),
                  pl.BlockSpec((S, VD), lambda h, i: (0, h)),
                  pl.BlockSpec((tq, VD), lambda h, i: (i, h)),
                  pl.BlockSpec((None, tq, 1), lambda h, i: (h, i, 0)),
                  pl.BlockSpec((None, tq, 1), lambda h, i: (h, i, 0))],
        out_specs=pl.BlockSpec((tq, HP), lambda h, i: (i, h)),
        out_shape=jax.ShapeDtypeStruct((S, H_MLA * HP), F32),
        scratch_shapes=[pltpu.VMEM((tq, VD), _MXU_DTYPE)],
        compiler_params=_cparams(("parallel", "arbitrary")),
    )(q, k, v, dcat, lse_c, dl_c)


def _mlstm_chunk_terms(g, q, k, v, gates, gates_t, bg_row, C, n_row, m):
    L = LCH
    d = g // HM
    h = g % HM
    i_idx = d * 8 + h
    f_idx = d * 8 + 4 + h
    rr = lax.broadcasted_iota(jnp.int32, (L, L), 0)
    cc = lax.broadcasted_iota(jnp.int32, (L, L), 1)
    order = (rr - cc) * (1 - 2 * d)
    tri = order >= 0
    eye = rr == cc
    lane = lax.broadcasted_iota(jnp.int32, gates.shape, 1)
    sub = lax.broadcasted_iota(jnp.int32, gates_t.shape, 0)
    lane_b = lax.broadcasted_iota(jnp.int32, bg_row.shape, 1)
    pick_c = lambda idx: jnp.sum(jnp.where(lane == idx, gates, 0.0), axis=1, keepdims=True)
    pick_r = lambda idx: jnp.sum(jnp.where(sub == idx, gates_t, 0.0), axis=0, keepdims=True)
    pick_b = lambda idx: jnp.sum(jnp.where(lane_b == idx, bg_row, 0.0), axis=1, keepdims=True)
    i_col, i_row = pick_c(i_idx) + pick_b(i_idx), pick_r(i_idx) + pick_b(i_idx)
    f_col, f_row = pick_c(f_idx) + pick_b(f_idx), pick_r(f_idx) + pick_b(f_idx)
    logsig = lambda x: jnp.minimum(x, 0.0) - jnp.log(1.0 + jnp.exp(-jnp.abs(x)))
    lf_col, lf_row = logsig(f_col), logsig(f_row)
    b_col = jnp.sum(jnp.where(tri, lf_row, 0.0), axis=1, keepdims=True)
    tri_t = order <= 0
    b_row = jnp.sum(jnp.where(tri_t, lf_col, 0.0), axis=0, keepdims=True)
    bL = jnp.sum(lf_row, axis=1, keepdims=True)
    log_inter = b_col + m
    logD = jnp.where(tri, b_col - b_row + i_row, -jnp.inf)
    m_t = jnp.maximum(log_inter, jnp.max(logD, axis=1, keepdims=True))
    Dm = jnp.exp(logD - m_t)
    w_inter = jnp.exp(log_inter - m_t)
    A = _dot(q, k, "nt")
    Sc = A * Dm
    numI = _dot(q, C, "nt")
    qf = q.astype(F32)
    kf = k.astype(F32)
    denI = jnp.sum(qf * n_row, axis=1, keepdims=True)
    num = _dot(Sc, v, "nn") + w_inter * numI
    den = jnp.sum(Sc, axis=1, keepdims=True) + w_inter * denI
    floor = jnp.exp(-m_t)
    Nst = jnp.maximum(jnp.abs(den), floor)
    log_w = bL - b_col + i_col
    m_new = jnp.maximum(bL + m, jnp.max(log_w, axis=0, keepdims=True))
    decay = jnp.exp(bL + m - m_new)
    w_col = jnp.exp(log_w - m_new)
    return dict(tri=tri, eye=eye, f_row=f_row, Dm=Dm, w_inter=w_inter, A=A, Sc=Sc, numI=numI, denI=denI,
                num=num, den=den, floor=floor, Nst=Nst, m_new=m_new, decay=decay, w_col=w_col, qf=qf, kf=kf)


def _mlstm_specs(S, nc):
    jeff = lambda g, j: j + (g // HM) * (nc - 1 - 2 * j)
    return jeff, [
        pl.BlockSpec((LCH, DM), lambda g, j: (jeff(g, j), g % HM)),
        pl.BlockSpec((LCH, DM), lambda g, j: (jeff(g, j), g % HM)),
        pl.BlockSpec((LCH, DM), lambda g, j: (jeff(g, j), P_VM // DM + g % HM)),
        pl.BlockSpec((LCH, 128), lambda g, j: (jeff(g, j), P_G // 128)),
        pl.BlockSpec((NG, LCH), lambda g, j: (0, jeff(g, j))),
        pl.BlockSpec((1, 128), lambda g, j: (0, 0)),
    ]


def _mlstm_fwd(qc, kc, proj, gates_t, bg_row):
    S = qc.shape[0]
    nc = S // LCH
    jeff, specs = _mlstm_specs(S, nc)

    def body(q_ref, k_ref, v_ref, g_ref, gt_ref, bg_ref, h_ref, cst_ref, nm_ref, C_s, n_s, m_s):
        g = pl.program_id(0)

        @pl.when(pl.program_id(1) == 0)
        def _():
            C_s[...] = jnp.zeros(C_s.shape, F32)
            n_s[...] = jnp.zeros(n_s.shape, F32)
            m_s[...] = jnp.full(m_s.shape, M_INIT, F32)

        C, n_row, m = C_s[...], n_s[0:1, :], m_s[0:1, 0:1]
        cst_ref[...] = C
        nm_ref[0:1, :] = n_row
        nm_ref[1:2, :] = jnp.broadcast_to(m, (1, DM))
        nm_ref[2:8, :] = jnp.zeros((6, DM), F32)
        q, k, v = q_ref[...], k_ref[...], v_ref[...]
        t = _mlstm_chunk_terms(g, q, k, v, g_ref[...], gt_ref[...], bg_ref[...], C, n_row, m)
        h_ref[...] = t["num"] / t["Nst"]
        wv = t["w_col"] * v
        C_s[...] = t["decay"] * C + _dot(wv, k, "tn")
        n_s[0:1, :] = t["decay"] * n_row + _colsum(t["w_col"] * t["kf"])
        m_s[...] = jnp.broadcast_to(t["m_new"], m_s.shape)

    return _call(
        body, name="mlstm_fwd", grid=(2 * HM, nc), in_specs=specs,
        out_specs=[pl.BlockSpec((None, LCH, DM), lambda g, j: (g // HM, jeff(g, j), g % HM)),
                   pl.BlockSpec((None, None, DM, DM), lambda g, j: (g, jeff(g, j), 0, 0)),
                   pl.BlockSpec((None, None, 8, DM), lambda g, j: (g, jeff(g, j), 0, 0))],
        out_shape=[jax.ShapeDtypeStruct((2, S, MW), F32),
                   jax.ShapeDtypeStruct((2 * HM, nc, DM, DM), F32),
                   jax.ShapeDtypeStruct((2 * HM, nc, 8, DM), F32)],
        scratch_shapes=[pltpu.VMEM((DM, DM), F32), pltpu.VMEM((8, DM), F32), pltpu.VMEM((8, 128), F32)],
        compiler_params=_cparams(("parallel", "arbitrary")),
    )(qc, kc, proj, proj, gates_t, bg_row)


def _mlstm_bwd(qc, kc, proj, gates_t, bg_row, dh, cst, nm):
    S = qc.shape[0]
    nc = S // LCH
    _, specs = _mlstm_specs(S, nc)
    jb = lambda g, j: (nc - 1 - j) + (g // HM) * (2 * j - (nc - 1))
    respec = lambda bs: pl.BlockSpec(bs.block_shape, (lambda g, j, im=bs.index_map: im(g, nc - 1 - j)))
    specs = [respec(s) for s in specs]

    def body(q_ref, k_ref, v_ref, g_ref, gt_ref, bg_ref, dh_ref, cst_ref, nm_ref,
             dq_ref, dk_ref, dv_ref, dg_ref, dC_s, dn_s):
        g = pl.program_id(0)

        @pl.when(pl.program_id(1) == 0)
        def _():
            dC_s[...] = jnp.zeros(dC_s.shape, F32)
            dn_s[...] = jnp.zeros(dn_s.shape, F32)

        C, n_row, m = cst_ref[...], nm_ref[0:1, :], nm_ref[1:2, 0:1]
        q, k, v = q_ref[...], k_ref[...], v_ref[...]
        t = _mlstm_chunk_terms(g, q, k, v, g_ref[...], gt_ref[...], bg_ref[...], C, n_row, m)
        tri, eye, qf, kf = t["tri"], t["eye"], t["qf"], t["kf"]
        w_inter, w_col, decay, Nst = t["w_inter"], t["w_col"], t["decay"], t["Nst"]
        dC, dn = dC_s[...], dn_s[0:1, :]
        dhv = dh_ref[...]
        hval = t["num"] / Nst
        dnum = dhv / Nst
        dNst = -jnp.sum(dhv * hval, axis=1, keepdims=True) / Nst
        dden = jnp.where(jnp.abs(t["den"]) > t["floor"], jnp.sign(t["den"]) * dNst, 0.0)
        dSc = _dot(dnum, v, "nt") + dden
        dA = dSc * t["Dm"]
        G = dSc * t["Sc"]
        KdC = _dot(k, dC, "nt")
        dq = _dot(dA, k, "nn") + w_inter * _dot(dnum, C, "nn") + (w_inter * dden) * n_row
        dk = _dot(dA, q, "tn") + w_col * _dot(v, dC, "nn") + w_col * dn
        dv = _dot(t["Sc"], dnum, "tn") + w_col * KdC
        dq_ref[...] = dq
        dk_ref[...] = dk
        dv_ref[...] = dv
        dlog_inter = w_inter * (jnp.sum(dnum * t["numI"], axis=1, keepdims=True) + dden * t["denI"])
        rowG = jnp.sum(G, axis=1, keepdims=True)
        colG = jnp.sum(G, axis=0, keepdims=True)
        u_col = w_col * (jnp.sum(v * KdC, axis=1, keepdims=True) + jnp.sum(kf * dn, axis=1, keepdims=True))
        colG_c = jnp.sum(jnp.where(eye, colG, 0.0), axis=1, keepdims=True)
        u_row = jnp.sum(jnp.where(eye, u_col, 0.0), axis=0, keepdims=True)
        db_col = rowG + dlog_inter - u_col - colG_c
        dbL = jnp.sum(u_col, axis=0, keepdims=True) + decay * (
            jnp.sum(jnp.sum(dC * C, axis=1, keepdims=True), axis=0, keepdims=True)
            + jnp.sum(dn * n_row, axis=1, keepdims=True))
        dlf_row = jnp.sum(jnp.where(tri, db_col, 0.0), axis=0, keepdims=True) + dbL
        di_row = colG + u_row
        df_row = dlf_row * (1.0 - _sigmoid(t["f_row"]))
        dg_ref[...] = jnp.zeros(dg_ref.shape, F32)
        dg_ref[0:1, :] = di_row
        dg_ref[1:2, :] = df_row
        dC_s[...] = decay * dC + _dot(w_inter * dnum, q, "tn")
        dn_s[0:1, :] = decay * dn + _colsum((w_inter * dden) * qf)

    return _call(
        body, name="mlstm_bwd", grid=(2 * HM, nc),
        in_specs=specs + [pl.BlockSpec((LCH, DM), lambda g, j: (jb(g, j), g % HM)),
                          pl.BlockSpec((None, None, DM, DM), lambda g, j: (g, jb(g, j), 0, 0)),
                          pl.BlockSpec((None, None, 8, DM), lambda g, j: (g, jb(g, j), 0, 0))],
        out_specs=[pl.BlockSpec((None, LCH, DM), lambda g, j: (g // HM, jb(g, j), g % HM))] * 3
        + [pl.BlockSpec((None, None, 8, LCH), lambda g, j: (g, jb(g, j), 0, 0))],
        out_shape=[jax.ShapeDtypeStruct((2, S, MW), F32)] * 3
        + [jax.ShapeDtypeStruct((2 * HM, nc, 8, LCH), F32)],
        scratch_shapes=[pltpu.VMEM((DM, DM), F32), pltpu.VMEM((8, DM), F32)],
        compiler_params=_cparams(("parallel", "arbitrary")),
    )(qc, kc, proj, proj, gates_t, bg_row, dh, cst, nm)


def _pad_w_in(w):
    cq, ckv, kpe, qm, km, vm, om, gt = _split_in(w)
    z = lambda n: jnp.zeros((w.shape[0], n), w.dtype)
    return jnp.concatenate([qm, km, vm, om, cq, ckv, kpe, z(HP - QK), gt, z(128 - NG)], axis=1)


def _split_in(w):
    out, o = [], 0
    for n in IN_SIZES:
        out.append(w[:, o:o + n])
        o += n
    return out


def _unpad_w_in(g):
    return jnp.concatenate([g[:, P_CQ:P_CQ + Q_LORA], g[:, P_CKV:P_CKV + KV_LORA], g[:, P_KPE:P_KPE + ROPE],
                            g[:, 0:4 * MW], g[:, P_G:P_G + NG]], axis=1)


def _pad_w_uq(w):
    return jnp.pad(w.reshape(Q_LORA, H_MLA, QK), ((0, 0), (0, 0), (0, HP - QK))).reshape(Q_LORA, H_MLA * HP)


def _unpad_w_uq(g):
    return g.reshape(Q_LORA, H_MLA, HP)[:, :, :QK].reshape(Q_LORA, H_MLA * QK)


def _perm_w_ukv(w):
    return w.reshape(KV_LORA, H_MLA, 2, NOPE).transpose(0, 2, 1, 3).reshape(KV_LORA, 2 * H_MLA * NOPE)


def _unperm_w_ukv(g):
    return g.reshape(KV_LORA, 2, H_MLA, NOPE).transpose(0, 2, 1, 3).reshape(KV_LORA, 2 * H_MLA * NOPE)


def _rope_tables(positions):
    half = ROPE // 2
    freqs = ROPE_THETA ** (-jnp.arange(half, dtype=F32) / half)
    ang = positions.astype(F32)[:, None] * freqs
    cos, sin = jnp.cos(ang), jnp.sin(ang)
    z32, z64 = jnp.zeros_like(cos), jnp.zeros((cos.shape[0], 64), F32)
    return (jnp.concatenate([cos, cos, z64], axis=1), jnp.concatenate([z32, sin, z64], axis=1),
            jnp.concatenate([-sin, z32, z64], axis=1))


def _device_step(x, tgt, positions, modv, W):
    S = x.shape[0]
    MX = _MXU_DTYPE
    cosp, rs1, rs2 = _rope_tables(positions)
    tabs = [(cosp, 128, 0), (rs1, 128, 0), (rs2, 128, 0)]
    cat1 = lambda vs: jnp.concatenate(vs, axis=1)
    hsl = lambda hh, w: slice(hh * w, (hh + 1) * w)

    def ln1(xv, g, mv):
        xhat, _ = _rms(xv, D)
        return [xhat * g * (1.0 + mv[1:2]) + mv[0:1]], []

    (h,) = _rowmap(ln1, [(x, D, 0)], [W["g_mix"], modv], [(D, MX)], tile=128, name="ln1")
    proj = _mm(h, W["w_in"], "nn", name="proj")

    def lora(cq, ckv, gq, gkv):
        return [_rms(cq, Q_LORA)[0] * gq, _rms(ckv, KV_LORA)[0] * gkv], []

    cqn, ckvn = _rowmap(lora, [(proj, Q_LORA, P_CQ // Q_LORA), (proj, KV_LORA, P_CKV // KV_LORA)],
                        [W["g_qlora"], W["g_kvlora"]], [(Q_LORA, MX), (KV_LORA, MX)], tile=256, name="lora_norm")
    q_raw = _mm(cqn, W["w_uq"], "nn", name="q_up")
    kv_raw = _mm(ckvn, W["w_ukv"], "nn", name="kv_up")

    def mla_q(qr, cp, a1, a2, gq):
        outs = []
        for hh in range(H_MLA):
            y = _rms(qr[:, hsl(hh, HP)], QK)[0] * gq
            outs += [y[:, :NOPE], _rope_fwd(y[:, NOPE:], cp, a1, a2)]
        return [cat1(outs) * _Q_PRESCALE], []

    (qh,) = _rowmap(mla_q, [(q_raw, H_MLA * HP, 0)] + tabs, [W["gq"]], [(H_MLA * HP, MX)], tile=128, name="mla_q")

    def mla_k(kvr, kpe, cp, a1, a2, gk):
        outs = []
        for hh in range(H_MLA):
            y = _rms(cat1([kvr[:, hsl(hh, NOPE)], kpe]), QK)[0] * gk
            outs += [y[:, :NOPE], _rope_fwd(y[:, NOPE:], cp, a1, a2)]
        return [cat1(outs), kvr[:, H_MLA * NOPE:]], []

    kh, vh = _rowmap(mla_k, [(kv_raw, 2 * H_MLA * NOPE, 0), (proj, 128, P_KPE // 128)] + tabs, [W["gk"]],
                     [(H_MLA * HP, MX), (H_MLA * VD, MX)], tile=128, name="mla_k")
    attn_o, lse = _attn_fwd(qh, kh, vh)

    qc, kc = _conv_fwd(proj, W["conv_w8"], W["conv_b"])
    gates_t = proj[:, P_G:P_G + NG].T
    h_dirs, cst, nm = _mlstm_fwd(qc, kc, proj, gates_t, W["bg_row"])
    hrows = [(h_dirs, MW, 0, 0), (h_dirs, MW, 0, 1), (proj, MW, P_OM // MW)]

    def ml_out(ao, hf, hb, om, gmn):
        outs = [ao.astype(F32)]
        hs = hf + hb
        for hh in range(HM):
            sl = hsl(hh, DM)
            outs.append(_sigmoid(om[:, sl]) * _rms(hs[:, sl], DM)[0] * gmn[:, sl])
        return [cat1(outs)], []

    (cat,) = _rowmap(ml_out, [(attn_o, MW, 0)] + hrows, [W["g_mn"]], [(D, MX)], tile=128, name="ml_out")
    mixed = _mm(cat, W["w_out"], "nn", name="out_proj")

    def res_ln2(xv, mx, g, mv):
        x1 = xv + mv[2:3] * mx
        return [x1, _rms(x1, D)[0] * g * (1.0 + mv[4:5]) + mv[3:4]], []

    x1, h2 = _rowmap(res_ln2, [(x, D, 0), (mixed, D, 0)], [W["g_mlp"], modv], [(D, F32), (D, MX)],
                     tile=128, name="res_ln2")
    a, u = _mm(h2, W["w_ff1"], "nn", name="ff1", out_dtypes=(MX, MX),
               epilogue=lambda r: (jnp.square(jnp.maximum(r, 0.0)), r))
    y = _mm(a, W["w_ff2"], "nn", name="ff2")

    def final(x1v, yv, tv, mv):
        err = x1v + mv[5:6] * yv - tv
        dout = err * (1.0 / D)
        loss = jnp.sum(jnp.sum(0.5 * err * dout, axis=1, keepdims=True), axis=0, keepdims=True)
        return [dout, mv[5:6] * dout], [loss, _colsum(dout * yv)]

    dout, dy, loss, dgate2 = _rowmap(final, [(x1, D, 0), (y, D, 0), (tgt, D, 0)], [modv], [(D, F32), (D, MX)],
                                     [(1, 1), (1, D)], tile=128, name="loss_head")

    du = _mm(dy, W["w_ff2"], "nt", name="ff2_dx", out_dtypes=(MX,), extras=(u,),
             epilogue=lambda r, uv: (r * (2.0 * jnp.maximum(uv.astype(F32), 0.0)),))
    g_ff2 = _mm(a, dy, "tn", name="ff2_dw")
    dh2 = _mm(du, W["w_ff1"], "nt", name="ff1_dx")
    g_ff1 = _mm(h2, du, "tn", name="ff1_dw")

    def ln2_bwd(dh2v, x1v, doutv, mxv, g, mv):
        xhat, r = _rms(x1v, D)
        dn2 = dh2v * (1.0 + mv[4:5])
        dx1 = doutv + _rms_bwd(dn2 * g, xhat, r, D)
        return [dx1, mv[2:3] * dx1], [_colsum(dh2v), _colsum(dh2v * xhat * g), _colsum(dn2 * xhat), _colsum(dx1 * mxv)]

    dx1, dmixed, dshift2, dscale2, dg_mlp, dgate1 = _rowmap(
        ln2_bwd, [(dh2, D, 0), (x1, D, 0), (dout, D, 0), (mixed, D, 0)], [W["g_mlp"], modv],
        [(D, F32), (D, MX)], [(1, D)] * 4, tile=128, name="ln2_bwd")
    dcat = _mm(dmixed, W["w_out"], "nt", name="out_dx")
    g_out = _mm(cat, dmixed, "tn", name="out_dw")

    def ml_out_bwd(dml, hf, hb, om, gmn):
        hs = hf + hb
        dhs, dos, dgs = [], [], []
        for hh in range(HM):
            sl = hsl(hh, DM)
            xhat, r = _rms(hs[:, sl], DM)
            g, sg, d = gmn[:, sl], _sigmoid(om[:, sl]), dml[:, sl]
            dos.append(d * xhat * g * sg * (1.0 - sg))
            dhn = d * sg
            dgs.append(_colsum(dhn * xhat))
            dhs.append(_rms_bwd(dhn * g, xhat, r, DM))
        return [cat1(dhs), cat1(dos)], [cat1(dgs)]

    dhs, do_m, dg_mn = _rowmap(ml_out_bwd, [(dcat, MW, 1)] + hrows, [W["g_mn"]], [(MW, F32), (MW, MX)],
                               [(1, MW)], tile=128, name="ml_out_bwd")
    dqd, dkd, dvd, dgates = _mlstm_bwd(qc, kc, proj, gates_t, W["bg_row"], dhs, cst, nm)
    dqk_m, dconv_w8, dconv_b = _conv_bwd(proj, dqd, dkd, W["conv_w8"], W["conv_b"])

    def delta_fn(ao, dov):
        lane = lax.broadcasted_iota(jnp.int32, (ao.shape[0], 128), 1)
        acc = jnp.zeros((ao.shape[0], 128), F32)
        for hh in range(H_MLA):
            sl = hsl(hh, VD)
            acc = acc + jnp.where(lane == hh, jnp.sum(ao[:, sl].astype(F32) * dov[:, sl], axis=1, keepdims=True), 0.0)
        return [acc], []

    (dl,) = _rowmap(delta_fn, [(attn_o, MW, 0), (dcat, MW, 0)], [], [(128, F32)], tile=256, name="attn_delta")
    tq = min(512, S)
    dl_hs = dl[:, :H_MLA].T
    dk_a, dv_a = _attn_bwd_dkv(qh, kh, vh, dcat, lse.reshape(H_MLA, S // tq, 1, tq),
                               dl_hs.reshape(H_MLA, S // tq, 1, tq))
    dq_a = _attn_bwd_dq(qh, kh, vh, dcat, lse, dl_hs.reshape(H_MLA, S, 1))

    def mla_q_bwd(dqv, qr, cp, a1, a2, gq):
        outs, dg = [], 0.0
        for hh in range(H_MLA):
            sl = hsl(hh, HP)
            xhat, r = _rms(qr[:, sl], QK)
            d = dqv[:, sl]
            dyv = cat1([d[:, :NOPE], _rope_bwd(d[:, NOPE:], cp, a1, a2)])
            dg = dg + _colsum(dyv * xhat)
            outs.append(_rms_bwd(dyv * gq, xhat, r, QK))
        return [cat1(outs)], [dg]

    dq_raw, dgq = _rowmap(mla_q_bwd, [(dq_a, H_MLA * HP, 0), (q_raw, H_MLA * HP, 0)] + tabs, [W["gq"]],
                          [(H_MLA * HP, MX)], [(1, HP)], tile=128, name="mla_q_bwd")
    dcqn = _mm(dq_raw, W["w_uq"], "nt", name="q_up_dx")
    g_uq = _mm(cqn, dq_raw, "tn", name="q_up_dw")

    def mla_k_bwd(dkv, dvv, kvr, kpe, cp, a1, a2, gk):
        dkn, dg, dkpe = [], 0.0, 0.0
        for hh in range(H_MLA):
            xhat, r = _rms(cat1([kvr[:, hsl(hh, NOPE)], kpe]), QK)
            d = dkv[:, hsl(hh, HP)]
            dyv = cat1([d[:, :NOPE], _rope_bwd(d[:, NOPE:], cp, a1, a2)])
            dg = dg + _colsum(dyv * xhat)
            dxv = _rms_bwd(dyv * gk, xhat, r, QK)
            dkn.append(dxv[:, :NOPE])
            dkpe = dkpe + dxv[:, NOPE:]
        return [cat1(dkn + [dvv]), dkpe], [dg]

    dkv_raw, dkpe, dgk = _rowmap(
        mla_k_bwd, [(dk_a, H_MLA * HP, 0), (dv_a, H_MLA * VD, 0), (kv_raw, 2 * H_MLA * NOPE, 0),
                    (proj, 128, P_KPE // 128)] + tabs, [W["gk"]],
        [(2 * H_MLA * NOPE, MX), (128, MX)], [(1, HP)], tile=128, name="mla_k_bwd")
    dckvn = _mm(dkv_raw, W["w_ukv"], "nt", name="kv_up_dx")
    g_ukv = _mm(ckvn, dkv_raw, "tn", name="kv_up_dw")

    def lora_bwd(dcq, dckv, cq, ckv, gq, gkv):
        xq, rq = _rms(cq, Q_LORA)
        xk, rk = _rms(ckv, KV_LORA)
        return ([_rms_bwd(dcq * gq, xq, rq, Q_LORA), _rms_bwd(dckv * gkv, xk, rk, KV_LORA)],
                [_colsum(dcq * xq), _colsum(dckv * xk)])

    dc_q, dc_kv, dg_qlora, dg_kvlora = _rowmap(
        lora_bwd, [(dcqn, Q_LORA, 0), (dckvn, KV_LORA, 0), (proj, Q_LORA, P_CQ // Q_LORA),
                   (proj, KV_LORA, P_CKV // KV_LORA)], [W["g_qlora"], W["g_kvlora"]],
        [(Q_LORA, MX), (KV_LORA, MX)], [(1, Q_LORA), (1, KV_LORA)], tile=256, name="lora_bwd")

    nc = S // LCH
    dg16 = dgates[:, :, 0:2, :].reshape(2, HM, nc, 2, LCH).transpose(2, 4, 0, 3, 1).reshape(S, NG)
    dg128 = jnp.pad(dg16, ((0, 0), (0, 128 - NG)))

    def assemble(dqk, dv0, dv1, dom, dcq, dckv, dkp, dgp):
        f = lambda t: t.astype(F32)
        return [cat1([f(dqk), dv0 + dv1, f(dom), f(dcq), f(dckv), f(dkp), dgp])], [_colsum(dgp)]

    dproj, dbg = _rowmap(
        assemble, [(dqk_m, 2 * MW, 0), (dvd, MW, 0, 0), (dvd, MW, 0, 1), (do_m, MW, 0), (dc_q, Q_LORA, 0),
                   (dc_kv, KV_LORA, 0), (dkpe, 128, 0), (dg128, 128, 0)], [], [(D_INP, MX)], [(1, 128)],
        tile=128, name="dproj")
    dh = _mm(dproj, W["w_in"], "nt", name="proj_dx")
    g_in = _mm(h, dproj, "tn", name="proj_dw")

    def ln1_bwd(dhv, xv, dx1v, g, mv):
        xhat, r = _rms(xv, D)
        dn = dhv * (1.0 + mv[1:2])
        return [dx1v + _rms_bwd(dn * g, xhat, r, D)], [_colsum(dhv), _colsum(dhv * xhat * g), _colsum(dn * xhat)]

    gx, dshift1, dscale1, dg_mix = _rowmap(ln1_bwd, [(dh, D, 0), (x, D, 0), (dx1, D, 0)], [W["g_mix"], modv],
                                           [(D, F32)], [(1, D)] * 3, tile=128, name="ln1_bwd")
    dmodv = jnp.concatenate([dshift1, dscale1, dgate1, dshift2, dscale2, dgate2], axis=0)
    grads = dict(w_in=g_in, w_uq=g_uq, w_ukv=g_ukv, w_out=g_out, w_ff1=g_ff1, w_ff2=g_ff2,
                 norm_mix_g=dg_mix, b_gates=dbg[:, :NG], conv_w=dconv_w8[:CONVW], conv_b=dconv_b,
                 q_lora_g=dg_qlora, kv_lora_g=dg_kvlora, q_norm_g=dgq[:, :QK], k_norm_g=dgk[:, :QK],
                 mlstm_norm_g=dg_mn, norm_mlp_g=dg_mlp)
    return loss, gx, dmodv, grads


def _prep_weights(w_in, w_uq, w_ukv, w_out, w_ff1, w_ff2, norm_mix_g, norm_mlp_g, q_lora_g, kv_lora_g,
                  q_norm_g, k_norm_g, mlstm_norm_g, conv_w, conv_b, b_gates):
    MX = _MXU_DTYPE
    padg = lambda g: jnp.pad(g.reshape(1, QK).astype(F32), ((0, 0), (0, HP - QK)))
    return dict(
        w_in=_pad_w_in(w_in).astype(MX), w_uq=_pad_w_uq(w_uq).astype(MX), w_ukv=_perm_w_ukv(w_ukv).astype(MX),
        w_out=w_out.astype(MX), w_ff1=w_ff1.astype(MX), w_ff2=w_ff2.astype(MX),
        g_mix=norm_mix_g.reshape(1, D), g_mlp=norm_mlp_g.reshape(1, D), g_qlora=q_lora_g.reshape(1, Q_LORA),
        g_kvlora=kv_lora_g.reshape(1, KV_LORA), gq=padg(q_norm_g), gk=padg(k_norm_g),
        g_mn=mlstm_norm_g.reshape(1, MW), conv_w8=jnp.pad(conv_w.reshape(CONVW, 2 * MW), ((0, 8 - CONVW), (0, 0))),
        conv_b=conv_b.reshape(1, 2 * MW), bg_row=jnp.pad(b_gates.reshape(1, NG), ((0, 0), (0, 128 - NG))))


MESH = pl.DeviceIdType.MESH
N_DEV = 8
N_CHIP = 4


def _comm_call(body, **kw):
    if _INTERPRET:
        kw["interpret"] = pltpu.InterpretParams()
    return pl.pallas_call(body, **kw)


def _allgather8(blk, *, name):
    m_per, n = blk.shape

    def body(x_ref, out_ref, send_sems, recv_sems, local_sem):
        x, y, c = lax.axis_index("x"), lax.axis_index("y"), lax.axis_index("c")
        me, sibling = (x, y, c), (x, y, 1 - c)
        chips = [(1 - x, y), (x, 1 - y), (1 - x, 1 - y)]

        def rows(px, py, pc):
            return out_ref.at[pl.ds((4 * px + 2 * py + pc) * m_per, m_per), :]

        def copy(k, block, to, src=None):
            return pltpu.make_async_remote_copy(
                src_ref=rows(*block) if src is None else src, dst_ref=rows(*block),
                send_sem=send_sems.at[k], recv_sem=recv_sems.at[k], device_id=to, device_id_type=MESH)

        mine = pltpu.make_async_copy(x_ref, rows(*me), local_sem)
        mine.start()
        first = [copy(0, me, sibling, src=x_ref)]
        first += [copy(1 + j, me, (*chip, c), src=x_ref) for j, chip in enumerate(chips)]
        for cp in first:
            cp.start()
        passed = [copy(4 + j, (*chip, c), sibling) for j, chip in enumerate(chips)]
        for j, chip in enumerate(chips):
            copy(1 + j, (*chip, c), me).wait_recv()
            passed[j].start()
        copy(0, sibling, me).wait_recv()
        for j, chip in enumerate(chips):
            copy(4 + j, (*chip, 1 - c), me).wait_recv()
        for cp in first + passed:
            cp.wait_send()
        mine.wait()

    return _comm_call(
        body, name=name, out_shape=jax.ShapeDtypeStruct((N_DEV * m_per, n), blk.dtype),
        in_specs=[pl.BlockSpec(memory_space=pltpu.VMEM)], out_specs=pl.BlockSpec(memory_space=pltpu.VMEM),
        scratch_shapes=[pltpu.SemaphoreType.DMA((7,)), pltpu.SemaphoreType.DMA((7,)), pltpu.SemaphoreType.DMA],
    )(blk)


def _chip_exchange(arrays, *, gather, name):
    n = len(arrays)
    shard = (lambda a: a.shape) if gather else (lambda a: a.shape[1:])

    def body(*refs):
        ins, outs = refs[:n], refs[n:2 * n]
        send_sems, recv_sems, local_sems = refs[2 * n:]
        x, y, c = lax.axis_index("x"), lax.axis_index("y"), lax.axis_index("c")
        k = 2 * x + y
        chips = [(1 - x, y), (x, 1 - y), (1 - x, 1 - y)]

        def remote(a, j):
            px, py = chips[j]
            src = ins[a] if gather else ins[a].at[2 * px + py]
            return pltpu.make_async_remote_copy(
                src_ref=src, dst_ref=outs[a].at[k], send_sem=send_sems.at[3 * a + j],
                recv_sem=recv_sems.at[3 * a + j], device_id=(px, py, c), device_id_type=MESH)

        def arrival(a, j):
            px, py = chips[j]
            src = ins[a] if gather else ins[a].at[k]
            return pltpu.make_async_remote_copy(
                src_ref=src, dst_ref=outs[a].at[2 * px + py], send_sem=send_sems.at[3 * a + j],
                recv_sem=recv_sems.at[3 * a + j], device_id=(px, py, c), device_id_type=MESH)

        local = [pltpu.make_async_copy(ins[a] if gather else ins[a].at[k], outs[a].at[k], local_sems.at[a])
                 for a in range(n)]
        for cp in local:
            cp.start()
        sent = [remote(a, j) for a in range(n) for j in range(3)]
        for cp in sent:
            cp.start()
        for a in range(n):
            for j in range(3):
                arrival(a, j).wait_recv()
        for cp in sent:
            cp.wait_send()
        for cp in local:
            cp.wait()

    hbm = pl.BlockSpec(memory_space=pltpu.HBM)
    return _comm_call(
        body, name=name,
        out_shape=[jax.ShapeDtypeStruct((N_CHIP, *shard(a)), a.dtype) for a in arrays],
        in_specs=[hbm] * n, out_specs=[hbm] * n,
        scratch_shapes=[pltpu.SemaphoreType.DMA((3 * n,)), pltpu.SemaphoreType.DMA((3 * n,)),
                        pltpu.SemaphoreType.DMA((n,))],
    )(*arrays)


def _sibling_exchange(arrays, *, name):
    n = len(arrays)

    def body(*refs):
        ins, outs = refs[:n], refs[n:2 * n]
        send_sems, recv_sems = refs[2 * n:]
        x, y, c = lax.axis_index("x"), lax.axis_index("y"), lax.axis_index("c")
        cps = [pltpu.make_async_remote_copy(
            src_ref=ins[a], dst_ref=outs[a], send_sem=send_sems.at[a], recv_sem=recv_sems.at[a],
            device_id=(x, y, 1 - c), device_id_type=MESH) for a in range(n)]
        for cp in cps:
            cp.start()
        for cp in cps:
            cp.wait()

    hbm = pl.BlockSpec(memory_space=pltpu.HBM)
    return _comm_call(
        body, name=name, out_shape=[jax.ShapeDtypeStruct(a.shape, a.dtype) for a in arrays],
        in_specs=[hbm] * n, out_specs=[hbm] * n,
        scratch_shapes=[pltpu.SemaphoreType.DMA((n,)), pltpu.SemaphoreType.DMA((n,))],
    )(*arrays)


def _sum_blocks(a, nblk, *, name):
    n = a.shape[1]

    def body(a_ref, o_ref):
        acc = a_ref[pl.ds(0, 8), :]
        for d in range(1, nblk):
            acc = acc + a_ref[pl.ds(8 * d, 8), :]
        o_ref[...] = acc

    return _call(body, name=name, out_shape=jax.ShapeDtypeStruct((8, n), F32))(a)


def _outer8(sct, dm, *, name, tm=256, tn=1024):
    R, N = sct.shape[0], dm.shape[1]
    tm, tn = min(tm, R), min(tn, N)

    def body(s_ref, d_ref, o_ref):
        s, dmv = s_ref[...], d_ref[...]
        acc = s[:, 0:1] * dmv[0:1, :]
        for b in range(1, 8):
            acc = acc + s[:, b:b + 1] * dmv[b:b + 1, :]
        o_ref[...] = acc

    return _call(
        body, name=name, grid=(R // tm, N // tn),
        in_specs=[pl.BlockSpec((tm, 8), lambda i, j: (i, 0)), pl.BlockSpec((8, tn), lambda i, j: (0, j))],
        out_specs=pl.BlockSpec((tm, tn), lambda i, j: (i, j)),
        out_shape=jax.ShapeDtypeStruct((R, N), F32),
        compiler_params=_cparams(("parallel", "parallel")),
    )(sct, dm)


_BC1 = 1.0 - ADAM_B1 ** ADAM_STEP
_BC2 = 1.0 - ADAM_B2 ** ADAM_STEP


def _adamw(w, g_parts, m, v, *, name, tile=128):
    R, C = w.shape
    tile = min(tile, R)
    assert R % tile == 0
    npart = len(g_parts)

    def body(*refs):
        w_ref, m_ref, v_ref = refs[npart:npart + 3]
        g_o, d_o, m_o, v_o = refs[npart + 3:]
        g = refs[0][...].astype(F32)
        for r in refs[1:npart]:
            g = g + r[...].astype(F32)
        mn = ADAM_B1 * m_ref[...] + (1.0 - ADAM_B1) * g
        vn = ADAM_B2 * v_ref[...] + (1.0 - ADAM_B2) * jnp.square(g)
        g_o[...] = g
        m_o[...] = mn
        v_o[...] = vn
        d_o[...] = -ADAM_LR * ((mn / _BC1) / (jnp.sqrt(vn / _BC2) + ADAM_EPS) + ADAM_WD * w_ref[...])

    spec = pl.BlockSpec((tile, C), lambda i: (i, 0))
    return _call(
        body, name=name, grid=(R // tile,), in_specs=[spec] * (npart + 3), out_specs=[spec] * 4,
        out_shape=[jax.ShapeDtypeStruct((R, C), F32)] * 4,
        compiler_params=_cparams(("parallel",)),
    )(*g_parts, w, m, v)


def _pack(vecs, rows8_cols):
    flat = jnp.concatenate([v.reshape(-1).astype(F32) for v in vecs])
    return jnp.pad(flat, (0, 8 * rows8_cols - flat.shape[0])).reshape(8, rows8_cols)


def _unpack(flat, shapes):
    out, o = [], 0
    for s in shapes:
        n = math.prod(s)
        out.append(flat[o:o + n].reshape(s))
        o += n
    return out


_BIG = ("w_in", "w_uq", "w_ukv", "w_out", "w_ff1", "w_ff2")
_SMALL = ("b_ada", "norm_mix_g", "b_gates", "conv_w", "conv_b", "q_lora_g", "kv_lora_g", "q_norm_g", "k_norm_g",
          "mlstm_norm_g", "norm_mlp_g")
_ORDER = ("w_ada", "b_ada", "norm_mix_g", "w_in", "b_gates", "conv_w", "conv_b", "q_lora_g", "w_uq", "kv_lora_g",
          "w_ukv", "q_norm_g", "k_norm_g", "mlstm_norm_g", "w_out", "norm_mlp_g", "w_ff1", "w_ff2")


def kernel(x, c, positions, w_ada, b_ada, norm_mix_g, w_in, b_gates, conv_w, conv_b, q_lora_g, w_uq, kv_lora_g, w_ukv, q_norm_g, k_norm_g, mlstm_norm_g, w_out, norm_mlp_g, w_ff1, w_ff2, loss_target, m_w_ada, m_b_ada, m_norm_mix_g, m_w_in, m_b_gates, m_conv_w, m_conv_b, m_q_lora_g, m_w_uq, m_kv_lora_g, m_w_ukv, m_q_norm_g, m_k_norm_g, m_mlstm_norm_g, m_w_out, m_norm_mlp_g, m_w_ff1, m_w_ff2, v_w_ada, v_b_ada, v_norm_mix_g, v_w_in, v_b_gates, v_conv_w, v_conv_b, v_q_lora_g, v_w_uq, v_kv_lora_g, v_w_ukv, v_q_norm_g, v_k_norm_g, v_mlstm_norm_g, v_w_out, v_norm_mlp_g, v_w_ff1, v_w_ff2):
    args = dict(locals())
    wts = {n: args[n] for n in _ORDER}
    mom = {n: args["m_" + n] for n in _ORDER}
    var = {n: args["v_" + n] for n in _ORDER}
    MX = _MXU_DTYPE
    xi, yi, ci = lax.axis_index("x"), lax.axis_index("y"), lax.axis_index("c")
    chip = 2 * xi + yi
    dev = 2 * chip + ci
    S = x.shape[1]
    CS = 2 * MW // N_CHIP
    GS = DM // N_CHIP

    pk = _pack([c, conv_w, mlstm_norm_g], 1024)
    allpk = _allgather8(pk, name="gather_small").reshape(N_DEV, 8 * 1024)
    c_all = allpk[:, :D]
    per_chip = allpk[0::2]
    conv_w_full = per_chip[:, D:D + CONVW * CS].reshape(N_CHIP, CONVW, CS).transpose(1, 0, 2).reshape(CONVW, 2 * MW)
    o = D + CONVW * CS
    mn_full = per_chip[:, o:o + HM * GS].reshape(N_CHIP, HM, GS).transpose(1, 0, 2).reshape(HM, DM)

    (sc,) = _rowmap(lambda cv: ([cv * _sigmoid(cv)], []), [(c_all, D, 0)], [], [(D, F32)], tile=8, name="silu_c")
    ncol = w_ada.shape[2]
    b_cols = lax.dynamic_slice(b_ada, (0, chip * ncol), (1, ncol))
    modp = _mm(sc, w_ada[0], "nn", name="ada_fwd", tm=8, tn=1024, tk=512, extras=(jnp.broadcast_to(b_cols, (8, ncol)),),
               epilogue=lambda r, b: (r + b,))
    modg = _allgather8(modp, name="gather_mod").reshape(N_CHIP, 2, 8, ncol)[:, 0]
    mod_all = modg.transpose(1, 0, 2).reshape(N_DEV, N_CHIP * ncol)
    modv = jnp.pad(lax.dynamic_slice(mod_all, (dev, 0), (1, 6 * D)).reshape(6, D), ((0, 2), (0, 0)))

    shards = [wts[n][0].astype(MX) for n in _BIG]
    gw_in, gw_uq, gw_ukv, gw_out, gw_ff1, gw_ff2 = _chip_exchange(shards, gather=True, name="gather_weights")
    cols = lambda g: g.transpose(1, 0, 2).reshape(g.shape[1], N_CHIP * g.shape[2])
    W = _prep_weights(cols(gw_in), cols(gw_uq), cols(gw_ukv), gw_out.reshape(D, D), cols(gw_ff1),
                      gw_ff2.reshape(DFF, D), norm_mix_g, norm_mlp_g, q_lora_g, kv_lora_g, q_norm_g, k_norm_g,
                      mn_full, conv_w_full, conv_b, b_gates)

    loss, gx, dmodv, g = _device_step(x[0], loss_target[0], positions[0], modv, W)

    small_shapes = [(6 * D,), (D,), (NG,), (CONVW, 2 * MW), (2 * MW,), (Q_LORA,), (KV_LORA,), (QK,), (QK,), (MW,), (D,), (1,)]
    pg = _pack([dmodv, g["norm_mix_g"], g["b_gates"], g["conv_w"], g["conv_b"], g["q_lora_g"], g["kv_lora_g"],
                g["q_norm_g"], g["k_norm_g"], g["mlstm_norm_g"], g["norm_mlp_g"], loss], 4096)
    allpg = _allgather8(pg, name="gather_small_grads")
    tot = _unpack(_sum_blocks(allpg, N_DEV, name="sum_small_grads").reshape(-1), small_shapes)
    dmod_all = allpg.reshape(N_DEV, 8 * 4096)[:, :6 * D]
    gsmall = dict(zip(_SMALL, [tot[0].reshape(1, 6 * D), tot[1].reshape(1, D), tot[2].reshape(1, NG),
                               lax.dynamic_slice(tot[3], (0, chip * CS), (CONVW, CS)).reshape(1, CONVW, CS),
                               tot[4].reshape(1, 2 * MW), tot[5].reshape(1, Q_LORA), tot[6].reshape(1, KV_LORA),
                               tot[7].reshape(1, QK), tot[8].reshape(1, QK),
                               lax.dynamic_slice(tot[9].reshape(HM, DM), (0, chip * GS), (HM, GS)).reshape(1, HM, GS),
                               tot[10].reshape(1, D)]))
    loss_tot = tot[11].reshape(())

    slabs = lambda gfull: gfull.reshape(gfull.shape[0], N_CHIP, -1).transpose(1, 0, 2).astype(MX)
    to_send = [slabs(_unpad_w_in(g["w_in"])), slabs(_unpad_w_uq(g["w_uq"])), slabs(_unperm_w_ukv(g["w_ukv"])),
               g["w_out"].reshape(N_CHIP, D // N_CHIP, D).astype(MX), slabs(g["w_ff1"]),
               g["w_ff2"].reshape(N_CHIP, DFF // N_CHIP, D).astype(MX)]
    got = _chip_exchange(to_send, gather=False, name="scatter_grads")
    part = []
    for nme, r in zip(_BIG, got):
        wd = r.shape[2]
        (p,) = _rowmap(lambda a0, a1, a2, a3: ([(a0.astype(F32) + a1.astype(F32)) + (a2.astype(F32) + a3.astype(F32))], []),
                       [(r, wd, 0, k) for k in range(N_CHIP)], [], [(wd, F32)], tile=256, name="sum_chips_" + nme)
        part.append(p)
    other = _sibling_exchange(part, name="exchange_cores")

    dm_cols = lax.dynamic_slice(dmod_all, (0, chip * ncol), (N_DEV, ncol))
    g_ada = _outer8(sc.T, dm_cols, name="ada_dw")

    res = {}
    for nme, p, q in zip(_BIG, part, other):
        res[nme] = _adamw(wts[nme][0], [p, q], mom[nme][0], var[nme][0], name="adamw_" + nme)
    res["w_ada"] = _adamw(w_ada[0], [g_ada], m_w_ada[0], v_w_ada[0], name="adamw_w_ada")
    sw = _pack([wts[n] for n in _SMALL], 3072)
    sg = _pack([gsmall[n] for n in _SMALL], 3072)
    sm = _pack([mom[n] for n in _SMALL], 3072)
    sv = _pack([var[n] for n in _SMALL], 3072)
    small_res = _adamw(sw, [sg], sm, sv, name="adamw_small", tile=8)
    shapes = [wts[n].shape for n in _SMALL]
    unp = [_unpack(r.reshape(-1), shapes) for r in small_res]
    for i, nme in enumerate(_SMALL):
        res[nme] = tuple(u[i] for u in unp)
    outs = [loss_tot, gx[None]]
    for kind in range(4):
        outs += [res[n][kind].reshape(wts[n].shape) for n in _ORDER]
    return tuple(outs)
```

```python
import functools
import math

import jax
import jax.numpy as jnp
from jax import lax
from jax.experimental import pallas as pl
from jax.experimental.pallas import tpu as pltpu

F32 = jnp.float32
BF16 = jnp.bfloat16
_MXU_DTYPE = jnp.bfloat16
_INTERPRET = False

D = 2048
H_MLA = 8
NOPE = 128
ROPE = 64
QK = NOPE + ROPE
HP = 256
VD = 128
Q_LORA = 512
KV_LORA = 256
HM = 4
DM = 256
MW = HM * DM
LCH = 128
CONVW = 5
NG = 16
DFF = 4 * D
EPS = 1e-6
M_INIT = -1e30
ROPE_THETA = 10000.0
IN_SIZES = (Q_LORA, KV_LORA, ROPE, MW, MW, MW, MW, NG)
D_IN = sum(IN_SIZES)
P_QM, P_KM, P_VM, P_OM, P_CQ, P_CKV, P_KPE, P_G = 0, 1024, 2048, 3072, 4096, 4608, 4864, 4992
D_INP = 5120

ADAM_LR, ADAM_B1, ADAM_B2, ADAM_EPS, ADAM_WD, ADAM_STEP = 0.001, 0.9, 0.999, 1e-08, 0.01, 10

V7X_VMEM_LIMIT = 56 * 1024 * 1024


def _cparams(sem):
    return pltpu.CompilerParams(dimension_semantics=sem, vmem_limit_bytes=V7X_VMEM_LIMIT)


def _call(body, **kw):
    if _INTERPRET:
        kw.pop("compiler_params", None)
        kw["interpret"] = pltpu.InterpretParams()
    return pl.pallas_call(body, **kw)


def _dot(a, b, form):
    dims = {"nn": ((1,), (0,)), "nt": ((1,), (1,)), "tn": ((0,), (0,))}[form]
    return lax.dot_general(a.astype(_MXU_DTYPE), b.astype(_MXU_DTYPE), (dims, ((), ())),
                           preferred_element_type=F32)


def _mm(a, b, form, *, name, out_dtypes=(F32,), epilogue=None, extras=(), tm=1024, tn=1024, tk=2048):
    if form == "nn":
        (M, K), (K2, N) = a.shape, b.shape
    elif form == "nt":
        (M, K), (N, K2) = a.shape, b.shape
    else:
        (K, M), (K2, N) = a.shape, b.shape
    assert K == K2, (a.shape, b.shape, form)
    tm, tn = min(tm, M), min(tn, N)
    tk = max(d for d in range(128, min(tk, K) + 1, 128) if K % d == 0) if K > 128 else K
    assert M % tm == 0 and N % tn == 0 and K % tk == 0, (M, N, K, tm, tn, tk)
    nk = K // tk
    ne, no = len(extras), len(out_dtypes)
    if form == "tn":
        a_spec = pl.BlockSpec((tk, tm), lambda i, j, k: (k, i))
    else:
        a_spec = pl.BlockSpec((tm, tk), lambda i, j, k: (i, k))
    if form == "nt":
        b_spec = pl.BlockSpec((tn, tk), lambda i, j, k: (j, k))
    else:
        b_spec = pl.BlockSpec((tk, tn), lambda i, j, k: (k, j))
    mn_spec = pl.BlockSpec((tm, tn), lambda i, j, k: (i, j))

    def body(a_ref, b_ref, *rest):
        ex, outs, acc = rest[:ne], rest[ne:ne + no], rest[ne + no]
        k = pl.program_id(2)
        prod = _dot(a_ref[...], b_ref[...], form)

        @pl.when(k == 0)
        def _():
            acc[...] = prod

        @pl.when(k > 0)
        def _():
            acc[...] += prod

        @pl.when(k == nk - 1)
        def _():
            r = acc[...]
            vals = (r,) if epilogue is None else epilogue(r, *[e[...] for e in ex])
            for o, v in zip(outs, vals):
                o[...] = v.astype(o.dtype)

    res = _call(
        body, name=name, grid=(M // tm, N // tn, nk),
        in_specs=[a_spec, b_spec] + [mn_spec] * ne,
        out_specs=[mn_spec] * no,
        out_shape=[jax.ShapeDtypeStruct((M, N), dt) for dt in out_dtypes],
        scratch_shapes=[pltpu.VMEM((tm, tn), F32)],
        compiler_params=_cparams(("parallel", "parallel", "arbitrary")),
    )(a, b, *extras)
    return res[0] if no == 1 else res


def _rowmap(fn, rows, bcasts, outs, accs=(), *, tile, name):
    rows = [r if len(r) == 4 else (*r, None) for r in rows]
    S = rows[0][0].shape[-2]
    tile = min(tile, S)
    assert S % tile == 0
    nr, nb, no, na = len(rows), len(bcasts), len(outs), len(accs)

    def body(*refs):
        vals = [r[...] for r in refs[:nr + nb]]
        o_refs, a_refs = refs[nr + nb:nr + nb + no], refs[nr + nb + no:]
        o_vals, a_vals = fn(*vals)
        for r, v in zip(o_refs, o_vals):
            r[...] = v.astype(r.dtype)
        if na:
            @pl.when(pl.program_id(0) == 0)
            def _():
                for r in a_refs:
                    r[...] = jnp.zeros(r.shape, r.dtype)
            for r, v in zip(a_refs, a_vals):
                r[...] += v

    in_specs = []
    for (arr, w, cb, lead) in rows:
        if lead is None:
            in_specs.append(pl.BlockSpec((tile, w), lambda i, cb=cb: (i, cb)))
        else:
            in_specs.append(pl.BlockSpec((None, tile, w), lambda i, cb=cb, lead=lead: (lead, i, cb)))
    in_specs += [pl.BlockSpec(b.shape, lambda i: (0, 0)) for b in bcasts]
    out_specs = [pl.BlockSpec((tile, w), lambda i: (i, 0)) for (w, _) in outs]
    out_specs += [pl.BlockSpec(s, lambda i: (0, 0)) for s in accs]
    out_shape = [jax.ShapeDtypeStruct((S, w), dt) for (w, dt) in outs]
    out_shape += [jax.ShapeDtypeStruct(s, F32) for s in accs]
    return _call(
        body, name=name, grid=(S // tile,), in_specs=in_specs, out_specs=out_specs, out_shape=out_shape,
        compiler_params=_cparams(("arbitrary",)),
    )(*[r[0] for r in rows], *bcasts)


def _colsum(v):
    return jnp.sum(v, axis=0, keepdims=True)


def _rms(x, n):
    r = lax.rsqrt(jnp.sum(x * x, axis=-1, keepdims=True) * (1.0 / n) + EPS)
    return x * r, r


def _rms_bwd(dxhat, xhat, r, n):
    return r * (dxhat - xhat * (jnp.sum(dxhat * xhat, axis=-1, keepdims=True) * (1.0 / n)))


def _rope_fwd(r, cosp, s1, s2):
    return r * cosp + pltpu.roll(r, 32, 1) * s1 + pltpu.roll(r, 96, 1) * s2


def _rope_bwd(d, cosp, s1, s2):
    return d * cosp + pltpu.roll(d * s1, 96, 1) + pltpu.roll(d * s2, 32, 1)


def _sigmoid(x):
    return 1.0 / (1.0 + jnp.exp(-x))


def _halo_specs(tile, halo, width, cb, S, lead=None):
    nh = tile // halo
    last = S // halo - 1
    if lead is None:
        return [
            pl.BlockSpec((tile, width), lambda i: (i, cb)),
            pl.BlockSpec((halo, width), lambda i: (jnp.maximum(i * nh - 1, 0), cb)),
            pl.BlockSpec((halo, width), lambda i: (jnp.minimum((i + 1) * nh, last), cb)),
        ]
    return [
        pl.BlockSpec((None, tile, width), lambda i: (lead, i, cb)),
        pl.BlockSpec((None, halo, width), lambda i: (lead, jnp.maximum(i * nh - 1, 0), cb)),
        pl.BlockSpec((None, halo, width), lambda i: (lead, jnp.minimum((i + 1) * nh, last), cb)),
    ]


def _conv_fwd(proj, conv_w8, conv_b, *, tile=256):
    S = proj.shape[0]
    T = min(tile, S)
    n = S // T
    W = 2 * MW

    def body(x_ref, xp_ref, xn_ref, w_ref, b_ref, q_ref, k_ref, ext):
        i = pl.program_id(0)
        ext[pl.ds(0, 8), :] = xp_ref[...] * (i > 0).astype(F32)
        ext[pl.ds(8, T), :] = x_ref[...]
        ext[pl.ds(8 + T, 8), :] = xn_ref[...] * (i < n - 1).astype(F32)
        w = w_ref[...]
        y = b_ref[...] + w[0:1, :] * ext[pl.ds(6, T), :]
        for o in range(1, CONVW):
            y = y + w[o:o + 1, :] * ext[pl.ds(6 + o, T), :]
        y = y * _sigmoid(y)
        q_ref[...] = y[:, :MW].astype(q_ref.dtype)
        k_ref[...] = (y[:, MW:] * (DM ** -0.5)).astype(k_ref.dtype)

    return _call(
        body, name="conv_fwd", grid=(n,),
        in_specs=_halo_specs(T, 8, W, 0, S) + [pl.BlockSpec((8, W), lambda i: (0, 0)),
                                                 pl.BlockSpec((1, W), lambda i: (0, 0))],
        out_specs=[pl.BlockSpec((T, MW), lambda i: (i, 0))] * 2,
        out_shape=[jax.ShapeDtypeStruct((S, MW), _MXU_DTYPE)] * 2,
        scratch_shapes=[pltpu.VMEM((T + 16, W), F32)],
        compiler_params=_cparams(("arbitrary",)),
    )(proj, proj, proj, conv_w8, conv_b)


def _conv_bwd(proj, dqd, dkd, conv_w8, conv_b, *, tile=256):
    S = proj.shape[0]
    T = min(tile, S)
    n = S // T
    W = 2 * MW

    def body(x_ref, xp_ref, xn_ref, *rest):
        g = rest[:12]
        w_ref, b_ref, dx_ref, dw_ref, db_ref, ext, edp = rest[12:]
        i = pl.program_id(0)
        mp = (i > 0).astype(F32)
        mn = (i < n - 1).astype(F32)
        ext[pl.ds(0, 16), :] = xp_ref[...] * mp
        ext[pl.ds(16, T), :] = x_ref[...]
        ext[pl.ds(16 + T, 16), :] = xn_ref[...] * mn
        w = w_ref[...]
        pre = b_ref[...] + w[0:1, :] * ext[pl.ds(6, T + 16), :]
        for o in range(1, CONVW):
            pre = pre + w[o:o + 1, :] * ext[pl.ds(6 + o, T + 16), :]
        sg = _sigmoid(pre)
        dsilu = sg * (1.0 + pre * (1.0 - sg))
        for half, (a0, a1) in enumerate(((g[0:3], g[3:6]), (g[6:9], g[9:12]))):
            sc = 1.0 if half == 0 else DM ** -0.5
            cols = pl.ds(half * MW, MW)
            edp[pl.ds(0, 8), cols] = (a0[1][...] + a1[1][...]) * (mp * sc)
            edp[pl.ds(8, T), cols] = (a0[0][...] + a1[0][...]) * sc
            edp[pl.ds(8 + T, 8), cols] = (a0[2][...] + a1[2][...]) * (mn * sc)
        edp[...] = edp[...] * dsilu
        dpm = edp[pl.ds(8, T), :]
        dx = w[0:1, :] * edp[pl.ds(10, T), :]
        for o in range(1, CONVW):
            dx = dx + w[o:o + 1, :] * edp[pl.ds(10 - o, T), :]
        dx_ref[...] = dx.astype(dx_ref.dtype)

        @pl.when(i == 0)
        def _():
            dw_ref[...] = jnp.zeros(dw_ref.shape, F32)
            db_ref[...] = jnp.zeros(db_ref.shape, F32)

        for o in range(CONVW):
            dw_ref[pl.ds(o, 1), :] += _colsum(ext[pl.ds(14 + o, T), :] * dpm)
        db_ref[...] += _colsum(dpm)

    gspecs = []
    for arr in (dqd, dkd):
        for d in (0, 1):
            gspecs += _halo_specs(T, 8, MW, 0, S, lead=d)
    return _call(
        body, name="conv_bwd", grid=(n,),
        in_specs=_halo_specs(T, 16, W, 0, S) + gspecs + [pl.BlockSpec((8, W), lambda i: (0, 0)),
                                                          pl.BlockSpec((1, W), lambda i: (0, 0))],
        out_specs=[pl.BlockSpec((T, W), lambda i: (i, 0)), pl.BlockSpec((8, W), lambda i: (0, 0)),
                   pl.BlockSpec((1, W), lambda i: (0, 0))],
        out_shape=[jax.ShapeDtypeStruct((S, W), _MXU_DTYPE), jax.ShapeDtypeStruct((8, W), F32),
                   jax.ShapeDtypeStruct((1, W), F32)],
        scratch_shapes=[pltpu.VMEM((T + 32, W), F32), pltpu.VMEM((T + 16, W), F32)],
        compiler_params=_cparams(("arbitrary",)),
    )(proj, proj, proj, *([dqd] * 6), *([dkd] * 6), conv_w8, conv_b)


_ATT_SCALE = QK ** -0.5
_LOG2E = math.log2(math.e)
_Q_PRESCALE = _ATT_SCALE * _LOG2E
_ATT_SPLIT = 2


def _side_exchange(side, gather, grid):
    ns = len(side)
    io = _exchange_io(side, gather) if ns else dict(specs=[], out_shape=[], scratch=[])

    def wrap(refs_in, refs_out, sems):
        if not ns:
            return (lambda: None), (lambda: None)
        start, wait = _exchange_ops(refs_in, refs_out, *sems, gather=gather)
        ids = [pl.program_id(a) for a in range(len(grid))]
        first = functools.reduce(jnp.logical_and, [i == 0 for i in ids])
        last = functools.reduce(jnp.logical_and, [i == g - 1 for i, g in zip(ids, grid)])
        return (lambda: pl.when(first)(start)), (lambda: pl.when(last)(wait))

    return ns, io, wrap


def _attn_fwd(q, k, v, *, side=(), tq=512, tk=512):
    S = q.shape[0]
    tq, tk = min(tq, S), min(tk, S)
    nkv = S // tk
    hq = tq // _ATT_SPLIT
    grid = (H_MLA, S // tq)
    ns, io, wrap = _side_exchange(side, True, grid)

    def body(q_ref, k_ref, v_ref, *rest):
        o_ref, lse_ref = rest[ns:ns + 2]
        m_s, l_s, acc_s = rest[2 * ns + 2:2 * ns + 5]
        side_start, side_wait = wrap(rest[:ns], rest[ns + 2:2 * ns + 2], rest[2 * ns + 5:])
        side_start()
        m_s[...] = jnp.full(m_s.shape, -1e30, F32)
        l_s[...] = jnp.zeros(l_s.shape, F32)
        acc_s[...] = jnp.zeros(acc_s.shape, F32)

        def step(j, carry):
            rows = pl.ds(pl.multiple_of(j * tk, tk), tk)
            kj, vj = k_ref[rows, :], v_ref[rows, :]
            for a in range(_ATT_SPLIT):
                r = pl.ds(a * hq, hq)
                s = _dot(q_ref[r, :], kj, "nt")
                m_old = m_s[r, :]
                m_new = jnp.maximum(m_old, jnp.max(s, axis=1, keepdims=True))
                p = jnp.exp2(s - m_new)
                alpha = jnp.exp2(m_old - m_new)
                l_s[r, :] = alpha * l_s[r, :] + jnp.sum(p, axis=1, keepdims=True)
                acc_s[r, :] = alpha * acc_s[r, :] + _dot(p, vj, "nn")
                m_s[r, :] = m_new
            return carry

        lax.fori_loop(0, nkv, step, 0, unroll=2 if nkv % 2 == 0 else 1)
        o_ref[...] = (acc_s[...] / l_s[...]).astype(o_ref.dtype)
        lse_ref[...] = m_s[...] + jnp.log2(l_s[...])
        side_wait()

    res = _call(
        body, name="attn_fwd", grid=grid,
        in_specs=[pl.BlockSpec((tq, HP), lambda h, i: (i, h)),
                  pl.BlockSpec((S, HP), lambda h, i: (0, h)),
                  pl.BlockSpec((S, VD), lambda h, i: (0, h))] + io["specs"],
        out_specs=[pl.BlockSpec((tq, VD), lambda h, i: (i, h)),
                   pl.BlockSpec((None, tq, 1), lambda h, i: (h, i, 0))] + io["specs"],
        out_shape=[jax.ShapeDtypeStruct((S, H_MLA * VD), _MXU_DTYPE),
                   jax.ShapeDtypeStruct((H_MLA, S, 1), F32)] + io["out_shape"],
        scratch_shapes=[pltpu.VMEM((tq, 1), F32), pltpu.VMEM((tq, 1), F32), pltpu.VMEM((tq, VD), F32)]
        + io["scratch"],
        compiler_params=_cparams(("arbitrary", "arbitrary")),
    )(q, k, v, *side)
    return res[0], res[1], list(res[2:])


def _attn_bwd_dkv(q, k, v, dcat, lse_r, dl_r, *, side=(), tq=512, tk=512):
    S = q.shape[0]
    tq, tk = min(tq, S), min(tk, S)
    nq = S // tq
    hk = tk // _ATT_SPLIT
    grid = (H_MLA, S // tk)
    ns, io, wrap = _side_exchange(side, False, grid)

    def body(q_ref, k_ref, v_ref, do_ref, lse_ref, dl_ref, *rest):
        dk_ref, dv_ref = rest[ns:ns + 2]
        side_start, side_wait = wrap(rest[:ns], rest[ns + 2:2 * ns + 2], rest[2 * ns + 2:])
        side_start()
        dk_ref[...] = jnp.zeros(dk_ref.shape, F32)
        dv_ref[...] = jnp.zeros(dv_ref.shape, F32)

        def step(i, carry):
            rows = pl.ds(pl.multiple_of(i * tq, tq), tq)
            qi, doi = q_ref[rows, :], do_ref[rows, :].astype(_MXU_DTYPE)
            lse, dl = lse_ref[i], dl_ref[i]
            for a in range(_ATT_SPLIT):
                r = pl.ds(a * hk, hk)
                pt = jnp.exp2(_dot(k_ref[r, :], qi, "nt") - lse)
                dv_ref[r, :] += _dot(pt, doi, "nn")
                dst = pt * (_dot(v_ref[r, :], doi, "nt") - dl)
                dk_ref[r, :] += _dot(dst, qi, "nn")
            return carry

        lax.fori_loop(0, nq, step, 0, unroll=2 if nq % 2 == 0 else 1)
        dk_ref[...] = dk_ref[...] * (1.0 / _LOG2E)
        side_wait()

    res = _call(
        body, name="attn_bwd_dkv", grid=grid,
        in_specs=[pl.BlockSpec((S, HP), lambda h, j: (0, h)),
                  pl.BlockSpec((tk, HP), lambda h, j: (j, h)),
                  pl.BlockSpec((tk, VD), lambda h, j: (j, h)),
                  pl.BlockSpec((S, VD), lambda h, j: (0, h)),
                  pl.BlockSpec((None, nq, 1, tq), lambda h, j: (h, 0, 0, 0)),
                  pl.BlockSpec((None, nq, 1, tq), lambda h, j: (h, 0, 0, 0))] + io["specs"],
        out_specs=[pl.BlockSpec((tk, HP), lambda h, j: (j, h)),
                   pl.BlockSpec((tk, VD), lambda h, j: (j, h))] + io["specs"],
        out_shape=[jax.ShapeDtypeStruct((S, H_MLA * HP), F32), jax.ShapeDtypeStruct((S, H_MLA * VD), F32)]
        + io["out_shape"],
        scratch_shapes=io["scratch"],
        compiler_params=_cparams(("arbitrary", "arbitrary")),
    )(q, k, v, dcat, lse_r, dl_r, *side)
    return res[0], res[1], list(res[2:])


def _attn_bwd_dq(q, k, v, dcat, lse_c, dl_c, *, tq=512, tk=512):
    S = q.shape[0]
    tq, tk = min(tq, S), min(tk, S)
    nkv = S // tk

    hq = tq // _ATT_SPLIT

    def body(q_ref, k_ref, v_ref, do_ref, lse_ref, dl_ref, dq_ref, do_s):
        dq_ref[...] = jnp.zeros(dq_ref.shape, F32)
        do_s[...] = do_ref[...].astype(do_s.dtype)

        def step(j, carry):
            rows = pl.ds(pl.multiple_of(j * tk, tk), tk)
            kj, vj = k_ref[rows, :], v_ref[rows, :]
            for a in range(_ATT_SPLIT):
                r = pl.ds(a * hq, hq)
                p = jnp.exp2(_dot(q_ref[r, :], kj, "nt") - lse_ref[r, :])
                ds = p * (_dot(do_s[r, :], vj, "nt") - dl_ref[r, :])
                dq_ref[r, :] += _dot(ds, kj, "nn")
            return carry

        lax.fori_loop(0, nkv, step, 0, unroll=2 if nkv % 2 == 0 else 1)
        dq_ref[...] = dq_ref[...] * _ATT_SCALE

    return _call(
        body, name="attn_bwd_dq", grid=(H_MLA, S // tq),
        in_specs=[pl.BlockSpec((tq, HP), lambda h, i: (i, h)),
                  pl.BlockSpec((S, HP), lambda h, i: (0, h)),
                  pl.BlockSpec((S, VD), lambda h, i: (0, h)),
                  pl.BlockSpec((tq, VD), lambda h, i: (i, h)),
                  pl.BlockSpec((None, tq, 1), lambda h, i: (h, i, 0)),
                  pl.BlockSpec((None, tq, 1), lambda h, i: (h, i, 0))],
        out_specs=pl.BlockSpec((tq, HP), lambda h, i: (i, h)),
        out_shape=jax.ShapeDtypeStruct((S, H_MLA * HP), F32),
        scratch_shapes=[pltpu.VMEM((tq, VD), _MXU_DTYPE)],
        compiler_params=_cparams(("parallel", "arbitrary")),
    )(q, k, v, dcat, lse_c, dl_c)


def _mlstm_chunk_terms(g, q, k, v, gates, gates_t, bg_row, C, n_row, m):
    L = LCH
    d = g // HM
    h = g % HM
    i_idx = d * 8 + h
    f_idx = d * 8 + 4 + h
    rr = lax.broadcasted_iota(jnp.int32, (L, L), 0)
    cc = lax.broadcasted_iota(jnp.int32, (L, L), 1)
    order = (rr - cc) * (1 - 2 * d)
    tri = order >= 0
    eye = rr == cc
    lane = lax.broadcasted_iota(jnp.int32, gates.shape, 1)
    sub = lax.broadcasted_iota(jnp.int32, gates_t.shape, 0)
    lane_b = lax.broadcasted_iota(jnp.int32, bg_row.shape, 1)
    pick_c = lambda idx: jnp.sum(jnp.where(lane == idx, gates, 0.0), axis=1, keepdims=True)
    pick_r = lambda idx: jnp.sum(jnp.where(sub == idx, gates_t, 0.0), axis=0, keepdims=True)
    pick_b = lambda idx: jnp.sum(jnp.where(lane_b == idx, bg_row, 0.0), axis=1, keepdims=True)
    i_col, i_row = pick_c(i_idx) + pick_b(i_idx), pick_r(i_idx) + pick_b(i_idx)
    f_col, f_row = pick_c(f_idx) + pick_b(f_idx), pick_r(f_idx) + pick_b(f_idx)
    logsig = lambda x: jnp.minimum(x, 0.0) - jnp.log(1.0 + jnp.exp(-jnp.abs(x)))
    lf_col, lf_row = logsig(f_col), logsig(f_row)
    b_col = jnp.sum(jnp.where(tri, lf_row, 0.0), axis=1, keepdims=True)
    tri_t = order <= 0
    b_row = jnp.sum(jnp.where(tri_t, lf_col, 0.0), axis=0, keepdims=True)
    bL = jnp.sum(lf_row, axis=1, keepdims=True)
    log_inter = b_col + m
    logD = jnp.where(tri, b_col - b_row + i_row, -jnp.inf)
    m_t = jnp.maximum(log_inter, jnp.max(logD, axis=1, keepdims=True))
    Dm = jnp.exp(logD - m_t)
    w_inter = jnp.exp(log_inter - m_t)
    A = _dot(q, k, "nt")
    Sc = A * Dm
    numI = _dot(q, C, "nt")
    qf = q.astype(F32)
    kf = k.astype(F32)
    denI = jnp.sum(qf * n_row, axis=1, keepdims=True)
    num = _dot(Sc, v, "nn") + w_inter * numI
    den = jnp.sum(Sc, axis=1, keepdims=True) + w_inter * denI
    floor = jnp.exp(-m_t)
    Nst = jnp.maximum(jnp.abs(den), floor)
    log_w = bL - b_col + i_col
    m_new = jnp.maximum(bL + m, jnp.max(log_w, axis=0, keepdims=True))
    decay = jnp.exp(bL + m - m_new)
    w_col = jnp.exp(log_w - m_new)
    return dict(tri=tri, eye=eye, f_row=f_row, Dm=Dm, w_inter=w_inter, A=A, Sc=Sc, numI=numI, denI=denI,
                num=num, den=den, floor=floor, Nst=Nst, m_new=m_new, decay=decay, w_col=w_col, qf=qf, kf=kf)


def _mlstm_specs(S, nc):
    jeff = lambda g, j: j + (g // HM) * (nc - 1 - 2 * j)
    return jeff, [
        pl.BlockSpec((LCH, DM), lambda g, j: (jeff(g, j), g % HM)),
        pl.BlockSpec((LCH, DM), lambda g, j: (jeff(g, j), g % HM)),
        pl.BlockSpec((LCH, DM), lambda g, j: (jeff(g, j), P_VM // DM + g % HM)),
        pl.BlockSpec((LCH, 128), lambda g, j: (jeff(g, j), P_G // 128)),
        pl.BlockSpec((NG, LCH), lambda g, j: (0, jeff(g, j))),
        pl.BlockSpec((1, 128), lambda g, j: (0, 0)),
    ]


def _mlstm_fwd(qc, kc, proj, gates_t, bg_row):
    S = qc.shape[0]
    nc = S // LCH
    jeff, specs = _mlstm_specs(S, nc)

    def body(q_ref, k_ref, v_ref, g_ref, gt_ref, bg_ref, h_ref, cst_ref, nm_ref, C_s, n_s, m_s):
        g = pl.program_id(0)

        @pl.when(pl.program_id(1) == 0)
        def _():
            C_s[...] = jnp.zeros(C_s.shape, F32)
            n_s[...] = jnp.zeros(n_s.shape, F32)
            m_s[...] = jnp.full(m_s.shape, M_INIT, F32)

        C, n_row, m = C_s[...], n_s[0:1, :], m_s[0:1, 0:1]
        cst_ref[...] = C
        nm_ref[0:1, :] = n_row
        nm_ref[1:2, :] = jnp.broadcast_to(m, (1, DM))
        nm_ref[2:8, :] = jnp.zeros((6, DM), F32)
        q, k, v = q_ref[...], k_ref[...], v_ref[...]
        t = _mlstm_chunk_terms(g, q, k, v, g_ref[...], gt_ref[...], bg_ref[...], C, n_row, m)
        h_ref[...] = t["num"] / t["Nst"]
        wv = t["w_col"] * v
        C_s[...] = t["decay"] * C + _dot(wv, k, "tn")
        n_s[0:1, :] = t["decay"] * n_row + _colsum(t["w_col"] * t["kf"])
        m_s[...] = jnp.broadcast_to(t["m_new"], m_s.shape)

    return _call(
        body, name="mlstm_fwd", grid=(2 * HM, nc), in_specs=specs,
        out_specs=[pl.BlockSpec((None, LCH, DM), lambda g, j: (g // HM, jeff(g, j), g % HM)),
                   pl.BlockSpec((None, None, DM, DM), lambda g, j: (g, jeff(g, j), 0, 0)),
                   pl.BlockSpec((None, None, 8, DM), lambda g, j: (g, jeff(g, j), 0, 0))],
        out_shape=[jax.ShapeDtypeStruct((2, S, MW), F32),
                   jax.ShapeDtypeStruct((2 * HM, nc, DM, DM), F32),
                   jax.ShapeDtypeStruct((2 * HM, nc, 8, DM), F32)],
        scratch_shapes=[pltpu.VMEM((DM, DM), F32), pltpu.VMEM((8, DM), F32), pltpu.VMEM((8, 128), F32)],
        compiler_params=_cparams(("parallel", "arbitrary")),
    )(qc, kc, proj, proj, gates_t, bg_row)


def _mlstm_bwd(qc, kc, proj, gates_t, bg_row, dh, cst, nm):
    S = qc.shape[0]
    nc = S // LCH
    _, specs = _mlstm_specs(S, nc)
    jb = lambda g, j: (nc - 1 - j) + (g // HM) * (2 * j - (nc - 1))
    respec = lambda bs: pl.BlockSpec(bs.block_shape, (lambda g, j, im=bs.index_map: im(g, nc - 1 - j)))
    specs = [respec(s) for s in specs]

    def body(q_ref, k_ref, v_ref, g_ref, gt_ref, bg_ref, dh_ref, cst_ref, nm_ref,
             dq_ref, dk_ref, dv_ref, dg_ref, dC_s, dn_s):
        g = pl.program_id(0)

        @pl.when(pl.program_id(1) == 0)
        def _():
            dC_s[...] = jnp.zeros(dC_s.shape, F32)
            dn_s[...] = jnp.zeros(dn_s.shape, F32)

        C, n_row, m = cst_ref[...], nm_ref[0:1, :], nm_ref[1:2, 0:1]
        q, k, v = q_ref[...], k_ref[...], v_ref[...]
        t = _mlstm_chunk_terms(g, q, k, v, g_ref[...], gt_ref[...], bg_ref[...], C, n_row, m)
        tri, eye, qf, kf = t["tri"], t["eye"], t["qf"], t["kf"]
        w_inter, w_col, decay, Nst = t["w_inter"], t["w_col"], t["decay"], t["Nst"]
        dC, dn = dC_s[...], dn_s[0:1, :]
        dhv = dh_ref[...]
        hval = t["num"] / Nst
        dnum = dhv / Nst
        dNst = -jnp.sum(dhv * hval, axis=1, keepdims=True) / Nst
        dden = jnp.where(jnp.abs(t["den"]) > t["floor"], jnp.sign(t["den"]) * dNst, 0.0)
        dSc = _dot(dnum, v, "nt") + dden
        dA = dSc * t["Dm"]
        G = dSc * t["Sc"]
        KdC = _dot(k, dC, "nt")
        dq = _dot(dA, k, "nn") + w_inter * _dot(dnum, C, "nn") + (w_inter * dden) * n_row
        dk = _dot(dA, q, "tn") + w_col * _dot(v, dC, "nn") + w_col * dn
        dv = _dot(t["Sc"], dnum, "tn") + w_col * KdC
        dq_ref[...] = dq
        dk_ref[...] = dk
        dv_ref[...] = dv
        dlog_inter = w_inter * (jnp.sum(dnum * t["numI"], axis=1, keepdims=True) + dden * t["denI"])
        rowG = jnp.sum(G, axis=1, keepdims=True)
        colG = jnp.sum(G, axis=0, keepdims=True)
        u_col = w_col * (jnp.sum(v * KdC, axis=1, keepdims=True) + jnp.sum(kf * dn, axis=1, keepdims=True))
        colG_c = jnp.sum(jnp.where(eye, colG, 0.0), axis=1, keepdims=True)
        u_row = jnp.sum(jnp.where(eye, u_col, 0.0), axis=0, keepdims=True)
        db_col = rowG + dlog_inter - u_col - colG_c
        dbL = jnp.sum(u_col, axis=0, keepdims=True) + decay * (
            jnp.sum(jnp.sum(dC * C, axis=1, keepdims=True), axis=0, keepdims=True)
            + jnp.sum(dn * n_row, axis=1, keepdims=True))
        dlf_row = jnp.sum(jnp.where(tri, db_col, 0.0), axis=0, keepdims=True) + dbL
        di_row = colG + u_row
        df_row = dlf_row * (1.0 - _sigmoid(t["f_row"]))
        dg_ref[...] = jnp.zeros(dg_ref.shape, F32)
        dg_ref[0:1, :] = di_row
        dg_ref[1:2, :] = df_row
        dC_s[...] = decay * dC + _dot(w_inter * dnum, q, "tn")
        dn_s[0:1, :] = decay * dn + _colsum((w_inter * dden) * qf)

    return _call(
        body, name="mlstm_bwd", grid=(2 * HM, nc),
        in_specs=specs + [pl.BlockSpec((LCH, DM), lambda g, j: (jb(g, j), g % HM)),
                          pl.BlockSpec((None, None, DM, DM), lambda g, j: (g, jb(g, j), 0, 0)),
                          pl.BlockSpec((None, None, 8, DM), lambda g, j: (g, jb(g, j), 0, 0))],
        out_specs=[pl.BlockSpec((None, LCH, DM), lambda g, j: (g // HM, jb(g, j), g % HM))] * 3
        + [pl.BlockSpec((None, None, 8, LCH), lambda g, j: (g, jb(g, j), 0, 0))],
        out_shape=[jax.ShapeDtypeStruct((2, S, MW), F32)] * 3
        + [jax.ShapeDtypeStruct((2 * HM, nc, 8, LCH), F32)],
        scratch_shapes=[pltpu.VMEM((DM, DM), F32), pltpu.VMEM((8, DM), F32)],
        compiler_params=_cparams(("parallel", "arbitrary")),
    )(qc, kc, proj, proj, gates_t, bg_row, dh, cst, nm)


def _pad_w_in(w):
    cq, ckv, kpe, qm, km, vm, om, gt = _split_in(w)
    z = lambda n: jnp.zeros((w.shape[0], n), w.dtype)
    return jnp.concatenate([qm, km, vm, om, cq, ckv, kpe, z(HP - QK), gt, z(128 - NG)], axis=1)


def _split_in(w):
    out, o = [], 0
    for n in IN_SIZES:
        out.append(w[:, o:o + n])
        o += n
    return out


def _unpad_w_in(g):
    return jnp.concatenate([g[:, P_CQ:P_CQ + Q_LORA], g[:, P_CKV:P_CKV + KV_LORA], g[:, P_KPE:P_KPE + ROPE],
                            g[:, 0:4 * MW], g[:, P_G:P_G + NG]], axis=1)


def _pad_w_uq(w):
    return jnp.pad(w.reshape(Q_LORA, H_MLA, QK), ((0, 0), (0, 0), (0, HP - QK))).reshape(Q_LORA, H_MLA * HP)


def _unpad_w_uq(g):
    return g.reshape(Q_LORA, H_MLA, HP)[:, :, :QK].reshape(Q_LORA, H_MLA * QK)


def _perm_w_ukv(w):
    return w.reshape(KV_LORA, H_MLA, 2, NOPE).transpose(0, 2, 1, 3).reshape(KV_LORA, 2 * H_MLA * NOPE)


def _unperm_w_ukv(g):
    return g.reshape(KV_LORA, 2, H_MLA, NOPE).transpose(0, 2, 1, 3).reshape(KV_LORA, 2 * H_MLA * NOPE)


def _rope_tables(positions):
    half = ROPE // 2
    freqs = ROPE_THETA ** (-jnp.arange(half, dtype=F32) / half)
    ang = positions.astype(F32)[:, None] * freqs
    cos, sin = jnp.cos(ang), jnp.sin(ang)
    z32, z64 = jnp.zeros_like(cos), jnp.zeros((cos.shape[0], 64), F32)
    return (jnp.concatenate([cos, cos, z64], axis=1), jnp.concatenate([z32, sin, z64], axis=1),
            jnp.concatenate([-sin, z32, z64], axis=1))


def _device_step(x, tgt, positions, modv, W, late=None):
    S = x.shape[0]
    MX = _MXU_DTYPE
    cosp, rs1, rs2 = _rope_tables(positions)
    tabs = [(cosp, 128, 0), (rs1, 128, 0), (rs2, 128, 0)]
    cat1 = lambda vs: jnp.concatenate(vs, axis=1)
    hsl = lambda hh, w: slice(hh * w, (hh + 1) * w)

    def ln1(xv, g, mv):
        xhat, _ = _rms(xv, D)
        return [xhat * g * (1.0 + mv[1:2]) + mv[0:1]], []

    (h,) = _rowmap(ln1, [(x, D, 0)], [W["g_mix"], modv], [(D, MX)], tile=128, name="ln1")
    proj = _mm(h, W["w_in"], "nn", name="proj")

    def lora(cq, ckv, gq, gkv):
        return [_rms(cq, Q_LORA)[0] * gq, _rms(ckv, KV_LORA)[0] * gkv], []

    cqn, ckvn = _rowmap(lora, [(proj, Q_LORA, P_CQ // Q_LORA), (proj, KV_LORA, P_CKV // KV_LORA)],
                        [W["g_qlora"], W["g_kvlora"]], [(Q_LORA, MX), (KV_LORA, MX)], tile=256, name="lora_norm")
    q_raw = _mm(cqn, W["w_uq"], "nn", name="q_up")
    kv_raw = _mm(ckvn, W["w_ukv"], "nn", name="kv_up")

    def mla_q(qr, cp, a1, a2, gq):
        outs = []
        for hh in range(H_MLA):
            y = _rms(qr[:, hsl(hh, HP)], QK)[0] * gq
            outs += [y[:, :NOPE], _rope_fwd(y[:, NOPE:], cp, a1, a2)]
        return [cat1(outs) * _Q_PRESCALE], []

    (qh,) = _rowmap(mla_q, [(q_raw, H_MLA * HP, 0)] + tabs, [W["gq"]], [(H_MLA * HP, MX)], tile=128, name="mla_q")

    def mla_k(kvr, kpe, cp, a1, a2, gk):
        outs = []
        for hh in range(H_MLA):
            y = _rms(cat1([kvr[:, hsl(hh, NOPE)], kpe]), QK)[0] * gk
            outs += [y[:, :NOPE], _rope_fwd(y[:, NOPE:], cp, a1, a2)]
        return [cat1(outs), kvr[:, H_MLA * NOPE:]], []

    kh, vh = _rowmap(mla_k, [(kv_raw, 2 * H_MLA * NOPE, 0), (proj, 128, P_KPE // 128)] + tabs, [W["gk"]],
                     [(H_MLA * HP, MX), (H_MLA * VD, MX)], tile=128, name="mla_k")
    attn_o, lse, gathered = _attn_fwd(qh, kh, vh, side=late or ())
    if late:
        W = dict(W, w_out=gathered[0].reshape(D, D), w_ff1=_cols(gathered[1]), w_ff2=gathered[2].reshape(DFF, D))

    qc, kc = _conv_fwd(proj, W["conv_w8"], W["conv_b"])
    gates_t = proj[:, P_G:P_G + NG].T
    h_dirs, cst, nm = _mlstm_fwd(qc, kc, proj, gates_t, W["bg_row"])
    hrows = [(h_dirs, MW, 0, 0), (h_dirs, MW, 0, 1), (proj, MW, P_OM // MW)]

    def ml_out(ao, hf, hb, om, gmn):
        outs = [ao.astype(F32)]
        hs = hf + hb
        for hh in range(HM):
            sl = hsl(hh, DM)
            outs.append(_sigmoid(om[:, sl]) * _rms(hs[:, sl], DM)[0] * gmn[:, sl])
        return [cat1(outs)], []

    (cat,) = _rowmap(ml_out, [(attn_o, MW, 0)] + hrows, [W["g_mn"]], [(D, MX)], tile=128, name="ml_out")
    mixed = _mm(cat, W["w_out"], "nn", name="out_proj")

    def res_ln2(xv, mx, g, mv):
        x1 = xv + mv[2:3] * mx
        return [x1, _rms(x1, D)[0] * g * (1.0 + mv[4:5]) + mv[3:4]], []

    x1, h2 = _rowmap(res_ln2, [(x, D, 0), (mixed, D, 0)], [W["g_mlp"], modv], [(D, F32), (D, MX)],
                     tile=128, name="res_ln2")
    a, u = _mm(h2, W["w_ff1"], "nn", name="ff1", out_dtypes=(MX, MX),
               epilogue=lambda r: (jnp.square(jnp.maximum(r, 0.0)), r))
    y = _mm(a, W["w_ff2"], "nn", name="ff2")

    def final(x1v, yv, tv, mv):
        err = x1v + mv[5:6] * yv - tv
        dout = err * (1.0 / D)
        loss = jnp.sum(jnp.sum(0.5 * err * dout, axis=1, keepdims=True), axis=0, keepdims=True)
        return [dout, mv[5:6] * dout], [loss, _colsum(dout * yv)]

    dout, dy, loss, dgate2 = _rowmap(final, [(x1, D, 0), (y, D, 0), (tgt, D, 0)], [modv], [(D, F32), (D, MX)],
                                     [(1, 1), (1, D)], tile=128, name="loss_head")

    du = _mm(dy, W["w_ff2"], "nt", name="ff2_dx", out_dtypes=(MX,), extras=(u,),
             epilogue=lambda r, uv: (r * (2.0 * jnp.maximum(uv.astype(F32), 0.0)),))
    g_ff2 = _mm(a, dy, "tn", name="ff2_dw")
    dh2 = _mm(du, W["w_ff1"], "nt", name="ff1_dx")
    g_ff1 = _mm(h2, du, "tn", name="ff1_dw")

    def ln2_bwd(dh2v, x1v, doutv, mxv, g, mv):
        xhat, r = _rms(x1v, D)
        dn2 = dh2v * (1.0 + mv[4:5])
        dx1 = doutv + _rms_bwd(dn2 * g, xhat, r, D)
        return [dx1, mv[2:3] * dx1], [_colsum(dh2v), _colsum(dh2v * xhat * g), _colsum(dn2 * xhat), _colsum(dx1 * mxv)]

    dx1, dmixed, dshift2, dscale2, dg_mlp, dgate1 = _rowmap(
        ln2_bwd, [(dh2, D, 0), (x1, D, 0), (dout, D, 0), (mixed, D, 0)], [W["g_mlp"], modv],
        [(D, F32), (D, MX)], [(1, D)] * 4, tile=128, name="ln2_bwd")
    dcat = _mm(dmixed, W["w_out"], "nt", name="out_dx")
    g_out = _mm(cat, dmixed, "tn", name="out_dw")

    def ml_out_bwd(dml, hf, hb, om, gmn):
        hs = hf + hb
        dhs, dos, dgs = [], [], []
        for hh in range(HM):
            sl = hsl(hh, DM)
            xhat, r = _rms(hs[:, sl], DM)
            g, sg, d = gmn[:, sl], _sigmoid(om[:, sl]), dml[:, sl]
            dos.append(d * xhat * g * sg * (1.0 - sg))
            dhn = d * sg
            dgs.append(_colsum(dhn * xhat))
            dhs.append(_rms_bwd(dhn * g, xhat, r, DM))
        return [cat1(dhs), cat1(dos)], [cat1(dgs)]

    dhs, do_m, dg_mn = _rowmap(ml_out_bwd, [(dcat, MW, 1)] + hrows, [W["g_mn"]], [(MW, F32), (MW, MX)],
                               [(1, MW)], tile=128, name="ml_out_bwd")
    dqd, dkd, dvd, dgates = _mlstm_bwd(qc, kc, proj, gates_t, W["bg_row"], dhs, cst, nm)
    dqk_m, dconv_w8, dconv_b = _conv_bwd(proj, dqd, dkd, W["conv_w8"], W["conv_b"])

    def delta_fn(ao, dov):
        lane = lax.broadcasted_iota(jnp.int32, (ao.shape[0], 128), 1)
        acc = jnp.zeros((ao.shape[0], 128), F32)
        for hh in range(H_MLA):
            sl = hsl(hh, VD)
            acc = acc + jnp.where(lane == hh, jnp.sum(ao[:, sl].astype(F32) * dov[:, sl], axis=1, keepdims=True), 0.0)
        return [acc], []

    (dl,) = _rowmap(delta_fn, [(attn_o, MW, 0), (dcat, MW, 0)], [], [(128, F32)], tile=256, name="attn_delta")
    tq = min(512, S)
    dl_hs = dl[:, :H_MLA].T
    side = [g_out.reshape(N_CHIP, D // N_CHIP, D).astype(MX), _slabs(g_ff1).astype(MX),
            g_ff2.reshape(N_CHIP, DFF // N_CHIP, D).astype(MX)] if late else ()
    dk_a, dv_a, late_got = _attn_bwd_dkv(qh, kh, vh, dcat, lse.reshape(H_MLA, S // tq, 1, tq),
                                         dl_hs.reshape(H_MLA, S // tq, 1, tq), side=side)
    dq_a = _attn_bwd_dq(qh, kh, vh, dcat, lse, dl_hs.reshape(H_MLA, S, 1))

    def mla_q_bwd(dqv, qr, cp, a1, a2, gq):
        outs, dg = [], 0.0
        for hh in range(H_MLA):
            sl = hsl(hh, HP)
            xhat, r = _rms(qr[:, sl], QK)
            d = dqv[:, sl]
            dyv = cat1([d[:, :NOPE], _rope_bwd(d[:, NOPE:], cp, a1, a2)])
            dg = dg + _colsum(dyv * xhat)
            outs.append(_rms_bwd(dyv * gq, xhat, r, QK))
        return [cat1(outs)], [dg]

    dq_raw, dgq = _rowmap(mla_q_bwd, [(dq_a, H_MLA * HP, 0), (q_raw, H_MLA * HP, 0)] + tabs, [W["gq"]],
                          [(H_MLA * HP, MX)], [(1, HP)], tile=128, name="mla_q_bwd")
    dcqn = _mm(dq_raw, W["w_uq"], "nt", name="q_up_dx")
    g_uq = _mm(cqn, dq_raw, "tn", name="q_up_dw")

    def mla_k_bwd(dkv, dvv, kvr, kpe, cp, a1, a2, gk):
        dkn, dg, dkpe = [], 0.0, 0.0
        for hh in range(H_MLA):
            xhat, r = _rms(cat1([kvr[:, hsl(hh, NOPE)], kpe]), QK)
            d = dkv[:, hsl(hh, HP)]
            dyv = cat1([d[:, :NOPE], _rope_bwd(d[:, NOPE:], cp, a1, a2)])
            dg = dg + _colsum(dyv * xhat)
            dxv = _rms_bwd(dyv * gk, xhat, r, QK)
            dkn.append(dxv[:, :NOPE])
            dkpe = dkpe + dxv[:, NOPE:]
        return [cat1(dkn + [dvv]), dkpe], [dg]

    dkv_raw, dkpe, dgk = _rowmap(
        mla_k_bwd, [(dk_a, H_MLA * HP, 0), (dv_a, H_MLA * VD, 0), (kv_raw, 2 * H_MLA * NOPE, 0),
                    (proj, 128, P_KPE // 128)] + tabs, [W["gk"]],
        [(2 * H_MLA * NOPE, MX), (128, MX)], [(1, HP)], tile=128, name="mla_k_bwd")
    dckvn = _mm(dkv_raw, W["w_ukv"], "nt", name="kv_up_dx")
    g_ukv = _mm(ckvn, dkv_raw, "tn", name="kv_up_dw")

    def lora_bwd(dcq, dckv, cq, ckv, gq, gkv):
        xq, rq = _rms(cq, Q_LORA)
        xk, rk = _rms(ckv, KV_LORA)
        return ([_rms_bwd(dcq * gq, xq, rq, Q_LORA), _rms_bwd(dckv * gkv, xk, rk, KV_LORA)],
                [_colsum(dcq * xq), _colsum(dckv * xk)])

    dc_q, dc_kv, dg_qlora, dg_kvlora = _rowmap(
        lora_bwd, [(dcqn, Q_LORA, 0), (dckvn, KV_LORA, 0), (proj, Q_LORA, P_CQ // Q_LORA),
                   (proj, KV_LORA, P_CKV // KV_LORA)], [W["g_qlora"], W["g_kvlora"]],
        [(Q_LORA, MX), (KV_LORA, MX)], [(1, Q_LORA), (1, KV_LORA)], tile=256, name="lora_bwd")

    nc = S // LCH
    dg16 = dgates[:, :, 0:2, :].reshape(2, HM, nc, 2, LCH).transpose(2, 4, 0, 3, 1).reshape(S, NG)
    dg128 = jnp.pad(dg16, ((0, 0), (0, 128 - NG)))

    def assemble(dqk, dv0, dv1, dom, dcq, dckv, dkp, dgp):
        f = lambda t: t.astype(F32)
        return [cat1([f(dqk), dv0 + dv1, f(dom), f(dcq), f(dckv), f(dkp), dgp])], [_colsum(dgp)]

    dproj, dbg = _rowmap(
        assemble, [(dqk_m, 2 * MW, 0), (dvd, MW, 0, 0), (dvd, MW, 0, 1), (do_m, MW, 0), (dc_q, Q_LORA, 0),
                   (dc_kv, KV_LORA, 0), (dkpe, 128, 0), (dg128, 128, 0)], [], [(D_INP, MX)], [(1, 128)],
        tile=128, name="dproj")
    dh = _mm(dproj, W["w_in"], "nt", name="proj_dx")
    g_in = _mm(h, dproj, "tn", name="proj_dw")

    def ln1_bwd(dhv, xv, dx1v, g, mv):
        xhat, r = _rms(xv, D)
        dn = dhv * (1.0 + mv[1:2])
        return [dx1v + _rms_bwd(dn * g, xhat, r, D)], [_colsum(dhv), _colsum(dhv * xhat * g), _colsum(dn * xhat)]

    gx, dshift1, dscale1, dg_mix = _rowmap(ln1_bwd, [(dh, D, 0), (x, D, 0), (dx1, D, 0)], [W["g_mix"], modv],
                                           [(D, F32)], [(1, D)] * 3, tile=128, name="ln1_bwd")
    dmodv = jnp.concatenate([dshift1, dscale1, dgate1, dshift2, dscale2, dgate2], axis=0)
    grads = dict(w_in=g_in, w_uq=g_uq, w_ukv=g_ukv, w_out=g_out, w_ff1=g_ff1, w_ff2=g_ff2,
                 norm_mix_g=dg_mix, b_gates=dbg[:, :NG], conv_w=dconv_w8[:CONVW], conv_b=dconv_b,
                 q_lora_g=dg_qlora, kv_lora_g=dg_kvlora, q_norm_g=dgq[:, :QK], k_norm_g=dgk[:, :QK],
                 mlstm_norm_g=dg_mn, norm_mlp_g=dg_mlp)
    grads["late"] = late_got
    return loss, gx, dmodv, grads


def _cols(g):
    return g.transpose(1, 0, 2).reshape(g.shape[1], N_CHIP * g.shape[2])


def _slabs(gfull):
    return gfull.reshape(gfull.shape[0], N_CHIP, -1).transpose(1, 0, 2)


def _prep_weights(w_in, w_uq, w_ukv, w_out, w_ff1, w_ff2, norm_mix_g, norm_mlp_g, q_lora_g, kv_lora_g,
                  q_norm_g, k_norm_g, mlstm_norm_g, conv_w, conv_b, b_gates):
    MX = _MXU_DTYPE
    padg = lambda g: jnp.pad(g.reshape(1, QK).astype(F32), ((0, 0), (0, HP - QK)))
    return dict(
        w_in=_pad_w_in(w_in).astype(MX), w_uq=_pad_w_uq(w_uq).astype(MX), w_ukv=_perm_w_ukv(w_ukv).astype(MX),
        w_out=None if w_out is None else w_out.astype(MX), w_ff1=None if w_ff1 is None else w_ff1.astype(MX),
        w_ff2=None if w_ff2 is None else w_ff2.astype(MX),
        g_mix=norm_mix_g.reshape(1, D), g_mlp=norm_mlp_g.reshape(1, D), g_qlora=q_lora_g.reshape(1, Q_LORA),
        g_kvlora=kv_lora_g.reshape(1, KV_LORA), gq=padg(q_norm_g), gk=padg(k_norm_g),
        g_mn=mlstm_norm_g.reshape(1, MW), conv_w8=jnp.pad(conv_w.reshape(CONVW, 2 * MW), ((0, 8 - CONVW), (0, 0))),
        conv_b=conv_b.reshape(1, 2 * MW), bg_row=jnp.pad(b_gates.reshape(1, NG), ((0, 0), (0, 128 - NG))))


MESH = pl.DeviceIdType.MESH
N_DEV = 8
N_CHIP = 4


def _comm_call(body, **kw):
    if _INTERPRET:
        kw["interpret"] = pltpu.InterpretParams()
    return pl.pallas_call(body, **kw)


def _allgather8(blk, *, name):
    m_per, n = blk.shape

    def body(x_ref, out_ref, send_sems, recv_sems, local_sem):
        x, y, c = lax.axis_index("x"), lax.axis_index("y"), lax.axis_index("c")
        me, sibling = (x, y, c), (x, y, 1 - c)
        chips = [(1 - x, y), (x, 1 - y), (1 - x, 1 - y)]

        def rows(px, py, pc):
            return out_ref.at[pl.ds((4 * px + 2 * py + pc) * m_per, m_per), :]

        def copy(k, block, to, src=None):
            return pltpu.make_async_remote_copy(
                src_ref=rows(*block) if src is None else src, dst_ref=rows(*block),
                send_sem=send_sems.at[k], recv_sem=recv_sems.at[k], device_id=to, device_id_type=MESH)

        mine = pltpu.make_async_copy(x_ref, rows(*me), local_sem)
        mine.start()
        first = [copy(0, me, sibling, src=x_ref)]
        first += [copy(1 + j, me, (*chip, c), src=x_ref) for j, chip in enumerate(chips)]
        for cp in first:
            cp.start()
        passed = [copy(4 + j, (*chip, c), sibling) for j, chip in enumerate(chips)]
        for j, chip in enumerate(chips):
            copy(1 + j, (*chip, c), me).wait_recv()
            passed[j].start()
        copy(0, sibling, me).wait_recv()
        for j, chip in enumerate(chips):
            copy(4 + j, (*chip, 1 - c), me).wait_recv()
        for cp in first + passed:
            cp.wait_send()
        mine.wait()

    return _comm_call(
        body, name=name, out_shape=jax.ShapeDtypeStruct((N_DEV * m_per, n), blk.dtype),
        in_specs=[pl.BlockSpec(memory_space=pltpu.VMEM)], out_specs=pl.BlockSpec(memory_space=pltpu.VMEM),
        scratch_shapes=[pltpu.SemaphoreType.DMA((7,)), pltpu.SemaphoreType.DMA((7,)), pltpu.SemaphoreType.DMA],
    )(blk)


def _chip_exchange(arrays, *, gather, name):
    n = len(arrays)

    def body(*refs):
        start, wait = _exchange_ops(refs[:n], refs[n:2 * n], *refs[2 * n:], gather=gather)
        start()
        wait()

    io = _exchange_io(arrays, gather)
    return _comm_call(body, name=name, out_shape=io["out_shape"], in_specs=io["specs"], out_specs=io["specs"],
                      scratch_shapes=io["scratch"])(*arrays)


def _exchange_io(arrays, gather):
    n = len(arrays)
    shard = (lambda a: a.shape) if gather else (lambda a: a.shape[1:])
    return dict(
        specs=[pl.BlockSpec(memory_space=pltpu.HBM)] * n,
        out_shape=[jax.ShapeDtypeStruct((N_CHIP, *shard(a)), a.dtype) for a in arrays],
        scratch=[pltpu.SemaphoreType.DMA((3 * n,)), pltpu.SemaphoreType.DMA((3 * n,)), pltpu.SemaphoreType.DMA((n,))])


def _exchange_ops(ins, outs, send_sems, recv_sems, local_sems, *, gather):
    n = len(ins)
    x, y, c = lax.axis_index("x"), lax.axis_index("y"), lax.axis_index("c")
    k = 2 * x + y
    chips = [(1 - x, y), (x, 1 - y), (1 - x, 1 - y)]

    def remote(a, j):
        px, py = chips[j]
        src = ins[a] if gather else ins[a].at[2 * px + py]
        return pltpu.make_async_remote_copy(
            src_ref=src, dst_ref=outs[a].at[k], send_sem=send_sems.at[3 * a + j],
            recv_sem=recv_sems.at[3 * a + j], device_id=(px, py, c), device_id_type=MESH)

    def arrival(a, j):
        px, py = chips[j]
        src = ins[a] if gather else ins[a].at[k]
        return pltpu.make_async_remote_copy(
            src_ref=src, dst_ref=outs[a].at[2 * px + py], send_sem=send_sems.at[3 * a + j],
            recv_sem=recv_sems.at[3 * a + j], device_id=(px, py, c), device_id_type=MESH)

    local = [pltpu.make_async_copy(ins[a] if gather else ins[a].at[k], outs[a].at[k], local_sems.at[a])
             for a in range(n)]
    sent = [remote(a, j) for a in range(n) for j in range(3)]

    def start():
        for cp in local + sent:
            cp.start()

    def wait():
        for a in range(n):
            for j in range(3):
                arrival(a, j).wait_recv()
        for cp in sent:
            cp.wait_send()
        for cp in local:
            cp.wait()

    return start, wait


def _sibling_exchange(arrays, *, name):
    n = len(arrays)

    def body(*refs):
        ins, outs = refs[:n], refs[n:2 * n]
        send_sems, recv_sems = refs[2 * n:]
        x, y, c = lax.axis_index("x"), lax.axis_index("y"), lax.axis_index("c")
        cps = [pltpu.make_async_remote_copy(
            src_ref=ins[a], dst_ref=outs[a], send_sem=send_sems.at[a], recv_sem=recv_sems.at[a],
            device_id=(x, y, 1 - c), device_id_type=MESH) for a in range(n)]
        for cp in cps:
            cp.start()
        for cp in cps:
            cp.wait()

    hbm = pl.BlockSpec(memory_space=pltpu.HBM)
    return _comm_call(
        body, name=name, out_shape=[jax.ShapeDtypeStruct(a.shape, a.dtype) for a in arrays],
        in_specs=[hbm] * n, out_specs=[hbm] * n,
        scratch_shapes=[pltpu.SemaphoreType.DMA((n,)), pltpu.SemaphoreType.DMA((n,))],
    )(*arrays)


def _sum_blocks(a, nblk, *, name):
    n = a.shape[1]

    def body(a_ref, o_ref):
        acc = a_ref[pl.ds(0, 8), :]
        for d in range(1, nblk):
            acc = acc + a_ref[pl.ds(8 * d, 8), :]
        o_ref[...] = acc

    return _call(body, name=name, out_shape=jax.ShapeDtypeStruct((8, n), F32))(a)


def _outer8(sct, dm, *, name, tm=256, tn=1024):
    R, N = sct.shape[0], dm.shape[1]
    tm, tn = min(tm, R), min(tn, N)

    def body(s_ref, d_ref, o_ref):
        s, dmv = s_ref[...], d_ref[...]
        acc = s[:, 0:1] * dmv[0:1, :]
        for b in range(1, 8):
            acc = acc + s[:, b:b + 1] * dmv[b:b + 1, :]
        o_ref[...] = acc

    return _call(
        body, name=name, grid=(R // tm, N // tn),
        in_specs=[pl.BlockSpec((tm, 8), lambda i, j: (i, 0)), pl.BlockSpec((8, tn), lambda i, j: (0, j))],
        out_specs=pl.BlockSpec((tm, tn), lambda i, j: (i, j)),
        out_shape=jax.ShapeDtypeStruct((R, N), F32),
        compiler_params=_cparams(("parallel", "parallel")),
    )(sct, dm)


_BC1 = 1.0 - ADAM_B1 ** ADAM_STEP
_BC2 = 1.0 - ADAM_B2 ** ADAM_STEP


def _adamw(w, g_parts, m, v, *, name, tile=128):
    R, C = w.shape
    tile = min(tile, R)
    assert R % tile == 0
    npart = len(g_parts)

    def body(*refs):
        w_ref, m_ref, v_ref = refs[npart:npart + 3]
        g_o, d_o, m_o, v_o = refs[npart + 3:]
        g = refs[0][...].astype(F32)
        for r in refs[1:npart]:
            g = g + r[...].astype(F32)
        mn = ADAM_B1 * m_ref[...] + (1.0 - ADAM_B1) * g
        vn = ADAM_B2 * v_ref[...] + (1.0 - ADAM_B2) * jnp.square(g)
        g_o[...] = g
        m_o[...] = mn
        v_o[...] = vn
        d_o[...] = -ADAM_LR * ((mn / _BC1) / (jnp.sqrt(vn / _BC2) + ADAM_EPS) + ADAM_WD * w_ref[...])

    spec = pl.BlockSpec((tile, C), lambda i: (i, 0))
    return _call(
        body, name=name, grid=(R // tile,), in_specs=[spec] * (npart + 3), out_specs=[spec] * 4,
        out_shape=[jax.ShapeDtypeStruct((R, C), F32)] * 4,
        compiler_params=_cparams(("parallel",)),
    )(*g_parts, w, m, v)


def _pack(vecs, rows8_cols):
    flat = jnp.concatenate([v.reshape(-1).astype(F32) for v in vecs])
    return jnp.pad(flat, (0, 8 * rows8_cols - flat.shape[0])).reshape(8, rows8_cols)


def _unpack(flat, shapes):
    out, o = [], 0
    for s in shapes:
        n = math.prod(s)
        out.append(flat[o:o + n].reshape(s))
        o += n
    return out


_BIG = ("w_in", "w_uq", "w_ukv", "w_out", "w_ff1", "w_ff2")
_SMALL = ("b_ada", "norm_mix_g", "b_gates", "conv_w", "conv_b", "q_lora_g", "kv_lora_g", "q_norm_g", "k_norm_g",
          "mlstm_norm_g", "norm_mlp_g")
_ORDER = ("w_ada", "b_ada", "norm_mix_g", "w_in", "b_gates", "conv_w", "conv_b", "q_lora_g", "w_uq", "kv_lora_g",
          "w_ukv", "q_norm_g", "k_norm_g", "mlstm_norm_g", "w_out", "norm_mlp_g", "w_ff1", "w_ff2")


def kernel(x, c, positions, w_ada, b_ada, norm_mix_g, w_in, b_gates, conv_w, conv_b, q_lora_g, w_uq, kv_lora_g, w_ukv, q_norm_g, k_norm_g, mlstm_norm_g, w_out, norm_mlp_g, w_ff1, w_ff2, loss_target, m_w_ada, m_b_ada, m_norm_mix_g, m_w_in, m_b_gates, m_conv_w, m_conv_b, m_q_lora_g, m_w_uq, m_kv_lora_g, m_w_ukv, m_q_norm_g, m_k_norm_g, m_mlstm_norm_g, m_w_out, m_norm_mlp_g, m_w_ff1, m_w_ff2, v_w_ada, v_b_ada, v_norm_mix_g, v_w_in, v_b_gates, v_conv_w, v_conv_b, v_q_lora_g, v_w_uq, v_kv_lora_g, v_w_ukv, v_q_norm_g, v_k_norm_g, v_mlstm_norm_g, v_w_out, v_norm_mlp_g, v_w_ff1, v_w_ff2):
    args = dict(locals())
    wts = {n: args[n] for n in _ORDER}
    mom = {n: args["m_" + n] for n in _ORDER}
    var = {n: args["v_" + n] for n in _ORDER}
    MX = _MXU_DTYPE
    xi, yi, ci = lax.axis_index("x"), lax.axis_index("y"), lax.axis_index("c")
    chip = 2 * xi + yi
    dev = 2 * chip + ci
    S = x.shape[1]
    CS = 2 * MW // N_CHIP
    GS = DM // N_CHIP

    pk = _pack([c, conv_w, mlstm_norm_g], 1024)
    allpk = _allgather8(pk, name="gather_small").reshape(N_DEV, 8 * 1024)
    c_all = allpk[:, :D]
    per_chip = allpk[0::2]
    conv_w_full = per_chip[:, D:D + CONVW * CS].reshape(N_CHIP, CONVW, CS).transpose(1, 0, 2).reshape(CONVW, 2 * MW)
    o = D + CONVW * CS
    mn_full = per_chip[:, o:o + HM * GS].reshape(N_CHIP, HM, GS).transpose(1, 0, 2).reshape(HM, DM)

    (sc,) = _rowmap(lambda cv: ([cv * _sigmoid(cv)], []), [(c_all, D, 0)], [], [(D, F32)], tile=8, name="silu_c")
    ncol = w_ada.shape[2]
    b_cols = lax.dynamic_slice(b_ada, (0, chip * ncol), (1, ncol))
    modp = _mm(sc, w_ada[0], "nn", name="ada_fwd", tm=8, tn=1024, tk=512, extras=(jnp.broadcast_to(b_cols, (8, ncol)),),
               epilogue=lambda r, b: (r + b,))
    modg = _allgather8(modp, name="gather_mod").reshape(N_CHIP, 2, 8, ncol)[:, 0]
    mod_all = modg.transpose(1, 0, 2).reshape(N_DEV, N_CHIP * ncol)
    modv = jnp.pad(lax.dynamic_slice(mod_all, (dev, 0), (1, 6 * D)).reshape(6, D), ((0, 2), (0, 0)))

    shards = [wts[n][0].astype(MX) for n in _BIG]
    gw_in, gw_uq, gw_ukv = _chip_exchange(shards[:3], gather=True, name="gather_weights")
    W = _prep_weights(_cols(gw_in), _cols(gw_uq), _cols(gw_ukv), None, None, None, norm_mix_g, norm_mlp_g,
                      q_lora_g, kv_lora_g, q_norm_g, k_norm_g, mn_full, conv_w_full, conv_b, b_gates)

    loss, gx, dmodv, g = _device_step(x[0], loss_target[0], positions[0], modv, W, late=shards[3:])

    small_shapes = [(6 * D,), (D,), (NG,), (CONVW, 2 * MW), (2 * MW,), (Q_LORA,), (KV_LORA,), (QK,), (QK,), (MW,), (D,), (1,)]
    pg = _pack([dmodv, g["norm_mix_g"], g["b_gates"], g["conv_w"], g["conv_b"], g["q_lora_g"], g["kv_lora_g"],
                g["q_norm_g"], g["k_norm_g"], g["mlstm_norm_g"], g["norm_mlp_g"], loss], 4096)
    allpg = _allgather8(pg, name="gather_small_grads")
    tot = _unpack(_sum_blocks(allpg, N_DEV, name="sum_small_grads").reshape(-1), small_shapes)
    dmod_all = allpg.reshape(N_DEV, 8 * 4096)[:, :6 * D]
    gsmall = dict(zip(_SMALL, [tot[0].reshape(1, 6 * D), tot[1].reshape(1, D), tot[2].reshape(1, NG),
                               lax.dynamic_slice(tot[3], (0, chip * CS), (CONVW, CS)).reshape(1, CONVW, CS),
                               tot[4].reshape(1, 2 * MW), tot[5].reshape(1, Q_LORA), tot[6].reshape(1, KV_LORA),
                               tot[7].reshape(1, QK), tot[8].reshape(1, QK),
                               lax.dynamic_slice(tot[9].reshape(HM, DM), (0, chip * GS), (HM, GS)).reshape(1, HM, GS),
                               tot[10].reshape(1, D)]))
    loss_tot = tot[11].reshape(())

    to_send = [_slabs(_unpad_w_in(g["w_in"])).astype(MX), _slabs(_unpad_w_uq(g["w_uq"])).astype(MX),
               _slabs(_unperm_w_ukv(g["w_ukv"])).astype(MX)]
    got = list(_chip_exchange(to_send, gather=False, name="scatter_grads")) + list(g["late"])
    part = []
    for nme, r in zip(_BIG, got):
        wd = r.shape[2]
        (p,) = _rowmap(lambda a0, a1, a2, a3: ([(a0.astype(F32) + a1.astype(F32)) + (a2.astype(F32) + a3.astype(F32))], []),
                       [(r, wd, 0, k) for k in range(N_CHIP)], [], [(wd, F32)], tile=256, name="sum_chips_" + nme)
        part.append(p)
    other = _sibling_exchange(part, name="exchange_cores")

    dm_cols = lax.dynamic_slice(dmod_all, (0, chip * ncol), (N_DEV, ncol))
    g_ada = _outer8(sc.T, dm_cols, name="ada_dw")

    res = {}
    for nme, p, q in zip(_BIG, part, other):
        res[nme] = _adamw(wts[nme][0], [p, q], mom[nme][0], var[nme][0], name="adamw_" + nme)
    res["w_ada"] = _adamw(w_ada[0], [g_ada], m_w_ada[0], v_w_ada[0], name="adamw_w_ada")
    sw = _pack([wts[n] for n in _SMALL], 3072)
    sg = _pack([gsmall[n] for n in _SMALL], 3072)
    sm = _pack([mom[n] for n in _SMALL], 3072)
    sv = _pack([var[n] for n in _SMALL], 3072)
    small_res = _adamw(sw, [sg], sm, sv, name="adamw_small", tile=8)
    shapes = [wts[n].shape for n in _SMALL]
    unp = [_unpack(r.reshape(-1), shapes) for r in small_res]
    for i, nme in enumerate(_SMALL):
        res[nme] = tuple(u[i] for u in unp)
    outs = [loss_tot, gx[None]]
    for kind in range(4):
        outs += [res[n][kind].reshape(wts[n].shape) for n in _ORDER]
    return tuple(outs)
```

```python
import functools
import math

import jax
import jax.numpy as jnp
from jax import lax
from jax.experimental import pallas as pl
from jax.experimental.pallas import tpu as pltpu

F32 = jnp.float32
BF16 = jnp.bfloat16
_MXU_DTYPE = jnp.bfloat16
_INTERPRET = False

D = 2048
H_MLA = 8
NOPE = 128
ROPE = 64
QK = NOPE + ROPE
HP = 256
VD = 128
Q_LORA = 512
KV_LORA = 256
HM = 4
DM = 256
MW = HM * DM
LCH = 128
CONVW = 5
NG = 16
DFF = 4 * D
EPS = 1e-6
M_INIT = -1e30
ROPE_THETA = 10000.0
IN_SIZES = (Q_LORA, KV_LORA, ROPE, MW, MW, MW, MW, NG)
D_IN = sum(IN_SIZES)
P_QM, P_KM, P_VM, P_OM, P_CQ, P_CKV, P_KPE, P_G = 0, 1024, 2048, 3072, 4096, 4608, 4864, 4992
D_INP = 5120

ADAM_LR, ADAM_B1, ADAM_B2, ADAM_EPS, ADAM_WD, ADAM_STEP = 0.001, 0.9, 0.999, 1e-08, 0.01, 10

V7X_VMEM_LIMIT = 56 * 1024 * 1024


def _cparams(sem):
    return pltpu.CompilerParams(dimension_semantics=sem, vmem_limit_bytes=V7X_VMEM_LIMIT)


def _call(body, **kw):
    if _INTERPRET:
        kw.pop("compiler_params", None)
        kw["interpret"] = pltpu.InterpretParams()
    return pl.pallas_call(body, **kw)


def _dot(a, b, form):
    dims = {"nn": ((1,), (0,)), "nt": ((1,), (1,)), "tn": ((0,), (0,))}[form]
    return lax.dot_general(a.astype(_MXU_DTYPE), b.astype(_MXU_DTYPE), (dims, ((), ())),
                           preferred_element_type=F32)


def _mm(a, b, form, *, name, out_dtypes=(F32,), epilogue=None, extras=(), tm=1024, tn=1024, tk=2048):
    if form == "nn":
        (M, K), (K2, N) = a.shape, b.shape
    elif form == "nt":
        (M, K), (N, K2) = a.shape, b.shape
    else:
        (K, M), (K2, N) = a.shape, b.shape
    assert K == K2, (a.shape, b.shape, form)
    tm, tn = min(tm, M), min(tn, N)
    tk = max(d for d in range(128, min(tk, K) + 1, 128) if K % d == 0) if K > 128 else K
    assert M % tm == 0 and N % tn == 0 and K % tk == 0, (M, N, K, tm, tn, tk)
    nk = K // tk
    ne, no = len(extras), len(out_dtypes)
    if form == "tn":
        a_spec = pl.BlockSpec((tk, tm), lambda i, j, k: (k, i))
    else:
        a_spec = pl.BlockSpec((tm, tk), lambda i, j, k: (i, k))
    if form == "nt":
        b_spec = pl.BlockSpec((tn, tk), lambda i, j, k: (j, k))
    else:
        b_spec = pl.BlockSpec((tk, tn), lambda i, j, k: (k, j))
    mn_spec = pl.BlockSpec((tm, tn), lambda i, j, k: (i, j))

    def body(a_ref, b_ref, *rest):
        ex, outs, acc = rest[:ne], rest[ne:ne + no], rest[ne + no]
        k = pl.program_id(2)
        prod = _dot(a_ref[...], b_ref[...], form)

        @pl.when(k == 0)
        def _():
            acc[...] = prod

        @pl.when(k > 0)
        def _():
            acc[...] += prod

        @pl.when(k == nk - 1)
        def _():
            r = acc[...]
            vals = (r,) if epilogue is None else epilogue(r, *[e[...] for e in ex])
            for o, v in zip(outs, vals):
                o[...] = v.astype(o.dtype)

    res = _call(
        body, name=name, grid=(M // tm, N // tn, nk),
        in_specs=[a_spec, b_spec] + [mn_spec] * ne,
        out_specs=[mn_spec] * no,
        out_shape=[jax.ShapeDtypeStruct((M, N), dt) for dt in out_dtypes],
        scratch_shapes=[pltpu.VMEM((tm, tn), F32)],
        compiler_params=_cparams(("parallel", "parallel", "arbitrary")),
    )(a, b, *extras)
    return res[0] if no == 1 else res


def _rowmap(fn, rows, bcasts, outs, accs=(), *, tile, name):
    rows = [r if len(r) == 4 else (*r, None) for r in rows]
    S = rows[0][0].shape[-2]
    tile = min(tile, S)
    assert S % tile == 0
    nr, nb, no, na = len(rows), len(bcasts), len(outs), len(accs)

    def body(*refs):
        vals = [r[...] for r in refs[:nr + nb]]
        o_refs, a_refs = refs[nr + nb:nr + nb + no], refs[nr + nb + no:]
        o_vals, a_vals = fn(*vals)
        for r, v in zip(o_refs, o_vals):
            r[...] = v.astype(r.dtype)
        if na:
            @pl.when(pl.program_id(0) == 0)
            def _():
                for r in a_refs:
                    r[...] = jnp.zeros(r.shape, r.dtype)
            for r, v in zip(a_refs, a_vals):
                r[...] += v

    in_specs = []
    for (arr, w, cb, lead) in rows:
        if lead is None:
            in_specs.append(pl.BlockSpec((tile, w), lambda i, cb=cb: (i, cb)))
        else:
            in_specs.append(pl.BlockSpec((None, tile, w), lambda i, cb=cb, lead=lead: (lead, i, cb)))
    in_specs += [pl.BlockSpec(b.shape, lambda i: (0, 0)) for b in bcasts]
    out_specs = [pl.BlockSpec((tile, w), lambda i: (i, 0)) for (w, _) in outs]
    out_specs += [pl.BlockSpec(s, lambda i: (0, 0)) for s in accs]
    out_shape = [jax.ShapeDtypeStruct((S, w), dt) for (w, dt) in outs]
    out_shape += [jax.ShapeDtypeStruct(s, F32) for s in accs]
    return _call(
        body, name=name, grid=(S // tile,), in_specs=in_specs, out_specs=out_specs, out_shape=out_shape,
        compiler_params=_cparams(("arbitrary",)),
    )(*[r[0] for r in rows], *bcasts)


def _colsum(v):
    return jnp.sum(v, axis=0, keepdims=True)


def _rms(x, n):
    r = lax.rsqrt(jnp.sum(x * x, axis=-1, keepdims=True) * (1.0 / n) + EPS)
    return x * r, r


def _rms_bwd(dxhat, xhat, r, n):
    return r * (dxhat - xhat * (jnp.sum(dxhat * xhat, axis=-1, keepdims=True) * (1.0 / n)))


def _rope_fwd(r, cosp, s1, s2):
    return r * cosp + pltpu.roll(r, 32, 1) * s1 + pltpu.roll(r, 96, 1) * s2


def _rope_bwd(d, cosp, s1, s2):
    return d * cosp + pltpu.roll(d * s1, 96, 1) + pltpu.roll(d * s2, 32, 1)


def _sigmoid(x):
    return 1.0 / (1.0 + jnp.exp(-x))


def _halo_specs(tile, halo, width, cb, S, lead=None):
    nh = tile // halo
    last = S // halo - 1
    if lead is None:
        return [
            pl.BlockSpec((tile, width), lambda i: (i, cb)),
            pl.BlockSpec((halo, width), lambda i: (jnp.maximum(i * nh - 1, 0), cb)),
            pl.BlockSpec((halo, width), lambda i: (jnp.minimum((i + 1) * nh, last), cb)),
        ]
    return [
        pl.BlockSpec((None, tile, width), lambda i: (lead, i, cb)),
        pl.BlockSpec((None, halo, width), lambda i: (lead, jnp.maximum(i * nh - 1, 0), cb)),
        pl.BlockSpec((None, halo, width), lambda i: (lead, jnp.minimum((i + 1) * nh, last), cb)),
    ]


def _conv_fwd(proj, conv_w8, conv_b, *, tile=256):
    S = proj.shape[0]
    T = min(tile, S)
    n = S // T
    W = 2 * MW

    def body(x_ref, xp_ref, xn_ref, w_ref, b_ref, q_ref, k_ref, ext):
        i = pl.program_id(0)
        ext[pl.ds(0, 8), :] = xp_ref[...] * (i > 0).astype(F32)
        ext[pl.ds(8, T), :] = x_ref[...]
        ext[pl.ds(8 + T, 8), :] = xn_ref[...] * (i < n - 1).astype(F32)
        w = w_ref[...]
        y = b_ref[...] + w[0:1, :] * ext[pl.ds(6, T), :]
        for o in range(1, CONVW):
            y = y + w[o:o + 1, :] * ext[pl.ds(6 + o, T), :]
        y = y * _sigmoid(y)
        q_ref[...] = y[:, :MW].astype(q_ref.dtype)
        k_ref[...] = (y[:, MW:] * (DM ** -0.5)).astype(k_ref.dtype)

    return _call(
        body, name="conv_fwd", grid=(n,),
        in_specs=_halo_specs(T, 8, W, 0, S) + [pl.BlockSpec((8, W), lambda i: (0, 0)),
                                                 pl.BlockSpec((1, W), lambda i: (0, 0))],
        out_specs=[pl.BlockSpec((T, MW), lambda i: (i, 0))] * 2,
        out_shape=[jax.ShapeDtypeStruct((S, MW), _MXU_DTYPE)] * 2,
        scratch_shapes=[pltpu.VMEM((T + 16, W), F32)],
        compiler_params=_cparams(("arbitrary",)),
    )(proj, proj, proj, conv_w8, conv_b)


def _conv_bwd(proj, dqd, dkd, conv_w8, conv_b, *, tile=256):
    S = proj.shape[0]
    T = min(tile, S)
    n = S // T
    W = 2 * MW

    def body(x_ref, xp_ref, xn_ref, *rest):
        g = rest[:12]
        w_ref, b_ref, dx_ref, dw_ref, db_ref, ext, edp = rest[12:]
        i = pl.program_id(0)
        mp = (i > 0).astype(F32)
        mn = (i < n - 1).astype(F32)
        ext[pl.ds(0, 16), :] = xp_ref[...] * mp
        ext[pl.ds(16, T), :] = x_ref[...]
        ext[pl.ds(16 + T, 16), :] = xn_ref[...] * mn
        w = w_ref[...]
        pre = b_ref[...] + w[0:1, :] * ext[pl.ds(6, T + 16), :]
        for o in range(1, CONVW):
            pre = pre + w[o:o + 1, :] * ext[pl.ds(6 + o, T + 16), :]
        sg = _sigmoid(pre)
        dsilu = sg * (1.0 + pre * (1.0 - sg))
        for half, (a0, a1) in enumerate(((g[0:3], g[3:6]), (g[6:9], g[9:12]))):
            sc = 1.0 if half == 0 else DM ** -0.5
            cols = pl.ds(half * MW, MW)
            edp[pl.ds(0, 8), cols] = (a0[1][...] + a1[1][...]) * (mp * sc)
            edp[pl.ds(8, T), cols] = (a0[0][...] + a1[0][...]) * sc
            edp[pl.ds(8 + T, 8), cols] = (a0[2][...] + a1[2][...]) * (mn * sc)
        edp[...] = edp[...] * dsilu
        dpm = edp[pl.ds(8, T), :]
        dx = w[0:1, :] * edp[pl.ds(10, T), :]
        for o in range(1, CONVW):
            dx = dx + w[o:o + 1, :] * edp[pl.ds(10 - o, T), :]
        dx_ref[...] = dx.astype(dx_ref.dtype)

        @pl.when(i == 0)
        def _():
            dw_ref[...] = jnp.zeros(dw_ref.shape, F32)
            db_ref[...] = jnp.zeros(db_ref.shape, F32)

        for o in range(CONVW):
            dw_ref[pl.ds(o, 1), :] += _colsum(ext[pl.ds(14 + o, T), :] * dpm)
        db_ref[...] += _colsum(dpm)

    gspecs = []
    for arr in (dqd, dkd):
        for d in (0, 1):
            gspecs += _halo_specs(T, 8, MW, 0, S, lead=d)
    return _call(
        body, name="conv_bwd", grid=(n,),
        in_specs=_halo_specs(T, 16, W, 0, S) + gspecs + [pl.BlockSpec((8, W), lambda i: (0, 0)),
                                                          pl.BlockSpec((1, W), lambda i: (0, 0))],
        out_specs=[pl.BlockSpec((T, W), lambda i: (i, 0)), pl.BlockSpec((8, W), lambda i: (0, 0)),
                   pl.BlockSpec((1, W), lambda i: (0, 0))],
        out_shape=[jax.ShapeDtypeStruct((S, W), _MXU_DTYPE), jax.ShapeDtypeStruct((8, W), F32),
                   jax.ShapeDtypeStruct((1, W), F32)],
        scratch_shapes=[pltpu.VMEM((T + 32, W), F32), pltpu.VMEM((T + 16, W), F32)],
        compiler_params=_cparams(("arbitrary",)),
    )(proj, proj, proj, *([dqd] * 6), *([dkd] * 6), conv_w8, conv_b)


_ATT_SCALE = QK ** -0.5
_LOG2E = math.log2(math.e)
_Q_PRESCALE = _ATT_SCALE * _LOG2E
_ATT_SPLIT = 2


def _side_exchange(side, gather, grid):
    ns = len(side)
    io = _exchange_io(side, gather) if ns else dict(specs=[], out_shape=[], scratch=[])

    def wrap(refs_in, refs_out, sems):
        if not ns:
            return (lambda: None), (lambda: None)
        start, wait = _exchange_ops(refs_in, refs_out, *sems, gather=gather)
        ids = [pl.program_id(a) for a in range(len(grid))]
        first = functools.reduce(jnp.logical_and, [i == 0 for i in ids])
        last = functools.reduce(jnp.logical_and, [i == g - 1 for i, g in zip(ids, grid)])
        return (lambda: pl.when(first)(start)), (lambda: pl.when(last)(wait))

    return ns, io, wrap


def _attn_fwd(q, k, v, *, side=(), tq=1024, tk=8192, split=4, unroll=1):
    S = q.shape[0]
    tq, tk = min(tq, S), min(tk, S)
    nkv = S // tk
    hq = tq // split
    grid = (H_MLA, S // tq)
    ns, io, wrap = _side_exchange(side, True, grid)

    def body(q_ref, k_ref, v_ref, *rest):
        o_ref, lse_ref = rest[ns:ns + 2]
        m_s, l_s, acc_s = rest[2 * ns + 2:2 * ns + 5]
        side_start, side_wait = wrap(rest[:ns], rest[ns + 2:2 * ns + 2], rest[2 * ns + 5:])
        side_start()
        m_s[...] = jnp.full(m_s.shape, -1e30, F32)
        l_s[...] = jnp.zeros(l_s.shape, F32)
        acc_s[...] = jnp.zeros(acc_s.shape, F32)

        def step(j, carry):
            rows = pl.ds(pl.multiple_of(j * tk, tk), tk)
            kj, vj = k_ref[rows, :], v_ref[rows, :]
            for a in range(split):
                r = pl.ds(a * hq, hq)
                s = _dot(q_ref[r, :], kj, "nt")
                m_old = m_s[r, :]
                m_new = jnp.maximum(m_old, jnp.max(s, axis=1, keepdims=True))
                p = jnp.exp2(s - m_new)
                alpha = jnp.exp2(m_old - m_new)
                l_s[r, :] = alpha * l_s[r, :] + jnp.sum(p, axis=1, keepdims=True)
                acc_s[r, :] = alpha * acc_s[r, :] + _dot(p, vj, "nn")
                m_s[r, :] = m_new
            return carry

        lax.fori_loop(0, nkv, step, 0, unroll=unroll if nkv % unroll == 0 else 1)
        o_ref[...] = (acc_s[...] / l_s[...]).astype(o_ref.dtype)
        lse_ref[...] = m_s[...] + jnp.log2(l_s[...])
        side_wait()

    res = _call(
        body, name="attn_fwd", grid=grid,
        in_specs=[pl.BlockSpec((tq, HP), lambda h, i: (i, h)),
                  pl.BlockSpec((S, HP), lambda h, i: (0, h)),
                  pl.BlockSpec((S, VD), lambda h, i: (0, h))] + io["specs"],
        out_specs=[pl.BlockSpec((tq, VD), lambda h, i: (i, h)),
                   pl.BlockSpec((None, tq, 1), lambda h, i: (h, i, 0))] + io["specs"],
        out_shape=[jax.ShapeDtypeStruct((S, H_MLA * VD), _MXU_DTYPE),
                   jax.ShapeDtypeStruct((H_MLA, S, 1), F32)] + io["out_shape"],
        scratch_shapes=[pltpu.VMEM((tq, 1), F32), pltpu.VMEM((tq, 1), F32), pltpu.VMEM((tq, VD), F32)]
        + io["scratch"],
        compiler_params=_cparams(("arbitrary", "arbitrary")),
    )(q, k, v, *side)
    return res[0], res[1], list(res[2:])


def _attn_bwd_dkv(q, k, v, dcat, lse_hs, dl_hs, *, side=(), tq=4096, tk=512, split=2, unroll=1):
    S = q.shape[0]
    tq, tk = min(tq, S), min(tk, S)
    nq = S // tq
    lse_r, dl_r = lse_hs.reshape(H_MLA, nq, 1, tq), dl_hs.reshape(H_MLA, nq, 1, tq)
    hk = tk // split
    grid = (H_MLA, S // tk)
    ns, io, wrap = _side_exchange(side, False, grid)

    def body(q_ref, k_ref, v_ref, do_ref, lse_ref, dl_ref, *rest):
        dk_ref, dv_ref = rest[ns:ns + 2]
        side_start, side_wait = wrap(rest[:ns], rest[ns + 2:2 * ns + 2], rest[2 * ns + 2:])
        side_start()
        dk_ref[...] = jnp.zeros(dk_ref.shape, F32)
        dv_ref[...] = jnp.zeros(dv_ref.shape, F32)

        def step(i, carry):
            rows = pl.ds(pl.multiple_of(i * tq, tq), tq)
            qi, doi = q_ref[rows, :], do_ref[rows, :].astype(_MXU_DTYPE)
            lse, dl = lse_ref[i], dl_ref[i]
            for a in range(split):
                r = pl.ds(a * hk, hk)
                pt = jnp.exp2(_dot(k_ref[r, :], qi, "nt") - lse)
                dv_ref[r, :] += _dot(pt, doi, "nn")
                dst = pt * (_dot(v_ref[r, :], doi, "nt") - dl)
                dk_ref[r, :] += _dot(dst, qi, "nn")
            return carry

        lax.fori_loop(0, nq, step, 0, unroll=unroll if nq % unroll == 0 else 1)
        dk_ref[...] = dk_ref[...] * (1.0 / _LOG2E)
        side_wait()

    res = _call(
        body, name="attn_bwd_dkv", grid=grid,
        in_specs=[pl.BlockSpec((S, HP), lambda h, j: (0, h)),
                  pl.BlockSpec((tk, HP), lambda h, j: (j, h)),
                  pl.BlockSpec((tk, VD), lambda h, j: (j, h)),
                  pl.BlockSpec((S, VD), lambda h, j: (0, h)),
                  pl.BlockSpec((None, nq, 1, tq), lambda h, j: (h, 0, 0, 0)),
                  pl.BlockSpec((None, nq, 1, tq), lambda h, j: (h, 0, 0, 0))] + io["specs"],
        out_specs=[pl.BlockSpec((tk, HP), lambda h, j: (j, h)),
                   pl.BlockSpec((tk, VD), lambda h, j: (j, h))] + io["specs"],
        out_shape=[jax.ShapeDtypeStruct((S, H_MLA * HP), F32), jax.ShapeDtypeStruct((S, H_MLA * VD), F32)]
        + io["out_shape"],
        scratch_shapes=io["scratch"],
        compiler_params=_cparams(("arbitrary", "arbitrary")),
    )(q, k, v, dcat, lse_r, dl_r, *side)
    return res[0], res[1], list(res[2:])


def _attn_bwd_dq(q, k, v, dcat, lse_c, dl_c, *, tq=512, tk=4096, split=2, unroll=1):
    S = q.shape[0]
    tq, tk = min(tq, S), min(tk, S)
    nkv = S // tk
    hq = tq // split

    def body(q_ref, k_ref, v_ref, do_ref, lse_ref, dl_ref, dq_ref, do_s):
        dq_ref[...] = jnp.zeros(dq_ref.shape, F32)
        do_s[...] = do_ref[...].astype(do_s.dtype)

        def step(j, carry):
            rows = pl.ds(pl.multiple_of(j * tk, tk), tk)
            kj, vj = k_ref[rows, :], v_ref[rows, :]
            for a in range(split):
                r = pl.ds(a * hq, hq)
                p = jnp.exp2(_dot(q_ref[r, :], kj, "nt") - lse_ref[r, :])
                ds = p * (_dot(do_s[r, :], vj, "nt") - dl_ref[r, :])
                dq_ref[r, :] += _dot(ds, kj, "nn")
            return carry

        lax.fori_loop(0, nkv, step, 0, unroll=unroll if nkv % unroll == 0 else 1)
        dq_ref[...] = dq_ref[...] * _ATT_SCALE

    return _call(
        body, name="attn_bwd_dq", grid=(H_MLA, S // tq),
        in_specs=[pl.BlockSpec((tq, HP), lambda h, i: (i, h)),
                  pl.BlockSpec((S, HP), lambda h, i: (0, h)),
                  pl.BlockSpec((S, VD), lambda h, i: (0, h)),
                  pl.BlockSpec((tq, VD), lambda h, i: (i, h)),
                  pl.BlockSpec((None, tq, 1), lambda h, i: (h, i, 0)),
                  pl.BlockSpec((None, tq, 1), lambda h, i: (h, i, 0))],
        out_specs=pl.BlockSpec((tq, HP), lambda h, i: (i, h)),
        out_shape=jax.ShapeDtypeStruct((S, H_MLA * HP), F32),
        scratch_shapes=[pltpu.VMEM((tq, VD), _MXU_DTYPE)],
        compiler_params=_cparams(("parallel", "arbitrary")),
    )(q, k, v, dcat, lse_c, dl_c)


def _mlstm_chunk_terms(g, q, k, v, gates, gates_t, bg_row, C, n_row, m):
    L = LCH
    d = g // HM
    h = g % HM
    i_idx = d * 8 + h
    f_idx = d * 8 + 4 + h
    rr = lax.broadcasted_iota(jnp.int32, (L, L), 0)
    cc = lax.broadcasted_iota(jnp.int32, (L, L), 1)
    order = (rr - cc) * (1 - 2 * d)
    tri = order >= 0
    eye = rr == cc
    lane = lax.broadcasted_iota(jnp.int32, gates.shape, 1)
    sub = lax.broadcasted_iota(jnp.int32, gates_t.shape, 0)
    lane_b = lax.broadcasted_iota(jnp.int32, bg_row.shape, 1)
    pick_c = lambda idx: jnp.sum(jnp.where(lane == idx, gates, 0.0), axis=1, keepdims=True)
    pick_r = lambda idx: jnp.sum(jnp.where(sub == idx, gates_t, 0.0), axis=0, keepdims=True)
    pick_b = lambda idx: jnp.sum(jnp.where(lane_b == idx, bg_row, 0.0), axis=1, keepdims=True)
    i_col, i_row = pick_c(i_idx) + pick_b(i_idx), pick_r(i_idx) + pick_b(i_idx)
    f_col, f_row = pick_c(f_idx) + pick_b(f_idx), pick_r(f_idx) + pick_b(f_idx)
    logsig = lambda x: jnp.minimum(x, 0.0) - jnp.log(1.0 + jnp.exp(-jnp.abs(x)))
    lf_col, lf_row = logsig(f_col), logsig(f_row)
    b_col = jnp.sum(jnp.where(tri, lf_row, 0.0), axis=1, keepdims=True)
    tri_t = order <= 0
    b_row = jnp.sum(jnp.where(tri_t, lf_col, 0.0), axis=0, keepdims=True)
    bL = jnp.sum(lf_row, axis=1, keepdims=True)
    log_inter = b_col + m
    logD = jnp.where(tri, b_col - b_row + i_row, -jnp.inf)
    m_t = jnp.maximum(log_inter, jnp.max(logD, axis=1, keepdims=True))
    Dm = jnp.exp(logD - m_t)
    w_inter = jnp.exp(log_inter - m_t)
    A = _dot(q, k, "nt")
    Sc = A * Dm
    numI = _dot(q, C, "nt")
    qf = q.astype(F32)
    kf = k.astype(F32)
    denI = jnp.sum(qf * n_row, axis=1, keepdims=True)
    num = _dot(Sc, v, "nn") + w_inter * numI
    den = jnp.sum(Sc, axis=1, keepdims=True) + w_inter * denI
    floor = jnp.exp(-m_t)
    Nst = jnp.maximum(jnp.abs(den), floor)
    log_w = bL - b_col + i_col
    m_new = jnp.maximum(bL + m, jnp.max(log_w, axis=0, keepdims=True))
    decay = jnp.exp(bL + m - m_new)
    w_col = jnp.exp(log_w - m_new)
    return dict(tri=tri, eye=eye, f_row=f_row, Dm=Dm, w_inter=w_inter, A=A, Sc=Sc, numI=numI, denI=denI,
                num=num, den=den, floor=floor, Nst=Nst, m_new=m_new, decay=decay, w_col=w_col, qf=qf, kf=kf)


def _mlstm_specs(S, nc):
    jeff = lambda g, j: j + (g // HM) * (nc - 1 - 2 * j)
    return jeff, [
        pl.BlockSpec((LCH, DM), lambda g, j: (jeff(g, j), g % HM)),
        pl.BlockSpec((LCH, DM), lambda g, j: (jeff(g, j), g % HM)),
        pl.BlockSpec((LCH, DM), lambda g, j: (jeff(g, j), P_VM // DM + g % HM)),
        pl.BlockSpec((LCH, 128), lambda g, j: (jeff(g, j), P_G // 128)),
        pl.BlockSpec((NG, LCH), lambda g, j: (0, jeff(g, j))),
        pl.BlockSpec((1, 128), lambda g, j: (0, 0)),
    ]


def _mlstm_fwd(qc, kc, proj, gates_t, bg_row):
    S = qc.shape[0]
    nc = S // LCH
    jeff, specs = _mlstm_specs(S, nc)

    def body(q_ref, k_ref, v_ref, g_ref, gt_ref, bg_ref, h_ref, cst_ref, nm_ref, C_s, n_s, m_s):
        g = pl.program_id(0)

        @pl.when(pl.program_id(1) == 0)
        def _():
            C_s[...] = jnp.zeros(C_s.shape, F32)
            n_s[...] = jnp.zeros(n_s.shape, F32)
            m_s[...] = jnp.full(m_s.shape, M_INIT, F32)

        C, n_row, m = C_s[...], n_s[0:1, :], m_s[0:1, 0:1]
        cst_ref[...] = C
        nm_ref[0:1, :] = n_row
        nm_ref[1:2, :] = jnp.broadcast_to(m, (1, DM))
        nm_ref[2:8, :] = jnp.zeros((6, DM), F32)
        q, k, v = q_ref[...], k_ref[...], v_ref[...]
        t = _mlstm_chunk_terms(g, q, k, v, g_ref[...], gt_ref[...], bg_ref[...], C, n_row, m)
        h_ref[...] = t["num"] / t["Nst"]
        wv = t["w_col"] * v
        C_s[...] = t["decay"] * C + _dot(wv, k, "tn")
        n_s[0:1, :] = t["decay"] * n_row + _colsum(t["w_col"] * t["kf"])
        m_s[...] = jnp.broadcast_to(t["m_new"], m_s.shape)

    return _call(
        body, name="mlstm_fwd", grid=(2 * HM, nc), in_specs=specs,
        out_specs=[pl.BlockSpec((None, LCH, DM), lambda g, j: (g // HM, jeff(g, j), g % HM)),
                   pl.BlockSpec((None, None, DM, DM), lambda g, j: (g, jeff(g, j), 0, 0)),
                   pl.BlockSpec((None, None, 8, DM), lambda g, j: (g, jeff(g, j), 0, 0))],
        out_shape=[jax.ShapeDtypeStruct((2, S, MW), F32),
                   jax.ShapeDtypeStruct((2 * HM, nc, DM, DM), F32),
                   jax.ShapeDtypeStruct((2 * HM, nc, 8, DM), F32)],
        scratch_shapes=[pltpu.VMEM((DM, DM), F32), pltpu.VMEM((8, DM), F32), pltpu.VMEM((8, 128), F32)],
        compiler_params=_cparams(("parallel", "arbitrary")),
    )(qc, kc, proj, proj, gates_t, bg_row)


def _mlstm_bwd(qc, kc, proj, gates_t, bg_row, dh, cst, nm):
    S = qc.shape[0]
    nc = S // LCH
    _, specs = _mlstm_specs(S, nc)
    jb = lambda g, j: (nc - 1 - j) + (g // HM) * (2 * j - (nc - 1))
    respec = lambda bs: pl.BlockSpec(bs.block_shape, (lambda g, j, im=bs.index_map: im(g, nc - 1 - j)))
    specs = [respec(s) for s in specs]

    def body(q_ref, k_ref, v_ref, g_ref, gt_ref, bg_ref, dh_ref, cst_ref, nm_ref,
             dq_ref, dk_ref, dv_ref, dg_ref, dC_s, dn_s):
        g = pl.program_id(0)

        @pl.when(pl.program_id(1) == 0)
        def _():
            dC_s[...] = jnp.zeros(dC_s.shape, F32)
            dn_s[...] = jnp.zeros(dn_s.shape, F32)

        C, n_row, m = cst_ref[...], nm_ref[0:1, :], nm_ref[1:2, 0:1]
        q, k, v = q_ref[...], k_ref[...], v_ref[...]
        t = _mlstm_chunk_terms(g, q, k, v, g_ref[...], gt_ref[...], bg_ref[...], C, n_row, m)
        tri, eye, qf, kf = t["tri"], t["eye"], t["qf"], t["kf"]
        w_inter, w_col, decay, Nst = t["w_inter"], t["w_col"], t["decay"], t["Nst"]
        dC, dn = dC_s[...], dn_s[0:1, :]
        dhv = dh_ref[...]
        hval = t["num"] / Nst
        dnum = dhv / Nst
        dNst = -jnp.sum(dhv * hval, axis=1, keepdims=True) / Nst
        dden = jnp.where(jnp.abs(t["den"]) > t["floor"], jnp.sign(t["den"]) * dNst, 0.0)
        dSc = _dot(dnum, v, "nt") + dden
        dA = dSc * t["Dm"]
        G = dSc * t["Sc"]
        KdC = _dot(k, dC, "nt")
        dq = _dot(dA, k, "nn") + w_inter * _dot(dnum, C, "nn") + (w_inter * dden) * n_row
        dk = _dot(dA, q, "tn") + w_col * _dot(v, dC, "nn") + w_col * dn
        dv = _dot(t["Sc"], dnum, "tn") + w_col * KdC
        dq_ref[...] = dq
        dk_ref[...] = dk
        dv_ref[...] = dv
        dlog_inter = w_inter * (jnp.sum(dnum * t["numI"], axis=1, keepdims=True) + dden * t["denI"])
        rowG = jnp.sum(G, axis=1, keepdims=True)
        colG = jnp.sum(G, axis=0, keepdims=True)
        u_col = w_col * (jnp.sum(v * KdC, axis=1, keepdims=True) + jnp.sum(kf * dn, axis=1, keepdims=True))
        colG_c = jnp.sum(jnp.where(eye, colG, 0.0), axis=1, keepdims=True)
        u_row = jnp.sum(jnp.where(eye, u_col, 0.0), axis=0, keepdims=True)
        db_col = rowG + dlog_inter - u_col - colG_c
        dbL = jnp.sum(u_col, axis=0, keepdims=True) + decay * (
            jnp.sum(jnp.sum(dC * C, axis=1, keepdims=True), axis=0, keepdims=True)
            + jnp.sum(dn * n_row, axis=1, keepdims=True))
        dlf_row = jnp.sum(jnp.where(tri, db_col, 0.0), axis=0, keepdims=True) + dbL
        di_row = colG + u_row
        df_row = dlf_row * (1.0 - _sigmoid(t["f_row"]))
        dg_ref[...] = jnp.zeros(dg_ref.shape, F32)
        dg_ref[0:1, :] = di_row
        dg_ref[1:2, :] = df_row
        dC_s[...] = decay * dC + _dot(w_inter * dnum, q, "tn")
        dn_s[0:1, :] = decay * dn + _colsum((w_inter * dden) * qf)

    return _call(
        body, name="mlstm_bwd", grid=(2 * HM, nc),
        in_specs=specs + [pl.BlockSpec((LCH, DM), lambda g, j: (jb(g, j), g % HM)),
                          pl.BlockSpec((None, None, DM, DM), lambda g, j: (g, jb(g, j), 0, 0)),
                          pl.BlockSpec((None, None, 8, DM), lambda g, j: (g, jb(g, j), 0, 0))],
        out_specs=[pl.BlockSpec((None, LCH, DM), lambda g, j: (g // HM, jb(g, j), g % HM))] * 3
        + [pl.BlockSpec((None, None, 8, LCH), lambda g, j: (g, jb(g, j), 0, 0))],
        out_shape=[jax.ShapeDtypeStruct((2, S, MW), F32)] * 3
        + [jax.ShapeDtypeStruct((2 * HM, nc, 8, LCH), F32)],
        scratch_shapes=[pltpu.VMEM((DM, DM), F32), pltpu.VMEM((8, DM), F32)],
        compiler_params=_cparams(("parallel", "arbitrary")),
    )(qc, kc, proj, proj, gates_t, bg_row, dh, cst, nm)


def _pad_w_in(w):
    cq, ckv, kpe, qm, km, vm, om, gt = _split_in(w)
    z = lambda n: jnp.zeros((w.shape[0], n), w.dtype)
    return jnp.concatenate([qm, km, vm, om, cq, ckv, kpe, z(HP - QK), gt, z(128 - NG)], axis=1)


def _split_in(w):
    out, o = [], 0
    for n in IN_SIZES:
        out.append(w[:, o:o + n])
        o += n
    return out


def _unpad_w_in(g):
    return jnp.concatenate([g[:, P_CQ:P_CQ + Q_LORA], g[:, P_CKV:P_CKV + KV_LORA], g[:, P_KPE:P_KPE + ROPE],
                            g[:, 0:4 * MW], g[:, P_G:P_G + NG]], axis=1)


def _pad_w_uq(w):
    return jnp.pad(w.reshape(Q_LORA, H_MLA, QK), ((0, 0), (0, 0), (0, HP - QK))).reshape(Q_LORA, H_MLA * HP)


def _unpad_w_uq(g):
    return g.reshape(Q_LORA, H_MLA, HP)[:, :, :QK].reshape(Q_LORA, H_MLA * QK)


def _perm_w_ukv(w):
    return w.reshape(KV_LORA, H_MLA, 2, NOPE).transpose(0, 2, 1, 3).reshape(KV_LORA, 2 * H_MLA * NOPE)


def _unperm_w_ukv(g):
    return g.reshape(KV_LORA, 2, H_MLA, NOPE).transpose(0, 2, 1, 3).reshape(KV_LORA, 2 * H_MLA * NOPE)


def _rope_tables(positions):
    half = ROPE // 2
    freqs = ROPE_THETA ** (-jnp.arange(half, dtype=F32) / half)
    ang = positions.astype(F32)[:, None] * freqs
    cos, sin = jnp.cos(ang), jnp.sin(ang)
    z32, z64 = jnp.zeros_like(cos), jnp.zeros((cos.shape[0], 64), F32)
    return (jnp.concatenate([cos, cos, z64], axis=1), jnp.concatenate([z32, sin, z64], axis=1),
            jnp.concatenate([-sin, z32, z64], axis=1))


def _device_step(x, tgt, positions, modv, W, late=None):
    S = x.shape[0]
    MX = _MXU_DTYPE
    cosp, rs1, rs2 = _rope_tables(positions)
    tabs = [(cosp, 128, 0), (rs1, 128, 0), (rs2, 128, 0)]
    cat1 = lambda vs: jnp.concatenate(vs, axis=1)
    hsl = lambda hh, w: slice(hh * w, (hh + 1) * w)

    def ln1(xv, g, mv):
        xhat, _ = _rms(xv, D)
        return [xhat * g * (1.0 + mv[1:2]) + mv[0:1]], []

    (h,) = _rowmap(ln1, [(x, D, 0)], [W["g_mix"], modv], [(D, MX)], tile=128, name="ln1")
    proj = _mm(h, W["w_in"], "nn", name="proj")

    def lora(cq, ckv, gq, gkv):
        return [_rms(cq, Q_LORA)[0] * gq, _rms(ckv, KV_LORA)[0] * gkv], []

    cqn, ckvn = _rowmap(lora, [(proj, Q_LORA, P_CQ // Q_LORA), (proj, KV_LORA, P_CKV // KV_LORA)],
                        [W["g_qlora"], W["g_kvlora"]], [(Q_LORA, MX), (KV_LORA, MX)], tile=256, name="lora_norm")
    q_raw = _mm(cqn, W["w_uq"], "nn", name="q_up")
    kv_raw = _mm(ckvn, W["w_ukv"], "nn", name="kv_up")

    def mla_q(qr, cp, a1, a2, gq):
        outs = []
        for hh in range(H_MLA):
            y = _rms(qr[:, hsl(hh, HP)], QK)[0] * gq
            outs += [y[:, :NOPE], _rope_fwd(y[:, NOPE:], cp, a1, a2)]
        return [cat1(outs) * _Q_PRESCALE], []

    (qh,) = _rowmap(mla_q, [(q_raw, H_MLA * HP, 0)] + tabs, [W["gq"]], [(H_MLA * HP, MX)], tile=128, name="mla_q")

    def mla_k(kvr, kpe, cp, a1, a2, gk):
        outs = []
        for hh in range(H_MLA):
            y = _rms(cat1([kvr[:, hsl(hh, NOPE)], kpe]), QK)[0] * gk
            outs += [y[:, :NOPE], _rope_fwd(y[:, NOPE:], cp, a1, a2)]
        return [cat1(outs), kvr[:, H_MLA * NOPE:]], []

    kh, vh = _rowmap(mla_k, [(kv_raw, 2 * H_MLA * NOPE, 0), (proj, 128, P_KPE // 128)] + tabs, [W["gk"]],
                     [(H_MLA * HP, MX), (H_MLA * VD, MX)], tile=128, name="mla_k")
    attn_o, lse, gathered = _attn_fwd(qh, kh, vh, side=late or ())
    if late:
        W = dict(W, w_out=gathered[0].reshape(D, D), w_ff1=_cols(gathered[1]), w_ff2=gathered[2].reshape(DFF, D))

    qc, kc = _conv_fwd(proj, W["conv_w8"], W["conv_b"])
    gates_t = proj[:, P_G:P_G + NG].T
    h_dirs, cst, nm = _mlstm_fwd(qc, kc, proj, gates_t, W["bg_row"])
    hrows = [(h_dirs, MW, 0, 0), (h_dirs, MW, 0, 1), (proj, MW, P_OM // MW)]

    def ml_out(ao, hf, hb, om, gmn):
        outs = [ao.astype(F32)]
        hs = hf + hb
        for hh in range(HM):
            sl = hsl(hh, DM)
            outs.append(_sigmoid(om[:, sl]) * _rms(hs[:, sl], DM)[0] * gmn[:, sl])
        return [cat1(outs)], []

    (cat,) = _rowmap(ml_out, [(attn_o, MW, 0)] + hrows, [W["g_mn"]], [(D, MX)], tile=128, name="ml_out")
    mixed = _mm(cat, W["w_out"], "nn", name="out_proj")

    def res_ln2(xv, mx, g, mv):
        x1 = xv + mv[2:3] * mx
        return [x1, _rms(x1, D)[0] * g * (1.0 + mv[4:5]) + mv[3:4]], []

    x1, h2 = _rowmap(res_ln2, [(x, D, 0), (mixed, D, 0)], [W["g_mlp"], modv], [(D, F32), (D, MX)],
                     tile=128, name="res_ln2")
    a, u = _mm(h2, W["w_ff1"], "nn", name="ff1", out_dtypes=(MX, MX),
               epilogue=lambda r: (jnp.square(jnp.maximum(r, 0.0)), r))
    y = _mm(a, W["w_ff2"], "nn", name="ff2")

    def final(x1v, yv, tv, mv):
        err = x1v + mv[5:6] * yv - tv
        dout = err * (1.0 / D)
        loss = jnp.sum(jnp.sum(0.5 * err * dout, axis=1, keepdims=True), axis=0, keepdims=True)
        return [dout, mv[5:6] * dout], [loss, _colsum(dout * yv)]

    dout, dy, loss, dgate2 = _rowmap(final, [(x1, D, 0), (y, D, 0), (tgt, D, 0)], [modv], [(D, F32), (D, MX)],
                                     [(1, 1), (1, D)], tile=128, name="loss_head")

    du = _mm(dy, W["w_ff2"], "nt", name="ff2_dx", out_dtypes=(MX,), extras=(u,),
             epilogue=lambda r, uv: (r * (2.0 * jnp.maximum(uv.astype(F32), 0.0)),))
    g_ff2 = _mm(a, dy, "tn", name="ff2_dw")
    dh2 = _mm(du, W["w_ff1"], "nt", name="ff1_dx")
    g_ff1 = _mm(h2, du, "tn", name="ff1_dw")

    def ln2_bwd(dh2v, x1v, doutv, mxv, g, mv):
        xhat, r = _rms(x1v, D)
        dn2 = dh2v * (1.0 + mv[4:5])
        dx1 = doutv + _rms_bwd(dn2 * g, xhat, r, D)
        return [dx1, mv[2:3] * dx1], [_colsum(dh2v), _colsum(dh2v * xhat * g), _colsum(dn2 * xhat), _colsum(dx1 * mxv)]

    dx1, dmixed, dshift2, dscale2, dg_mlp, dgate1 = _rowmap(
        ln2_bwd, [(dh2, D, 0), (x1, D, 0), (dout, D, 0), (mixed, D, 0)], [W["g_mlp"], modv],
        [(D, F32), (D, MX)], [(1, D)] * 4, tile=128, name="ln2_bwd")
    dcat = _mm(dmixed, W["w_out"], "nt", name="out_dx")
    g_out = _mm(cat, dmixed, "tn", name="out_dw")

    def ml_out_bwd(dml, hf, hb, om, gmn):
        hs = hf + hb
        dhs, dos, dgs = [], [], []
        for hh in range(HM):
            sl = hsl(hh, DM)
            xhat, r = _rms(hs[:, sl], DM)
            g, sg, d = gmn[:, sl], _sigmoid(om[:, sl]), dml[:, sl]
            dos.append(d * xhat * g * sg * (1.0 - sg))
            dhn = d * sg
            dgs.append(_colsum(dhn * xhat))
            dhs.append(_rms_bwd(dhn * g, xhat, r, DM))
        return [cat1(dhs), cat1(dos)], [cat1(dgs)]

    dhs, do_m, dg_mn = _rowmap(ml_out_bwd, [(dcat, MW, 1)] + hrows, [W["g_mn"]], [(MW, F32), (MW, MX)],
                               [(1, MW)], tile=128, name="ml_out_bwd")
    dqd, dkd, dvd, dgates = _mlstm_bwd(qc, kc, proj, gates_t, W["bg_row"], dhs, cst, nm)
    dqk_m, dconv_w8, dconv_b = _conv_bwd(proj, dqd, dkd, W["conv_w8"], W["conv_b"])

    def delta_fn(ao, dov):
        lane = lax.broadcasted_iota(jnp.int32, (ao.shape[0], 128), 1)
        acc = jnp.zeros((ao.shape[0], 128), F32)
        for hh in range(H_MLA):
            sl = hsl(hh, VD)
            acc = acc + jnp.where(lane == hh, jnp.sum(ao[:, sl].astype(F32) * dov[:, sl], axis=1, keepdims=True), 0.0)
        return [acc], []

    (dl,) = _rowmap(delta_fn, [(attn_o, MW, 0), (dcat, MW, 0)], [], [(128, F32)], tile=256, name="attn_delta")
    dl_hs = dl[:, :H_MLA].T
    side = [g_out.reshape(N_CHIP, D // N_CHIP, D).astype(MX), _slabs(g_ff1).astype(MX),
            g_ff2.reshape(N_CHIP, DFF // N_CHIP, D).astype(MX)] if late else ()
    dk_a, dv_a, late_got = _attn_bwd_dkv(qh, kh, vh, dcat, lse.reshape(H_MLA, S), dl_hs, side=side)
    dq_a = _attn_bwd_dq(qh, kh, vh, dcat, lse, dl_hs.reshape(H_MLA, S, 1))

    def mla_q_bwd(dqv, qr, cp, a1, a2, gq):
        outs, dg = [], 0.0
        for hh in range(H_MLA):
            sl = hsl(hh, HP)
            xhat, r = _rms(qr[:, sl], QK)
            d = dqv[:, sl]
            dyv = cat1([d[:, :NOPE], _rope_bwd(d[:, NOPE:], cp, a1, a2)])
            dg = dg + _colsum(dyv * xhat)
            outs.append(_rms_bwd(dyv * gq, xhat, r, QK))
        return [cat1(outs)], [dg]

    dq_raw, dgq = _rowmap(mla_q_bwd, [(dq_a, H_MLA * HP, 0), (q_raw, H_MLA * HP, 0)] + tabs, [W["gq"]],
                          [(H_MLA * HP, MX)], [(1, HP)], tile=128, name="mla_q_bwd")
    dcqn = _mm(dq_raw, W["w_uq"], "nt", name="q_up_dx")
    g_uq = _mm(cqn, dq_raw, "tn", name="q_up_dw")

    def mla_k_bwd(dkv, dvv, kvr, kpe, cp, a1, a2, gk):
        dkn, dg, dkpe = [], 0.0, 0.0
        for hh in range(H_MLA):
            xhat, r = _rms(cat1([kvr[:, hsl(hh, NOPE)], kpe]), QK)
            d = dkv[:, hsl(hh, HP)]
            dyv = cat1([d[:, :NOPE], _rope_bwd(d[:, NOPE:], cp, a1, a2)])
            dg = dg + _colsum(dyv * xhat)
            dxv = _rms_bwd(dyv * gk, xhat, r, QK)
            dkn.append(dxv[:, :NOPE])
            dkpe = dkpe + dxv[:, NOPE:]
        return [cat1(dkn + [dvv]), dkpe], [dg]

    dkv_raw, dkpe, dgk = _rowmap(
        mla_k_bwd, [(dk_a, H_MLA * HP, 0), (dv_a, H_MLA * VD, 0), (kv_raw, 2 * H_MLA * NOPE, 0),
                    (proj, 128, P_KPE // 128)] + tabs, [W["gk"]],
        [(2 * H_MLA * NOPE, MX), (128, MX)], [(1, HP)], tile=128, name="mla_k_bwd")
    dckvn = _mm(dkv_raw, W["w_ukv"], "nt", name="kv_up_dx")
    g_ukv = _mm(ckvn, dkv_raw, "tn", name="kv_up_dw")

    def lora_bwd(dcq, dckv, cq, ckv, gq, gkv):
        xq, rq = _rms(cq, Q_LORA)
        xk, rk = _rms(ckv, KV_LORA)
        return ([_rms_bwd(dcq * gq, xq, rq, Q_LORA), _rms_bwd(dckv * gkv, xk, rk, KV_LORA)],
                [_colsum(dcq * xq), _colsum(dckv * xk)])

    dc_q, dc_kv, dg_qlora, dg_kvlora = _rowmap(
        lora_bwd, [(dcqn, Q_LORA, 0), (dckvn, KV_LORA, 0), (proj, Q_LORA, P_CQ // Q_LORA),
                   (proj, KV_LORA, P_CKV // KV_LORA)], [W["g_qlora"], W["g_kvlora"]],
        [(Q_LORA, MX), (KV_LORA, MX)], [(1, Q_LORA), (1, KV_LORA)], tile=256, name="lora_bwd")

    nc = S // LCH
    dg16 = dgates[:, :, 0:2, :].reshape(2, HM, nc, 2, LCH).transpose(2, 4, 0, 3, 1).reshape(S, NG)
    dg128 = jnp.pad(dg16, ((0, 0), (0, 128 - NG)))

    def assemble(dqk, dv0, dv1, dom, dcq, dckv, dkp, dgp):
        f = lambda t: t.astype(F32)
        return [cat1([f(dqk), dv0 + dv1, f(dom), f(dcq), f(dckv), f(dkp), dgp])], [_colsum(dgp)]

    dproj, dbg = _rowmap(
        assemble, [(dqk_m, 2 * MW, 0), (dvd, MW, 0, 0), (dvd, MW, 0, 1), (do_m, MW, 0), (dc_q, Q_LORA, 0),
                   (dc_kv, KV_LORA, 0), (dkpe, 128, 0), (dg128, 128, 0)], [], [(D_INP, MX)], [(1, 128)],
        tile=128, name="dproj")
    dh = _mm(dproj, W["w_in"], "nt", name="proj_dx")
    g_in = _mm(h, dproj, "tn", name="proj_dw")

    def ln1_bwd(dhv, xv, dx1v, g, mv):
        xhat, r = _rms(xv, D)
        dn = dhv * (1.0 + mv[1:2])
        return [dx1v + _rms_bwd(dn * g, xhat, r, D)], [_colsum(dhv), _colsum(dhv * xhat * g), _colsum(dn * xhat)]

    gx, dshift1, dscale1, dg_mix = _rowmap(ln1_bwd, [(dh, D, 0), (x, D, 0), (dx1, D, 0)], [W["g_mix"], modv],
                                           [(D, F32)], [(1, D)] * 3, tile=128, name="ln1_bwd")
    dmodv = jnp.concatenate([dshift1, dscale1, dgate1, dshift2, dscale2, dgate2], axis=0)
    grads = dict(w_in=g_in, w_uq=g_uq, w_ukv=g_ukv, w_out=g_out, w_ff1=g_ff1, w_ff2=g_ff2,
                 norm_mix_g=dg_mix, b_gates=dbg[:, :NG], conv_w=dconv_w8[:CONVW], conv_b=dconv_b,
                 q_lora_g=dg_qlora, kv_lora_g=dg_kvlora, q_norm_g=dgq[:, :QK], k_norm_g=dgk[:, :QK],
                 mlstm_norm_g=dg_mn, norm_mlp_g=dg_mlp)
    grads["late"] = late_got
    return loss, gx, dmodv, grads


def _cols(g):
    return g.transpose(1, 0, 2).reshape(g.shape[1], N_CHIP * g.shape[2])


def _slabs(gfull):
    return gfull.reshape(gfull.shape[0], N_CHIP, -1).transpose(1, 0, 2)


def _prep_weights(w_in, w_uq, w_ukv, w_out, w_ff1, w_ff2, norm_mix_g, norm_mlp_g, q_lora_g, kv_lora_g,
                  q_norm_g, k_norm_g, mlstm_norm_g, conv_w, conv_b, b_gates):
    MX = _MXU_DTYPE
    padg = lambda g: jnp.pad(g.reshape(1, QK).astype(F32), ((0, 0), (0, HP - QK)))
    return dict(
        w_in=_pad_w_in(w_in).astype(MX), w_uq=_pad_w_uq(w_uq).astype(MX), w_ukv=_perm_w_ukv(w_ukv).astype(MX),
        w_out=None if w_out is None else w_out.astype(MX), w_ff1=None if w_ff1 is None else w_ff1.astype(MX),
        w_ff2=None if w_ff2 is None else w_ff2.astype(MX),
        g_mix=norm_mix_g.reshape(1, D), g_mlp=norm_mlp_g.reshape(1, D), g_qlora=q_lora_g.reshape(1, Q_LORA),
        g_kvlora=kv_lora_g.reshape(1, KV_LORA), gq=padg(q_norm_g), gk=padg(k_norm_g),
        g_mn=mlstm_norm_g.reshape(1, MW), conv_w8=jnp.pad(conv_w.reshape(CONVW, 2 * MW), ((0, 8 - CONVW), (0, 0))),
        conv_b=conv_b.reshape(1, 2 * MW), bg_row=jnp.pad(b_gates.reshape(1, NG), ((0, 0), (0, 128 - NG))))


MESH = pl.DeviceIdType.MESH
N_DEV = 8
N_CHIP = 4


def _comm_call(body, **kw):
    if _INTERPRET:
        kw["interpret"] = pltpu.InterpretParams()
    return pl.pallas_call(body, **kw)


def _allgather8(blk, *, name):
    m_per, n = blk.shape

    def body(x_ref, out_ref, send_sems, recv_sems, local_sem):
        x, y, c = lax.axis_index("x"), lax.axis_index("y"), lax.axis_index("c")
        me, sibling = (x, y, c), (x, y, 1 - c)
        chips = [(1 - x, y), (x, 1 - y), (1 - x, 1 - y)]

        def rows(px, py, pc):
            return out_ref.at[pl.ds((4 * px + 2 * py + pc) * m_per, m_per), :]

        def copy(k, block, to, src=None):
            return pltpu.make_async_remote_copy(
                src_ref=rows(*block) if src is None else src, dst_ref=rows(*block),
                send_sem=send_sems.at[k], recv_sem=recv_sems.at[k], device_id=to, device_id_type=MESH)

        mine = pltpu.make_async_copy(x_ref, rows(*me), local_sem)
        mine.start()
        first = [copy(0, me, sibling, src=x_ref)]
        first += [copy(1 + j, me, (*chip, c), src=x_ref) for j, chip in enumerate(chips)]
        for cp in first:
            cp.start()
        passed = [copy(4 + j, (*chip, c), sibling) for j, chip in enumerate(chips)]
        for j, chip in enumerate(chips):
            copy(1 + j, (*chip, c), me).wait_recv()
            passed[j].start()
        copy(0, sibling, me).wait_recv()
        for j, chip in enumerate(chips):
            copy(4 + j, (*chip, 1 - c), me).wait_recv()
        for cp in first + passed:
            cp.wait_send()
        mine.wait()

    return _comm_call(
        body, name=name, out_shape=jax.ShapeDtypeStruct((N_DEV * m_per, n), blk.dtype),
        in_specs=[pl.BlockSpec(memory_space=pltpu.VMEM)], out_specs=pl.BlockSpec(memory_space=pltpu.VMEM),
        scratch_shapes=[pltpu.SemaphoreType.DMA((7,)), pltpu.SemaphoreType.DMA((7,)), pltpu.SemaphoreType.DMA],
    )(blk)


def _chip_exchange(arrays, *, gather, name):
    n = len(arrays)

    def body(*refs):
        start, wait = _exchange_ops(refs[:n], refs[n:2 * n], *refs[2 * n:], gather=gather)
        start()
        wait()

    io = _exchange_io(arrays, gather)
    return _comm_call(body, name=name, out_shape=io["out_shape"], in_specs=io["specs"], out_specs=io["specs"],
                      scratch_shapes=io["scratch"])(*arrays)


def _exchange_io(arrays, gather):
    n = len(arrays)
    shard = (lambda a: a.shape) if gather else (lambda a: a.shape[1:])
    return dict(
        specs=[pl.BlockSpec(memory_space=pltpu.HBM)] * n,
        out_shape=[jax.ShapeDtypeStruct((N_CHIP, *shard(a)), a.dtype) for a in arrays],
        scratch=[pltpu.SemaphoreType.DMA((3 * n,)), pltpu.SemaphoreType.DMA((3 * n,)), pltpu.SemaphoreType.DMA((n,))])


def _exchange_ops(ins, outs, send_sems, recv_sems, local_sems, *, gather):
    n = len(ins)
    x, y, c = lax.axis_index("x"), lax.axis_index("y"), lax.axis_index("c")
    k = 2 * x + y
    chips = [(1 - x, y), (x, 1 - y), (1 - x, 1 - y)]

    def remote(a, j):
        px, py = chips[j]
        src = ins[a] if gather else ins[a].at[2 * px + py]
        return pltpu.make_async_remote_copy(
            src_ref=src, dst_ref=outs[a].at[k], send_sem=send_sems.at[3 * a + j],
            recv_sem=recv_sems.at[3 * a + j], device_id=(px, py, c), device_id_type=MESH)

    def arrival(a, j):
        px, py = chips[j]
        src = ins[a] if gather else ins[a].at[k]
        return pltpu.make_async_remote_copy(
            src_ref=src, dst_ref=outs[a].at[2 * px + py], send_sem=send_sems.at[3 * a + j],
            recv_sem=recv_sems.at[3 * a + j], device_id=(px, py, c), device_id_type=MESH)

    local = [pltpu.make_async_copy(ins[a] if gather else ins[a].at[k], outs[a].at[k], local_sems.at[a])
             for a in range(n)]
    sent = [remote(a, j) for a in range(n) for j in range(3)]

    def start():
        for cp in local + sent:
            cp.start()

    def wait():
        for a in range(n):
            for j in range(3):
                arrival(a, j).wait_recv()
        for cp in sent:
            cp.wait_send()
        for cp in local:
            cp.wait()

    return start, wait


def _sibling_exchange(arrays, *, name):
    n = len(arrays)

    def body(*refs):
        ins, outs = refs[:n], refs[n:2 * n]
        send_sems, recv_sems = refs[2 * n:]
        x, y, c = lax.axis_index("x"), lax.axis_index("y"), lax.axis_index("c")
        cps = [pltpu.make_async_remote_copy(
            src_ref=ins[a], dst_ref=outs[a], send_sem=send_sems.at[a], recv_sem=recv_sems.at[a],
            device_id=(x, y, 1 - c), device_id_type=MESH) for a in range(n)]
        for cp in cps:
            cp.start()
        for cp in cps:
            cp.wait()

    hbm = pl.BlockSpec(memory_space=pltpu.HBM)
    return _comm_call(
        body, name=name, out_shape=[jax.ShapeDtypeStruct(a.shape, a.dtype) for a in arrays],
        in_specs=[hbm] * n, out_specs=[hbm] * n,
        scratch_shapes=[pltpu.SemaphoreType.DMA((n,)), pltpu.SemaphoreType.DMA((n,))],
    )(*arrays)


def _sum_blocks(a, nblk, *, name):
    n = a.shape[1]

    def body(a_ref, o_ref):
        acc = a_ref[pl.ds(0, 8), :]
        for d in range(1, nblk):
            acc = acc + a_ref[pl.ds(8 * d, 8), :]
        o_ref[...] = acc

    return _call(body, name=name, out_shape=jax.ShapeDtypeStruct((8, n), F32))(a)


def _outer8(sct, dm, *, name, tm=256, tn=1024):
    R, N = sct.shape[0], dm.shape[1]
    tm, tn = min(tm, R), min(tn, N)

    def body(s_ref, d_ref, o_ref):
        s, dmv = s_ref[...], d_ref[...]
        acc = s[:, 0:1] * dmv[0:1, :]
        for b in range(1, 8):
            acc = acc + s[:, b:b + 1] * dmv[b:b + 1, :]
        o_ref[...] = acc

    return _call(
        body, name=name, grid=(R // tm, N // tn),
        in_specs=[pl.BlockSpec((tm, 8), lambda i, j: (i, 0)), pl.BlockSpec((8, tn), lambda i, j: (0, j))],
        out_specs=pl.BlockSpec((tm, tn), lambda i, j: (i, j)),
        out_shape=jax.ShapeDtypeStruct((R, N), F32),
        compiler_params=_cparams(("parallel", "parallel")),
    )(sct, dm)


_BC1 = 1.0 - ADAM_B1 ** ADAM_STEP
_BC2 = 1.0 - ADAM_B2 ** ADAM_STEP


def _adamw(w, g_parts, m, v, *, name, tile=128):
    R, C = w.shape
    tile = min(tile, R)
    assert R % tile == 0
    npart = len(g_parts)

    def body(*refs):
        w_ref, m_ref, v_ref = refs[npart:npart + 3]
        g_o, d_o, m_o, v_o = refs[npart + 3:]
        g = refs[0][...].astype(F32)
        for r in refs[1:npart]:
            g = g + r[...].astype(F32)
        mn = ADAM_B1 * m_ref[...] + (1.0 - ADAM_B1) * g
        vn = ADAM_B2 * v_ref[...] + (1.0 - ADAM_B2) * jnp.square(g)
        g_o[...] = g
        m_o[...] = mn
        v_o[...] = vn
        d_o[...] = -ADAM_LR * ((mn / _BC1) / (jnp.sqrt(vn / _BC2) + ADAM_EPS) + ADAM_WD * w_ref[...])

    spec = pl.BlockSpec((tile, C), lambda i: (i, 0))
    return _call(
        body, name=name, grid=(R // tile,), in_specs=[spec] * (npart + 3), out_specs=[spec] * 4,
        out_shape=[jax.ShapeDtypeStruct((R, C), F32)] * 4,
        compiler_params=_cparams(("parallel",)),
    )(*g_parts, w, m, v)


def _pack(vecs, rows8_cols):
    flat = jnp.concatenate([v.reshape(-1).astype(F32) for v in vecs])
    return jnp.pad(flat, (0, 8 * rows8_cols - flat.shape[0])).reshape(8, rows8_cols)


def _unpack(flat, shapes):
    out, o = [], 0
    for s in shapes:
        n = math.prod(s)
        out.append(flat[o:o + n].reshape(s))
        o += n
    return out


_BIG = ("w_in", "w_uq", "w_ukv", "w_out", "w_ff1", "w_ff2")
_SMALL = ("b_ada", "norm_mix_g", "b_gates", "conv_w", "conv_b", "q_lora_g", "kv_lora_g", "q_norm_g", "k_norm_g",
          "mlstm_norm_g", "norm_mlp_g")
_ORDER = ("w_ada", "b_ada", "norm_mix_g", "w_in", "b_gates", "conv_w", "conv_b", "q_lora_g", "w_uq", "kv_lora_g",
          "w_ukv", "q_norm_g", "k_norm_g", "mlstm_norm_g", "w_out", "norm_mlp_g", "w_ff1", "w_ff2")


def kernel(x, c, positions, w_ada, b_ada, norm_mix_g, w_in, b_gates, conv_w, conv_b, q_lora_g, w_uq, kv_lora_g, w_ukv, q_norm_g, k_norm_g, mlstm_norm_g, w_out, norm_mlp_g, w_ff1, w_ff2, loss_target, m_w_ada, m_b_ada, m_norm_mix_g, m_w_in, m_b_gates, m_conv_w, m_conv_b, m_q_lora_g, m_w_uq, m_kv_lora_g, m_w_ukv, m_q_norm_g, m_k_norm_g, m_mlstm_norm_g, m_w_out, m_norm_mlp_g, m_w_ff1, m_w_ff2, v_w_ada, v_b_ada, v_norm_mix_g, v_w_in, v_b_gates, v_conv_w, v_conv_b, v_q_lora_g, v_w_uq, v_kv_lora_g, v_w_ukv, v_q_norm_g, v_k_norm_g, v_mlstm_norm_g, v_w_out, v_norm_mlp_g, v_w_ff1, v_w_ff2):
    args = dict(locals())
    wts = {n: args[n] for n in _ORDER}
    mom = {n: args["m_" + n] for n in _ORDER}
    var = {n: args["v_" + n] for n in _ORDER}
    MX = _MXU_DTYPE
    xi, yi, ci = lax.axis_index("x"), lax.axis_index("y"), lax.axis_index("c")
    chip = 2 * xi + yi
    dev = 2 * chip + ci
    S = x.shape[1]
    CS = 2 * MW // N_CHIP
    GS = DM // N_CHIP

    pk = _pack([c, conv_w, mlstm_norm_g], 1024)
    allpk = _allgather8(pk, name="gather_small").reshape(N_DEV, 8 * 1024)
    c_all = allpk[:, :D]
    per_chip = allpk[0::2]
    conv_w_full = per_chip[:, D:D + CONVW * CS].reshape(N_CHIP, CONVW, CS).transpose(1, 0, 2).reshape(CONVW, 2 * MW)
    o = D + CONVW * CS
    mn_full = per_chip[:, o:o + HM * GS].reshape(N_CHIP, HM, GS).transpose(1, 0, 2).reshape(HM, DM)

    (sc,) = _rowmap(lambda cv: ([cv * _sigmoid(cv)], []), [(c_all, D, 0)], [], [(D, F32)], tile=8, name="silu_c")
    ncol = w_ada.shape[2]
    b_cols = lax.dynamic_slice(b_ada, (0, chip * ncol), (1, ncol))
    modp = _mm(sc, w_ada[0], "nn", name="ada_fwd", tm=8, tn=1024, tk=512, extras=(jnp.broadcast_to(b_cols, (8, ncol)),),
               epilogue=lambda r, b: (r + b,))
    modg = _allgather8(modp, name="gather_mod").reshape(N_CHIP, 2, 8, ncol)[:, 0]
    mod_all = modg.transpose(1, 0, 2).reshape(N_DEV, N_CHIP * ncol)
    modv = jnp.pad(lax.dynamic_slice(mod_all, (dev, 0), (1, 6 * D)).reshape(6, D), ((0, 2), (0, 0)))

    shards = [wts[n][0].astype(MX) for n in _BIG]
    gw_in, gw_uq, gw_ukv = _chip_exchange(shards[:3], gather=True, name="gather_weights")
    W = _prep_weights(_cols(gw_in), _cols(gw_uq), _cols(gw_ukv), None, None, None, norm_mix_g, norm_mlp_g,
                      q_lora_g, kv_lora_g, q_norm_g, k_norm_g, mn_full, conv_w_full, conv_b, b_gates)

    loss, gx, dmodv, g = _device_step(x[0], loss_target[0], positions[0], modv, W, late=shards[3:])

    small_shapes = [(6 * D,), (D,), (NG,), (CONVW, 2 * MW), (2 * MW,), (Q_LORA,), (KV_LORA,), (QK,), (QK,), (MW,), (D,), (1,)]
    pg = _pack([dmodv, g["norm_mix_g"], g["b_gates"], g["conv_w"], g["conv_b"], g["q_lora_g"], g["kv_lora_g"],
                g["q_norm_g"], g["k_norm_g"], g["mlstm_norm_g"], g["norm_mlp_g"], loss], 4096)
    allpg = _allgather8(pg, name="gather_small_grads")
    tot = _unpack(_sum_blocks(allpg, N_DEV, name="sum_small_grads").reshape(-1), small_shapes)
    dmod_all = allpg.reshape(N_DEV, 8 * 4096)[:, :6 * D]
    gsmall = dict(zip(_SMALL, [tot[0].reshape(1, 6 * D), tot[1].reshape(1, D), tot[2].reshape(1, NG),
                               lax.dynamic_slice(tot[3], (0, chip * CS), (CONVW, CS)).reshape(1, CONVW, CS),
                               tot[4].reshape(1, 2 * MW), tot[5].reshape(1, Q_LORA), tot[6].reshape(1, KV_LORA),
                               tot[7].reshape(1, QK), tot[8].reshape(1, QK),
                               lax.dynamic_slice(tot[9].reshape(HM, DM), (0, chip * GS), (HM, GS)).reshape(1, HM, GS),
                               tot[10].reshape(1, D)]))
    loss_tot = tot[11].reshape(())

    to_send = [_slabs(_unpad_w_in(g["w_in"])).astype(MX), _slabs(_unpad_w_uq(g["w_uq"])).astype(MX),
               _slabs(_unperm_w_ukv(g["w_ukv"])).astype(MX)]
    got = list(_chip_exchange(to_send, gather=False, name="scatter_grads")) + list(g["late"])
    part = []
    for nme, r in zip(_BIG, got):
        wd = r.shape[2]
        (p,) = _rowmap(lambda a0, a1, a2, a3: ([(a0.astype(F32) + a1.astype(F32)) + (a2.astype(F32) + a3.astype(F32))], []),
                       [(r, wd, 0, k) for k in range(N_CHIP)], [], [(wd, F32)], tile=256, name="sum_chips_" + nme)
        part.append(p)
    other = _sibling_exchange(part, name="exchange_cores")

    dm_cols = lax.dynamic_slice(dmod_all, (0, chip * ncol), (N_DEV, ncol))
    g_ada = _outer8(sc.T, dm_cols, name="ada_dw")

    res = {}
    for nme, p, q in zip(_BIG, part, other):
        res[nme] = _adamw(wts[nme][0], [p, q], mom[nme][0], var[nme][0], name="adamw_" + nme)
    res["w_ada"] = _adamw(w_ada[0], [g_ada], m_w_ada[0], v_w_ada[0], name="adamw_w_ada")
    sw = _pack([wts[n] for n in _SMALL], 3072)
    sg = _pack([gsmall[n] for n in _SMALL], 3072)
    sm = _pack([mom[n] for n in _SMALL], 3072)
    sv = _pack([var[n] for n in _SMALL], 3072)
    small_res = _adamw(sw, [sg], sm, sv, name="adamw_small", tile=8)
    shapes = [wts[n].shape for n in _SMALL]
    unp = [_unpack(r.reshape(-1), shapes) for r in small_res]
    for i, nme in enumerate(_SMALL):
        res[nme] = tuple(u[i] for u in unp)
    outs = [loss_tot, gx[None]]
    for kind in range(4):
        outs += [res[n][kind].reshape(wts[n].shape) for n in _ORDER]
    return tuple(outs)
```

```python
import functools
import math

import jax
import jax.numpy as jnp
from jax import lax
from jax.experimental import pallas as pl
from jax.experimental.pallas import tpu as pltpu

F32 = jnp.float32
BF16 = jnp.bfloat16
_MXU_DTYPE = jnp.bfloat16
_INTERPRET = False

D = 2048
H_MLA = 8
NOPE = 128
ROPE = 64
QK = NOPE + ROPE
HP = 256
VD = 128
Q_LORA = 512
KV_LORA = 256
HM = 4
DM = 256
MW = HM * DM
LCH = 128
CONVW = 5
NG = 16
DFF = 4 * D
EPS = 1e-6
M_INIT = -1e30
ROPE_THETA = 10000.0
IN_SIZES = (Q_LORA, KV_LORA, ROPE, MW, MW, MW, MW, NG)
D_IN = sum(IN_SIZES)
P_QM, P_KM, P_VM, P_OM, P_CQ, P_CKV, P_KPE, P_G = 0, 1024, 2048, 3072, 4096, 4608, 4864, 4992
D_INP = 5120

ADAM_LR, ADAM_B1, ADAM_B2, ADAM_EPS, ADAM_WD, ADAM_STEP = 0.001, 0.9, 0.999, 1e-08, 0.01, 10

V7X_VMEM_LIMIT = 56 * 1024 * 1024


def _cparams(sem):
    return pltpu.CompilerParams(dimension_semantics=sem, vmem_limit_bytes=V7X_VMEM_LIMIT)


def _call(body, **kw):
    if _INTERPRET:
        kw.pop("compiler_params", None)
        kw["interpret"] = pltpu.InterpretParams()
    return pl.pallas_call(body, **kw)


def _dot(a, b, form):
    dims = {"nn": ((1,), (0,)), "nt": ((1,), (1,)), "tn": ((0,), (0,))}[form]
    return lax.dot_general(a.astype(_MXU_DTYPE), b.astype(_MXU_DTYPE), (dims, ((), ())),
                           preferred_element_type=F32)


def _mm(a, b, form, *, name, out_dtypes=(F32,), epilogue=None, extras=(), tm=1024, tn=1024, tk=2048, side=()):
    if form == "nn":
        (M, K), (K2, N) = a.shape, b.shape
    elif form == "nt":
        (M, K), (N, K2) = a.shape, b.shape
    else:
        (K, M), (K2, N) = a.shape, b.shape
    assert K == K2, (a.shape, b.shape, form)
    tm, tn = min(tm, M), min(tn, N)
    tk = max(d for d in range(128, min(tk, K) + 1, 128) if K % d == 0) if K > 128 else K
    assert M % tm == 0 and N % tn == 0 and K % tk == 0, (M, N, K, tm, tn, tk)
    nk = K // tk
    ne, no = len(extras), len(out_dtypes)
    if form == "tn":
        a_spec = pl.BlockSpec((tk, tm), lambda i, j, k: (k, i))
    else:
        a_spec = pl.BlockSpec((tm, tk), lambda i, j, k: (i, k))
    if form == "nt":
        b_spec = pl.BlockSpec((tn, tk), lambda i, j, k: (j, k))
    else:
        b_spec = pl.BlockSpec((tk, tn), lambda i, j, k: (k, j))
    mn_spec = pl.BlockSpec((tm, tn), lambda i, j, k: (i, j))
    grid = (M // tm, N // tn, nk)
    ns, io, wrap = _side_exchange(side, False, grid)

    def body(a_ref, b_ref, *rest):
        ex, outs = rest[:ne], rest[ne + ns:ne + ns + no]
        scratch = rest[ne + 2 * ns + no:]
        side_start, side_wait = wrap(rest[ne:ne + ns], rest[ne + ns + no:ne + 2 * ns + no], scratch[1:])
        side_start()
        prod = _dot(a_ref[...], b_ref[...], form)

        def finish(r):
            vals = (r,) if epilogue is None else epilogue(r, *[e[...] for e in ex])
            for o, v in zip(outs, vals):
                o[...] = v.astype(o.dtype)

        if nk == 1:
            finish(prod)
        else:
            acc, k = scratch[0], pl.program_id(2)

            @pl.when(k == 0)
            def _():
                acc[...] = prod

            @pl.when(k > 0)
            def _():
                acc[...] += prod

            @pl.when(k == nk - 1)
            def _():
                finish(acc[...])
        side_wait()

    res = _call(
        body, name=name, grid=grid,
        in_specs=[a_spec, b_spec] + [mn_spec] * ne + io["specs"],
        out_specs=[mn_spec] * no + io["specs"],
        out_shape=[jax.ShapeDtypeStruct((M, N), dt) for dt in out_dtypes] + io["out_shape"],
        scratch_shapes=[pltpu.VMEM((tm, tn) if nk > 1 else (8, 128), F32)] + io["scratch"],
        compiler_params=_cparams(("arbitrary",) * 3 if ns else ("parallel", "parallel", "arbitrary")),
    )(a, b, *extras, *side)
    if ns:
        return (res[0] if no == 1 else res[:no]), list(res[no:])
    return res[0] if no == 1 else res


def _rowmap(fn, rows, bcasts, outs, accs=(), *, tile, name):
    rows = [r if len(r) == 4 else (*r, None) for r in rows]
    S = rows[0][0].shape[-2]
    tile = min(tile, S)
    assert S % tile == 0
    nr, nb, no, na = len(rows), len(bcasts), len(outs), len(accs)

    def body(*refs):
        vals = [r[...] for r in refs[:nr + nb]]
        o_refs, a_refs = refs[nr + nb:nr + nb + no], refs[nr + nb + no:]
        o_vals, a_vals = fn(*vals)
        for r, v in zip(o_refs, o_vals):
            r[...] = v.astype(r.dtype)
        if na:
            @pl.when(pl.program_id(0) == 0)
            def _():
                for r in a_refs:
                    r[...] = jnp.zeros(r.shape, r.dtype)
            for r, v in zip(a_refs, a_vals):
                r[...] += v

    in_specs = []
    for (arr, w, cb, lead) in rows:
        if lead is None:
            in_specs.append(pl.BlockSpec((tile, w), lambda i, cb=cb: (i, cb)))
        else:
            in_specs.append(pl.BlockSpec((None, tile, w), lambda i, cb=cb, lead=lead: (lead, i, cb)))
    in_specs += [pl.BlockSpec(b.shape, lambda i: (0, 0)) for b in bcasts]
    out_specs = [pl.BlockSpec((tile, w), lambda i: (i, 0)) for (w, _) in outs]
    out_specs += [pl.BlockSpec(s, lambda i: (0, 0)) for s in accs]
    out_shape = [jax.ShapeDtypeStruct((S, w), dt) for (w, dt) in outs]
    out_shape += [jax.ShapeDtypeStruct(s, F32) for s in accs]
    return _call(
        body, name=name, grid=(S // tile,), in_specs=in_specs, out_specs=out_specs, out_shape=out_shape,
        compiler_params=_cparams(("arbitrary",)),
    )(*[r[0] for r in rows], *bcasts)


def _colsum(v):
    return jnp.sum(v, axis=0, keepdims=True)


def _rms(x, n):
    r = lax.rsqrt(jnp.sum(x * x, axis=-1, keepdims=True) * (1.0 / n) + EPS)
    return x * r, r


def _rms_bwd(dxhat, xhat, r, n):
    return r * (dxhat - xhat * (jnp.sum(dxhat * xhat, axis=-1, keepdims=True) * (1.0 / n)))


def _rope_fwd(r, cosp, s1, s2):
    return r * cosp + pltpu.roll(r, 32, 1) * s1 + pltpu.roll(r, 96, 1) * s2


def _rope_bwd(d, cosp, s1, s2):
    return d * cosp + pltpu.roll(d * s1, 96, 1) + pltpu.roll(d * s2, 32, 1)


def _sigmoid(x):
    return 1.0 / (1.0 + jnp.exp(-x))


def _halo_specs(tile, halo, width, cb, S, lead=None):
    nh = tile // halo
    last = S // halo - 1
    if lead is None:
        return [
            pl.BlockSpec((tile, width), lambda i: (i, cb)),
            pl.BlockSpec((halo, width), lambda i: (jnp.maximum(i * nh - 1, 0), cb)),
            pl.BlockSpec((halo, width), lambda i: (jnp.minimum((i + 1) * nh, last), cb)),
        ]
    return [
        pl.BlockSpec((None, tile, width), lambda i: (lead, i, cb)),
        pl.BlockSpec((None, halo, width), lambda i: (lead, jnp.maximum(i * nh - 1, 0), cb)),
        pl.BlockSpec((None, halo, width), lambda i: (lead, jnp.minimum((i + 1) * nh, last), cb)),
    ]


def _conv_fwd(proj, conv_w8, conv_b, *, tile=256):
    S = proj.shape[0]
    T = min(tile, S)
    n = S // T
    W = 2 * MW

    def body(x_ref, xp_ref, xn_ref, w_ref, b_ref, q_ref, k_ref, ext):
        i = pl.program_id(0)
        ext[pl.ds(0, 8), :] = xp_ref[...] * (i > 0).astype(F32)
        ext[pl.ds(8, T), :] = x_ref[...]
        ext[pl.ds(8 + T, 8), :] = xn_ref[...] * (i < n - 1).astype(F32)
        w = w_ref[...]
        y = b_ref[...] + w[0:1, :] * ext[pl.ds(6, T), :]
        for o in range(1, CONVW):
            y = y + w[o:o + 1, :] * ext[pl.ds(6 + o, T), :]
        y = y * _sigmoid(y)
        q_ref[...] = y[:, :MW].astype(q_ref.dtype)
        k_ref[...] = (y[:, MW:] * (DM ** -0.5)).astype(k_ref.dtype)

    return _call(
        body, name="conv_fwd", grid=(n,),
        in_specs=_halo_specs(T, 8, W, 0, S) + [pl.BlockSpec((8, W), lambda i: (0, 0)),
                                                 pl.BlockSpec((1, W), lambda i: (0, 0))],
        out_specs=[pl.BlockSpec((T, MW), lambda i: (i, 0))] * 2,
        out_shape=[jax.ShapeDtypeStruct((S, MW), _MXU_DTYPE)] * 2,
        scratch_shapes=[pltpu.VMEM((T + 16, W), F32)],
        compiler_params=_cparams(("arbitrary",)),
    )(proj, proj, proj, conv_w8, conv_b)


def _conv_bwd(proj, dqd, dkd, conv_w8, conv_b, *, tile=256):
    S = proj.shape[0]
    T = min(tile, S)
    n = S // T
    W = 2 * MW

    def body(x_ref, xp_ref, xn_ref, *rest):
        g = rest[:12]
        w_ref, b_ref, dx_ref, dw_ref, db_ref, ext, edp = rest[12:]
        i = pl.program_id(0)
        mp = (i > 0).astype(F32)
        mn = (i < n - 1).astype(F32)
        ext[pl.ds(0, 16), :] = xp_ref[...] * mp
        ext[pl.ds(16, T), :] = x_ref[...]
        ext[pl.ds(16 + T, 16), :] = xn_ref[...] * mn
        w = w_ref[...]
        pre = b_ref[...] + w[0:1, :] * ext[pl.ds(6, T + 16), :]
        for o in range(1, CONVW):
            pre = pre + w[o:o + 1, :] * ext[pl.ds(6 + o, T + 16), :]
        sg = _sigmoid(pre)
        dsilu = sg * (1.0 + pre * (1.0 - sg))
        for half, (a0, a1) in enumerate(((g[0:3], g[3:6]), (g[6:9], g[9:12]))):
            sc = 1.0 if half == 0 else DM ** -0.5
            cols = pl.ds(half * MW, MW)
            edp[pl.ds(0, 8), cols] = (a0[1][...] + a1[1][...]) * (mp * sc)
            edp[pl.ds(8, T), cols] = (a0[0][...] + a1[0][...]) * sc
            edp[pl.ds(8 + T, 8), cols] = (a0[2][...] + a1[2][...]) * (mn * sc)
        edp[...] = edp[...] * dsilu
        dpm = edp[pl.ds(8, T), :]
        dx = w[0:1, :] * edp[pl.ds(10, T), :]
        for o in range(1, CONVW):
            dx = dx + w[o:o + 1, :] * edp[pl.ds(10 - o, T), :]
        dx_ref[...] = dx.astype(dx_ref.dtype)

        @pl.when(i == 0)
        def _():
            dw_ref[...] = jnp.zeros(dw_ref.shape, F32)
            db_ref[...] = jnp.zeros(db_ref.shape, F32)

        for o in range(CONVW):
            dw_ref[pl.ds(o, 1), :] += _colsum(ext[pl.ds(14 + o, T), :] * dpm)
        db_ref[...] += _colsum(dpm)

    gspecs = _halo_specs(T, 8, MW, 0, S) * 4
    return _call(
        body, name="conv_bwd", grid=(n,),
        in_specs=_halo_specs(T, 16, W, 0, S) + gspecs + [pl.BlockSpec((8, W), lambda i: (0, 0)),
                                                          pl.BlockSpec((1, W), lambda i: (0, 0))],
        out_specs=[pl.BlockSpec((T, W), lambda i: (i, 0)), pl.BlockSpec((8, W), lambda i: (0, 0)),
                   pl.BlockSpec((1, W), lambda i: (0, 0))],
        out_shape=[jax.ShapeDtypeStruct((S, W), _MXU_DTYPE), jax.ShapeDtypeStruct((8, W), F32),
                   jax.ShapeDtypeStruct((1, W), F32)],
        scratch_shapes=[pltpu.VMEM((T + 32, W), F32), pltpu.VMEM((T + 16, W), F32)],
        compiler_params=_cparams(("arbitrary",)),
    )(proj, proj, proj, *([dqd[0]] * 3), *([dqd[1]] * 3), *([dkd[0]] * 3), *([dkd[1]] * 3), conv_w8, conv_b)


_ATT_SCALE = QK ** -0.5
_LOG2E = math.log2(math.e)
_Q_PRESCALE = _ATT_SCALE * _LOG2E
_ATT_SPLIT = 2


def _side_exchange(side, gather, grid):
    ns = len(side)
    io = _exchange_io(side, gather) if ns else dict(specs=[], out_shape=[], scratch=[])

    def wrap(refs_in, refs_out, sems):
        if not ns:
            return (lambda: None), (lambda: None)
        start, wait = _exchange_ops(refs_in, refs_out, *sems, gather=gather)
        ids = [pl.program_id(a) for a in range(len(grid))]
        first = functools.reduce(jnp.logical_and, [i == 0 for i in ids])
        last = functools.reduce(jnp.logical_and, [i == g - 1 for i, g in zip(ids, grid)])
        return (lambda: pl.when(first)(start)), (lambda: pl.when(last)(wait))

    return ns, io, wrap


def _attn_fwd(q, k, v, *, side=(), tq=1024, tk=8192, split=4, unroll=1):
    S = q.shape[0]
    tq, tk = min(tq, S), min(tk, S)
    nkv = S // tk
    hq = tq // split
    grid = (H_MLA, S // tq)
    ns, io, wrap = _side_exchange(side, True, grid)

    def body(q_ref, k_ref, v_ref, *rest):
        o_ref, lse_ref = rest[ns:ns + 2]
        m_s, l_s, acc_s = rest[2 * ns + 2:2 * ns + 5]
        side_start, side_wait = wrap(rest[:ns], rest[ns + 2:2 * ns + 2], rest[2 * ns + 5:])
        side_start()
        m_s[...] = jnp.full(m_s.shape, -1e30, F32)
        l_s[...] = jnp.zeros(l_s.shape, F32)
        acc_s[...] = jnp.zeros(acc_s.shape, F32)

        def step(j, carry):
            rows = pl.ds(pl.multiple_of(j * tk, tk), tk)
            kj, vj = k_ref[rows, :], v_ref[rows, :]
            for a in range(split):
                r = pl.ds(a * hq, hq)
                s = _dot(q_ref[r, :], kj, "nt")
                m_old = m_s[r, :]
                m_new = jnp.maximum(m_old, jnp.max(s, axis=1, keepdims=True))
                p = jnp.exp2(s - m_new)
                alpha = jnp.exp2(m_old - m_new)
                l_s[r, :] = alpha * l_s[r, :] + jnp.sum(p, axis=1, keepdims=True)
                acc_s[r, :] = alpha * acc_s[r, :] + _dot(p, vj, "nn")
                m_s[r, :] = m_new
            return carry

        lax.fori_loop(0, nkv, step, 0, unroll=unroll if nkv % unroll == 0 else 1)
        o_ref[...] = (acc_s[...] / l_s[...]).astype(o_ref.dtype)
        lse_ref[...] = m_s[...] + jnp.log2(l_s[...])
        side_wait()

    res = _call(
        body, name="attn_fwd", grid=grid,
        in_specs=[pl.BlockSpec((tq, HP), lambda h, i: (i, h)),
                  pl.BlockSpec((S, HP), lambda h, i: (0, h)),
                  pl.BlockSpec((S, VD), lambda h, i: (0, h))] + io["specs"],
        out_specs=[pl.BlockSpec((tq, VD), lambda h, i: (i, h)),
                   pl.BlockSpec((None, tq, 1), lambda h, i: (h, i, 0))] + io["specs"],
        out_shape=[jax.ShapeDtypeStruct((S, H_MLA * VD), _MXU_DTYPE),
                   jax.ShapeDtypeStruct((H_MLA, S, 1), F32)] + io["out_shape"],
        scratch_shapes=[pltpu.VMEM((tq, 1), F32), pltpu.VMEM((tq, 1), F32), pltpu.VMEM((tq, VD), F32)]
        + io["scratch"],
        compiler_params=_cparams(("arbitrary", "arbitrary")),
    )(q, k, v, *side)
    return res[0], res[1], list(res[2:])


def _attn_bwd_dkv(q, k, v, dcat, lse_hs, dl_hs, *, side=(), tq=4096, tk=512, split=2, unroll=1):
    S = q.shape[0]
    tq, tk = min(tq, S), min(tk, S)
    nq = S // tq
    lse_r, dl_r = lse_hs.reshape(H_MLA, nq, 1, tq), dl_hs.reshape(H_MLA, nq, 1, tq)
    hk = tk // split
    grid = (H_MLA, S // tk)
    ns, io, wrap = _side_exchange(side, False, grid)

    def body(q_ref, k_ref, v_ref, do_ref, lse_ref, dl_ref, *rest):
        dk_ref, dv_ref = rest[ns:ns + 2]
        side_start, side_wait = wrap(rest[:ns], rest[ns + 2:2 * ns + 2], rest[2 * ns + 2:])
        side_start()
        dk_ref[...] = jnp.zeros(dk_ref.shape, F32)
        dv_ref[...] = jnp.zeros(dv_ref.shape, F32)

        def step(i, carry):
            rows = pl.ds(pl.multiple_of(i * tq, tq), tq)
            qi, doi = q_ref[rows, :], do_ref[rows, :].astype(_MXU_DTYPE)
            lse, dl = lse_ref[i], dl_ref[i]
            for a in range(split):
                r = pl.ds(a * hk, hk)
                pt = jnp.exp2(_dot(k_ref[r, :], qi, "nt") - lse)
                dv_ref[r, :] += _dot(pt, doi, "nn")
                dst = pt * (_dot(v_ref[r, :], doi, "nt") - dl)
                dk_ref[r, :] += _dot(dst, qi, "nn")
            return carry

        lax.fori_loop(0, nq, step, 0, unroll=unroll if nq % unroll == 0 else 1)
        dk_ref[...] = dk_ref[...] * (1.0 / _LOG2E)
        side_wait()

    res = _call(
        body, name="attn_bwd_dkv", grid=grid,
        in_specs=[pl.BlockSpec((S, HP), lambda h, j: (0, h)),
                  pl.BlockSpec((tk, HP), lambda h, j: (j, h)),
                  pl.BlockSpec((tk, VD), lambda h, j: (j, h)),
                  pl.BlockSpec((S, VD), lambda h, j: (0, h)),
                  pl.BlockSpec((None, nq, 1, tq), lambda h, j: (h, 0, 0, 0)),
                  pl.BlockSpec((None, nq, 1, tq), lambda h, j: (h, 0, 0, 0))] + io["specs"],
        out_specs=[pl.BlockSpec((tk, HP), lambda h, j: (j, h)),
                   pl.BlockSpec((tk, VD), lambda h, j: (j, h))] + io["specs"],
        out_shape=[jax.ShapeDtypeStruct((S, H_MLA * HP), F32), jax.ShapeDtypeStruct((S, H_MLA * VD), F32)]
        + io["out_shape"],
        scratch_shapes=io["scratch"],
        compiler_params=_cparams(("arbitrary", "arbitrary")),
    )(q, k, v, dcat, lse_r, dl_r, *side)
    return res[0], res[1], list(res[2:])


def _attn_bwd_dq(q, k, v, dcat, lse_c, dl_c, *, tq=512, tk=4096, split=2, unroll=1):
    S = q.shape[0]
    tq, tk = min(tq, S), min(tk, S)
    nkv = S // tk
    hq = tq // split

    def body(q_ref, k_ref, v_ref, do_ref, lse_ref, dl_ref, dq_ref, do_s):
        dq_ref[...] = jnp.zeros(dq_ref.shape, F32)
        do_s[...] = do_ref[...].astype(do_s.dtype)

        def step(j, carry):
            rows = pl.ds(pl.multiple_of(j * tk, tk), tk)
            kj, vj = k_ref[rows, :], v_ref[rows, :]
            for a in range(split):
                r = pl.ds(a * hq, hq)
                p = jnp.exp2(_dot(q_ref[r, :], kj, "nt") - lse_ref[r, :])
                ds = p * (_dot(do_s[r, :], vj, "nt") - dl_ref[r, :])
                dq_ref[r, :] += _dot(ds, kj, "nn")
            return carry

        lax.fori_loop(0, nkv, step, 0, unroll=unroll if nkv % unroll == 0 else 1)
        dq_ref[...] = dq_ref[...] * _ATT_SCALE

    return _call(
        body, name="attn_bwd_dq", grid=(H_MLA, S // tq),
        in_specs=[pl.BlockSpec((tq, HP), lambda h, i: (i, h)),
                  pl.BlockSpec((S, HP), lambda h, i: (0, h)),
                  pl.BlockSpec((S, VD), lambda h, i: (0, h)),
                  pl.BlockSpec((tq, VD), lambda h, i: (i, h)),
                  pl.BlockSpec((None, tq, 1), lambda h, i: (h, i, 0)),
                  pl.BlockSpec((None, tq, 1), lambda h, i: (h, i, 0))],
        out_specs=pl.BlockSpec((tq, HP), lambda h, i: (i, h)),
        out_shape=jax.ShapeDtypeStruct((S, H_MLA * HP), F32),
        scratch_shapes=[pltpu.VMEM((tq, VD), _MXU_DTYPE)],
        compiler_params=_cparams(("parallel", "arbitrary")),
    )(q, k, v, dcat, lse_c, dl_c)


def _mlstm_chunk_terms(g, q, k, v, gates, gates_t, bg_row, C, n_row, m):
    L = LCH
    d = g // HM
    h = g % HM
    i_idx = d * 8 + h
    f_idx = d * 8 + 4 + h
    rr = lax.broadcasted_iota(jnp.int32, (L, L), 0)
    cc = lax.broadcasted_iota(jnp.int32, (L, L), 1)
    order = (rr - cc) * (1 - 2 * d)
    tri = order >= 0
    eye = rr == cc
    lane = lax.broadcasted_iota(jnp.int32, gates.shape, 1)
    sub = lax.broadcasted_iota(jnp.int32, gates_t.shape, 0)
    lane_b = lax.broadcasted_iota(jnp.int32, bg_row.shape, 1)
    pick_c = lambda idx: jnp.sum(jnp.where(lane == idx, gates, 0.0), axis=1, keepdims=True)
    pick_r = lambda idx: jnp.sum(jnp.where(sub == idx, gates_t, 0.0), axis=0, keepdims=True)
    pick_b = lambda idx: jnp.sum(jnp.where(lane_b == idx, bg_row, 0.0), axis=1, keepdims=True)
    i_col, i_row = pick_c(i_idx) + pick_b(i_idx), pick_r(i_idx) + pick_b(i_idx)
    f_col, f_row = pick_c(f_idx) + pick_b(f_idx), pick_r(f_idx) + pick_b(f_idx)
    logsig = lambda x: jnp.minimum(x, 0.0) - jnp.log(1.0 + jnp.exp(-jnp.abs(x)))
    lf_col, lf_row = logsig(f_col), logsig(f_row)
    b_col = jnp.sum(jnp.where(tri, lf_row, 0.0), axis=1, keepdims=True)
    tri_t = order <= 0
    b_row = jnp.sum(jnp.where(tri_t, lf_col, 0.0), axis=0, keepdims=True)
    bL = jnp.sum(lf_row, axis=1, keepdims=True)
    log_inter = b_col + m
    logD = jnp.where(tri, b_col - b_row + i_row, -jnp.inf)
    m_t = jnp.maximum(log_inter, jnp.max(logD, axis=1, keepdims=True))
    Dm = jnp.exp(logD - m_t)
    w_inter = jnp.exp(log_inter - m_t)
    A = _dot(q, k, "nt")
    Sc = A * Dm
    numI = _dot(q, C, "nt")
    qf = q.astype(F32)
    kf = k.astype(F32)
    denI = jnp.sum(qf * n_row, axis=1, keepdims=True)
    num = _dot(Sc, v, "nn") + w_inter * numI
    den = jnp.sum(Sc, axis=1, keepdims=True) + w_inter * denI
    floor = jnp.exp(-m_t)
    Nst = jnp.maximum(jnp.abs(den), floor)
    log_w = bL - b_col + i_col
    m_new = jnp.maximum(bL + m, jnp.max(log_w, axis=0, keepdims=True))
    decay = jnp.exp(bL + m - m_new)
    w_col = jnp.exp(log_w - m_new)
    return dict(tri=tri, eye=eye, f_row=f_row, Dm=Dm, w_inter=w_inter, A=A, Sc=Sc, numI=numI, denI=denI,
                num=num, den=den, floor=floor, Nst=Nst, m_new=m_new, decay=decay, w_col=w_col, qf=qf, kf=kf)


def _mlstm_specs(nc, d, step_of):
    chunk = lambda j: step_of(j) if d == 0 else nc - 1 - step_of(j)
    return chunk, [
        pl.BlockSpec((LCH, DM), lambda h, j: (chunk(j), h)),
        pl.BlockSpec((LCH, DM), lambda h, j: (chunk(j), h)),
        pl.BlockSpec((LCH, DM), lambda h, j: (chunk(j), P_VM // DM + h)),
        pl.BlockSpec((LCH, 128), lambda h, j: (chunk(j), P_G // 128)),
        pl.BlockSpec((NG, LCH), lambda h, j: (0, chunk(j))),
    ]


def _mlstm_fwd(qc, kc, proj, gates_t, bg_row):
    S = qc.shape[0]
    nc = S // LCH
    in_specs, out_specs = [], []
    for d in (0, 1):
        chunk, specs = _mlstm_specs(nc, d, lambda j: j)
        in_specs += specs
        out_specs += [pl.BlockSpec((LCH, DM), lambda h, j, chunk=chunk: (chunk(j), h)),
                      pl.BlockSpec((None, None, DM, DM), lambda h, j, chunk=chunk: (h, chunk(j), 0, 0)),
                      pl.BlockSpec((None, None, 8, DM), lambda h, j, chunk=chunk: (h, chunk(j), 0, 0))]
    in_specs.append(pl.BlockSpec((1, 128), lambda h, j: (0, 0)))

    def body(*refs):
        bg_ref, outs, (C_s, n_s, m_s) = refs[10], refs[11:17], refs[17:]

        @pl.when(pl.program_id(1) == 0)
        def _():
            C_s[...] = jnp.zeros(C_s.shape, F32)
            n_s[...] = jnp.zeros(n_s.shape, F32)
            m_s[...] = jnp.full(m_s.shape, M_INIT, F32)

        for d in (0, 1):
            q_ref, k_ref, v_ref, g_ref, gt_ref = refs[5 * d:5 * d + 5]
            h_ref, cst_ref, nm_ref = outs[3 * d:3 * d + 3]
            g = d * HM + pl.program_id(0)
            C, n_row, m = C_s[d], n_s[d, 0:1, :], m_s[d, 0:1, 0:1]
            cst_ref[...] = C
            nm_ref[0:1, :] = n_row
            nm_ref[1:2, :] = jnp.broadcast_to(m, (1, DM))
            nm_ref[2:8, :] = jnp.zeros((6, DM), F32)
            q, k, v = q_ref[...], k_ref[...], v_ref[...]
            t = _mlstm_chunk_terms(g, q, k, v, g_ref[...], gt_ref[...], bg_ref[...], C, n_row, m)
            h_ref[...] = t["num"] / t["Nst"]
            wv = t["w_col"] * v
            C_s[d] = t["decay"] * C + _dot(wv, k, "tn")
            n_s[d, 0:1, :] = t["decay"] * n_row + _colsum(t["w_col"] * t["kf"])
            m_s[d] = jnp.broadcast_to(t["m_new"], (8, 128))

    res = _call(
        body, name="mlstm_fwd", grid=(HM, nc), in_specs=in_specs, out_specs=out_specs,
        out_shape=[jax.ShapeDtypeStruct((S, MW), F32), jax.ShapeDtypeStruct((HM, nc, DM, DM), F32),
                   jax.ShapeDtypeStruct((HM, nc, 8, DM), F32)] * 2,
        scratch_shapes=[pltpu.VMEM((2, DM, DM), F32), pltpu.VMEM((2, 8, DM), F32), pltpu.VMEM((2, 8, 128), F32)],
        compiler_params=_cparams(("parallel", "arbitrary")),
    )(*([qc, kc, proj, proj, gates_t] * 2), bg_row)
    return (res[0], res[3]), (res[1], res[4]), (res[2], res[5])


def _mlstm_bwd(qc, kc, proj, gates_t, bg_row, dh, cst, nm):
    S = qc.shape[0]
    nc = S // LCH
    in_specs, out_specs = [], []
    for d in (0, 1):
        chunk, specs = _mlstm_specs(nc, d, lambda j: nc - 1 - j)
        in_specs += specs + [pl.BlockSpec((LCH, DM), lambda h, j, chunk=chunk: (chunk(j), h)),
                             pl.BlockSpec((None, None, DM, DM), lambda h, j, chunk=chunk: (h, chunk(j), 0, 0)),
                             pl.BlockSpec((None, None, 8, DM), lambda h, j, chunk=chunk: (h, chunk(j), 0, 0))]
        out_specs += [pl.BlockSpec((LCH, DM), lambda h, j, chunk=chunk: (chunk(j), h))] * 3
        out_specs += [pl.BlockSpec((None, None, 8, LCH), lambda h, j, chunk=chunk: (h, chunk(j), 0, 0))]
    in_specs.append(pl.BlockSpec((1, 128), lambda h, j: (0, 0)))

    def body(*refs):
        bg_ref, outs, (dC_s, dn_s) = refs[16], refs[17:25], refs[25:]

        @pl.when(pl.program_id(1) == 0)
        def _():
            dC_s[...] = jnp.zeros(dC_s.shape, F32)
            dn_s[...] = jnp.zeros(dn_s.shape, F32)

        for d in (0, 1):
            _mlstm_bwd_chain(d, refs[8 * d:8 * d + 8], bg_ref, outs[4 * d:4 * d + 4], dC_s, dn_s)

    res = _call(
        body, name="mlstm_bwd", grid=(HM, nc), in_specs=in_specs, out_specs=out_specs,
        out_shape=([jax.ShapeDtypeStruct((S, MW), F32)] * 3 + [jax.ShapeDtypeStruct((HM, nc, 8, LCH), F32)]) * 2,
        scratch_shapes=[pltpu.VMEM((2, DM, DM), F32), pltpu.VMEM((2, 8, DM), F32)],
        compiler_params=_cparams(("parallel", "arbitrary")),
    )(*[a for d in (0, 1) for a in (qc, kc, proj, proj, gates_t, dh, cst[d], nm[d])], bg_row)
    return (res[0], res[4]), (res[1], res[5]), (res[2], res[6]), (res[3], res[7])


def _mlstm_bwd_chain(d, ins, bg_ref, outs, dC_s, dn_s):
        q_ref, k_ref, v_ref, g_ref, gt_ref, dh_ref, cst_ref, nm_ref = ins
        dq_ref, dk_ref, dv_ref, dg_ref = outs
        g = d * HM + pl.program_id(0)
        C, n_row, m = cst_ref[...], nm_ref[0:1, :], nm_ref[1:2, 0:1]
        q, k, v = q_ref[...], k_ref[...], v_ref[...]
        t = _mlstm_chunk_terms(g, q, k, v, g_ref[...], gt_ref[...], bg_ref[...], C, n_row, m)
        tri, eye, qf, kf = t["tri"], t["eye"], t["qf"], t["kf"]
        w_inter, w_col, decay, Nst = t["w_inter"], t["w_col"], t["decay"], t["Nst"]
        dC, dn = dC_s[d], dn_s[d, 0:1, :]
        dhv = dh_ref[...]
        hval = t["num"] / Nst
        dnum = dhv / Nst
        dNst = -jnp.sum(dhv * hval, axis=1, keepdims=True) / Nst
        dden = jnp.where(jnp.abs(t["den"]) > t["floor"], jnp.sign(t["den"]) * dNst, 0.0)
        dSc = _dot(dnum, v, "nt") + dden
        dA = dSc * t["Dm"]
        G = dSc * t["Sc"]
        KdC = _dot(k, dC, "nt")
        dq = _dot(dA, k, "nn") + w_inter * _dot(dnum, C, "nn") + (w_inter * dden) * n_row
        dk = _dot(dA, q, "tn") + w_col * _dot(v, dC, "nn") + w_col * dn
        dv = _dot(t["Sc"], dnum, "tn") + w_col * KdC
        dq_ref[...] = dq
        dk_ref[...] = dk
        dv_ref[...] = dv
        dlog_inter = w_inter * (jnp.sum(dnum * t["numI"], axis=1, keepdims=True) + dden * t["denI"])
        rowG = jnp.sum(G, axis=1, keepdims=True)
        colG = jnp.sum(G, axis=0, keepdims=True)
        u_col = w_col * (jnp.sum(v * KdC, axis=1, keepdims=True) + jnp.sum(kf * dn, axis=1, keepdims=True))
        colG_c = jnp.sum(jnp.where(eye, colG, 0.0), axis=1, keepdims=True)
        u_row = jnp.sum(jnp.where(eye, u_col, 0.0), axis=0, keepdims=True)
        db_col = rowG + dlog_inter - u_col - colG_c
        dbL = jnp.sum(u_col, axis=0, keepdims=True) + decay * (
            jnp.sum(jnp.sum(dC * C, axis=1, keepdims=True), axis=0, keepdims=True)
            + jnp.sum(dn * n_row, axis=1, keepdims=True))
        dlf_row = jnp.sum(jnp.where(tri, db_col, 0.0), axis=0, keepdims=True) + dbL
        di_row = colG + u_row
        df_row = dlf_row * (1.0 - _sigmoid(t["f_row"]))
        dg_ref[...] = jnp.zeros(dg_ref.shape, F32)
        dg_ref[0:1, :] = di_row
        dg_ref[1:2, :] = df_row
        dC_s[d] = decay * dC + _dot(w_inter * dnum, q, "tn")
        dn_s[d, 0:1, :] = decay * dn + _colsum((w_inter * dden) * qf)


def _pad_w_in(w):
    cq, ckv, kpe, qm, km, vm, om, gt = _split_in(w)
    z = lambda n: jnp.zeros((w.shape[0], n), w.dtype)
    return jnp.concatenate([qm, km, vm, om, cq, ckv, kpe, z(HP - QK), gt, z(128 - NG)], axis=1)


def _split_in(w):
    out, o = [], 0
    for n in IN_SIZES:
        out.append(w[:, o:o + n])
        o += n
    return out


def _unpad_w_in(g):
    return jnp.concatenate([g[:, P_CQ:P_CQ + Q_LORA], g[:, P_CKV:P_CKV + KV_LORA], g[:, P_KPE:P_KPE + ROPE],
                            g[:, 0:4 * MW], g[:, P_G:P_G + NG]], axis=1)


def _pad_w_uq(w):
    return jnp.pad(w.reshape(Q_LORA, H_MLA, QK), ((0, 0), (0, 0), (0, HP - QK))).reshape(Q_LORA, H_MLA * HP)


def _unpad_w_uq(g):
    return g.reshape(Q_LORA, H_MLA, HP)[:, :, :QK].reshape(Q_LORA, H_MLA * QK)


def _perm_w_ukv(w):
    return w.reshape(KV_LORA, H_MLA, 2, NOPE).transpose(0, 2, 1, 3).reshape(KV_LORA, 2 * H_MLA * NOPE)


def _unperm_w_ukv(g):
    return g.reshape(KV_LORA, 2, H_MLA, NOPE).transpose(0, 2, 1, 3).reshape(KV_LORA, 2 * H_MLA * NOPE)


def _rope_tables(positions):
    half = ROPE // 2
    freqs = ROPE_THETA ** (-jnp.arange(half, dtype=F32) / half)
    ang = positions.astype(F32)[:, None] * freqs
    cos, sin = jnp.cos(ang), jnp.sin(ang)
    z32, z64 = jnp.zeros_like(cos), jnp.zeros((cos.shape[0], 64), F32)
    return (jnp.concatenate([cos, cos, z64], axis=1), jnp.concatenate([z32, sin, z64], axis=1),
            jnp.concatenate([-sin, z32, z64], axis=1))


def _device_step(x, tgt, positions, modv, W, late=None):
    S = x.shape[0]
    MX = _MXU_DTYPE
    cosp, rs1, rs2 = _rope_tables(positions)
    tabs = [(cosp, 128, 0), (rs1, 128, 0), (rs2, 128, 0)]
    cat1 = lambda vs: jnp.concatenate(vs, axis=1)
    hsl = lambda hh, w: slice(hh * w, (hh + 1) * w)

    def ln1(xv, g, mv):
        xhat, _ = _rms(xv, D)
        return [xhat * g * (1.0 + mv[1:2]) + mv[0:1]], []

    (h,) = _rowmap(ln1, [(x, D, 0)], [W["g_mix"], modv], [(D, MX)], tile=128, name="ln1")
    proj = _mm(h, W["w_in"], "nn", name="proj")

    def lora(cq, ckv, gq, gkv):
        return [_rms(cq, Q_LORA)[0] * gq, _rms(ckv, KV_LORA)[0] * gkv], []

    cqn, ckvn = _rowmap(lora, [(proj, Q_LORA, P_CQ // Q_LORA), (proj, KV_LORA, P_CKV // KV_LORA)],
                        [W["g_qlora"], W["g_kvlora"]], [(Q_LORA, MX), (KV_LORA, MX)], tile=256, name="lora_norm")
    q_raw = _mm(cqn, W["w_uq"], "nn", name="q_up")
    kv_raw = _mm(ckvn, W["w_ukv"], "nn", name="kv_up")

    def mla_q(qr, cp, a1, a2, gq):
        outs = []
        for hh in range(H_MLA):
            y = _rms(qr[:, hsl(hh, HP)], QK)[0] * gq
            outs += [y[:, :NOPE], _rope_fwd(y[:, NOPE:], cp, a1, a2)]
        return [cat1(outs) * _Q_PRESCALE], []

    (qh,) = _rowmap(mla_q, [(q_raw, H_MLA * HP, 0)] + tabs, [W["gq"]], [(H_MLA * HP, MX)], tile=128, name="mla_q")

    def mla_k(kvr, kpe, cp, a1, a2, gk):
        outs = []
        for hh in range(H_MLA):
            y = _rms(cat1([kvr[:, hsl(hh, NOPE)], kpe]), QK)[0] * gk
            outs += [y[:, :NOPE], _rope_fwd(y[:, NOPE:], cp, a1, a2)]
        return [cat1(outs), kvr[:, H_MLA * NOPE:]], []

    kh, vh = _rowmap(mla_k, [(kv_raw, 2 * H_MLA * NOPE, 0), (proj, 128, P_KPE // 128)] + tabs, [W["gk"]],
                     [(H_MLA * HP, MX), (H_MLA * VD, MX)], tile=128, name="mla_k")
    attn_o, lse, gathered = _attn_fwd(qh, kh, vh, side=late or ())
    if late:
        W = dict(W, w_out=gathered[0].reshape(D, D), w_ff1=_cols(gathered[1]), w_ff2=gathered[2].reshape(DFF, D))

    qc, kc = _conv_fwd(proj, W["conv_w8"], W["conv_b"])
    gates_t = proj[:, P_G:P_G + NG].T
    (h_f, h_b), cst, nm = _mlstm_fwd(qc, kc, proj, gates_t, W["bg_row"])
    hrows = [(h_f, MW, 0), (h_b, MW, 0), (proj, MW, P_OM // MW)]

    def ml_out(ao, hf, hb, om, gmn):
        outs = [ao.astype(F32)]
        hs = hf + hb
        for hh in range(HM):
            sl = hsl(hh, DM)
            outs.append(_sigmoid(om[:, sl]) * _rms(hs[:, sl], DM)[0] * gmn[:, sl])
        return [cat1(outs)], []

    (cat,) = _rowmap(ml_out, [(attn_o, MW, 0)] + hrows, [W["g_mn"]], [(D, MX)], tile=128, name="ml_out")
    mixed = _mm(cat, W["w_out"], "nn", name="out_proj")

    def res_ln2(xv, mx, g, mv):
        x1 = xv + mv[2:3] * mx
        return [x1, _rms(x1, D)[0] * g * (1.0 + mv[4:5]) + mv[3:4]], []

    x1, h2 = _rowmap(res_ln2, [(x, D, 0), (mixed, D, 0)], [W["g_mlp"], modv], [(D, F32), (D, MX)],
                     tile=128, name="res_ln2")
    a, u = _mm(h2, W["w_ff1"], "nn", name="ff1", out_dtypes=(MX, MX),
               epilogue=lambda r: (jnp.square(jnp.maximum(r, 0.0)), r))
    y = _mm(a, W["w_ff2"], "nn", name="ff2")

    def final(x1v, yv, tv, mv):
        err = x1v + mv[5:6] * yv - tv
        dout = err * (1.0 / D)
        loss = jnp.sum(jnp.sum(0.5 * err * dout, axis=1, keepdims=True), axis=0, keepdims=True)
        return [dout, mv[5:6] * dout], [loss, _colsum(dout * yv)]

    dout, dy, loss, dgate2 = _rowmap(final, [(x1, D, 0), (y, D, 0), (tgt, D, 0)], [modv], [(D, F32), (D, MX)],
                                     [(1, 1), (1, D)], tile=128, name="loss_head")

    du = _mm(dy, W["w_ff2"], "nt", name="ff2_dx", out_dtypes=(MX,), extras=(u,),
             epilogue=lambda r, uv: (r * (2.0 * jnp.maximum(uv.astype(F32), 0.0)),))
    g_ff2 = _mm(a, dy, "tn", name="ff2_dw")
    dh2 = _mm(du, W["w_ff1"], "nt", name="ff1_dx")
    g_ff1 = _mm(h2, du, "tn", name="ff1_dw")

    def ln2_bwd(dh2v, x1v, doutv, mxv, g, mv):
        xhat, r = _rms(x1v, D)
        dn2 = dh2v * (1.0 + mv[4:5])
        dx1 = doutv + _rms_bwd(dn2 * g, xhat, r, D)
        return [dx1, mv[2:3] * dx1], [_colsum(dh2v), _colsum(dh2v * xhat * g), _colsum(dn2 * xhat), _colsum(dx1 * mxv)]

    dx1, dmixed, dshift2, dscale2, dg_mlp, dgate1 = _rowmap(
        ln2_bwd, [(dh2, D, 0), (x1, D, 0), (dout, D, 0), (mixed, D, 0)], [W["g_mlp"], modv],
        [(D, F32), (D, MX)], [(1, D)] * 4, tile=128, name="ln2_bwd")
    dcat = _mm(dmixed, W["w_out"], "nt", name="out_dx")
    g_out = _mm(cat, dmixed, "tn", name="out_dw")

    def ml_out_bwd(dml, hf, hb, om, gmn):
        hs = hf + hb
        dhs, dos, dgs = [], [], []
        for hh in range(HM):
            sl = hsl(hh, DM)
            xhat, r = _rms(hs[:, sl], DM)
            g, sg, d = gmn[:, sl], _sigmoid(om[:, sl]), dml[:, sl]
            dos.append(d * xhat * g * sg * (1.0 - sg))
            dhn = d * sg
            dgs.append(_colsum(dhn * xhat))
            dhs.append(_rms_bwd(dhn * g, xhat, r, DM))
        return [cat1(dhs), cat1(dos)], [cat1(dgs)]

    dhs, do_m, dg_mn = _rowmap(ml_out_bwd, [(dcat, MW, 1)] + hrows, [W["g_mn"]], [(MW, F32), (MW, MX)],
                               [(1, MW)], tile=128, name="ml_out_bwd")
    dqd, dkd, dvd, dgates = _mlstm_bwd(qc, kc, proj, gates_t, W["bg_row"], dhs, cst, nm)
    dqk_m, dconv_w8, dconv_b = _conv_bwd(proj, dqd, dkd, W["conv_w8"], W["conv_b"])

    def delta_fn(ao, dov):
        lane = lax.broadcasted_iota(jnp.int32, (ao.shape[0], 128), 1)
        acc = jnp.zeros((ao.shape[0], 128), F32)
        for hh in range(H_MLA):
            sl = hsl(hh, VD)
            acc = acc + jnp.where(lane == hh, jnp.sum(ao[:, sl].astype(F32) * dov[:, sl], axis=1, keepdims=True), 0.0)
        return [acc], []

    (dl,) = _rowmap(delta_fn, [(attn_o, MW, 0), (dcat, MW, 0)], [], [(128, F32)], tile=256, name="attn_delta")
    dl_hs = dl[:, :H_MLA].T
    side = [g_out.reshape(N_CHIP, D // N_CHIP, D).astype(MX), _slabs(g_ff1).astype(MX),
            g_ff2.reshape(N_CHIP, DFF // N_CHIP, D).astype(MX)] if late else ()
    dk_a, dv_a, late_got = _attn_bwd_dkv(qh, kh, vh, dcat, lse.reshape(H_MLA, S), dl_hs, side=side)
    dq_a = _attn_bwd_dq(qh, kh, vh, dcat, lse, dl_hs.reshape(H_MLA, S, 1))

    def mla_q_bwd(dqv, qr, cp, a1, a2, gq):
        outs, dg = [], 0.0
        for hh in range(H_MLA):
            sl = hsl(hh, HP)
            xhat, r = _rms(qr[:, sl], QK)
            d = dqv[:, sl]
            dyv = cat1([d[:, :NOPE], _rope_bwd(d[:, NOPE:], cp, a1, a2)])
            dg = dg + _colsum(dyv * xhat)
            outs.append(_rms_bwd(dyv * gq, xhat, r, QK))
        return [cat1(outs)], [dg]

    dq_raw, dgq = _rowmap(mla_q_bwd, [(dq_a, H_MLA * HP, 0), (q_raw, H_MLA * HP, 0)] + tabs, [W["gq"]],
                          [(H_MLA * HP, MX)], [(1, HP)], tile=128, name="mla_q_bwd")
    dcqn = _mm(dq_raw, W["w_uq"], "nt", name="q_up_dx")
    g_uq = _mm(cqn, dq_raw, "tn", name="q_up_dw")

    def mla_k_bwd(dkv, dvv, kvr, kpe, cp, a1, a2, gk):
        dkn, dg, dkpe = [], 0.0, 0.0
        for hh in range(H_MLA):
            xhat, r = _rms(cat1([kvr[:, hsl(hh, NOPE)], kpe]), QK)
            d = dkv[:, hsl(hh, HP)]
            dyv = cat1([d[:, :NOPE], _rope_bwd(d[:, NOPE:], cp, a1, a2)])
            dg = dg + _colsum(dyv * xhat)
            dxv = _rms_bwd(dyv * gk, xhat, r, QK)
            dkn.append(dxv[:, :NOPE])
            dkpe = dkpe + dxv[:, NOPE:]
        return [cat1(dkn + [dvv]), dkpe], [dg]

    dkv_raw, dkpe, dgk = _rowmap(
        mla_k_bwd, [(dk_a, H_MLA * HP, 0), (dv_a, H_MLA * VD, 0), (kv_raw, 2 * H_MLA * NOPE, 0),
                    (proj, 128, P_KPE // 128)] + tabs, [W["gk"]],
        [(2 * H_MLA * NOPE, MX), (128, MX)], [(1, HP)], tile=128, name="mla_k_bwd")
    dckvn = _mm(dkv_raw, W["w_ukv"], "nt", name="kv_up_dx")
    g_ukv = _mm(ckvn, dkv_raw, "tn", name="kv_up_dw")

    def lora_bwd(dcq, dckv, cq, ckv, gq, gkv):
        xq, rq = _rms(cq, Q_LORA)
        xk, rk = _rms(ckv, KV_LORA)
        return ([_rms_bwd(dcq * gq, xq, rq, Q_LORA), _rms_bwd(dckv * gkv, xk, rk, KV_LORA)],
                [_colsum(dcq * xq), _colsum(dckv * xk)])

    dc_q, dc_kv, dg_qlora, dg_kvlora = _rowmap(
        lora_bwd, [(dcqn, Q_LORA, 0), (dckvn, KV_LORA, 0), (proj, Q_LORA, P_CQ // Q_LORA),
                   (proj, KV_LORA, P_CKV // KV_LORA)], [W["g_qlora"], W["g_kvlora"]],
        [(Q_LORA, MX), (KV_LORA, MX)], [(1, Q_LORA), (1, KV_LORA)], tile=256, name="lora_bwd")

    nc = S // LCH
    dg16 = jnp.stack(dgates)[:, :, :, 0:2, :].transpose(2, 4, 0, 3, 1).reshape(S, NG)
    dg128 = jnp.pad(dg16, ((0, 0), (0, 128 - NG)))

    def assemble(dqk, dv0, dv1, dom, dcq, dckv, dkp, dgp):
        f = lambda t: t.astype(F32)
        return [cat1([f(dqk), dv0 + dv1, f(dom), f(dcq), f(dckv), f(dkp), dgp])], [_colsum(dgp)]

    dproj, dbg = _rowmap(
        assemble, [(dqk_m, 2 * MW, 0), (dvd[0], MW, 0), (dvd[1], MW, 0), (do_m, MW, 0), (dc_q, Q_LORA, 0),
                   (dc_kv, KV_LORA, 0), (dkpe, 128, 0), (dg128, 128, 0)], [], [(D_INP, MX)], [(1, 128)],
        tile=128, name="dproj")
    g_in = _mm(h, dproj, "tn", name="proj_dw")
    early_got = ()
    if late:
        side = [_slabs(_unpad_w_in(g_in)).astype(MX), _slabs(_unpad_w_uq(g_uq)).astype(MX),
                _slabs(_unperm_w_ukv(g_ukv)).astype(MX)]
        dh, early_got = _mm(dproj, W["w_in"], "nt", name="proj_dx", side=side)
    else:
        dh = _mm(dproj, W["w_in"], "nt", name="proj_dx")

    def ln1_bwd(dhv, xv, dx1v, g, mv):
        xhat, r = _rms(xv, D)
        dn = dhv * (1.0 + mv[1:2])
        return [dx1v + _rms_bwd(dn * g, xhat, r, D)], [_colsum(dhv), _colsum(dhv * xhat * g), _colsum(dn * xhat)]

    gx, dshift1, dscale1, dg_mix = _rowmap(ln1_bwd, [(dh, D, 0), (x, D, 0), (dx1, D, 0)], [W["g_mix"], modv],
                                           [(D, F32)], [(1, D)] * 3, tile=128, name="ln1_bwd")
    dmodv = jnp.concatenate([dshift1, dscale1, dgate1, dshift2, dscale2, dgate2], axis=0)
    grads = dict(w_in=g_in, w_uq=g_uq, w_ukv=g_ukv, w_out=g_out, w_ff1=g_ff1, w_ff2=g_ff2,
                 norm_mix_g=dg_mix, b_gates=dbg[:, :NG], conv_w=dconv_w8[:CONVW], conv_b=dconv_b,
                 q_lora_g=dg_qlora, kv_lora_g=dg_kvlora, q_norm_g=dgq[:, :QK], k_norm_g=dgk[:, :QK],
                 mlstm_norm_g=dg_mn, norm_mlp_g=dg_mlp)
    grads["got"] = list(early_got) + list(late_got)
    return loss, gx, dmodv, grads


def _cols(g):
    return g.transpose(1, 0, 2).reshape(g.shape[1], N_CHIP * g.shape[2])


def _slabs(gfull):
    return gfull.reshape(gfull.shape[0], N_CHIP, -1).transpose(1, 0, 2)


def _prep_weights(w_in, w_uq, w_ukv, w_out, w_ff1, w_ff2, norm_mix_g, norm_mlp_g, q_lora_g, kv_lora_g,
                  q_norm_g, k_norm_g, mlstm_norm_g, conv_w, conv_b, b_gates):
    MX = _MXU_DTYPE
    padg = lambda g: jnp.pad(g.reshape(1, QK).astype(F32), ((0, 0), (0, HP - QK)))
    return dict(
        w_in=_pad_w_in(w_in).astype(MX), w_uq=_pad_w_uq(w_uq).astype(MX), w_ukv=_perm_w_ukv(w_ukv).astype(MX),
        w_out=None if w_out is None else w_out.astype(MX), w_ff1=None if w_ff1 is None else w_ff1.astype(MX),
        w_ff2=None if w_ff2 is None else w_ff2.astype(MX),
        g_mix=norm_mix_g.reshape(1, D), g_mlp=norm_mlp_g.reshape(1, D), g_qlora=q_lora_g.reshape(1, Q_LORA),
        g_kvlora=kv_lora_g.reshape(1, KV_LORA), gq=padg(q_norm_g), gk=padg(k_norm_g),
        g_mn=mlstm_norm_g.reshape(1, MW), conv_w8=jnp.pad(conv_w.reshape(CONVW, 2 * MW), ((0, 8 - CONVW), (0, 0))),
        conv_b=conv_b.reshape(1, 2 * MW), bg_row=jnp.pad(b_gates.reshape(1, NG), ((0, 0), (0, 128 - NG))))


MESH = pl.DeviceIdType.MESH
N_DEV = 8
N_CHIP = 4


def _comm_call(body, **kw):
    if _INTERPRET:
        kw["interpret"] = pltpu.InterpretParams()
    return pl.pallas_call(body, **kw)


def _allgather8(blk, *, name):
    m_per, n = blk.shape

    def body(x_ref, out_ref, send_sems, recv_sems, local_sem):
        x, y, c = lax.axis_index("x"), lax.axis_index("y"), lax.axis_index("c")
        me, sibling = (x, y, c), (x, y, 1 - c)
        chips = [(1 - x, y), (x, 1 - y), (1 - x, 1 - y)]

        def rows(px, py, pc):
            return out_ref.at[pl.ds((4 * px + 2 * py + pc) * m_per, m_per), :]

        def copy(k, block, to, src=None):
            return pltpu.make_async_remote_copy(
                src_ref=rows(*block) if src is None else src, dst_ref=rows(*block),
                send_sem=send_sems.at[k], recv_sem=recv_sems.at[k], device_id=to, device_id_type=MESH)

        mine = pltpu.make_async_copy(x_ref, rows(*me), local_sem)
        mine.start()
        first = [copy(0, me, sibling, src=x_ref)]
        first += [copy(1 + j, me, (*chip, c), src=x_ref) for j, chip in enumerate(chips)]
        for cp in first:
            cp.start()
        passed = [copy(4 + j, (*chip, c), sibling) for j, chip in enumerate(chips)]
        for j, chip in enumerate(chips):
            copy(1 + j, (*chip, c), me).wait_recv()
            passed[j].start()
        copy(0, sibling, me).wait_recv()
        for j, chip in enumerate(chips):
            copy(4 + j, (*chip, 1 - c), me).wait_recv()
        for cp in first + passed:
            cp.wait_send()
        mine.wait()

    return _comm_call(
        body, name=name, out_shape=jax.ShapeDtypeStruct((N_DEV * m_per, n), blk.dtype),
        in_specs=[pl.BlockSpec(memory_space=pltpu.VMEM)], out_specs=pl.BlockSpec(memory_space=pltpu.VMEM),
        scratch_shapes=[pltpu.SemaphoreType.DMA((7,)), pltpu.SemaphoreType.DMA((7,)), pltpu.SemaphoreType.DMA],
    )(blk)


def _chip_exchange(arrays, *, gather, name):
    n = len(arrays)

    def body(*refs):
        start, wait = _exchange_ops(refs[:n], refs[n:2 * n], *refs[2 * n:], gather=gather)
        start()
        wait()

    io = _exchange_io(arrays, gather)
    return _comm_call(body, name=name, out_shape=io["out_shape"], in_specs=io["specs"], out_specs=io["specs"],
                      scratch_shapes=io["scratch"])(*arrays)


def _exchange_io(arrays, gather):
    n = len(arrays)
    shard = (lambda a: a.shape) if gather else (lambda a: a.shape[1:])
    return dict(
        specs=[pl.BlockSpec(memory_space=pltpu.HBM)] * n,
        out_shape=[jax.ShapeDtypeStruct((N_CHIP, *shard(a)), a.dtype) for a in arrays],
        scratch=[pltpu.SemaphoreType.DMA((3 * n,)), pltpu.SemaphoreType.DMA((3 * n,)), pltpu.SemaphoreType.DMA((n,))])


def _exchange_ops(ins, outs, send_sems, recv_sems, local_sems, *, gather):
    n = len(ins)
    x, y, c = lax.axis_index("x"), lax.axis_index("y"), lax.axis_index("c")
    k = 2 * x + y
    chips = [(1 - x, y), (x, 1 - y), (1 - x, 1 - y)]

    def remote(a, j):
        px, py = chips[j]
        src = ins[a] if gather else ins[a].at[2 * px + py]
        return pltpu.make_async_remote_copy(
            src_ref=src, dst_ref=outs[a].at[k], send_sem=send_sems.at[3 * a + j],
            recv_sem=recv_sems.at[3 * a + j], device_id=(px, py, c), device_id_type=MESH)

    def arrival(a, j):
        px, py = chips[j]
        src = ins[a] if gather else ins[a].at[k]
        return pltpu.make_async_remote_copy(
            src_ref=src, dst_ref=outs[a].at[2 * px + py], send_sem=send_sems.at[3 * a + j],
            recv_sem=recv_sems.at[3 * a + j], device_id=(px, py, c), device_id_type=MESH)

    local = [pltpu.make_async_copy(ins[a] if gather else ins[a].at[k], outs[a].at[k], local_sems.at[a])
             for a in range(n)]
    sent = [remote(a, j) for a in range(n) for j in range(3)]

    def start():
        for cp in local + sent:
            cp.start()

    def wait():
        for a in range(n):
            for j in range(3):
                arrival(a, j).wait_recv()
        for cp in sent:
            cp.wait_send()
        for cp in local:
            cp.wait()

    return start, wait


def _chip_allgather_halved(shards, *, name):
    n = len(shards)
    half_rows = [s.shape[0] // 2 for s in shards]
    assert all(s.shape[0] % 16 == 0 for s in shards)

    def body(*refs):
        ins, outs = refs[:n], refs[n:2 * n]
        ici_send, ici_recv, d2d_send, d2d_recv, local_sems = refs[2 * n:]
        x, y, c = lax.axis_index("x"), lax.axis_index("y"), lax.axis_index("c")
        k = 2 * x + y
        chips = [(1 - x, y), (x, 1 - y), (1 - x, 1 - y)]

        def half(a, slab, core):
            return outs[a].at[slab, pl.ds(core * half_rows[a], half_rows[a])]

        def ici(a, j, slab):
            px, py = chips[j]
            return pltpu.make_async_remote_copy(
                src_ref=ins[a].at[pl.ds(c * half_rows[a], half_rows[a])], dst_ref=half(a, slab, c),
                send_sem=ici_send.at[3 * a + j], recv_sem=ici_recv.at[3 * a + j],
                device_id=(px, py, c), device_id_type=MESH)

        def d2d(a, j, core):
            px, py = chips[j]
            return pltpu.make_async_remote_copy(
                src_ref=half(a, 2 * px + py, core), dst_ref=half(a, 2 * px + py, core),
                send_sem=d2d_send.at[3 * a + j], recv_sem=d2d_recv.at[3 * a + j],
                device_id=(x, y, 1 - c), device_id_type=MESH)

        local = [pltpu.make_async_copy(ins[a], outs[a].at[k], local_sems.at[a]) for a in range(n)]
        sent = [ici(a, j, k) for a in range(n) for j in range(3)]
        for cp in local + sent:
            cp.start()
        passed = []
        for a in range(n):
            for j, (px, py) in enumerate(chips):
                ici(a, j, 2 * px + py).wait_recv()
                passed.append(d2d(a, j, c))
                passed[-1].start()
        for a in range(n):
            for j in range(3):
                d2d(a, j, 1 - c).wait_recv()
        for cp in sent + passed:
            cp.wait_send()
        for cp in local:
            cp.wait()

    hbm = pl.BlockSpec(memory_space=pltpu.HBM)
    return _comm_call(
        body, name=name, out_shape=[jax.ShapeDtypeStruct((N_CHIP, *s.shape), s.dtype) for s in shards],
        in_specs=[hbm] * n, out_specs=[hbm] * n,
        scratch_shapes=[pltpu.SemaphoreType.DMA((3 * n,))] * 4 + [pltpu.SemaphoreType.DMA((n,))],
    )(*shards)


def _sibling_exchange(arrays, *, name):
    n = len(arrays)

    def body(*refs):
        ins, outs = refs[:n], refs[n:2 * n]
        send_sems, recv_sems = refs[2 * n:]
        x, y, c = lax.axis_index("x"), lax.axis_index("y"), lax.axis_index("c")
        cps = [pltpu.make_async_remote_copy(
            src_ref=ins[a], dst_ref=outs[a], send_sem=send_sems.at[a], recv_sem=recv_sems.at[a],
            device_id=(x, y, 1 - c), device_id_type=MESH) for a in range(n)]
        for cp in cps:
            cp.start()
        for cp in cps:
            cp.wait()

    hbm = pl.BlockSpec(memory_space=pltpu.HBM)
    return _comm_call(
        body, name=name, out_shape=[jax.ShapeDtypeStruct(a.shape, a.dtype) for a in arrays],
        in_specs=[hbm] * n, out_specs=[hbm] * n,
        scratch_shapes=[pltpu.SemaphoreType.DMA((n,)), pltpu.SemaphoreType.DMA((n,))],
    )(*arrays)


def _sum_blocks(a, nblk, *, name):
    n = a.shape[1]

    def body(a_ref, o_ref):
        acc = a_ref[pl.ds(0, 8), :]
        for d in range(1, nblk):
            acc = acc + a_ref[pl.ds(8 * d, 8), :]
        o_ref[...] = acc

    return _call(body, name=name, out_shape=jax.ShapeDtypeStruct((8, n), F32))(a)


def _outer8(sct, dm, *, name, tm=256, tn=1024):
    R, N = sct.shape[0], dm.shape[1]
    tm, tn = min(tm, R), min(tn, N)

    def body(s_ref, d_ref, o_ref):
        s, dmv = s_ref[...], d_ref[...]
        acc = s[:, 0:1] * dmv[0:1, :]
        for b in range(1, 8):
            acc = acc + s[:, b:b + 1] * dmv[b:b + 1, :]
        o_ref[...] = acc

    return _call(
        body, name=name, grid=(R // tm, N // tn),
        in_specs=[pl.BlockSpec((tm, 8), lambda i, j: (i, 0)), pl.BlockSpec((8, tn), lambda i, j: (0, j))],
        out_specs=pl.BlockSpec((tm, tn), lambda i, j: (i, j)),
        out_shape=jax.ShapeDtypeStruct((R, N), F32),
        compiler_params=_cparams(("parallel", "parallel")),
    )(sct, dm)


_BC1 = 1.0 - ADAM_B1 ** ADAM_STEP
_BC2 = 1.0 - ADAM_B2 ** ADAM_STEP


def _adamw(w, g_parts, m, v, *, name, tile=128):
    R, C = w.shape
    tile = min(tile, R)
    assert R % tile == 0
    npart = len(g_parts)

    def body(*refs):
        w_ref, m_ref, v_ref = refs[npart:npart + 3]
        g_o, d_o, m_o, v_o = refs[npart + 3:]
        g = refs[0][...].astype(F32)
        for r in refs[1:npart]:
            g = g + r[...].astype(F32)
        mn = ADAM_B1 * m_ref[...] + (1.0 - ADAM_B1) * g
        vn = ADAM_B2 * v_ref[...] + (1.0 - ADAM_B2) * jnp.square(g)
        g_o[...] = g
        m_o[...] = mn
        v_o[...] = vn
        d_o[...] = -ADAM_LR * ((mn / _BC1) / (jnp.sqrt(vn / _BC2) + ADAM_EPS) + ADAM_WD * w_ref[...])

    spec = pl.BlockSpec((tile, C), lambda i: (i, 0))
    return _call(
        body, name=name, grid=(R // tile,), in_specs=[spec] * (npart + 3), out_specs=[spec] * 4,
        out_shape=[jax.ShapeDtypeStruct((R, C), F32)] * 4,
        compiler_params=_cparams(("parallel",)),
    )(*g_parts, w, m, v)


def _pack(vecs, rows8_cols):
    flat = jnp.concatenate([v.reshape(-1).astype(F32) for v in vecs])
    return jnp.pad(flat, (0, 8 * rows8_cols - flat.shape[0])).reshape(8, rows8_cols)


def _unpack(flat, shapes):
    out, o = [], 0
    for s in shapes:
        n = math.prod(s)
        out.append(flat[o:o + n].reshape(s))
        o += n
    return out


_BIG = ("w_in", "w_uq", "w_ukv", "w_out", "w_ff1", "w_ff2")
_SMALL = ("b_ada", "norm_mix_g", "b_gates", "conv_w", "conv_b", "q_lora_g", "kv_lora_g", "q_norm_g", "k_norm_g",
          "mlstm_norm_g", "norm_mlp_g")
_ORDER = ("w_ada", "b_ada", "norm_mix_g", "w_in", "b_gates", "conv_w", "conv_b", "q_lora_g", "w_uq", "kv_lora_g",
          "w_ukv", "q_norm_g", "k_norm_g", "mlstm_norm_g", "w_out", "norm_mlp_g", "w_ff1", "w_ff2")


def kernel(x, c, positions, w_ada, b_ada, norm_mix_g, w_in, b_gates, conv_w, conv_b, q_lora_g, w_uq, kv_lora_g, w_ukv, q_norm_g, k_norm_g, mlstm_norm_g, w_out, norm_mlp_g, w_ff1, w_ff2, loss_target, m_w_ada, m_b_ada, m_norm_mix_g, m_w_in, m_b_gates, m_conv_w, m_conv_b, m_q_lora_g, m_w_uq, m_kv_lora_g, m_w_ukv, m_q_norm_g, m_k_norm_g, m_mlstm_norm_g, m_w_out, m_norm_mlp_g, m_w_ff1, m_w_ff2, v_w_ada, v_b_ada, v_norm_mix_g, v_w_in, v_b_gates, v_conv_w, v_conv_b, v_q_lora_g, v_w_uq, v_kv_lora_g, v_w_ukv, v_q_norm_g, v_k_norm_g, v_mlstm_norm_g, v_w_out, v_norm_mlp_g, v_w_ff1, v_w_ff2):
    args = dict(locals())
    wts = {n: args[n] for n in _ORDER}
    mom = {n: args["m_" + n] for n in _ORDER}
    var = {n: args["v_" + n] for n in _ORDER}
    MX = _MXU_DTYPE
    xi, yi, ci = lax.axis_index("x"), lax.axis_index("y"), lax.axis_index("c")
    chip = 2 * xi + yi
    dev = 2 * chip + ci
    S = x.shape[1]
    CS = 2 * MW // N_CHIP
    GS = DM // N_CHIP

    pk = _pack([c, conv_w, mlstm_norm_g], 1024)
    allpk = _allgather8(pk, name="gather_small").reshape(N_DEV, 8 * 1024)
    c_all = allpk[:, :D]
    per_chip = allpk[0::2]
    conv_w_full = per_chip[:, D:D + CONVW * CS].reshape(N_CHIP, CONVW, CS).transpose(1, 0, 2).reshape(CONVW, 2 * MW)
    o = D + CONVW * CS
    mn_full = per_chip[:, o:o + HM * GS].reshape(N_CHIP, HM, GS).transpose(1, 0, 2).reshape(HM, DM)

    (sc,) = _rowmap(lambda cv: ([cv * _sigmoid(cv)], []), [(c_all, D, 0)], [], [(D, F32)], tile=8, name="silu_c")
    ncol = w_ada.shape[2]
    b_cols = lax.dynamic_slice(b_ada, (0, chip * ncol), (1, ncol))
    modp = _mm(sc, w_ada[0], "nn", name="ada_fwd", tm=8, tn=1024, tk=512, extras=(jnp.broadcast_to(b_cols, (8, ncol)),),
               epilogue=lambda r, b: (r + b,))
    modg = _allgather8(modp, name="gather_mod").reshape(N_CHIP, 2, 8, ncol)[:, 0]
    mod_all = modg.transpose(1, 0, 2).reshape(N_DEV, N_CHIP * ncol)
    modv = jnp.pad(lax.dynamic_slice(mod_all, (dev, 0), (1, 6 * D)).reshape(6, D), ((0, 2), (0, 0)))

    shards = [wts[n][0].astype(MX) for n in _BIG]
    gw_in, gw_uq, gw_ukv = _chip_allgather_halved(shards[:3], name="gather_weights")
    W = _prep_weights(_cols(gw_in), _cols(gw_uq), _cols(gw_ukv), None, None, None, norm_mix_g, norm_mlp_g,
                      q_lora_g, kv_lora_g, q_norm_g, k_norm_g, mn_full, conv_w_full, conv_b, b_gates)

    loss, gx, dmodv, g = _device_step(x[0], loss_target[0], positions[0], modv, W, late=shards[3:])

    small_shapes = [(6 * D,), (D,), (NG,), (CONVW, 2 * MW), (2 * MW,), (Q_LORA,), (KV_LORA,), (QK,), (QK,), (MW,), (D,), (1,)]
    pg = _pack([dmodv, g["norm_mix_g"], g["b_gates"], g["conv_w"], g["conv_b"], g["q_lora_g"], g["kv_lora_g"],
                g["q_norm_g"], g["k_norm_g"], g["mlstm_norm_g"], g["norm_mlp_g"], loss], 4096)
    allpg = _allgather8(pg, name="gather_small_grads")
    tot = _unpack(_sum_blocks(allpg, N_DEV, name="sum_small_grads").reshape(-1), small_shapes)
    dmod_all = allpg.reshape(N_DEV, 8 * 4096)[:, :6 * D]
    gsmall = dict(zip(_SMALL, [tot[0].reshape(1, 6 * D), tot[1].reshape(1, D), tot[2].reshape(1, NG),
                               lax.dynamic_slice(tot[3], (0, chip * CS), (CONVW, CS)).reshape(1, CONVW, CS),
                               tot[4].reshape(1, 2 * MW), tot[5].reshape(1, Q_LORA), tot[6].reshape(1, KV_LORA),
                               tot[7].reshape(1, QK), tot[8].reshape(1, QK),
                               lax.dynamic_slice(tot[9].reshape(HM, DM), (0, chip * GS), (HM, GS)).reshape(1, HM, GS),
                               tot[10].reshape(1, D)]))
    loss_tot = tot[11].reshape(())

    got = g["got"]
    part = []
    for nme, r in zip(_BIG, got):
        wd = r.shape[2]
        (p,) = _rowmap(lambda a0, a1, a2, a3: ([(a0.astype(F32) + a1.astype(F32)) + (a2.astype(F32) + a3.astype(F32))], []),
                       [(r, wd, 0, k) for k in range(N_CHIP)], [], [(wd, F32)], tile=256, name="sum_chips_" + nme)
        part.append(p)
    other = _sibling_exchange(part, name="exchange_cores")

    dm_cols = lax.dynamic_slice(dmod_all, (0, chip * ncol), (N_DEV, ncol))
    g_ada = _outer8(sc.T, dm_cols, name="ada_dw")

    res = {}
    for nme, p, q in zip(_BIG, part, other):
        res[nme] = _adamw(wts[nme][0], [p, q], mom[nme][0], var[nme][0], name="adamw_" + nme)
    res["w_ada"] = _adamw(w_ada[0], [g_ada], m_w_ada[0], v_w_ada[0], name="adamw_w_ada")
    sw = _pack([wts[n] for n in _SMALL], 3072)
    sg = _pack([gsmall[n] for n in _SMALL], 3072)
    sm = _pack([mom[n] for n in _SMALL], 3072)
    sv = _pack([var[n] for n in _SMALL], 3072)
    small_res = _adamw(sw, [sg], sm, sv, name="adamw_small", tile=8)
    shapes = [wts[n].shape for n in _SMALL]
    unp = [_unpack(r.reshape(-1), shapes) for r in small_res]
    for i, nme in enumerate(_SMALL):
        res[nme] = tuple(u[i] for u in unp)
    outs = [loss_tot, gx[None]]
    for kind in range(4):
        outs += [res[n][kind].reshape(wts[n].shape) for n in _ORDER]
    return tuple(outs)
```

```python
import functools
import math

import jax
import jax.numpy as jnp
from jax import lax
from jax.experimental import pallas as pl
from jax.experimental.pallas import tpu as pltpu

F32 = jnp.float32
BF16 = jnp.bfloat16
_MXU_DTYPE = jnp.bfloat16
_INTERPRET = False

D = 2048
H_MLA = 8
NOPE = 128
ROPE = 64
QK = NOPE + ROPE
HP = 256
VD = 128
Q_LORA = 512
KV_LORA = 256
HM = 4
DM = 256
MW = HM * DM
LCH = 128
CONVW = 5
NG = 16
DFF = 4 * D
EPS = 1e-6
M_INIT = -1e30
ROPE_THETA = 10000.0
IN_SIZES = (Q_LORA, KV_LORA, ROPE, MW, MW, MW, MW, NG)
D_IN = sum(IN_SIZES)
P_QM, P_KM, P_VM, P_OM, P_CQ, P_CKV, P_KPE, P_G = 0, 1024, 2048, 3072, 4096, 4608, 4864, 4992
D_INP = 5120

ADAM_LR, ADAM_B1, ADAM_B2, ADAM_EPS, ADAM_WD, ADAM_STEP = 0.001, 0.9, 0.999, 1e-08, 0.01, 10

V7X_VMEM_LIMIT = 56 * 1024 * 1024


def _cparams(sem):
    return pltpu.CompilerParams(dimension_semantics=sem, vmem_limit_bytes=V7X_VMEM_LIMIT)


def _call(body, **kw):
    if _INTERPRET:
        kw.pop("compiler_params", None)
        kw["interpret"] = pltpu.InterpretParams()
    return pl.pallas_call(body, **kw)


def _dot(a, b, form):
    dims = {"nn": ((1,), (0,)), "nt": ((1,), (1,)), "tn": ((0,), (0,))}[form]
    return lax.dot_general(a.astype(_MXU_DTYPE), b.astype(_MXU_DTYPE), (dims, ((), ())),
                           preferred_element_type=F32)


def _mm(a, b, form, *, name, out_dtypes=(F32,), epilogue=None, extras=(), tm=1024, tn=1024, tk=2048, side=()):
    if form == "nn":
        (M, K), (K2, N) = a.shape, b.shape
    elif form == "nt":
        (M, K), (N, K2) = a.shape, b.shape
    else:
        (K, M), (K2, N) = a.shape, b.shape
    assert K == K2, (a.shape, b.shape, form)
    tm, tn = min(tm, M), min(tn, N)
    tk = max(d for d in range(128, min(tk, K) + 1, 128) if K % d == 0) if K > 128 else K
    assert M % tm == 0 and N % tn == 0 and K % tk == 0, (M, N, K, tm, tn, tk)
    nk = K // tk
    ne, no = len(extras), len(out_dtypes)
    if form == "tn":
        a_spec = pl.BlockSpec((tk, tm), lambda i, j, k: (k, i))
    else:
        a_spec = pl.BlockSpec((tm, tk), lambda i, j, k: (i, k))
    if form == "nt":
        b_spec = pl.BlockSpec((tn, tk), lambda i, j, k: (j, k))
    else:
        b_spec = pl.BlockSpec((tk, tn), lambda i, j, k: (k, j))
    mn_spec = pl.BlockSpec((tm, tn), lambda i, j, k: (i, j))
    grid = (M // tm, N // tn, nk)
    ns, io, wrap = _side_exchange(side, False, grid)

    def body(a_ref, b_ref, *rest):
        ex, outs = rest[:ne], rest[ne + ns:ne + ns + no]
        scratch = rest[ne + 2 * ns + no:]
        side_start, side_wait = wrap(rest[ne:ne + ns], rest[ne + ns + no:ne + 2 * ns + no], scratch[1:])
        side_start()
        prod = _dot(a_ref[...], b_ref[...], form)

        def finish(r):
            vals = (r,) if epilogue is None else epilogue(r, *[e[...] for e in ex])
            for o, v in zip(outs, vals):
                o[...] = v.astype(o.dtype)

        if nk == 1:
            finish(prod)
        else:
            acc, k = scratch[0], pl.program_id(2)

            @pl.when(k == 0)
            def _():
                acc[...] = prod

            @pl.when(k > 0)
            def _():
                acc[...] += prod

            @pl.when(k == nk - 1)
            def _():
                finish(acc[...])
        side_wait()

    res = _call(
        body, name=name, grid=grid,
        in_specs=[a_spec, b_spec] + [mn_spec] * ne + io["specs"],
        out_specs=[mn_spec] * no + io["specs"],
        out_shape=[jax.ShapeDtypeStruct((M, N), dt) for dt in out_dtypes] + io["out_shape"],
        scratch_shapes=[pltpu.VMEM((tm, tn) if nk > 1 else (8, 128), F32)] + io["scratch"],
        compiler_params=_cparams(("arbitrary",) * 3 if ns else ("parallel", "parallel", "arbitrary")),
    )(a, b, *extras, *side)
    if ns:
        return (res[0] if no == 1 else res[:no]), list(res[no:])
    return res[0] if no == 1 else res


def _rowmap(fn, rows, bcasts, outs, accs=(), *, tile, name):
    rows = [r if len(r) == 4 else (*r, None) for r in rows]
    S = rows[0][0].shape[-2]
    tile = min(tile, S)
    assert S % tile == 0
    nr, nb, no, na = len(rows), len(bcasts), len(outs), len(accs)

    def body(*refs):
        vals = [r[...] for r in refs[:nr + nb]]
        o_refs, a_refs = refs[nr + nb:nr + nb + no], refs[nr + nb + no:]
        o_vals, a_vals = fn(*vals)
        for r, v in zip(o_refs, o_vals):
            r[...] = v.astype(r.dtype)
        if na:
            @pl.when(pl.program_id(0) == 0)
            def _():
                for r in a_refs:
                    r[...] = jnp.zeros(r.shape, r.dtype)
            for r, v in zip(a_refs, a_vals):
                r[...] += v

    in_specs = []
    for (arr, w, cb, lead) in rows:
        if lead is None:
            in_specs.append(pl.BlockSpec((tile, w), lambda i, cb=cb: (i, cb)))
        else:
            in_specs.append(pl.BlockSpec((None, tile, w), lambda i, cb=cb, lead=lead: (lead, i, cb)))
    in_specs += [pl.BlockSpec(b.shape, lambda i: (0, 0)) for b in bcasts]
    out_specs = [pl.BlockSpec((tile, w), lambda i: (i, 0)) for (w, _) in outs]
    out_specs += [pl.BlockSpec(s, lambda i: (0, 0)) for s in accs]
    out_shape = [jax.ShapeDtypeStruct((S, w), dt) for (w, dt) in outs]
    out_shape += [jax.ShapeDtypeStruct(s, F32) for s in accs]
    return _call(
        body, name=name, grid=(S // tile,), in_specs=in_specs, out_specs=out_specs, out_shape=out_shape,
        compiler_params=_cparams(("arbitrary",)),
    )(*[r[0] for r in rows], *bcasts)


def _colsum(v):
    return jnp.sum(v, axis=0, keepdims=True)


def _rms(x, n):
    r = lax.rsqrt(jnp.sum(x * x, axis=-1, keepdims=True) * (1.0 / n) + EPS)
    return x * r, r


def _rms_bwd(dxhat, xhat, r, n):
    return r * (dxhat - xhat * (jnp.sum(dxhat * xhat, axis=-1, keepdims=True) * (1.0 / n)))


def _rope_fwd(r, cosp, s1, s2):
    return r * cosp + pltpu.roll(r, 32, 1) * s1 + pltpu.roll(r, 96, 1) * s2


def _rope_bwd(d, cosp, s1, s2):
    return d * cosp + pltpu.roll(d * s1, 96, 1) + pltpu.roll(d * s2, 32, 1)


def _sigmoid(x):
    return 1.0 / (1.0 + jnp.exp(-x))


def _halo_specs(tile, halo, width, cb, S, lead=None):
    nh = tile // halo
    last = S // halo - 1
    if lead is None:
        return [
            pl.BlockSpec((tile, width), lambda i: (i, cb)),
            pl.BlockSpec((halo, width), lambda i: (jnp.maximum(i * nh - 1, 0), cb)),
            pl.BlockSpec((halo, width), lambda i: (jnp.minimum((i + 1) * nh, last), cb)),
        ]
    return [
        pl.BlockSpec((None, tile, width), lambda i: (lead, i, cb)),
        pl.BlockSpec((None, halo, width), lambda i: (lead, jnp.maximum(i * nh - 1, 0), cb)),
        pl.BlockSpec((None, halo, width), lambda i: (lead, jnp.minimum((i + 1) * nh, last), cb)),
    ]


def _conv_fwd(proj, conv_w8, conv_b, *, tile=256):
    S = proj.shape[0]
    T = min(tile, S)
    n = S // T
    W = 2 * MW

    def body(x_ref, xp_ref, xn_ref, w_ref, b_ref, q_ref, k_ref, ext):
        i = pl.program_id(0)
        ext[pl.ds(0, 8), :] = xp_ref[...] * (i > 0).astype(F32)
        ext[pl.ds(8, T), :] = x_ref[...]
        ext[pl.ds(8 + T, 8), :] = xn_ref[...] * (i < n - 1).astype(F32)
        w = w_ref[...]
        y = b_ref[...] + w[0:1, :] * ext[pl.ds(6, T), :]
        for o in range(1, CONVW):
            y = y + w[o:o + 1, :] * ext[pl.ds(6 + o, T), :]
        y = y * _sigmoid(y)
        q_ref[...] = y[:, :MW].astype(q_ref.dtype)
        k_ref[...] = (y[:, MW:] * (DM ** -0.5)).astype(k_ref.dtype)

    return _call(
        body, name="conv_fwd", grid=(n,),
        in_specs=_halo_specs(T, 8, W, 0, S) + [pl.BlockSpec((8, W), lambda i: (0, 0)),
                                                 pl.BlockSpec((1, W), lambda i: (0, 0))],
        out_specs=[pl.BlockSpec((T, MW), lambda i: (i, 0))] * 2,
        out_shape=[jax.ShapeDtypeStruct((S, MW), _MXU_DTYPE)] * 2,
        scratch_shapes=[pltpu.VMEM((T + 16, W), F32)],
        compiler_params=_cparams(("arbitrary",)),
    )(proj, proj, proj, conv_w8, conv_b)


def _conv_bwd(proj, dqd, dkd, conv_w8, conv_b, *, tile=256):
    S = proj.shape[0]
    T = min(tile, S)
    n = S // T
    W = 2 * MW

    def body(x_ref, xp_ref, xn_ref, *rest):
        g = rest[:12]
        w_ref, b_ref, dx_ref, dw_ref, db_ref, ext, edp = rest[12:]
        i = pl.program_id(0)
        mp = (i > 0).astype(F32)
        mn = (i < n - 1).astype(F32)
        ext[pl.ds(0, 16), :] = xp_ref[...] * mp
        ext[pl.ds(16, T), :] = x_ref[...]
        ext[pl.ds(16 + T, 16), :] = xn_ref[...] * mn
        w = w_ref[...]
        pre = b_ref[...] + w[0:1, :] * ext[pl.ds(6, T + 16), :]
        for o in range(1, CONVW):
            pre = pre + w[o:o + 1, :] * ext[pl.ds(6 + o, T + 16), :]
        sg = _sigmoid(pre)
        dsilu = sg * (1.0 + pre * (1.0 - sg))
        for half, (a0, a1) in enumerate(((g[0:3], g[3:6]), (g[6:9], g[9:12]))):
            sc = 1.0 if half == 0 else DM ** -0.5
            cols = pl.ds(half * MW, MW)
            edp[pl.ds(0, 8), cols] = (a0[1][...] + a1[1][...]) * (mp * sc)
            edp[pl.ds(8, T), cols] = (a0[0][...] + a1[0][...]) * sc
            edp[pl.ds(8 + T, 8), cols] = (a0[2][...] + a1[2][...]) * (mn * sc)
        edp[...] = edp[...] * dsilu
        dpm = edp[pl.ds(8, T), :]
        dx = w[0:1, :] * edp[pl.ds(10, T), :]
        for o in range(1, CONVW):
            dx = dx + w[o:o + 1, :] * edp[pl.ds(10 - o, T), :]
        dx_ref[...] = dx.astype(dx_ref.dtype)

        @pl.when(i == 0)
        def _():
            dw_ref[...] = jnp.zeros(dw_ref.shape, F32)
            db_ref[...] = jnp.zeros(db_ref.shape, F32)

        for o in range(CONVW):
            dw_ref[pl.ds(o, 1), :] += _colsum(ext[pl.ds(14 + o, T), :] * dpm)
        db_ref[...] += _colsum(dpm)

    gspecs = _halo_specs(T, 8, MW, 0, S) * 4
    return _call(
        body, name="conv_bwd", grid=(n,),
        in_specs=_halo_specs(T, 16, W, 0, S) + gspecs + [pl.BlockSpec((8, W), lambda i: (0, 0)),
                                                          pl.BlockSpec((1, W), lambda i: (0, 0))],
        out_specs=[pl.BlockSpec((T, W), lambda i: (i, 0)), pl.BlockSpec((8, W), lambda i: (0, 0)),
                   pl.BlockSpec((1, W), lambda i: (0, 0))],
        out_shape=[jax.ShapeDtypeStruct((S, W), _MXU_DTYPE), jax.ShapeDtypeStruct((8, W), F32),
                   jax.ShapeDtypeStruct((1, W), F32)],
        scratch_shapes=[pltpu.VMEM((T + 32, W), F32), pltpu.VMEM((T + 16, W), F32)],
        compiler_params=_cparams(("arbitrary",)),
    )(proj, proj, proj, *([dqd[0]] * 3), *([dqd[1]] * 3), *([dkd[0]] * 3), *([dkd[1]] * 3), conv_w8, conv_b)


_ATT_SCALE = QK ** -0.5
_LOG2E = math.log2(math.e)
_Q_PRESCALE = _ATT_SCALE * _LOG2E


def _side_exchange(side, gather, grid):
    ns = len(side)
    io = _exchange_io(side, gather) if ns else dict(specs=[], out_shape=[], scratch=[])

    def wrap(refs_in, refs_out, sems):
        if not ns:
            return (lambda: None), (lambda: None)
        start, wait = _exchange_ops(refs_in, refs_out, *sems, gather=gather)
        ids = [pl.program_id(a) for a in range(len(grid))]
        first = functools.reduce(jnp.logical_and, [i == 0 for i in ids])
        last = functools.reduce(jnp.logical_and, [i == g - 1 for i, g in zip(ids, grid)])
        return (lambda: pl.when(first)(start)), (lambda: pl.when(last)(wait))

    return ns, io, wrap


def _attn_fwd(q, k, v, *, side=(), tq=1024, tk=8192, split=4, unroll=1):
    S = q.shape[0]
    tq, tk = min(tq, S), min(tk, S)
    nkv = S // tk
    hq = tq // split
    grid = (H_MLA, S // tq)
    ns, io, wrap = _side_exchange(side, True, grid)

    def body(q_ref, k_ref, v_ref, *rest):
        o_ref, qa_ref = rest[ns:ns + 2]
        m_s, l_s, acc_s = rest[2 * ns + 2:2 * ns + 5]
        side_start, side_wait = wrap(rest[:ns], rest[ns + 2:2 * ns + 2], rest[2 * ns + 5:])
        side_start()
        m_s[...] = jnp.full(m_s.shape, -1e30, F32)
        l_s[...] = jnp.zeros(l_s.shape, F32)
        acc_s[...] = jnp.zeros(acc_s.shape, F32)

        def step(j, carry):
            rows = pl.ds(pl.multiple_of(j * tk, tk), tk)
            kj, vj = k_ref[rows, :], v_ref[rows, :]
            for a in range(split):
                r = pl.ds(a * hq, hq)
                s = _dot(q_ref[r, :], kj, "nt")
                m_old = m_s[r, :]
                m_new = jnp.maximum(m_old, jnp.max(s, axis=1, keepdims=True))
                p = jnp.exp2(s - m_new)
                alpha = jnp.exp2(m_old - m_new)
                l_s[r, :] = alpha * l_s[r, :] + jnp.sum(p, axis=1, keepdims=True)
                acc_s[r, :] = alpha * acc_s[r, :] + _dot(p, vj, "nn")
                m_s[r, :] = m_new
            return carry

        lax.fori_loop(0, nkv, step, 0, unroll=unroll if nkv % unroll == 0 else 1)
        o_ref[...] = (acc_s[...] / l_s[...]).astype(o_ref.dtype)
        lse = m_s[...] + jnp.log2(l_s[...])
        hi = lse.astype(_MXU_DTYPE).astype(F32)
        lane = lax.broadcasted_iota(jnp.int32, (tq, HP), 1)
        qa = jnp.where(lane == QK, -hi, jnp.where(lane == QK + 1, hi - lse, q_ref[...].astype(F32)))
        qa_ref[...] = qa.astype(qa_ref.dtype)
        side_wait()

    res = _call(
        body, name="attn_fwd", grid=grid,
        in_specs=[pl.BlockSpec((tq, HP), lambda h, i: (i, h)),
                  pl.BlockSpec((S, HP), lambda h, i: (0, h)),
                  pl.BlockSpec((S, VD), lambda h, i: (0, h))] + io["specs"],
        out_specs=[pl.BlockSpec((tq, VD), lambda h, i: (i, h)),
                   pl.BlockSpec((tq, HP), lambda h, i: (i, h))] + io["specs"],
        out_shape=[jax.ShapeDtypeStruct((S, H_MLA * VD), _MXU_DTYPE),
                   jax.ShapeDtypeStruct((S, H_MLA * HP), _MXU_DTYPE)] + io["out_shape"],
        scratch_shapes=[pltpu.VMEM((tq, 1), F32), pltpu.VMEM((tq, 1), F32), pltpu.VMEM((tq, VD), F32)]
        + io["scratch"],
        compiler_params=_cparams(("arbitrary", "arbitrary")),
    )(q, k, v, *side)
    return res[0], res[1], list(res[2:])


def _attn_bwd(qa, k, va, doa, *, side=(), tq=4096, tk=512, split=4, unroll=1):
    S = qa.shape[0]
    tq, tk = min(tq, S), min(tk, S)
    nq, nkb = S // tq, S // tk
    hq = tq // split
    grid = (H_MLA, nkb)
    ns, io, wrap = _side_exchange(side, False, grid)

    def body(q_ref, k_ref, v_ref, do_ref, *rest):
        dq_ref, dk_ref, dv_ref = rest[ns:ns + 3]
        side_start, side_wait = wrap(rest[:ns], rest[ns + 3:2 * ns + 3], rest[2 * ns + 3:])
        side_start()
        j = pl.program_id(1)

        @pl.when(j == 0)
        def _():
            dq_ref[...] = jnp.zeros(dq_ref.shape, F32)

        dk_ref[...] = jnp.zeros(dk_ref.shape, F32)
        dv_ref[...] = jnp.zeros(dv_ref.shape, F32)
        kb, vb = k_ref[...], v_ref[...]

        def step(i, carry):
            for a in range(split):
                r = pl.ds(pl.multiple_of(i * tq + a * hq, hq), hq)
                qg, dog = q_ref[r, :], do_ref[r, :]
                p = jnp.exp2(_dot(qg, kb, "nt"))
                ds = (p * _dot(dog, vb, "nt")).astype(_MXU_DTYPE)
                dq_ref[r, :] += _dot(ds, kb, "nn")
                dv_ref[...] += _dot(p, dog, "tn")
                dk_ref[...] += _dot(ds, qg, "tn")
            return carry

        lax.fori_loop(0, nq, step, 0, unroll=unroll if nq % unroll == 0 else 1)
        dk_ref[...] = dk_ref[...] * (1.0 / _LOG2E)

        @pl.when(j == nkb - 1)
        def _():
            dq_ref[...] = dq_ref[...] * _ATT_SCALE

        side_wait()

    blk = pl.BlockSpec((tk, HP), lambda h, j: (j, h))
    whole = pl.BlockSpec((S, HP), lambda h, j: (0, h))
    res = _call(
        body, name="attn_bwd", grid=grid,
        in_specs=[whole, blk, blk, whole] + io["specs"],
        out_specs=[whole, blk, blk] + io["specs"],
        out_shape=[jax.ShapeDtypeStruct((S, H_MLA * HP), F32)] * 3 + io["out_shape"],
        scratch_shapes=io["scratch"],
        compiler_params=_cparams(("arbitrary", "arbitrary")),
    )(qa, k, va, doa, *side)
    return res[0], res[1], res[2], list(res[3:])


def _mlstm_chunk_terms(g, q, k, v, gates, gates_t, bg_row, C, n_row, m):
    L = LCH
    d = g // HM
    h = g % HM
    i_idx = d * 8 + h
    f_idx = d * 8 + 4 + h
    rr = lax.broadcasted_iota(jnp.int32, (L, L), 0)
    cc = lax.broadcasted_iota(jnp.int32, (L, L), 1)
    order = (rr - cc) * (1 - 2 * d)
    tri = order >= 0
    eye = rr == cc
    lane = lax.broadcasted_iota(jnp.int32, gates.shape, 1)
    sub = lax.broadcasted_iota(jnp.int32, gates_t.shape, 0)
    lane_b = lax.broadcasted_iota(jnp.int32, bg_row.shape, 1)
    pick_c = lambda idx: jnp.sum(jnp.where(lane == idx, gates, 0.0), axis=1, keepdims=True)
    pick_r = lambda idx: jnp.sum(jnp.where(sub == idx, gates_t, 0.0), axis=0, keepdims=True)
    pick_b = lambda idx: jnp.sum(jnp.where(lane_b == idx, bg_row, 0.0), axis=1, keepdims=True)
    i_col, i_row = pick_c(i_idx) + pick_b(i_idx), pick_r(i_idx) + pick_b(i_idx)
    f_col, f_row = pick_c(f_idx) + pick_b(f_idx), pick_r(f_idx) + pick_b(f_idx)
    logsig = lambda x: jnp.minimum(x, 0.0) - jnp.log(1.0 + jnp.exp(-jnp.abs(x)))
    lf_col, lf_row = logsig(f_col), logsig(f_row)
    b_col = jnp.sum(jnp.where(tri, lf_row, 0.0), axis=1, keepdims=True)
    tri_t = order <= 0
    b_row = jnp.sum(jnp.where(tri_t, lf_col, 0.0), axis=0, keepdims=True)
    bL = jnp.sum(lf_row, axis=1, keepdims=True)
    log_inter = b_col + m
    logD = jnp.where(tri, b_col - b_row + i_row, -jnp.inf)
    m_t = jnp.maximum(log_inter, jnp.max(logD, axis=1, keepdims=True))
    Dm = jnp.exp(logD - m_t)
    w_inter = jnp.exp(log_inter - m_t)
    A = _dot(q, k, "nt")
    Sc = A * Dm
    numI = _dot(q, C, "nt")
    qf = q.astype(F32)
    kf = k.astype(F32)
    denI = jnp.sum(qf * n_row, axis=1, keepdims=True)
    num = _dot(Sc, v, "nn") + w_inter * numI
    den = jnp.sum(Sc, axis=1, keepdims=True) + w_inter * denI
    floor = jnp.exp(-m_t)
    Nst = jnp.maximum(jnp.abs(den), floor)
    log_w = bL - b_col + i_col
    m_new = jnp.maximum(bL + m, jnp.max(log_w, axis=0, keepdims=True))
    decay = jnp.exp(bL + m - m_new)
    w_col = jnp.exp(log_w - m_new)
    return dict(tri=tri, eye=eye, f_row=f_row, Dm=Dm, w_inter=w_inter, A=A, Sc=Sc, numI=numI, denI=denI,
                num=num, den=den, floor=floor, Nst=Nst, m_new=m_new, decay=decay, w_col=w_col, qf=qf, kf=kf)


def _mlstm_specs(nc, d, step_of):
    chunk = lambda j: step_of(j) if d == 0 else nc - 1 - step_of(j)
    return chunk, [
        pl.BlockSpec((LCH, DM), lambda h, j: (chunk(j), h)),
        pl.BlockSpec((LCH, DM), lambda h, j: (chunk(j), h)),
        pl.BlockSpec((LCH, DM), lambda h, j: (chunk(j), P_VM // DM + h)),
        pl.BlockSpec((LCH, 128), lambda h, j: (chunk(j), P_G // 128)),
        pl.BlockSpec((NG, LCH), lambda h, j: (0, chunk(j))),
    ]


def _mlstm_fwd(qc, kc, proj, gates_t, bg_row):
    S = qc.shape[0]
    nc = S // LCH
    in_specs, out_specs = [], []
    for d in (0, 1):
        chunk, specs = _mlstm_specs(nc, d, lambda j: j)
        in_specs += specs
        out_specs += [pl.BlockSpec((LCH, DM), lambda h, j, chunk=chunk: (chunk(j), h)),
                      pl.BlockSpec((None, None, DM, DM), lambda h, j, chunk=chunk: (h, chunk(j), 0, 0)),
                      pl.BlockSpec((None, None, 8, DM), lambda h, j, chunk=chunk: (h, chunk(j), 0, 0))]
    in_specs.append(pl.BlockSpec((1, 128), lambda h, j: (0, 0)))

    def body(*refs):
        bg_ref, outs, (C_s, n_s, m_s) = refs[10], refs[11:17], refs[17:]

        @pl.when(pl.program_id(1) == 0)
        def _():
            C_s[...] = jnp.zeros(C_s.shape, F32)
            n_s[...] = jnp.zeros(n_s.shape, F32)
            m_s[...] = jnp.full(m_s.shape, M_INIT, F32)

        for d in (0, 1):
            q_ref, k_ref, v_ref, g_ref, gt_ref = refs[5 * d:5 * d + 5]
            h_ref, cst_ref, nm_ref = outs[3 * d:3 * d + 3]
            g = d * HM + pl.program_id(0)
            C, n_row, m = C_s[d], n_s[d, 0:1, :], m_s[d, 0:1, 0:1]
            cst_ref[...] = C
            nm_ref[0:1, :] = n_row
            nm_ref[1:2, :] = jnp.broadcast_to(m, (1, DM))
            nm_ref[2:8, :] = jnp.zeros((6, DM), F32)
            q, k, v = q_ref[...], k_ref[...], v_ref[...]
            t = _mlstm_chunk_terms(g, q, k, v, g_ref[...], gt_ref[...], bg_ref[...], C, n_row, m)
            h_ref[...] = t["num"] / t["Nst"]
            wv = t["w_col"] * v
            C_s[d] = t["decay"] * C + _dot(wv, k, "tn")
            n_s[d, 0:1, :] = t["decay"] * n_row + _colsum(t["w_col"] * t["kf"])
            m_s[d] = jnp.broadcast_to(t["m_new"], (8, 128))

    res = _call(
        body, name="mlstm_fwd", grid=(HM, nc), in_specs=in_specs, out_specs=out_specs,
        out_shape=[jax.ShapeDtypeStruct((S, MW), F32), jax.ShapeDtypeStruct((HM, nc, DM, DM), F32),
                   jax.ShapeDtypeStruct((HM, nc, 8, DM), F32)] * 2,
        scratch_shapes=[pltpu.VMEM((2, DM, DM), F32), pltpu.VMEM((2, 8, DM), F32), pltpu.VMEM((2, 8, 128), F32)],
        compiler_params=_cparams(("parallel", "arbitrary")),
    )(*([qc, kc, proj, proj, gates_t] * 2), bg_row)
    return (res[0], res[3]), (res[1], res[4]), (res[2], res[5])


def _mlstm_bwd(qc, kc, proj, gates_t, bg_row, dh, cst, nm):
    S = qc.shape[0]
    nc = S // LCH
    in_specs, out_specs = [], []
    for d in (0, 1):
        chunk, specs = _mlstm_specs(nc, d, lambda j: nc - 1 - j)
        in_specs += specs + [pl.BlockSpec((LCH, DM), lambda h, j, chunk=chunk: (chunk(j), h)),
                             pl.BlockSpec((None, None, DM, DM), lambda h, j, chunk=chunk: (h, chunk(j), 0, 0)),
                             pl.BlockSpec((None, None, 8, DM), lambda h, j, chunk=chunk: (h, chunk(j), 0, 0))]
        out_specs += [pl.BlockSpec((LCH, DM), lambda h, j, chunk=chunk: (chunk(j), h))] * 3
        out_specs += [pl.BlockSpec((None, None, 8, LCH), lambda h, j, chunk=chunk: (h, chunk(j), 0, 0))]
    in_specs.append(pl.BlockSpec((1, 128), lambda h, j: (0, 0)))

    def body(*refs):
        bg_ref, outs, (dC_s, dn_s) = refs[16], refs[17:25], refs[25:]

        @pl.when(pl.program_id(1) == 0)
        def _():
            dC_s[...] = jnp.zeros(dC_s.shape, F32)
            dn_s[...] = jnp.zeros(dn_s.shape, F32)

        for d in (0, 1):
            _mlstm_bwd_chain(d, refs[8 * d:8 * d + 8], bg_ref, outs[4 * d:4 * d + 4], dC_s, dn_s)

    res = _call(
        body, name="mlstm_bwd", grid=(HM, nc), in_specs=in_specs, out_specs=out_specs,
        out_shape=([jax.ShapeDtypeStruct((S, MW), F32)] * 3 + [jax.ShapeDtypeStruct((HM, nc, 8, LCH), F32)]) * 2,
        scratch_shapes=[pltpu.VMEM((2, DM, DM), F32), pltpu.VMEM((2, 8, DM), F32)],
        compiler_params=_cparams(("parallel", "arbitrary")),
    )(*[a for d in (0, 1) for a in (qc, kc, proj, proj, gates_t, dh, cst[d], nm[d])], bg_row)
    return (res[0], res[4]), (res[1], res[5]), (res[2], res[6]), (res[3], res[7])


def _mlstm_bwd_chain(d, ins, bg_ref, outs, dC_s, dn_s):
        q_ref, k_ref, v_ref, g_ref, gt_ref, dh_ref, cst_ref, nm_ref = ins
        dq_ref, dk_ref, dv_ref, dg_ref = outs
        g = d * HM + pl.program_id(0)
        C, n_row, m = cst_ref[...], nm_ref[0:1, :], nm_ref[1:2, 0:1]
        q, k, v = q_ref[...], k_ref[...], v_ref[...]
        t = _mlstm_chunk_terms(g, q, k, v, g_ref[...], gt_ref[...], bg_ref[...], C, n_row, m)
        tri, eye, qf, kf = t["tri"], t["eye"], t["qf"], t["kf"]
        w_inter, w_col, decay, Nst = t["w_inter"], t["w_col"], t["decay"], t["Nst"]
        dC, dn = dC_s[d], dn_s[d, 0:1, :]
        dhv = dh_ref[...]
        hval = t["num"] / Nst
        dnum = dhv / Nst
        dNst = -jnp.sum(dhv * hval, axis=1, keepdims=True) / Nst
        dden = jnp.where(jnp.abs(t["den"]) > t["floor"], jnp.sign(t["den"]) * dNst, 0.0)
        dSc = _dot(dnum, v, "nt") + dden
        dA = dSc * t["Dm"]
        G = dSc * t["Sc"]
        KdC = _dot(k, dC, "nt")
        dq = _dot(dA, k, "nn") + w_inter * _dot(dnum, C, "nn") + (w_inter * dden) * n_row
        dk = _dot(dA, q, "tn") + w_col * _dot(v, dC, "nn") + w_col * dn
        dv = _dot(t["Sc"], dnum, "tn") + w_col * KdC
        dq_ref[...] = dq
        dk_ref[...] = dk
        dv_ref[...] = dv
        dlog_inter = w_inter * (jnp.sum(dnum * t["numI"], axis=1, keepdims=True) + dden * t["denI"])
        rowG = jnp.sum(G, axis=1, keepdims=True)
        colG = jnp.sum(G, axis=0, keepdims=True)
        u_col = w_col * (jnp.sum(v * KdC, axis=1, keepdims=True) + jnp.sum(kf * dn, axis=1, keepdims=True))
        colG_c = jnp.sum(jnp.where(eye, colG, 0.0), axis=1, keepdims=True)
        u_row = jnp.sum(jnp.where(eye, u_col, 0.0), axis=0, keepdims=True)
        db_col = rowG + dlog_inter - u_col - colG_c
        dbL = jnp.sum(u_col, axis=0, keepdims=True) + decay * (
            jnp.sum(jnp.sum(dC * C, axis=1, keepdims=True), axis=0, keepdims=True)
            + jnp.sum(dn * n_row, axis=1, keepdims=True))
        dlf_row = jnp.sum(jnp.where(tri, db_col, 0.0), axis=0, keepdims=True) + dbL
        di_row = colG + u_row
        df_row = dlf_row * (1.0 - _sigmoid(t["f_row"]))
        dg_ref[...] = jnp.zeros(dg_ref.shape, F32)
        dg_ref[0:1, :] = di_row
        dg_ref[1:2, :] = df_row
        dC_s[d] = decay * dC + _dot(w_inter * dnum, q, "tn")
        dn_s[d, 0:1, :] = decay * dn + _colsum((w_inter * dden) * qf)


def _pad_w_in(w):
    cq, ckv, kpe, qm, km, vm, om, gt = _split_in(w)
    z = lambda n: jnp.zeros((w.shape[0], n), w.dtype)
    return jnp.concatenate([qm, km, vm, om, cq, ckv, kpe, z(HP - QK), gt, z(128 - NG)], axis=1)


def _split_in(w):
    out, o = [], 0
    for n in IN_SIZES:
        out.append(w[:, o:o + n])
        o += n
    return out


def _unpad_w_in(g):
    return jnp.concatenate([g[:, P_CQ:P_CQ + Q_LORA], g[:, P_CKV:P_CKV + KV_LORA], g[:, P_KPE:P_KPE + ROPE],
                            g[:, 0:4 * MW], g[:, P_G:P_G + NG]], axis=1)


def _pad_w_uq(w):
    return jnp.pad(w.reshape(Q_LORA, H_MLA, QK), ((0, 0), (0, 0), (0, HP - QK))).reshape(Q_LORA, H_MLA * HP)


def _unpad_w_uq(g):
    return g.reshape(Q_LORA, H_MLA, HP)[:, :, :QK].reshape(Q_LORA, H_MLA * QK)


def _perm_w_ukv(w):
    return w.reshape(KV_LORA, H_MLA, 2, NOPE).transpose(0, 2, 1, 3).reshape(KV_LORA, 2 * H_MLA * NOPE)


def _unperm_w_ukv(g):
    return g.reshape(KV_LORA, 2, H_MLA, NOPE).transpose(0, 2, 1, 3).reshape(KV_LORA, 2 * H_MLA * NOPE)


def _rope_tables(positions):
    half = ROPE // 2
    freqs = ROPE_THETA ** (-jnp.arange(half, dtype=F32) / half)
    ang = positions.astype(F32)[:, None] * freqs
    cos, sin = jnp.cos(ang), jnp.sin(ang)
    z32, z64 = jnp.zeros_like(cos), jnp.zeros((cos.shape[0], 64), F32)
    return (jnp.concatenate([cos, cos, z64], axis=1), jnp.concatenate([z32, sin, z64], axis=1),
            jnp.concatenate([-sin, z32, z64], axis=1))


def _device_step(x, tgt, positions, modv, W, late=None):
    S = x.shape[0]
    MX = _MXU_DTYPE
    cosp, rs1, rs2 = _rope_tables(positions)
    tabs = [(cosp, 128, 0), (rs1, 128, 0), (rs2, 128, 0)]
    cat1 = lambda vs: jnp.concatenate(vs, axis=1)
    hsl = lambda hh, w: slice(hh * w, (hh + 1) * w)

    def ln1(xv, g, mv):
        xhat, _ = _rms(xv, D)
        return [xhat * g * (1.0 + mv[1:2]) + mv[0:1]], []

    (h,) = _rowmap(ln1, [(x, D, 0)], [W["g_mix"], modv], [(D, MX)], tile=128, name="ln1")
    proj = _mm(h, W["w_in"], "nn", name="proj")

    def lora(cq, ckv, gq, gkv):
        return [_rms(cq, Q_LORA)[0] * gq, _rms(ckv, KV_LORA)[0] * gkv], []

    cqn, ckvn = _rowmap(lora, [(proj, Q_LORA, P_CQ // Q_LORA), (proj, KV_LORA, P_CKV // KV_LORA)],
                        [W["g_qlora"], W["g_kvlora"]], [(Q_LORA, MX), (KV_LORA, MX)], tile=256, name="lora_norm")
    q_raw = _mm(cqn, W["w_uq"], "nn", name="q_up")
    kv_raw = _mm(ckvn, W["w_ukv"], "nn", name="kv_up")

    def mla_q(qr, cp, a1, a2, gq):
        outs = []
        for hh in range(H_MLA):
            y = _rms(qr[:, hsl(hh, HP)], QK)[0] * gq
            outs += [y[:, :NOPE], _rope_fwd(y[:, NOPE:], cp, a1, a2)]
        return [cat1(outs) * _Q_PRESCALE], []

    (qh,) = _rowmap(mla_q, [(q_raw, H_MLA * HP, 0)] + tabs, [W["gq"]], [(H_MLA * HP, MX)], tile=128, name="mla_q")

    def mla_k(kvr, kpe, cp, a1, a2, gk):
        lane = lax.broadcasted_iota(jnp.int32, (kvr.shape[0], 128), 1)
        outs, vas = [], []
        for hh in range(H_MLA):
            y = _rms(cat1([kvr[:, hsl(hh, NOPE)], kpe]), QK)[0] * gk
            outs += [y[:, :NOPE], _rope_fwd(y[:, NOPE:], cp, a1, a2) + ((lane == ROPE) | (lane == ROPE + 1)).astype(F32)]
            vas += [kvr[:, H_MLA * NOPE + hh * VD:H_MLA * NOPE + (hh + 1) * VD], (lane < 2).astype(F32)]
        return [cat1(outs), kvr[:, H_MLA * NOPE:], cat1(vas)], []

    kh, vh, va = _rowmap(mla_k, [(kv_raw, 2 * H_MLA * NOPE, 0), (proj, 128, P_KPE // 128)] + tabs, [W["gk"]],
                         [(H_MLA * HP, MX), (H_MLA * VD, MX), (H_MLA * HP, MX)], tile=128, name="mla_k")
    attn_o, qa, gathered = _attn_fwd(qh, kh, vh, side=late or ())
    if late:
        W = dict(W, w_out=gathered[0].reshape(D, D), w_ff1=_cols(gathered[1]), w_ff2=gathered[2].reshape(DFF, D))

    qc, kc = _conv_fwd(proj, W["conv_w8"], W["conv_b"])
    gates_t = proj[:, P_G:P_G + NG].T
    (h_f, h_b), cst, nm = _mlstm_fwd(qc, kc, proj, gates_t, W["bg_row"])
    hrows = [(h_f, MW, 0), (h_b, MW, 0), (proj, MW, P_OM // MW)]

    def ml_out(ao, hf, hb, om, gmn):
        outs = [ao.astype(F32)]
        hs = hf + hb
        for hh in range(HM):
            sl = hsl(hh, DM)
            outs.append(_sigmoid(om[:, sl]) * _rms(hs[:, sl], DM)[0] * gmn[:, sl])
        return [cat1(outs)], []

    (cat,) = _rowmap(ml_out, [(attn_o, MW, 0)] + hrows, [W["g_mn"]], [(D, MX)], tile=128, name="ml_out")
    mixed = _mm(cat, W["w_out"], "nn", name="out_proj")

    def res_ln2(xv, mx, g, mv):
        x1 = xv + mv[2:3] * mx
        return [x1, _rms(x1, D)[0] * g * (1.0 + mv[4:5]) + mv[3:4]], []

    x1, h2 = _rowmap(res_ln2, [(x, D, 0), (mixed, D, 0)], [W["g_mlp"], modv], [(D, F32), (D, MX)],
                     tile=128, name="res_ln2")
    a, u = _mm(h2, W["w_ff1"], "nn", name="ff1", out_dtypes=(MX, MX),
               epilogue=lambda r: (jnp.square(jnp.maximum(r, 0.0)), r))
    y = _mm(a, W["w_ff2"], "nn", name="ff2")

    def final(x1v, yv, tv, mv):
        err = x1v + mv[5:6] * yv - tv
        dout = err * (1.0 / D)
        loss = jnp.sum(jnp.sum(0.5 * err * dout, axis=1, keepdims=True), axis=0, keepdims=True)
        return [dout, mv[5:6] * dout], [loss, _colsum(dout * yv)]

    dout, dy, loss, dgate2 = _rowmap(final, [(x1, D, 0), (y, D, 0), (tgt, D, 0)], [modv], [(D, F32), (D, MX)],
                                     [(1, 1), (1, D)], tile=128, name="loss_head")

    du = _mm(dy, W["w_ff2"], "nt", name="ff2_dx", out_dtypes=(MX,), extras=(u,),
             epilogue=lambda r, uv: (r * (2.0 * jnp.maximum(uv.astype(F32), 0.0)),))
    g_ff2 = _mm(a, dy, "tn", name="ff2_dw")
    dh2 = _mm(du, W["w_ff1"], "nt", name="ff1_dx")
    g_ff1 = _mm(h2, du, "tn", name="ff1_dw")

    def ln2_bwd(dh2v, x1v, doutv, mxv, g, mv):
        xhat, r = _rms(x1v, D)
        dn2 = dh2v * (1.0 + mv[4:5])
        dx1 = doutv + _rms_bwd(dn2 * g, xhat, r, D)
        return [dx1, mv[2:3] * dx1], [_colsum(dh2v), _colsum(dh2v * xhat * g), _colsum(dn2 * xhat), _colsum(dx1 * mxv)]

    dx1, dmixed, dshift2, dscale2, dg_mlp, dgate1 = _rowmap(
        ln2_bwd, [(dh2, D, 0), (x1, D, 0), (dout, D, 0), (mixed, D, 0)], [W["g_mlp"], modv],
        [(D, F32), (D, MX)], [(1, D)] * 4, tile=128, name="ln2_bwd")
    dcat = _mm(dmixed, W["w_out"], "nt", name="out_dx")
    g_out = _mm(cat, dmixed, "tn", name="out_dw")

    def ml_out_bwd(dml, hf, hb, om, gmn):
        hs = hf + hb
        dhs, dos, dgs = [], [], []
        for hh in range(HM):
            sl = hsl(hh, DM)
            xhat, r = _rms(hs[:, sl], DM)
            g, sg, d = gmn[:, sl], _sigmoid(om[:, sl]), dml[:, sl]
            dos.append(d * xhat * g * sg * (1.0 - sg))
            dhn = d * sg
            dgs.append(_colsum(dhn * xhat))
            dhs.append(_rms_bwd(dhn * g, xhat, r, DM))
        return [cat1(dhs), cat1(dos)], [cat1(dgs)]

    dhs, do_m, dg_mn = _rowmap(ml_out_bwd, [(dcat, MW, 1)] + hrows, [W["g_mn"]], [(MW, F32), (MW, MX)],
                               [(1, MW)], tile=128, name="ml_out_bwd")
    dqd, dkd, dvd, dgates = _mlstm_bwd(qc, kc, proj, gates_t, W["bg_row"], dhs, cst, nm)
    dqk_m, dconv_w8, dconv_b = _conv_bwd(proj, dqd, dkd, W["conv_w8"], W["conv_b"])

    def do_aug(ao, dov):
        lane = lax.broadcasted_iota(jnp.int32, (ao.shape[0], 128), 1)
        outs = []
        for hh in range(H_MLA):
            sl = hsl(hh, VD)
            dl = jnp.sum(ao[:, sl].astype(F32) * dov[:, sl], axis=1, keepdims=True)
            hi = dl.astype(MX).astype(F32)
            outs += [dov[:, sl], jnp.where(lane == 0, -hi, jnp.where(lane == 1, hi - dl, 0.0))]
        return [cat1(outs)], []

    (doa,) = _rowmap(do_aug, [(attn_o, MW, 0), (dcat, MW, 0)], [], [(H_MLA * HP, MX)], tile=256, name="attn_delta")
    side = [g_out.reshape(N_CHIP, D // N_CHIP, D).astype(MX), _slabs(g_ff1).astype(MX),
            g_ff2.reshape(N_CHIP, DFF // N_CHIP, D).astype(MX)] if late else ()
    dq_a, dk_a, dv_a, late_got = _attn_bwd(qa, kh, va, doa, side=side)

    def mla_q_bwd(dqv, qr, cp, a1, a2, gq):
        outs, dg = [], 0.0
        for hh in range(H_MLA):
            sl = hsl(hh, HP)
            xhat, r = _rms(qr[:, sl], QK)
            d = dqv[:, sl]
            dyv = cat1([d[:, :NOPE], _rope_bwd(d[:, NOPE:], cp, a1, a2)])
            dg = dg + _colsum(dyv * xhat)
            outs.append(_rms_bwd(dyv * gq, xhat, r, QK))
        return [cat1(outs)], [dg]

    dq_raw, dgq = _rowmap(mla_q_bwd, [(dq_a, H_MLA * HP, 0), (q_raw, H_MLA * HP, 0)] + tabs, [W["gq"]],
                          [(H_MLA * HP, MX)], [(1, HP)], tile=128, name="mla_q_bwd")
    dcqn = _mm(dq_raw, W["w_uq"], "nt", name="q_up_dx")
    g_uq = _mm(cqn, dq_raw, "tn", name="q_up_dw")

    def mla_k_bwd(dkv, dvv, kvr, kpe, cp, a1, a2, gk):
        dkn, dg, dkpe = [], 0.0, 0.0
        for hh in range(H_MLA):
            xhat, r = _rms(cat1([kvr[:, hsl(hh, NOPE)], kpe]), QK)
            d = dkv[:, hsl(hh, HP)]
            dyv = cat1([d[:, :NOPE], _rope_bwd(d[:, NOPE:], cp, a1, a2)])
            dg = dg + _colsum(dyv * xhat)
            dxv = _rms_bwd(dyv * gk, xhat, r, QK)
            dkn.append(dxv[:, :NOPE])
            dkpe = dkpe + dxv[:, NOPE:]
        return [cat1(dkn + [dvv[:, hh * HP:hh * HP + VD] for hh in range(H_MLA)]), dkpe], [dg]

    dkv_raw, dkpe, dgk = _rowmap(
        mla_k_bwd, [(dk_a, H_MLA * HP, 0), (dv_a, H_MLA * HP, 0), (kv_raw, 2 * H_MLA * NOPE, 0),
                    (proj, 128, P_KPE // 128)] + tabs, [W["gk"]],
        [(2 * H_MLA * NOPE, MX), (128, MX)], [(1, HP)], tile=128, name="mla_k_bwd")
    dckvn = _mm(dkv_raw, W["w_ukv"], "nt", name="kv_up_dx")
    g_ukv = _mm(ckvn, dkv_raw, "tn", name="kv_up_dw")

    def lora_bwd(dcq, dckv, cq, ckv, gq, gkv):
        xq, rq = _rms(cq, Q_LORA)
        xk, rk = _rms(ckv, KV_LORA)
        return ([_rms_bwd(dcq * gq, xq, rq, Q_LORA), _rms_bwd(dckv * gkv, xk, rk, KV_LORA)],
                [_colsum(dcq * xq), _colsum(dckv * xk)])

    dc_q, dc_kv, dg_qlora, dg_kvlora = _rowmap(
        lora_bwd, [(dcqn, Q_LORA, 0), (dckvn, KV_LORA, 0), (proj, Q_LORA, P_CQ // Q_LORA),
                   (proj, KV_LORA, P_CKV // KV_LORA)], [W["g_qlora"], W["g_kvlora"]],
        [(Q_LORA, MX), (KV_LORA, MX)], [(1, Q_LORA), (1, KV_LORA)], tile=256, name="lora_bwd")

    nc = S // LCH
    dg16 = jnp.stack(dgates)[:, :, :, 0:2, :].transpose(2, 4, 0, 3, 1).reshape(S, NG)
    dg128 = jnp.pad(dg16, ((0, 0), (0, 128 - NG)))

    def assemble(dqk, dv0, dv1, dom, dcq, dckv, dkp, dgp):
        f = lambda t: t.astype(F32)
        return [cat1([f(dqk), dv0 + dv1, f(dom), f(dcq), f(dckv), f(dkp), dgp])], [_colsum(dgp)]

    dproj, dbg = _rowmap(
        assemble, [(dqk_m, 2 * MW, 0), (dvd[0], MW, 0), (dvd[1], MW, 0), (do_m, MW, 0), (dc_q, Q_LORA, 0),
                   (dc_kv, KV_LORA, 0), (dkpe, 128, 0), (dg128, 128, 0)], [], [(D_INP, MX)], [(1, 128)],
        tile=128, name="dproj")
    g_in = _mm(h, dproj, "tn", name="proj_dw")
    early_got = ()
    if late:
        side = [_slabs(_unpad_w_in(g_in)).astype(MX), _slabs(_unpad_w_uq(g_uq)).astype(MX),
                _slabs(_unperm_w_ukv(g_ukv)).astype(MX)]
        dh, early_got = _mm(dproj, W["w_in"], "nt", name="proj_dx", side=side)
    else:
        dh = _mm(dproj, W["w_in"], "nt", name="proj_dx")

    def ln1_bwd(dhv, xv, dx1v, g, mv):
        xhat, r = _rms(xv, D)
        dn = dhv * (1.0 + mv[1:2])
        return [dx1v + _rms_bwd(dn * g, xhat, r, D)], [_colsum(dhv), _colsum(dhv * xhat * g), _colsum(dn * xhat)]

    gx, dshift1, dscale1, dg_mix = _rowmap(ln1_bwd, [(dh, D, 0), (x, D, 0), (dx1, D, 0)], [W["g_mix"], modv],
                                           [(D, F32)], [(1, D)] * 3, tile=128, name="ln1_bwd")
    dmodv = jnp.concatenate([dshift1, dscale1, dgate1, dshift2, dscale2, dgate2], axis=0)
    grads = dict(w_in=g_in, w_uq=g_uq, w_ukv=g_ukv, w_out=g_out, w_ff1=g_ff1, w_ff2=g_ff2,
                 norm_mix_g=dg_mix, b_gates=dbg[:, :NG], conv_w=dconv_w8[:CONVW], conv_b=dconv_b,
                 q_lora_g=dg_qlora, kv_lora_g=dg_kvlora, q_norm_g=dgq[:, :QK], k_norm_g=dgk[:, :QK],
                 mlstm_norm_g=dg_mn, norm_mlp_g=dg_mlp)
    grads["got"] = list(early_got) + list(late_got)
    return loss, gx, dmodv, grads


def _cols(g):
    return g.transpose(1, 0, 2).reshape(g.shape[1], N_CHIP * g.shape[2])


def _slabs(gfull):
    return gfull.reshape(gfull.shape[0], N_CHIP, -1).transpose(1, 0, 2)


def _prep_weights(w_in, w_uq, w_ukv, w_out, w_ff1, w_ff2, norm_mix_g, norm_mlp_g, q_lora_g, kv_lora_g,
                  q_norm_g, k_norm_g, mlstm_norm_g, conv_w, conv_b, b_gates):
    MX = _MXU_DTYPE
    padg = lambda g: jnp.pad(g.reshape(1, QK).astype(F32), ((0, 0), (0, HP - QK)))
    return dict(
        w_in=_pad_w_in(w_in).astype(MX), w_uq=_pad_w_uq(w_uq).astype(MX), w_ukv=_perm_w_ukv(w_ukv).astype(MX),
        w_out=None if w_out is None else w_out.astype(MX), w_ff1=None if w_ff1 is None else w_ff1.astype(MX),
        w_ff2=None if w_ff2 is None else w_ff2.astype(MX),
        g_mix=norm_mix_g.reshape(1, D), g_mlp=norm_mlp_g.reshape(1, D), g_qlora=q_lora_g.reshape(1, Q_LORA),
        g_kvlora=kv_lora_g.reshape(1, KV_LORA), gq=padg(q_norm_g), gk=padg(k_norm_g),
        g_mn=mlstm_norm_g.reshape(1, MW), conv_w8=jnp.pad(conv_w.reshape(CONVW, 2 * MW), ((0, 8 - CONVW), (0, 0))),
        conv_b=conv_b.reshape(1, 2 * MW), bg_row=jnp.pad(b_gates.reshape(1, NG), ((0, 0), (0, 128 - NG))))


MESH = pl.DeviceIdType.MESH
N_DEV = 8
N_CHIP = 4


def _comm_call(body, **kw):
    if _INTERPRET:
        kw["interpret"] = pltpu.InterpretParams()
    return pl.pallas_call(body, **kw)


def _allgather8(blk, *, name):
    m_per, n = blk.shape

    def body(x_ref, out_ref, send_sems, recv_sems, local_sem):
        x, y, c = lax.axis_index("x"), lax.axis_index("y"), lax.axis_index("c")
        me, sibling = (x, y, c), (x, y, 1 - c)
        chips = [(1 - x, y), (x, 1 - y), (1 - x, 1 - y)]

        def rows(px, py, pc):
            return out_ref.at[pl.ds((4 * px + 2 * py + pc) * m_per, m_per), :]

        def copy(k, block, to, src=None):
            return pltpu.make_async_remote_copy(
                src_ref=rows(*block) if src is None else src, dst_ref=rows(*block),
                send_sem=send_sems.at[k], recv_sem=recv_sems.at[k], device_id=to, device_id_type=MESH)

        mine = pltpu.make_async_copy(x_ref, rows(*me), local_sem)
        mine.start()
        first = [copy(0, me, sibling, src=x_ref)]
        first += [copy(1 + j, me, (*chip, c), src=x_ref) for j, chip in enumerate(chips)]
        for cp in first:
            cp.start()
        passed = [copy(4 + j, (*chip, c), sibling) for j, chip in enumerate(chips)]
        for j, chip in enumerate(chips):
            copy(1 + j, (*chip, c), me).wait_recv()
            passed[j].start()
        copy(0, sibling, me).wait_recv()
        for j, chip in enumerate(chips):
            copy(4 + j, (*chip, 1 - c), me).wait_recv()
        for cp in first + passed:
            cp.wait_send()
        mine.wait()

    return _comm_call(
        body, name=name, out_shape=jax.ShapeDtypeStruct((N_DEV * m_per, n), blk.dtype),
        in_specs=[pl.BlockSpec(memory_space=pltpu.VMEM)], out_specs=pl.BlockSpec(memory_space=pltpu.VMEM),
        scratch_shapes=[pltpu.SemaphoreType.DMA((7,)), pltpu.SemaphoreType.DMA((7,)), pltpu.SemaphoreType.DMA],
    )(blk)


def _chip_exchange(arrays, *, gather, name):
    n = len(arrays)

    def body(*refs):
        start, wait = _exchange_ops(refs[:n], refs[n:2 * n], *refs[2 * n:], gather=gather)
        start()
        wait()

    io = _exchange_io(arrays, gather)
    return _comm_call(body, name=name, out_shape=io["out_shape"], in_specs=io["specs"], out_specs=io["specs"],
                      scratch_shapes=io["scratch"])(*arrays)


def _exchange_io(arrays, gather):
    n = len(arrays)
    shard = (lambda a: a.shape) if gather else (lambda a: a.shape[1:])
    return dict(
        specs=[pl.BlockSpec(memory_space=pltpu.HBM)] * n,
        out_shape=[jax.ShapeDtypeStruct((N_CHIP, *shard(a)), a.dtype) for a in arrays],
        scratch=[pltpu.SemaphoreType.DMA((3 * n,)), pltpu.SemaphoreType.DMA((3 * n,)), pltpu.SemaphoreType.DMA((n,))])


def _exchange_ops(ins, outs, send_sems, recv_sems, local_sems, *, gather):
    n = len(ins)
    x, y, c = lax.axis_index("x"), lax.axis_index("y"), lax.axis_index("c")
    k = 2 * x + y
    chips = [(1 - x, y), (x, 1 - y), (1 - x, 1 - y)]

    def remote(a, j):
        px, py = chips[j]
        src = ins[a] if gather else ins[a].at[2 * px + py]
        return pltpu.make_async_remote_copy(
            src_ref=src, dst_ref=outs[a].at[k], send_sem=send_sems.at[3 * a + j],
            recv_sem=recv_sems.at[3 * a + j], device_id=(px, py, c), device_id_type=MESH)

    def arrival(a, j):
        px, py = chips[j]
        src = ins[a] if gather else ins[a].at[k]
        return pltpu.make_async_remote_copy(
            src_ref=src, dst_ref=outs[a].at[2 * px + py], send_sem=send_sems.at[3 * a + j],
            recv_sem=recv_sems.at[3 * a + j], device_id=(px, py, c), device_id_type=MESH)

    local = [pltpu.make_async_copy(ins[a] if gather else ins[a].at[k], outs[a].at[k], local_sems.at[a])
             for a in range(n)]
    sent = [remote(a, j) for a in range(n) for j in range(3)]

    def start():
        for cp in local + sent:
            cp.start()

    def wait():
        for a in range(n):
            for j in range(3):
                arrival(a, j).wait_recv()
        for cp in sent:
            cp.wait_send()
        for cp in local:
            cp.wait()

    return start, wait


def _chip_allgather_halved(shards, *, name):
    n = len(shards)
    half_rows = [s.shape[0] // 2 for s in shards]
    assert all(s.shape[0] % 16 == 0 for s in shards)

    def body(*refs):
        ins, outs = refs[:n], refs[n:2 * n]
        ici_send, ici_recv, d2d_send, d2d_recv, local_sems = refs[2 * n:]
        x, y, c = lax.axis_index("x"), lax.axis_index("y"), lax.axis_index("c")
        k = 2 * x + y
        chips = [(1 - x, y), (x, 1 - y), (1 - x, 1 - y)]

        def half(a, slab, core):
            return outs[a].at[slab, pl.ds(core * half_rows[a], half_rows[a])]

        def ici(a, j, slab):
            px, py = chips[j]
            return pltpu.make_async_remote_copy(
                src_ref=ins[a].at[pl.ds(c * half_rows[a], half_rows[a])], dst_ref=half(a, slab, c),
                send_sem=ici_send.at[3 * a + j], recv_sem=ici_recv.at[3 * a + j],
                device_id=(px, py, c), device_id_type=MESH)

        def d2d(a, j, core):
            px, py = chips[j]
            return pltpu.make_async_remote_copy(
                src_ref=half(a, 2 * px + py, core), dst_ref=half(a, 2 * px + py, core),
                send_sem=d2d_send.at[3 * a + j], recv_sem=d2d_recv.at[3 * a + j],
                device_id=(x, y, 1 - c), device_id_type=MESH)

        local = [pltpu.make_async_copy(ins[a], outs[a].at[k], local_sems.at[a]) for a in range(n)]
        sent = [ici(a, j, k) for a in range(n) for j in range(3)]
        for cp in local + sent:
            cp.start()
        passed = []
        for a in range(n):
            for j, (px, py) in enumerate(chips):
                ici(a, j, 2 * px + py).wait_recv()
                passed.append(d2d(a, j, c))
                passed[-1].start()
        for a in range(n):
            for j in range(3):
                d2d(a, j, 1 - c).wait_recv()
        for cp in sent + passed:
            cp.wait_send()
        for cp in local:
            cp.wait()

    hbm = pl.BlockSpec(memory_space=pltpu.HBM)
    return _comm_call(
        body, name=name, out_shape=[jax.ShapeDtypeStruct((N_CHIP, *s.shape), s.dtype) for s in shards],
        in_specs=[hbm] * n, out_specs=[hbm] * n,
        scratch_shapes=[pltpu.SemaphoreType.DMA((3 * n,))] * 4 + [pltpu.SemaphoreType.DMA((n,))],
    )(*shards)


def _sibling_exchange(arrays, *, name):
    n = len(arrays)

    def body(*refs):
        ins, outs = refs[:n], refs[n:2 * n]
        send_sems, recv_sems = refs[2 * n:]
        x, y, c = lax.axis_index("x"), lax.axis_index("y"), lax.axis_index("c")
        cps = [pltpu.make_async_remote_copy(
            src_ref=ins[a], dst_ref=outs[a], send_sem=send_sems.at[a], recv_sem=recv_sems.at[a],
            device_id=(x, y, 1 - c), device_id_type=MESH) for a in range(n)]
        for cp in cps:
            cp.start()
        for cp in cps:
            cp.wait()

    hbm = pl.BlockSpec(memory_space=pltpu.HBM)
    return _comm_call(
        body, name=name, out_shape=[jax.ShapeDtypeStruct(a.shape, a.dtype) for a in arrays],
        in_specs=[hbm] * n, out_specs=[hbm] * n,
        scratch_shapes=[pltpu.SemaphoreType.DMA((n,)), pltpu.SemaphoreType.DMA((n,))],
    )(*arrays)


def _sum_blocks(a, nblk, *, name):
    n = a.shape[1]

    def body(a_ref, o_ref):
        acc = a_ref[pl.ds(0, 8), :]
        for d in range(1, nblk):
            acc = acc + a_ref[pl.ds(8 * d, 8), :]
        o_ref[...] = acc

    return _call(body, name=name, out_shape=jax.ShapeDtypeStruct((8, n), F32))(a)


def _outer8(sct, dm, *, name, tm=256, tn=1024):
    R, N = sct.shape[0], dm.shape[1]
    tm, tn = min(tm, R), min(tn, N)

    def body(s_ref, d_ref, o_ref):
        s, dmv = s_ref[...], d_ref[...]
        acc = s[:, 0:1] * dmv[0:1, :]
        for b in range(1, 8):
            acc = acc + s[:, b:b + 1] * dmv[b:b + 1, :]
        o_ref[...] = acc

    return _call(
        body, name=name, grid=(R // tm, N // tn),
        in_specs=[pl.BlockSpec((tm, 8), lambda i, j: (i, 0)), pl.BlockSpec((8, tn), lambda i, j: (0, j))],
        out_specs=pl.BlockSpec((tm, tn), lambda i, j: (i, j)),
        out_shape=jax.ShapeDtypeStruct((R, N), F32),
        compiler_params=_cparams(("parallel", "parallel")),
    )(sct, dm)


_BC1 = 1.0 - ADAM_B1 ** ADAM_STEP
_BC2 = 1.0 - ADAM_B2 ** ADAM_STEP


def _adamw(w, g_parts, m, v, *, name, tile=128):
    R, C = w.shape
    tile = min(tile, R)
    assert R % tile == 0
    npart = len(g_parts)

    def body(*refs):
        w_ref, m_ref, v_ref = refs[npart:npart + 3]
        g_o, d_o, m_o, v_o = refs[npart + 3:]
        g = refs[0][...].astype(F32)
        for r in refs[1:npart]:
            g = g + r[...].astype(F32)
        mn = ADAM_B1 * m_ref[...] + (1.0 - ADAM_B1) * g
        vn = ADAM_B2 * v_ref[...] + (1.0 - ADAM_B2) * jnp.square(g)
        g_o[...] = g
        m_o[...] = mn
        v_o[...] = vn
        d_o[...] = -ADAM_LR * ((mn / _BC1) / (jnp.sqrt(vn / _BC2) + ADAM_EPS) + ADAM_WD * w_ref[...])

    spec = pl.BlockSpec((tile, C), lambda i: (i, 0))
    return _call(
        body, name=name, grid=(R // tile,), in_specs=[spec] * (npart + 3), out_specs=[spec] * 4,
        out_shape=[jax.ShapeDtypeStruct((R, C), F32)] * 4,
        compiler_params=_cparams(("parallel",)),
    )(*g_parts, w, m, v)


def _pack(vecs, rows8_cols):
    flat = jnp.concatenate([v.reshape(-1).astype(F32) for v in vecs])
    return jnp.pad(flat, (0, 8 * rows8_cols - flat.shape[0])).reshape(8, rows8_cols)


def _unpack(flat, shapes):
    out, o = [], 0
    for s in shapes:
        n = math.prod(s)
        out.append(flat[o:o + n].reshape(s))
        o += n
    return out


_BIG = ("w_in", "w_uq", "w_ukv", "w_out", "w_ff1", "w_ff2")
_SMALL = ("b_ada", "norm_mix_g", "b_gates", "conv_w", "conv_b", "q_lora_g", "kv_lora_g", "q_norm_g", "k_norm_g",
          "mlstm_norm_g", "norm_mlp_g")
_ORDER = ("w_ada", "b_ada", "norm_mix_g", "w_in", "b_gates", "conv_w", "conv_b", "q_lora_g", "w_uq", "kv_lora_g",
          "w_ukv", "q_norm_g", "k_norm_g", "mlstm_norm_g", "w_out", "norm_mlp_g", "w_ff1", "w_ff2")


def kernel(x, c, positions, w_ada, b_ada, norm_mix_g, w_in, b_gates, conv_w, conv_b, q_lora_g, w_uq, kv_lora_g, w_ukv, q_norm_g, k_norm_g, mlstm_norm_g, w_out, norm_mlp_g, w_ff1, w_ff2, loss_target, m_w_ada, m_b_ada, m_norm_mix_g, m_w_in, m_b_gates, m_conv_w, m_conv_b, m_q_lora_g, m_w_uq, m_kv_lora_g, m_w_ukv, m_q_norm_g, m_k_norm_g, m_mlstm_norm_g, m_w_out, m_norm_mlp_g, m_w_ff1, m_w_ff2, v_w_ada, v_b_ada, v_norm_mix_g, v_w_in, v_b_gates, v_conv_w, v_conv_b, v_q_lora_g, v_w_uq, v_kv_lora_g, v_w_ukv, v_q_norm_g, v_k_norm_g, v_mlstm_norm_g, v_w_out, v_norm_mlp_g, v_w_ff1, v_w_ff2):
    args = dict(locals())
    wts = {n: args[n] for n in _ORDER}
    mom = {n: args["m_" + n] for n in _ORDER}
    var = {n: args["v_" + n] for n in _ORDER}
    MX = _MXU_DTYPE
    xi, yi, ci = lax.axis_index("x"), lax.axis_index("y"), lax.axis_index("c")
    chip = 2 * xi + yi
    dev = 2 * chip + ci
    S = x.shape[1]
    CS = 2 * MW // N_CHIP
    GS = DM // N_CHIP

    pk = _pack([c, conv_w, mlstm_norm_g], 1024)
    allpk = _allgather8(pk, name="gather_small").reshape(N_DEV, 8 * 1024)
    c_all = allpk[:, :D]
    per_chip = allpk[0::2]
    conv_w_full = per_chip[:, D:D + CONVW * CS].reshape(N_CHIP, CONVW, CS).transpose(1, 0, 2).reshape(CONVW, 2 * MW)
    o = D + CONVW * CS
    mn_full = per_chip[:, o:o + HM * GS].reshape(N_CHIP, HM, GS).transpose(1, 0, 2).reshape(HM, DM)

    (sc,) = _rowmap(lambda cv: ([cv * _sigmoid(cv)], []), [(c_all, D, 0)], [], [(D, F32)], tile=8, name="silu_c")
    ncol = w_ada.shape[2]
    b_cols = lax.dynamic_slice(b_ada, (0, chip * ncol), (1, ncol))
    modp = _mm(sc, w_ada[0], "nn", name="ada_fwd", tm=8, tn=1024, tk=512, extras=(jnp.broadcast_to(b_cols, (8, ncol)),),
               epilogue=lambda r, b: (r + b,))
    modg = _allgather8(modp, name="gather_mod").reshape(N_CHIP, 2, 8, ncol)[:, 0]
    mod_all = modg.transpose(1, 0, 2).reshape(N_DEV, N_CHIP * ncol)
    modv = jnp.pad(lax.dynamic_slice(mod_all, (dev, 0), (1, 6 * D)).reshape(6, D), ((0, 2), (0, 0)))

    shards = [wts[n][0].astype(MX) for n in _BIG]
    gw_in, gw_uq, gw_ukv = _chip_allgather_halved(shards[:3], name="gather_weights")
    W = _prep_weights(_cols(gw_in), _cols(gw_uq), _cols(gw_ukv), None, None, None, norm_mix_g, norm_mlp_g,
                      q_lora_g, kv_lora_g, q_norm_g, k_norm_g, mn_full, conv_w_full, conv_b, b_gates)

    loss, gx, dmodv, g = _device_step(x[0], loss_target[0], positions[0], modv, W, late=shards[3:])

    small_shapes = [(6 * D,), (D,), (NG,), (CONVW, 2 * MW), (2 * MW,), (Q_LORA,), (KV_LORA,), (QK,), (QK,), (MW,), (D,), (1,)]
    pg = _pack([dmodv, g["norm_mix_g"], g["b_gates"], g["conv_w"], g["conv_b"], g["q_lora_g"], g["kv_lora_g"],
                g["q_norm_g"], g["k_norm_g"], g["mlstm_norm_g"], g["norm_mlp_g"], loss], 4096)
    allpg = _allgather8(pg, name="gather_small_grads")
    tot = _unpack(_sum_blocks(allpg, N_DEV, name="sum_small_grads").reshape(-1), small_shapes)
    dmod_all = allpg.reshape(N_DEV, 8 * 4096)[:, :6 * D]
    gsmall = dict(zip(_SMALL, [tot[0].reshape(1, 6 * D), tot[1].reshape(1, D), tot[2].reshape(1, NG),
                               lax.dynamic_slice(tot[3], (0, chip * CS), (CONVW, CS)).reshape(1, CONVW, CS),
                               tot[4].reshape(1, 2 * MW), tot[5].reshape(1, Q_LORA), tot[6].reshape(1, KV_LORA),
                               tot[7].reshape(1, QK), tot[8].reshape(1, QK),
                               lax.dynamic_slice(tot[9].reshape(HM, DM), (0, chip * GS), (HM, GS)).reshape(1, HM, GS),
                               tot[10].reshape(1, D)]))
    loss_tot = tot[11].reshape(())

    got = g["got"]
    part = []
    for nme, r in zip(_BIG, got):
        wd = r.shape[2]
        (p,) = _rowmap(lambda a0, a1, a2, a3: ([(a0.astype(F32) + a1.astype(F32)) + (a2.astype(F32) + a3.astype(F32))], []),
                       [(r, wd, 0, k) for k in range(N_CHIP)], [], [(wd, F32)], tile=256, name="sum_chips_" + nme)
        part.append(p)
    other = _sibling_exchange(part, name="exchange_cores")

    dm_cols = lax.dynamic_slice(dmod_all, (0, chip * ncol), (N_DEV, ncol))
    g_ada = _outer8(sc.T, dm_cols, name="ada_dw")

    res = {}
    for nme, p, q in zip(_BIG, part, other):
        res[nme] = _adamw(wts[nme][0], [p, q], mom[nme][0], var[nme][0], name="adamw_" + nme)
    res["w_ada"] = _adamw(w_ada[0], [g_ada], m_w_ada[0], v_w_ada[0], name="adamw_w_ada")
    sw = _pack([wts[n] for n in _SMALL], 3072)
    sg = _pack([gsmall[n] for n in _SMALL], 3072)
    sm = _pack([mom[n] for n in _SMALL], 3072)
    sv = _pack([var[n] for n in _SMALL], 3072)
    small_res = _adamw(sw, [sg], sm, sv, name="adamw_small", tile=8)
    shapes = [wts[n].shape for n in _SMALL]
    unp = [_unpack(r.reshape(-1), shapes) for r in small_res]
    for i, nme in enumerate(_SMALL):
        res[nme] = tuple(u[i] for u in unp)
    outs = [loss_tot, gx[None]]
    for kind in range(4):
        outs += [res[n][kind].reshape(wts[n].shape) for n in _ORDER]
    return tuple(outs)
```

```python
import functools
import math

import jax
import jax.numpy as jnp
from jax import lax
from jax.experimental import pallas as pl
from jax.experimental.pallas import tpu as pltpu

F32 = jnp.float32
BF16 = jnp.bfloat16
_MXU_DTYPE = jnp.bfloat16
_INTERPRET = False

D = 2048
H_MLA = 8
NOPE = 128
ROPE = 64
QK = NOPE + ROPE
HP = 256
VD = 128
Q_LORA = 512
KV_LORA = 256
HM = 4
DM = 256
MW = HM * DM
LCH = 128
CONVW = 5
NG = 16
DFF = 4 * D
EPS = 1e-6
M_INIT = -1e30
ROPE_THETA = 10000.0
IN_SIZES = (Q_LORA, KV_LORA, ROPE, MW, MW, MW, MW, NG)
D_IN = sum(IN_SIZES)
P_QM, P_KM, P_VM, P_OM, P_CQ, P_CKV, P_KPE, P_G = 0, 1024, 2048, 3072, 4096, 4608, 4864, 4992
D_INP = 5120

ADAM_LR, ADAM_B1, ADAM_B2, ADAM_EPS, ADAM_WD, ADAM_STEP = 0.001, 0.9, 0.999, 1e-08, 0.01, 10

V7X_VMEM_LIMIT = 56 * 1024 * 1024


def _cparams(sem):
    return pltpu.CompilerParams(dimension_semantics=sem, vmem_limit_bytes=V7X_VMEM_LIMIT)


def _call(body, **kw):
    if _INTERPRET:
        kw.pop("compiler_params", None)
        kw["interpret"] = pltpu.InterpretParams()
    return pl.pallas_call(body, **kw)


def _dot(a, b, form):
    dims = {"nn": ((1,), (0,)), "nt": ((1,), (1,)), "tn": ((0,), (0,))}[form]
    return lax.dot_general(a.astype(_MXU_DTYPE), b.astype(_MXU_DTYPE), (dims, ((), ())),
                           preferred_element_type=F32)


def _mm(a, b, form, *, name, out_dtypes=(F32,), epilogue=None, extras=(), tm=1024, tn=1024, tk=2048, side=()):
    if form == "nn":
        (M, K), (K2, N) = a.shape, b.shape
    elif form == "nt":
        (M, K), (N, K2) = a.shape, b.shape
    else:
        (K, M), (K2, N) = a.shape, b.shape
    assert K == K2, (a.shape, b.shape, form)
    tm, tn = min(tm, M), min(tn, N)
    tk = max(d for d in range(128, min(tk, K) + 1, 128) if K % d == 0) if K > 128 else K
    assert M % tm == 0 and N % tn == 0 and K % tk == 0, (M, N, K, tm, tn, tk)
    nk = K // tk
    ne, no = len(extras), len(out_dtypes)
    if form == "tn":
        a_spec = pl.BlockSpec((tk, tm), lambda i, j, k: (k, i))
    else:
        a_spec = pl.BlockSpec((tm, tk), lambda i, j, k: (i, k))
    if form == "nt":
        b_spec = pl.BlockSpec((tn, tk), lambda i, j, k: (j, k))
    else:
        b_spec = pl.BlockSpec((tk, tn), lambda i, j, k: (k, j))
    mn_spec = pl.BlockSpec((tm, tn), lambda i, j, k: (i, j))
    grid = (M // tm, N // tn, nk)
    ns, io, wrap = _side_exchange(side, False, grid)

    def body(a_ref, b_ref, *rest):
        ex, outs = rest[:ne], rest[ne + ns:ne + ns + no]
        scratch = rest[ne + 2 * ns + no:]
        side_start, side_wait = wrap(rest[ne:ne + ns], rest[ne + ns + no:ne + 2 * ns + no], scratch[1:])
        side_start()
        prod = _dot(a_ref[...], b_ref[...], form)

        def finish(r):
            vals = (r,) if epilogue is None else epilogue(r, *[e[...] for e in ex])
            for o, v in zip(outs, vals):
                o[...] = v.astype(o.dtype)

        if nk == 1:
            finish(prod)
        else:
            acc, k = scratch[0], pl.program_id(2)

            @pl.when(k == 0)
            def _():
                acc[...] = prod

            @pl.when(k > 0)
            def _():
                acc[...] += prod

            @pl.when(k == nk - 1)
            def _():
                finish(acc[...])
        side_wait()

    res = _call(
        body, name=name, grid=grid,
        in_specs=[a_spec, b_spec] + [mn_spec] * ne + io["specs"],
        out_specs=[mn_spec] * no + io["specs"],
        out_shape=[jax.ShapeDtypeStruct((M, N), dt) for dt in out_dtypes] + io["out_shape"],
        scratch_shapes=[pltpu.VMEM((tm, tn) if nk > 1 else (8, 128), F32)] + io["scratch"],
        compiler_params=_cparams(("arbitrary",) * 3 if ns else ("parallel", "parallel", "arbitrary")),
    )(a, b, *extras, *side)
    if ns:
        return (res[0] if no == 1 else res[:no]), list(res[no:])
    return res[0] if no == 1 else res


def _rowmap(fn, rows, bcasts, outs, accs=(), *, tile, name):
    rows = [r if len(r) == 4 else (*r, None) for r in rows]
    S = rows[0][0].shape[-2]
    tile = min(tile, S)
    assert S % tile == 0
    nr, nb, no, na = len(rows), len(bcasts), len(outs), len(accs)

    def body(*refs):
        vals = [r[...] for r in refs[:nr + nb]]
        o_refs, a_refs = refs[nr + nb:nr + nb + no], refs[nr + nb + no:]
        o_vals, a_vals = fn(*vals)
        for r, v in zip(o_refs, o_vals):
            r[...] = v.astype(r.dtype)
        if na:
            @pl.when(pl.program_id(0) == 0)
            def _():
                for r in a_refs:
                    r[...] = jnp.zeros(r.shape, r.dtype)
            for r, v in zip(a_refs, a_vals):
                r[...] += v

    in_specs = []
    for (arr, w, cb, lead) in rows:
        if lead is None:
            in_specs.append(pl.BlockSpec((tile, w), lambda i, cb=cb: (i, cb)))
        else:
            in_specs.append(pl.BlockSpec((None, tile, w), lambda i, cb=cb, lead=lead: (lead, i, cb)))
    in_specs += [pl.BlockSpec(b.shape, lambda i: (0, 0)) for b in bcasts]
    out_specs = [pl.BlockSpec((tile, w), lambda i: (i, 0)) for (w, _) in outs]
    out_specs += [pl.BlockSpec(s, lambda i: (0, 0)) for s in accs]
    out_shape = [jax.ShapeDtypeStruct((S, w), dt) for (w, dt) in outs]
    out_shape += [jax.ShapeDtypeStruct(s, F32) for s in accs]
    return _call(
        body, name=name, grid=(S // tile,), in_specs=in_specs, out_specs=out_specs, out_shape=out_shape,
        compiler_params=_cparams(("arbitrary",)),
    )(*[r[0] for r in rows], *bcasts)


def _colsum(v):
    return jnp.sum(v, axis=0, keepdims=True)


def _rms(x, n):
    r = lax.rsqrt(jnp.sum(x * x, axis=-1, keepdims=True) * (1.0 / n) + EPS)
    return x * r, r


def _rms_bwd(dxhat, xhat, r, n):
    return r * (dxhat - xhat * (jnp.sum(dxhat * xhat, axis=-1, keepdims=True) * (1.0 / n)))


def _rope_fwd(r, cosp, s1, s2):
    return r * cosp + pltpu.roll(r, 32, 1) * s1 + pltpu.roll(r, 96, 1) * s2


def _rope_bwd(d, cosp, s1, s2):
    return d * cosp + pltpu.roll(d * s1, 96, 1) + pltpu.roll(d * s2, 32, 1)


def _sigmoid(x):
    return 1.0 / (1.0 + jnp.exp(-x))


def _halo_specs(tile, halo, width, cb, S, lead=None):
    nh = tile // halo
    last = S // halo - 1
    if lead is None:
        return [
            pl.BlockSpec((tile, width), lambda i: (i, cb)),
            pl.BlockSpec((halo, width), lambda i: (jnp.maximum(i * nh - 1, 0), cb)),
            pl.BlockSpec((halo, width), lambda i: (jnp.minimum((i + 1) * nh, last), cb)),
        ]
    return [
        pl.BlockSpec((None, tile, width), lambda i: (lead, i, cb)),
        pl.BlockSpec((None, halo, width), lambda i: (lead, jnp.maximum(i * nh - 1, 0), cb)),
        pl.BlockSpec((None, halo, width), lambda i: (lead, jnp.minimum((i + 1) * nh, last), cb)),
    ]


def _conv_fwd(proj, conv_w8, conv_b, *, tile=256):
    S = proj.shape[0]
    T = min(tile, S)
    n = S // T
    W = 2 * MW

    def body(x_ref, xp_ref, xn_ref, w_ref, b_ref, q_ref, k_ref, ext):
        i = pl.program_id(0)
        ext[pl.ds(0, 8), :] = xp_ref[...] * (i > 0).astype(F32)
        ext[pl.ds(8, T), :] = x_ref[...]
        ext[pl.ds(8 + T, 8), :] = xn_ref[...] * (i < n - 1).astype(F32)
        w = w_ref[...]
        y = b_ref[...] + w[0:1, :] * ext[pl.ds(6, T), :]
        for o in range(1, CONVW):
            y = y + w[o:o + 1, :] * ext[pl.ds(6 + o, T), :]
        y = y * _sigmoid(y)
        q_ref[...] = y[:, :MW].astype(q_ref.dtype)
        k_ref[...] = (y[:, MW:] * (DM ** -0.5)).astype(k_ref.dtype)

    return _call(
        body, name="conv_fwd", grid=(n,),
        in_specs=_halo_specs(T, 8, W, 0, S) + [pl.BlockSpec((8, W), lambda i: (0, 0)),
                                                 pl.BlockSpec((1, W), lambda i: (0, 0))],
        out_specs=[pl.BlockSpec((T, MW), lambda i: (i, 0))] * 2,
        out_shape=[jax.ShapeDtypeStruct((S, MW), _MXU_DTYPE)] * 2,
        scratch_shapes=[pltpu.VMEM((T + 16, W), F32)],
        compiler_params=_cparams(("arbitrary",)),
    )(proj, proj, proj, conv_w8, conv_b)


def _conv_bwd(proj, dqd, dkd, conv_w8, conv_b, *, tile=256):
    S = proj.shape[0]
    T = min(tile, S)
    n = S // T
    W = 2 * MW

    def body(x_ref, xp_ref, xn_ref, *rest):
        g = rest[:12]
        w_ref, b_ref, dx_ref, dw_ref, db_ref, ext, edp = rest[12:]
        i = pl.program_id(0)
        mp = (i > 0).astype(F32)
        mn = (i < n - 1).astype(F32)
        ext[pl.ds(0, 16), :] = xp_ref[...] * mp
        ext[pl.ds(16, T), :] = x_ref[...]
        ext[pl.ds(16 + T, 16), :] = xn_ref[...] * mn
        w = w_ref[...]
        pre = b_ref[...] + w[0:1, :] * ext[pl.ds(6, T + 16), :]
        for o in range(1, CONVW):
            pre = pre + w[o:o + 1, :] * ext[pl.ds(6 + o, T + 16), :]
        sg = _sigmoid(pre)
        dsilu = sg * (1.0 + pre * (1.0 - sg))
        for half, (a0, a1) in enumerate(((g[0:3], g[3:6]), (g[6:9], g[9:12]))):
            sc = 1.0 if half == 0 else DM ** -0.5
            cols = pl.ds(half * MW, MW)
            edp[pl.ds(0, 8), cols] = (a0[1][...] + a1[1][...]) * (mp * sc)
            edp[pl.ds(8, T), cols] = (a0[0][...] + a1[0][...]) * sc
            edp[pl.ds(8 + T, 8), cols] = (a0[2][...] + a1[2][...]) * (mn * sc)
        edp[...] = edp[...] * dsilu
        dpm = edp[pl.ds(8, T), :]
        dx = w[0:1, :] * edp[pl.ds(10, T), :]
        for o in range(1, CONVW):
            dx = dx + w[o:o + 1, :] * edp[pl.ds(10 - o, T), :]
        dx_ref[...] = dx.astype(dx_ref.dtype)

        @pl.when(i == 0)
        def _():
            dw_ref[...] = jnp.zeros(dw_ref.shape, F32)
            db_ref[...] = jnp.zeros(db_ref.shape, F32)

        for o in range(CONVW):
            dw_ref[pl.ds(o, 1), :] += _colsum(ext[pl.ds(14 + o, T), :] * dpm)
        db_ref[...] += _colsum(dpm)

    gspecs = _halo_specs(T, 8, MW, 0, S) * 4
    return _call(
        body, name="conv_bwd", grid=(n,),
        in_specs=_halo_specs(T, 16, W, 0, S) + gspecs + [pl.BlockSpec((8, W), lambda i: (0, 0)),
                                                          pl.BlockSpec((1, W), lambda i: (0, 0))],
        out_specs=[pl.BlockSpec((T, W), lambda i: (i, 0)), pl.BlockSpec((8, W), lambda i: (0, 0)),
                   pl.BlockSpec((1, W), lambda i: (0, 0))],
        out_shape=[jax.ShapeDtypeStruct((S, W), _MXU_DTYPE), jax.ShapeDtypeStruct((8, W), F32),
                   jax.ShapeDtypeStruct((1, W), F32)],
        scratch_shapes=[pltpu.VMEM((T + 32, W), F32), pltpu.VMEM((T + 16, W), F32)],
        compiler_params=_cparams(("arbitrary",)),
    )(proj, proj, proj, *([dqd[0]] * 3), *([dqd[1]] * 3), *([dkd[0]] * 3), *([dkd[1]] * 3), conv_w8, conv_b)


_ATT_SCALE = QK ** -0.5
_LOG2E = math.log2(math.e)
_Q_PRESCALE = _ATT_SCALE * _LOG2E


def _side_exchange(side, gather, grid, cols=()):
    ns = len(side)
    io = _exchange_io(side, gather, cols) if ns else dict(specs=[], out_shape=[], scratch=[])

    def wrap(refs_in, refs_out, sems):
        if not ns:
            return (lambda: None), (lambda: None)
        start, wait = _exchange_ops(refs_in, refs_out, *sems, gather=gather, cols=cols)
        ids = [pl.program_id(a) for a in range(len(grid))]
        first = functools.reduce(jnp.logical_and, [i == 0 for i in ids])
        last = functools.reduce(jnp.logical_and, [i == g - 1 for i, g in zip(ids, grid)])
        return (lambda: pl.when(first)(start)), (lambda: pl.when(last)(wait))

    return ns, io, wrap


def _attn_fwd(q, k, v, *, side=(), side_cols=(), tq=1024, tk=8192, split=4, unroll=1):
    S = q.shape[0]
    tq, tk = min(tq, S), min(tk, S)
    nkv = S // tk
    hq = tq // split
    grid = (H_MLA, S // tq)
    ns, io, wrap = _side_exchange(side, True, grid, side_cols)

    def body(q_ref, k_ref, v_ref, *rest):
        o_ref, qa_ref = rest[ns:ns + 2]
        m_s, acc_s = rest[2 * ns + 2:2 * ns + 4]
        side_start, side_wait = wrap(rest[:ns], rest[ns + 2:2 * ns + 2], rest[2 * ns + 4:])
        side_start()
        m_s[...] = jnp.full(m_s.shape, -1e30, F32)
        acc_s[...] = jnp.zeros(acc_s.shape, F32)

        def step(j, carry):
            rows = pl.ds(pl.multiple_of(j * tk, tk), tk)
            kj, vj = k_ref[rows, :], v_ref[rows, :]
            for a in range(split):
                r = pl.ds(a * hq, hq)
                s = _dot(q_ref[r, :], kj, "nt")
                m_old = m_s[r, :]
                m_new = jnp.maximum(m_old, jnp.max(s, axis=1, keepdims=True))
                p = jnp.exp2(s - m_new)
                acc_s[r, :] = jnp.exp2(m_old - m_new) * acc_s[r, :] + _dot(p, vj, "nn")
                m_s[r, :] = m_new
            return carry

        lax.fori_loop(0, nkv, step, 0, unroll=unroll if nkv % unroll == 0 else 1)
        l = acc_s[:, VD:VD + 1]
        o_ref[...] = (acc_s[:, :VD] / l).astype(o_ref.dtype)
        lse = m_s[...] + jnp.log2(l)
        hi = lse.astype(_MXU_DTYPE).astype(F32)
        lane = lax.broadcasted_iota(jnp.int32, (tq, HP), 1)
        qa = jnp.where(lane == QK, -hi, jnp.where(lane == QK + 1, hi - lse, q_ref[...].astype(F32)))
        qa_ref[...] = qa.astype(qa_ref.dtype)
        side_wait()

    res = _call(
        body, name="attn_fwd", grid=grid,
        in_specs=[pl.BlockSpec((tq, HP), lambda h, i: (i, h)),
                  pl.BlockSpec((S, HP), lambda h, i: (0, h)),
                  pl.BlockSpec((S, HP), lambda h, i: (0, h))] + io["specs"],
        out_specs=[pl.BlockSpec((tq, VD), lambda h, i: (i, h)),
                   pl.BlockSpec((tq, HP), lambda h, i: (i, h))] + io["specs"],
        out_shape=[jax.ShapeDtypeStruct((S, H_MLA * VD), _MXU_DTYPE),
                   jax.ShapeDtypeStruct((S, H_MLA * HP), _MXU_DTYPE)] + io["out_shape"],
        scratch_shapes=[pltpu.VMEM((tq, 1), F32), pltpu.VMEM((tq, HP), F32)] + io["scratch"],
        compiler_params=_cparams(("arbitrary", "arbitrary")),
    )(q, k, v, *side)
    return res[0], res[1], list(res[2:])


def _attn_bwd(qa, k, va, doa, *, side=(), side_cols=(), tq=4096, tk=512, split=4, unroll=1):
    S = qa.shape[0]
    tq, tk = min(tq, S), min(tk, S)
    nq, nkb = S // tq, S // tk
    hq = tq // split
    grid = (H_MLA, nkb)
    ns, io, wrap = _side_exchange(side, False, grid, side_cols)

    def body(q_ref, k_ref, v_ref, do_ref, *rest):
        dq_ref, dk_ref, dv_ref = rest[ns:ns + 3]
        side_start, side_wait = wrap(rest[:ns], rest[ns + 3:2 * ns + 3], rest[2 * ns + 3:])
        side_start()
        j = pl.program_id(1)

        @pl.when(j == 0)
        def _():
            dq_ref[...] = jnp.zeros(dq_ref.shape, F32)

        dk_ref[...] = jnp.zeros(dk_ref.shape, F32)
        dv_ref[...] = jnp.zeros(dv_ref.shape, F32)
        kb, vb = k_ref[...], v_ref[...]

        def step(i, carry):
            for a in range(split):
                r = pl.ds(pl.multiple_of(i * tq + a * hq, hq), hq)
                qg, dog = q_ref[r, :], do_ref[r, :]
                p = jnp.exp2(_dot(qg, kb, "nt"))
                ds = (p * _dot(dog, vb, "nt")).astype(_MXU_DTYPE)
                dq_ref[r, :] += _dot(ds, kb, "nn")
                dv_ref[...] += _dot(p, dog, "tn")
                dk_ref[...] += _dot(ds, qg, "tn")
            return carry

        lax.fori_loop(0, nq, step, 0, unroll=unroll if nq % unroll == 0 else 1)
        dk_ref[...] = dk_ref[...] * (1.0 / _LOG2E)

        @pl.when(j == nkb - 1)
        def _():
            dq_ref[...] = dq_ref[...] * _ATT_SCALE

        side_wait()

    blk = pl.BlockSpec((tk, HP), lambda h, j: (j, h))
    whole = pl.BlockSpec((S, HP), lambda h, j: (0, h))
    res = _call(
        body, name="attn_bwd", grid=grid,
        in_specs=[whole, blk, blk, whole] + io["specs"],
        out_specs=[whole, blk, blk] + io["specs"],
        out_shape=[jax.ShapeDtypeStruct((S, H_MLA * HP), F32)] * 3 + io["out_shape"],
        scratch_shapes=io["scratch"],
        compiler_params=_cparams(("arbitrary", "arbitrary")),
    )(qa, k, va, doa, *side)
    return res[0], res[1], res[2], list(res[3:])


def _mlstm_chunk_terms(g, q, k, v, gates, gates_t, bg_row, C, n_row, m):
    L = LCH
    d = g // HM
    h = g % HM
    i_idx = d * 8 + h
    f_idx = d * 8 + 4 + h
    rr = lax.broadcasted_iota(jnp.int32, (L, L), 0)
    cc = lax.broadcasted_iota(jnp.int32, (L, L), 1)
    order = (rr - cc) * (1 - 2 * d)
    tri = order >= 0
    eye = rr == cc
    lane = lax.broadcasted_iota(jnp.int32, gates.shape, 1)
    sub = lax.broadcasted_iota(jnp.int32, gates_t.shape, 0)
    lane_b = lax.broadcasted_iota(jnp.int32, bg_row.shape, 1)
    pick_c = lambda idx: jnp.sum(jnp.where(lane == idx, gates, 0.0), axis=1, keepdims=True)
    pick_r = lambda idx: jnp.sum(jnp.where(sub == idx, gates_t, 0.0), axis=0, keepdims=True)
    pick_b = lambda idx: jnp.sum(jnp.where(lane_b == idx, bg_row, 0.0), axis=1, keepdims=True)
    i_col, i_row = pick_c(i_idx) + pick_b(i_idx), pick_r(i_idx) + pick_b(i_idx)
    f_col, f_row = pick_c(f_idx) + pick_b(f_idx), pick_r(f_idx) + pick_b(f_idx)
    logsig = lambda x: jnp.minimum(x, 0.0) - jnp.log(1.0 + jnp.exp(-jnp.abs(x)))
    lf_col, lf_row = logsig(f_col), logsig(f_row)
    b_col = jnp.sum(jnp.where(tri, lf_row, 0.0), axis=1, keepdims=True)
    tri_t = order <= 0
    b_row = jnp.sum(jnp.where(tri_t, lf_col, 0.0), axis=0, keepdims=True)
    bL = jnp.sum(lf_row, axis=1, keepdims=True)
    log_inter = b_col + m
    logD = jnp.where(tri, b_col - b_row + i_row, -jnp.inf)
    m_t = jnp.maximum(log_inter, jnp.max(logD, axis=1, keepdims=True))
    Dm = jnp.exp(logD - m_t)
    w_inter = jnp.exp(log_inter - m_t)
    A = _dot(q, k, "nt")
    Sc = A * Dm
    numI = _dot(q, C, "nt")
    qf = q.astype(F32)
    kf = k.astype(F32)
    denI = jnp.sum(qf * n_row, axis=1, keepdims=True)
    num = _dot(Sc, v, "nn") + w_inter * numI
    den = jnp.sum(Sc, axis=1, keepdims=True) + w_inter * denI
    floor = jnp.exp(-m_t)
    Nst = jnp.maximum(jnp.abs(den), floor)
    log_w = bL - b_col + i_col
    m_new = jnp.maximum(bL + m, jnp.max(log_w, axis=0, keepdims=True))
    decay = jnp.exp(bL + m - m_new)
    w_col = jnp.exp(log_w - m_new)
    return dict(tri=tri, eye=eye, f_row=f_row, Dm=Dm, w_inter=w_inter, A=A, Sc=Sc, numI=numI, denI=denI,
                num=num, den=den, floor=floor, Nst=Nst, m_new=m_new, decay=decay, w_col=w_col, qf=qf, kf=kf)


def _mlstm_specs(nc, d, step_of):
    chunk = lambda j: step_of(j) if d == 0 else nc - 1 - step_of(j)
    return chunk, [
        pl.BlockSpec((LCH, DM), lambda h, j: (chunk(j), h)),
        pl.BlockSpec((LCH, DM), lambda h, j: (chunk(j), h)),
        pl.BlockSpec((LCH, DM), lambda h, j: (chunk(j), P_VM // DM + h)),
        pl.BlockSpec((LCH, 128), lambda h, j: (chunk(j), P_G // 128)),
        pl.BlockSpec((NG, LCH), lambda h, j: (0, chunk(j))),
    ]


def _mlstm_fwd(qc, kc, proj, gates_t, bg_row):
    S = qc.shape[0]
    nc = S // LCH
    in_specs, out_specs = [], []
    for d in (0, 1):
        chunk, specs = _mlstm_specs(nc, d, lambda j: j)
        in_specs += specs
        out_specs += [pl.BlockSpec((LCH, DM), lambda h, j, chunk=chunk: (chunk(j), h)),
                      pl.BlockSpec((None, None, DM, DM), lambda h, j, chunk=chunk: (h, chunk(j), 0, 0)),
                      pl.BlockSpec((None, None, 8, DM), lambda h, j, chunk=chunk: (h, chunk(j), 0, 0))]
    in_specs.append(pl.BlockSpec((1, 128), lambda h, j: (0, 0)))

    def body(*refs):
        bg_ref, outs, (C_s, n_s, m_s) = refs[10], refs[11:17], refs[17:]

        @pl.when(pl.program_id(1) == 0)
        def _():
            C_s[...] = jnp.zeros(C_s.shape, F32)
            n_s[...] = jnp.zeros(n_s.shape, F32)
            m_s[...] = jnp.full(m_s.shape, M_INIT, F32)

        for d in (0, 1):
            q_ref, k_ref, v_ref, g_ref, gt_ref = refs[5 * d:5 * d + 5]
            h_ref, cst_ref, nm_ref = outs[3 * d:3 * d + 3]
            g = d * HM + pl.program_id(0)
            C, n_row, m = C_s[d], n_s[d, 0:1, :], m_s[d, 0:1, 0:1]
            cst_ref[...] = C
            nm_ref[0:1, :] = n_row
            nm_ref[1:2, :] = jnp.broadcast_to(m, (1, DM))
            nm_ref[2:8, :] = jnp.zeros((6, DM), F32)
            q, k, v = q_ref[...], k_ref[...], v_ref[...]
            t = _mlstm_chunk_terms(g, q, k, v, g_ref[...], gt_ref[...], bg_ref[...], C, n_row, m)
            h_ref[...] = t["num"] / t["Nst"]
            wv = t["w_col"] * v
            C_s[d] = t["decay"] * C + _dot(wv, k, "tn")
            n_s[d, 0:1, :] = t["decay"] * n_row + _colsum(t["w_col"] * t["kf"])
            m_s[d] = jnp.broadcast_to(t["m_new"], (8, 128))

    res = _call(
        body, name="mlstm_fwd", grid=(HM, nc), in_specs=in_specs, out_specs=out_specs,
        out_shape=[jax.ShapeDtypeStruct((S, MW), F32), jax.ShapeDtypeStruct((HM, nc, DM, DM), F32),
                   jax.ShapeDtypeStruct((HM, nc, 8, DM), F32)] * 2,
        scratch_shapes=[pltpu.VMEM((2, DM, DM), F32), pltpu.VMEM((2, 8, DM), F32), pltpu.VMEM((2, 8, 128), F32)],
        compiler_params=_cparams(("parallel", "arbitrary")),
    )(*([qc, kc, proj, proj, gates_t] * 2), bg_row)
    return (res[0], res[3]), (res[1], res[4]), (res[2], res[5])


def _mlstm_bwd(qc, kc, proj, gates_t, bg_row, dh, cst, nm):
    S = qc.shape[0]
    nc = S // LCH
    in_specs, out_specs = [], []
    for d in (0, 1):
        chunk, specs = _mlstm_specs(nc, d, lambda j: nc - 1 - j)
        in_specs += specs + [pl.BlockSpec((LCH, DM), lambda h, j, chunk=chunk: (chunk(j), h)),
                             pl.BlockSpec((None, None, DM, DM), lambda h, j, chunk=chunk: (h, chunk(j), 0, 0)),
                             pl.BlockSpec((None, None, 8, DM), lambda h, j, chunk=chunk: (h, chunk(j), 0, 0))]
        out_specs += [pl.BlockSpec((LCH, DM), lambda h, j, chunk=chunk: (chunk(j), h))] * 3
        out_specs += [pl.BlockSpec((None, None, 8, LCH), lambda h, j, chunk=chunk: (h, chunk(j), 0, 0))]
    in_specs.append(pl.BlockSpec((1, 128), lambda h, j: (0, 0)))

    def body(*refs):
        bg_ref, outs, (dC_s, dn_s) = refs[16], refs[17:25], refs[25:]

        @pl.when(pl.program_id(1) == 0)
        def _():
            dC_s[...] = jnp.zeros(dC_s.shape, F32)
            dn_s[...] = jnp.zeros(dn_s.shape, F32)

        for d in (0, 1):
            _mlstm_bwd_chain(d, refs[8 * d:8 * d + 8], bg_ref, outs[4 * d:4 * d + 4], dC_s, dn_s)

    res = _call(
        body, name="mlstm_bwd", grid=(HM, nc), in_specs=in_specs, out_specs=out_specs,
        out_shape=([jax.ShapeDtypeStruct((S, MW), F32)] * 3 + [jax.ShapeDtypeStruct((HM, nc, 8, LCH), F32)]) * 2,
        scratch_shapes=[pltpu.VMEM((2, DM, DM), F32), pltpu.VMEM((2, 8, DM), F32)],
        compiler_params=_cparams(("parallel", "arbitrary")),
    )(*[a for d in (0, 1) for a in (qc, kc, proj, proj, gates_t, dh, cst[d], nm[d])], bg_row)
    return (res[0], res[4]), (res[1], res[5]), (res[2], res[6]), (res[3], res[7])


def _mlstm_bwd_chain(d, ins, bg_ref, outs, dC_s, dn_s):
        q_ref, k_ref, v_ref, g_ref, gt_ref, dh_ref, cst_ref, nm_ref = ins
        dq_ref, dk_ref, dv_ref, dg_ref = outs
        g = d * HM + pl.program_id(0)
        C, n_row, m = cst_ref[...], nm_ref[0:1, :], nm_ref[1:2, 0:1]
        q, k, v = q_ref[...], k_ref[...], v_ref[...]
        t = _mlstm_chunk_terms(g, q, k, v, g_ref[...], gt_ref[...], bg_ref[...], C, n_row, m)
        tri, eye, qf, kf = t["tri"], t["eye"], t["qf"], t["kf"]
        w_inter, w_col, decay, Nst = t["w_inter"], t["w_col"], t["decay"], t["Nst"]
        dC, dn = dC_s[d], dn_s[d, 0:1, :]
        dhv = dh_ref[...]
        hval = t["num"] / Nst
        dnum = dhv / Nst
        dNst = -jnp.sum(dhv * hval, axis=1, keepdims=True) / Nst
        dden = jnp.where(jnp.abs(t["den"]) > t["floor"], jnp.sign(t["den"]) * dNst, 0.0)
        dSc = _dot(dnum, v, "nt") + dden
        dA = dSc * t["Dm"]
        G = dSc * t["Sc"]
        KdC = _dot(k, dC, "nt")
        dq = _dot(dA, k, "nn") + w_inter * _dot(dnum, C, "nn") + (w_inter * dden) * n_row
        dk = _dot(dA, q, "tn") + w_col * _dot(v, dC, "nn") + w_col * dn
        dv = _dot(t["Sc"], dnum, "tn") + w_col * KdC
        dq_ref[...] = dq
        dk_ref[...] = dk
        dv_ref[...] = dv
        dlog_inter = w_inter * (jnp.sum(dnum * t["numI"], axis=1, keepdims=True) + dden * t["denI"])
        rowG = jnp.sum(G, axis=1, keepdims=True)
        colG = jnp.sum(G, axis=0, keepdims=True)
        u_col = w_col * (jnp.sum(v * KdC, axis=1, keepdims=True) + jnp.sum(kf * dn, axis=1, keepdims=True))
        colG_c = jnp.sum(jnp.where(eye, colG, 0.0), axis=1, keepdims=True)
        u_row = jnp.sum(jnp.where(eye, u_col, 0.0), axis=0, keepdims=True)
        db_col = rowG + dlog_inter - u_col - colG_c
        dbL = jnp.sum(u_col, axis=0, keepdims=True) + decay * (
            jnp.sum(jnp.sum(dC * C, axis=1, keepdims=True), axis=0, keepdims=True)
            + jnp.sum(dn * n_row, axis=1, keepdims=True))
        dlf_row = jnp.sum(jnp.where(tri, db_col, 0.0), axis=0, keepdims=True) + dbL
        di_row = colG + u_row
        df_row = dlf_row * (1.0 - _sigmoid(t["f_row"]))
        dg_ref[...] = jnp.zeros(dg_ref.shape, F32)
        dg_ref[0:1, :] = di_row
        dg_ref[1:2, :] = df_row
        dC_s[d] = decay * dC + _dot(w_inter * dnum, q, "tn")
        dn_s[d, 0:1, :] = decay * dn + _colsum((w_inter * dden) * qf)


def _pad_w_in(w):
    cq, ckv, kpe, qm, km, vm, om, gt = _split_in(w)
    z = lambda n: jnp.zeros((w.shape[0], n), w.dtype)
    return jnp.concatenate([qm, km, vm, om, cq, ckv, kpe, z(HP - QK), gt, z(128 - NG)], axis=1)


def _split_in(w):
    out, o = [], 0
    for n in IN_SIZES:
        out.append(w[:, o:o + n])
        o += n
    return out


def _unpad_w_in(g):
    return jnp.concatenate([g[:, P_CQ:P_CQ + Q_LORA], g[:, P_CKV:P_CKV + KV_LORA], g[:, P_KPE:P_KPE + ROPE],
                            g[:, 0:4 * MW], g[:, P_G:P_G + NG]], axis=1)


def _pad_w_uq(w):
    return jnp.pad(w.reshape(Q_LORA, H_MLA, QK), ((0, 0), (0, 0), (0, HP - QK))).reshape(Q_LORA, H_MLA * HP)


def _unpad_w_uq(g):
    return g.reshape(Q_LORA, H_MLA, HP)[:, :, :QK].reshape(Q_LORA, H_MLA * QK)


def _perm_w_ukv(w):
    return w.reshape(KV_LORA, H_MLA, 2, NOPE).transpose(0, 2, 1, 3).reshape(KV_LORA, 2 * H_MLA * NOPE)


def _unperm_w_ukv(g):
    return g.reshape(KV_LORA, 2, H_MLA, NOPE).transpose(0, 2, 1, 3).reshape(KV_LORA, 2 * H_MLA * NOPE)


def _rope_tables(positions):
    half = ROPE // 2
    freqs = ROPE_THETA ** (-jnp.arange(half, dtype=F32) / half)
    ang = positions.astype(F32)[:, None] * freqs
    cos, sin = jnp.cos(ang), jnp.sin(ang)
    z32, z64 = jnp.zeros_like(cos), jnp.zeros((cos.shape[0], 64), F32)
    return (jnp.concatenate([cos, cos, z64], axis=1), jnp.concatenate([z32, sin, z64], axis=1),
            jnp.concatenate([-sin, z32, z64], axis=1))


def _device_step(x, tgt, positions, modv, W, late=None):
    S = x.shape[0]
    MX = _MXU_DTYPE
    cosp, rs1, rs2 = _rope_tables(positions)
    tabs = [(cosp, 128, 0), (rs1, 128, 0), (rs2, 128, 0)]
    cat1 = lambda vs: jnp.concatenate(vs, axis=1)
    hsl = lambda hh, w: slice(hh * w, (hh + 1) * w)

    def ln1(xv, g, mv):
        xhat, _ = _rms(xv, D)
        return [xhat * g * (1.0 + mv[1:2]) + mv[0:1]], []

    (h,) = _rowmap(ln1, [(x, D, 0)], [W["g_mix"], modv], [(D, MX)], tile=256, name="ln1")
    proj = _mm(h, W["w_in"], "nn", name="proj")

    def lora(cq, ckv, gq, gkv):
        return [_rms(cq, Q_LORA)[0] * gq, _rms(ckv, KV_LORA)[0] * gkv], []

    cqn, ckvn = _rowmap(lora, [(proj, Q_LORA, P_CQ // Q_LORA), (proj, KV_LORA, P_CKV // KV_LORA)],
                        [W["g_qlora"], W["g_kvlora"]], [(Q_LORA, MX), (KV_LORA, MX)], tile=256, name="lora_norm")
    q_raw = _mm(cqn, W["w_uq"], "nn", name="q_up")
    kv_raw = _mm(ckvn, W["w_ukv"], "nn", name="kv_up")

    def mla_q(qr, cp, a1, a2, gq):
        outs = []
        for hh in range(H_MLA):
            y = _rms(qr[:, hsl(hh, HP)], QK)[0] * gq
            outs += [y[:, :NOPE], _rope_fwd(y[:, NOPE:], cp, a1, a2)]
        return [cat1(outs) * _Q_PRESCALE], []

    (qh,) = _rowmap(mla_q, [(q_raw, H_MLA * HP, 0)] + tabs, [W["gq"]], [(H_MLA * HP, MX)], tile=256, name="mla_q")

    def mla_k(kvr, kpe, cp, a1, a2, gk):
        lane = lax.broadcasted_iota(jnp.int32, (kvr.shape[0], 128), 1)
        outs, vas = [], []
        for hh in range(H_MLA):
            y = _rms(cat1([kvr[:, hsl(hh, NOPE)], kpe]), QK)[0] * gk
            outs += [y[:, :NOPE], _rope_fwd(y[:, NOPE:], cp, a1, a2) + ((lane == ROPE) | (lane == ROPE + 1)).astype(F32)]
            vas += [kvr[:, H_MLA * NOPE + hh * VD:H_MLA * NOPE + (hh + 1) * VD], (lane < 2).astype(F32)]
        return [cat1(outs), cat1(vas)], []

    kh, va = _rowmap(mla_k, [(kv_raw, 2 * H_MLA * NOPE, 0), (proj, 128, P_KPE // 128)] + tabs, [W["gk"]],
                     [(H_MLA * HP, MX), (H_MLA * HP, MX)], tile=256, name="mla_k")
    attn_o, qa, gathered = _attn_fwd(qh, kh, va, side=late or (), side_cols=(1,))
    if late:
        W = dict(W, w_out=gathered[0].reshape(D, D), w_ff1=gathered[1], w_ff2=gathered[2].reshape(DFF, D))

    qc, kc = _conv_fwd(proj, W["conv_w8"], W["conv_b"])
    gates_t = proj[:, P_G:P_G + NG].T
    (h_f, h_b), cst, nm = _mlstm_fwd(qc, kc, proj, gates_t, W["bg_row"])
    hrows = [(h_f, MW, 0), (h_b, MW, 0), (proj, MW, P_OM // MW)]

    def ml_out(ao, hf, hb, om, gmn):
        outs = [ao.astype(F32)]
        hs = hf + hb
        for hh in range(HM):
            sl = hsl(hh, DM)
            outs.append(_sigmoid(om[:, sl]) * _rms(hs[:, sl], DM)[0] * gmn[:, sl])
        return [cat1(outs)], []

    (cat,) = _rowmap(ml_out, [(attn_o, MW, 0)] + hrows, [W["g_mn"]], [(D, MX)], tile=256, name="ml_out")
    mixed = _mm(cat, W["w_out"], "nn", name="out_proj")

    def res_ln2(xv, mx, g, mv):
        x1 = xv + mv[2:3] * mx
        return [x1, _rms(x1, D)[0] * g * (1.0 + mv[4:5]) + mv[3:4]], []

    x1, h2 = _rowmap(res_ln2, [(x, D, 0), (mixed, D, 0)], [W["g_mlp"], modv], [(D, F32), (D, MX)],
                     tile=256, name="res_ln2")
    a, u = _mm(h2, W["w_ff1"], "nn", name="ff1", out_dtypes=(MX, MX),
               epilogue=lambda r: (jnp.square(jnp.maximum(r, 0.0)), r))
    y = _mm(a, W["w_ff2"], "nn", name="ff2")

    def final(x1v, yv, tv, mv):
        err = x1v + mv[5:6] * yv - tv
        dout = err * (1.0 / D)
        loss = jnp.sum(jnp.sum(0.5 * err * dout, axis=1, keepdims=True), axis=0, keepdims=True)
        return [dout, mv[5:6] * dout], [loss, _colsum(dout * yv)]

    dout, dy, loss, dgate2 = _rowmap(final, [(x1, D, 0), (y, D, 0), (tgt, D, 0)], [modv], [(D, F32), (D, MX)],
                                     [(1, 1), (1, D)], tile=256, name="loss_head")

    du = _mm(dy, W["w_ff2"], "nt", name="ff2_dx", out_dtypes=(MX,), extras=(u,),
             epilogue=lambda r, uv: (r * (2.0 * jnp.maximum(uv.astype(F32), 0.0)),))
    gdt = (MX,)
    g_ff2 = _mm(a, dy, "tn", name="ff2_dw", out_dtypes=gdt)
    dh2 = _mm(du, W["w_ff1"], "nt", name="ff1_dx")
    g_ff1 = _mm(h2, du, "tn", name="ff1_dw", out_dtypes=gdt)

    def ln2_bwd(dh2v, x1v, doutv, mxv, g, mv):
        xhat, r = _rms(x1v, D)
        dn2 = dh2v * (1.0 + mv[4:5])
        dx1 = doutv + _rms_bwd(dn2 * g, xhat, r, D)
        return [dx1, mv[2:3] * dx1], [_colsum(dh2v), _colsum(dh2v * xhat * g), _colsum(dn2 * xhat), _colsum(dx1 * mxv)]

    dx1, dmixed, dshift2, dscale2, dg_mlp, dgate1 = _rowmap(
        ln2_bwd, [(dh2, D, 0), (x1, D, 0), (dout, D, 0), (mixed, D, 0)], [W["g_mlp"], modv],
        [(D, F32), (D, MX)], [(1, D)] * 4, tile=256, name="ln2_bwd")
    dcat = _mm(dmixed, W["w_out"], "nt", name="out_dx")
    g_out = _mm(cat, dmixed, "tn", name="out_dw", out_dtypes=gdt)

    def ml_out_bwd(dml, hf, hb, om, gmn):
        hs = hf + hb
        dhs, dos, dgs = [], [], []
        for hh in range(HM):
            sl = hsl(hh, DM)
            xhat, r = _rms(hs[:, sl], DM)
            g, sg, d = gmn[:, sl], _sigmoid(om[:, sl]), dml[:, sl]
            dos.append(d * xhat * g * sg * (1.0 - sg))
            dhn = d * sg
            dgs.append(_colsum(dhn * xhat))
            dhs.append(_rms_bwd(dhn * g, xhat, r, DM))
        return [cat1(dhs), cat1(dos)], [cat1(dgs)]

    dhs, do_m, dg_mn = _rowmap(ml_out_bwd, [(dcat, MW, 1)] + hrows, [W["g_mn"]], [(MW, F32), (MW, MX)],
                               [(1, MW)], tile=256, name="ml_out_bwd")
    dqd, dkd, dvd, dgates = _mlstm_bwd(qc, kc, proj, gates_t, W["bg_row"], dhs, cst, nm)
    dqk_m, dconv_w8, dconv_b = _conv_bwd(proj, dqd, dkd, W["conv_w8"], W["conv_b"])

    def do_aug(ao, dov):
        lane = lax.broadcasted_iota(jnp.int32, (ao.shape[0], 128), 1)
        outs = []
        for hh in range(H_MLA):
            sl = hsl(hh, VD)
            dl = jnp.sum(ao[:, sl].astype(F32) * dov[:, sl], axis=1, keepdims=True)
            hi = dl.astype(MX).astype(F32)
            outs += [dov[:, sl], jnp.where(lane == 0, -hi, jnp.where(lane == 1, hi - dl, 0.0))]
        return [cat1(outs)], []

    (doa,) = _rowmap(do_aug, [(attn_o, MW, 0), (dcat, MW, 0)], [], [(H_MLA * HP, MX)], tile=256, name="attn_delta")
    side = [g_out.reshape(N_CHIP, D // N_CHIP, D), g_ff1, g_ff2.reshape(N_CHIP, DFF // N_CHIP, D)] if late else ()
    dq_a, dk_a, dv_a, late_got = _attn_bwd(qa, kh, va, doa, side=side, side_cols=(1,))

    def mla_q_bwd(dqv, qr, cp, a1, a2, gq):
        outs, dg = [], 0.0
        for hh in range(H_MLA):
            sl = hsl(hh, HP)
            xhat, r = _rms(qr[:, sl], QK)
            d = dqv[:, sl]
            dyv = cat1([d[:, :NOPE], _rope_bwd(d[:, NOPE:], cp, a1, a2)])
            dg = dg + _colsum(dyv * xhat)
            outs.append(_rms_bwd(dyv * gq, xhat, r, QK))
        return [cat1(outs)], [dg]

    dq_raw, dgq = _rowmap(mla_q_bwd, [(dq_a, H_MLA * HP, 0), (q_raw, H_MLA * HP, 0)] + tabs, [W["gq"]],
                          [(H_MLA * HP, MX)], [(1, HP)], tile=256, name="mla_q_bwd")
    dcqn = _mm(dq_raw, W["w_uq"], "nt", name="q_up_dx")
    g_uq = _mm(cqn, dq_raw, "tn", name="q_up_dw", out_dtypes=gdt)

    def mla_k_bwd(dkv, dvv, kvr, kpe, cp, a1, a2, gk):
        dkn, dg, dkpe = [], 0.0, 0.0
        for hh in range(H_MLA):
            xhat, r = _rms(cat1([kvr[:, hsl(hh, NOPE)], kpe]), QK)
            d = dkv[:, hsl(hh, HP)]
            dyv = cat1([d[:, :NOPE], _rope_bwd(d[:, NOPE:], cp, a1, a2)])
            dg = dg + _colsum(dyv * xhat)
            dxv = _rms_bwd(dyv * gk, xhat, r, QK)
            dkn.append(dxv[:, :NOPE])
            dkpe = dkpe + dxv[:, NOPE:]
        return [cat1(dkn + [dvv[:, hh * HP:hh * HP + VD] for hh in range(H_MLA)]), dkpe], [dg]

    dkv_raw, dkpe, dgk = _rowmap(
        mla_k_bwd, [(dk_a, H_MLA * HP, 0), (dv_a, H_MLA * HP, 0), (kv_raw, 2 * H_MLA * NOPE, 0),
                    (proj, 128, P_KPE // 128)] + tabs, [W["gk"]],
        [(2 * H_MLA * NOPE, MX), (128, MX)], [(1, HP)], tile=256, name="mla_k_bwd")
    dckvn = _mm(dkv_raw, W["w_ukv"], "nt", name="kv_up_dx")
    g_ukv = _mm(ckvn, dkv_raw, "tn", name="kv_up_dw", out_dtypes=gdt)

    def lora_bwd(dcq, dckv, cq, ckv, gq, gkv):
        xq, rq = _rms(cq, Q_LORA)
        xk, rk = _rms(ckv, KV_LORA)
        return ([_rms_bwd(dcq * gq, xq, rq, Q_LORA), _rms_bwd(dckv * gkv, xk, rk, KV_LORA)],
                [_colsum(dcq * xq), _colsum(dckv * xk)])

    dc_q, dc_kv, dg_qlora, dg_kvlora = _rowmap(
        lora_bwd, [(dcqn, Q_LORA, 0), (dckvn, KV_LORA, 0), (proj, Q_LORA, P_CQ // Q_LORA),
                   (proj, KV_LORA, P_CKV // KV_LORA)], [W["g_qlora"], W["g_kvlora"]],
        [(Q_LORA, MX), (KV_LORA, MX)], [(1, Q_LORA), (1, KV_LORA)], tile=256, name="lora_bwd")

    nc = S // LCH
    dg16 = jnp.stack(dgates)[:, :, :, 0:2, :].transpose(2, 4, 0, 3, 1).reshape(S, NG)
    dg128 = jnp.pad(dg16, ((0, 0), (0, 128 - NG)))

    def assemble(dqk, dv0, dv1, dom, dcq, dckv, dkp, dgp):
        f = lambda t: t.astype(F32)
        return [cat1([f(dqk), dv0 + dv1, f(dom), f(dcq), f(dckv), f(dkp), dgp])], [_colsum(dgp)]

    dproj, dbg = _rowmap(
        assemble, [(dqk_m, 2 * MW, 0), (dvd[0], MW, 0), (dvd[1], MW, 0), (do_m, MW, 0), (dc_q, Q_LORA, 0),
                   (dc_kv, KV_LORA, 0), (dkpe, 128, 0), (dg128, 128, 0)], [], [(D_INP, MX)], [(1, 128)],
        tile=256, name="dproj")
    g_in = _mm(h, dproj, "tn", name="proj_dw", out_dtypes=gdt)
    early_got = ()
    if late:
        side = [_slabs(_unpad_w_in(g_in)).astype(MX), _slabs(_unpad_w_uq(g_uq)).astype(MX),
                _slabs(_unperm_w_ukv(g_ukv)).astype(MX)]
        dh, early_got = _mm(dproj, W["w_in"], "nt", name="proj_dx", side=side)
    else:
        dh = _mm(dproj, W["w_in"], "nt", name="proj_dx")

    def ln1_bwd(dhv, xv, dx1v, g, mv):
        xhat, r = _rms(xv, D)
        dn = dhv * (1.0 + mv[1:2])
        return [dx1v + _rms_bwd(dn * g, xhat, r, D)], [_colsum(dhv), _colsum(dhv * xhat * g), _colsum(dn * xhat)]

    gx, dshift1, dscale1, dg_mix = _rowmap(ln1_bwd, [(dh, D, 0), (x, D, 0), (dx1, D, 0)], [W["g_mix"], modv],
                                           [(D, F32)], [(1, D)] * 3, tile=256, name="ln1_bwd")
    dmodv = jnp.concatenate([dshift1, dscale1, dgate1, dshift2, dscale2, dgate2], axis=0)
    grads = dict(w_in=g_in, w_uq=g_uq, w_ukv=g_ukv, w_out=g_out, w_ff1=g_ff1, w_ff2=g_ff2,
                 norm_mix_g=dg_mix, b_gates=dbg[:, :NG], conv_w=dconv_w8[:CONVW], conv_b=dconv_b,
                 q_lora_g=dg_qlora, kv_lora_g=dg_kvlora, q_norm_g=dgq[:, :QK], k_norm_g=dgk[:, :QK],
                 mlstm_norm_g=dg_mn, norm_mlp_g=dg_mlp)
    grads["got"] = list(early_got) + list(late_got)
    return loss, gx, dmodv, grads


def _cols(g):
    return g.transpose(1, 0, 2).reshape(g.shape[1], N_CHIP * g.shape[2])


def _slabs(gfull):
    return gfull.reshape(gfull.shape[0], N_CHIP, -1).transpose(1, 0, 2)


def _prep_weights(w_in, w_uq, w_ukv, w_out, w_ff1, w_ff2, norm_mix_g, norm_mlp_g, q_lora_g, kv_lora_g,
                  q_norm_g, k_norm_g, mlstm_norm_g, conv_w, conv_b, b_gates):
    MX = _MXU_DTYPE
    padg = lambda g: jnp.pad(g.reshape(1, QK).astype(F32), ((0, 0), (0, HP - QK)))
    return dict(
        w_in=_pad_w_in(w_in).astype(MX), w_uq=_pad_w_uq(w_uq).astype(MX), w_ukv=_perm_w_ukv(w_ukv).astype(MX),
        w_out=None if w_out is None else w_out.astype(MX), w_ff1=None if w_ff1 is None else w_ff1.astype(MX),
        w_ff2=None if w_ff2 is None else w_ff2.astype(MX),
        g_mix=norm_mix_g.reshape(1, D), g_mlp=norm_mlp_g.reshape(1, D), g_qlora=q_lora_g.reshape(1, Q_LORA),
        g_kvlora=kv_lora_g.reshape(1, KV_LORA), gq=padg(q_norm_g), gk=padg(k_norm_g),
        g_mn=mlstm_norm_g.reshape(1, MW), conv_w8=jnp.pad(conv_w.reshape(CONVW, 2 * MW), ((0, 8 - CONVW), (0, 0))),
        conv_b=conv_b.reshape(1, 2 * MW), bg_row=jnp.pad(b_gates.reshape(1, NG), ((0, 0), (0, 128 - NG))))


MESH = pl.DeviceIdType.MESH
N_DEV = 8
N_CHIP = 4


def _comm_call(body, **kw):
    if _INTERPRET:
        kw["interpret"] = pltpu.InterpretParams()
    return pl.pallas_call(body, **kw)


def _allgather8(blk, *, name):
    m_per, n = blk.shape

    def body(x_ref, out_ref, send_sems, recv_sems, local_sem):
        x, y, c = lax.axis_index("x"), lax.axis_index("y"), lax.axis_index("c")
        me, sibling = (x, y, c), (x, y, 1 - c)
        chips = [(1 - x, y), (x, 1 - y), (1 - x, 1 - y)]

        def rows(px, py, pc):
            return out_ref.at[pl.ds((4 * px + 2 * py + pc) * m_per, m_per), :]

        def copy(k, block, to, src=None):
            return pltpu.make_async_remote_copy(
                src_ref=rows(*block) if src is None else src, dst_ref=rows(*block),
                send_sem=send_sems.at[k], recv_sem=recv_sems.at[k], device_id=to, device_id_type=MESH)

        mine = pltpu.make_async_copy(x_ref, rows(*me), local_sem)
        mine.start()
        first = [copy(0, me, sibling, src=x_ref)]
        first += [copy(1 + j, me, (*chip, c), src=x_ref) for j, chip in enumerate(chips)]
        for cp in first:
            cp.start()
        passed = [copy(4 + j, (*chip, c), sibling) for j, chip in enumerate(chips)]
        for j, chip in enumerate(chips):
            copy(1 + j, (*chip, c), me).wait_recv()
            passed[j].start()
        copy(0, sibling, me).wait_recv()
        for j, chip in enumerate(chips):
            copy(4 + j, (*chip, 1 - c), me).wait_recv()
        for cp in first + passed:
            cp.wait_send()
        mine.wait()

    return _comm_call(
        body, name=name, out_shape=jax.ShapeDtypeStruct((N_DEV * m_per, n), blk.dtype),
        in_specs=[pl.BlockSpec(memory_space=pltpu.VMEM)], out_specs=pl.BlockSpec(memory_space=pltpu.VMEM),
        scratch_shapes=[pltpu.SemaphoreType.DMA((7,)), pltpu.SemaphoreType.DMA((7,)), pltpu.SemaphoreType.DMA],
    )(blk)


def _chip_exchange(arrays, *, gather, name):
    n = len(arrays)

    def body(*refs):
        start, wait = _exchange_ops(refs[:n], refs[n:2 * n], *refs[2 * n:], gather=gather)
        start()
        wait()

    io = _exchange_io(arrays, gather)
    return _comm_call(body, name=name, out_shape=io["out_shape"], in_specs=io["specs"], out_specs=io["specs"],
                      scratch_shapes=io["scratch"])(*arrays)


def _exchange_io(arrays, gather, cols=()):
    n = len(arrays)

    def out(i, a):
        if gather:
            return (a.shape[0], N_CHIP * a.shape[1]) if i in cols else (N_CHIP, *a.shape)
        return (N_CHIP, a.shape[0], a.shape[1] // N_CHIP) if i in cols else a.shape

    return dict(
        specs=[pl.BlockSpec(memory_space=pltpu.HBM)] * n,
        out_shape=[jax.ShapeDtypeStruct(out(i, a), a.dtype) for i, a in enumerate(arrays)],
        scratch=[pltpu.SemaphoreType.DMA((3 * n,)), pltpu.SemaphoreType.DMA((3 * n,)), pltpu.SemaphoreType.DMA((n,))])


def _exchange_ops(ins, outs, send_sems, recv_sems, local_sems, *, gather, cols=()):
    n = len(ins)
    x, y, c = lax.axis_index("x"), lax.axis_index("y"), lax.axis_index("c")
    k = 2 * x + y
    chips = [(1 - x, y), (x, 1 - y), (1 - x, 1 - y)]

    def piece(ref, a, chip, windowed):
        if not windowed:
            return ref.at[chip]
        width = ref.shape[1] // N_CHIP
        return ref.at[:, pl.ds(pl.multiple_of(chip * width, 128), width)]

    src_of = lambda a, chip: ins[a] if gather else piece(ins[a], a, chip, a in cols)
    dst_of = lambda a, chip: piece(outs[a], a, chip, gather and a in cols)

    def remote(a, j):
        px, py = chips[j]
        return pltpu.make_async_remote_copy(
            src_ref=src_of(a, 2 * px + py), dst_ref=dst_of(a, k), send_sem=send_sems.at[3 * a + j],
            recv_sem=recv_sems.at[3 * a + j], device_id=(px, py, c), device_id_type=MESH)

    def arrival(a, j):
        px, py = chips[j]
        return pltpu.make_async_remote_copy(
            src_ref=src_of(a, k), dst_ref=dst_of(a, 2 * px + py), send_sem=send_sems.at[3 * a + j],
            recv_sem=recv_sems.at[3 * a + j], device_id=(px, py, c), device_id_type=MESH)

    local = [pltpu.make_async_copy(src_of(a, k), dst_of(a, k), local_sems.at[a]) for a in range(n)]
    sent = [remote(a, j) for a in range(n) for j in range(3)]

    def start():
        for cp in local + sent:
            cp.start()

    def wait():
        for a in range(n):
            for j in range(3):
                arrival(a, j).wait_recv()
        for cp in sent:
            cp.wait_send()
        for cp in local:
            cp.wait()

    return start, wait


def _chip_allgather_halved(shards, *, name):
    n = len(shards)
    half_rows = [s.shape[0] // 2 for s in shards]
    assert all(s.shape[0] % 16 == 0 for s in shards)

    def body(*refs):
        ins, outs = refs[:n], refs[n:2 * n]
        ici_send, ici_recv, d2d_send, d2d_recv, local_sems = refs[2 * n:]
        x, y, c = lax.axis_index("x"), lax.axis_index("y"), lax.axis_index("c")
        k = 2 * x + y
        chips = [(1 - x, y), (x, 1 - y), (1 - x, 1 - y)]

        def half(a, slab, core):
            return outs[a].at[slab, pl.ds(core * half_rows[a], half_rows[a])]

        def ici(a, j, slab):
            px, py = chips[j]
            return pltpu.make_async_remote_copy(
                src_ref=ins[a].at[pl.ds(c * half_rows[a], half_rows[a])], dst_ref=half(a, slab, c),
                send_sem=ici_send.at[3 * a + j], recv_sem=ici_recv.at[3 * a + j],
                device_id=(px, py, c), device_id_type=MESH)

        def d2d(a, j, core):
            px, py = chips[j]
            return pltpu.make_async_remote_copy(
                src_ref=half(a, 2 * px + py, core), dst_ref=half(a, 2 * px + py, core),
                send_sem=d2d_send.at[3 * a + j], recv_sem=d2d_recv.at[3 * a + j],
                device_id=(x, y, 1 - c), device_id_type=MESH)

        local = [pltpu.make_async_copy(ins[a], outs[a].at[k], local_sems.at[a]) for a in range(n)]
        sent = [ici(a, j, k) for a in range(n) for j in range(3)]
        for cp in local + sent:
            cp.start()
        passed = []
        for a in range(n):
            for j, (px, py) in enumerate(chips):
                ici(a, j, 2 * px + py).wait_recv()
                passed.append(d2d(a, j, c))
                passed[-1].start()
        for a in range(n):
            for j in range(3):
                d2d(a, j, 1 - c).wait_recv()
        for cp in sent + passed:
            cp.wait_send()
        for cp in local:
            cp.wait()

    hbm = pl.BlockSpec(memory_space=pltpu.HBM)
    return _comm_call(
        body, name=name, out_shape=[jax.ShapeDtypeStruct((N_CHIP, *s.shape), s.dtype) for s in shards],
        in_specs=[hbm] * n, out_specs=[hbm] * n,
        scratch_shapes=[pltpu.SemaphoreType.DMA((3 * n,))] * 4 + [pltpu.SemaphoreType.DMA((n,))],
    )(*shards)


def _sibling_exchange(arrays, *, name):
    n = len(arrays)

    def body(*refs):
        ins, outs = refs[:n], refs[n:2 * n]
        send_sems, recv_sems = refs[2 * n:]
        x, y, c = lax.axis_index("x"), lax.axis_index("y"), lax.axis_index("c")
        cps = [pltpu.make_async_remote_copy(
            src_ref=ins[a], dst_ref=outs[a], send_sem=send_sems.at[a], recv_sem=recv_sems.at[a],
            device_id=(x, y, 1 - c), device_id_type=MESH) for a in range(n)]
        for cp in cps:
            cp.start()
        for cp in cps:
            cp.wait()

    hbm = pl.BlockSpec(memory_space=pltpu.HBM)
    return _comm_call(
        body, name=name, out_shape=[jax.ShapeDtypeStruct(a.shape, a.dtype) for a in arrays],
        in_specs=[hbm] * n, out_specs=[hbm] * n,
        scratch_shapes=[pltpu.SemaphoreType.DMA((n,)), pltpu.SemaphoreType.DMA((n,))],
    )(*arrays)


def _sum_blocks(a, nblk, *, name):
    n = a.shape[1]

    def body(a_ref, o_ref):
        acc = a_ref[pl.ds(0, 8), :]
        for d in range(1, nblk):
            acc = acc + a_ref[pl.ds(8 * d, 8), :]
        o_ref[...] = acc

    return _call(body, name=name, out_shape=jax.ShapeDtypeStruct((8, n), F32))(a)


def _outer8(sct, dm, *, name, tm=256, tn=1024):
    R, N = sct.shape[0], dm.shape[1]
    tm, tn = min(tm, R), min(tn, N)

    def body(s_ref, d_ref, o_ref):
        s, dmv = s_ref[...], d_ref[...]
        acc = s[:, 0:1] * dmv[0:1, :]
        for b in range(1, 8):
            acc = acc + s[:, b:b + 1] * dmv[b:b + 1, :]
        o_ref[...] = acc

    return _call(
        body, name=name, grid=(R // tm, N // tn),
        in_specs=[pl.BlockSpec((tm, 8), lambda i, j: (i, 0)), pl.BlockSpec((8, tn), lambda i, j: (0, j))],
        out_specs=pl.BlockSpec((tm, tn), lambda i, j: (i, j)),
        out_shape=jax.ShapeDtypeStruct((R, N), F32),
        compiler_params=_cparams(("parallel", "parallel")),
    )(sct, dm)


_BC1 = 1.0 - ADAM_B1 ** ADAM_STEP
_BC2 = 1.0 - ADAM_B2 ** ADAM_STEP


def _adamw(w, g_parts, m, v, *, name, tile=128):
    R, C = w.shape
    tile = min(tile, R)
    assert R % tile == 0
    npart = len(g_parts)

    def body(*refs):
        w_ref, m_ref, v_ref = refs[npart:npart + 3]
        g_o, d_o, m_o, v_o = refs[npart + 3:]
        g = refs[0][...].astype(F32)
        for r in refs[1:npart]:
            g = g + r[...].astype(F32)
        mn = ADAM_B1 * m_ref[...] + (1.0 - ADAM_B1) * g
        vn = ADAM_B2 * v_ref[...] + (1.0 - ADAM_B2) * jnp.square(g)
        g_o[...] = g
        m_o[...] = mn
        v_o[...] = vn
        d_o[...] = -ADAM_LR * ((mn / _BC1) / (jnp.sqrt(vn / _BC2) + ADAM_EPS) + ADAM_WD * w_ref[...])

    spec = pl.BlockSpec((tile, C), lambda i: (i, 0))
    return _call(
        body, name=name, grid=(R // tile,), in_specs=[spec] * (npart + 3), out_specs=[spec] * 4,
        out_shape=[jax.ShapeDtypeStruct((R, C), F32)] * 4,
        compiler_params=_cparams(("parallel",)),
    )(*g_parts, w, m, v)


def _pack(vecs, rows8_cols):
    flat = jnp.concatenate([v.reshape(-1).astype(F32) for v in vecs])
    return jnp.pad(flat, (0, 8 * rows8_cols - flat.shape[0])).reshape(8, rows8_cols)


def _unpack(flat, shapes):
    out, o = [], 0
    for s in shapes:
        n = math.prod(s)
        out.append(flat[o:o + n].reshape(s))
        o += n
    return out


_BIG = ("w_in", "w_uq", "w_ukv", "w_out", "w_ff1", "w_ff2")
_SMALL = ("b_ada", "norm_mix_g", "b_gates", "conv_w", "conv_b", "q_lora_g", "kv_lora_g", "q_norm_g", "k_norm_g",
          "mlstm_norm_g", "norm_mlp_g")
_ORDER = ("w_ada", "b_ada", "norm_mix_g", "w_in", "b_gates", "conv_w", "conv_b", "q_lora_g", "w_uq", "kv_lora_g",
          "w_ukv", "q_norm_g", "k_norm_g", "mlstm_norm_g", "w_out", "norm_mlp_g", "w_ff1", "w_ff2")


def kernel(x, c, positions, w_ada, b_ada, norm_mix_g, w_in, b_gates, conv_w, conv_b, q_lora_g, w_uq, kv_lora_g, w_ukv, q_norm_g, k_norm_g, mlstm_norm_g, w_out, norm_mlp_g, w_ff1, w_ff2, loss_target, m_w_ada, m_b_ada, m_norm_mix_g, m_w_in, m_b_gates, m_conv_w, m_conv_b, m_q_lora_g, m_w_uq, m_kv_lora_g, m_w_ukv, m_q_norm_g, m_k_norm_g, m_mlstm_norm_g, m_w_out, m_norm_mlp_g, m_w_ff1, m_w_ff2, v_w_ada, v_b_ada, v_norm_mix_g, v_w_in, v_b_gates, v_conv_w, v_conv_b, v_q_lora_g, v_w_uq, v_kv_lora_g, v_w_ukv, v_q_norm_g, v_k_norm_g, v_mlstm_norm_g, v_w_out, v_norm_mlp_g, v_w_ff1, v_w_ff2):
    args = dict(locals())
    wts = {n: args[n] for n in _ORDER}
    mom = {n: args["m_" + n] for n in _ORDER}
    var = {n: args["v_" + n] for n in _ORDER}
    MX = _MXU_DTYPE
    xi, yi, ci = lax.axis_index("x"), lax.axis_index("y"), lax.axis_index("c")
    chip = 2 * xi + yi
    dev = 2 * chip + ci
    S = x.shape[1]
    CS = 2 * MW // N_CHIP
    GS = DM // N_CHIP

    pk = _pack([c, conv_w, mlstm_norm_g], 1024)
    allpk = _allgather8(pk, name="gather_small").reshape(N_DEV, 8 * 1024)
    c_all = allpk[:, :D]
    per_chip = allpk[0::2]
    conv_w_full = per_chip[:, D:D + CONVW * CS].reshape(N_CHIP, CONVW, CS).transpose(1, 0, 2).reshape(CONVW, 2 * MW)
    o = D + CONVW * CS
    mn_full = per_chip[:, o:o + HM * GS].reshape(N_CHIP, HM, GS).transpose(1, 0, 2).reshape(HM, DM)

    (sc,) = _rowmap(lambda cv: ([cv * _sigmoid(cv)], []), [(c_all, D, 0)], [], [(D, F32)], tile=8, name="silu_c")
    ncol = w_ada.shape[2]
    b_cols = lax.dynamic_slice(b_ada, (0, chip * ncol), (1, ncol))
    modp = _mm(sc, w_ada[0], "nn", name="ada_fwd", tm=8, tn=1024, tk=512, extras=(jnp.broadcast_to(b_cols, (8, ncol)),),
               epilogue=lambda r, b: (r + b,))
    modg = _allgather8(modp, name="gather_mod").reshape(N_CHIP, 2, 8, ncol)[:, 0]
    mod_all = modg.transpose(1, 0, 2).reshape(N_DEV, N_CHIP * ncol)
    modv = jnp.pad(lax.dynamic_slice(mod_all, (dev, 0), (1, 6 * D)).reshape(6, D), ((0, 2), (0, 0)))

    shards = [wts[n][0].astype(MX) for n in _BIG]
    gw_in, gw_uq, gw_ukv = _chip_allgather_halved(shards[:3], name="gather_weights")
    W = _prep_weights(_cols(gw_in), _cols(gw_uq), _cols(gw_ukv), None, None, None, norm_mix_g, norm_mlp_g,
                      q_lora_g, kv_lora_g, q_norm_g, k_norm_g, mn_full, conv_w_full, conv_b, b_gates)

    loss, gx, dmodv, g = _device_step(x[0], loss_target[0], positions[0], modv, W, late=shards[3:])

    small_shapes = [(6 * D,), (D,), (NG,), (CONVW, 2 * MW), (2 * MW,), (Q_LORA,), (KV_LORA,), (QK,), (QK,), (MW,), (D,), (1,)]
    pg = _pack([dmodv, g["norm_mix_g"], g["b_gates"], g["conv_w"], g["conv_b"], g["q_lora_g"], g["kv_lora_g"],
                g["q_norm_g"], g["k_norm_g"], g["mlstm_norm_g"], g["norm_mlp_g"], loss], 4096)
    allpg = _allgather8(pg, name="gather_small_grads")
    tot = _unpack(_sum_blocks(allpg, N_DEV, name="sum_small_grads").reshape(-1), small_shapes)
    dmod_all = allpg.reshape(N_DEV, 8 * 4096)[:, :6 * D]
    gsmall = dict(zip(_SMALL, [tot[0].reshape(1, 6 * D), tot[1].reshape(1, D), tot[2].reshape(1, NG),
                               lax.dynamic_slice(tot[3], (0, chip * CS), (CONVW, CS)).reshape(1, CONVW, CS),
                               tot[4].reshape(1, 2 * MW), tot[5].reshape(1, Q_LORA), tot[6].reshape(1, KV_LORA),
                               tot[7].reshape(1, QK), tot[8].reshape(1, QK),
                               lax.dynamic_slice(tot[9].reshape(HM, DM), (0, chip * GS), (HM, GS)).reshape(1, HM, GS),
                               tot[10].reshape(1, D)]))
    loss_tot = tot[11].reshape(())

    got = g["got"]
    part = []
    for nme, r in zip(_BIG, got):
        wd = r.shape[2]
        (p,) = _rowmap(lambda a0, a1, a2, a3: ([(a0.astype(F32) + a1.astype(F32)) + (a2.astype(F32) + a3.astype(F32))], []),
                       [(r, wd, 0, k) for k in range(N_CHIP)], [], [(wd, F32)], tile=256, name="sum_chips_" + nme)
        part.append(p)
    other = _sibling_exchange(part, name="exchange_cores")

    dm_cols = lax.dynamic_slice(dmod_all, (0, chip * ncol), (N_DEV, ncol))
    g_ada = _outer8(sc.T, dm_cols, name="ada_dw")

    res = {}
    for nme, p, q in zip(_BIG, part, other):
        res[nme] = _adamw(wts[nme][0], [p, q], mom[nme][0], var[nme][0], name="adamw_" + nme)
    res["w_ada"] = _adamw(w_ada[0], [g_ada], m_w_ada[0], v_w_ada[0], name="adamw_w_ada")
    sw = _pack([wts[n] for n in _SMALL], 3072)
    sg = _pack([gsmall[n] for n in _SMALL], 3072)
    sm = _pack([mom[n] for n in _SMALL], 3072)
    sv = _pack([var[n] for n in _SMALL], 3072)
    small_res = _adamw(sw, [sg], sm, sv, name="adamw_small", tile=8)
    shapes = [wts[n].shape for n in _SMALL]
    unp = [_unpack(r.reshape(-1), shapes) for r in small_res]
    for i, nme in enumerate(_SMALL):
        res[nme] = tuple(u[i] for u in unp)
    outs = [loss_tot, gx[None]]
    for kind in range(4):
        outs += [res[n][kind].reshape(wts[n].shape) for n in _ORDER]
    return tuple(outs)
```

```python
import functools
import math

import jax
import jax.numpy as jnp
from jax import lax
from jax.experimental import pallas as pl
from jax.experimental.pallas import tpu as pltpu

F32 = jnp.float32
BF16 = jnp.bfloat16
_MXU_DTYPE = jnp.bfloat16
_INTERPRET = False

D = 2048
H_MLA = 8
NOPE = 128
ROPE = 64
QK = NOPE + ROPE
HP = 256
VD = 128
Q_LORA = 512
KV_LORA = 256
HM = 4
DM = 256
MW = HM * DM
LCH = 128
CONVW = 5
NG = 16
DFF = 4 * D
EPS = 1e-6
M_INIT = -1e30
ROPE_THETA = 10000.0
IN_SIZES = (Q_LORA, KV_LORA, ROPE, MW, MW, MW, MW, NG)
D_IN = sum(IN_SIZES)
P_QM, P_KM, P_VM, P_OM, P_CQ, P_CKV, P_KPE, P_G = 0, 1024, 2048, 3072, 4096, 4608, 4864, 4992
D_INP = 5120

ADAM_LR, ADAM_B1, ADAM_B2, ADAM_EPS, ADAM_WD, ADAM_STEP = 0.001, 0.9, 0.999, 1e-08, 0.01, 10

V7X_VMEM_LIMIT = 56 * 1024 * 1024


def _cparams(sem):
    return pltpu.CompilerParams(dimension_semantics=sem, vmem_limit_bytes=V7X_VMEM_LIMIT)


def _call(body, **kw):
    if _INTERPRET:
        kw.pop("compiler_params", None)
        kw["interpret"] = pltpu.InterpretParams()
    return pl.pallas_call(body, **kw)


def _dot(a, b, form):
    dims = {"nn": ((1,), (0,)), "nt": ((1,), (1,)), "tn": ((0,), (0,))}[form]
    return lax.dot_general(a.astype(_MXU_DTYPE), b.astype(_MXU_DTYPE), (dims, ((), ())),
                           preferred_element_type=F32)


def _mm(a, b, form, *, name, out_dtypes=(F32,), epilogue=None, extras=(), tm=1024, tn=1024, tk=2048, side=()):
    if form == "nn":
        (M, K), (K2, N) = a.shape, b.shape
    elif form == "nt":
        (M, K), (N, K2) = a.shape, b.shape
    else:
        (K, M), (K2, N) = a.shape, b.shape
    assert K == K2, (a.shape, b.shape, form)
    tm, tn = min(tm, M), min(tn, N)
    tk = max(d for d in range(128, min(tk, K) + 1, 128) if K % d == 0) if K > 128 else K
    assert M % tm == 0 and N % tn == 0 and K % tk == 0, (M, N, K, tm, tn, tk)
    nk = K // tk
    ne, no = len(extras), len(out_dtypes)
    if form == "tn":
        a_spec = pl.BlockSpec((tk, tm), lambda i, j, k: (k, i))
    else:
        a_spec = pl.BlockSpec((tm, tk), lambda i, j, k: (i, k))
    if form == "nt":
        b_spec = pl.BlockSpec((tn, tk), lambda i, j, k: (j, k))
    else:
        b_spec = pl.BlockSpec((tk, tn), lambda i, j, k: (k, j))
    mn_spec = pl.BlockSpec((tm, tn), lambda i, j, k: (i, j))
    grid = (M // tm, N // tn, nk)
    ns, io, wrap = _side_exchange(side, False, grid)

    def body(a_ref, b_ref, *rest):
        ex, outs = rest[:ne], rest[ne + ns:ne + ns + no]
        scratch = rest[ne + 2 * ns + no:]
        side_start, side_wait = wrap(rest[ne:ne + ns], rest[ne + ns + no:ne + 2 * ns + no], scratch[1:])
        side_start()
        prod = _dot(a_ref[...], b_ref[...], form)

        def finish(r):
            vals = (r,) if epilogue is None else epilogue(r, *[e[...] for e in ex])
            for o, v in zip(outs, vals):
                o[...] = v.astype(o.dtype)

        if nk == 1:
            finish(prod)
        else:
            acc, k = scratch[0], pl.program_id(2)

            @pl.when(k == 0)
            def _():
                acc[...] = prod

            @pl.when(k > 0)
            def _():
                acc[...] += prod

            @pl.when(k == nk - 1)
            def _():
                finish(acc[...])
        side_wait()

    res = _call(
        body, name=name, grid=grid,
        in_specs=[a_spec, b_spec] + [mn_spec] * ne + io["specs"],
        out_specs=[mn_spec] * no + io["specs"],
        out_shape=[jax.ShapeDtypeStruct((M, N), dt) for dt in out_dtypes] + io["out_shape"],
        scratch_shapes=[pltpu.VMEM((tm, tn) if nk > 1 else (8, 128), F32)] + io["scratch"],
        compiler_params=_cparams(("arbitrary",) * 3 if ns else ("parallel", "parallel", "arbitrary")),
    )(a, b, *extras, *side)
    if ns:
        return (res[0] if no == 1 else res[:no]), list(res[no:])
    return res[0] if no == 1 else res


def _rowmap(fn, rows, bcasts, outs, accs=(), *, tile, name):
    rows = [r if len(r) == 4 else (*r, None) for r in rows]
    S = rows[0][0].shape[-2]
    tile = min(tile, S)
    assert S % tile == 0
    nr, nb, no, na = len(rows), len(bcasts), len(outs), len(accs)

    def body(*refs):
        vals = [r[...] for r in refs[:nr + nb]]
        o_refs, a_refs = refs[nr + nb:nr + nb + no], refs[nr + nb + no:]
        o_vals, a_vals = fn(*vals)
        for r, v in zip(o_refs, o_vals):
            r[...] = v.astype(r.dtype)
        if na:
            @pl.when(pl.program_id(0) == 0)
            def _():
                for r in a_refs:
                    r[...] = jnp.zeros(r.shape, r.dtype)
            for r, v in zip(a_refs, a_vals):
                r[...] += v

    in_specs = []
    for (arr, w, cb, lead) in rows:
        if lead is None:
            in_specs.append(pl.BlockSpec((tile, w), lambda i, cb=cb: (i, cb)))
        else:
            in_specs.append(pl.BlockSpec((None, tile, w), lambda i, cb=cb, lead=lead: (lead, i, cb)))
    in_specs += [pl.BlockSpec(b.shape, lambda i: (0, 0)) for b in bcasts]
    out_specs = [pl.BlockSpec((tile, w), lambda i: (i, 0)) for (w, _) in outs]
    out_specs += [pl.BlockSpec(s, lambda i: (0, 0)) for s in accs]
    out_shape = [jax.ShapeDtypeStruct((S, w), dt) for (w, dt) in outs]
    out_shape += [jax.ShapeDtypeStruct(s, F32) for s in accs]
    return _call(
        body, name=name, grid=(S // tile,), in_specs=in_specs, out_specs=out_specs, out_shape=out_shape,
        compiler_params=_cparams(("arbitrary",)),
    )(*[r[0] for r in rows], *bcasts)


def _colsum(v):
    return jnp.sum(v, axis=0, keepdims=True)


def _rms(x, n):
    r = lax.rsqrt(jnp.sum(x * x, axis=-1, keepdims=True) * (1.0 / n) + EPS)
    return x * r, r


def _rms_bwd(dxhat, xhat, r, n):
    return r * (dxhat - xhat * (jnp.sum(dxhat * xhat, axis=-1, keepdims=True) * (1.0 / n)))


def _rope_fwd(r, cosp, s1, s2):
    return r * cosp + pltpu.roll(r, 32, 1) * s1 + pltpu.roll(r, 96, 1) * s2


def _rope_bwd(d, cosp, s1, s2):
    return d * cosp + pltpu.roll(d * s1, 96, 1) + pltpu.roll(d * s2, 32, 1)


def _sigmoid(x):
    return 1.0 / (1.0 + jnp.exp(-x))


def _halo_specs(tile, halo, width, cb, S, lead=None):
    nh = tile // halo
    last = S // halo - 1
    if lead is None:
        return [
            pl.BlockSpec((tile, width), lambda i: (i, cb)),
            pl.BlockSpec((halo, width), lambda i: (jnp.maximum(i * nh - 1, 0), cb)),
            pl.BlockSpec((halo, width), lambda i: (jnp.minimum((i + 1) * nh, last), cb)),
        ]
    return [
        pl.BlockSpec((None, tile, width), lambda i: (lead, i, cb)),
        pl.BlockSpec((None, halo, width), lambda i: (lead, jnp.maximum(i * nh - 1, 0), cb)),
        pl.BlockSpec((None, halo, width), lambda i: (lead, jnp.minimum((i + 1) * nh, last), cb)),
    ]


def _conv_fwd(proj, conv_w8, conv_b, *, tile=256):
    S = proj.shape[0]
    T = min(tile, S)
    n = S // T
    W = 2 * MW

    def body(x_ref, xp_ref, xn_ref, w_ref, b_ref, q_ref, k_ref, ext):
        i = pl.program_id(0)
        ext[pl.ds(0, 8), :] = xp_ref[...] * (i > 0).astype(F32)
        ext[pl.ds(8, T), :] = x_ref[...]
        ext[pl.ds(8 + T, 8), :] = xn_ref[...] * (i < n - 1).astype(F32)
        w = w_ref[...]
        y = b_ref[...] + w[0:1, :] * ext[pl.ds(6, T), :]
        for o in range(1, CONVW):
            y = y + w[o:o + 1, :] * ext[pl.ds(6 + o, T), :]
        y = y * _sigmoid(y)
        q_ref[...] = y[:, :MW].astype(q_ref.dtype)
        k_ref[...] = (y[:, MW:] * (DM ** -0.5)).astype(k_ref.dtype)

    return _call(
        body, name="conv_fwd", grid=(n,),
        in_specs=_halo_specs(T, 8, W, 0, S) + [pl.BlockSpec((8, W), lambda i: (0, 0)),
                                                 pl.BlockSpec((1, W), lambda i: (0, 0))],
        out_specs=[pl.BlockSpec((T, MW), lambda i: (i, 0))] * 2,
        out_shape=[jax.ShapeDtypeStruct((S, MW), _MXU_DTYPE)] * 2,
        scratch_shapes=[pltpu.VMEM((T + 16, W), F32)],
        compiler_params=_cparams(("arbitrary",)),
    )(proj, proj, proj, conv_w8, conv_b)


def _conv_bwd(proj, dqd, dkd, conv_w8, conv_b, *, tile=256):
    S = proj.shape[0]
    T = min(tile, S)
    n = S // T
    W = 2 * MW

    def body(x_ref, xp_ref, xn_ref, *rest):
        g = rest[:12]
        w_ref, b_ref, dx_ref, dw_ref, db_ref, ext, edp = rest[12:]
        i = pl.program_id(0)
        mp = (i > 0).astype(F32)
        mn = (i < n - 1).astype(F32)
        ext[pl.ds(0, 16), :] = xp_ref[...] * mp
        ext[pl.ds(16, T), :] = x_ref[...]
        ext[pl.ds(16 + T, 16), :] = xn_ref[...] * mn
        w = w_ref[...]
        pre = b_ref[...] + w[0:1, :] * ext[pl.ds(6, T + 16), :]
        for o in range(1, CONVW):
            pre = pre + w[o:o + 1, :] * ext[pl.ds(6 + o, T + 16), :]
        sg = _sigmoid(pre)
        dsilu = sg * (1.0 + pre * (1.0 - sg))
        for half, (a0, a1) in enumerate(((g[0:3], g[3:6]), (g[6:9], g[9:12]))):
            sc = 1.0 if half == 0 else DM ** -0.5
            cols = pl.ds(half * MW, MW)
            edp[pl.ds(0, 8), cols] = (a0[1][...] + a1[1][...]) * (mp * sc)
            edp[pl.ds(8, T), cols] = (a0[0][...] + a1[0][...]) * sc
            edp[pl.ds(8 + T, 8), cols] = (a0[2][...] + a1[2][...]) * (mn * sc)
        edp[...] = edp[...] * dsilu
        @pl.when(i == 0)
        def _():
            dw_ref[...] = jnp.zeros(dw_ref.shape, F32)
            db_ref[...] = jnp.zeros(db_ref.shape, F32)

        x_main = ext[pl.ds(16, T), :]
        dx = None
        for o in range(CONVW):
            view = edp[pl.ds(10 - o, T), :]
            dx = w[o:o + 1, :] * view if dx is None else dx + w[o:o + 1, :] * view
            dw_ref[pl.ds(o, 1), :] += _colsum(x_main * view)
        dx_ref[...] = dx.astype(dx_ref.dtype)
        db_ref[...] += _colsum(edp[pl.ds(8, T), :])

    gspecs = _halo_specs(T, 8, MW, 0, S) * 4
    return _call(
        body, name="conv_bwd", grid=(n,),
        in_specs=_halo_specs(T, 16, W, 0, S) + gspecs + [pl.BlockSpec((8, W), lambda i: (0, 0)),
                                                          pl.BlockSpec((1, W), lambda i: (0, 0))],
        out_specs=[pl.BlockSpec((T, W), lambda i: (i, 0)), pl.BlockSpec((8, W), lambda i: (0, 0)),
                   pl.BlockSpec((1, W), lambda i: (0, 0))],
        out_shape=[jax.ShapeDtypeStruct((S, W), _MXU_DTYPE), jax.ShapeDtypeStruct((8, W), F32),
                   jax.ShapeDtypeStruct((1, W), F32)],
        scratch_shapes=[pltpu.VMEM((T + 32, W), F32), pltpu.VMEM((T + 16, W), F32)],
        compiler_params=_cparams(("arbitrary",)),
    )(proj, proj, proj, *([dqd[0]] * 3), *([dqd[1]] * 3), *([dkd[0]] * 3), *([dkd[1]] * 3), conv_w8, conv_b)


_ATT_SCALE = QK ** -0.5
_LOG2E = math.log2(math.e)
_Q_PRESCALE = _ATT_SCALE * _LOG2E


def _side_exchange(side, gather, grid, cols=()):
    ns = len(side)
    io = _exchange_io(side, gather, cols) if ns else dict(specs=[], out_shape=[], scratch=[])

    def wrap(refs_in, refs_out, sems):
        if not ns:
            return (lambda: None), (lambda: None)
        start, wait = _exchange_ops(refs_in, refs_out, *sems, gather=gather, cols=cols)
        ids = [pl.program_id(a) for a in range(len(grid))]
        first = functools.reduce(jnp.logical_and, [i == 0 for i in ids])
        last = functools.reduce(jnp.logical_and, [i == g - 1 for i, g in zip(ids, grid)])
        return (lambda: pl.when(first)(start)), (lambda: pl.when(last)(wait))

    return ns, io, wrap


def _attn_fwd(q, k, v, *, side=(), side_cols=(), tq=1024, tk=8192, split=4, unroll=1):
    S = q.shape[0]
    tq, tk = min(tq, S), min(tk, S)
    nkv = S // tk
    hq = tq // split
    grid = (H_MLA, S // tq)
    ns, io, wrap = _side_exchange(side, True, grid, side_cols)

    def body(q_ref, k_ref, v_ref, *rest):
        o_ref, qa_ref = rest[ns:ns + 2]
        m_s, acc_s = rest[2 * ns + 2:2 * ns + 4]
        side_start, side_wait = wrap(rest[:ns], rest[ns + 2:2 * ns + 2], rest[2 * ns + 4:])
        side_start()
        m_s[...] = jnp.full(m_s.shape, -1e30, F32)
        acc_s[...] = jnp.zeros(acc_s.shape, F32)

        def step(j, carry):
            rows = pl.ds(pl.multiple_of(j * tk, tk), tk)
            kj, vj = k_ref[rows, :], v_ref[rows, :]
            for a in range(split):
                r = pl.ds(a * hq, hq)
                s = _dot(q_ref[r, :], kj, "nt")
                m_old = m_s[r, :]
                m_new = jnp.maximum(m_old, jnp.max(s, axis=1, keepdims=True))
                p = jnp.exp2(s - m_new)
                acc_s[r, :] = jnp.exp2(m_old - m_new) * acc_s[r, :] + _dot(p, vj, "nn")
                m_s[r, :] = m_new
            return carry

        lax.fori_loop(0, nkv, step, 0, unroll=unroll if nkv % unroll == 0 else 1)
        l = acc_s[:, VD:VD + 1]
        o_ref[...] = (acc_s[:, :VD] / l).astype(o_ref.dtype)
        lse = m_s[...] + jnp.log2(l)
        hi = lse.astype(_MXU_DTYPE).astype(F32)
        lane = lax.broadcasted_iota(jnp.int32, (tq, HP), 1)
        qa = jnp.where(lane == QK, -hi, jnp.where(lane == QK + 1, hi - lse, q_ref[...].astype(F32)))
        qa_ref[...] = qa.astype(qa_ref.dtype)
        side_wait()

    res = _call(
        body, name="attn_fwd", grid=grid,
        in_specs=[pl.BlockSpec((tq, HP), lambda h, i: (i, h)),
                  pl.BlockSpec((S, HP), lambda h, i: (0, h)),
                  pl.BlockSpec((S, HP), lambda h, i: (0, h))] + io["specs"],
        out_specs=[pl.BlockSpec((tq, VD), lambda h, i: (i, h)),
                   pl.BlockSpec((tq, HP), lambda h, i: (i, h))] + io["specs"],
        out_shape=[jax.ShapeDtypeStruct((S, H_MLA * VD), _MXU_DTYPE),
                   jax.ShapeDtypeStruct((S, H_MLA * HP), _MXU_DTYPE)] + io["out_shape"],
        scratch_shapes=[pltpu.VMEM((tq, 1), F32), pltpu.VMEM((tq, HP), F32)] + io["scratch"],
        compiler_params=_cparams(("arbitrary", "arbitrary")),
    )(q, k, v, *side)
    return res[0], res[1], list(res[2:])


def _attn_bwd(qa, k, va, doa, *, side=(), side_cols=(), tq=4096, tk=512, split=4, unroll=1):
    S = qa.shape[0]
    tq, tk = min(tq, S), min(tk, S)
    nq, nkb = S // tq, S // tk
    hq = tq // split
    grid = (H_MLA, nkb)
    ns, io, wrap = _side_exchange(side, False, grid, side_cols)

    def body(q_ref, k_ref, v_ref, do_ref, *rest):
        dq_ref, dk_ref, dv_ref = rest[ns:ns + 3]
        side_start, side_wait = wrap(rest[:ns], rest[ns + 3:2 * ns + 3], rest[2 * ns + 3:])
        side_start()
        j = pl.program_id(1)

        @pl.when(j == 0)
        def _():
            dq_ref[...] = jnp.zeros(dq_ref.shape, F32)

        dk_ref[...] = jnp.zeros(dk_ref.shape, F32)
        dv_ref[...] = jnp.zeros(dv_ref.shape, F32)
        kb, vb = k_ref[...], v_ref[...]

        def step(i, carry):
            for a in range(split):
                r = pl.ds(pl.multiple_of(i * tq + a * hq, hq), hq)
                qg, dog = q_ref[r, :], do_ref[r, :]
                p = jnp.exp2(_dot(qg, kb, "nt"))
                ds = (p * _dot(dog, vb, "nt")).astype(_MXU_DTYPE)
                dq_ref[r, :] += _dot(ds, kb, "nn")
                dv_ref[...] += _dot(p, dog, "tn")
                dk_ref[...] += _dot(ds, qg, "tn")
            return carry

        lax.fori_loop(0, nq, step, 0, unroll=unroll if nq % unroll == 0 else 1)
        dk_ref[...] = dk_ref[...] * (1.0 / _LOG2E)

        @pl.when(j == nkb - 1)
        def _():
            dq_ref[...] = dq_ref[...] * _ATT_SCALE

        side_wait()

    blk = pl.BlockSpec((tk, HP), lambda h, j: (j, h))
    whole = pl.BlockSpec((S, HP), lambda h, j: (0, h))
    res = _call(
        body, name="attn_bwd", grid=grid,
        in_specs=[whole, blk, blk, whole] + io["specs"],
        out_specs=[whole, blk, blk] + io["specs"],
        out_shape=[jax.ShapeDtypeStruct((S, H_MLA * HP), F32)] * 3 + io["out_shape"],
        scratch_shapes=io["scratch"],
        compiler_params=_cparams(("arbitrary", "arbitrary")),
    )(qa, k, va, doa, *side)
    return res[0], res[1], res[2], list(res[3:])


def _mlstm_chunk_terms(g, q, k, v, gates, gates_t, bg_row, C, n_row, m):
    L = LCH
    d = g // HM
    h = g % HM
    i_idx = d * 8 + h
    f_idx = d * 8 + 4 + h
    rr = lax.broadcasted_iota(jnp.int32, (L, L), 0)
    cc = lax.broadcasted_iota(jnp.int32, (L, L), 1)
    order = (rr - cc) * (1 - 2 * d)
    tri = order >= 0
    eye = rr == cc
    lane = lax.broadcasted_iota(jnp.int32, gates.shape, 1)
    sub = lax.broadcasted_iota(jnp.int32, gates_t.shape, 0)
    lane_b = lax.broadcasted_iota(jnp.int32, bg_row.shape, 1)
    pick_c = lambda idx: jnp.sum(jnp.where(lane == idx, gates, 0.0), axis=1, keepdims=True)
    pick_r = lambda idx: jnp.sum(jnp.where(sub == idx, gates_t, 0.0), axis=0, keepdims=True)
    pick_b = lambda idx: jnp.sum(jnp.where(lane_b == idx, bg_row, 0.0), axis=1, keepdims=True)
    i_col, i_row = pick_c(i_idx) + pick_b(i_idx), pick_r(i_idx) + pick_b(i_idx)
    f_col, f_row = pick_c(f_idx) + pick_b(f_idx), pick_r(f_idx) + pick_b(f_idx)
    logsig = lambda x: jnp.minimum(x, 0.0) - jnp.log(1.0 + jnp.exp(-jnp.abs(x)))
    lf_col, lf_row = logsig(f_col), logsig(f_row)
    b_col = jnp.sum(jnp.where(tri, lf_row, 0.0), axis=1, keepdims=True)
    tri_t = order <= 0
    b_row = jnp.sum(jnp.where(tri_t, lf_col, 0.0), axis=0, keepdims=True)
    bL = jnp.sum(lf_row, axis=1, keepdims=True)
    log_inter = b_col + m
    logD = jnp.where(tri, b_col - b_row + i_row, -jnp.inf)
    m_t = jnp.maximum(log_inter, jnp.max(logD, axis=1, keepdims=True))
    Dm = jnp.exp(logD - m_t)
    w_inter = jnp.exp(log_inter - m_t)
    A = _dot(q, k, "nt")
    Sc = A * Dm
    numI = _dot(q, C, "nt")
    qf = q.astype(F32)
    kf = k.astype(F32)
    denI = jnp.sum(qf * n_row, axis=1, keepdims=True)
    num = _dot(Sc, v, "nn") + w_inter * numI
    den = jnp.sum(Sc, axis=1, keepdims=True) + w_inter * denI
    floor = jnp.exp(-m_t)
    Nst = jnp.maximum(jnp.abs(den), floor)
    log_w = bL - b_col + i_col
    m_new = jnp.maximum(bL + m, jnp.max(log_w, axis=0, keepdims=True))
    decay = jnp.exp(bL + m - m_new)
    w_col = jnp.exp(log_w - m_new)
    return dict(tri=tri, eye=eye, f_row=f_row, Dm=Dm, w_inter=w_inter, A=A, Sc=Sc, numI=numI, denI=denI,
                num=num, den=den, floor=floor, Nst=Nst, m_new=m_new, decay=decay, w_col=w_col, qf=qf, kf=kf)


def _mlstm_specs(nc, d, step_of):
    chunk = lambda j: step_of(j) if d == 0 else nc - 1 - step_of(j)
    return chunk, [
        pl.BlockSpec((LCH, DM), lambda h, j: (chunk(j), h)),
        pl.BlockSpec((LCH, DM), lambda h, j: (chunk(j), h)),
        pl.BlockSpec((LCH, DM), lambda h, j: (chunk(j), P_VM // DM + h)),
        pl.BlockSpec((LCH, 128), lambda h, j: (chunk(j), P_G // 128)),
        pl.BlockSpec((NG, LCH), lambda h, j: (0, chunk(j))),
    ]


def _mlstm_fwd(qc, kc, proj, gates_t, bg_row):
    S = qc.shape[0]
    nc = S // LCH
    in_specs, out_specs = [], []
    for d in (0, 1):
        chunk, specs = _mlstm_specs(nc, d, lambda j: j)
        in_specs += specs
        out_specs += [pl.BlockSpec((LCH, DM), lambda h, j, chunk=chunk: (chunk(j), h)),
                      pl.BlockSpec((None, None, DM, DM), lambda h, j, chunk=chunk: (h, chunk(j), 0, 0)),
                      pl.BlockSpec((None, None, 8, DM), lambda h, j, chunk=chunk: (h, chunk(j), 0, 0))]
    in_specs.append(pl.BlockSpec((1, 128), lambda h, j: (0, 0)))

    def body(*refs):
        bg_ref, outs, (C_s, n_s, m_s) = refs[10], refs[11:17], refs[17:]

        @pl.when(pl.program_id(1) == 0)
        def _():
            C_s[...] = jnp.zeros(C_s.shape, F32)
            n_s[...] = jnp.zeros(n_s.shape, F32)
            m_s[...] = jnp.full(m_s.shape, M_INIT, F32)

        for d in (0, 1):
            q_ref, k_ref, v_ref, g_ref, gt_ref = refs[5 * d:5 * d + 5]
            h_ref, cst_ref, nm_ref = outs[3 * d:3 * d + 3]
            g = d * HM + pl.program_id(0)
            C, n_row, m = C_s[d], n_s[d, 0:1, :], m_s[d, 0:1, 0:1]
            cst_ref[...] = C
            nm_ref[0:1, :] = n_row
            nm_ref[1:2, :] = jnp.broadcast_to(m, (1, DM))
            nm_ref[2:8, :] = jnp.zeros((6, DM), F32)
            q, k, v = q_ref[...], k_ref[...], v_ref[...]
            t = _mlstm_chunk_terms(g, q, k, v, g_ref[...], gt_ref[...], bg_ref[...], C, n_row, m)
            h_ref[...] = t["num"] / t["Nst"]
            wv = t["w_col"] * v
            C_s[d] = t["decay"] * C + _dot(wv, k, "tn")
            n_s[d, 0:1, :] = t["decay"] * n_row + _colsum(t["w_col"] * t["kf"])
            m_s[d] = jnp.broadcast_to(t["m_new"], (8, 128))

    res = _call(
        body, name="mlstm_fwd", grid=(HM, nc), in_specs=in_specs, out_specs=out_specs,
        out_shape=[jax.ShapeDtypeStruct((S, MW), F32), jax.ShapeDtypeStruct((HM, nc, DM, DM), F32),
                   jax.ShapeDtypeStruct((HM, nc, 8, DM), F32)] * 2,
        scratch_shapes=[pltpu.VMEM((2, DM, DM), F32), pltpu.VMEM((2, 8, DM), F32), pltpu.VMEM((2, 8, 128), F32)],
        compiler_params=_cparams(("parallel", "arbitrary")),
    )(*([qc, kc, proj, proj, gates_t] * 2), bg_row)
    return (res[0], res[3]), (res[1], res[4]), (res[2], res[5])


def _mlstm_bwd(qc, kc, proj, gates_t, bg_row, dh, cst, nm):
    S = qc.shape[0]
    nc = S // LCH
    in_specs, out_specs = [], []
    for d in (0, 1):
        chunk, specs = _mlstm_specs(nc, d, lambda j: nc - 1 - j)
        in_specs += specs + [pl.BlockSpec((LCH, DM), lambda h, j, chunk=chunk: (chunk(j), h)),
                             pl.BlockSpec((None, None, DM, DM), lambda h, j, chunk=chunk: (h, chunk(j), 0, 0)),
                             pl.BlockSpec((None, None, 8, DM), lambda h, j, chunk=chunk: (h, chunk(j), 0, 0))]
        out_specs += [pl.BlockSpec((LCH, DM), lambda h, j, chunk=chunk: (chunk(j), h))] * 3
        out_specs += [pl.BlockSpec((None, None, 8, LCH), lambda h, j, chunk=chunk: (h, chunk(j), 0, 0))]
    in_specs.append(pl.BlockSpec((1, 128), lambda h, j: (0, 0)))

    def body(*refs):
        bg_ref, outs, (dC_s, dn_s) = refs[16], refs[17:25], refs[25:]

        @pl.when(pl.program_id(1) == 0)
        def _():
            dC_s[...] = jnp.zeros(dC_s.shape, F32)
            dn_s[...] = jnp.zeros(dn_s.shape, F32)

        for d in (0, 1):
            _mlstm_bwd_chain(d, refs[8 * d:8 * d + 8], bg_ref, outs[4 * d:4 * d + 4], dC_s, dn_s)

    res = _call(
        body, name="mlstm_bwd", grid=(HM, nc), in_specs=in_specs, out_specs=out_specs,
        out_shape=([jax.ShapeDtypeStruct((S, MW), F32)] * 3 + [jax.ShapeDtypeStruct((HM, nc, 8, LCH), F32)]) * 2,
        scratch_shapes=[pltpu.VMEM((2, DM, DM), F32), pltpu.VMEM((2, 8, DM), F32)],
        compiler_params=_cparams(("parallel", "arbitrary")),
    )(*[a for d in (0, 1) for a in (qc, kc, proj, proj, gates_t, dh, cst[d], nm[d])], bg_row)
    return (res[0], res[4]), (res[1], res[5]), (res[2], res[6]), (res[3], res[7])


def _mlstm_bwd_chain(d, ins, bg_ref, outs, dC_s, dn_s):
        q_ref, k_ref, v_ref, g_ref, gt_ref, dh_ref, cst_ref, nm_ref = ins
        dq_ref, dk_ref, dv_ref, dg_ref = outs
        g = d * HM + pl.program_id(0)
        C, n_row, m = cst_ref[...], nm_ref[0:1, :], nm_ref[1:2, 0:1]
        q, k, v = q_ref[...], k_ref[...], v_ref[...]
        t = _mlstm_chunk_terms(g, q, k, v, g_ref[...], gt_ref[...], bg_ref[...], C, n_row, m)
        tri, eye, qf, kf = t["tri"], t["eye"], t["qf"], t["kf"]
        w_inter, w_col, decay, Nst = t["w_inter"], t["w_col"], t["decay"], t["Nst"]
        dC, dn = dC_s[d], dn_s[d, 0:1, :]
        dhv = dh_ref[...]
        hval = t["num"] / Nst
        dnum = dhv / Nst
        dNst = -jnp.sum(dhv * hval, axis=1, keepdims=True) / Nst
        dden = jnp.where(jnp.abs(t["den"]) > t["floor"], jnp.sign(t["den"]) * dNst, 0.0)
        dSc = _dot(dnum, v, "nt") + dden
        dA = dSc * t["Dm"]
        G = dSc * t["Sc"]
        KdC = _dot(k, dC, "nt")
        dq = _dot(dA, k, "nn") + w_inter * _dot(dnum, C, "nn") + (w_inter * dden) * n_row
        dk = _dot(dA, q, "tn") + w_col * _dot(v, dC, "nn") + w_col * dn
        dv = _dot(t["Sc"], dnum, "tn") + w_col * KdC
        dq_ref[...] = dq
        dk_ref[...] = dk
        dv_ref[...] = dv
        dlog_inter = w_inter * (jnp.sum(dnum * t["numI"], axis=1, keepdims=True) + dden * t["denI"])
        rowG = jnp.sum(G, axis=1, keepdims=True)
        colG = jnp.sum(G, axis=0, keepdims=True)
        u_col = w_col * (jnp.sum(v * KdC, axis=1, keepdims=True) + jnp.sum(kf * dn, axis=1, keepdims=True))
        colG_c = jnp.sum(jnp.where(eye, colG, 0.0), axis=1, keepdims=True)
        u_row = jnp.sum(jnp.where(eye, u_col, 0.0), axis=0, keepdims=True)
        db_col = rowG + dlog_inter - u_col - colG_c
        dbL = jnp.sum(u_col, axis=0, keepdims=True) + decay * (
            jnp.sum(jnp.sum(dC * C, axis=1, keepdims=True), axis=0, keepdims=True)
            + jnp.sum(dn * n_row, axis=1, keepdims=True))
        dlf_row = jnp.sum(jnp.where(tri, db_col, 0.0), axis=0, keepdims=True) + dbL
        di_row = colG + u_row
        df_row = dlf_row * (1.0 - _sigmoid(t["f_row"]))
        dg_ref[...] = jnp.zeros(dg_ref.shape, F32)
        dg_ref[0:1, :] = di_row
        dg_ref[1:2, :] = df_row
        dC_s[d] = decay * dC + _dot(w_inter * dnum, q, "tn")
        dn_s[d, 0:1, :] = decay * dn + _colsum((w_inter * dden) * qf)


def _pad_w_in(w):
    cq, ckv, kpe, qm, km, vm, om, gt = _split_in(w)
    z = lambda n: jnp.zeros((w.shape[0], n), w.dtype)
    return jnp.concatenate([qm, km, vm, om, cq, ckv, kpe, z(HP - QK), gt, z(128 - NG)], axis=1)


def _split_in(w):
    out, o = [], 0
    for n in IN_SIZES:
        out.append(w[:, o:o + n])
        o += n
    return out


def _unpad_w_in(g):
    return jnp.concatenate([g[:, P_CQ:P_CQ + Q_LORA], g[:, P_CKV:P_CKV + KV_LORA], g[:, P_KPE:P_KPE + ROPE],
                            g[:, 0:4 * MW], g[:, P_G:P_G + NG]], axis=1)


_IN_SHARD = D_IN // 4
_IN_SEGMENTS = ((0, 512, P_CQ), (512, 768, P_CKV), (768, 832, P_KPE), (832, 4928, P_QM), (4928, 4944, P_G))


def _pad_w_in_slabs(slabs):
    def orig(a, b):
        out = []
        for k in range(4):
            lo, hi = max(a, k * _IN_SHARD), min(b, (k + 1) * _IN_SHARD)
            if lo < hi:
                out.append(slabs[k][:, lo - k * _IN_SHARD:hi - k * _IN_SHARD])
        return out

    z = lambda n: jnp.zeros((slabs.shape[1], n), slabs.dtype)
    return jnp.concatenate(orig(832, 4928) + orig(0, 512) + orig(512, 768) + orig(768, 832) + [z(HP - QK)]
                           + orig(4928, 4944) + [z(128 - NG)], axis=1)


def _unpad_w_in_slabs(g):
    slabs = []
    for k in range(4):
        pieces = []
        for a, b, p in _IN_SEGMENTS:
            lo, hi = max(a, k * _IN_SHARD), min(b, (k + 1) * _IN_SHARD)
            if lo < hi:
                pieces.append(g[:, p + lo - a:p + hi - a])
        slabs.append(jnp.concatenate(pieces, axis=1))
    return jnp.stack(slabs)


def _pad_w_uq(w):
    return jnp.pad(w.reshape(Q_LORA, H_MLA, QK), ((0, 0), (0, 0), (0, HP - QK))).reshape(Q_LORA, H_MLA * HP)


def _unpad_w_uq(g):
    return g.reshape(Q_LORA, H_MLA, HP)[:, :, :QK].reshape(Q_LORA, H_MLA * QK)


def _perm_w_ukv(w):
    return w.reshape(KV_LORA, H_MLA, 2, NOPE).transpose(0, 2, 1, 3).reshape(KV_LORA, 2 * H_MLA * NOPE)


def _unperm_w_ukv(g):
    return g.reshape(KV_LORA, 2, H_MLA, NOPE).transpose(0, 2, 1, 3).reshape(KV_LORA, 2 * H_MLA * NOPE)


def _rope_tables(positions):
    half = ROPE // 2
    freqs = ROPE_THETA ** (-jnp.arange(half, dtype=F32) / half)
    ang = positions.astype(F32)[:, None] * freqs
    cos, sin = jnp.cos(ang), jnp.sin(ang)
    z32, z64 = jnp.zeros_like(cos), jnp.zeros((cos.shape[0], 64), F32)
    return (jnp.concatenate([cos, cos, z64], axis=1), jnp.concatenate([z32, sin, z64], axis=1),
            jnp.concatenate([-sin, z32, z64], axis=1))


def _device_step(x, tgt, positions, modv, W, late=None):
    S = x.shape[0]
    MX = _MXU_DTYPE
    cosp, rs1, rs2 = _rope_tables(positions)
    tabs = [(cosp, 128, 0), (rs1, 128, 0), (rs2, 128, 0)]
    cat1 = lambda vs: jnp.concatenate(vs, axis=1)
    hsl = lambda hh, w: slice(hh * w, (hh + 1) * w)

    def ln1(xv, g, mv):
        xhat, _ = _rms(xv, D)
        return [xhat * g * (1.0 + mv[1:2]) + mv[0:1]], []

    (h,) = _rowmap(ln1, [(x, D, 0)], [W["g_mix"], modv], [(D, MX)], tile=512, name="ln1")
    proj = _mm(h, W["w_in"], "nn", name="proj")

    def lora(cq, ckv, gq, gkv):
        return [_rms(cq, Q_LORA)[0] * gq, _rms(ckv, KV_LORA)[0] * gkv], []

    cqn, ckvn = _rowmap(lora, [(proj, Q_LORA, P_CQ // Q_LORA), (proj, KV_LORA, P_CKV // KV_LORA)],
                        [W["g_qlora"], W["g_kvlora"]], [(Q_LORA, MX), (KV_LORA, MX)], tile=512, name="lora_norm")
    q_raw = _mm(cqn, W["w_uq"], "nn", name="q_up")
    kv_raw = _mm(ckvn, W["w_ukv"], "nn", name="kv_up")

    def mla_q(qr, cp, a1, a2, gq):
        outs = []
        for hh in range(H_MLA):
            y = _rms(qr[:, hsl(hh, HP)], QK)[0] * gq
            outs += [y[:, :NOPE], _rope_fwd(y[:, NOPE:], cp, a1, a2)]
        return [cat1(outs) * _Q_PRESCALE], []

    (qh,) = _rowmap(mla_q, [(q_raw, H_MLA * HP, 0)] + tabs, [W["gq"]], [(H_MLA * HP, MX)], tile=512, name="mla_q")

    def mla_k(kvr, kpe, cp, a1, a2, gk):
        lane = lax.broadcasted_iota(jnp.int32, (kvr.shape[0], 128), 1)
        outs, vas = [], []
        for hh in range(H_MLA):
            y = _rms(cat1([kvr[:, hsl(hh, NOPE)], kpe]), QK)[0] * gk
            outs += [y[:, :NOPE], _rope_fwd(y[:, NOPE:], cp, a1, a2) + ((lane == ROPE) | (lane == ROPE + 1)).astype(F32)]
            vas += [kvr[:, H_MLA * NOPE + hh * VD:H_MLA * NOPE + (hh + 1) * VD], (lane < 2).astype(F32)]
        return [cat1(outs), cat1(vas)], []

    kh, va = _rowmap(mla_k, [(kv_raw, 2 * H_MLA * NOPE, 0), (proj, 128, P_KPE // 128)] + tabs, [W["gk"]],
                     [(H_MLA * HP, MX), (H_MLA * HP, MX)], tile=512, name="mla_k")
    attn_o, qa, gathered = _attn_fwd(qh, kh, va, side=late or (), side_cols=(1,))
    if late:
        W = dict(W, w_out=gathered[0].reshape(D, D), w_ff1=gathered[1], w_ff2=gathered[2].reshape(DFF, D))

    qc, kc = _conv_fwd(proj, W["conv_w8"], W["conv_b"])
    gates_t = proj[:, P_G:P_G + NG].T
    (h_f, h_b), cst, nm = _mlstm_fwd(qc, kc, proj, gates_t, W["bg_row"])
    hrows = [(h_f, MW, 0), (h_b, MW, 0), (proj, MW, P_OM // MW)]

    def ml_out(ao, hf, hb, om, gmn):
        outs = [ao.astype(F32)]
        hs = hf + hb
        for hh in range(HM):
            sl = hsl(hh, DM)
            outs.append(_sigmoid(om[:, sl]) * _rms(hs[:, sl], DM)[0] * gmn[:, sl])
        return [cat1(outs)], []

    (cat,) = _rowmap(ml_out, [(attn_o, MW, 0)] + hrows, [W["g_mn"]], [(D, MX)], tile=512, name="ml_out")
    mixed = _mm(cat, W["w_out"], "nn", name="out_proj")

    def res_ln2(xv, mx, g, mv):
        x1 = xv + mv[2:3] * mx
        return [x1, _rms(x1, D)[0] * g * (1.0 + mv[4:5]) + mv[3:4]], []

    x1, h2 = _rowmap(res_ln2, [(x, D, 0), (mixed, D, 0)], [W["g_mlp"], modv], [(D, F32), (D, MX)],
                     tile=512, name="res_ln2")
    a, u = _mm(h2, W["w_ff1"], "nn", name="ff1", out_dtypes=(MX, MX),
               epilogue=lambda r: (jnp.square(jnp.maximum(r, 0.0)), r))
    y = _mm(a, W["w_ff2"], "nn", name="ff2")

    def final(x1v, yv, tv, mv):
        err = x1v + mv[5:6] * yv - tv
        dout = err * (1.0 / D)
        loss = jnp.sum(jnp.sum(0.5 * err * dout, axis=1, keepdims=True), axis=0, keepdims=True)
        return [dout, mv[5:6] * dout], [loss, _colsum(dout * yv)]

    dout, dy, loss, dgate2 = _rowmap(final, [(x1, D, 0), (y, D, 0), (tgt, D, 0)], [modv], [(D, F32), (D, MX)],
                                     [(1, 1), (1, D)], tile=256, name="loss_head")

    du = _mm(dy, W["w_ff2"], "nt", name="ff2_dx", out_dtypes=(MX,), extras=(u,),
             epilogue=lambda r, uv: (r * (2.0 * jnp.maximum(uv.astype(F32), 0.0)),))
    gdt = (MX,)
    g_ff2 = _mm(a, dy, "tn", name="ff2_dw", out_dtypes=gdt)
    dh2 = _mm(du, W["w_ff1"], "nt", name="ff1_dx")
    g_ff1 = _mm(h2, du, "tn", name="ff1_dw", out_dtypes=gdt)

    def ln2_bwd(dh2v, x1v, doutv, mxv, g, mv):
        xhat, r = _rms(x1v, D)
        dn2 = dh2v * (1.0 + mv[4:5])
        dx1 = doutv + _rms_bwd(dn2 * g, xhat, r, D)
        return [dx1, mv[2:3] * dx1], [_colsum(dh2v), _colsum(dh2v * xhat * g), _colsum(dn2 * xhat), _colsum(dx1 * mxv)]

    dx1, dmixed, dshift2, dscale2, dg_mlp, dgate1 = _rowmap(
        ln2_bwd, [(dh2, D, 0), (x1, D, 0), (dout, D, 0), (mixed, D, 0)], [W["g_mlp"], modv],
        [(D, F32), (D, MX)], [(1, D)] * 4, tile=256, name="ln2_bwd")
    dcat = _mm(dmixed, W["w_out"], "nt", name="out_dx")
    g_out = _mm(cat, dmixed, "tn", name="out_dw", out_dtypes=gdt)

    def ml_out_bwd(dml, hf, hb, om, gmn):
        hs = hf + hb
        dhs, dos, dgs = [], [], []
        for hh in range(HM):
            sl = hsl(hh, DM)
            xhat, r = _rms(hs[:, sl], DM)
            g, sg, d = gmn[:, sl], _sigmoid(om[:, sl]), dml[:, sl]
            dos.append(d * xhat * g * sg * (1.0 - sg))
            dhn = d * sg
            dgs.append(_colsum(dhn * xhat))
            dhs.append(_rms_bwd(dhn * g, xhat, r, DM))
        return [cat1(dhs), cat1(dos)], [cat1(dgs)]

    dhs, do_m, dg_mn = _rowmap(ml_out_bwd, [(dcat, MW, 1)] + hrows, [W["g_mn"]], [(MW, F32), (MW, MX)],
                               [(1, MW)], tile=512, name="ml_out_bwd")
    dqd, dkd, dvd, dgates = _mlstm_bwd(qc, kc, proj, gates_t, W["bg_row"], dhs, cst, nm)
    dqk_m, dconv_w8, dconv_b = _conv_bwd(proj, dqd, dkd, W["conv_w8"], W["conv_b"])

    def do_aug(ao, dov):
        lane = lax.broadcasted_iota(jnp.int32, (ao.shape[0], 128), 1)
        outs = []
        for hh in range(H_MLA):
            sl = hsl(hh, VD)
            dl = jnp.sum(ao[:, sl].astype(F32) * dov[:, sl], axis=1, keepdims=True)
            hi = dl.astype(MX).astype(F32)
            outs += [dov[:, sl], jnp.where(lane == 0, -hi, jnp.where(lane == 1, hi - dl, 0.0))]
        return [cat1(outs)], []

    (doa,) = _rowmap(do_aug, [(attn_o, MW, 0), (dcat, MW, 0)], [], [(H_MLA * HP, MX)], tile=512, name="attn_delta")
    side = [g_out.reshape(N_CHIP, D // N_CHIP, D), g_ff1, g_ff2.reshape(N_CHIP, DFF // N_CHIP, D)] if late else ()
    dq_a, dk_a, dv_a, late_got = _attn_bwd(qa, kh, va, doa, side=side, side_cols=(1,))

    def mla_q_bwd(dqv, qr, cp, a1, a2, gq):
        outs, dg = [], 0.0
        for hh in range(H_MLA):
            sl = hsl(hh, HP)
            xhat, r = _rms(qr[:, sl], QK)
            d = dqv[:, sl]
            dyv = cat1([d[:, :NOPE], _rope_bwd(d[:, NOPE:], cp, a1, a2)])
            dg = dg + _colsum(dyv * xhat)
            outs.append(_rms_bwd(dyv * gq, xhat, r, QK))
        return [cat1(outs)], [dg]

    dq_raw, dgq = _rowmap(mla_q_bwd, [(dq_a, H_MLA * HP, 0), (q_raw, H_MLA * HP, 0)] + tabs, [W["gq"]],
                          [(H_MLA * HP, MX)], [(1, HP)], tile=512, name="mla_q_bwd")
    dcqn = _mm(dq_raw, W["w_uq"], "nt", name="q_up_dx")
    g_uq = _mm(cqn, dq_raw, "tn", name="q_up_dw", out_dtypes=gdt)

    def mla_k_bwd(dkv, dvv, kvr, kpe, cp, a1, a2, gk):
        dkn, dg, dkpe = [], 0.0, 0.0
        for hh in range(H_MLA):
            xhat, r = _rms(cat1([kvr[:, hsl(hh, NOPE)], kpe]), QK)
            d = dkv[:, hsl(hh, HP)]
            dyv = cat1([d[:, :NOPE], _rope_bwd(d[:, NOPE:], cp, a1, a2)])
            dg = dg + _colsum(dyv * xhat)
            dxv = _rms_bwd(dyv * gk, xhat, r, QK)
            dkn.append(dxv[:, :NOPE])
            dkpe = dkpe + dxv[:, NOPE:]
        return [cat1(dkn + [dvv[:, hh * HP:hh * HP + VD] for hh in range(H_MLA)]), dkpe], [dg]

    dkv_raw, dkpe, dgk = _rowmap(
        mla_k_bwd, [(dk_a, H_MLA * HP, 0), (dv_a, H_MLA * HP, 0), (kv_raw, 2 * H_MLA * NOPE, 0),
                    (proj, 128, P_KPE // 128)] + tabs, [W["gk"]],
        [(2 * H_MLA * NOPE, MX), (128, MX)], [(1, HP)], tile=256, name="mla_k_bwd")
    dckvn = _mm(dkv_raw, W["w_ukv"], "nt", name="kv_up_dx")
    g_ukv = _mm(ckvn, dkv_raw, "tn", name="kv_up_dw", out_dtypes=gdt)

    def lora_bwd(dcq, dckv, cq, ckv, gq, gkv):
        xq, rq = _rms(cq, Q_LORA)
        xk, rk = _rms(ckv, KV_LORA)
        return ([_rms_bwd(dcq * gq, xq, rq, Q_LORA), _rms_bwd(dckv * gkv, xk, rk, KV_LORA)],
                [_colsum(dcq * xq), _colsum(dckv * xk)])

    dc_q, dc_kv, dg_qlora, dg_kvlora = _rowmap(
        lora_bwd, [(dcqn, Q_LORA, 0), (dckvn, KV_LORA, 0), (proj, Q_LORA, P_CQ // Q_LORA),
                   (proj, KV_LORA, P_CKV // KV_LORA)], [W["g_qlora"], W["g_kvlora"]],
        [(Q_LORA, MX), (KV_LORA, MX)], [(1, Q_LORA), (1, KV_LORA)], tile=512, name="lora_bwd")

    nc = S // LCH
    dg16 = jnp.stack(dgates)[:, :, :, 0:2, :].transpose(2, 4, 0, 3, 1).reshape(S, NG)
    dg128 = jnp.pad(dg16, ((0, 0), (0, 128 - NG)))

    def assemble(dqk, dv0, dv1, dom, dcq, dckv, dkp, dgp):
        f = lambda t: t.astype(F32)
        return [cat1([f(dqk), dv0 + dv1, f(dom), f(dcq), f(dckv), f(dkp), dgp])], [_colsum(dgp)]

    dproj, dbg = _rowmap(
        assemble, [(dqk_m, 2 * MW, 0), (dvd[0], MW, 0), (dvd[1], MW, 0), (do_m, MW, 0), (dc_q, Q_LORA, 0),
                   (dc_kv, KV_LORA, 0), (dkpe, 128, 0), (dg128, 128, 0)], [], [(D_INP, MX)], [(1, 128)],
        tile=256, name="dproj")
    g_in = _mm(h, dproj, "tn", name="proj_dw", out_dtypes=gdt)
    early_got = ()
    if late:
        side = [_unpad_w_in_slabs(g_in).astype(MX), _slabs(_unpad_w_uq(g_uq)).astype(MX),
                _slabs(_unperm_w_ukv(g_ukv)).astype(MX)]
        dh, early_got = _mm(dproj, W["w_in"], "nt", name="proj_dx", side=side)
    else:
        dh = _mm(dproj, W["w_in"], "nt", name="proj_dx")

    def ln1_bwd(dhv, xv, dx1v, g, mv):
        xhat, r = _rms(xv, D)
        dn = dhv * (1.0 + mv[1:2])
        return [dx1v + _rms_bwd(dn * g, xhat, r, D)], [_colsum(dhv), _colsum(dhv * xhat * g), _colsum(dn * xhat)]

    gx, dshift1, dscale1, dg_mix = _rowmap(ln1_bwd, [(dh, D, 0), (x, D, 0), (dx1, D, 0)], [W["g_mix"], modv],
                                           [(D, F32)], [(1, D)] * 3, tile=256, name="ln1_bwd")
    dmodv = jnp.concatenate([dshift1, dscale1, dgate1, dshift2, dscale2, dgate2], axis=0)
    grads = dict(w_in=g_in, w_uq=g_uq, w_ukv=g_ukv, w_out=g_out, w_ff1=g_ff1, w_ff2=g_ff2,
                 norm_mix_g=dg_mix, b_gates=dbg[:, :NG], conv_w=dconv_w8[:CONVW], conv_b=dconv_b,
                 q_lora_g=dg_qlora, kv_lora_g=dg_kvlora, q_norm_g=dgq[:, :QK], k_norm_g=dgk[:, :QK],
                 mlstm_norm_g=dg_mn, norm_mlp_g=dg_mlp)
    grads["got"] = list(early_got) + list(late_got)
    return loss, gx, dmodv, grads


def _cols(g):
    return g.transpose(1, 0, 2).reshape(g.shape[1], N_CHIP * g.shape[2])


def _slabs(gfull):
    return gfull.reshape(gfull.shape[0], N_CHIP, -1).transpose(1, 0, 2)


def _prep_weights(w_in, w_uq, w_ukv, w_out, w_ff1, w_ff2, norm_mix_g, norm_mlp_g, q_lora_g, kv_lora_g,
                  q_norm_g, k_norm_g, mlstm_norm_g, conv_w, conv_b, b_gates):
    MX = _MXU_DTYPE
    padg = lambda g: jnp.pad(g.reshape(1, QK).astype(F32), ((0, 0), (0, HP - QK)))
    return dict(
        w_in=(_pad_w_in_slabs(w_in) if w_in.ndim == 3 else _pad_w_in(w_in)).astype(MX), w_uq=_pad_w_uq(w_uq).astype(MX), w_ukv=_perm_w_ukv(w_ukv).astype(MX),
        w_out=None if w_out is None else w_out.astype(MX), w_ff1=None if w_ff1 is None else w_ff1.astype(MX),
        w_ff2=None if w_ff2 is None else w_ff2.astype(MX),
        g_mix=norm_mix_g.reshape(1, D), g_mlp=norm_mlp_g.reshape(1, D), g_qlora=q_lora_g.reshape(1, Q_LORA),
        g_kvlora=kv_lora_g.reshape(1, KV_LORA), gq=padg(q_norm_g), gk=padg(k_norm_g),
        g_mn=mlstm_norm_g.reshape(1, MW), conv_w8=jnp.pad(conv_w.reshape(CONVW, 2 * MW), ((0, 8 - CONVW), (0, 0))),
        conv_b=conv_b.reshape(1, 2 * MW), bg_row=jnp.pad(b_gates.reshape(1, NG), ((0, 0), (0, 128 - NG))))


MESH = pl.DeviceIdType.MESH
N_DEV = 8
N_CHIP = 4


def _comm_call(body, **kw):
    if _INTERPRET:
        kw["interpret"] = pltpu.InterpretParams()
    return pl.pallas_call(body, **kw)


def _allgather8(blk, *, name):
    m_per, n = blk.shape

    def body(x_ref, out_ref, send_sems, recv_sems, local_sem):
        x, y, c = lax.axis_index("x"), lax.axis_index("y"), lax.axis_index("c")
        me, sibling = (x, y, c), (x, y, 1 - c)
        chips = [(1 - x, y), (x, 1 - y), (1 - x, 1 - y)]

        def rows(px, py, pc):
            return out_ref.at[pl.ds((4 * px + 2 * py + pc) * m_per, m_per), :]

        def copy(k, block, to, src=None):
            return pltpu.make_async_remote_copy(
                src_ref=rows(*block) if src is None else src, dst_ref=rows(*block),
                send_sem=send_sems.at[k], recv_sem=recv_sems.at[k], device_id=to, device_id_type=MESH)

        mine = pltpu.make_async_copy(x_ref, rows(*me), local_sem)
        mine.start()
        first = [copy(0, me, sibling, src=x_ref)]
        first += [copy(1 + j, me, (*chip, c), src=x_ref) for j, chip in enumerate(chips)]
        for cp in first:
            cp.start()
        passed = [copy(4 + j, (*chip, c), sibling) for j, chip in enumerate(chips)]
        for j, chip in enumerate(chips):
            copy(1 + j, (*chip, c), me).wait_recv()
            passed[j].start()
        copy(0, sibling, me).wait_recv()
        for j, chip in enumerate(chips):
            copy(4 + j, (*chip, 1 - c), me).wait_recv()
        for cp in first + passed:
            cp.wait_send()
        mine.wait()

    return _comm_call(
        body, name=name, out_shape=jax.ShapeDtypeStruct((N_DEV * m_per, n), blk.dtype),
        in_specs=[pl.BlockSpec(memory_space=pltpu.VMEM)], out_specs=pl.BlockSpec(memory_space=pltpu.VMEM),
        scratch_shapes=[pltpu.SemaphoreType.DMA((7,)), pltpu.SemaphoreType.DMA((7,)), pltpu.SemaphoreType.DMA],
    )(blk)


def _chip_exchange(arrays, *, gather, name):
    n = len(arrays)

    def body(*refs):
        start, wait = _exchange_ops(refs[:n], refs[n:2 * n], *refs[2 * n:], gather=gather)
        start()
        wait()

    io = _exchange_io(arrays, gather)
    return _comm_call(body, name=name, out_shape=io["out_shape"], in_specs=io["specs"], out_specs=io["specs"],
                      scratch_shapes=io["scratch"])(*arrays)


def _exchange_io(arrays, gather, cols=()):
    n = len(arrays)

    def out(i, a):
        if gather:
            return (a.shape[0], N_CHIP * a.shape[1]) if i in cols else (N_CHIP, *a.shape)
        return (N_CHIP, a.shape[0], a.shape[1] // N_CHIP) if i in cols else a.shape

    return dict(
        specs=[pl.BlockSpec(memory_space=pltpu.HBM)] * n,
        out_shape=[jax.ShapeDtypeStruct(out(i, a), a.dtype) for i, a in enumerate(arrays)],
        scratch=[pltpu.SemaphoreType.DMA((3 * n,)), pltpu.SemaphoreType.DMA((3 * n,)), pltpu.SemaphoreType.DMA((n,))])


def _exchange_ops(ins, outs, send_sems, recv_sems, local_sems, *, gather, cols=()):
    n = len(ins)
    x, y, c = lax.axis_index("x"), lax.axis_index("y"), lax.axis_index("c")
    k = 2 * x + y
    chips = [(1 - x, y), (x, 1 - y), (1 - x, 1 - y)]

    def piece(ref, a, chip, windowed):
        if not windowed:
            return ref.at[chip]
        width = ref.shape[1] // N_CHIP
        return ref.at[:, pl.ds(pl.multiple_of(chip * width, 128), width)]

    src_of = lambda a, chip: ins[a] if gather else piece(ins[a], a, chip, a in cols)
    dst_of = lambda a, chip: piece(outs[a], a, chip, gather and a in cols)

    def remote(a, j):
        px, py = chips[j]
        return pltpu.make_async_remote_copy(
            src_ref=src_of(a, 2 * px + py), dst_ref=dst_of(a, k), send_sem=send_sems.at[3 * a + j],
            recv_sem=recv_sems.at[3 * a + j], device_id=(px, py, c), device_id_type=MESH)

    def arrival(a, j):
        px, py = chips[j]
        return pltpu.make_async_remote_copy(
            src_ref=src_of(a, k), dst_ref=dst_of(a, 2 * px + py), send_sem=send_sems.at[3 * a + j],
            recv_sem=recv_sems.at[3 * a + j], device_id=(px, py, c), device_id_type=MESH)

    local = [pltpu.make_async_copy(src_of(a, k), dst_of(a, k), local_sems.at[a]) for a in range(n)]
    sent = [remote(a, j) for a in range(n) for j in range(3)]

    def start():
        for cp in local + sent:
            cp.start()

    def wait():
        for a in range(n):
            for j in range(3):
                arrival(a, j).wait_recv()
        for cp in sent:
            cp.wait_send()
        for cp in local:
            cp.wait()

    return start, wait


def _chip_allgather_halved(shards, *, name):
    n = len(shards)
    half_rows = [s.shape[0] // 2 for s in shards]
    assert all(s.shape[0] % 16 == 0 for s in shards)

    def body(*refs):
        ins, outs = refs[:n], refs[n:2 * n]
        ici_send, ici_recv, d2d_send, d2d_recv, local_sems = refs[2 * n:]
        x, y, c = lax.axis_index("x"), lax.axis_index("y"), lax.axis_index("c")
        k = 2 * x + y
        chips = [(1 - x, y), (x, 1 - y), (1 - x, 1 - y)]

        def half(a, slab, core):
            return outs[a].at[slab, pl.ds(core * half_rows[a], half_rows[a])]

        def ici(a, j, slab):
            px, py = chips[j]
            return pltpu.make_async_remote_copy(
                src_ref=ins[a].at[pl.ds(c * half_rows[a], half_rows[a])], dst_ref=half(a, slab, c),
                send_sem=ici_send.at[3 * a + j], recv_sem=ici_recv.at[3 * a + j],
                device_id=(px, py, c), device_id_type=MESH)

        def d2d(a, j, core):
            px, py = chips[j]
            return pltpu.make_async_remote_copy(
                src_ref=half(a, 2 * px + py, core), dst_ref=half(a, 2 * px + py, core),
                send_sem=d2d_send.at[3 * a + j], recv_sem=d2d_recv.at[3 * a + j],
                device_id=(x, y, 1 - c), device_id_type=MESH)

        local = [pltpu.make_async_copy(ins[a], outs[a].at[k], local_sems.at[a]) for a in range(n)]
        sent = [ici(a, j, k) for a in range(n) for j in range(3)]
        for cp in local + sent:
            cp.start()
        passed = []
        for a in range(n):
            for j, (px, py) in enumerate(chips):
                ici(a, j, 2 * px + py).wait_recv()
                passed.append(d2d(a, j, c))
                passed[-1].start()
        for a in range(n):
            for j in range(3):
                d2d(a, j, 1 - c).wait_recv()
        for cp in sent + passed:
            cp.wait_send()
        for cp in local:
            cp.wait()

    hbm = pl.BlockSpec(memory_space=pltpu.HBM)
    return _comm_call(
        body, name=name, out_shape=[jax.ShapeDtypeStruct((N_CHIP, *s.shape), s.dtype) for s in shards],
        in_specs=[hbm] * n, out_specs=[hbm] * n,
        scratch_shapes=[pltpu.SemaphoreType.DMA((3 * n,))] * 4 + [pltpu.SemaphoreType.DMA((n,))],
    )(*shards)


def _sibling_exchange(arrays, *, name):
    n = len(arrays)

    def body(*refs):
        ins, outs = refs[:n], refs[n:2 * n]
        send_sems, recv_sems = refs[2 * n:]
        x, y, c = lax.axis_index("x"), lax.axis_index("y"), lax.axis_index("c")
        cps = [pltpu.make_async_remote_copy(
            src_ref=ins[a], dst_ref=outs[a], send_sem=send_sems.at[a], recv_sem=recv_sems.at[a],
            device_id=(x, y, 1 - c), device_id_type=MESH) for a in range(n)]
        for cp in cps:
            cp.start()
        for cp in cps:
            cp.wait()

    hbm = pl.BlockSpec(memory_space=pltpu.HBM)
    return _comm_call(
        body, name=name, out_shape=[jax.ShapeDtypeStruct(a.shape, a.dtype) for a in arrays],
        in_specs=[hbm] * n, out_specs=[hbm] * n,
        scratch_shapes=[pltpu.SemaphoreType.DMA((n,)), pltpu.SemaphoreType.DMA((n,))],
    )(*arrays)


def _sum_blocks(a, nblk, *, name):
    n = a.shape[1]

    def body(a_ref, o_ref):
        acc = a_ref[pl.ds(0, 8), :]
        for d in range(1, nblk):
            acc = acc + a_ref[pl.ds(8 * d, 8), :]
        o_ref[...] = acc

    return _call(body, name=name, out_shape=jax.ShapeDtypeStruct((8, n), F32))(a)


def _outer8(sct, dm, *, name, tm=256, tn=1024):
    R, N = sct.shape[0], dm.shape[1]
    tm, tn = min(tm, R), min(tn, N)

    def body(s_ref, d_ref, o_ref):
        s, dmv = s_ref[...], d_ref[...]
        acc = s[:, 0:1] * dmv[0:1, :]
        for b in range(1, 8):
            acc = acc + s[:, b:b + 1] * dmv[b:b + 1, :]
        o_ref[...] = acc

    return _call(
        body, name=name, grid=(R // tm, N // tn),
        in_specs=[pl.BlockSpec((tm, 8), lambda i, j: (i, 0)), pl.BlockSpec((8, tn), lambda i, j: (0, j))],
        out_specs=pl.BlockSpec((tm, tn), lambda i, j: (i, j)),
        out_shape=jax.ShapeDtypeStruct((R, N), F32),
        compiler_params=_cparams(("parallel", "parallel")),
    )(sct, dm)


_BC1 = 1.0 - ADAM_B1 ** ADAM_STEP
_BC2 = 1.0 - ADAM_B2 ** ADAM_STEP


def _adamw(w, g_parts, m, v, *, name, tile=128):
    R, C = w.shape
    tile = min(tile, R)
    assert R % tile == 0
    npart = len(g_parts)

    def body(*refs):
        w_ref, m_ref, v_ref = refs[npart:npart + 3]
        g_o, d_o, m_o, v_o = refs[npart + 3:]
        g = refs[0][...].astype(F32)
        for r in refs[1:npart]:
            g = g + r[...].astype(F32)
        mn = ADAM_B1 * m_ref[...] + (1.0 - ADAM_B1) * g
        vn = ADAM_B2 * v_ref[...] + (1.0 - ADAM_B2) * jnp.square(g)
        g_o[...] = g
        m_o[...] = mn
        v_o[...] = vn
        d_o[...] = -ADAM_LR * ((mn / _BC1) / (jnp.sqrt(vn / _BC2) + ADAM_EPS) + ADAM_WD * w_ref[...])

    spec = pl.BlockSpec((tile, C), lambda i: (i, 0))
    return _call(
        body, name=name, grid=(R // tile,), in_specs=[spec] * (npart + 3), out_specs=[spec] * 4,
        out_shape=[jax.ShapeDtypeStruct((R, C), F32)] * 4,
        compiler_params=_cparams(("parallel",)),
    )(*g_parts, w, m, v)


def _pack(vecs, rows8_cols):
    flat = jnp.concatenate([v.reshape(-1).astype(F32) for v in vecs])
    return jnp.pad(flat, (0, 8 * rows8_cols - flat.shape[0])).reshape(8, rows8_cols)


def _unpack(flat, shapes):
    out, o = [], 0
    for s in shapes:
        n = math.prod(s)
        out.append(flat[o:o + n].reshape(s))
        o += n
    return out


_BIG = ("w_in", "w_uq", "w_ukv", "w_out", "w_ff1", "w_ff2")
_SMALL = ("b_ada", "norm_mix_g", "b_gates", "conv_w", "conv_b", "q_lora_g", "kv_lora_g", "q_norm_g", "k_norm_g",
          "mlstm_norm_g", "norm_mlp_g")
_ORDER = ("w_ada", "b_ada", "norm_mix_g", "w_in", "b_gates", "conv_w", "conv_b", "q_lora_g", "w_uq", "kv_lora_g",
          "w_ukv", "q_norm_g", "k_norm_g", "mlstm_norm_g", "w_out", "norm_mlp_g", "w_ff1", "w_ff2")


def kernel(x, c, positions, w_ada, b_ada, norm_mix_g, w_in, b_gates, conv_w, conv_b, q_lora_g, w_uq, kv_lora_g, w_ukv, q_norm_g, k_norm_g, mlstm_norm_g, w_out, norm_mlp_g, w_ff1, w_ff2, loss_target, m_w_ada, m_b_ada, m_norm_mix_g, m_w_in, m_b_gates, m_conv_w, m_conv_b, m_q_lora_g, m_w_uq, m_kv_lora_g, m_w_ukv, m_q_norm_g, m_k_norm_g, m_mlstm_norm_g, m_w_out, m_norm_mlp_g, m_w_ff1, m_w_ff2, v_w_ada, v_b_ada, v_norm_mix_g, v_w_in, v_b_gates, v_conv_w, v_conv_b, v_q_lora_g, v_w_uq, v_kv_lora_g, v_w_ukv, v_q_norm_g, v_k_norm_g, v_mlstm_norm_g, v_w_out, v_norm_mlp_g, v_w_ff1, v_w_ff2):
    args = dict(locals())
    wts = {n: args[n] for n in _ORDER}
    mom = {n: args["m_" + n] for n in _ORDER}
    var = {n: args["v_" + n] for n in _ORDER}
    MX = _MXU_DTYPE
    xi, yi, ci = lax.axis_index("x"), lax.axis_index("y"), lax.axis_index("c")
    chip = 2 * xi + yi
    dev = 2 * chip + ci
    S = x.shape[1]
    CS = 2 * MW // N_CHIP
    GS = DM // N_CHIP

    pk = _pack([c, conv_w, mlstm_norm_g], 1024)
    allpk = _allgather8(pk, name="gather_small").reshape(N_DEV, 8 * 1024)
    c_all = allpk[:, :D]
    per_chip = allpk[0::2]
    conv_w_full = per_chip[:, D:D + CONVW * CS].reshape(N_CHIP, CONVW, CS).transpose(1, 0, 2).reshape(CONVW, 2 * MW)
    o = D + CONVW * CS
    mn_full = per_chip[:, o:o + HM * GS].reshape(N_CHIP, HM, GS).transpose(1, 0, 2).reshape(HM, DM)

    (sc,) = _rowmap(lambda cv: ([cv * _sigmoid(cv)], []), [(c_all, D, 0)], [], [(D, F32)], tile=8, name="silu_c")
    ncol = w_ada.shape[2]
    b_cols = lax.dynamic_slice(b_ada, (0, chip * ncol), (1, ncol))
    modp = _mm(sc, w_ada[0], "nn", name="ada_fwd", tm=8, tn=1024, tk=512, extras=(jnp.broadcast_to(b_cols, (8, ncol)),),
               epilogue=lambda r, b: (r + b,))
    modg = _allgather8(modp, name="gather_mod").reshape(N_CHIP, 2, 8, ncol)[:, 0]
    mod_all = modg.transpose(1, 0, 2).reshape(N_DEV, N_CHIP * ncol)
    modv = jnp.pad(lax.dynamic_slice(mod_all, (dev, 0), (1, 6 * D)).reshape(6, D), ((0, 2), (0, 0)))

    shards = [wts[n][0].astype(MX) for n in _BIG]
    gw_in, gw_uq, gw_ukv = _chip_allgather_halved(shards[:3], name="gather_weights")
    W = _prep_weights(gw_in, _cols(gw_uq), _cols(gw_ukv), None, None, None, norm_mix_g, norm_mlp_g,
                      q_lora_g, kv_lora_g, q_norm_g, k_norm_g, mn_full, conv_w_full, conv_b, b_gates)

    loss, gx, dmodv, g = _device_step(x[0], loss_target[0], positions[0], modv, W, late=shards[3:])

    small_shapes = [(6 * D,), (D,), (NG,), (CONVW, 2 * MW), (2 * MW,), (Q_LORA,), (KV_LORA,), (QK,), (QK,), (MW,), (D,), (1,)]
    pg = _pack([dmodv, g["norm_mix_g"], g["b_gates"], g["conv_w"], g["conv_b"], g["q_lora_g"], g["kv_lora_g"],
                g["q_norm_g"], g["k_norm_g"], g["mlstm_norm_g"], g["norm_mlp_g"], loss], 4096)
    allpg = _allgather8(pg, name="gather_small_grads")
    tot = _unpack(_sum_blocks(allpg, N_DEV, name="sum_small_grads").reshape(-1), small_shapes)
    dmod_all = allpg.reshape(N_DEV, 8 * 4096)[:, :6 * D]
    gsmall = dict(zip(_SMALL, [tot[0].reshape(1, 6 * D), tot[1].reshape(1, D), tot[2].reshape(1, NG),
                               lax.dynamic_slice(tot[3], (0, chip * CS), (CONVW, CS)).reshape(1, CONVW, CS),
                               tot[4].reshape(1, 2 * MW), tot[5].reshape(1, Q_LORA), tot[6].reshape(1, KV_LORA),
                               tot[7].reshape(1, QK), tot[8].reshape(1, QK),
                               lax.dynamic_slice(tot[9].reshape(HM, DM), (0, chip * GS), (HM, GS)).reshape(1, HM, GS),
                               tot[10].reshape(1, D)]))
    loss_tot = tot[11].reshape(())

    got = g["got"]
    part = []
    for nme, r in zip(_BIG, got):
        wd = r.shape[2]
        (p,) = _rowmap(lambda a0, a1, a2, a3: ([(a0.astype(F32) + a1.astype(F32)) + (a2.astype(F32) + a3.astype(F32))], []),
                       [(r, wd, 0, k) for k in range(N_CHIP)], [], [(wd, F32)], tile=256, name="sum_chips_" + nme)
        part.append(p)
    other = _sibling_exchange(part, name="exchange_cores")

    dm_cols = lax.dynamic_slice(dmod_all, (0, chip * ncol), (N_DEV, ncol))
    g_ada = _outer8(sc.T, dm_cols, name="ada_dw")

    res = {}
    for nme, p, q in zip(_BIG, part, other):
        res[nme] = _adamw(wts[nme][0], [p, q], mom[nme][0], var[nme][0], name="adamw_" + nme)
    res["w_ada"] = _adamw(w_ada[0], [g_ada], m_w_ada[0], v_w_ada[0], name="adamw_w_ada")
    sw = _pack([wts[n] for n in _SMALL], 3072)
    sg = _pack([gsmall[n] for n in _SMALL], 3072)
    sm = _pack([mom[n] for n in _SMALL], 3072)
    sv = _pack([var[n] for n in _SMALL], 3072)
    small_res = _adamw(sw, [sg], sm, sv, name="adamw_small", tile=8)
    shapes = [wts[n].shape for n in _SMALL]
    unp = [_unpack(r.reshape(-1), shapes) for r in small_res]
    for i, nme in enumerate(_SMALL):
        res[nme] = tuple(u[i] for u in unp)
    outs = [loss_tot, gx[None]]
    for kind in range(4):
        outs += [res[n][kind].reshape(wts[n].shape) for n in _ORDER]
    return tuple(outs)
```

```python
import functools
import math

import jax
import jax.numpy as jnp
from jax import lax
from jax.experimental import pallas as pl
from jax.experimental.pallas import tpu as pltpu

F32 = jnp.float32
BF16 = jnp.bfloat16
_MXU_DTYPE = jnp.bfloat16
_INTERPRET = False

D = 2048
H_MLA = 8
NOPE = 128
ROPE = 64
QK = NOPE + ROPE
HP = 256
VD = 128
Q_LORA = 512
KV_LORA = 256
HM = 4
DM = 256
MW = HM * DM
LCH = 128
CONVW = 5
NG = 16
DFF = 4 * D
EPS = 1e-6
M_INIT = -1e30
ROPE_THETA = 10000.0
IN_SIZES = (Q_LORA, KV_LORA, ROPE, MW, MW, MW, MW, NG)
D_IN = sum(IN_SIZES)
P_QM, P_KM, P_VM, P_OM, P_CQ, P_CKV, P_KPE, P_G = 0, 1024, 2048, 3072, 4096, 4608, 4864, 4992
D_INP = 5120

ADAM_LR, ADAM_B1, ADAM_B2, ADAM_EPS, ADAM_WD, ADAM_STEP = 0.001, 0.9, 0.999, 1e-08, 0.01, 10

V7X_VMEM_LIMIT = 56 * 1024 * 1024


def _cparams(sem):
    return pltpu.CompilerParams(dimension_semantics=sem, vmem_limit_bytes=V7X_VMEM_LIMIT)


def _call(body, **kw):
    if _INTERPRET:
        kw.pop("compiler_params", None)
        kw["interpret"] = pltpu.InterpretParams()
    return pl.pallas_call(body, **kw)


def _dot(a, b, form):
    dims = {"nn": ((1,), (0,)), "nt": ((1,), (1,)), "tn": ((0,), (0,))}[form]
    return lax.dot_general(a.astype(_MXU_DTYPE), b.astype(_MXU_DTYPE), (dims, ((), ())),
                           preferred_element_type=F32)


def _mm(a, b, form, *, name, out_dtypes=(F32,), epilogue=None, extras=(), tm=1024, tn=1024, tk=2048, side=()):
    if form == "nn":
        (M, K), (K2, N) = a.shape, b.shape
    elif form == "nt":
        (M, K), (N, K2) = a.shape, b.shape
    else:
        (K, M), (K2, N) = a.shape, b.shape
    assert K == K2, (a.shape, b.shape, form)
    tm, tn = min(tm, M), min(tn, N)
    tk = max(d for d in range(128, min(tk, K) + 1, 128) if K % d == 0) if K > 128 else K
    assert M % tm == 0 and N % tn == 0 and K % tk == 0, (M, N, K, tm, tn, tk)
    nk = K // tk
    ne, no = len(extras), len(out_dtypes)
    if form == "tn":
        a_spec = pl.BlockSpec((tk, tm), lambda i, j, k: (k, i))
    else:
        a_spec = pl.BlockSpec((tm, tk), lambda i, j, k: (i, k))
    if form == "nt":
        b_spec = pl.BlockSpec((tn, tk), lambda i, j, k: (j, k))
    else:
        b_spec = pl.BlockSpec((tk, tn), lambda i, j, k: (k, j))
    mn_spec = pl.BlockSpec((tm, tn), lambda i, j, k: (i, j))
    grid = (M // tm, N // tn, nk)
    ns, io, wrap = _side_exchange(side, False, grid)

    def body(a_ref, b_ref, *rest):
        ex, outs = rest[:ne], rest[ne + ns:ne + ns + no]
        scratch = rest[ne + 2 * ns + no:]
        side_start, side_wait = wrap(rest[ne:ne + ns], rest[ne + ns + no:ne + 2 * ns + no], scratch[1:])
        side_start()
        prod = _dot(a_ref[...], b_ref[...], form)

        def finish(r):
            vals = (r,) if epilogue is None else epilogue(r, *[e[...] for e in ex])
            for o, v in zip(outs, vals):
                o[...] = v.astype(o.dtype)

        if nk == 1:
            finish(prod)
        else:
            acc, k = scratch[0], pl.program_id(2)

            @pl.when(k == 0)
            def _():
                acc[...] = prod

            @pl.when(k > 0)
            def _():
                acc[...] += prod

            @pl.when(k == nk - 1)
            def _():
                finish(acc[...])
        side_wait()

    res = _call(
        body, name=name, grid=grid,
        in_specs=[a_spec, b_spec] + [mn_spec] * ne + io["specs"],
        out_specs=[mn_spec] * no + io["specs"],
        out_shape=[jax.ShapeDtypeStruct((M, N), dt) for dt in out_dtypes] + io["out_shape"],
        scratch_shapes=[pltpu.VMEM((tm, tn) if nk > 1 else (8, 128), F32)] + io["scratch"],
        compiler_params=_cparams(("arbitrary",) * 3 if ns else ("parallel", "parallel", "arbitrary")),
    )(a, b, *extras, *side)
    if ns:
        return (res[0] if no == 1 else res[:no]), list(res[no:])
    return res[0] if no == 1 else res


def _rowmap(fn, rows, bcasts, outs, accs=(), *, tile, name):
    rows = [r if len(r) == 4 else (*r, None) for r in rows]
    S = rows[0][0].shape[-2]
    tile = min(tile, S)
    assert S % tile == 0
    nr, nb, no, na = len(rows), len(bcasts), len(outs), len(accs)

    def body(*refs):
        vals = [r[...] for r in refs[:nr + nb]]
        o_refs, a_refs = refs[nr + nb:nr + nb + no], refs[nr + nb + no:]
        o_vals, a_vals = fn(*vals)
        for r, v in zip(o_refs, o_vals):
            r[...] = v.astype(r.dtype)
        if na:
            @pl.when(pl.program_id(0) == 0)
            def _():
                for r in a_refs:
                    r[...] = jnp.zeros(r.shape, r.dtype)
            for r, v in zip(a_refs, a_vals):
                r[...] += v

    in_specs = []
    for (arr, w, cb, lead) in rows:
        if lead is None:
            in_specs.append(pl.BlockSpec((tile, w), lambda i, cb=cb: (i, cb)))
        else:
            in_specs.append(pl.BlockSpec((None, tile, w), lambda i, cb=cb, lead=lead: (lead, i, cb)))
    in_specs += [pl.BlockSpec(b.shape, lambda i: (0, 0)) for b in bcasts]
    out_specs = [pl.BlockSpec((tile, w), lambda i: (i, 0)) for (w, _) in outs]
    out_specs += [pl.BlockSpec(s, lambda i: (0, 0)) for s in accs]
    out_shape = [jax.ShapeDtypeStruct((S, w), dt) for (w, dt) in outs]
    out_shape += [jax.ShapeDtypeStruct(s, F32) for s in accs]
    return _call(
        body, name=name, grid=(S // tile,), in_specs=in_specs, out_specs=out_specs, out_shape=out_shape,
        compiler_params=_cparams(("arbitrary",)),
    )(*[r[0] for r in rows], *bcasts)


def _colsum(v):
    return jnp.sum(v, axis=0, keepdims=True)


def _rms(x, n):
    r = lax.rsqrt(jnp.sum(x * x, axis=-1, keepdims=True) * (1.0 / n) + EPS)
    return x * r, r


def _rms_bwd(dxhat, xhat, r, n):
    return r * (dxhat - xhat * (jnp.sum(dxhat * xhat, axis=-1, keepdims=True) * (1.0 / n)))


def _rope_fwd(r, cosp, s1, s2):
    return r * cosp + pltpu.roll(r, 32, 1) * s1 + pltpu.roll(r, 96, 1) * s2


def _rope_bwd(d, cosp, s1, s2):
    return d * cosp + pltpu.roll(d * s1, 96, 1) + pltpu.roll(d * s2, 32, 1)


def _sigmoid(x):
    return 1.0 / (1.0 + jnp.exp(-x))


def _halo_specs(tile, halo, width, cb, S, lead=None):
    nh = tile // halo
    last = S // halo - 1
    if lead is None:
        return [
            pl.BlockSpec((tile, width), lambda i: (i, cb)),
            pl.BlockSpec((halo, width), lambda i: (jnp.maximum(i * nh - 1, 0), cb)),
            pl.BlockSpec((halo, width), lambda i: (jnp.minimum((i + 1) * nh, last), cb)),
        ]
    return [
        pl.BlockSpec((None, tile, width), lambda i: (lead, i, cb)),
        pl.BlockSpec((None, halo, width), lambda i: (lead, jnp.maximum(i * nh - 1, 0), cb)),
        pl.BlockSpec((None, halo, width), lambda i: (lead, jnp.minimum((i + 1) * nh, last), cb)),
    ]


def _conv_fwd(proj, conv_w8, conv_b, *, tile=256):
    S = proj.shape[0]
    T = min(tile, S)
    n = S // T
    W = 2 * MW

    def body(x_ref, xp_ref, xn_ref, w_ref, b_ref, q_ref, k_ref, ext):
        i = pl.program_id(0)
        ext[pl.ds(0, 8), :] = xp_ref[...] * (i > 0).astype(F32)
        ext[pl.ds(8, T), :] = x_ref[...]
        ext[pl.ds(8 + T, 8), :] = xn_ref[...] * (i < n - 1).astype(F32)
        w = w_ref[...]
        y = b_ref[...] + w[0:1, :] * ext[pl.ds(6, T), :]
        for o in range(1, CONVW):
            y = y + w[o:o + 1, :] * ext[pl.ds(6 + o, T), :]
        y = y * _sigmoid(y)
        q_ref[...] = y[:, :MW].astype(q_ref.dtype)
        k_ref[...] = (y[:, MW:] * (DM ** -0.5)).astype(k_ref.dtype)

    return _call(
        body, name="conv_fwd", grid=(n,),
        in_specs=_halo_specs(T, 8, W, 0, S) + [pl.BlockSpec((8, W), lambda i: (0, 0)),
                                                 pl.BlockSpec((1, W), lambda i: (0, 0))],
        out_specs=[pl.BlockSpec((T, MW), lambda i: (i, 0))] * 2,
        out_shape=[jax.ShapeDtypeStruct((S, MW), _MXU_DTYPE)] * 2,
        scratch_shapes=[pltpu.VMEM((T + 16, W), F32)],
        compiler_params=_cparams(("arbitrary",)),
    )(proj, proj, proj, conv_w8, conv_b)


def _conv_bwd(proj, dqd, dkd, conv_w8, conv_b, *, tile=256):
    S = proj.shape[0]
    T = min(tile, S)
    n = S // T
    W = 2 * MW

    def body(x_ref, xp_ref, xn_ref, *rest):
        g = rest[:12]
        w_ref, b_ref, dx_ref, dw_ref, db_ref, ext, edp = rest[12:]
        i = pl.program_id(0)
        mp = (i > 0).astype(F32)
        mn = (i < n - 1).astype(F32)
        ext[pl.ds(0, 16), :] = xp_ref[...] * mp
        ext[pl.ds(16, T), :] = x_ref[...]
        ext[pl.ds(16 + T, 16), :] = xn_ref[...] * mn
        w = w_ref[...]
        pre = b_ref[...] + w[0:1, :] * ext[pl.ds(6, T + 16), :]
        for o in range(1, CONVW):
            pre = pre + w[o:o + 1, :] * ext[pl.ds(6 + o, T + 16), :]
        sg = _sigmoid(pre)
        dsilu = sg * (1.0 + pre * (1.0 - sg))
        for half, (a0, a1) in enumerate(((g[0:3], g[3:6]), (g[6:9], g[9:12]))):
            sc = 1.0 if half == 0 else DM ** -0.5
            cols = pl.ds(half * MW, MW)
            edp[pl.ds(0, 8), cols] = (a0[1][...] + a1[1][...]) * (mp * sc)
            edp[pl.ds(8, T), cols] = (a0[0][...] + a1[0][...]) * sc
            edp[pl.ds(8 + T, 8), cols] = (a0[2][...] + a1[2][...]) * (mn * sc)
        edp[...] = edp[...] * dsilu
        @pl.when(i == 0)
        def _():
            dw_ref[...] = jnp.zeros(dw_ref.shape, F32)
            db_ref[...] = jnp.zeros(db_ref.shape, F32)

        x_main = ext[pl.ds(16, T), :]
        dx = None
        for o in range(CONVW):
            view = edp[pl.ds(10 - o, T), :]
            dx = w[o:o + 1, :] * view if dx is None else dx + w[o:o + 1, :] * view
            dw_ref[pl.ds(o, 1), :] += _colsum(x_main * view)
        dx_ref[...] = dx.astype(dx_ref.dtype)
        db_ref[...] += _colsum(edp[pl.ds(8, T), :])

    gspecs = _halo_specs(T, 8, MW, 0, S) * 4
    return _call(
        body, name="conv_bwd", grid=(n,),
        in_specs=_halo_specs(T, 16, W, 0, S) + gspecs + [pl.BlockSpec((8, W), lambda i: (0, 0)),
                                                          pl.BlockSpec((1, W), lambda i: (0, 0))],
        out_specs=[pl.BlockSpec((T, W), lambda i: (i, 0)), pl.BlockSpec((8, W), lambda i: (0, 0)),
                   pl.BlockSpec((1, W), lambda i: (0, 0))],
        out_shape=[jax.ShapeDtypeStruct((S, W), _MXU_DTYPE), jax.ShapeDtypeStruct((8, W), F32),
                   jax.ShapeDtypeStruct((1, W), F32)],
        scratch_shapes=[pltpu.VMEM((T + 32, W), F32), pltpu.VMEM((T + 16, W), F32)],
        compiler_params=_cparams(("arbitrary",)),
    )(proj, proj, proj, *([dqd[0]] * 3), *([dqd[1]] * 3), *([dkd[0]] * 3), *([dkd[1]] * 3), conv_w8, conv_b)


_ATT_SCALE = QK ** -0.5
_LOG2E = math.log2(math.e)
_Q_PRESCALE = _ATT_SCALE * _LOG2E


def _side_exchange(side, gather, grid, cols=()):
    ns = len(side)
    io = _exchange_io(side, gather, cols) if ns else dict(specs=[], out_shape=[], scratch=[])

    def wrap(refs_in, refs_out, sems):
        if not ns:
            return (lambda: None), (lambda: None)
        start, wait = _exchange_ops(refs_in, refs_out, *sems, gather=gather, cols=cols)
        ids = [pl.program_id(a) for a in range(len(grid))]
        first = functools.reduce(jnp.logical_and, [i == 0 for i in ids])
        last = functools.reduce(jnp.logical_and, [i == g - 1 for i, g in zip(ids, grid)])
        return (lambda: pl.when(first)(start)), (lambda: pl.when(last)(wait))

    return ns, io, wrap


def _attn_fwd(q, k, v, *, side=(), side_cols=(), tq=1024, tk=8192, split=4, unroll=1):
    S = q.shape[0]
    tq, tk = min(tq, S), min(tk, S)
    nkv = S // tk
    hq = tq // split
    grid = (H_MLA, S // tq)
    ns, io, wrap = _side_exchange(side, True, grid, side_cols)

    def body(q_ref, k_ref, v_ref, *rest):
        o_ref, qa_ref = rest[ns:ns + 2]
        m_s, acc_s = rest[2 * ns + 2:2 * ns + 4]
        side_start, side_wait = wrap(rest[:ns], rest[ns + 2:2 * ns + 2], rest[2 * ns + 4:])
        side_start()
        m_s[...] = jnp.full(m_s.shape, -1e30, F32)
        acc_s[...] = jnp.zeros(acc_s.shape, F32)

        def step(j, carry):
            rows = pl.ds(pl.multiple_of(j * tk, tk), tk)
            kj, vj = k_ref[rows, :], v_ref[rows, :]
            for a in range(split):
                r = pl.ds(a * hq, hq)
                s = _dot(q_ref[r, :], kj, "nt")
                m_old = m_s[r, :]
                m_new = jnp.maximum(m_old, jnp.max(s, axis=1, keepdims=True))
                p = jnp.exp2(s - m_new)
                acc_s[r, :] = jnp.exp2(m_old - m_new) * acc_s[r, :] + _dot(p, vj, "nn")
                m_s[r, :] = m_new
            return carry

        lax.fori_loop(0, nkv, step, 0, unroll=unroll if nkv % unroll == 0 else 1)
        l = acc_s[:, VD:VD + 1]
        o_ref[...] = (acc_s[:, :VD] / l).astype(o_ref.dtype)
        lse = m_s[...] + jnp.log2(l)
        hi = lse.astype(_MXU_DTYPE).astype(F32)
        lane = lax.broadcasted_iota(jnp.int32, (tq, HP), 1)
        qa = jnp.where(lane == QK, -hi, jnp.where(lane == QK + 1, hi - lse, q_ref[...].astype(F32)))
        qa_ref[...] = qa.astype(qa_ref.dtype)
        side_wait()

    res = _call(
        body, name="attn_fwd", grid=grid,
        in_specs=[pl.BlockSpec((tq, HP), lambda h, i: (i, h)),
                  pl.BlockSpec((S, HP), lambda h, i: (0, h)),
                  pl.BlockSpec((S, HP), lambda h, i: (0, h))] + io["specs"],
        out_specs=[pl.BlockSpec((tq, VD), lambda h, i: (i, h)),
                   pl.BlockSpec((tq, HP), lambda h, i: (i, h))] + io["specs"],
        out_shape=[jax.ShapeDtypeStruct((S, H_MLA * VD), _MXU_DTYPE),
                   jax.ShapeDtypeStruct((S, H_MLA * HP), _MXU_DTYPE)] + io["out_shape"],
        scratch_shapes=[pltpu.VMEM((tq, 1), F32), pltpu.VMEM((tq, HP), F32)] + io["scratch"],
        compiler_params=_cparams(("arbitrary", "arbitrary")),
    )(q, k, v, *side)
    return res[0], res[1], list(res[2:])


def _attn_bwd(qa, k, va, doa, *, side=(), side_cols=(), tq=4096, tk=512, split=4, unroll=1):
    S = qa.shape[0]
    tq, tk = min(tq, S), min(tk, S)
    nq, nkb = S // tq, S // tk
    hq = tq // split
    grid = (H_MLA, nkb)
    ns, io, wrap = _side_exchange(side, False, grid, side_cols)

    def body(q_ref, k_ref, v_ref, do_ref, *rest):
        dq_ref, dk_ref, dv_ref = rest[ns:ns + 3]
        side_start, side_wait = wrap(rest[:ns], rest[ns + 3:2 * ns + 3], rest[2 * ns + 3:])
        side_start()
        j = pl.program_id(1)

        @pl.when(j == 0)
        def _():
            dq_ref[...] = jnp.zeros(dq_ref.shape, F32)

        dk_ref[...] = jnp.zeros(dk_ref.shape, F32)
        dv_ref[...] = jnp.zeros(dv_ref.shape, F32)
        kb, vb = k_ref[...], v_ref[...]

        def step(i, carry):
            for a in range(split):
                r = pl.ds(pl.multiple_of(i * tq + a * hq, hq), hq)
                qg, dog = q_ref[r, :], do_ref[r, :]
                p = jnp.exp2(_dot(qg, kb, "nt"))
                ds = (p * _dot(dog, vb, "nt")).astype(_MXU_DTYPE)
                dq_ref[r, :] += _dot(ds, kb, "nn")
                dv_ref[...] += _dot(p, dog, "tn")
                dk_ref[...] += _dot(ds, qg, "tn")
            return carry

        lax.fori_loop(0, nq, step, 0, unroll=unroll if nq % unroll == 0 else 1)
        dk_ref[...] = dk_ref[...] * (1.0 / _LOG2E)

        @pl.when(j == nkb - 1)
        def _():
            dq_ref[...] = dq_ref[...] * _ATT_SCALE

        side_wait()

    blk = pl.BlockSpec((tk, HP), lambda h, j: (j, h))
    whole = pl.BlockSpec((S, HP), lambda h, j: (0, h))
    res = _call(
        body, name="attn_bwd", grid=grid,
        in_specs=[whole, blk, blk, whole] + io["specs"],
        out_specs=[whole, blk, blk] + io["specs"],
        out_shape=[jax.ShapeDtypeStruct((S, H_MLA * HP), F32)] * 3 + io["out_shape"],
        scratch_shapes=io["scratch"],
        compiler_params=_cparams(("arbitrary", "arbitrary")),
    )(qa, k, va, doa, *side)
    return res[0], res[1], res[2], list(res[3:])


def _mlstm_chunk_terms(g, q, k, v, gates, gates_t, bg_row, C, n_row, m):
    L = LCH
    d = g // HM
    h = g % HM
    i_idx = d * 8 + h
    f_idx = d * 8 + 4 + h
    rr = lax.broadcasted_iota(jnp.int32, (L, L), 0)
    cc = lax.broadcasted_iota(jnp.int32, (L, L), 1)
    order = (rr - cc) * (1 - 2 * d)
    tri = order >= 0
    eye = rr == cc
    lane = lax.broadcasted_iota(jnp.int32, gates.shape, 1)
    sub = lax.broadcasted_iota(jnp.int32, gates_t.shape, 0)
    lane_b = lax.broadcasted_iota(jnp.int32, bg_row.shape, 1)
    pick_c = lambda idx: jnp.sum(jnp.where(lane == idx, gates, 0.0), axis=1, keepdims=True)
    pick_r = lambda idx: jnp.sum(jnp.where(sub == idx, gates_t, 0.0), axis=0, keepdims=True)
    pick_b = lambda idx: jnp.sum(jnp.where(lane_b == idx, bg_row, 0.0), axis=1, keepdims=True)
    i_col, i_row = pick_c(i_idx) + pick_b(i_idx), pick_r(i_idx) + pick_b(i_idx)
    f_col, f_row = pick_c(f_idx) + pick_b(f_idx), pick_r(f_idx) + pick_b(f_idx)
    logsig = lambda x: jnp.minimum(x, 0.0) - jnp.log(1.0 + jnp.exp(-jnp.abs(x)))
    lf_col, lf_row = logsig(f_col), logsig(f_row)
    b_col = jnp.sum(jnp.where(tri, lf_row, 0.0), axis=1, keepdims=True)
    tri_t = order <= 0
    b_row = jnp.sum(jnp.where(tri_t, lf_col, 0.0), axis=0, keepdims=True)
    bL = jnp.sum(lf_row, axis=1, keepdims=True)
    log_inter = b_col + m
    logD = jnp.where(tri, b_col - b_row + i_row, -jnp.inf)
    m_t = jnp.maximum(log_inter, jnp.max(logD, axis=1, keepdims=True))
    Dm = jnp.exp(logD - m_t)
    w_inter = jnp.exp(log_inter - m_t)
    A = _dot(q, k, "nt")
    Sc = A * Dm
    numI = _dot(q, C, "nt")
    qf = q.astype(F32)
    kf = k.astype(F32)
    denI = jnp.sum(qf * n_row, axis=1, keepdims=True)
    num = _dot(Sc, v, "nn") + w_inter * numI
    den = jnp.sum(Sc, axis=1, keepdims=True) + w_inter * denI
    floor = jnp.exp(-m_t)
    Nst = jnp.maximum(jnp.abs(den), floor)
    log_w = bL - b_col + i_col
    m_new = jnp.maximum(bL + m, jnp.max(log_w, axis=0, keepdims=True))
    decay = jnp.exp(bL + m - m_new)
    w_col = jnp.exp(log_w - m_new)
    return dict(tri=tri, eye=eye, f_row=f_row, Dm=Dm, w_inter=w_inter, A=A, Sc=Sc, numI=numI, denI=denI,
                num=num, den=den, floor=floor, Nst=Nst, m_new=m_new, decay=decay, w_col=w_col, qf=qf, kf=kf)


def _mlstm_specs(nc, d, step_of):
    chunk = lambda j: step_of(j) if d == 0 else nc - 1 - step_of(j)
    return chunk, [
        pl.BlockSpec((LCH, DM), lambda h, j: (chunk(j), h)),
        pl.BlockSpec((LCH, DM), lambda h, j: (chunk(j), h)),
        pl.BlockSpec((LCH, DM), lambda h, j: (chunk(j), P_VM // DM + h)),
        pl.BlockSpec((LCH, 128), lambda h, j: (chunk(j), P_G // 128)),
        pl.BlockSpec((NG, LCH), lambda h, j: (0, chunk(j))),
    ]


def _mlstm_fwd(qc, kc, proj, gates_t, bg_row):
    S = qc.shape[0]
    nc = S // LCH
    in_specs, out_specs = [], []
    for d in (0, 1):
        chunk, specs = _mlstm_specs(nc, d, lambda j: j)
        in_specs += specs
        out_specs += [pl.BlockSpec((LCH, DM), lambda h, j, chunk=chunk: (chunk(j), h)),
                      pl.BlockSpec((None, None, DM, DM), lambda h, j, chunk=chunk: (h, chunk(j), 0, 0)),
                      pl.BlockSpec((None, None, 8, DM), lambda h, j, chunk=chunk: (h, chunk(j), 0, 0))]
    in_specs.append(pl.BlockSpec((1, 128), lambda h, j: (0, 0)))

    def body(*refs):
        bg_ref, outs, (C_s, n_s, m_s) = refs[10], refs[11:17], refs[17:]

        @pl.when(pl.program_id(1) == 0)
        def _():
            C_s[...] = jnp.zeros(C_s.shape, F32)
            n_s[...] = jnp.zeros(n_s.shape, F32)
            m_s[...] = jnp.full(m_s.shape, M_INIT, F32)

        for d in (0, 1):
            q_ref, k_ref, v_ref, g_ref, gt_ref = refs[5 * d:5 * d + 5]
            h_ref, cst_ref, nm_ref = outs[3 * d:3 * d + 3]
            g = d * HM + pl.program_id(0)
            C, n_row, m = C_s[d], n_s[d, 0:1, :], m_s[d, 0:1, 0:1]
            cst_ref[...] = C
            nm_ref[0:1, :] = n_row
            nm_ref[1:2, :] = jnp.broadcast_to(m, (1, DM))
            nm_ref[2:8, :] = jnp.zeros((6, DM), F32)
            q, k, v = q_ref[...], k_ref[...], v_ref[...]
            t = _mlstm_chunk_terms(g, q, k, v, g_ref[...], gt_ref[...], bg_ref[...], C, n_row, m)
            h_ref[...] = t["num"] / t["Nst"]
            wv = t["w_col"] * v
            C_s[d] = t["decay"] * C + _dot(wv, k, "tn")
            n_s[d, 0:1, :] = t["decay"] * n_row + _colsum(t["w_col"] * t["kf"])
            m_s[d] = jnp.broadcast_to(t["m_new"], (8, 128))

    res = _call(
        body, name="mlstm_fwd", grid=(HM, nc), in_specs=in_specs, out_specs=out_specs,
        out_shape=[jax.ShapeDtypeStruct((S, MW), F32), jax.ShapeDtypeStruct((HM, nc, DM, DM), F32),
                   jax.ShapeDtypeStruct((HM, nc, 8, DM), F32)] * 2,
        scratch_shapes=[pltpu.VMEM((2, DM, DM), F32), pltpu.VMEM((2, 8, DM), F32), pltpu.VMEM((2, 8, 128), F32)],
        compiler_params=_cparams(("parallel", "arbitrary")),
    )(*([qc, kc, proj, proj, gates_t] * 2), bg_row)
    return (res[0], res[3]), (res[1], res[4]), (res[2], res[5])


def _mlstm_bwd(qc, kc, proj, gates_t, bg_row, dh, cst, nm):
    S = qc.shape[0]
    nc = S // LCH
    in_specs, out_specs = [], []
    for d in (0, 1):
        chunk, specs = _mlstm_specs(nc, d, lambda j: nc - 1 - j)
        in_specs += specs + [pl.BlockSpec((LCH, DM), lambda h, j, chunk=chunk: (chunk(j), h)),
                             pl.BlockSpec((None, None, DM, DM), lambda h, j, chunk=chunk: (h, chunk(j), 0, 0)),
                             pl.BlockSpec((None, None, 8, DM), lambda h, j, chunk=chunk: (h, chunk(j), 0, 0))]
        out_specs += [pl.BlockSpec((LCH, DM), lambda h, j, chunk=chunk: (chunk(j), h))] * 3
        out_specs += [pl.BlockSpec((None, None, 8, LCH), lambda h, j, chunk=chunk: (h, chunk(j), 0, 0))]
    in_specs.append(pl.BlockSpec((1, 128), lambda h, j: (0, 0)))

    def body(*refs):
        bg_ref, outs, (dC_s, dn_s) = refs[16], refs[17:25], refs[25:]

        @pl.when(pl.program_id(1) == 0)
        def _():
            dC_s[...] = jnp.zeros(dC_s.shape, F32)
            dn_s[...] = jnp.zeros(dn_s.shape, F32)

        for d in (0, 1):
            _mlstm_bwd_chain(d, refs[8 * d:8 * d + 8], bg_ref, outs[4 * d:4 * d + 4], dC_s, dn_s)

    res = _call(
        body, name="mlstm_bwd", grid=(HM, nc), in_specs=in_specs, out_specs=out_specs,
        out_shape=([jax.ShapeDtypeStruct((S, MW), F32)] * 3 + [jax.ShapeDtypeStruct((HM, nc, 8, LCH), F32)]) * 2,
        scratch_shapes=[pltpu.VMEM((2, DM, DM), F32), pltpu.VMEM((2, 8, DM), F32)],
        compiler_params=_cparams(("parallel", "arbitrary")),
    )(*[a for d in (0, 1) for a in (qc, kc, proj, proj, gates_t, dh, cst[d], nm[d])], bg_row)
    return (res[0], res[4]), (res[1], res[5]), (res[2], res[6]), (res[3], res[7])


def _mlstm_bwd_chain(d, ins, bg_ref, outs, dC_s, dn_s):
        q_ref, k_ref, v_ref, g_ref, gt_ref, dh_ref, cst_ref, nm_ref = ins
        dq_ref, dk_ref, dv_ref, dg_ref = outs
        g = d * HM + pl.program_id(0)
        C, n_row, m = cst_ref[...], nm_ref[0:1, :], nm_ref[1:2, 0:1]
        q, k, v = q_ref[...], k_ref[...], v_ref[...]
        t = _mlstm_chunk_terms(g, q, k, v, g_ref[...], gt_ref[...], bg_ref[...], C, n_row, m)
        tri, eye, qf, kf = t["tri"], t["eye"], t["qf"], t["kf"]
        w_inter, w_col, decay, Nst = t["w_inter"], t["w_col"], t["decay"], t["Nst"]
        dC, dn = dC_s[d], dn_s[d, 0:1, :]
        dhv = dh_ref[...]
        hval = t["num"] / Nst
        dnum = dhv / Nst
        dNst = -jnp.sum(dhv * hval, axis=1, keepdims=True) / Nst
        dden = jnp.where(jnp.abs(t["den"]) > t["floor"], jnp.sign(t["den"]) * dNst, 0.0)
        dSc = _dot(dnum, v, "nt") + dden
        dA = dSc * t["Dm"]
        G = dSc * t["Sc"]
        KdC = _dot(k, dC, "nt")
        dq = _dot(dA, k, "nn") + w_inter * _dot(dnum, C, "nn") + (w_inter * dden) * n_row
        dk = _dot(dA, q, "tn") + w_col * _dot(v, dC, "nn") + w_col * dn
        dv = _dot(t["Sc"], dnum, "tn") + w_col * KdC
        dq_ref[...] = dq
        dk_ref[...] = dk
        dv_ref[...] = dv
        dlog_inter = w_inter * (jnp.sum(dnum * t["numI"], axis=1, keepdims=True) + dden * t["denI"])
        rowG = jnp.sum(G, axis=1, keepdims=True)
        colG = jnp.sum(G, axis=0, keepdims=True)
        u_col = w_col * (jnp.sum(v * KdC, axis=1, keepdims=True) + jnp.sum(kf * dn, axis=1, keepdims=True))
        colG_c = jnp.sum(jnp.where(eye, colG, 0.0), axis=1, keepdims=True)
        u_row = jnp.sum(jnp.where(eye, u_col, 0.0), axis=0, keepdims=True)
        db_col = rowG + dlog_inter - u_col - colG_c
        dbL = jnp.sum(u_col, axis=0, keepdims=True) + decay * (
            jnp.sum(jnp.sum(dC * C, axis=1, keepdims=True), axis=0, keepdims=True)
            + jnp.sum(dn * n_row, axis=1, keepdims=True))
        dlf_row = jnp.sum(jnp.where(tri, db_col, 0.0), axis=0, keepdims=True) + dbL
        di_row = colG + u_row
        df_row = dlf_row * (1.0 - _sigmoid(t["f_row"]))
        dg_ref[...] = jnp.zeros(dg_ref.shape, F32)
        dg_ref[0:1, :] = di_row
        dg_ref[1:2, :] = df_row
        dC_s[d] = decay * dC + _dot(w_inter * dnum, q, "tn")
        dn_s[d, 0:1, :] = decay * dn + _colsum((w_inter * dden) * qf)


def _pad_w_in(w):
    cq, ckv, kpe, qm, km, vm, om, gt = _split_in(w)
    z = lambda n: jnp.zeros((w.shape[0], n), w.dtype)
    return jnp.concatenate([qm, km, vm, om, cq, ckv, kpe, z(HP - QK), gt, z(128 - NG)], axis=1)


def _split_in(w):
    out, o = [], 0
    for n in IN_SIZES:
        out.append(w[:, o:o + n])
        o += n
    return out


def _unpad_w_in(g):
    return jnp.concatenate([g[:, P_CQ:P_CQ + Q_LORA], g[:, P_CKV:P_CKV + KV_LORA], g[:, P_KPE:P_KPE + ROPE],
                            g[:, 0:4 * MW], g[:, P_G:P_G + NG]], axis=1)


_IN_SHARD = D_IN // 4
_IN_SEGMENTS = ((0, 512, P_CQ), (512, 768, P_CKV), (768, 832, P_KPE), (832, 4928, P_QM), (4928, 4944, P_G))


def _pad_w_in_slabs(slabs):
    def orig(a, b):
        out = []
        for k in range(4):
            lo, hi = max(a, k * _IN_SHARD), min(b, (k + 1) * _IN_SHARD)
            if lo < hi:
                out.append(slabs[k][:, lo - k * _IN_SHARD:hi - k * _IN_SHARD])
        return out

    z = lambda n: jnp.zeros((slabs.shape[1], n), slabs.dtype)
    return jnp.concatenate(orig(832, 4928) + orig(0, 512) + orig(512, 768) + orig(768, 832) + [z(HP - QK)]
                           + orig(4928, 4944) + [z(128 - NG)], axis=1)


def _unpad_w_in_slabs(g):
    slabs = []
    for k in range(4):
        pieces = []
        for a, b, p in _IN_SEGMENTS:
            lo, hi = max(a, k * _IN_SHARD), min(b, (k + 1) * _IN_SHARD)
            if lo < hi:
                pieces.append(g[:, p + lo - a:p + hi - a])
        slabs.append(jnp.concatenate(pieces, axis=1))
    return jnp.stack(slabs)


def _pad_w_uq(w):
    return jnp.pad(w.reshape(Q_LORA, H_MLA, QK), ((0, 0), (0, 0), (0, HP - QK))).reshape(Q_LORA, H_MLA * HP)


def _unpad_w_uq(g):
    return g.reshape(Q_LORA, H_MLA, HP)[:, :, :QK].reshape(Q_LORA, H_MLA * QK)


def _perm_w_ukv(w):
    return w.reshape(KV_LORA, H_MLA, 2, NOPE).transpose(0, 2, 1, 3).reshape(KV_LORA, 2 * H_MLA * NOPE)


def _unperm_w_ukv(g):
    return g.reshape(KV_LORA, 2, H_MLA, NOPE).transpose(0, 2, 1, 3).reshape(KV_LORA, 2 * H_MLA * NOPE)


def _rope_tables(positions):
    half = ROPE // 2
    freqs = ROPE_THETA ** (-jnp.arange(half, dtype=F32) / half)
    ang = positions.astype(F32)[:, None] * freqs
    cos, sin = jnp.cos(ang), jnp.sin(ang)
    z32, z64 = jnp.zeros_like(cos), jnp.zeros((cos.shape[0], 64), F32)
    return (jnp.concatenate([cos, cos, z64], axis=1), jnp.concatenate([z32, sin, z64], axis=1),
            jnp.concatenate([-sin, z32, z64], axis=1))


def _device_step(x, tgt, positions, modv, W, late=None):
    S = x.shape[0]
    MX = _MXU_DTYPE
    cosp, rs1, rs2 = _rope_tables(positions)
    tabs = [(cosp, 128, 0), (rs1, 128, 0), (rs2, 128, 0)]
    cat1 = lambda vs: jnp.concatenate(vs, axis=1)
    hsl = lambda hh, w: slice(hh * w, (hh + 1) * w)

    def ln1(xv, g, mv):
        xhat, _ = _rms(xv, D)
        return [xhat * g * (1.0 + mv[1:2]) + mv[0:1]], []

    (h,) = _rowmap(ln1, [(x, D, 0)], [W["g_mix"], modv], [(D, MX)], tile=512, name="ln1")
    proj = _mm(h, W["w_in"], "nn", name="proj")

    def lora(cq, ckv, gq, gkv):
        return [_rms(cq, Q_LORA)[0] * gq, _rms(ckv, KV_LORA)[0] * gkv], []

    cqn, ckvn = _rowmap(lora, [(proj, Q_LORA, P_CQ // Q_LORA), (proj, KV_LORA, P_CKV // KV_LORA)],
                        [W["g_qlora"], W["g_kvlora"]], [(Q_LORA, MX), (KV_LORA, MX)], tile=512, name="lora_norm")
    q_raw = _mm(cqn, W["w_uq"], "nn", name="q_up")
    kv_raw = _mm(ckvn, W["w_ukv"], "nn", name="kv_up")

    def mla_q(qr, cp, a1, a2, gq):
        outs = []
        for hh in range(H_MLA):
            y = _rms(qr[:, hsl(hh, HP)], QK)[0] * gq
            outs += [y[:, :NOPE], _rope_fwd(y[:, NOPE:], cp, a1, a2)]
        return [cat1(outs) * _Q_PRESCALE], []

    (qh,) = _rowmap(mla_q, [(q_raw, H_MLA * HP, 0)] + tabs, [W["gq"]], [(H_MLA * HP, MX)], tile=512, name="mla_q")

    def mla_k(kvr, kpe, cp, a1, a2, gk):
        lane = lax.broadcasted_iota(jnp.int32, (kvr.shape[0], 128), 1)
        outs, vas = [], []
        for hh in range(H_MLA):
            y = _rms(cat1([kvr[:, hsl(hh, NOPE)], kpe]), QK)[0] * gk
            outs += [y[:, :NOPE], _rope_fwd(y[:, NOPE:], cp, a1, a2) + ((lane == ROPE) | (lane == ROPE + 1)).astype(F32)]
            vas += [kvr[:, H_MLA * NOPE + hh * VD:H_MLA * NOPE + (hh + 1) * VD], (lane < 2).astype(F32)]
        return [cat1(outs), cat1(vas)], []

    kh, va = _rowmap(mla_k, [(kv_raw, 2 * H_MLA * NOPE, 0), (proj, 128, P_KPE // 128)] + tabs, [W["gk"]],
                     [(H_MLA * HP, MX), (H_MLA * HP, MX)], tile=512, name="mla_k")
    attn_o, qa, gathered = _attn_fwd(qh, kh, va, side=late or (), side_cols=(1,))
    if late:
        W = dict(W, w_out=gathered[0].reshape(D, D), w_ff1=gathered[1], w_ff2=gathered[2].reshape(DFF, D))

    qc, kc = _conv_fwd(proj, W["conv_w8"], W["conv_b"])
    gates_t = proj[:, P_G:P_G + NG].T
    (h_f, h_b), cst, nm = _mlstm_fwd(qc, kc, proj, gates_t, W["bg_row"])
    hrows = [(h_f, MW, 0), (h_b, MW, 0), (proj, MW, P_OM // MW)]

    def ml_out(ao, hf, hb, om, gmn):
        outs = [ao.astype(F32)]
        hs = hf + hb
        for hh in range(HM):
            sl = hsl(hh, DM)
            outs.append(_sigmoid(om[:, sl]) * _rms(hs[:, sl], DM)[0] * gmn[:, sl])
        return [cat1(outs)], []

    (cat,) = _rowmap(ml_out, [(attn_o, MW, 0)] + hrows, [W["g_mn"]], [(D, MX)], tile=512, name="ml_out")
    mixed = _mm(cat, W["w_out"], "nn", name="out_proj")

    def res_ln2(xv, mx, g, mv):
        x1 = xv + mv[2:3] * mx
        return [x1, _rms(x1, D)[0] * g * (1.0 + mv[4:5]) + mv[3:4]], []

    x1, h2 = _rowmap(res_ln2, [(x, D, 0), (mixed, D, 0)], [W["g_mlp"], modv], [(D, F32), (D, MX)],
                     tile=512, name="res_ln2")
    a, u = _mm(h2, W["w_ff1"], "nn", name="ff1", out_dtypes=(MX, MX),
               epilogue=lambda r: (jnp.square(jnp.maximum(r, 0.0)), r))
    y = _mm(a, W["w_ff2"], "nn", name="ff2")

    def final(x1v, yv, tv, mv):
        err = x1v + mv[5:6] * yv - tv
        dout = err * (1.0 / D)
        loss = jnp.sum(jnp.sum(0.5 * err * dout, axis=1, keepdims=True), axis=0, keepdims=True)
        return [dout, mv[5:6] * dout], [loss, _colsum(dout * yv)]

    dout, dy, loss, dgate2 = _rowmap(final, [(x1, D, 0), (y, D, 0), (tgt, D, 0)], [modv], [(D, F32), (D, MX)],
                                     [(1, 1), (1, D)], tile=256, name="loss_head")

    du = _mm(dy, W["w_ff2"], "nt", name="ff2_dx", out_dtypes=(MX,), extras=(u,),
             epilogue=lambda r, uv: (r * (2.0 * jnp.maximum(uv.astype(F32), 0.0)),))
    gdt = (MX,)
    g_ff2 = _mm(a, dy, "tn", name="ff2_dw", out_dtypes=gdt)
    dh2 = _mm(du, W["w_ff1"], "nt", name="ff1_dx")
    g_ff1 = _mm(h2, du, "tn", name="ff1_dw", out_dtypes=gdt)

    def ln2_bwd(dh2v, x1v, doutv, mxv, g, mv):
        xhat, r = _rms(x1v, D)
        dn2 = dh2v * (1.0 + mv[4:5])
        dx1 = doutv + _rms_bwd(dn2 * g, xhat, r, D)
        return [dx1, mv[2:3] * dx1], [_colsum(dh2v), _colsum(dh2v * xhat * g), _colsum(dn2 * xhat), _colsum(dx1 * mxv)]

    dx1, dmixed, dshift2, dscale2, dg_mlp, dgate1 = _rowmap(
        ln2_bwd, [(dh2, D, 0), (x1, D, 0), (dout, D, 0), (mixed, D, 0)], [W["g_mlp"], modv],
        [(D, F32), (D, MX)], [(1, D)] * 4, tile=256, name="ln2_bwd")
    dcat = _mm(dmixed, W["w_out"], "nt", name="out_dx")
    g_out = _mm(cat, dmixed, "tn", name="out_dw", out_dtypes=gdt)

    def ml_out_bwd(dml, hf, hb, om, gmn):
        hs = hf + hb
        dhs, dos, dgs = [], [], []
        for hh in range(HM):
            sl = hsl(hh, DM)
            xhat, r = _rms(hs[:, sl], DM)
            g, sg, d = gmn[:, sl], _sigmoid(om[:, sl]), dml[:, sl]
            dos.append(d * xhat * g * sg * (1.0 - sg))
            dhn = d * sg
            dgs.append(_colsum(dhn * xhat))
            dhs.append(_rms_bwd(dhn * g, xhat, r, DM))
        return [cat1(dhs), cat1(dos)], [cat1(dgs)]

    dhs, do_m, dg_mn = _rowmap(ml_out_bwd, [(dcat, MW, 1)] + hrows, [W["g_mn"]], [(MW, F32), (MW, MX)],
                               [(1, MW)], tile=512, name="ml_out_bwd")
    dqd, dkd, dvd, dgates = _mlstm_bwd(qc, kc, proj, gates_t, W["bg_row"], dhs, cst, nm)
    dqk_m, dconv_w8, dconv_b = _conv_bwd(proj, dqd, dkd, W["conv_w8"], W["conv_b"])

    def do_aug(ao, dov):
        lane = lax.broadcasted_iota(jnp.int32, (ao.shape[0], 128), 1)
        outs = []
        for hh in range(H_MLA):
            sl = hsl(hh, VD)
            dl = jnp.sum(ao[:, sl].astype(F32) * dov[:, sl], axis=1, keepdims=True)
            hi = dl.astype(MX).astype(F32)
            outs += [dov[:, sl], jnp.where(lane == 0, -hi, jnp.where(lane == 1, hi - dl, 0.0))]
        return [cat1(outs)], []

    (doa,) = _rowmap(do_aug, [(attn_o, MW, 0), (dcat, MW, 0)], [], [(H_MLA * HP, MX)], tile=512, name="attn_delta")
    side = [g_out.reshape(N_CHIP, D // N_CHIP, D), g_ff1, g_ff2.reshape(N_CHIP, DFF // N_CHIP, D)] if late else ()
    dq_a, dk_a, dv_a, late_got = _attn_bwd(qa, kh, va, doa, side=side, side_cols=(1,))

    def mla_q_bwd(dqv, qr, cp, a1, a2, gq):
        outs, dg = [], 0.0
        for hh in range(H_MLA):
            sl = hsl(hh, HP)
            xhat, r = _rms(qr[:, sl], QK)
            d = dqv[:, sl]
            dyv = cat1([d[:, :NOPE], _rope_bwd(d[:, NOPE:], cp, a1, a2)])
            dg = dg + _colsum(dyv * xhat)
            outs.append(_rms_bwd(dyv * gq, xhat, r, QK))
        return [cat1(outs)], [dg]

    dq_raw, dgq = _rowmap(mla_q_bwd, [(dq_a, H_MLA * HP, 0), (q_raw, H_MLA * HP, 0)] + tabs, [W["gq"]],
                          [(H_MLA * HP, MX)], [(1, HP)], tile=512, name="mla_q_bwd")
    dcqn = _mm(dq_raw, W["w_uq"], "nt", name="q_up_dx")
    g_uq = _mm(cqn, dq_raw, "tn", name="q_up_dw", out_dtypes=gdt)

    def mla_k_bwd(dkv, dvv, kvr, kpe, cp, a1, a2, gk):
        dkn, dg, dkpe = [], 0.0, 0.0
        for hh in range(H_MLA):
            xhat, r = _rms(cat1([kvr[:, hsl(hh, NOPE)], kpe]), QK)
            d = dkv[:, hsl(hh, HP)]
            dyv = cat1([d[:, :NOPE], _rope_bwd(d[:, NOPE:], cp, a1, a2)])
            dg = dg + _colsum(dyv * xhat)
            dxv = _rms_bwd(dyv * gk, xhat, r, QK)
            dkn.append(dxv[:, :NOPE])
            dkpe = dkpe + dxv[:, NOPE:]
        return [cat1(dkn + [dvv[:, hh * HP:hh * HP + VD] for hh in range(H_MLA)]), dkpe], [dg]

    dkv_raw, dkpe, dgk = _rowmap(
        mla_k_bwd, [(dk_a, H_MLA * HP, 0), (dv_a, H_MLA * HP, 0), (kv_raw, 2 * H_MLA * NOPE, 0),
                    (proj, 128, P_KPE // 128)] + tabs, [W["gk"]],
        [(2 * H_MLA * NOPE, MX), (128, MX)], [(1, HP)], tile=256, name="mla_k_bwd")
    dckvn = _mm(dkv_raw, W["w_ukv"], "nt", name="kv_up_dx")
    g_ukv = _mm(ckvn, dkv_raw, "tn", name="kv_up_dw", out_dtypes=gdt)

    def lora_bwd(dcq, dckv, cq, ckv, gq, gkv):
        xq, rq = _rms(cq, Q_LORA)
        xk, rk = _rms(ckv, KV_LORA)
        return ([_rms_bwd(dcq * gq, xq, rq, Q_LORA), _rms_bwd(dckv * gkv, xk, rk, KV_LORA)],
                [_colsum(dcq * xq), _colsum(dckv * xk)])

    dc_q, dc_kv, dg_qlora, dg_kvlora = _rowmap(
        lora_bwd, [(dcqn, Q_LORA, 0), (dckvn, KV_LORA, 0), (proj, Q_LORA, P_CQ // Q_LORA),
                   (proj, KV_LORA, P_CKV // KV_LORA)], [W["g_qlora"], W["g_kvlora"]],
        [(Q_LORA, MX), (KV_LORA, MX)], [(1, Q_LORA), (1, KV_LORA)], tile=512, name="lora_bwd")

    nc = S // LCH
    dg16 = jnp.stack(dgates)[:, :, :, 0:2, :].transpose(2, 4, 0, 3, 1).reshape(S, NG)
    dg128 = jnp.pad(dg16, ((0, 0), (0, 128 - NG)))

    def assemble(dqk, dv0, dv1, dom, dcq, dckv, dkp, dgp):
        f = lambda t: t.astype(F32)
        return [cat1([f(dqk), dv0 + dv1, f(dom), f(dcq), f(dckv), f(dkp), dgp])], [_colsum(dgp)]

    dproj, dbg = _rowmap(
        assemble, [(dqk_m, 2 * MW, 0), (dvd[0], MW, 0), (dvd[1], MW, 0), (do_m, MW, 0), (dc_q, Q_LORA, 0),
                   (dc_kv, KV_LORA, 0), (dkpe, 128, 0), (dg128, 128, 0)], [], [(D_INP, MX)], [(1, 128)],
        tile=256, name="dproj")
    g_in = _mm(h, dproj, "tn", name="proj_dw", out_dtypes=gdt)
    early_got = ()
    if late:
        side = [_unpad_w_in_slabs(g_in).astype(MX), _slabs(_unpad_w_uq(g_uq)).astype(MX),
                _slabs(_unperm_w_ukv(g_ukv)).astype(MX)]
        dh, early_got = _mm(dproj, W["w_in"], "nt", name="proj_dx", side=side)
    else:
        dh = _mm(dproj, W["w_in"], "nt", name="proj_dx")

    def ln1_bwd(dhv, xv, dx1v, g, mv):
        xhat, r = _rms(xv, D)
        dn = dhv * (1.0 + mv[1:2])
        return [dx1v + _rms_bwd(dn * g, xhat, r, D)], [_colsum(dhv), _colsum(dhv * xhat * g), _colsum(dn * xhat)]

    gx, dshift1, dscale1, dg_mix = _rowmap(ln1_bwd, [(dh, D, 0), (x, D, 0), (dx1, D, 0)], [W["g_mix"], modv],
                                           [(D, F32)], [(1, D)] * 3, tile=256, name="ln1_bwd")
    dmodv = jnp.concatenate([dshift1, dscale1, dgate1, dshift2, dscale2, dgate2], axis=0)
    grads = dict(w_in=g_in, w_uq=g_uq, w_ukv=g_ukv, w_out=g_out, w_ff1=g_ff1, w_ff2=g_ff2,
                 norm_mix_g=dg_mix, b_gates=dbg[:, :NG], conv_w=dconv_w8[:CONVW], conv_b=dconv_b,
                 q_lora_g=dg_qlora, kv_lora_g=dg_kvlora, q_norm_g=dgq[:, :QK], k_norm_g=dgk[:, :QK],
                 mlstm_norm_g=dg_mn, norm_mlp_g=dg_mlp)
    grads["got"] = list(early_got) + list(late_got)
    return loss, gx, dmodv, grads


def _cols(g):
    return g.transpose(1, 0, 2).reshape(g.shape[1], N_CHIP * g.shape[2])


def _slabs(gfull):
    return gfull.reshape(gfull.shape[0], N_CHIP, -1).transpose(1, 0, 2)


def _prep_weights(w_in, w_uq, w_ukv, w_out, w_ff1, w_ff2, norm_mix_g, norm_mlp_g, q_lora_g, kv_lora_g,
                  q_norm_g, k_norm_g, mlstm_norm_g, conv_w, conv_b, b_gates):
    MX = _MXU_DTYPE
    padg = lambda g: jnp.pad(g.reshape(1, QK).astype(F32), ((0, 0), (0, HP - QK)))
    return dict(
        w_in=(_pad_w_in_slabs(w_in) if w_in.ndim == 3 else _pad_w_in(w_in)).astype(MX), w_uq=_pad_w_uq(w_uq).astype(MX), w_ukv=_perm_w_ukv(w_ukv).astype(MX),
        w_out=None if w_out is None else w_out.astype(MX), w_ff1=None if w_ff1 is None else w_ff1.astype(MX),
        w_ff2=None if w_ff2 is None else w_ff2.astype(MX),
        g_mix=norm_mix_g.reshape(1, D), g_mlp=norm_mlp_g.reshape(1, D), g_qlora=q_lora_g.reshape(1, Q_LORA),
        g_kvlora=kv_lora_g.reshape(1, KV_LORA), gq=padg(q_norm_g), gk=padg(k_norm_g),
        g_mn=mlstm_norm_g.reshape(1, MW), conv_w8=jnp.pad(conv_w.reshape(CONVW, 2 * MW), ((0, 8 - CONVW), (0, 0))),
        conv_b=conv_b.reshape(1, 2 * MW), bg_row=jnp.pad(b_gates.reshape(1, NG), ((0, 0), (0, 128 - NG))))


MESH = pl.DeviceIdType.MESH
N_DEV = 8
N_CHIP = 4


def _comm_call(body, **kw):
    if _INTERPRET:
        kw["interpret"] = pltpu.InterpretParams()
    return pl.pallas_call(body, **kw)


def _allgather8(blk, *, name):
    m_per, n = blk.shape

    def body(x_ref, out_ref, send_sems, recv_sems, local_sem):
        x, y, c = lax.axis_index("x"), lax.axis_index("y"), lax.axis_index("c")
        me, sibling = (x, y, c), (x, y, 1 - c)
        chips = [(1 - x, y), (x, 1 - y), (1 - x, 1 - y)]

        def rows(px, py, pc):
            return out_ref.at[pl.ds((4 * px + 2 * py + pc) * m_per, m_per), :]

        def copy(k, block, to, src=None):
            return pltpu.make_async_remote_copy(
                src_ref=rows(*block) if src is None else src, dst_ref=rows(*block),
                send_sem=send_sems.at[k], recv_sem=recv_sems.at[k], device_id=to, device_id_type=MESH)

        mine = pltpu.make_async_copy(x_ref, rows(*me), local_sem)
        mine.start()
        first = [copy(0, me, sibling, src=x_ref)]
        first += [copy(1 + j, me, (*chip, c), src=x_ref) for j, chip in enumerate(chips)]
        for cp in first:
            cp.start()
        passed = [copy(4 + j, (*chip, c), sibling) for j, chip in enumerate(chips)]
        for j, chip in enumerate(chips):
            copy(1 + j, (*chip, c), me).wait_recv()
            passed[j].start()
        copy(0, sibling, me).wait_recv()
        for j, chip in enumerate(chips):
            copy(4 + j, (*chip, 1 - c), me).wait_recv()
        for cp in first + passed:
            cp.wait_send()
        mine.wait()

    return _comm_call(
        body, name=name, out_shape=jax.ShapeDtypeStruct((N_DEV * m_per, n), blk.dtype),
        in_specs=[pl.BlockSpec(memory_space=pltpu.VMEM)], out_specs=pl.BlockSpec(memory_space=pltpu.VMEM),
        scratch_shapes=[pltpu.SemaphoreType.DMA((7,)), pltpu.SemaphoreType.DMA((7,)), pltpu.SemaphoreType.DMA],
    )(blk)


def _exchange_io(arrays, gather, cols=()):
    n = len(arrays)

    def out(i, a):
        if gather:
            return (a.shape[0], N_CHIP * a.shape[1]) if i in cols else (N_CHIP, *a.shape)
        return (N_CHIP, a.shape[0], a.shape[1] // N_CHIP) if i in cols else a.shape

    return dict(
        specs=[pl.BlockSpec(memory_space=pltpu.HBM)] * n,
        out_shape=[jax.ShapeDtypeStruct(out(i, a), a.dtype) for i, a in enumerate(arrays)],
        scratch=[pltpu.SemaphoreType.DMA((3 * n,)), pltpu.SemaphoreType.DMA((3 * n,)), pltpu.SemaphoreType.DMA((n,))])


def _exchange_ops(ins, outs, send_sems, recv_sems, local_sems, *, gather, cols=()):
    n = len(ins)
    x, y, c = lax.axis_index("x"), lax.axis_index("y"), lax.axis_index("c")
    k = 2 * x + y
    chips = [(1 - x, y), (x, 1 - y), (1 - x, 1 - y)]

    def piece(ref, a, chip, windowed):
        if not windowed:
            return ref.at[chip]
        width = ref.shape[1] // N_CHIP
        return ref.at[:, pl.ds(pl.multiple_of(chip * width, 128), width)]

    src_of = lambda a, chip: ins[a] if gather else piece(ins[a], a, chip, a in cols)
    dst_of = lambda a, chip: piece(outs[a], a, chip, gather and a in cols)

    def remote(a, j):
        px, py = chips[j]
        return pltpu.make_async_remote_copy(
            src_ref=src_of(a, 2 * px + py), dst_ref=dst_of(a, k), send_sem=send_sems.at[3 * a + j],
            recv_sem=recv_sems.at[3 * a + j], device_id=(px, py, c), device_id_type=MESH)

    def arrival(a, j):
        px, py = chips[j]
        return pltpu.make_async_remote_copy(
            src_ref=src_of(a, k), dst_ref=dst_of(a, 2 * px + py), send_sem=send_sems.at[3 * a + j],
            recv_sem=recv_sems.at[3 * a + j], device_id=(px, py, c), device_id_type=MESH)

    local = [pltpu.make_async_copy(src_of(a, k), dst_of(a, k), local_sems.at[a]) for a in range(n)]
    sent = [remote(a, j) for a in range(n) for j in range(3)]

    def start():
        for cp in local + sent:
            cp.start()

    def wait():
        for a in range(n):
            for j in range(3):
                arrival(a, j).wait_recv()
        for cp in sent:
            cp.wait_send()
        for cp in local:
            cp.wait()

    return start, wait


def _chip_allgather_halved(shards, *, name):
    n = len(shards)
    half_rows = [s.shape[0] // 2 for s in shards]
    assert all(s.shape[0] % 16 == 0 for s in shards)

    def body(*refs):
        ins, outs = refs[:n], refs[n:2 * n]
        ici_send, ici_recv, d2d_send, d2d_recv, local_sems = refs[2 * n:]
        x, y, c = lax.axis_index("x"), lax.axis_index("y"), lax.axis_index("c")
        k = 2 * x + y
        chips = [(1 - x, y), (x, 1 - y), (1 - x, 1 - y)]

        def half(a, slab, core):
            return outs[a].at[slab, pl.ds(core * half_rows[a], half_rows[a])]

        def ici(a, j, slab):
            px, py = chips[j]
            return pltpu.make_async_remote_copy(
                src_ref=ins[a].at[pl.ds(c * half_rows[a], half_rows[a])], dst_ref=half(a, slab, c),
                send_sem=ici_send.at[3 * a + j], recv_sem=ici_recv.at[3 * a + j],
                device_id=(px, py, c), device_id_type=MESH)

        def d2d(a, j, core):
            px, py = chips[j]
            return pltpu.make_async_remote_copy(
                src_ref=half(a, 2 * px + py, core), dst_ref=half(a, 2 * px + py, core),
                send_sem=d2d_send.at[3 * a + j], recv_sem=d2d_recv.at[3 * a + j],
                device_id=(x, y, 1 - c), device_id_type=MESH)

        local = [pltpu.make_async_copy(ins[a], outs[a].at[k], local_sems.at[a]) for a in range(n)]
        sent = [ici(a, j, k) for a in range(n) for j in range(3)]
        for cp in local + sent:
            cp.start()
        passed = []
        for a in range(n):
            for j, (px, py) in enumerate(chips):
                ici(a, j, 2 * px + py).wait_recv()
                passed.append(d2d(a, j, c))
                passed[-1].start()
        for a in range(n):
            for j in range(3):
                d2d(a, j, 1 - c).wait_recv()
        for cp in sent + passed:
            cp.wait_send()
        for cp in local:
            cp.wait()

    hbm = pl.BlockSpec(memory_space=pltpu.HBM)
    return _comm_call(
        body, name=name, out_shape=[jax.ShapeDtypeStruct((N_CHIP, *s.shape), s.dtype) for s in shards],
        in_specs=[hbm] * n, out_specs=[hbm] * n,
        scratch_shapes=[pltpu.SemaphoreType.DMA((3 * n,))] * 4 + [pltpu.SemaphoreType.DMA((n,))],
    )(*shards)


def _sum_blocks(a, nblk, *, name):
    n = a.shape[1]

    def body(a_ref, o_ref):
        acc = a_ref[pl.ds(0, 8), :]
        for d in range(1, nblk):
            acc = acc + a_ref[pl.ds(8 * d, 8), :]
        o_ref[...] = acc

    return _call(body, name=name, out_shape=jax.ShapeDtypeStruct((8, n), F32))(a)


def _outer8(sct, dm, *, name, tm=256, tn=1024):
    R, N = sct.shape[0], dm.shape[1]
    tm, tn = min(tm, R), min(tn, N)

    def body(s_ref, d_ref, o_ref):
        s, dmv = s_ref[...], d_ref[...]
        acc = s[:, 0:1] * dmv[0:1, :]
        for b in range(1, 8):
            acc = acc + s[:, b:b + 1] * dmv[b:b + 1, :]
        o_ref[...] = acc

    return _call(
        body, name=name, grid=(R // tm, N // tn),
        in_specs=[pl.BlockSpec((tm, 8), lambda i, j: (i, 0)), pl.BlockSpec((8, tn), lambda i, j: (0, j))],
        out_specs=pl.BlockSpec((tm, tn), lambda i, j: (i, j)),
        out_shape=jax.ShapeDtypeStruct((R, N), F32),
        compiler_params=_cparams(("parallel", "parallel")),
    )(sct, dm)


_BC1 = 1.0 - ADAM_B1 ** ADAM_STEP
_BC2 = 1.0 - ADAM_B2 ** ADAM_STEP


def _adamw(w, g_parts, m, v, *, name, tile=128, sibling=()):
    R, C = w.shape[-2:]
    tile = min(tile, R)
    assert R % tile == 0
    npart, ns, nt = len(g_parts), len(sibling), R // tile

    def body(*refs):
        w_ref, m_ref, v_ref = refs[npart:npart + 3]
        g_o, d_o, m_o, v_o = refs[npart + 3 + ns:npart + 7 + ns]
        if ns:
            sib_in, sib_out = refs[npart + 3:npart + 3 + ns], refs[npart + 7 + ns:npart + 7 + 2 * ns]
            send_sems, recv_sems = refs[npart + 7 + 2 * ns:]
            x, y, c = lax.axis_index("x"), lax.axis_index("y"), lax.axis_index("c")
            swaps = [pltpu.make_async_remote_copy(
                src_ref=sib_in[a], dst_ref=sib_out[a], send_sem=send_sems.at[a], recv_sem=recv_sems.at[a],
                device_id=(x, y, 1 - c), device_id_type=MESH) for a in range(ns)]

            @pl.when(pl.program_id(0) == 0)
            def _():
                for cp in swaps:
                    cp.start()

        g = refs[0][...].astype(F32)
        for r in refs[1:npart]:
            g = g + r[...].astype(F32)
        mn = ADAM_B1 * m_ref[...] + (1.0 - ADAM_B1) * g
        vn = ADAM_B2 * v_ref[...] + (1.0 - ADAM_B2) * jnp.square(g)
        g_o[...] = g
        m_o[...] = mn
        v_o[...] = vn
        d_o[...] = -ADAM_LR * ((mn / _BC1) / (jnp.sqrt(vn / _BC2) + ADAM_EPS) + ADAM_WD * w_ref[...])
        if ns:
            @pl.when(pl.program_id(0) == nt - 1)
            def _():
                for cp in swaps:
                    cp.wait()

    spec = pl.BlockSpec((tile, C), lambda i: (i, 0))
    wspec = spec if w.ndim == 2 else pl.BlockSpec((None, tile, C), lambda i: (0, i, 0))
    hbm = pl.BlockSpec(memory_space=pltpu.HBM)
    res = _call(
        body, name=name, grid=(nt,), in_specs=[spec] * npart + [wspec] * 3 + [hbm] * ns,
        out_specs=[wspec] * 4 + [hbm] * ns,
        out_shape=[jax.ShapeDtypeStruct(w.shape, F32)] * 4 + [jax.ShapeDtypeStruct(a.shape, a.dtype) for a in sibling],
        scratch_shapes=[pltpu.SemaphoreType.DMA((ns,)), pltpu.SemaphoreType.DMA((ns,))] if ns else [],
        compiler_params=_cparams(("arbitrary",) if ns else ("parallel",)),
    )(*g_parts, w, m, v, *sibling)
    return (tuple(res[:4]), list(res[4:])) if ns else tuple(res)


def _pack(vecs, rows8_cols):
    flat = jnp.concatenate([v.reshape(-1).astype(F32) for v in vecs])
    return jnp.pad(flat, (0, 8 * rows8_cols - flat.shape[0])).reshape(8, rows8_cols)


def _unpack(flat, shapes):
    out, o = [], 0
    for s in shapes:
        n = math.prod(s)
        out.append(flat[o:o + n].reshape(s))
        o += n
    return out


_BIG = ("w_in", "w_uq", "w_ukv", "w_out", "w_ff1", "w_ff2")
_SMALL = ("b_ada", "norm_mix_g", "b_gates", "conv_w", "conv_b", "q_lora_g", "kv_lora_g", "q_norm_g", "k_norm_g",
          "mlstm_norm_g", "norm_mlp_g")
_ORDER = ("w_ada", "b_ada", "norm_mix_g", "w_in", "b_gates", "conv_w", "conv_b", "q_lora_g", "w_uq", "kv_lora_g",
          "w_ukv", "q_norm_g", "k_norm_g", "mlstm_norm_g", "w_out", "norm_mlp_g", "w_ff1", "w_ff2")


def kernel(x, c, positions, w_ada, b_ada, norm_mix_g, w_in, b_gates, conv_w, conv_b, q_lora_g, w_uq, kv_lora_g, w_ukv, q_norm_g, k_norm_g, mlstm_norm_g, w_out, norm_mlp_g, w_ff1, w_ff2, loss_target, m_w_ada, m_b_ada, m_norm_mix_g, m_w_in, m_b_gates, m_conv_w, m_conv_b, m_q_lora_g, m_w_uq, m_kv_lora_g, m_w_ukv, m_q_norm_g, m_k_norm_g, m_mlstm_norm_g, m_w_out, m_norm_mlp_g, m_w_ff1, m_w_ff2, v_w_ada, v_b_ada, v_norm_mix_g, v_w_in, v_b_gates, v_conv_w, v_conv_b, v_q_lora_g, v_w_uq, v_kv_lora_g, v_w_ukv, v_q_norm_g, v_k_norm_g, v_mlstm_norm_g, v_w_out, v_norm_mlp_g, v_w_ff1, v_w_ff2):
    args = dict(locals())
    wts = {n: args[n] for n in _ORDER}
    mom = {n: args["m_" + n] for n in _ORDER}
    var = {n: args["v_" + n] for n in _ORDER}
    MX = _MXU_DTYPE
    xi, yi, ci = lax.axis_index("x"), lax.axis_index("y"), lax.axis_index("c")
    chip = 2 * xi + yi
    dev = 2 * chip + ci
    S = x.shape[1]
    CS = 2 * MW // N_CHIP
    GS = DM // N_CHIP

    pk = _pack([c, conv_w, mlstm_norm_g], 1024)
    allpk = _allgather8(pk, name="gather_small").reshape(N_DEV, 8 * 1024)
    c_all = allpk[:, :D]
    per_chip = allpk[0::2]
    conv_w_full = per_chip[:, D:D + CONVW * CS].reshape(N_CHIP, CONVW, CS).transpose(1, 0, 2).reshape(CONVW, 2 * MW)
    o = D + CONVW * CS
    mn_full = per_chip[:, o:o + HM * GS].reshape(N_CHIP, HM, GS).transpose(1, 0, 2).reshape(HM, DM)

    (sc,) = _rowmap(lambda cv: ([cv * _sigmoid(cv)], []), [(c_all, D, 0)], [], [(D, F32)], tile=8, name="silu_c")
    ncol = w_ada.shape[2]
    b_cols = lax.dynamic_slice(b_ada, (0, chip * ncol), (1, ncol))
    modp = _mm(sc, w_ada[0], "nn", name="ada_fwd", tm=8, tn=1024, tk=512, extras=(jnp.broadcast_to(b_cols, (8, ncol)),),
               epilogue=lambda r, b: (r + b,))
    modg = _allgather8(modp, name="gather_mod").reshape(N_CHIP, 2, 8, ncol)[:, 0]
    mod_all = modg.transpose(1, 0, 2).reshape(N_DEV, N_CHIP * ncol)
    modv = jnp.pad(lax.dynamic_slice(mod_all, (dev, 0), (1, 6 * D)).reshape(6, D), ((0, 2), (0, 0)))

    shards = [wts[n][0].astype(MX) for n in _BIG]
    gw_in, gw_uq, gw_ukv = _chip_allgather_halved(shards[:3], name="gather_weights")
    W = _prep_weights(gw_in, _cols(gw_uq), _cols(gw_ukv), None, None, None, norm_mix_g, norm_mlp_g,
                      q_lora_g, kv_lora_g, q_norm_g, k_norm_g, mn_full, conv_w_full, conv_b, b_gates)

    loss, gx, dmodv, g = _device_step(x[0], loss_target[0], positions[0], modv, W, late=shards[3:])

    small_shapes = [(6 * D,), (D,), (NG,), (CONVW, 2 * MW), (2 * MW,), (Q_LORA,), (KV_LORA,), (QK,), (QK,), (MW,), (D,), (1,)]
    pg = _pack([dmodv, g["norm_mix_g"], g["b_gates"], g["conv_w"], g["conv_b"], g["q_lora_g"], g["kv_lora_g"],
                g["q_norm_g"], g["k_norm_g"], g["mlstm_norm_g"], g["norm_mlp_g"], loss], 4096)
    allpg = _allgather8(pg, name="gather_small_grads")
    tot = _unpack(_sum_blocks(allpg, N_DEV, name="sum_small_grads").reshape(-1), small_shapes)
    dmod_all = allpg.reshape(N_DEV, 8 * 4096)[:, :6 * D]
    gsmall = dict(zip(_SMALL, [tot[0].reshape(1, 6 * D), tot[1].reshape(1, D), tot[2].reshape(1, NG),
                               lax.dynamic_slice(tot[3], (0, chip * CS), (CONVW, CS)).reshape(1, CONVW, CS),
                               tot[4].reshape(1, 2 * MW), tot[5].reshape(1, Q_LORA), tot[6].reshape(1, KV_LORA),
                               tot[7].reshape(1, QK), tot[8].reshape(1, QK),
                               lax.dynamic_slice(tot[9].reshape(HM, DM), (0, chip * GS), (HM, GS)).reshape(1, HM, GS),
                               tot[10].reshape(1, D)]))
    loss_tot = tot[11].reshape(())

    got = g["got"]
    part = []
    for nme, r in zip(_BIG, got):
        wd = r.shape[2]
        (p,) = _rowmap(lambda a0, a1, a2, a3: ([(a0.astype(F32) + a1.astype(F32)) + (a2.astype(F32) + a3.astype(F32))], []),
                       [(r, wd, 0, k) for k in range(N_CHIP)], [], [(wd, F32)], tile=256, name="sum_chips_" + nme)
        part.append(p)

    dm_cols = lax.dynamic_slice(dmod_all, (0, chip * ncol), (N_DEV, ncol))
    g_ada = _outer8(sc.T, dm_cols, name="ada_dw")

    res = {}
    res["w_ada"], other = _adamw(w_ada, [g_ada], m_w_ada, v_w_ada, name="adamw_w_ada", sibling=part)
    for nme, p, q in zip(_BIG, part, other):
        res[nme] = _adamw(wts[nme], [p, q], mom[nme], var[nme], name="adamw_" + nme)
    sw = _pack([wts[n] for n in _SMALL], 3072)
    sg = _pack([gsmall[n] for n in _SMALL], 3072)
    sm = _pack([mom[n] for n in _SMALL], 3072)
    sv = _pack([var[n] for n in _SMALL], 3072)
    small_res = _adamw(sw, [sg], sm, sv, name="adamw_small", tile=8)
    shapes = [wts[n].shape for n in _SMALL]
    unp = [_unpack(r.reshape(-1), shapes) for r in small_res]
    for i, nme in enumerate(_SMALL):
        res[nme] = tuple(u[i] for u in unp)
    outs = [loss_tot, gx[None]]
    for kind in range(4):
        outs += [res[n][kind].reshape(wts[n].shape) for n in _ORDER]
    return tuple(outs)
```

```python
import functools
import math

import jax
import jax.numpy as jnp
from jax import lax
from jax.experimental import pallas as pl
from jax.experimental.pallas import tpu as pltpu

F32 = jnp.float32
BF16 = jnp.bfloat16
_MXU_DTYPE = jnp.bfloat16
_INTERPRET = False

D = 2048
H_MLA = 8
NOPE = 128
ROPE = 64
QK = NOPE + ROPE
HP = 256
VD = 128
Q_LORA = 512
KV_LORA = 256
HM = 4
DM = 256
MW = HM * DM
LCH = 128
CONVW = 5
NG = 16
DFF = 4 * D
EPS = 1e-6
M_INIT = -1e30
ROPE_THETA = 10000.0
IN_SIZES = (Q_LORA, KV_LORA, ROPE, MW, MW, MW, MW, NG)
D_IN = sum(IN_SIZES)
P_QM, P_KM, P_VM, P_OM, P_CQ, P_CKV, P_KPE, P_G = 0, 1024, 2048, 3072, 4096, 4608, 4864, 4992
D_INP = 5120

ADAM_LR, ADAM_B1, ADAM_B2, ADAM_EPS, ADAM_WD, ADAM_STEP = 0.001, 0.9, 0.999, 1e-08, 0.01, 10

V7X_VMEM_LIMIT = 56 * 1024 * 1024


def _cparams(sem):
    return pltpu.CompilerParams(dimension_semantics=sem, vmem_limit_bytes=V7X_VMEM_LIMIT)


def _call(body, **kw):
    if _INTERPRET:
        kw.pop("compiler_params", None)
        kw["interpret"] = pltpu.InterpretParams()
    return pl.pallas_call(body, **kw)


def _dot(a, b, form):
    dims = {"nn": ((1,), (0,)), "nt": ((1,), (1,)), "tn": ((0,), (0,))}[form]
    return lax.dot_general(a.astype(_MXU_DTYPE), b.astype(_MXU_DTYPE), (dims, ((), ())),
                           preferred_element_type=F32)


def _mm(a, b, form, *, name, out_dtypes=(F32,), epilogue=None, extras=(), tm=1024, tn=1024, tk=2048, side=()):
    if form == "nn":
        (M, K), (K2, N) = a.shape, b.shape
    elif form == "nt":
        (M, K), (N, K2) = a.shape, b.shape
    else:
        (K, M), (K2, N) = a.shape, b.shape
    assert K == K2, (a.shape, b.shape, form)
    tm, tn = min(tm, M), min(tn, N)
    tk = max(d for d in range(128, min(tk, K) + 1, 128) if K % d == 0) if K > 128 else K
    assert M % tm == 0 and N % tn == 0 and K % tk == 0, (M, N, K, tm, tn, tk)
    nk = K // tk
    ne, no = len(extras), len(out_dtypes)
    if form == "tn":
        a_spec = pl.BlockSpec((tk, tm), lambda i, j, k: (k, i))
    else:
        a_spec = pl.BlockSpec((tm, tk), lambda i, j, k: (i, k))
    if form == "nt":
        b_spec = pl.BlockSpec((tn, tk), lambda i, j, k: (j, k))
    else:
        b_spec = pl.BlockSpec((tk, tn), lambda i, j, k: (k, j))
    mn_spec = pl.BlockSpec((tm, tn), lambda i, j, k: (i, j))
    grid = (M // tm, N // tn, nk)
    ns, io, wrap = _side_exchange(side, False, grid)

    def body(a_ref, b_ref, *rest):
        ex, outs = rest[:ne], rest[ne + ns:ne + ns + no]
        scratch = rest[ne + 2 * ns + no:]
        side_start, side_wait = wrap(rest[ne:ne + ns], rest[ne + ns + no:ne + 2 * ns + no], scratch[1:])
        side_start()
        prod = _dot(a_ref[...], b_ref[...], form)

        def finish(r):
            vals = (r,) if epilogue is None else epilogue(r, *[e[...] for e in ex])
            for o, v in zip(outs, vals):
                o[...] = v.astype(o.dtype)

        if nk == 1:
            finish(prod)
        else:
            acc, k = scratch[0], pl.program_id(2)

            @pl.when(k == 0)
            def _():
                acc[...] = prod

            @pl.when(k > 0)
            def _():
                acc[...] += prod

            @pl.when(k == nk - 1)
            def _():
                finish(acc[...])
        side_wait()

    res = _call(
        body, name=name, grid=grid,
        in_specs=[a_spec, b_spec] + [mn_spec] * ne + io["specs"],
        out_specs=[mn_spec] * no + io["specs"],
        out_shape=[jax.ShapeDtypeStruct((M, N), dt) for dt in out_dtypes] + io["out_shape"],
        scratch_shapes=[pltpu.VMEM((tm, tn) if nk > 1 else (8, 128), F32)] + io["scratch"],
        compiler_params=_cparams(("arbitrary",) * 3 if ns else ("parallel", "parallel", "arbitrary")),
    )(a, b, *extras, *side)
    if ns:
        return (res[0] if no == 1 else res[:no]), list(res[no:])
    return res[0] if no == 1 else res


def _rowmap(fn, rows, bcasts, outs, accs=(), *, tile, name):
    rows = [r if len(r) == 4 else (*r, None) for r in rows]
    S = rows[0][0].shape[-2]
    tile = min(tile, S)
    assert S % tile == 0
    nr, nb, no, na = len(rows), len(bcasts), len(outs), len(accs)

    def body(*refs):
        vals = [r[...] for r in refs[:nr + nb]]
        o_refs, a_refs = refs[nr + nb:nr + nb + no], refs[nr + nb + no:]
        o_vals, a_vals = fn(*vals)
        for r, v in zip(o_refs, o_vals):
            r[...] = v.astype(r.dtype)
        if na:
            @pl.when(pl.program_id(0) == 0)
            def _():
                for r in a_refs:
                    r[...] = jnp.zeros(r.shape, r.dtype)
            for r, v in zip(a_refs, a_vals):
                r[...] += v

    in_specs = []
    for (arr, w, cb, lead) in rows:
        if lead is None:
            in_specs.append(pl.BlockSpec((tile, w), lambda i, cb=cb: (i, cb)))
        else:
            in_specs.append(pl.BlockSpec((None, tile, w), lambda i, cb=cb, lead=lead: (lead, i, cb)))
    in_specs += [pl.BlockSpec(b.shape, lambda i: (0, 0)) for b in bcasts]
    out_specs = [pl.BlockSpec((tile, w), lambda i: (i, 0)) for (w, _) in outs]
    out_specs += [pl.BlockSpec(s, lambda i: (0, 0)) for s in accs]
    out_shape = [jax.ShapeDtypeStruct((S, w), dt) for (w, dt) in outs]
    out_shape += [jax.ShapeDtypeStruct(s, F32) for s in accs]
    return _call(
        body, name=name, grid=(S // tile,), in_specs=in_specs, out_specs=out_specs, out_shape=out_shape,
        compiler_params=_cparams(("arbitrary",)),
    )(*[r[0] for r in rows], *bcasts)


def _colsum(v):
    return jnp.sum(v, axis=0, keepdims=True)


def _rms(x, n):
    r = lax.rsqrt(jnp.sum(x * x, axis=-1, keepdims=True) * (1.0 / n) + EPS)
    return x * r, r


def _rms_bwd(dxhat, xhat, r, n):
    return r * (dxhat - xhat * (jnp.sum(dxhat * xhat, axis=-1, keepdims=True) * (1.0 / n)))


def _rope_fwd(r, cosp, s1, s2):
    return r * cosp + pltpu.roll(r, 32, 1) * s1 + pltpu.roll(r, 96, 1) * s2


def _rope_bwd(d, cosp, s1, s2):
    return d * cosp + pltpu.roll(d * s1, 96, 1) + pltpu.roll(d * s2, 32, 1)


def _sigmoid(x):
    return 1.0 / (1.0 + jnp.exp(-x))


def _halo_specs(tile, halo, width, cb, S, lead=None):
    nh = tile // halo
    last = S // halo - 1
    if lead is None:
        return [
            pl.BlockSpec((tile, width), lambda i: (i, cb)),
            pl.BlockSpec((halo, width), lambda i: (jnp.maximum(i * nh - 1, 0), cb)),
            pl.BlockSpec((halo, width), lambda i: (jnp.minimum((i + 1) * nh, last), cb)),
        ]
    return [
        pl.BlockSpec((None, tile, width), lambda i: (lead, i, cb)),
        pl.BlockSpec((None, halo, width), lambda i: (lead, jnp.maximum(i * nh - 1, 0), cb)),
        pl.BlockSpec((None, halo, width), lambda i: (lead, jnp.minimum((i + 1) * nh, last), cb)),
    ]


def _conv_fwd(proj, conv_w8, conv_b, *, tile=256):
    S = proj.shape[0]
    T = min(tile, S)
    n = S // T
    W = 2 * MW

    def body(x_ref, xp_ref, xn_ref, w_ref, b_ref, q_ref, k_ref, ext):
        i = pl.program_id(0)
        ext[pl.ds(0, 8), :] = xp_ref[...] * (i > 0).astype(F32)
        ext[pl.ds(8, T), :] = x_ref[...]
        ext[pl.ds(8 + T, 8), :] = xn_ref[...] * (i < n - 1).astype(F32)
        w = w_ref[...]
        y = b_ref[...] + w[0:1, :] * ext[pl.ds(6, T), :]
        for o in range(1, CONVW):
            y = y + w[o:o + 1, :] * ext[pl.ds(6 + o, T), :]
        y = y * _sigmoid(y)
        q_ref[...] = y[:, :MW].astype(q_ref.dtype)
        k_ref[...] = (y[:, MW:] * (DM ** -0.5)).astype(k_ref.dtype)

    return _call(
        body, name="conv_fwd", grid=(n,),
        in_specs=_halo_specs(T, 8, W, 0, S) + [pl.BlockSpec((8, W), lambda i: (0, 0)),
                                                 pl.BlockSpec((1, W), lambda i: (0, 0))],
        out_specs=[pl.BlockSpec((T, MW), lambda i: (i, 0))] * 2,
        out_shape=[jax.ShapeDtypeStruct((S, MW), _MXU_DTYPE)] * 2,
        scratch_shapes=[pltpu.VMEM((T + 16, W), F32)],
        compiler_params=_cparams(("arbitrary",)),
    )(proj, proj, proj, conv_w8, conv_b)


def _conv_bwd(proj, dqd, dkd, conv_w8, conv_b, *, tile=256):
    S = proj.shape[0]
    T = min(tile, S)
    n = S // T
    W = 2 * MW

    def body(x_ref, xp_ref, xn_ref, *rest):
        g = rest[:12]
        w_ref, b_ref, dx_ref, dw_ref, db_ref, ext, edp = rest[12:]
        i = pl.program_id(0)
        mp = (i > 0).astype(F32)
        mn = (i < n - 1).astype(F32)
        ext[pl.ds(0, 16), :] = xp_ref[...] * mp
        ext[pl.ds(16, T), :] = x_ref[...]
        ext[pl.ds(16 + T, 16), :] = xn_ref[...] * mn
        w = w_ref[...]
        pre = b_ref[...] + w[0:1, :] * ext[pl.ds(6, T + 16), :]
        for o in range(1, CONVW):
            pre = pre + w[o:o + 1, :] * ext[pl.ds(6 + o, T + 16), :]
        sg = _sigmoid(pre)
        dsilu = sg * (1.0 + pre * (1.0 - sg))
        for half, (a0, a1) in enumerate(((g[0:3], g[3:6]), (g[6:9], g[9:12]))):
            sc = 1.0 if half == 0 else DM ** -0.5
            cols = pl.ds(half * MW, MW)
            edp[pl.ds(0, 8), cols] = (a0[1][...] + a1[1][...]) * (mp * sc)
            edp[pl.ds(8, T), cols] = (a0[0][...] + a1[0][...]) * sc
            edp[pl.ds(8 + T, 8), cols] = (a0[2][...] + a1[2][...]) * (mn * sc)
        edp[...] = edp[...] * dsilu
        @pl.when(i == 0)
        def _():
            dw_ref[...] = jnp.zeros(dw_ref.shape, F32)
            db_ref[...] = jnp.zeros(db_ref.shape, F32)

        x_main = ext[pl.ds(16, T), :]
        dx = None
        for o in range(CONVW):
            view = edp[pl.ds(10 - o, T), :]
            dx = w[o:o + 1, :] * view if dx is None else dx + w[o:o + 1, :] * view
            dw_ref[pl.ds(o, 1), :] += _colsum(x_main * view)
        dx_ref[...] = dx.astype(dx_ref.dtype)
        db_ref[...] += _colsum(edp[pl.ds(8, T), :])

    gspecs = _halo_specs(T, 8, MW, 0, S) * 4
    return _call(
        body, name="conv_bwd", grid=(n,),
        in_specs=_halo_specs(T, 16, W, 0, S) + gspecs + [pl.BlockSpec((8, W), lambda i: (0, 0)),
                                                          pl.BlockSpec((1, W), lambda i: (0, 0))],
        out_specs=[pl.BlockSpec((T, W), lambda i: (i, 0)), pl.BlockSpec((8, W), lambda i: (0, 0)),
                   pl.BlockSpec((1, W), lambda i: (0, 0))],
        out_shape=[jax.ShapeDtypeStruct((S, W), _MXU_DTYPE), jax.ShapeDtypeStruct((8, W), F32),
                   jax.ShapeDtypeStruct((1, W), F32)],
        scratch_shapes=[pltpu.VMEM((T + 32, W), F32), pltpu.VMEM((T + 16, W), F32)],
        compiler_params=_cparams(("arbitrary",)),
    )(proj, proj, proj, *([dqd[0]] * 3), *([dqd[1]] * 3), *([dkd[0]] * 3), *([dkd[1]] * 3), conv_w8, conv_b)


_ATT_SCALE = QK ** -0.5
_LOG2E = math.log2(math.e)
_Q_PRESCALE = _ATT_SCALE * _LOG2E


def _side_exchange(side, gather, grid, cols=()):
    ns = len(side)
    io = _exchange_io(side, gather, cols) if ns else dict(specs=[], out_shape=[], scratch=[])

    def wrap(refs_in, refs_out, sems):
        if not ns:
            return (lambda: None), (lambda: None)
        start, wait = _exchange_ops(refs_in, refs_out, *sems, gather=gather, cols=cols)
        ids = [pl.program_id(a) for a in range(len(grid))]
        first = functools.reduce(jnp.logical_and, [i == 0 for i in ids])
        last = functools.reduce(jnp.logical_and, [i == g - 1 for i, g in zip(ids, grid)])
        return (lambda: pl.when(first)(start)), (lambda: pl.when(last)(wait))

    return ns, io, wrap


def _attn_fwd(q, k, v, *, side=(), side_cols=(), tq=1024, split=4):
    S = q.shape[0]
    tq = min(tq, S)
    hq = tq // split
    grid = (H_MLA, S // tq)
    ns, io, wrap = _side_exchange(side, True, grid, side_cols)

    def body(q_ref, k_ref, v_ref, *rest):
        o_ref, qa_ref = rest[ns:ns + 2]
        side_start, side_wait = wrap(rest[:ns], rest[ns + 2:2 * ns + 2], rest[2 * ns + 2:])
        side_start()
        kv, vv = k_ref[...], v_ref[...]
        lane = lax.broadcasted_iota(jnp.int32, (hq, HP), 1)
        for a in range(split):
            r = pl.ds(a * hq, hq)
            qv = q_ref[r, :]
            s = _dot(qv, kv, "nt")
            m = jnp.max(s, axis=1, keepdims=True)
            acc = _dot(jnp.exp2(s - m), vv, "nn")
            l = acc[:, VD:VD + 1]
            o_ref[r, :] = (acc[:, :VD] / l).astype(o_ref.dtype)
            lse = m + jnp.log2(l)
            hi = lse.astype(_MXU_DTYPE).astype(F32)
            qa = jnp.where(lane == QK, -hi, jnp.where(lane == QK + 1, hi - lse, qv.astype(F32)))
            qa_ref[r, :] = qa.astype(qa_ref.dtype)
        side_wait()

    res = _call(
        body, name="attn_fwd", grid=grid,
        in_specs=[pl.BlockSpec((tq, HP), lambda h, i: (i, h)),
                  pl.BlockSpec((S, HP), lambda h, i: (0, h)),
                  pl.BlockSpec((S, HP), lambda h, i: (0, h))] + io["specs"],
        out_specs=[pl.BlockSpec((tq, VD), lambda h, i: (i, h)),
                   pl.BlockSpec((tq, HP), lambda h, i: (i, h))] + io["specs"],
        out_shape=[jax.ShapeDtypeStruct((S, H_MLA * VD), _MXU_DTYPE),
                   jax.ShapeDtypeStruct((S, H_MLA * HP), _MXU_DTYPE)] + io["out_shape"],
        scratch_shapes=io["scratch"],
        compiler_params=_cparams(("arbitrary", "arbitrary")),
    )(q, k, v, *side)
    return res[0], res[1], list(res[2:])


def _attn_bwd(qa, k, va, doa, *, side=(), side_cols=(), tq=4096, tk=1024, split=4, unroll=1):
    S = qa.shape[0]
    tq, tk = min(tq, S), min(tk, S)
    nq, nkb = S // tq, S // tk
    hq = tq // split
    grid = (H_MLA, nkb)
    ns, io, wrap = _side_exchange(side, False, grid, side_cols)

    def body(q_ref, k_ref, v_ref, do_ref, *rest):
        dq_ref, dk_ref, dv_ref = rest[ns:ns + 3]
        side_start, side_wait = wrap(rest[:ns], rest[ns + 3:2 * ns + 3], rest[2 * ns + 3:])
        side_start()
        j = pl.program_id(1)

        @pl.when(j == 0)
        def _():
            dq_ref[...] = jnp.zeros(dq_ref.shape, F32)

        dk_ref[...] = jnp.zeros(dk_ref.shape, F32)
        dv_ref[...] = jnp.zeros(dv_ref.shape, F32)
        kb, vb = k_ref[...], v_ref[...]

        def step(i, carry):
            for a in range(split):
                r = pl.ds(pl.multiple_of(i * tq + a * hq, hq), hq)
                qg, dog = q_ref[r, :], do_ref[r, :]
                p = jnp.exp2(_dot(qg, kb, "nt"))
                ds = (p * _dot(dog, vb, "nt")).astype(_MXU_DTYPE)
                dq_ref[r, :] += _dot(ds, kb, "nn")
                dv_ref[...] += _dot(p, dog, "tn")
                dk_ref[...] += _dot(ds, qg, "tn")
            return carry

        lax.fori_loop(0, nq, step, 0, unroll=unroll if nq % unroll == 0 else 1)
        dk_ref[...] = dk_ref[...] * (1.0 / _LOG2E)

        @pl.when(j == nkb - 1)
        def _():
            dq_ref[...] = dq_ref[...] * _ATT_SCALE

        side_wait()

    blk = pl.BlockSpec((tk, HP), lambda h, j: (j, h))
    whole = pl.BlockSpec((S, HP), lambda h, j: (0, h))
    res = _call(
        body, name="attn_bwd", grid=grid,
        in_specs=[whole, blk, blk, whole] + io["specs"],
        out_specs=[whole, blk, blk] + io["specs"],
        out_shape=[jax.ShapeDtypeStruct((S, H_MLA * HP), F32)] * 3 + io["out_shape"],
        scratch_shapes=io["scratch"],
        compiler_params=_cparams(("arbitrary", "arbitrary")),
    )(qa, k, va, doa, *side)
    return res[0], res[1], res[2], list(res[3:])


def _mlstm_chunk_terms(g, q, k, v, gates, gates_t, bg_row, C, n_row, m):
    L = LCH
    d = g // HM
    h = g % HM
    i_idx = d * 8 + h
    f_idx = d * 8 + 4 + h
    rr = lax.broadcasted_iota(jnp.int32, (L, L), 0)
    cc = lax.broadcasted_iota(jnp.int32, (L, L), 1)
    order = (rr - cc) * (1 - 2 * d)
    tri = order >= 0
    eye = rr == cc
    lane = lax.broadcasted_iota(jnp.int32, gates.shape, 1)
    sub = lax.broadcasted_iota(jnp.int32, gates_t.shape, 0)
    lane_b = lax.broadcasted_iota(jnp.int32, bg_row.shape, 1)
    pick_c = lambda idx: jnp.sum(jnp.where(lane == idx, gates, 0.0), axis=1, keepdims=True)
    pick_r = lambda idx: jnp.sum(jnp.where(sub == idx, gates_t, 0.0), axis=0, keepdims=True)
    pick_b = lambda idx: jnp.sum(jnp.where(lane_b == idx, bg_row, 0.0), axis=1, keepdims=True)
    i_col, i_row = pick_c(i_idx) + pick_b(i_idx), pick_r(i_idx) + pick_b(i_idx)
    f_col, f_row = pick_c(f_idx) + pick_b(f_idx), pick_r(f_idx) + pick_b(f_idx)
    logsig = lambda x: jnp.minimum(x, 0.0) - jnp.log(1.0 + jnp.exp(-jnp.abs(x)))
    lf_col, lf_row = logsig(f_col), logsig(f_row)
    b_col = jnp.sum(jnp.where(tri, lf_row, 0.0), axis=1, keepdims=True)
    tri_t = order <= 0
    b_row = jnp.sum(jnp.where(tri_t, lf_col, 0.0), axis=0, keepdims=True)
    bL = jnp.sum(lf_row, axis=1, keepdims=True)
    log_inter = b_col + m
    logD = jnp.where(tri, b_col - b_row + i_row, -jnp.inf)
    m_t = jnp.maximum(log_inter, jnp.max(logD, axis=1, keepdims=True))
    Dm = jnp.exp(logD - m_t)
    w_inter = jnp.exp(log_inter - m_t)
    A = _dot(q, k, "nt")
    Sc = A * Dm
    numI = _dot(q, C, "nt")
    qf = q.astype(F32)
    kf = k.astype(F32)
    denI = jnp.sum(qf * n_row, axis=1, keepdims=True)
    num = _dot(Sc, v, "nn") + w_inter * numI
    den = jnp.sum(Sc, axis=1, keepdims=True) + w_inter * denI
    floor = jnp.exp(-m_t)
    Nst = jnp.maximum(jnp.abs(den), floor)
    log_w = bL - b_col + i_col
    m_new = jnp.maximum(bL + m, jnp.max(log_w, axis=0, keepdims=True))
    decay = jnp.exp(bL + m - m_new)
    w_col = jnp.exp(log_w - m_new)
    return dict(tri=tri, eye=eye, f_row=f_row, Dm=Dm, w_inter=w_inter, A=A, Sc=Sc, numI=numI, denI=denI,
                num=num, den=den, floor=floor, Nst=Nst, m_new=m_new, decay=decay, w_col=w_col, qf=qf, kf=kf)


def _mlstm_specs(nc, d, step_of):
    chunk = lambda j: step_of(j) if d == 0 else nc - 1 - step_of(j)
    return chunk, [
        pl.BlockSpec((LCH, DM), lambda h, j: (chunk(j), h)),
        pl.BlockSpec((LCH, DM), lambda h, j: (chunk(j), h)),
        pl.BlockSpec((LCH, DM), lambda h, j: (chunk(j), P_VM // DM + h)),
        pl.BlockSpec((LCH, 128), lambda h, j: (chunk(j), P_G // 128)),
        pl.BlockSpec((NG, LCH), lambda h, j: (0, chunk(j))),
    ]


def _mlstm_fwd(qc, kc, proj, gates_t, bg_row):
    S = qc.shape[0]
    nc = S // LCH
    in_specs, out_specs = [], []
    for d in (0, 1):
        chunk, specs = _mlstm_specs(nc, d, lambda j: j)
        in_specs += specs
        out_specs += [pl.BlockSpec((LCH, DM), lambda h, j, chunk=chunk: (chunk(j), h)),
                      pl.BlockSpec((None, None, DM, DM), lambda h, j, chunk=chunk: (h, chunk(j), 0, 0)),
                      pl.BlockSpec((None, None, 8, DM), lambda h, j, chunk=chunk: (h, chunk(j), 0, 0))]
    in_specs.append(pl.BlockSpec((1, 128), lambda h, j: (0, 0)))

    def body(*refs):
        bg_ref, outs, (C_s, n_s, m_s) = refs[10], refs[11:17], refs[17:]

        @pl.when(pl.program_id(1) == 0)
        def _():
            C_s[...] = jnp.zeros(C_s.shape, F32)
            n_s[...] = jnp.zeros(n_s.shape, F32)
            m_s[...] = jnp.full(m_s.shape, M_INIT, F32)

        for d in (0, 1):
            q_ref, k_ref, v_ref, g_ref, gt_ref = refs[5 * d:5 * d + 5]
            h_ref, cst_ref, nm_ref = outs[3 * d:3 * d + 3]
            g = d * HM + pl.program_id(0)
            C, n_row, m = C_s[d], n_s[d, 0:1, :], m_s[d, 0:1, 0:1]
            cst_ref[...] = C
            nm_ref[0:1, :] = n_row
            nm_ref[1:2, :] = jnp.broadcast_to(m, (1, DM))
            nm_ref[2:8, :] = jnp.zeros((6, DM), F32)
            q, k, v = q_ref[...], k_ref[...], v_ref[...]
            t = _mlstm_chunk_terms(g, q, k, v, g_ref[...], gt_ref[...], bg_ref[...], C, n_row, m)
            h_ref[...] = t["num"] / t["Nst"]
            wv = t["w_col"] * v
            C_s[d] = t["decay"] * C + _dot(wv, k, "tn")
            n_s[d, 0:1, :] = t["decay"] * n_row + _colsum(t["w_col"] * t["kf"])
            m_s[d] = jnp.broadcast_to(t["m_new"], (8, 128))

    res = _call(
        body, name="mlstm_fwd", grid=(HM, nc), in_specs=in_specs, out_specs=out_specs,
        out_shape=[jax.ShapeDtypeStruct((S, MW), F32), jax.ShapeDtypeStruct((HM, nc, DM, DM), F32),
                   jax.ShapeDtypeStruct((HM, nc, 8, DM), F32)] * 2,
        scratch_shapes=[pltpu.VMEM((2, DM, DM), F32), pltpu.VMEM((2, 8, DM), F32), pltpu.VMEM((2, 8, 128), F32)],
        compiler_params=_cparams(("parallel", "arbitrary")),
    )(*([qc, kc, proj, proj, gates_t] * 2), bg_row)
    return (res[0], res[3]), (res[1], res[4]), (res[2], res[5])


def _mlstm_bwd(qc, kc, proj, gates_t, bg_row, dh, cst, nm):
    S = qc.shape[0]
    nc = S // LCH
    in_specs, out_specs = [], []
    for d in (0, 1):
        chunk, specs = _mlstm_specs(nc, d, lambda j: nc - 1 - j)
        in_specs += specs + [pl.BlockSpec((LCH, DM), lambda h, j, chunk=chunk: (chunk(j), h)),
                             pl.BlockSpec((None, None, DM, DM), lambda h, j, chunk=chunk: (h, chunk(j), 0, 0)),
                             pl.BlockSpec((None, None, 8, DM), lambda h, j, chunk=chunk: (h, chunk(j), 0, 0))]
        out_specs += [pl.BlockSpec((LCH, DM), lambda h, j, chunk=chunk: (chunk(j), h))] * 3
        out_specs += [pl.BlockSpec((None, None, 8, LCH), lambda h, j, chunk=chunk: (h, chunk(j), 0, 0))]
    in_specs.append(pl.BlockSpec((1, 128), lambda h, j: (0, 0)))

    def body(*refs):
        bg_ref, outs, (dC_s, dn_s) = refs[16], refs[17:25], refs[25:]

        @pl.when(pl.program_id(1) == 0)
        def _():
            dC_s[...] = jnp.zeros(dC_s.shape, F32)
            dn_s[...] = jnp.zeros(dn_s.shape, F32)

        for d in (0, 1):
            _mlstm_bwd_chain(d, refs[8 * d:8 * d + 8], bg_ref, outs[4 * d:4 * d + 4], dC_s, dn_s)

    res = _call(
        body, name="mlstm_bwd", grid=(HM, nc), in_specs=in_specs, out_specs=out_specs,
        out_shape=([jax.ShapeDtypeStruct((S, MW), F32)] * 3 + [jax.ShapeDtypeStruct((HM, nc, 8, LCH), F32)]) * 2,
        scratch_shapes=[pltpu.VMEM((2, DM, DM), F32), pltpu.VMEM((2, 8, DM), F32)],
        compiler_params=_cparams(("parallel", "arbitrary")),
    )(*[a for d in (0, 1) for a in (qc, kc, proj, proj, gates_t, dh, cst[d], nm[d])], bg_row)
    return (res[0], res[4]), (res[1], res[5]), (res[2], res[6]), (res[3], res[7])


def _mlstm_bwd_chain(d, ins, bg_ref, outs, dC_s, dn_s):
        q_ref, k_ref, v_ref, g_ref, gt_ref, dh_ref, cst_ref, nm_ref = ins
        dq_ref, dk_ref, dv_ref, dg_ref = outs
        g = d * HM + pl.program_id(0)
        C, n_row, m = cst_ref[...], nm_ref[0:1, :], nm_ref[1:2, 0:1]
        q, k, v = q_ref[...], k_ref[...], v_ref[...]
        t = _mlstm_chunk_terms(g, q, k, v, g_ref[...], gt_ref[...], bg_ref[...], C, n_row, m)
        tri, eye, qf, kf = t["tri"], t["eye"], t["qf"], t["kf"]
        w_inter, w_col, decay, Nst = t["w_inter"], t["w_col"], t["decay"], t["Nst"]
        dC, dn = dC_s[d], dn_s[d, 0:1, :]
        dhv = dh_ref[...]
        hval = t["num"] / Nst
        dnum = dhv / Nst
        dNst = -jnp.sum(dhv * hval, axis=1, keepdims=True) / Nst
        dden = jnp.where(jnp.abs(t["den"]) > t["floor"], jnp.sign(t["den"]) * dNst, 0.0)
        dSc = _dot(dnum, v, "nt") + dden
        dA = dSc * t["Dm"]
        G = dSc * t["Sc"]
        KdC = _dot(k, dC, "nt")
        dq = _dot(dA, k, "nn") + w_inter * _dot(dnum, C, "nn") + (w_inter * dden) * n_row
        dk = _dot(dA, q, "tn") + w_col * _dot(v, dC, "nn") + w_col * dn
        dv = _dot(t["Sc"], dnum, "tn") + w_col * KdC
        dq_ref[...] = dq
        dk_ref[...] = dk
        dv_ref[...] = dv
        dlog_inter = w_inter * (jnp.sum(dnum * t["numI"], axis=1, keepdims=True) + dden * t["denI"])
        rowG = jnp.sum(G, axis=1, keepdims=True)
        colG = jnp.sum(G, axis=0, keepdims=True)
        u_col = w_col * (jnp.sum(v * KdC, axis=1, keepdims=True) + jnp.sum(kf * dn, axis=1, keepdims=True))
        colG_c = jnp.sum(jnp.where(eye, colG, 0.0), axis=1, keepdims=True)
        u_row = jnp.sum(jnp.where(eye, u_col, 0.0), axis=0, keepdims=True)
        db_col = rowG + dlog_inter - u_col - colG_c
        dbL = jnp.sum(u_col, axis=0, keepdims=True) + decay * (
            jnp.sum(jnp.sum(dC * C, axis=1, keepdims=True), axis=0, keepdims=True)
            + jnp.sum(dn * n_row, axis=1, keepdims=True))
        dlf_row = jnp.sum(jnp.where(tri, db_col, 0.0), axis=0, keepdims=True) + dbL
        di_row = colG + u_row
        df_row = dlf_row * (1.0 - _sigmoid(t["f_row"]))
        dg_ref[...] = jnp.zeros(dg_ref.shape, F32)
        dg_ref[0:1, :] = di_row
        dg_ref[1:2, :] = df_row
        dC_s[d] = decay * dC + _dot(w_inter * dnum, q, "tn")
        dn_s[d, 0:1, :] = decay * dn + _colsum((w_inter * dden) * qf)


def _pad_w_in(w):
    cq, ckv, kpe, qm, km, vm, om, gt = _split_in(w)
    z = lambda n: jnp.zeros((w.shape[0], n), w.dtype)
    return jnp.concatenate([qm, km, vm, om, cq, ckv, kpe, z(HP - QK), gt, z(128 - NG)], axis=1)


def _split_in(w):
    out, o = [], 0
    for n in IN_SIZES:
        out.append(w[:, o:o + n])
        o += n
    return out


def _unpad_w_in(g):
    return jnp.concatenate([g[:, P_CQ:P_CQ + Q_LORA], g[:, P_CKV:P_CKV + KV_LORA], g[:, P_KPE:P_KPE + ROPE],
                            g[:, 0:4 * MW], g[:, P_G:P_G + NG]], axis=1)


_IN_SHARD = D_IN // 4
_IN_SEGMENTS = ((0, 512, P_CQ), (512, 768, P_CKV), (768, 832, P_KPE), (832, 4928, P_QM), (4928, 4944, P_G))


def _pad_w_in_slabs(slabs):
    def orig(a, b):
        out = []
        for k in range(4):
            lo, hi = max(a, k * _IN_SHARD), min(b, (k + 1) * _IN_SHARD)
            if lo < hi:
                out.append(slabs[k][:, lo - k * _IN_SHARD:hi - k * _IN_SHARD])
        return out

    z = lambda n: jnp.zeros((slabs.shape[1], n), slabs.dtype)
    return jnp.concatenate(orig(832, 4928) + orig(0, 512) + orig(512, 768) + orig(768, 832) + [z(HP - QK)]
                           + orig(4928, 4944) + [z(128 - NG)], axis=1)


def _unpad_w_in_slabs(g):
    slabs = []
    for k in range(4):
        pieces = []
        for a, b, p in _IN_SEGMENTS:
            lo, hi = max(a, k * _IN_SHARD), min(b, (k + 1) * _IN_SHARD)
            if lo < hi:
                pieces.append(g[:, p + lo - a:p + hi - a])
        slabs.append(jnp.concatenate(pieces, axis=1))
    return jnp.stack(slabs)


def _pad_w_uq(w):
    return jnp.pad(w.reshape(Q_LORA, H_MLA, QK), ((0, 0), (0, 0), (0, HP - QK))).reshape(Q_LORA, H_MLA * HP)


def _unpad_w_uq(g):
    return g.reshape(Q_LORA, H_MLA, HP)[:, :, :QK].reshape(Q_LORA, H_MLA * QK)


def _perm_w_ukv(w):
    return w.reshape(KV_LORA, H_MLA, 2, NOPE).transpose(0, 2, 1, 3).reshape(KV_LORA, 2 * H_MLA * NOPE)


def _unperm_w_ukv(g):
    return g.reshape(KV_LORA, 2, H_MLA, NOPE).transpose(0, 2, 1, 3).reshape(KV_LORA, 2 * H_MLA * NOPE)


def _rope_tables(positions):
    half = ROPE // 2
    freqs = ROPE_THETA ** (-jnp.arange(half, dtype=F32) / half)
    ang = positions.astype(F32)[:, None] * freqs
    cos, sin = jnp.cos(ang), jnp.sin(ang)
    z32, z64 = jnp.zeros_like(cos), jnp.zeros((cos.shape[0], 64), F32)
    return (jnp.concatenate([cos, cos, z64], axis=1), jnp.concatenate([z32, sin, z64], axis=1),
            jnp.concatenate([-sin, z32, z64], axis=1))


def _device_step(x, tgt, positions, modv, W, late=None):
    S = x.shape[0]
    MX = _MXU_DTYPE
    cosp, rs1, rs2 = _rope_tables(positions)
    tabs = [(cosp, 128, 0), (rs1, 128, 0), (rs2, 128, 0)]
    cat1 = lambda vs: jnp.concatenate(vs, axis=1)
    hsl = lambda hh, w: slice(hh * w, (hh + 1) * w)

    def ln1(xv, g, mv):
        xhat, _ = _rms(xv, D)
        return [xhat * g * (1.0 + mv[1:2]) + mv[0:1]], []

    (h,) = _rowmap(ln1, [(x, D, 0)], [W["g_mix"], modv], [(D, MX)], tile=512, name="ln1")
    proj = _mm(h, W["w_in"], "nn", name="proj")

    def lora(cq, ckv, gq, gkv):
        return [_rms(cq, Q_LORA)[0] * gq, _rms(ckv, KV_LORA)[0] * gkv], []

    cqn, ckvn = _rowmap(lora, [(proj, Q_LORA, P_CQ // Q_LORA), (proj, KV_LORA, P_CKV // KV_LORA)],
                        [W["g_qlora"], W["g_kvlora"]], [(Q_LORA, MX), (KV_LORA, MX)], tile=512, name="lora_norm")
    q_raw = _mm(cqn, W["w_uq"], "nn", name="q_up")
    kv_raw = _mm(ckvn, W["w_ukv"], "nn", name="kv_up")

    def mla_q(qr, cp, a1, a2, gq):
        outs = []
        for hh in range(H_MLA):
            y = _rms(qr[:, hsl(hh, HP)], QK)[0] * gq
            outs += [y[:, :NOPE], _rope_fwd(y[:, NOPE:], cp, a1, a2)]
        return [cat1(outs) * _Q_PRESCALE], []

    (qh,) = _rowmap(mla_q, [(q_raw, H_MLA * HP, 0)] + tabs, [W["gq"]], [(H_MLA * HP, MX)], tile=512, name="mla_q")

    def mla_k(kvr, kpe, cp, a1, a2, gk):
        lane = lax.broadcasted_iota(jnp.int32, (kvr.shape[0], 128), 1)
        outs, vas = [], []
        for hh in range(H_MLA):
            y = _rms(cat1([kvr[:, hsl(hh, NOPE)], kpe]), QK)[0] * gk
            outs += [y[:, :NOPE], _rope_fwd(y[:, NOPE:], cp, a1, a2) + ((lane == ROPE) | (lane == ROPE + 1)).astype(F32)]
            vas += [kvr[:, H_MLA * NOPE + hh * VD:H_MLA * NOPE + (hh + 1) * VD], (lane < 2).astype(F32)]
        return [cat1(outs), cat1(vas)], []

    kh, va = _rowmap(mla_k, [(kv_raw, 2 * H_MLA * NOPE, 0), (proj, 128, P_KPE // 128)] + tabs, [W["gk"]],
                     [(H_MLA * HP, MX), (H_MLA * HP, MX)], tile=512, name="mla_k")
    attn_o, qa, gathered = _attn_fwd(qh, kh, va, side=late or (), side_cols=(1,))
    if late:
        W = dict(W, w_out=gathered[0].reshape(D, D), w_ff1=gathered[1], w_ff2=gathered[2].reshape(DFF, D))

    qc, kc = _conv_fwd(proj, W["conv_w8"], W["conv_b"])
    gates_t = proj[:, P_G:P_G + NG].T
    (h_f, h_b), cst, nm = _mlstm_fwd(qc, kc, proj, gates_t, W["bg_row"])
    hrows = [(h_f, MW, 0), (h_b, MW, 0), (proj, MW, P_OM // MW)]

    def ml_out(ao, hf, hb, om, gmn):
        outs = [ao.astype(F32)]
        hs = hf + hb
        for hh in range(HM):
            sl = hsl(hh, DM)
            outs.append(_sigmoid(om[:, sl]) * _rms(hs[:, sl], DM)[0] * gmn[:, sl])
        return [cat1(outs)], []

    (cat,) = _rowmap(ml_out, [(attn_o, MW, 0)] + hrows, [W["g_mn"]], [(D, MX)], tile=512, name="ml_out")
    mixed = _mm(cat, W["w_out"], "nn", name="out_proj")

    def res_ln2(xv, mx, g, mv):
        x1 = xv + mv[2:3] * mx
        return [x1, _rms(x1, D)[0] * g * (1.0 + mv[4:5]) + mv[3:4]], []

    x1, h2 = _rowmap(res_ln2, [(x, D, 0), (mixed, D, 0)], [W["g_mlp"], modv], [(D, F32), (D, MX)],
                     tile=512, name="res_ln2")
    a, u = _mm(h2, W["w_ff1"], "nn", name="ff1", out_dtypes=(MX, MX),
               epilogue=lambda r: (jnp.square(jnp.maximum(r, 0.0)), r))
    y = _mm(a, W["w_ff2"], "nn", name="ff2")

    def final(x1v, yv, tv, mv):
        err = x1v + mv[5:6] * yv - tv
        dout = err * (1.0 / D)
        loss = jnp.sum(jnp.sum(0.5 * err * dout, axis=1, keepdims=True), axis=0, keepdims=True)
        return [dout, mv[5:6] * dout], [loss, _colsum(dout * yv)]

    dout, dy, loss, dgate2 = _rowmap(final, [(x1, D, 0), (y, D, 0), (tgt, D, 0)], [modv], [(D, F32), (D, MX)],
                                     [(1, 1), (1, D)], tile=256, name="loss_head")

    du = _mm(dy, W["w_ff2"], "nt", name="ff2_dx", out_dtypes=(MX,), extras=(u,),
             epilogue=lambda r, uv: (r * (2.0 * jnp.maximum(uv.astype(F32), 0.0)),))
    gdt = (MX,)
    g_ff2 = _mm(a, dy, "tn", name="ff2_dw", out_dtypes=gdt)
    dh2 = _mm(du, W["w_ff1"], "nt", name="ff1_dx")
    g_ff1 = _mm(h2, du, "tn", name="ff1_dw", out_dtypes=gdt)

    def ln2_bwd(dh2v, x1v, doutv, mxv, g, mv):
        xhat, r = _rms(x1v, D)
        dn2 = dh2v * (1.0 + mv[4:5])
        dx1 = doutv + _rms_bwd(dn2 * g, xhat, r, D)
        return [dx1, mv[2:3] * dx1], [_colsum(dh2v), _colsum(dh2v * xhat * g), _colsum(dn2 * xhat), _colsum(dx1 * mxv)]

    dx1, dmixed, dshift2, dscale2, dg_mlp, dgate1 = _rowmap(
        ln2_bwd, [(dh2, D, 0), (x1, D, 0), (dout, D, 0), (mixed, D, 0)], [W["g_mlp"], modv],
        [(D, F32), (D, MX)], [(1, D)] * 4, tile=256, name="ln2_bwd")
    dcat = _mm(dmixed, W["w_out"], "nt", name="out_dx")
    g_out = _mm(cat, dmixed, "tn", name="out_dw", out_dtypes=gdt)

    def ml_out_bwd(dml, hf, hb, om, gmn):
        hs = hf + hb
        dhs, dos, dgs = [], [], []
        for hh in range(HM):
            sl = hsl(hh, DM)
            xhat, r = _rms(hs[:, sl], DM)
            g, sg, d = gmn[:, sl], _sigmoid(om[:, sl]), dml[:, sl]
            dos.append(d * xhat * g * sg * (1.0 - sg))
            dhn = d * sg
            dgs.append(_colsum(dhn * xhat))
            dhs.append(_rms_bwd(dhn * g, xhat, r, DM))
        return [cat1(dhs), cat1(dos)], [cat1(dgs)]

    dhs, do_m, dg_mn = _rowmap(ml_out_bwd, [(dcat, MW, 1)] + hrows, [W["g_mn"]], [(MW, F32), (MW, MX)],
                               [(1, MW)], tile=512, name="ml_out_bwd")
    dqd, dkd, dvd, dgates = _mlstm_bwd(qc, kc, proj, gates_t, W["bg_row"], dhs, cst, nm)
    dqk_m, dconv_w8, dconv_b = _conv_bwd(proj, dqd, dkd, W["conv_w8"], W["conv_b"])

    def do_aug(ao, dov):
        lane = lax.broadcasted_iota(jnp.int32, (ao.shape[0], 128), 1)
        outs = []
        for hh in range(H_MLA):
            sl = hsl(hh, VD)
            dl = jnp.sum(ao[:, sl].astype(F32) * dov[:, sl], axis=1, keepdims=True)
            hi = dl.astype(MX).astype(F32)
            outs += [dov[:, sl], jnp.where(lane == 0, -hi, jnp.where(lane == 1, hi - dl, 0.0))]
        return [cat1(outs)], []

    (doa,) = _rowmap(do_aug, [(attn_o, MW, 0), (dcat, MW, 0)], [], [(H_MLA * HP, MX)], tile=512, name="attn_delta")
    side = [g_out.reshape(N_CHIP, D // N_CHIP, D), g_ff1, g_ff2.reshape(N_CHIP, DFF // N_CHIP, D)] if late else ()
    dq_a, dk_a, dv_a, late_got = _attn_bwd(qa, kh, va, doa, side=side, side_cols=(1,))

    def mla_q_bwd(dqv, qr, cp, a1, a2, gq):
        outs, dg = [], 0.0
        for hh in range(H_MLA):
            sl = hsl(hh, HP)
            xhat, r = _rms(qr[:, sl], QK)
            d = dqv[:, sl]
            dyv = cat1([d[:, :NOPE], _rope_bwd(d[:, NOPE:], cp, a1, a2)])
            dg = dg + _colsum(dyv * xhat)
            outs.append(_rms_bwd(dyv * gq, xhat, r, QK))
        return [cat1(outs)], [dg]

    dq_raw, dgq = _rowmap(mla_q_bwd, [(dq_a, H_MLA * HP, 0), (q_raw, H_MLA * HP, 0)] + tabs, [W["gq"]],
                          [(H_MLA * HP, MX)], [(1, HP)], tile=512, name="mla_q_bwd")
    dcqn = _mm(dq_raw, W["w_uq"], "nt", name="q_up_dx")
    g_uq = _mm(cqn, dq_raw, "tn", name="q_up_dw", out_dtypes=gdt)

    def mla_k_bwd(dkv, dvv, kvr, kpe, cp, a1, a2, gk):
        dkn, dg, dkpe = [], 0.0, 0.0
        for hh in range(H_MLA):
            xhat, r = _rms(cat1([kvr[:, hsl(hh, NOPE)], kpe]), QK)
            d = dkv[:, hsl(hh, HP)]
            dyv = cat1([d[:, :NOPE], _rope_bwd(d[:, NOPE:], cp, a1, a2)])
            dg = dg + _colsum(dyv * xhat)
            dxv = _rms_bwd(dyv * gk, xhat, r, QK)
            dkn.append(dxv[:, :NOPE])
            dkpe = dkpe + dxv[:, NOPE:]
        return [cat1(dkn + [dvv[:, hh * HP:hh * HP + VD] for hh in range(H_MLA)]), dkpe], [dg]

    dkv_raw, dkpe, dgk = _rowmap(
        mla_k_bwd, [(dk_a, H_MLA * HP, 0), (dv_a, H_MLA * HP, 0), (kv_raw, 2 * H_MLA * NOPE, 0),
                    (proj, 128, P_KPE // 128)] + tabs, [W["gk"]],
        [(2 * H_MLA * NOPE, MX), (128, MX)], [(1, HP)], tile=256, name="mla_k_bwd")
    dckvn = _mm(dkv_raw, W["w_ukv"], "nt", name="kv_up_dx")
    g_ukv = _mm(ckvn, dkv_raw, "tn", name="kv_up_dw", out_dtypes=gdt)

    def lora_bwd(dcq, dckv, cq, ckv, gq, gkv):
        xq, rq = _rms(cq, Q_LORA)
        xk, rk = _rms(ckv, KV_LORA)
        return ([_rms_bwd(dcq * gq, xq, rq, Q_LORA), _rms_bwd(dckv * gkv, xk, rk, KV_LORA)],
                [_colsum(dcq * xq), _colsum(dckv * xk)])

    dc_q, dc_kv, dg_qlora, dg_kvlora = _rowmap(
        lora_bwd, [(dcqn, Q_LORA, 0), (dckvn, KV_LORA, 0), (proj, Q_LORA, P_CQ // Q_LORA),
                   (proj, KV_LORA, P_CKV // KV_LORA)], [W["g_qlora"], W["g_kvlora"]],
        [(Q_LORA, MX), (KV_LORA, MX)], [(1, Q_LORA), (1, KV_LORA)], tile=512, name="lora_bwd")

    nc = S // LCH
    dg16 = jnp.stack(dgates)[:, :, :, 0:2, :].transpose(2, 4, 0, 3, 1).reshape(S, NG)
    dg128 = jnp.pad(dg16, ((0, 0), (0, 128 - NG)))

    def assemble(dqk, dv0, dv1, dom, dcq, dckv, dkp, dgp):
        f = lambda t: t.astype(F32)
        return [cat1([f(dqk), dv0 + dv1, f(dom), f(dcq), f(dckv), f(dkp), dgp])], [_colsum(dgp)]

    dproj, dbg = _rowmap(
        assemble, [(dqk_m, 2 * MW, 0), (dvd[0], MW, 0), (dvd[1], MW, 0), (do_m, MW, 0), (dc_q, Q_LORA, 0),
                   (dc_kv, KV_LORA, 0), (dkpe, 128, 0), (dg128, 128, 0)], [], [(D_INP, MX)], [(1, 128)],
        tile=256, name="dproj")
    g_in = _mm(h, dproj, "tn", name="proj_dw", out_dtypes=gdt)
    early_got = ()
    if late:
        side = [_unpad_w_in_slabs(g_in).astype(MX), _slabs(_unpad_w_uq(g_uq)).astype(MX),
                _slabs(_unperm_w_ukv(g_ukv)).astype(MX)]
        dh, early_got = _mm(dproj, W["w_in"], "nt", name="proj_dx", side=side)
    else:
        dh = _mm(dproj, W["w_in"], "nt", name="proj_dx")

    def ln1_bwd(dhv, xv, dx1v, g, mv):
        xhat, r = _rms(xv, D)
        dn = dhv * (1.0 + mv[1:2])
        return [dx1v + _rms_bwd(dn * g, xhat, r, D)], [_colsum(dhv), _colsum(dhv * xhat * g), _colsum(dn * xhat)]

    gx, dshift1, dscale1, dg_mix = _rowmap(ln1_bwd, [(dh, D, 0), (x, D, 0), (dx1, D, 0)], [W["g_mix"], modv],
                                           [(D, F32)], [(1, D)] * 3, tile=256, name="ln1_bwd")
    dmodv = jnp.concatenate([dshift1, dscale1, dgate1, dshift2, dscale2, dgate2], axis=0)
    grads = dict(w_in=g_in, w_uq=g_uq, w_ukv=g_ukv, w_out=g_out, w_ff1=g_ff1, w_ff2=g_ff2,
                 norm_mix_g=dg_mix, b_gates=dbg[:, :NG], conv_w=dconv_w8[:CONVW], conv_b=dconv_b,
                 q_lora_g=dg_qlora, kv_lora_g=dg_kvlora, q_norm_g=dgq[:, :QK], k_norm_g=dgk[:, :QK],
                 mlstm_norm_g=dg_mn, norm_mlp_g=dg_mlp)
    grads["got"] = list(early_got) + list(late_got)
    return loss, gx, dmodv, grads


def _cols(g):
    return g.transpose(1, 0, 2).reshape(g.shape[1], N_CHIP * g.shape[2])


def _slabs(gfull):
    return gfull.reshape(gfull.shape[0], N_CHIP, -1).transpose(1, 0, 2)


def _prep_weights(w_in, w_uq, w_ukv, w_out, w_ff1, w_ff2, norm_mix_g, norm_mlp_g, q_lora_g, kv_lora_g,
                  q_norm_g, k_norm_g, mlstm_norm_g, conv_w, conv_b, b_gates):
    MX = _MXU_DTYPE
    padg = lambda g: jnp.pad(g.reshape(1, QK).astype(F32), ((0, 0), (0, HP - QK)))
    return dict(
        w_in=(_pad_w_in_slabs(w_in) if w_in.ndim == 3 else _pad_w_in(w_in)).astype(MX), w_uq=_pad_w_uq(w_uq).astype(MX), w_ukv=_perm_w_ukv(w_ukv).astype(MX),
        w_out=None if w_out is None else w_out.astype(MX), w_ff1=None if w_ff1 is None else w_ff1.astype(MX),
        w_ff2=None if w_ff2 is None else w_ff2.astype(MX),
        g_mix=norm_mix_g.reshape(1, D), g_mlp=norm_mlp_g.reshape(1, D), g_qlora=q_lora_g.reshape(1, Q_LORA),
        g_kvlora=kv_lora_g.reshape(1, KV_LORA), gq=padg(q_norm_g), gk=padg(k_norm_g),
        g_mn=mlstm_norm_g.reshape(1, MW), conv_w8=jnp.pad(conv_w.reshape(CONVW, 2 * MW), ((0, 8 - CONVW), (0, 0))),
        conv_b=conv_b.reshape(1, 2 * MW), bg_row=jnp.pad(b_gates.reshape(1, NG), ((0, 0), (0, 128 - NG))))


MESH = pl.DeviceIdType.MESH
N_DEV = 8
N_CHIP = 4


def _comm_call(body, **kw):
    if _INTERPRET:
        kw["interpret"] = pltpu.InterpretParams()
    return pl.pallas_call(body, **kw)


def _allgather8(blk, *, name):
    m_per, n = blk.shape

    def body(x_ref, out_ref, send_sems, recv_sems, local_sem):
        x, y, c = lax.axis_index("x"), lax.axis_index("y"), lax.axis_index("c")
        me, sibling = (x, y, c), (x, y, 1 - c)
        chips = [(1 - x, y), (x, 1 - y), (1 - x, 1 - y)]

        def rows(px, py, pc):
            return out_ref.at[pl.ds((4 * px + 2 * py + pc) * m_per, m_per), :]

        def copy(k, block, to, src=None):
            return pltpu.make_async_remote_copy(
                src_ref=rows(*block) if src is None else src, dst_ref=rows(*block),
                send_sem=send_sems.at[k], recv_sem=recv_sems.at[k], device_id=to, device_id_type=MESH)

        mine = pltpu.make_async_copy(x_ref, rows(*me), local_sem)
        mine.start()
        first = [copy(0, me, sibling, src=x_ref)]
        first += [copy(1 + j, me, (*chip, c), src=x_ref) for j, chip in enumerate(chips)]
        for cp in first:
            cp.start()
        passed = [copy(4 + j, (*chip, c), sibling) for j, chip in enumerate(chips)]
        for j, chip in enumerate(chips):
            copy(1 + j, (*chip, c), me).wait_recv()
            passed[j].start()
        copy(0, sibling, me).wait_recv()
        for j, chip in enumerate(chips):
            copy(4 + j, (*chip, 1 - c), me).wait_recv()
        for cp in first + passed:
            cp.wait_send()
        mine.wait()

    return _comm_call(
        body, name=name, out_shape=jax.ShapeDtypeStruct((N_DEV * m_per, n), blk.dtype),
        in_specs=[pl.BlockSpec(memory_space=pltpu.VMEM)], out_specs=pl.BlockSpec(memory_space=pltpu.VMEM),
        scratch_shapes=[pltpu.SemaphoreType.DMA((7,)), pltpu.SemaphoreType.DMA((7,)), pltpu.SemaphoreType.DMA],
    )(blk)


def _chip_exchange(arrays, *, gather, name):
    n = len(arrays)

    def body(*refs):
        start, wait = _exchange_ops(refs[:n], refs[n:2 * n], *refs[2 * n:], gather=gather)
        start()
        wait()

    io = _exchange_io(arrays, gather)
    return _comm_call(body, name=name, out_shape=io["out_shape"], in_specs=io["specs"], out_specs=io["specs"],
                      scratch_shapes=io["scratch"])(*arrays)


def _exchange_io(arrays, gather, cols=()):
    n = len(arrays)

    def out(i, a):
        if gather:
            return (a.shape[0], N_CHIP * a.shape[1]) if i in cols else (N_CHIP, *a.shape)
        return (N_CHIP, a.shape[0], a.shape[1] // N_CHIP) if i in cols else a.shape

    return dict(
        specs=[pl.BlockSpec(memory_space=pltpu.HBM)] * n,
        out_shape=[jax.ShapeDtypeStruct(out(i, a), a.dtype) for i, a in enumerate(arrays)],
        scratch=[pltpu.SemaphoreType.DMA((3 * n,)), pltpu.SemaphoreType.DMA((3 * n,)), pltpu.SemaphoreType.DMA((n,))])


def _exchange_ops(ins, outs, send_sems, recv_sems, local_sems, *, gather, cols=()):
    n = len(ins)
    x, y, c = lax.axis_index("x"), lax.axis_index("y"), lax.axis_index("c")
    k = 2 * x + y
    chips = [(1 - x, y), (x, 1 - y), (1 - x, 1 - y)]

    def piece(ref, a, chip, windowed):
        if not windowed:
            return ref.at[chip]
        width = ref.shape[1] // N_CHIP
        return ref.at[:, pl.ds(pl.multiple_of(chip * width, 128), width)]

    src_of = lambda a, chip: ins[a] if gather else piece(ins[a], a, chip, a in cols)
    dst_of = lambda a, chip: piece(outs[a], a, chip, gather and a in cols)

    def remote(a, j):
        px, py = chips[j]
        return pltpu.make_async_remote_copy(
            src_ref=src_of(a, 2 * px + py), dst_ref=dst_of(a, k), send_sem=send_sems.at[3 * a + j],
            recv_sem=recv_sems.at[3 * a + j], device_id=(px, py, c), device_id_type=MESH)

    def arrival(a, j):
        px, py = chips[j]
        return pltpu.make_async_remote_copy(
            src_ref=src_of(a, k), dst_ref=dst_of(a, 2 * px + py), send_sem=send_sems.at[3 * a + j],
            recv_sem=recv_sems.at[3 * a + j], device_id=(px, py, c), device_id_type=MESH)

    local = [pltpu.make_async_copy(src_of(a, k), dst_of(a, k), local_sems.at[a]) for a in range(n)]
    sent = [remote(a, j) for a in range(n) for j in range(3)]

    def start():
        for cp in local + sent:
            cp.start()

    def wait():
        for a in range(n):
            for j in range(3):
                arrival(a, j).wait_recv()
        for cp in sent:
            cp.wait_send()
        for cp in local:
            cp.wait()

    return start, wait


def _chip_allgather_halved(shards, *, name):
    n = len(shards)
    half_rows = [s.shape[0] // 2 for s in shards]
    assert all(s.shape[0] % 16 == 0 for s in shards)

    def body(*refs):
        ins, outs = refs[:n], refs[n:2 * n]
        ici_send, ici_recv, d2d_send, d2d_recv, local_sems = refs[2 * n:]
        x, y, c = lax.axis_index("x"), lax.axis_index("y"), lax.axis_index("c")
        k = 2 * x + y
        chips = [(1 - x, y), (x, 1 - y), (1 - x, 1 - y)]

        def half(a, slab, core):
            return outs[a].at[slab, pl.ds(core * half_rows[a], half_rows[a])]

        def ici(a, j, slab):
            px, py = chips[j]
            return pltpu.make_async_remote_copy(
                src_ref=ins[a].at[pl.ds(c * half_rows[a], half_rows[a])], dst_ref=half(a, slab, c),
                send_sem=ici_send.at[3 * a + j], recv_sem=ici_recv.at[3 * a + j],
                device_id=(px, py, c), device_id_type=MESH)

        def d2d(a, j, core):
            px, py = chips[j]
            return pltpu.make_async_remote_copy(
                src_ref=half(a, 2 * px + py, core), dst_ref=half(a, 2 * px + py, core),
                send_sem=d2d_send.at[3 * a + j], recv_sem=d2d_recv.at[3 * a + j],
                device_id=(x, y, 1 - c), device_id_type=MESH)

        local = [pltpu.make_async_copy(ins[a], outs[a].at[k], local_sems.at[a]) for a in range(n)]
        sent = [ici(a, j, k) for a in range(n) for j in range(3)]
        for cp in local + sent:
            cp.start()
        passed = []
        for a in range(n):
            for j, (px, py) in enumerate(chips):
                ici(a, j, 2 * px + py).wait_recv()
                passed.append(d2d(a, j, c))
                passed[-1].start()
        for a in range(n):
            for j in range(3):
                d2d(a, j, 1 - c).wait_recv()
        for cp in sent + passed:
            cp.wait_send()
        for cp in local:
            cp.wait()

    hbm = pl.BlockSpec(memory_space=pltpu.HBM)
    return _comm_call(
        body, name=name, out_shape=[jax.ShapeDtypeStruct((N_CHIP, *s.shape), s.dtype) for s in shards],
        in_specs=[hbm] * n, out_specs=[hbm] * n,
        scratch_shapes=[pltpu.SemaphoreType.DMA((3 * n,))] * 4 + [pltpu.SemaphoreType.DMA((n,))],
    )(*shards)


def _sibling_exchange(arrays, *, name):
    n = len(arrays)

    def body(*refs):
        ins, outs = refs[:n], refs[n:2 * n]
        send_sems, recv_sems = refs[2 * n:]
        x, y, c = lax.axis_index("x"), lax.axis_index("y"), lax.axis_index("c")
        cps = [pltpu.make_async_remote_copy(
            src_ref=ins[a], dst_ref=outs[a], send_sem=send_sems.at[a], recv_sem=recv_sems.at[a],
            device_id=(x, y, 1 - c), device_id_type=MESH) for a in range(n)]
        for cp in cps:
            cp.start()
        for cp in cps:
            cp.wait()

    hbm = pl.BlockSpec(memory_space=pltpu.HBM)
    return _comm_call(
        body, name=name, out_shape=[jax.ShapeDtypeStruct(a.shape, a.dtype) for a in arrays],
        in_specs=[hbm] * n, out_specs=[hbm] * n,
        scratch_shapes=[pltpu.SemaphoreType.DMA((n,)), pltpu.SemaphoreType.DMA((n,))],
    )(*arrays)


def _sum_blocks(a, nblk, *, name):
    n = a.shape[1]

    def body(a_ref, o_ref):
        acc = a_ref[pl.ds(0, 8), :]
        for d in range(1, nblk):
            acc = acc + a_ref[pl.ds(8 * d, 8), :]
        o_ref[...] = acc

    return _call(body, name=name, out_shape=jax.ShapeDtypeStruct((8, n), F32))(a)


def _outer8(sct, dm, *, name, tm=256, tn=1024):
    R, N = sct.shape[0], dm.shape[1]
    tm, tn = min(tm, R), min(tn, N)

    def body(s_ref, d_ref, o_ref):
        s, dmv = s_ref[...], d_ref[...]
        acc = s[:, 0:1] * dmv[0:1, :]
        for b in range(1, 8):
            acc = acc + s[:, b:b + 1] * dmv[b:b + 1, :]
        o_ref[...] = acc

    return _call(
        body, name=name, grid=(R // tm, N // tn),
        in_specs=[pl.BlockSpec((tm, 8), lambda i, j: (i, 0)), pl.BlockSpec((8, tn), lambda i, j: (0, j))],
        out_specs=pl.BlockSpec((tm, tn), lambda i, j: (i, j)),
        out_shape=jax.ShapeDtypeStruct((R, N), F32),
        compiler_params=_cparams(("parallel", "parallel")),
    )(sct, dm)


_BC1 = 1.0 - ADAM_B1 ** ADAM_STEP
_BC2 = 1.0 - ADAM_B2 ** ADAM_STEP


def _adamw(w, g_parts, m, v, *, name, tile=128):
    R, C = w.shape
    tile = min(tile, R)
    assert R % tile == 0
    npart = len(g_parts)

    def body(*refs):
        w_ref, m_ref, v_ref = refs[npart:npart + 3]
        g_o, d_o, m_o, v_o = refs[npart + 3:]
        g = refs[0][...].astype(F32)
        for r in refs[1:npart]:
            g = g + r[...].astype(F32)
        mn = ADAM_B1 * m_ref[...] + (1.0 - ADAM_B1) * g
        vn = ADAM_B2 * v_ref[...] + (1.0 - ADAM_B2) * jnp.square(g)
        g_o[...] = g
        m_o[...] = mn
        v_o[...] = vn
        d_o[...] = -ADAM_LR * ((mn / _BC1) / (jnp.sqrt(vn / _BC2) + ADAM_EPS) + ADAM_WD * w_ref[...])

    spec = pl.BlockSpec((tile, C), lambda i: (i, 0))
    return _call(
        body, name=name, grid=(R // tile,), in_specs=[spec] * (npart + 3), out_specs=[spec] * 4,
        out_shape=[jax.ShapeDtypeStruct((R, C), F32)] * 4,
        compiler_params=_cparams(("parallel",)),
    )(*g_parts, w, m, v)


def _pack(vecs, rows8_cols):
    flat = jnp.concatenate([v.reshape(-1).astype(F32) for v in vecs])
    return jnp.pad(flat, (0, 8 * rows8_cols - flat.shape[0])).reshape(8, rows8_cols)


def _unpack(flat, shapes):
    out, o = [], 0
    for s in shapes:
        n = math.prod(s)
        out.append(flat[o:o + n].reshape(s))
        o += n
    return out


_BIG = ("w_in", "w_uq", "w_ukv", "w_out", "w_ff1", "w_ff2")
_SMALL = ("b_ada", "norm_mix_g", "b_gates", "conv_w", "conv_b", "q_lora_g", "kv_lora_g", "q_norm_g", "k_norm_g",
          "mlstm_norm_g", "norm_mlp_g")
_ORDER = ("w_ada", "b_ada", "norm_mix_g", "w_in", "b_gates", "conv_w", "conv_b", "q_lora_g", "w_uq", "kv_lora_g",
          "w_ukv", "q_norm_g", "k_norm_g", "mlstm_norm_g", "w_out", "norm_mlp_g", "w_ff1", "w_ff2")


def kernel(x, c, positions, w_ada, b_ada, norm_mix_g, w_in, b_gates, conv_w, conv_b, q_lora_g, w_uq, kv_lora_g, w_ukv, q_norm_g, k_norm_g, mlstm_norm_g, w_out, norm_mlp_g, w_ff1, w_ff2, loss_target, m_w_ada, m_b_ada, m_norm_mix_g, m_w_in, m_b_gates, m_conv_w, m_conv_b, m_q_lora_g, m_w_uq, m_kv_lora_g, m_w_ukv, m_q_norm_g, m_k_norm_g, m_mlstm_norm_g, m_w_out, m_norm_mlp_g, m_w_ff1, m_w_ff2, v_w_ada, v_b_ada, v_norm_mix_g, v_w_in, v_b_gates, v_conv_w, v_conv_b, v_q_lora_g, v_w_uq, v_kv_lora_g, v_w_ukv, v_q_norm_g, v_k_norm_g, v_mlstm_norm_g, v_w_out, v_norm_mlp_g, v_w_ff1, v_w_ff2):
    args = dict(locals())
    wts = {n: args[n] for n in _ORDER}
    mom = {n: args["m_" + n] for n in _ORDER}
    var = {n: args["v_" + n] for n in _ORDER}
    MX = _MXU_DTYPE
    xi, yi, ci = lax.axis_index("x"), lax.axis_index("y"), lax.axis_index("c")
    chip = 2 * xi + yi
    dev = 2 * chip + ci
    S = x.shape[1]
    CS = 2 * MW // N_CHIP
    GS = DM // N_CHIP

    pk = _pack([c, conv_w, mlstm_norm_g], 1024)
    allpk = _allgather8(pk, name="gather_small").reshape(N_DEV, 8 * 1024)
    c_all = allpk[:, :D]
    per_chip = allpk[0::2]
    conv_w_full = per_chip[:, D:D + CONVW * CS].reshape(N_CHIP, CONVW, CS).transpose(1, 0, 2).reshape(CONVW, 2 * MW)
    o = D + CONVW * CS
    mn_full = per_chip[:, o:o + HM * GS].reshape(N_CHIP, HM, GS).transpose(1, 0, 2).reshape(HM, DM)

    (sc,) = _rowmap(lambda cv: ([cv * _sigmoid(cv)], []), [(c_all, D, 0)], [], [(D, F32)], tile=8, name="silu_c")
    ncol = w_ada.shape[2]
    b_cols = lax.dynamic_slice(b_ada, (0, chip * ncol), (1, ncol))
    modp = _mm(sc, w_ada[0], "nn", name="ada_fwd", tm=8, tn=1024, tk=512, extras=(jnp.broadcast_to(b_cols, (8, ncol)),),
               epilogue=lambda r, b: (r + b,))
    modg = _allgather8(modp, name="gather_mod").reshape(N_CHIP, 2, 8, ncol)[:, 0]
    mod_all = modg.transpose(1, 0, 2).reshape(N_DEV, N_CHIP * ncol)
    modv = jnp.pad(lax.dynamic_slice(mod_all, (dev, 0), (1, 6 * D)).reshape(6, D), ((0, 2), (0, 0)))

    shards = [wts[n][0].astype(MX) for n in _BIG]
    gw_in, gw_uq, gw_ukv = _chip_allgather_halved(shards[:3], name="gather_weights")
    W = _prep_weights(gw_in, _cols(gw_uq), _cols(gw_ukv), None, None, None, norm_mix_g, norm_mlp_g,
                      q_lora_g, kv_lora_g, q_norm_g, k_norm_g, mn_full, conv_w_full, conv_b, b_gates)

    loss, gx, dmodv, g = _device_step(x[0], loss_target[0], positions[0], modv, W, late=shards[3:])

    small_shapes = [(6 * D,), (D,), (NG,), (CONVW, 2 * MW), (2 * MW,), (Q_LORA,), (KV_LORA,), (QK,), (QK,), (MW,), (D,), (1,)]
    pg = _pack([dmodv, g["norm_mix_g"], g["b_gates"], g["conv_w"], g["conv_b"], g["q_lora_g"], g["kv_lora_g"],
                g["q_norm_g"], g["k_norm_g"], g["mlstm_norm_g"], g["norm_mlp_g"], loss], 4096)
    allpg = _allgather8(pg, name="gather_small_grads")
    tot = _unpack(_sum_blocks(allpg, N_DEV, name="sum_small_grads").reshape(-1), small_shapes)
    dmod_all = allpg.reshape(N_DEV, 8 * 4096)[:, :6 * D]
    gsmall = dict(zip(_SMALL, [tot[0].reshape(1, 6 * D), tot[1].reshape(1, D), tot[2].reshape(1, NG),
                               lax.dynamic_slice(tot[3], (0, chip * CS), (CONVW, CS)).reshape(1, CONVW, CS),
                               tot[4].reshape(1, 2 * MW), tot[5].reshape(1, Q_LORA), tot[6].reshape(1, KV_LORA),
                               tot[7].reshape(1, QK), tot[8].reshape(1, QK),
                               lax.dynamic_slice(tot[9].reshape(HM, DM), (0, chip * GS), (HM, GS)).reshape(1, HM, GS),
                               tot[10].reshape(1, D)]))
    loss_tot = tot[11].reshape(())

    got = g["got"]
    part = []
    for nme, r in zip(_BIG, got):
        wd = r.shape[2]
        (p,) = _rowmap(lambda a0, a1, a2, a3: ([(a0.astype(F32) + a1.astype(F32)) + (a2.astype(F32) + a3.astype(F32))], []),
                       [(r, wd, 0, k) for k in range(N_CHIP)], [], [(wd, MX)], tile=256, name="sum_chips_" + nme)
        part.append(p)
    other = _sibling_exchange(part, name="exchange_cores")

    dm_cols = lax.dynamic_slice(dmod_all, (0, chip * ncol), (N_DEV, ncol))
    g_ada = _outer8(sc.T, dm_cols, name="ada_dw")

    res = {}
    for nme, p, q in zip(_BIG, part, other):
        res[nme] = _adamw(wts[nme][0], [p, q], mom[nme][0], var[nme][0], name="adamw_" + nme)
    res["w_ada"] = _adamw(w_ada[0], [g_ada], m_w_ada[0], v_w_ada[0], name="adamw_w_ada")
    sw = _pack([wts[n] for n in _SMALL], 3072)
    sg = _pack([gsmall[n] for n in _SMALL], 3072)
    sm = _pack([mom[n] for n in _SMALL], 3072)
    sv = _pack([var[n] for n in _SMALL], 3072)
    small_res = _adamw(sw, [sg], sm, sv, name="adamw_small", tile=8)
    shapes = [wts[n].shape for n in _SMALL]
    unp = [_unpack(r.reshape(-1), shapes) for r in small_res]
    for i, nme in enumerate(_SMALL):
        res[nme] = tuple(u[i] for u in unp)
    outs = [loss_tot, gx[None]]
    for kind in range(4):
        outs += [res[n][kind].reshape(wts[n].shape) for n in _ORDER]
    return tuple(outs)
```

```python
import functools
import math

import jax
import jax.numpy as jnp
from jax import lax
from jax.experimental import pallas as pl
from jax.experimental.pallas import tpu as pltpu

F32 = jnp.float32
BF16 = jnp.bfloat16
_MXU_DTYPE = jnp.bfloat16
_INTERPRET = False

D = 2048
H_MLA = 8
NOPE = 128
ROPE = 64
QK = NOPE + ROPE
HP = 256
VD = 128
Q_LORA = 512
KV_LORA = 256
HM = 4
DM = 256
MW = HM * DM
LCH = 128
CONVW = 5
NG = 16
DFF = 4 * D
EPS = 1e-6
M_INIT = -1e30
ROPE_THETA = 10000.0
IN_SIZES = (Q_LORA, KV_LORA, ROPE, MW, MW, MW, MW, NG)
D_IN = sum(IN_SIZES)
P_QM, P_KM, P_VM, P_OM, P_CQ, P_CKV, P_KPE, P_G = 0, 1024, 2048, 3072, 4096, 4608, 4864, 4992
D_INP = 5120

ADAM_LR, ADAM_B1, ADAM_B2, ADAM_EPS, ADAM_WD, ADAM_STEP = 0.001, 0.9, 0.999, 1e-08, 0.01, 10

V7X_VMEM_LIMIT = 56 * 1024 * 1024


def _cparams(sem):
    return pltpu.CompilerParams(dimension_semantics=sem, vmem_limit_bytes=V7X_VMEM_LIMIT)


def _call(body, **kw):
    if _INTERPRET:
        kw.pop("compiler_params", None)
        kw["interpret"] = pltpu.InterpretParams()
    return pl.pallas_call(body, **kw)


def _dot(a, b, form):
    dims = {"nn": ((1,), (0,)), "nt": ((1,), (1,)), "tn": ((0,), (0,))}[form]
    return lax.dot_general(a.astype(_MXU_DTYPE), b.astype(_MXU_DTYPE), (dims, ((), ())),
                           preferred_element_type=F32)


def _mm(a, b, form, *, name, out_dtypes=(F32,), epilogue=None, extras=(), tm=1024, tn=1024, tk=2048, side=()):
    if form == "nn":
        (M, K), (K2, N) = a.shape, b.shape
    elif form == "nt":
        (M, K), (N, K2) = a.shape, b.shape
    else:
        (K, M), (K2, N) = a.shape, b.shape
    assert K == K2, (a.shape, b.shape, form)
    tm, tn = min(tm, M), min(tn, N)
    tk = max(d for d in range(128, min(tk, K) + 1, 128) if K % d == 0) if K > 128 else K
    assert M % tm == 0 and N % tn == 0 and K % tk == 0, (M, N, K, tm, tn, tk)
    nk = K // tk
    ne, no = len(extras), len(out_dtypes)
    if form == "tn":
        a_spec = pl.BlockSpec((tk, tm), lambda i, j, k: (k, i))
    else:
        a_spec = pl.BlockSpec((tm, tk), lambda i, j, k: (i, k))
    if form == "nt":
        b_spec = pl.BlockSpec((tn, tk), lambda i, j, k: (j, k))
    else:
        b_spec = pl.BlockSpec((tk, tn), lambda i, j, k: (k, j))
    mn_spec = pl.BlockSpec((tm, tn), lambda i, j, k: (i, j))
    grid = (M // tm, N // tn, nk)
    ns, io, wrap = _side_exchange(side, False, grid)

    def body(a_ref, b_ref, *rest):
        ex, outs = rest[:ne], rest[ne + ns:ne + ns + no]
        scratch = rest[ne + 2 * ns + no:]
        side_start, side_wait = wrap(rest[ne:ne + ns], rest[ne + ns + no:ne + 2 * ns + no], scratch[1:])
        side_start()
        prod = _dot(a_ref[...], b_ref[...], form)

        def finish(r):
            vals = (r,) if epilogue is None else epilogue(r, *[e[...] for e in ex])
            for o, v in zip(outs, vals):
                o[...] = v.astype(o.dtype)

        if nk == 1:
            finish(prod)
        else:
            acc, k = scratch[0], pl.program_id(2)

            @pl.when(k == 0)
            def _():
                acc[...] = prod

            @pl.when(k > 0)
            def _():
                acc[...] += prod

            @pl.when(k == nk - 1)
            def _():
                finish(acc[...])
        side_wait()

    res = _call(
        body, name=name, grid=grid,
        in_specs=[a_spec, b_spec] + [mn_spec] * ne + io["specs"],
        out_specs=[mn_spec] * no + io["specs"],
        out_shape=[jax.ShapeDtypeStruct((M, N), dt) for dt in out_dtypes] + io["out_shape"],
        scratch_shapes=[pltpu.VMEM((tm, tn) if nk > 1 else (8, 128), F32)] + io["scratch"],
        compiler_params=_cparams(("arbitrary",) * 3 if ns else ("parallel", "parallel", "arbitrary")),
    )(a, b, *extras, *side)
    if ns:
        return (res[0] if no == 1 else res[:no]), list(res[no:])
    return res[0] if no == 1 else res


def _rowmap(fn, rows, bcasts, outs, accs=(), *, tile, name):
    rows = [r if len(r) == 4 else (*r, None) for r in rows]
    S = rows[0][0].shape[-2]
    tile = min(tile, S)
    assert S % tile == 0
    nr, nb, no, na = len(rows), len(bcasts), len(outs), len(accs)

    def body(*refs):
        vals = [r[...] for r in refs[:nr + nb]]
        o_refs, a_refs = refs[nr + nb:nr + nb + no], refs[nr + nb + no:]
        o_vals, a_vals = fn(*vals)
        for r, v in zip(o_refs, o_vals):
            r[...] = v.astype(r.dtype)
        if na:
            @pl.when(pl.program_id(0) == 0)
            def _():
                for r in a_refs:
                    r[...] = jnp.zeros(r.shape, r.dtype)
            for r, v in zip(a_refs, a_vals):
                r[...] += v

    in_specs = []
    for (arr, w, cb, lead) in rows:
        if lead is None:
            in_specs.append(pl.BlockSpec((tile, w), lambda i, cb=cb: (i, cb)))
        else:
            in_specs.append(pl.BlockSpec((None, tile, w), lambda i, cb=cb, lead=lead: (lead, i, cb)))
    in_specs += [pl.BlockSpec(b.shape, lambda i: (0, 0)) for b in bcasts]
    out_specs = [pl.BlockSpec((tile, w), lambda i: (i, 0)) for (w, _) in outs]
    out_specs += [pl.BlockSpec(s, lambda i: (0, 0)) for s in accs]
    out_shape = [jax.ShapeDtypeStruct((S, w), dt) for (w, dt) in outs]
    out_shape += [jax.ShapeDtypeStruct(s, F32) for s in accs]
    return _call(
        body, name=name, grid=(S // tile,), in_specs=in_specs, out_specs=out_specs, out_shape=out_shape,
        compiler_params=_cparams(("arbitrary",)),
    )(*[r[0] for r in rows], *bcasts)


def _colsum(v):
    return jnp.sum(v, axis=0, keepdims=True)


def _rms(x, n):
    r = lax.rsqrt(jnp.sum(x * x, axis=-1, keepdims=True) * (1.0 / n) + EPS)
    return x * r, r


def _rms_bwd(dxhat, xhat, r, n):
    return r * (dxhat - xhat * (jnp.sum(dxhat * xhat, axis=-1, keepdims=True) * (1.0 / n)))


def _rope_fwd(r, cosp, s1, s2):
    return r * cosp + pltpu.roll(r, 32, 1) * s1 + pltpu.roll(r, 96, 1) * s2


def _rope_bwd(d, cosp, s1, s2):
    return d * cosp + pltpu.roll(d * s1, 96, 1) + pltpu.roll(d * s2, 32, 1)


def _sigmoid(x):
    return 1.0 / (1.0 + jnp.exp(-x))


def _halo_specs(tile, halo, width, cb, S, lead=None):
    nh = tile // halo
    last = S // halo - 1
    if lead is None:
        return [
            pl.BlockSpec((tile, width), lambda i: (i, cb)),
            pl.BlockSpec((halo, width), lambda i: (jnp.maximum(i * nh - 1, 0), cb)),
            pl.BlockSpec((halo, width), lambda i: (jnp.minimum((i + 1) * nh, last), cb)),
        ]
    return [
        pl.BlockSpec((None, tile, width), lambda i: (lead, i, cb)),
        pl.BlockSpec((None, halo, width), lambda i: (lead, jnp.maximum(i * nh - 1, 0), cb)),
        pl.BlockSpec((None, halo, width), lambda i: (lead, jnp.minimum((i + 1) * nh, last), cb)),
    ]


def _conv_fwd(proj, conv_w8, conv_b, *, tile=256):
    S = proj.shape[0]
    T = min(tile, S)
    n = S // T
    W = 2 * MW

    def body(x_ref, xp_ref, xn_ref, w_ref, b_ref, q_ref, k_ref, ext):
        i = pl.program_id(0)
        ext[pl.ds(0, 8), :] = xp_ref[...] * (i > 0).astype(F32)
        ext[pl.ds(8, T), :] = x_ref[...]
        ext[pl.ds(8 + T, 8), :] = xn_ref[...] * (i < n - 1).astype(F32)
        w = w_ref[...]
        y = b_ref[...] + w[0:1, :] * ext[pl.ds(6, T), :]
        for o in range(1, CONVW):
            y = y + w[o:o + 1, :] * ext[pl.ds(6 + o, T), :]
        y = y * _sigmoid(y)
        q_ref[...] = y[:, :MW].astype(q_ref.dtype)
        k_ref[...] = (y[:, MW:] * (DM ** -0.5)).astype(k_ref.dtype)

    return _call(
        body, name="conv_fwd", grid=(n,),
        in_specs=_halo_specs(T, 8, W, 0, S) + [pl.BlockSpec((8, W), lambda i: (0, 0)),
                                                 pl.BlockSpec((1, W), lambda i: (0, 0))],
        out_specs=[pl.BlockSpec((T, MW), lambda i: (i, 0))] * 2,
        out_shape=[jax.ShapeDtypeStruct((S, MW), _MXU_DTYPE)] * 2,
        scratch_shapes=[pltpu.VMEM((T + 16, W), F32)],
        compiler_params=_cparams(("arbitrary",)),
    )(proj, proj, proj, conv_w8, conv_b)


def _conv_bwd(proj, dqd, dkd, conv_w8, conv_b, *, tile=256):
    S = proj.shape[0]
    T = min(tile, S)
    n = S // T
    W = 2 * MW

    def body(x_ref, xp_ref, xn_ref, *rest):
        g = rest[:12]
        w_ref, b_ref, dx_ref, dw_ref, db_ref, ext, edp = rest[12:]
        i = pl.program_id(0)
        mp = (i > 0).astype(F32)
        mn = (i < n - 1).astype(F32)
        ext[pl.ds(0, 16), :] = xp_ref[...] * mp
        ext[pl.ds(16, T), :] = x_ref[...]
        ext[pl.ds(16 + T, 16), :] = xn_ref[...] * mn
        w = w_ref[...]
        pre = b_ref[...] + w[0:1, :] * ext[pl.ds(6, T + 16), :]
        for o in range(1, CONVW):
            pre = pre + w[o:o + 1, :] * ext[pl.ds(6 + o, T + 16), :]
        sg = _sigmoid(pre)
        dsilu = sg * (1.0 + pre * (1.0 - sg))
        for half, (a0, a1) in enumerate(((g[0:3], g[3:6]), (g[6:9], g[9:12]))):
            sc = 1.0 if half == 0 else DM ** -0.5
            cols = pl.ds(half * MW, MW)
            edp[pl.ds(0, 8), cols] = (a0[1][...] + a1[1][...]) * (mp * sc)
            edp[pl.ds(8, T), cols] = (a0[0][...] + a1[0][...]) * sc
            edp[pl.ds(8 + T, 8), cols] = (a0[2][...] + a1[2][...]) * (mn * sc)
        edp[...] = edp[...] * dsilu
        @pl.when(i == 0)
        def _():
            dw_ref[...] = jnp.zeros(dw_ref.shape, F32)
            db_ref[...] = jnp.zeros(db_ref.shape, F32)

        x_main = ext[pl.ds(16, T), :]
        dx = None
        for o in range(CONVW):
            view = edp[pl.ds(10 - o, T), :]
            dx = w[o:o + 1, :] * view if dx is None else dx + w[o:o + 1, :] * view
            dw_ref[pl.ds(o, 1), :] += _colsum(x_main * view)
        dx_ref[...] = dx.astype(dx_ref.dtype)
        db_ref[...] += _colsum(edp[pl.ds(8, T), :])

    gspecs = _halo_specs(T, 8, MW, 0, S) * 4
    return _call(
        body, name="conv_bwd", grid=(n,),
        in_specs=_halo_specs(T, 16, W, 0, S) + gspecs + [pl.BlockSpec((8, W), lambda i: (0, 0)),
                                                          pl.BlockSpec((1, W), lambda i: (0, 0))],
        out_specs=[pl.BlockSpec((T, W), lambda i: (i, 0)), pl.BlockSpec((8, W), lambda i: (0, 0)),
                   pl.BlockSpec((1, W), lambda i: (0, 0))],
        out_shape=[jax.ShapeDtypeStruct((S, W), _MXU_DTYPE), jax.ShapeDtypeStruct((8, W), F32),
                   jax.ShapeDtypeStruct((1, W), F32)],
        scratch_shapes=[pltpu.VMEM((T + 32, W), F32), pltpu.VMEM((T + 16, W), F32)],
        compiler_params=_cparams(("arbitrary",)),
    )(proj, proj, proj, *([dqd[0]] * 3), *([dqd[1]] * 3), *([dkd[0]] * 3), *([dkd[1]] * 3), conv_w8, conv_b)


_ATT_SCALE = QK ** -0.5
_LOG2E = math.log2(math.e)
_Q_PRESCALE = _ATT_SCALE * _LOG2E


def _side_exchange(side, gather, grid, cols=()):
    ns = len(side)
    io = _exchange_io(side, gather, cols) if ns else dict(specs=[], out_shape=[], scratch=[])

    def wrap(refs_in, refs_out, sems):
        if not ns:
            return (lambda: None), (lambda: None)
        start, wait = _exchange_ops(refs_in, refs_out, *sems, gather=gather, cols=cols)
        ids = [pl.program_id(a) for a in range(len(grid))]
        first = functools.reduce(jnp.logical_and, [i == 0 for i in ids])
        last = functools.reduce(jnp.logical_and, [i == g - 1 for i, g in zip(ids, grid)])
        return (lambda: pl.when(first)(start)), (lambda: pl.when(last)(wait))

    return ns, io, wrap


def _side_gather_halved(side, grid, cols, mid_step):
    ns = len(side)
    hbm = pl.BlockSpec(memory_space=pltpu.HBM)
    shape = lambda i, a: (a.shape[0], N_CHIP * a.shape[1]) if i in cols else (N_CHIP, *a.shape)
    io = dict(specs=[hbm] * ns, out_shape=[jax.ShapeDtypeStruct(shape(i, a), a.dtype) for i, a in enumerate(side)],
              scratch=([pltpu.SemaphoreType.DMA((3 * ns,))] * 4 + [pltpu.SemaphoreType.DMA((ns,))]) if ns else [])

    def wrap(ins, outs, sems):
        if not ns:
            return (lambda: None,) * 3
        ici_send, ici_recv, d2d_send, d2d_recv, local_sems = sems
        x, y, c = lax.axis_index("x"), lax.axis_index("y"), lax.axis_index("c")
        k = 2 * x + y
        chips = [(1 - x, y), (x, 1 - y), (1 - x, 1 - y)]
        half = [r.shape[0] // 2 for r in ins]

        def piece(a, chip, core=None):
            rows = slice(None) if core is None else pl.ds(core * half[a], half[a])
            if a in cols:
                width = ins[a].shape[1]
                return outs[a].at[rows, pl.ds(pl.multiple_of(chip * width, 128), width)]
            return outs[a].at[chip, rows]

        def ici(a, j, chip):
            px, py = chips[j]
            return pltpu.make_async_remote_copy(
                src_ref=ins[a].at[pl.ds(c * half[a], half[a])], dst_ref=piece(a, chip, c),
                send_sem=ici_send.at[3 * a + j], recv_sem=ici_recv.at[3 * a + j],
                device_id=(px, py, c), device_id_type=MESH)

        def d2d(a, j, core):
            px, py = chips[j]
            return pltpu.make_async_remote_copy(
                src_ref=piece(a, 2 * px + py, core), dst_ref=piece(a, 2 * px + py, core),
                send_sem=d2d_send.at[3 * a + j], recv_sem=d2d_recv.at[3 * a + j],
                device_id=(x, y, 1 - c), device_id_type=MESH)

        local = [pltpu.make_async_copy(ins[a], piece(a, k), local_sems.at[a]) for a in range(ns)]
        pairs = [(a, j) for a in range(ns) for j in range(3)]

        def start():
            for cp in local + [ici(a, j, k) for a, j in pairs]:
                cp.start()

        def mid():
            for a, j in pairs:
                px, py = chips[j]
                ici(a, j, 2 * px + py).wait_recv()
                d2d(a, j, c).start()

        def wait():
            for a, j in pairs:
                d2d(a, j, 1 - c).wait_recv()
            for a, j in pairs:
                ici(a, j, k).wait_send()
                d2d(a, j, c).wait_send()
            for cp in local:
                cp.wait()

        ids = [pl.program_id(a) for a in range(len(grid))]
        at = lambda step: functools.reduce(jnp.logical_and, [i == s for i, s in zip(ids, step)])
        return (lambda: pl.when(at((0,) * len(grid)))(start), lambda: pl.when(at(mid_step))(mid),
                lambda: pl.when(at(tuple(g - 1 for g in grid)))(wait))

    return ns, io, wrap


def _attn_fwd(q, k, v, *, side=(), side_cols=(), tq=1024, split=4):
    S = q.shape[0]
    tq = min(tq, S)
    hq = tq // split
    grid = (H_MLA, S // tq)
    ns, io, wrap = _side_gather_halved(side, grid, side_cols, (grid[0] * 5 // 8, 0))

    def body(q_ref, k_ref, v_ref, *rest):
        o_ref, qa_ref = rest[ns:ns + 2]
        side_start, side_mid, side_wait = wrap(rest[:ns], rest[ns + 2:2 * ns + 2], rest[2 * ns + 2:])
        side_start()
        side_mid()
        kv, vv = k_ref[...], v_ref[...]
        lane = lax.broadcasted_iota(jnp.int32, (hq, HP), 1)
        for a in range(split):
            r = pl.ds(a * hq, hq)
            qv = q_ref[r, :]
            s = _dot(qv, kv, "nt")
            m = jnp.max(s, axis=1, keepdims=True)
            acc = _dot(jnp.exp2(s - m), vv, "nn")
            l = acc[:, VD:VD + 1]
            o_ref[r, :] = (acc[:, :VD] / l).astype(o_ref.dtype)
            lse = m + jnp.log2(l)
            hi = lse.astype(_MXU_DTYPE).astype(F32)
            qa = jnp.where(lane == QK, -hi, jnp.where(lane == QK + 1, hi - lse, qv.astype(F32)))
            qa_ref[r, :] = qa.astype(qa_ref.dtype)
        side_wait()

    res = _call(
        body, name="attn_fwd", grid=grid,
        in_specs=[pl.BlockSpec((tq, HP), lambda h, i: (i, h)),
                  pl.BlockSpec((S, HP), lambda h, i: (0, h)),
                  pl.BlockSpec((S, HP), lambda h, i: (0, h))] + io["specs"],
        out_specs=[pl.BlockSpec((tq, VD), lambda h, i: (i, h)),
                   pl.BlockSpec((tq, HP), lambda h, i: (i, h))] + io["specs"],
        out_shape=[jax.ShapeDtypeStruct((S, H_MLA * VD), _MXU_DTYPE),
                   jax.ShapeDtypeStruct((S, H_MLA * HP), _MXU_DTYPE)] + io["out_shape"],
        scratch_shapes=io["scratch"],
        compiler_params=_cparams(("arbitrary", "arbitrary")),
    )(q, k, v, *side)
    return res[0], res[1], list(res[2:])


def _attn_bwd(qa, k, va, doa, *, side=(), side_cols=(), tq=4096, tk=1024, split=4, unroll=1):
    S = qa.shape[0]
    tq, tk = min(tq, S), min(tk, S)
    nq, nkb = S // tq, S // tk
    hq = tq // split
    grid = (H_MLA, nkb)
    ns, io, wrap = _side_exchange(side, False, grid, side_cols)

    def body(q_ref, k_ref, v_ref, do_ref, *rest):
        dq_ref, dk_ref, dv_ref = rest[ns:ns + 3]
        side_start, side_wait = wrap(rest[:ns], rest[ns + 3:2 * ns + 3], rest[2 * ns + 3:])
        side_start()
        j = pl.program_id(1)

        @pl.when(j == 0)
        def _():
            dq_ref[...] = jnp.zeros(dq_ref.shape, F32)

        dk_ref[...] = jnp.zeros(dk_ref.shape, F32)
        dv_ref[...] = jnp.zeros(dv_ref.shape, F32)
        kb, vb = k_ref[...], v_ref[...]

        def step(i, carry):
            for a in range(split):
                r = pl.ds(pl.multiple_of(i * tq + a * hq, hq), hq)
                qg, dog = q_ref[r, :], do_ref[r, :]
                p = jnp.exp2(_dot(qg, kb, "nt"))
                ds = (p * _dot(dog, vb, "nt")).astype(_MXU_DTYPE)
                dq_ref[r, :] += _dot(ds, kb, "nn")
                dv_ref[...] += _dot(p, dog, "tn")
                dk_ref[...] += _dot(ds, qg, "tn")
            return carry

        lax.fori_loop(0, nq, step, 0, unroll=unroll if nq % unroll == 0 else 1)
        dk_ref[...] = dk_ref[...] * (1.0 / _LOG2E)

        @pl.when(j == nkb - 1)
        def _():
            dq_ref[...] = dq_ref[...] * _ATT_SCALE

        side_wait()

    blk = pl.BlockSpec((tk, HP), lambda h, j: (j, h))
    whole = pl.BlockSpec((S, HP), lambda h, j: (0, h))
    res = _call(
        body, name="attn_bwd", grid=grid,
        in_specs=[whole, blk, blk, whole] + io["specs"],
        out_specs=[whole, blk, blk] + io["specs"],
        out_shape=[jax.ShapeDtypeStruct((S, H_MLA * HP), F32)] * 3 + io["out_shape"],
        scratch_shapes=io["scratch"],
        compiler_params=_cparams(("arbitrary", "arbitrary")),
    )(qa, k, va, doa, *side)
    return res[0], res[1], res[2], list(res[3:])


def _mlstm_chunk_terms(g, q, k, v, gates, gates_t, bg_row, C, n_row, m):
    L = LCH
    d = g // HM
    h = g % HM
    i_idx = d * 8 + h
    f_idx = d * 8 + 4 + h
    rr = lax.broadcasted_iota(jnp.int32, (L, L), 0)
    cc = lax.broadcasted_iota(jnp.int32, (L, L), 1)
    order = (rr - cc) * (1 - 2 * d)
    tri = order >= 0
    eye = rr == cc
    lane = lax.broadcasted_iota(jnp.int32, gates.shape, 1)
    sub = lax.broadcasted_iota(jnp.int32, gates_t.shape, 0)
    lane_b = lax.broadcasted_iota(jnp.int32, bg_row.shape, 1)
    pick_c = lambda idx: jnp.sum(jnp.where(lane == idx, gates, 0.0), axis=1, keepdims=True)
    pick_r = lambda idx: jnp.sum(jnp.where(sub == idx, gates_t, 0.0), axis=0, keepdims=True)
    pick_b = lambda idx: jnp.sum(jnp.where(lane_b == idx, bg_row, 0.0), axis=1, keepdims=True)
    i_col, i_row = pick_c(i_idx) + pick_b(i_idx), pick_r(i_idx) + pick_b(i_idx)
    f_col, f_row = pick_c(f_idx) + pick_b(f_idx), pick_r(f_idx) + pick_b(f_idx)
    logsig = lambda x: jnp.minimum(x, 0.0) - jnp.log(1.0 + jnp.exp(-jnp.abs(x)))
    lf_col, lf_row = logsig(f_col), logsig(f_row)
    b_col = jnp.sum(jnp.where(tri, lf_row, 0.0), axis=1, keepdims=True)
    tri_t = order <= 0
    b_row = jnp.sum(jnp.where(tri_t, lf_col, 0.0), axis=0, keepdims=True)
    bL = jnp.sum(lf_row, axis=1, keepdims=True)
    log_inter = b_col + m
    logD = jnp.where(tri, b_col - b_row + i_row, -jnp.inf)
    m_t = jnp.maximum(log_inter, jnp.max(logD, axis=1, keepdims=True))
    Dm = jnp.exp(logD - m_t)
    w_inter = jnp.exp(log_inter - m_t)
    A = _dot(q, k, "nt")
    Sc = A * Dm
    numI = _dot(q, C, "nt")
    qf = q.astype(F32)
    kf = k.astype(F32)
    denI = jnp.sum(qf * n_row, axis=1, keepdims=True)
    num = _dot(Sc, v, "nn") + w_inter * numI
    den = jnp.sum(Sc, axis=1, keepdims=True) + w_inter * denI
    floor = jnp.exp(-m_t)
    Nst = jnp.maximum(jnp.abs(den), floor)
    log_w = bL - b_col + i_col
    m_new = jnp.maximum(bL + m, jnp.max(log_w, axis=0, keepdims=True))
    decay = jnp.exp(bL + m - m_new)
    w_col = jnp.exp(log_w - m_new)
    return dict(tri=tri, eye=eye, f_row=f_row, Dm=Dm, w_inter=w_inter, A=A, Sc=Sc, numI=numI, denI=denI,
                num=num, den=den, floor=floor, Nst=Nst, m_new=m_new, decay=decay, w_col=w_col, qf=qf, kf=kf)


def _mlstm_specs(nc, d, step_of):
    chunk = lambda j: step_of(j) if d == 0 else nc - 1 - step_of(j)
    return chunk, [
        pl.BlockSpec((LCH, DM), lambda h, j: (chunk(j), h)),
        pl.BlockSpec((LCH, DM), lambda h, j: (chunk(j), h)),
        pl.BlockSpec((LCH, DM), lambda h, j: (chunk(j), P_VM // DM + h)),
        pl.BlockSpec((LCH, 128), lambda h, j: (chunk(j), P_G // 128)),
        pl.BlockSpec((NG, LCH), lambda h, j: (0, chunk(j))),
    ]


def _mlstm_fwd(qc, kc, proj, gates_t, bg_row):
    S = qc.shape[0]
    nc = S // LCH
    in_specs, out_specs = [], []
    for d in (0, 1):
        chunk, specs = _mlstm_specs(nc, d, lambda j: j)
        in_specs += specs
        out_specs += [pl.BlockSpec((LCH, DM), lambda h, j, chunk=chunk: (chunk(j), h)),
                      pl.BlockSpec((None, None, DM, DM), lambda h, j, chunk=chunk: (h, chunk(j), 0, 0)),
                      pl.BlockSpec((None, None, 8, DM), lambda h, j, chunk=chunk: (h, chunk(j), 0, 0))]
    in_specs.append(pl.BlockSpec((1, 128), lambda h, j: (0, 0)))

    def body(*refs):
        bg_ref, outs, (C_s, n_s, m_s) = refs[10], refs[11:17], refs[17:]

        @pl.when(pl.program_id(1) == 0)
        def _():
            C_s[...] = jnp.zeros(C_s.shape, F32)
            n_s[...] = jnp.zeros(n_s.shape, F32)
            m_s[...] = jnp.full(m_s.shape, M_INIT, F32)

        for d in (0, 1):
            q_ref, k_ref, v_ref, g_ref, gt_ref = refs[5 * d:5 * d + 5]
            h_ref, cst_ref, nm_ref = outs[3 * d:3 * d + 3]
            g = d * HM + pl.program_id(0)
            C, n_row, m = C_s[d], n_s[d, 0:1, :], m_s[d, 0:1, 0:1]
            cst_ref[...] = C
            nm_ref[0:1, :] = n_row
            nm_ref[1:2, :] = jnp.broadcast_to(m, (1, DM))
            nm_ref[2:8, :] = jnp.zeros((6, DM), F32)
            q, k, v = q_ref[...], k_ref[...], v_ref[...]
            t = _mlstm_chunk_terms(g, q, k, v, g_ref[...], gt_ref[...], bg_ref[...], C, n_row, m)
            h_ref[...] = t["num"] / t["Nst"]
            wv = t["w_col"] * v
            C_s[d] = t["decay"] * C + _dot(wv, k, "tn")
            n_s[d, 0:1, :] = t["decay"] * n_row + _colsum(t["w_col"] * t["kf"])
            m_s[d] = jnp.broadcast_to(t["m_new"], (8, 128))

    res = _call(
        body, name="mlstm_fwd", grid=(HM, nc), in_specs=in_specs, out_specs=out_specs,
        out_shape=[jax.ShapeDtypeStruct((S, MW), F32), jax.ShapeDtypeStruct((HM, nc, DM, DM), F32),
                   jax.ShapeDtypeStruct((HM, nc, 8, DM), F32)] * 2,
        scratch_shapes=[pltpu.VMEM((2, DM, DM), F32), pltpu.VMEM((2, 8, DM), F32), pltpu.VMEM((2, 8, 128), F32)],
        compiler_params=_cparams(("parallel", "arbitrary")),
    )(*([qc, kc, proj, proj, gates_t] * 2), bg_row)
    return (res[0], res[3]), (res[1], res[4]), (res[2], res[5])


def _mlstm_bwd(qc, kc, proj, gates_t, bg_row, dh, cst, nm):
    S = qc.shape[0]
    nc = S // LCH
    in_specs, out_specs = [], []
    for d in (0, 1):
        chunk, specs = _mlstm_specs(nc, d, lambda j: nc - 1 - j)
        in_specs += specs + [pl.BlockSpec((LCH, DM), lambda h, j, chunk=chunk: (chunk(j), h)),
                             pl.BlockSpec((None, None, DM, DM), lambda h, j, chunk=chunk: (h, chunk(j), 0, 0)),
                             pl.BlockSpec((None, None, 8, DM), lambda h, j, chunk=chunk: (h, chunk(j), 0, 0))]
        out_specs += [pl.BlockSpec((LCH, DM), lambda h, j, chunk=chunk: (chunk(j), h))] * 3
        out_specs += [pl.BlockSpec((None, None, 8, LCH), lambda h, j, chunk=chunk: (h, chunk(j), 0, 0))]
    in_specs.append(pl.BlockSpec((1, 128), lambda h, j: (0, 0)))

    def body(*refs):
        bg_ref, outs, (dC_s, dn_s) = refs[16], refs[17:25], refs[25:]

        @pl.when(pl.program_id(1) == 0)
        def _():
            dC_s[...] = jnp.zeros(dC_s.shape, F32)
            dn_s[...] = jnp.zeros(dn_s.shape, F32)

        for d in (0, 1):
            _mlstm_bwd_chain(d, refs[8 * d:8 * d + 8], bg_ref, outs[4 * d:4 * d + 4], dC_s, dn_s)

    res = _call(
        body, name="mlstm_bwd", grid=(HM, nc), in_specs=in_specs, out_specs=out_specs,
        out_shape=([jax.ShapeDtypeStruct((S, MW), F32)] * 3 + [jax.ShapeDtypeStruct((HM, nc, 8, LCH), F32)]) * 2,
        scratch_shapes=[pltpu.VMEM((2, DM, DM), F32), pltpu.VMEM((2, 8, DM), F32)],
        compiler_params=_cparams(("parallel", "arbitrary")),
    )(*[a for d in (0, 1) for a in (qc, kc, proj, proj, gates_t, dh, cst[d], nm[d])], bg_row)
    return (res[0], res[4]), (res[1], res[5]), (res[2], res[6]), (res[3], res[7])


def _mlstm_bwd_chain(d, ins, bg_ref, outs, dC_s, dn_s):
        q_ref, k_ref, v_ref, g_ref, gt_ref, dh_ref, cst_ref, nm_ref = ins
        dq_ref, dk_ref, dv_ref, dg_ref = outs
        g = d * HM + pl.program_id(0)
        C, n_row, m = cst_ref[...], nm_ref[0:1, :], nm_ref[1:2, 0:1]
        q, k, v = q_ref[...], k_ref[...], v_ref[...]
        t = _mlstm_chunk_terms(g, q, k, v, g_ref[...], gt_ref[...], bg_ref[...], C, n_row, m)
        tri, eye, qf, kf = t["tri"], t["eye"], t["qf"], t["kf"]
        w_inter, w_col, decay, Nst = t["w_inter"], t["w_col"], t["decay"], t["Nst"]
        dC, dn = dC_s[d], dn_s[d, 0:1, :]
        dhv = dh_ref[...]
        hval = t["num"] / Nst
        dnum = dhv / Nst
        dNst = -jnp.sum(dhv * hval, axis=1, keepdims=True) / Nst
        dden = jnp.where(jnp.abs(t["den"]) > t["floor"], jnp.sign(t["den"]) * dNst, 0.0)
        dSc = _dot(dnum, v, "nt") + dden
        dA = dSc * t["Dm"]
        G = dSc * t["Sc"]
        KdC = _dot(k, dC, "nt")
        dq = _dot(dA, k, "nn") + w_inter * _dot(dnum, C, "nn") + (w_inter * dden) * n_row
        dk = _dot(dA, q, "tn") + w_col * _dot(v, dC, "nn") + w_col * dn
        dv = _dot(t["Sc"], dnum, "tn") + w_col * KdC
        dq_ref[...] = dq
        dk_ref[...] = dk
        dv_ref[...] = dv
        dlog_inter = w_inter * (jnp.sum(dnum * t["numI"], axis=1, keepdims=True) + dden * t["denI"])
        rowG = jnp.sum(G, axis=1, keepdims=True)
        colG = jnp.sum(G, axis=0, keepdims=True)
        u_col = w_col * (jnp.sum(v * KdC, axis=1, keepdims=True) + jnp.sum(kf * dn, axis=1, keepdims=True))
        colG_c = jnp.sum(jnp.where(eye, colG, 0.0), axis=1, keepdims=True)
        u_row = jnp.sum(jnp.where(eye, u_col, 0.0), axis=0, keepdims=True)
        db_col = rowG + dlog_inter - u_col - colG_c
        dbL = jnp.sum(u_col, axis=0, keepdims=True) + decay * (
            jnp.sum(jnp.sum(dC * C, axis=1, keepdims=True), axis=0, keepdims=True)
            + jnp.sum(dn * n_row, axis=1, keepdims=True))
        dlf_row = jnp.sum(jnp.where(tri, db_col, 0.0), axis=0, keepdims=True) + dbL
        di_row = colG + u_row
        df_row = dlf_row * (1.0 - _sigmoid(t["f_row"]))
        dg_ref[...] = jnp.zeros(dg_ref.shape, F32)
        dg_ref[0:1, :] = di_row
        dg_ref[1:2, :] = df_row
        dC_s[d] = decay * dC + _dot(w_inter * dnum, q, "tn")
        dn_s[d, 0:1, :] = decay * dn + _colsum((w_inter * dden) * qf)


def _pad_w_in(w):
    cq, ckv, kpe, qm, km, vm, om, gt = _split_in(w)
    z = lambda n: jnp.zeros((w.shape[0], n), w.dtype)
    return jnp.concatenate([qm, km, vm, om, cq, ckv, kpe, z(HP - QK), gt, z(128 - NG)], axis=1)


def _split_in(w):
    out, o = [], 0
    for n in IN_SIZES:
        out.append(w[:, o:o + n])
        o += n
    return out


def _unpad_w_in(g):
    return jnp.concatenate([g[:, P_CQ:P_CQ + Q_LORA], g[:, P_CKV:P_CKV + KV_LORA], g[:, P_KPE:P_KPE + ROPE],
                            g[:, 0:4 * MW], g[:, P_G:P_G + NG]], axis=1)


_IN_SHARD = D_IN // 4
_IN_SEGMENTS = ((0, 512, P_CQ), (512, 768, P_CKV), (768, 832, P_KPE), (832, 4928, P_QM), (4928, 4944, P_G))


def _pad_w_in_slabs(slabs):
    def orig(a, b):
        out = []
        for k in range(4):
            lo, hi = max(a, k * _IN_SHARD), min(b, (k + 1) * _IN_SHARD)
            if lo < hi:
                out.append(slabs[k][:, lo - k * _IN_SHARD:hi - k * _IN_SHARD])
        return out

    z = lambda n: jnp.zeros((slabs.shape[1], n), slabs.dtype)
    return jnp.concatenate(orig(832, 4928) + orig(0, 512) + orig(512, 768) + orig(768, 832) + [z(HP - QK)]
                           + orig(4928, 4944) + [z(128 - NG)], axis=1)


def _unpad_w_in_slabs(g):
    slabs = []
    for k in range(4):
        pieces = []
        for a, b, p in _IN_SEGMENTS:
            lo, hi = max(a, k * _IN_SHARD), min(b, (k + 1) * _IN_SHARD)
            if lo < hi:
                pieces.append(g[:, p + lo - a:p + hi - a])
        slabs.append(jnp.concatenate(pieces, axis=1))
    return jnp.stack(slabs)


def _pad_w_uq(w):
    return jnp.pad(w.reshape(Q_LORA, H_MLA, QK), ((0, 0), (0, 0), (0, HP - QK))).reshape(Q_LORA, H_MLA * HP)


def _unpad_w_uq(g):
    return g.reshape(Q_LORA, H_MLA, HP)[:, :, :QK].reshape(Q_LORA, H_MLA * QK)


def _perm_w_ukv(w):
    return w.reshape(KV_LORA, H_MLA, 2, NOPE).transpose(0, 2, 1, 3).reshape(KV_LORA, 2 * H_MLA * NOPE)


def _unperm_w_ukv(g):
    return g.reshape(KV_LORA, 2, H_MLA, NOPE).transpose(0, 2, 1, 3).reshape(KV_LORA, 2 * H_MLA * NOPE)


def _rope_tables(positions):
    half = ROPE // 2
    freqs = ROPE_THETA ** (-jnp.arange(half, dtype=F32) / half)
    ang = positions.astype(F32)[:, None] * freqs
    cos, sin = jnp.cos(ang), jnp.sin(ang)
    z32, z64 = jnp.zeros_like(cos), jnp.zeros((cos.shape[0], 64), F32)
    return (jnp.concatenate([cos, cos, z64], axis=1), jnp.concatenate([z32, sin, z64], axis=1),
            jnp.concatenate([-sin, z32, z64], axis=1))


def _device_step(x, tgt, positions, modv, W, late=None):
    S = x.shape[0]
    MX = _MXU_DTYPE
    cosp, rs1, rs2 = _rope_tables(positions)
    tabs = [(cosp, 128, 0), (rs1, 128, 0), (rs2, 128, 0)]
    cat1 = lambda vs: jnp.concatenate(vs, axis=1)
    hsl = lambda hh, w: slice(hh * w, (hh + 1) * w)

    def ln1(xv, g, mv):
        xhat, _ = _rms(xv, D)
        return [xhat * g * (1.0 + mv[1:2]) + mv[0:1]], []

    (h,) = _rowmap(ln1, [(x, D, 0)], [W["g_mix"], modv], [(D, MX)], tile=512, name="ln1")
    proj = _mm(h, W["w_in"], "nn", name="proj")

    def lora(cq, ckv, gq, gkv):
        return [_rms(cq, Q_LORA)[0] * gq, _rms(ckv, KV_LORA)[0] * gkv], []

    cqn, ckvn = _rowmap(lora, [(proj, Q_LORA, P_CQ // Q_LORA), (proj, KV_LORA, P_CKV // KV_LORA)],
                        [W["g_qlora"], W["g_kvlora"]], [(Q_LORA, MX), (KV_LORA, MX)], tile=512, name="lora_norm")
    q_raw = _mm(cqn, W["w_uq"], "nn", name="q_up")
    kv_raw = _mm(ckvn, W["w_ukv"], "nn", name="kv_up")

    def mla_q(qr, cp, a1, a2, gq):
        outs = []
        for hh in range(H_MLA):
            y = _rms(qr[:, hsl(hh, HP)], QK)[0] * gq
            outs += [y[:, :NOPE], _rope_fwd(y[:, NOPE:], cp, a1, a2)]
        return [cat1(outs) * _Q_PRESCALE], []

    (qh,) = _rowmap(mla_q, [(q_raw, H_MLA * HP, 0)] + tabs, [W["gq"]], [(H_MLA * HP, MX)], tile=512, name="mla_q")

    def mla_k(kvr, kpe, cp, a1, a2, gk):
        lane = lax.broadcasted_iota(jnp.int32, (kvr.shape[0], 128), 1)
        outs, vas = [], []
        for hh in range(H_MLA):
            y = _rms(cat1([kvr[:, hsl(hh, NOPE)], kpe]), QK)[0] * gk
            outs += [y[:, :NOPE], _rope_fwd(y[:, NOPE:], cp, a1, a2) + ((lane == ROPE) | (lane == ROPE + 1)).astype(F32)]
            vas += [kvr[:, H_MLA * NOPE + hh * VD:H_MLA * NOPE + (hh + 1) * VD], (lane < 2).astype(F32)]
        return [cat1(outs), cat1(vas)], []

    kh, va = _rowmap(mla_k, [(kv_raw, 2 * H_MLA * NOPE, 0), (proj, 128, P_KPE // 128)] + tabs, [W["gk"]],
                     [(H_MLA * HP, MX), (H_MLA * HP, MX)], tile=512, name="mla_k")
    attn_o, qa, gathered = _attn_fwd(qh, kh, va, side=late or (), side_cols=(1,))
    if late:
        W = dict(W, w_out=gathered[0].reshape(D, D), w_ff1=gathered[1], w_ff2=gathered[2].reshape(DFF, D))

    qc, kc = _conv_fwd(proj, W["conv_w8"], W["conv_b"])
    gates_t = proj[:, P_G:P_G + NG].T
    (h_f, h_b), cst, nm = _mlstm_fwd(qc, kc, proj, gates_t, W["bg_row"])
    hrows = [(h_f, MW, 0), (h_b, MW, 0), (proj, MW, P_OM // MW)]

    def ml_out(ao, hf, hb, om, gmn):
        outs = [ao.astype(F32)]
        hs = hf + hb
        for hh in range(HM):
            sl = hsl(hh, DM)
            outs.append(_sigmoid(om[:, sl]) * _rms(hs[:, sl], DM)[0] * gmn[:, sl])
        return [cat1(outs)], []

    (cat,) = _rowmap(ml_out, [(attn_o, MW, 0)] + hrows, [W["g_mn"]], [(D, MX)], tile=512, name="ml_out")
    mixed = _mm(cat, W["w_out"], "nn", name="out_proj")

    def res_ln2(xv, mx, g, mv):
        x1 = xv + mv[2:3] * mx
        return [x1, _rms(x1, D)[0] * g * (1.0 + mv[4:5]) + mv[3:4]], []

    x1, h2 = _rowmap(res_ln2, [(x, D, 0), (mixed, D, 0)], [W["g_mlp"], modv], [(D, F32), (D, MX)],
                     tile=512, name="res_ln2")
    a, u = _mm(h2, W["w_ff1"], "nn", name="ff1", out_dtypes=(MX, MX),
               epilogue=lambda r: (jnp.square(jnp.maximum(r, 0.0)), r))
    y = _mm(a, W["w_ff2"], "nn", name="ff2")

    def final(x1v, yv, tv, mv):
        err = x1v + mv[5:6] * yv - tv
        dout = err * (1.0 / D)
        loss = jnp.sum(jnp.sum(0.5 * err * dout, axis=1, keepdims=True), axis=0, keepdims=True)
        return [dout, mv[5:6] * dout], [loss, _colsum(dout * yv)]

    dout, dy, loss, dgate2 = _rowmap(final, [(x1, D, 0), (y, D, 0), (tgt, D, 0)], [modv], [(D, F32), (D, MX)],
                                     [(1, 1), (1, D)], tile=256, name="loss_head")

    du = _mm(dy, W["w_ff2"], "nt", name="ff2_dx", out_dtypes=(MX,), extras=(u,),
             epilogue=lambda r, uv: (r * (2.0 * jnp.maximum(uv.astype(F32), 0.0)),))
    gdt = (MX,)
    g_ff2 = _mm(a, dy, "tn", name="ff2_dw", out_dtypes=gdt)
    dh2 = _mm(du, W["w_ff1"], "nt", name="ff1_dx")
    g_ff1 = _mm(h2, du, "tn", name="ff1_dw", out_dtypes=gdt)

    def ln2_bwd(dh2v, x1v, doutv, mxv, g, mv):
        xhat, r = _rms(x1v, D)
        dn2 = dh2v * (1.0 + mv[4:5])
        dx1 = doutv + _rms_bwd(dn2 * g, xhat, r, D)
        return [dx1, mv[2:3] * dx1], [_colsum(dh2v), _colsum(dh2v * xhat * g), _colsum(dn2 * xhat), _colsum(dx1 * mxv)]

    dx1, dmixed, dshift2, dscale2, dg_mlp, dgate1 = _rowmap(
        ln2_bwd, [(dh2, D, 0), (x1, D, 0), (dout, D, 0), (mixed, D, 0)], [W["g_mlp"], modv],
        [(D, F32), (D, MX)], [(1, D)] * 4, tile=256, name="ln2_bwd")
    dcat = _mm(dmixed, W["w_out"], "nt", name="out_dx")
    g_out = _mm(cat, dmixed, "tn", name="out_dw", out_dtypes=gdt)

    def ml_out_bwd(dml, hf, hb, om, gmn):
        hs = hf + hb
        dhs, dos, dgs = [], [], []
        for hh in range(HM):
            sl = hsl(hh, DM)
            xhat, r = _rms(hs[:, sl], DM)
            g, sg, d = gmn[:, sl], _sigmoid(om[:, sl]), dml[:, sl]
            dos.append(d * xhat * g * sg * (1.0 - sg))
            dhn = d * sg
            dgs.append(_colsum(dhn * xhat))
            dhs.append(_rms_bwd(dhn * g, xhat, r, DM))
        return [cat1(dhs), cat1(dos)], [cat1(dgs)]

    dhs, do_m, dg_mn = _rowmap(ml_out_bwd, [(dcat, MW, 1)] + hrows, [W["g_mn"]], [(MW, F32), (MW, MX)],
                               [(1, MW)], tile=512, name="ml_out_bwd")
    dqd, dkd, dvd, dgates = _mlstm_bwd(qc, kc, proj, gates_t, W["bg_row"], dhs, cst, nm)
    dqk_m, dconv_w8, dconv_b = _conv_bwd(proj, dqd, dkd, W["conv_w8"], W["conv_b"])

    def do_aug(ao, dov):
        lane = lax.broadcasted_iota(jnp.int32, (ao.shape[0], 128), 1)
        outs = []
        for hh in range(H_MLA):
            sl = hsl(hh, VD)
            dl = jnp.sum(ao[:, sl].astype(F32) * dov[:, sl], axis=1, keepdims=True)
            hi = dl.astype(MX).astype(F32)
            outs += [dov[:, sl], jnp.where(lane == 0, -hi, jnp.where(lane == 1, hi - dl, 0.0))]
        return [cat1(outs)], []

    (doa,) = _rowmap(do_aug, [(attn_o, MW, 0), (dcat, MW, 0)], [], [(H_MLA * HP, MX)], tile=512, name="attn_delta")
    side = [g_out.reshape(N_CHIP, D // N_CHIP, D), g_ff1, g_ff2.reshape(N_CHIP, DFF // N_CHIP, D)] if late else ()
    dq_a, dk_a, dv_a, late_got = _attn_bwd(qa, kh, va, doa, side=side, side_cols=(1,))

    def mla_q_bwd(dqv, qr, cp, a1, a2, gq):
        outs, dg = [], 0.0
        for hh in range(H_MLA):
            sl = hsl(hh, HP)
            xhat, r = _rms(qr[:, sl], QK)
            d = dqv[:, sl]
            dyv = cat1([d[:, :NOPE], _rope_bwd(d[:, NOPE:], cp, a1, a2)])
            dg = dg + _colsum(dyv * xhat)
            outs.append(_rms_bwd(dyv * gq, xhat, r, QK))
        return [cat1(outs)], [dg]

    dq_raw, dgq = _rowmap(mla_q_bwd, [(dq_a, H_MLA * HP, 0), (q_raw, H_MLA * HP, 0)] + tabs, [W["gq"]],
                          [(H_MLA * HP, MX)], [(1, HP)], tile=512, name="mla_q_bwd")
    dcqn = _mm(dq_raw, W["w_uq"], "nt", name="q_up_dx")
    g_uq = _mm(cqn, dq_raw, "tn", name="q_up_dw", out_dtypes=gdt)

    def mla_k_bwd(dkv, dvv, kvr, kpe, cp, a1, a2, gk):
        dkn, dg, dkpe = [], 0.0, 0.0
        for hh in range(H_MLA):
            xhat, r = _rms(cat1([kvr[:, hsl(hh, NOPE)], kpe]), QK)
            d = dkv[:, hsl(hh, HP)]
            dyv = cat1([d[:, :NOPE], _rope_bwd(d[:, NOPE:], cp, a1, a2)])
            dg = dg + _colsum(dyv * xhat)
            dxv = _rms_bwd(dyv * gk, xhat, r, QK)
            dkn.append(dxv[:, :NOPE])
            dkpe = dkpe + dxv[:, NOPE:]
        return [cat1(dkn + [dvv[:, hh * HP:hh * HP + VD] for hh in range(H_MLA)]), dkpe], [dg]

    dkv_raw, dkpe, dgk = _rowmap(
        mla_k_bwd, [(dk_a, H_MLA * HP, 0), (dv_a, H_MLA * HP, 0), (kv_raw, 2 * H_MLA * NOPE, 0),
                    (proj, 128, P_KPE // 128)] + tabs, [W["gk"]],
        [(2 * H_MLA * NOPE, MX), (128, MX)], [(1, HP)], tile=256, name="mla_k_bwd")
    dckvn = _mm(dkv_raw, W["w_ukv"], "nt", name="kv_up_dx")
    g_ukv = _mm(ckvn, dkv_raw, "tn", name="kv_up_dw", out_dtypes=gdt)

    def lora_bwd(dcq, dckv, cq, ckv, gq, gkv):
        xq, rq = _rms(cq, Q_LORA)
        xk, rk = _rms(ckv, KV_LORA)
        return ([_rms_bwd(dcq * gq, xq, rq, Q_LORA), _rms_bwd(dckv * gkv, xk, rk, KV_LORA)],
                [_colsum(dcq * xq), _colsum(dckv * xk)])

    dc_q, dc_kv, dg_qlora, dg_kvlora = _rowmap(
        lora_bwd, [(dcqn, Q_LORA, 0), (dckvn, KV_LORA, 0), (proj, Q_LORA, P_CQ // Q_LORA),
                   (proj, KV_LORA, P_CKV // KV_LORA)], [W["g_qlora"], W["g_kvlora"]],
        [(Q_LORA, MX), (KV_LORA, MX)], [(1, Q_LORA), (1, KV_LORA)], tile=512, name="lora_bwd")

    nc = S // LCH
    dg16 = jnp.stack(dgates)[:, :, :, 0:2, :].transpose(2, 4, 0, 3, 1).reshape(S, NG)
    dg128 = jnp.pad(dg16, ((0, 0), (0, 128 - NG)))

    def assemble(dqk, dv0, dv1, dom, dcq, dckv, dkp, dgp):
        f = lambda t: t.astype(F32)
        return [cat1([f(dqk), dv0 + dv1, f(dom), f(dcq), f(dckv), f(dkp), dgp])], [_colsum(dgp)]

    dproj, dbg = _rowmap(
        assemble, [(dqk_m, 2 * MW, 0), (dvd[0], MW, 0), (dvd[1], MW, 0), (do_m, MW, 0), (dc_q, Q_LORA, 0),
                   (dc_kv, KV_LORA, 0), (dkpe, 128, 0), (dg128, 128, 0)], [], [(D_INP, MX)], [(1, 128)],
        tile=256, name="dproj")
    g_in = _mm(h, dproj, "tn", name="proj_dw", out_dtypes=gdt)
    early_got = ()
    if late:
        side = [_unpad_w_in_slabs(g_in).astype(MX), _slabs(_unpad_w_uq(g_uq)).astype(MX),
                _slabs(_unperm_w_ukv(g_ukv)).astype(MX)]
        dh, early_got = _mm(dproj, W["w_in"], "nt", name="proj_dx", side=side)
    else:
        dh = _mm(dproj, W["w_in"], "nt", name="proj_dx")

    def ln1_bwd(dhv, xv, dx1v, g, mv):
        xhat, r = _rms(xv, D)
        dn = dhv * (1.0 + mv[1:2])
        return [dx1v + _rms_bwd(dn * g, xhat, r, D)], [_colsum(dhv), _colsum(dhv * xhat * g), _colsum(dn * xhat)]

    gx, dshift1, dscale1, dg_mix = _rowmap(ln1_bwd, [(dh, D, 0), (x, D, 0), (dx1, D, 0)], [W["g_mix"], modv],
                                           [(D, F32)], [(1, D)] * 3, tile=256, name="ln1_bwd")
    dmodv = jnp.concatenate([dshift1, dscale1, dgate1, dshift2, dscale2, dgate2], axis=0)
    grads = dict(w_in=g_in, w_uq=g_uq, w_ukv=g_ukv, w_out=g_out, w_ff1=g_ff1, w_ff2=g_ff2,
                 norm_mix_g=dg_mix, b_gates=dbg[:, :NG], conv_w=dconv_w8[:CONVW], conv_b=dconv_b,
                 q_lora_g=dg_qlora, kv_lora_g=dg_kvlora, q_norm_g=dgq[:, :QK], k_norm_g=dgk[:, :QK],
                 mlstm_norm_g=dg_mn, norm_mlp_g=dg_mlp)
    grads["got"] = list(early_got) + list(late_got)
    return loss, gx, dmodv, grads


def _cols(g):
    return g.transpose(1, 0, 2).reshape(g.shape[1], N_CHIP * g.shape[2])


def _slabs(gfull):
    return gfull.reshape(gfull.shape[0], N_CHIP, -1).transpose(1, 0, 2)


def _prep_weights(w_in, w_uq, w_ukv, w_out, w_ff1, w_ff2, norm_mix_g, norm_mlp_g, q_lora_g, kv_lora_g,
                  q_norm_g, k_norm_g, mlstm_norm_g, conv_w, conv_b, b_gates):
    MX = _MXU_DTYPE
    padg = lambda g: jnp.pad(g.reshape(1, QK).astype(F32), ((0, 0), (0, HP - QK)))
    return dict(
        w_in=(_pad_w_in_slabs(w_in) if w_in.ndim == 3 else _pad_w_in(w_in)).astype(MX), w_uq=_pad_w_uq(w_uq).astype(MX), w_ukv=_perm_w_ukv(w_ukv).astype(MX),
        w_out=None if w_out is None else w_out.astype(MX), w_ff1=None if w_ff1 is None else w_ff1.astype(MX),
        w_ff2=None if w_ff2 is None else w_ff2.astype(MX),
        g_mix=norm_mix_g.reshape(1, D), g_mlp=norm_mlp_g.reshape(1, D), g_qlora=q_lora_g.reshape(1, Q_LORA),
        g_kvlora=kv_lora_g.reshape(1, KV_LORA), gq=padg(q_norm_g), gk=padg(k_norm_g),
        g_mn=mlstm_norm_g.reshape(1, MW), conv_w8=jnp.pad(conv_w.reshape(CONVW, 2 * MW), ((0, 8 - CONVW), (0, 0))),
        conv_b=conv_b.reshape(1, 2 * MW), bg_row=jnp.pad(b_gates.reshape(1, NG), ((0, 0), (0, 128 - NG))))


MESH = pl.DeviceIdType.MESH
N_DEV = 8
N_CHIP = 4


def _comm_call(body, **kw):
    if _INTERPRET:
        kw["interpret"] = pltpu.InterpretParams()
    return pl.pallas_call(body, **kw)


def _allgather8(blk, *, name):
    m_per, n = blk.shape

    def body(x_ref, out_ref, send_sems, recv_sems, local_sem):
        x, y, c = lax.axis_index("x"), lax.axis_index("y"), lax.axis_index("c")
        me, sibling = (x, y, c), (x, y, 1 - c)
        chips = [(1 - x, y), (x, 1 - y), (1 - x, 1 - y)]

        def rows(px, py, pc):
            return out_ref.at[pl.ds((4 * px + 2 * py + pc) * m_per, m_per), :]

        def copy(k, block, to, src=None):
            return pltpu.make_async_remote_copy(
                src_ref=rows(*block) if src is None else src, dst_ref=rows(*block),
                send_sem=send_sems.at[k], recv_sem=recv_sems.at[k], device_id=to, device_id_type=MESH)

        mine = pltpu.make_async_copy(x_ref, rows(*me), local_sem)
        mine.start()
        first = [copy(0, me, sibling, src=x_ref)]
        first += [copy(1 + j, me, (*chip, c), src=x_ref) for j, chip in enumerate(chips)]
        for cp in first:
            cp.start()
        passed = [copy(4 + j, (*chip, c), sibling) for j, chip in enumerate(chips)]
        for j, chip in enumerate(chips):
            copy(1 + j, (*chip, c), me).wait_recv()
            passed[j].start()
        copy(0, sibling, me).wait_recv()
        for j, chip in enumerate(chips):
            copy(4 + j, (*chip, 1 - c), me).wait_recv()
        for cp in first + passed:
            cp.wait_send()
        mine.wait()

    return _comm_call(
        body, name=name, out_shape=jax.ShapeDtypeStruct((N_DEV * m_per, n), blk.dtype),
        in_specs=[pl.BlockSpec(memory_space=pltpu.VMEM)], out_specs=pl.BlockSpec(memory_space=pltpu.VMEM),
        scratch_shapes=[pltpu.SemaphoreType.DMA((7,)), pltpu.SemaphoreType.DMA((7,)), pltpu.SemaphoreType.DMA],
    )(blk)


def _chip_exchange(arrays, *, gather, name):
    n = len(arrays)

    def body(*refs):
        start, wait = _exchange_ops(refs[:n], refs[n:2 * n], *refs[2 * n:], gather=gather)
        start()
        wait()

    io = _exchange_io(arrays, gather)
    return _comm_call(body, name=name, out_shape=io["out_shape"], in_specs=io["specs"], out_specs=io["specs"],
                      scratch_shapes=io["scratch"])(*arrays)


def _exchange_io(arrays, gather, cols=()):
    n = len(arrays)

    def out(i, a):
        if gather:
            return (a.shape[0], N_CHIP * a.shape[1]) if i in cols else (N_CHIP, *a.shape)
        return (N_CHIP, a.shape[0], a.shape[1] // N_CHIP) if i in cols else a.shape

    return dict(
        specs=[pl.BlockSpec(memory_space=pltpu.HBM)] * n,
        out_shape=[jax.ShapeDtypeStruct(out(i, a), a.dtype) for i, a in enumerate(arrays)],
        scratch=[pltpu.SemaphoreType.DMA((3 * n,)), pltpu.SemaphoreType.DMA((3 * n,)), pltpu.SemaphoreType.DMA((n,))])


def _exchange_ops(ins, outs, send_sems, recv_sems, local_sems, *, gather, cols=()):
    n = len(ins)
    x, y, c = lax.axis_index("x"), lax.axis_index("y"), lax.axis_index("c")
    k = 2 * x + y
    chips = [(1 - x, y), (x, 1 - y), (1 - x, 1 - y)]

    def piece(ref, a, chip, windowed):
        if not windowed:
            return ref.at[chip]
        width = ref.shape[1] // N_CHIP
        return ref.at[:, pl.ds(pl.multiple_of(chip * width, 128), width)]

    src_of = lambda a, chip: ins[a] if gather else piece(ins[a], a, chip, a in cols)
    dst_of = lambda a, chip: piece(outs[a], a, chip, gather and a in cols)

    def remote(a, j):
        px, py = chips[j]
        return pltpu.make_async_remote_copy(
            src_ref=src_of(a, 2 * px + py), dst_ref=dst_of(a, k), send_sem=send_sems.at[3 * a + j],
            recv_sem=recv_sems.at[3 * a + j], device_id=(px, py, c), device_id_type=MESH)

    def arrival(a, j):
        px, py = chips[j]
        return pltpu.make_async_remote_copy(
            src_ref=src_of(a, k), dst_ref=dst_of(a, 2 * px + py), send_sem=send_sems.at[3 * a + j],
            recv_sem=recv_sems.at[3 * a + j], device_id=(px, py, c), device_id_type=MESH)

    local = [pltpu.make_async_copy(src_of(a, k), dst_of(a, k), local_sems.at[a]) for a in range(n)]
    sent = [remote(a, j) for a in range(n) for j in range(3)]

    def start():
        for cp in local + sent:
            cp.start()

    def wait():
        for a in range(n):
            for j in range(3):
                arrival(a, j).wait_recv()
        for cp in sent:
            cp.wait_send()
        for cp in local:
            cp.wait()

    return start, wait


def _chip_allgather_halved(shards, *, name):
    n = len(shards)
    half_rows = [s.shape[0] // 2 for s in shards]
    assert all(s.shape[0] % 16 == 0 for s in shards)

    def body(*refs):
        ins, outs = refs[:n], refs[n:2 * n]
        ici_send, ici_recv, d2d_send, d2d_recv, local_sems = refs[2 * n:]
        x, y, c = lax.axis_index("x"), lax.axis_index("y"), lax.axis_index("c")
        k = 2 * x + y
        chips = [(1 - x, y), (x, 1 - y), (1 - x, 1 - y)]

        def half(a, slab, core):
            return outs[a].at[slab, pl.ds(core * half_rows[a], half_rows[a])]

        def ici(a, j, slab):
            px, py = chips[j]
            return pltpu.make_async_remote_copy(
                src_ref=ins[a].at[pl.ds(c * half_rows[a], half_rows[a])], dst_ref=half(a, slab, c),
                send_sem=ici_send.at[3 * a + j], recv_sem=ici_recv.at[3 * a + j],
                device_id=(px, py, c), device_id_type=MESH)

        def d2d(a, j, core):
            px, py = chips[j]
            return pltpu.make_async_remote_copy(
                src_ref=half(a, 2 * px + py, core), dst_ref=half(a, 2 * px + py, core),
                send_sem=d2d_send.at[3 * a + j], recv_sem=d2d_recv.at[3 * a + j],
                device_id=(x, y, 1 - c), device_id_type=MESH)

        local = [pltpu.make_async_copy(ins[a], outs[a].at[k], local_sems.at[a]) for a in range(n)]
        sent = [ici(a, j, k) for a in range(n) for j in range(3)]
        for cp in local + sent:
            cp.start()
        passed = []
        for a in range(n):
            for j, (px, py) in enumerate(chips):
                ici(a, j, 2 * px + py).wait_recv()
                passed.append(d2d(a, j, c))
                passed[-1].start()
        for a in range(n):
            for j in range(3):
                d2d(a, j, 1 - c).wait_recv()
        for cp in sent + passed:
            cp.wait_send()
        for cp in local:
            cp.wait()

    hbm = pl.BlockSpec(memory_space=pltpu.HBM)
    return _comm_call(
        body, name=name, out_shape=[jax.ShapeDtypeStruct((N_CHIP, *s.shape), s.dtype) for s in shards],
        in_specs=[hbm] * n, out_specs=[hbm] * n,
        scratch_shapes=[pltpu.SemaphoreType.DMA((3 * n,))] * 4 + [pltpu.SemaphoreType.DMA((n,))],
    )(*shards)


def _sibling_exchange(arrays, *, name):
    n = len(arrays)

    def body(*refs):
        ins, outs = refs[:n], refs[n:2 * n]
        send_sems, recv_sems = refs[2 * n:]
        x, y, c = lax.axis_index("x"), lax.axis_index("y"), lax.axis_index("c")
        cps = [pltpu.make_async_remote_copy(
            src_ref=ins[a], dst_ref=outs[a], send_sem=send_sems.at[a], recv_sem=recv_sems.at[a],
            device_id=(x, y, 1 - c), device_id_type=MESH) for a in range(n)]
        for cp in cps:
            cp.start()
        for cp in cps:
            cp.wait()

    hbm = pl.BlockSpec(memory_space=pltpu.HBM)
    return _comm_call(
        body, name=name, out_shape=[jax.ShapeDtypeStruct(a.shape, a.dtype) for a in arrays],
        in_specs=[hbm] * n, out_specs=[hbm] * n,
        scratch_shapes=[pltpu.SemaphoreType.DMA((n,)), pltpu.SemaphoreType.DMA((n,))],
    )(*arrays)


def _sum_blocks(a, nblk, *, name):
    n = a.shape[1]

    def body(a_ref, o_ref):
        acc = a_ref[pl.ds(0, 8), :]
        for d in range(1, nblk):
            acc = acc + a_ref[pl.ds(8 * d, 8), :]
        o_ref[...] = acc

    return _call(body, name=name, out_shape=jax.ShapeDtypeStruct((8, n), F32))(a)


def _outer8(sct, dm, *, name, tm=256, tn=1024):
    R, N = sct.shape[0], dm.shape[1]
    tm, tn = min(tm, R), min(tn, N)

    def body(s_ref, d_ref, o_ref):
        s, dmv = s_ref[...], d_ref[...]
        acc = s[:, 0:1] * dmv[0:1, :]
        for b in range(1, 8):
            acc = acc + s[:, b:b + 1] * dmv[b:b + 1, :]
        o_ref[...] = acc

    return _call(
        body, name=name, grid=(R // tm, N // tn),
        in_specs=[pl.BlockSpec((tm, 8), lambda i, j: (i, 0)), pl.BlockSpec((8, tn), lambda i, j: (0, j))],
        out_specs=pl.BlockSpec((tm, tn), lambda i, j: (i, j)),
        out_shape=jax.ShapeDtypeStruct((R, N), F32),
        compiler_params=_cparams(("parallel", "parallel")),
    )(sct, dm)


_BC1 = 1.0 - ADAM_B1 ** ADAM_STEP
_BC2 = 1.0 - ADAM_B2 ** ADAM_STEP


def _adamw(w, g_parts, m, v, *, name, tile=128):
    R, C = w.shape
    tile = min(tile, R)
    assert R % tile == 0
    npart = len(g_parts)

    def body(*refs):
        w_ref, m_ref, v_ref = refs[npart:npart + 3]
        g_o, d_o, m_o, v_o = refs[npart + 3:]
        g = refs[0][...].astype(F32)
        for r in refs[1:npart]:
            g = g + r[...].astype(F32)
        mn = ADAM_B1 * m_ref[...] + (1.0 - ADAM_B1) * g
        vn = ADAM_B2 * v_ref[...] + (1.0 - ADAM_B2) * jnp.square(g)
        g_o[...] = g
        m_o[...] = mn
        v_o[...] = vn
        d_o[...] = -ADAM_LR * ((mn / _BC1) / (jnp.sqrt(vn / _BC2) + ADAM_EPS) + ADAM_WD * w_ref[...])

    spec = pl.BlockSpec((tile, C), lambda i: (i, 0))
    return _call(
        body, name=name, grid=(R // tile,), in_specs=[spec] * (npart + 3), out_specs=[spec] * 4,
        out_shape=[jax.ShapeDtypeStruct((R, C), F32)] * 4,
        compiler_params=_cparams(("parallel",)),
    )(*g_parts, w, m, v)


def _pack(vecs, rows8_cols):
    flat = jnp.concatenate([v.reshape(-1).astype(F32) for v in vecs])
    return jnp.pad(flat, (0, 8 * rows8_cols - flat.shape[0])).reshape(8, rows8_cols)


def _unpack(flat, shapes):
    out, o = [], 0
    for s in shapes:
        n = math.prod(s)
        out.append(flat[o:o + n].reshape(s))
        o += n
    return out


_BIG = ("w_in", "w_uq", "w_ukv", "w_out", "w_ff1", "w_ff2")
_SMALL = ("b_ada", "norm_mix_g", "b_gates", "conv_w", "conv_b", "q_lora_g", "kv_lora_g", "q_norm_g", "k_norm_g",
          "mlstm_norm_g", "norm_mlp_g")
_ORDER = ("w_ada", "b_ada", "norm_mix_g", "w_in", "b_gates", "conv_w", "conv_b", "q_lora_g", "w_uq", "kv_lora_g",
          "w_ukv", "q_norm_g", "k_norm_g", "mlstm_norm_g", "w_out", "norm_mlp_g", "w_ff1", "w_ff2")


def kernel(x, c, positions, w_ada, b_ada, norm_mix_g, w_in, b_gates, conv_w, conv_b, q_lora_g, w_uq, kv_lora_g, w_ukv, q_norm_g, k_norm_g, mlstm_norm_g, w_out, norm_mlp_g, w_ff1, w_ff2, loss_target, m_w_ada, m_b_ada, m_norm_mix_g, m_w_in, m_b_gates, m_conv_w, m_conv_b, m_q_lora_g, m_w_uq, m_kv_lora_g, m_w_ukv, m_q_norm_g, m_k_norm_g, m_mlstm_norm_g, m_w_out, m_norm_mlp_g, m_w_ff1, m_w_ff2, v_w_ada, v_b_ada, v_norm_mix_g, v_w_in, v_b_gates, v_conv_w, v_conv_b, v_q_lora_g, v_w_uq, v_kv_lora_g, v_w_ukv, v_q_norm_g, v_k_norm_g, v_mlstm_norm_g, v_w_out, v_norm_mlp_g, v_w_ff1, v_w_ff2):
    args = dict(locals())
    wts = {n: args[n] for n in _ORDER}
    mom = {n: args["m_" + n] for n in _ORDER}
    var = {n: args["v_" + n] for n in _ORDER}
    MX = _MXU_DTYPE
    xi, yi, ci = lax.axis_index("x"), lax.axis_index("y"), lax.axis_index("c")
    chip = 2 * xi + yi
    dev = 2 * chip + ci
    S = x.shape[1]
    CS = 2 * MW // N_CHIP
    GS = DM // N_CHIP

    pk = _pack([c, conv_w, mlstm_norm_g], 1024)
    allpk = _allgather8(pk, name="gather_small").reshape(N_DEV, 8 * 1024)
    c_all = allpk[:, :D]
    per_chip = allpk[0::2]
    conv_w_full = per_chip[:, D:D + CONVW * CS].reshape(N_CHIP, CONVW, CS).transpose(1, 0, 2).reshape(CONVW, 2 * MW)
    o = D + CONVW * CS
    mn_full = per_chip[:, o:o + HM * GS].reshape(N_CHIP, HM, GS).transpose(1, 0, 2).reshape(HM, DM)

    (sc,) = _rowmap(lambda cv: ([cv * _sigmoid(cv)], []), [(c_all, D, 0)], [], [(D, F32)], tile=8, name="silu_c")
    ncol = w_ada.shape[2]
    b_cols = lax.dynamic_slice(b_ada, (0, chip * ncol), (1, ncol))
    modp = _mm(sc, w_ada[0], "nn", name="ada_fwd", tm=8, tn=1024, tk=512, extras=(jnp.broadcast_to(b_cols, (8, ncol)),),
               epilogue=lambda r, b: (r + b,))
    modg = _allgather8(modp, name="gather_mod").reshape(N_CHIP, 2, 8, ncol)[:, 0]
    mod_all = modg.transpose(1, 0, 2).reshape(N_DEV, N_CHIP * ncol)
    modv = jnp.pad(lax.dynamic_slice(mod_all, (dev, 0), (1, 6 * D)).reshape(6, D), ((0, 2), (0, 0)))

    shards = [wts[n][0].astype(MX) for n in _BIG]
    gw_in, gw_uq, gw_ukv = _chip_allgather_halved(shards[:3], name="gather_weights")
    W = _prep_weights(gw_in, _cols(gw_uq), _cols(gw_ukv), None, None, None, norm_mix_g, norm_mlp_g,
                      q_lora_g, kv_lora_g, q_norm_g, k_norm_g, mn_full, conv_w_full, conv_b, b_gates)

    loss, gx, dmodv, g = _device_step(x[0], loss_target[0], positions[0], modv, W, late=shards[3:])

    small_shapes = [(6 * D,), (D,), (NG,), (CONVW, 2 * MW), (2 * MW,), (Q_LORA,), (KV_LORA,), (QK,), (QK,), (MW,), (D,), (1,)]
    pg = _pack([dmodv, g["norm_mix_g"], g["b_gates"], g["conv_w"], g["conv_b"], g["q_lora_g"], g["kv_lora_g"],
                g["q_norm_g"], g["k_norm_g"], g["mlstm_norm_g"], g["norm_mlp_g"], loss], 4096)
    allpg = _allgather8(pg, name="gather_small_grads")
    tot = _unpack(_sum_blocks(allpg, N_DEV, name="sum_small_grads").reshape(-1), small_shapes)
    dmod_all = allpg.reshape(N_DEV, 8 * 4096)[:, :6 * D]
    gsmall = dict(zip(_SMALL, [tot[0].reshape(1, 6 * D), tot[1].reshape(1, D), tot[2].reshape(1, NG),
                               lax.dynamic_slice(tot[3], (0, chip * CS), (CONVW, CS)).reshape(1, CONVW, CS),
                               tot[4].reshape(1, 2 * MW), tot[5].reshape(1, Q_LORA), tot[6].reshape(1, KV_LORA),
                               tot[7].reshape(1, QK), tot[8].reshape(1, QK),
                               lax.dynamic_slice(tot[9].reshape(HM, DM), (0, chip * GS), (HM, GS)).reshape(1, HM, GS),
                               tot[10].reshape(1, D)]))
    loss_tot = tot[11].reshape(())

    got = g["got"]
    part = []
    for nme, r in zip(_BIG, got):
        wd = r.shape[2]
        (p,) = _rowmap(lambda a0, a1, a2, a3: ([(a0.astype(F32) + a1.astype(F32)) + (a2.astype(F32) + a3.astype(F32))], []),
                       [(r, wd, 0, k) for k in range(N_CHIP)], [], [(wd, MX)], tile=256, name="sum_chips_" + nme)
        part.append(p)
    other = _sibling_exchange(part, name="exchange_cores")

    dm_cols = lax.dynamic_slice(dmod_all, (0, chip * ncol), (N_DEV, ncol))
    g_ada = _outer8(sc.T, dm_cols, name="ada_dw")

    res = {}
    for nme, p, q in zip(_BIG, part, other):
        res[nme] = _adamw(wts[nme][0], [p, q], mom[nme][0], var[nme][0], name="adamw_" + nme)
    res["w_ada"] = _adamw(w_ada[0], [g_ada], m_w_ada[0], v_w_ada[0], name="adamw_w_ada")
    sw = _pack([wts[n] for n in _SMALL], 3072)
    sg = _pack([gsmall[n] for n in _SMALL], 3072)
    sm = _pack([mom[n] for n in _SMALL], 3072)
    sv = _pack([var[n] for n in _SMALL], 3072)
    small_res = _adamw(sw, [sg], sm, sv, name="adamw_small", tile=8)
    shapes = [wts[n].shape for n in _SMALL]
    unp = [_unpack(r.reshape(-1), shapes) for r in small_res]
    for i, nme in enumerate(_SMALL):
        res[nme] = tuple(u[i] for u in unp)
    outs = [loss_tot, gx[None]]
    for kind in range(4):
        outs += [res[n][kind].reshape(wts[n].shape) for n in _ORDER]
    return tuple(outs)
```

```python
import functools
import math

import jax
import jax.numpy as jnp
from jax import lax
from jax.experimental import pallas as pl
from jax.experimental.pallas import tpu as pltpu

F32 = jnp.float32
BF16 = jnp.bfloat16
_MXU_DTYPE = jnp.bfloat16
_INTERPRET = False

D = 2048
H_MLA = 8
NOPE = 128
ROPE = 64
QK = NOPE + ROPE
HP = 256
VD = 128
Q_LORA = 512
KV_LORA = 256
HM = 4
DM = 256
MW = HM * DM
LCH = 128
CONVW = 5
NG = 16
DFF = 4 * D
EPS = 1e-6
M_INIT = -1e30
ROPE_THETA = 10000.0
IN_SIZES = (Q_LORA, KV_LORA, ROPE, MW, MW, MW, MW, NG)
D_IN = sum(IN_SIZES)
P_QM, P_KM, P_VM, P_OM, P_CQ, P_CKV, P_KPE, P_G = 0, 1024, 2048, 3072, 4096, 4608, 4864, 4992
D_INP = 5120

ADAM_LR, ADAM_B1, ADAM_B2, ADAM_EPS, ADAM_WD, ADAM_STEP = 0.001, 0.9, 0.999, 1e-08, 0.01, 10

V7X_VMEM_LIMIT = 56 * 1024 * 1024


def _cparams(sem):
    return pltpu.CompilerParams(dimension_semantics=sem, vmem_limit_bytes=V7X_VMEM_LIMIT)


def _call(body, **kw):
    if _INTERPRET:
        kw.pop("compiler_params", None)
        kw["interpret"] = pltpu.InterpretParams()
    return pl.pallas_call(body, **kw)


def _dot(a, b, form):
    dims = {"nn": ((1,), (0,)), "nt": ((1,), (1,)), "tn": ((0,), (0,))}[form]
    return lax.dot_general(a.astype(_MXU_DTYPE), b.astype(_MXU_DTYPE), (dims, ((), ())),
                           preferred_element_type=F32)


def _mm(a, b, form, *, name, out_dtypes=(F32,), epilogue=None, extras=(), tm=1024, tn=1024, tk=2048, side=()):
    if form == "nn":
        (M, K), (K2, N) = a.shape, b.shape
    elif form == "nt":
        (M, K), (N, K2) = a.shape, b.shape
    else:
        (K, M), (K2, N) = a.shape, b.shape
    assert K == K2, (a.shape, b.shape, form)
    tm, tn = min(tm, M), min(tn, N)
    tk = max(d for d in range(128, min(tk, K) + 1, 128) if K % d == 0) if K > 128 else K
    assert M % tm == 0 and N % tn == 0 and K % tk == 0, (M, N, K, tm, tn, tk)
    nk = K // tk
    ne, no = len(extras), len(out_dtypes)
    if form == "tn":
        a_spec = pl.BlockSpec((tk, tm), lambda i, j, k: (k, i))
    else:
        a_spec = pl.BlockSpec((tm, tk), lambda i, j, k: (i, k))
    if form == "nt":
        b_spec = pl.BlockSpec((tn, tk), lambda i, j, k: (j, k))
    else:
        b_spec = pl.BlockSpec((tk, tn), lambda i, j, k: (k, j))
    mn_spec = pl.BlockSpec((tm, tn), lambda i, j, k: (i, j))
    grid = (M // tm, N // tn, nk)
    ns, io, wrap = _side_exchange(side, False, grid)

    def body(a_ref, b_ref, *rest):
        ex, outs = rest[:ne], rest[ne + ns:ne + ns + no]
        scratch = rest[ne + 2 * ns + no:]
        side_start, side_wait = wrap(rest[ne:ne + ns], rest[ne + ns + no:ne + 2 * ns + no], scratch[1:])
        side_start()
        prod = _dot(a_ref[...], b_ref[...], form)

        def finish(r):
            vals = (r,) if epilogue is None else epilogue(r, *[e[...] for e in ex])
            for o, v in zip(outs, vals):
                o[...] = v.astype(o.dtype)

        if nk == 1:
            finish(prod)
        else:
            acc, k = scratch[0], pl.program_id(2)

            @pl.when(k == 0)
            def _():
                acc[...] = prod

            @pl.when(k > 0)
            def _():
                acc[...] += prod

            @pl.when(k == nk - 1)
            def _():
                finish(acc[...])
        side_wait()

    res = _call(
        body, name=name, grid=grid,
        in_specs=[a_spec, b_spec] + [mn_spec] * ne + io["specs"],
        out_specs=[mn_spec] * no + io["specs"],
        out_shape=[jax.ShapeDtypeStruct((M, N), dt) for dt in out_dtypes] + io["out_shape"],
        scratch_shapes=[pltpu.VMEM((tm, tn) if nk > 1 else (8, 128), F32)] + io["scratch"],
        compiler_params=_cparams(("arbitrary",) * 3 if ns else ("parallel", "parallel", "arbitrary")),
    )(a, b, *extras, *side)
    if ns:
        return (res[0] if no == 1 else res[:no]), list(res[no:])
    return res[0] if no == 1 else res


def _rowmap(fn, rows, bcasts, outs, accs=(), *, tile, name):
    rows = [r if len(r) == 4 else (*r, None) for r in rows]
    S = rows[0][0].shape[-2]
    tile = min(tile, S)
    assert S % tile == 0
    nr, nb, no, na = len(rows), len(bcasts), len(outs), len(accs)

    def body(*refs):
        vals = [r[...] for r in refs[:nr + nb]]
        o_refs, a_refs = refs[nr + nb:nr + nb + no], refs[nr + nb + no:]
        o_vals, a_vals = fn(*vals)
        for r, v in zip(o_refs, o_vals):
            r[...] = v.astype(r.dtype)
        if na:
            @pl.when(pl.program_id(0) == 0)
            def _():
                for r in a_refs:
                    r[...] = jnp.zeros(r.shape, r.dtype)
            for r, v in zip(a_refs, a_vals):
                r[...] += v

    in_specs = []
    for (arr, w, cb, lead) in rows:
        if lead is None:
            in_specs.append(pl.BlockSpec((tile, w), lambda i, cb=cb: (i, cb)))
        else:
            in_specs.append(pl.BlockSpec((None, tile, w), lambda i, cb=cb, lead=lead: (lead, i, cb)))
    in_specs += [pl.BlockSpec(b.shape, lambda i: (0, 0)) for b in bcasts]
    out_specs = [pl.BlockSpec((tile, w), lambda i: (i, 0)) for (w, _) in outs]
    out_specs += [pl.BlockSpec(s, lambda i: (0, 0)) for s in accs]
    out_shape = [jax.ShapeDtypeStruct((S, w), dt) for (w, dt) in outs]
    out_shape += [jax.ShapeDtypeStruct(s, F32) for s in accs]
    return _call(
        body, name=name, grid=(S // tile,), in_specs=in_specs, out_specs=out_specs, out_shape=out_shape,
        compiler_params=_cparams(("arbitrary",)),
    )(*[r[0] for r in rows], *bcasts)


def _colsum(v):
    return jnp.sum(v, axis=0, keepdims=True)


def _rms(x, n):
    r = lax.rsqrt(jnp.sum(x * x, axis=-1, keepdims=True) * (1.0 / n) + EPS)
    return x * r, r


def _rms_bwd(dxhat, xhat, r, n):
    return r * (dxhat - xhat * (jnp.sum(dxhat * xhat, axis=-1, keepdims=True) * (1.0 / n)))


def _rope_fwd(r, cosp, s1, s2):
    return r * cosp + pltpu.roll(r, 32, 1) * s1 + pltpu.roll(r, 96, 1) * s2


def _rope_bwd(d, cosp, s1, s2):
    return d * cosp + pltpu.roll(d * s1, 96, 1) + pltpu.roll(d * s2, 32, 1)


def _sigmoid(x):
    return 1.0 / (1.0 + jnp.exp(-x))


def _halo_specs(tile, halo, width, cb, S, lead=None):
    nh = tile // halo
    last = S // halo - 1
    if lead is None:
        return [
            pl.BlockSpec((tile, width), lambda i: (i, cb)),
            pl.BlockSpec((halo, width), lambda i: (jnp.maximum(i * nh - 1, 0), cb)),
            pl.BlockSpec((halo, width), lambda i: (jnp.minimum((i + 1) * nh, last), cb)),
        ]
    return [
        pl.BlockSpec((None, tile, width), lambda i: (lead, i, cb)),
        pl.BlockSpec((None, halo, width), lambda i: (lead, jnp.maximum(i * nh - 1, 0), cb)),
        pl.BlockSpec((None, halo, width), lambda i: (lead, jnp.minimum((i + 1) * nh, last), cb)),
    ]


def _conv_fwd(proj, conv_w8, conv_b, *, tile=256):
    S = proj.shape[0]
    T = min(tile, S)
    n = S // T
    W = 2 * MW

    def body(x_ref, xp_ref, xn_ref, w_ref, b_ref, q_ref, k_ref, ext):
        i = pl.program_id(0)
        ext[pl.ds(0, 8), :] = xp_ref[...] * (i > 0).astype(F32)
        ext[pl.ds(8, T), :] = x_ref[...]
        ext[pl.ds(8 + T, 8), :] = xn_ref[...] * (i < n - 1).astype(F32)
        w = w_ref[...]
        y = b_ref[...] + w[0:1, :] * ext[pl.ds(6, T), :]
        for o in range(1, CONVW):
            y = y + w[o:o + 1, :] * ext[pl.ds(6 + o, T), :]
        y = y * _sigmoid(y)
        q_ref[...] = y[:, :MW].astype(q_ref.dtype)
        k_ref[...] = (y[:, MW:] * (DM ** -0.5)).astype(k_ref.dtype)

    return _call(
        body, name="conv_fwd", grid=(n,),
        in_specs=_halo_specs(T, 8, W, 0, S) + [pl.BlockSpec((8, W), lambda i: (0, 0)),
                                                 pl.BlockSpec((1, W), lambda i: (0, 0))],
        out_specs=[pl.BlockSpec((T, MW), lambda i: (i, 0))] * 2,
        out_shape=[jax.ShapeDtypeStruct((S, MW), _MXU_DTYPE)] * 2,
        scratch_shapes=[pltpu.VMEM((T + 16, W), F32)],
        compiler_params=_cparams(("arbitrary",)),
    )(proj, proj, proj, conv_w8, conv_b)


def _conv_bwd(proj, dqd, dkd, conv_w8, conv_b, *, tile=256):
    S = proj.shape[0]
    T = min(tile, S)
    n = S // T
    W = 2 * MW

    def body(x_ref, xp_ref, xn_ref, *rest):
        g = rest[:12]
        w_ref, b_ref, dx_ref, dw_ref, db_ref, ext, edp = rest[12:]
        i = pl.program_id(0)
        mp = (i > 0).astype(F32)
        mn = (i < n - 1).astype(F32)
        ext[pl.ds(0, 16), :] = xp_ref[...] * mp
        ext[pl.ds(16, T), :] = x_ref[...]
        ext[pl.ds(16 + T, 16), :] = xn_ref[...] * mn
        w = w_ref[...]
        pre = b_ref[...] + w[0:1, :] * ext[pl.ds(6, T + 16), :]
        for o in range(1, CONVW):
            pre = pre + w[o:o + 1, :] * ext[pl.ds(6 + o, T + 16), :]
        sg = _sigmoid(pre)
        dsilu = sg * (1.0 + pre * (1.0 - sg))
        for half, (a0, a1) in enumerate(((g[0:3], g[3:6]), (g[6:9], g[9:12]))):
            sc = 1.0 if half == 0 else DM ** -0.5
            cols = pl.ds(half * MW, MW)
            edp[pl.ds(0, 8), cols] = (a0[1][...] + a1[1][...]) * (mp * sc)
            edp[pl.ds(8, T), cols] = (a0[0][...] + a1[0][...]) * sc
            edp[pl.ds(8 + T, 8), cols] = (a0[2][...] + a1[2][...]) * (mn * sc)
        edp[...] = edp[...] * dsilu
        @pl.when(i == 0)
        def _():
            dw_ref[...] = jnp.zeros(dw_ref.shape, F32)
            db_ref[...] = jnp.zeros(db_ref.shape, F32)

        x_main = ext[pl.ds(16, T), :]
        dx = None
        for o in range(CONVW):
            view = edp[pl.ds(10 - o, T), :]
            dx = w[o:o + 1, :] * view if dx is None else dx + w[o:o + 1, :] * view
            dw_ref[pl.ds(o, 1), :] += _colsum(x_main * view)
        dx_ref[...] = dx.astype(dx_ref.dtype)
        db_ref[...] += _colsum(edp[pl.ds(8, T), :])

    gspecs = _halo_specs(T, 8, MW, 0, S) * 4
    return _call(
        body, name="conv_bwd", grid=(n,),
        in_specs=_halo_specs(T, 16, W, 0, S) + gspecs + [pl.BlockSpec((8, W), lambda i: (0, 0)),
                                                          pl.BlockSpec((1, W), lambda i: (0, 0))],
        out_specs=[pl.BlockSpec((T, W), lambda i: (i, 0)), pl.BlockSpec((8, W), lambda i: (0, 0)),
                   pl.BlockSpec((1, W), lambda i: (0, 0))],
        out_shape=[jax.ShapeDtypeStruct((S, W), _MXU_DTYPE), jax.ShapeDtypeStruct((8, W), F32),
                   jax.ShapeDtypeStruct((1, W), F32)],
        scratch_shapes=[pltpu.VMEM((T + 32, W), F32), pltpu.VMEM((T + 16, W), F32)],
        compiler_params=_cparams(("arbitrary",)),
    )(proj, proj, proj, *([dqd[0]] * 3), *([dqd[1]] * 3), *([dkd[0]] * 3), *([dkd[1]] * 3), conv_w8, conv_b)


_ATT_SCALE = QK ** -0.5
_LOG2E = math.log2(math.e)
_Q_PRESCALE = _ATT_SCALE * _LOG2E


def _side_exchange(side, gather, grid, cols=()):
    ns = len(side)
    io = _exchange_io(side, gather, cols) if ns else dict(specs=[], out_shape=[], scratch=[])

    def wrap(refs_in, refs_out, sems):
        if not ns:
            return (lambda: None), (lambda: None)
        start, wait = _exchange_ops(refs_in, refs_out, *sems, gather=gather, cols=cols)
        ids = [pl.program_id(a) for a in range(len(grid))]
        first = functools.reduce(jnp.logical_and, [i == 0 for i in ids])
        last = functools.reduce(jnp.logical_and, [i == g - 1 for i, g in zip(ids, grid)])
        return (lambda: pl.when(first)(start)), (lambda: pl.when(last)(wait))

    return ns, io, wrap


def _side_gather_halved(side, grid, cols, mid_step):
    ns = len(side)
    hbm = pl.BlockSpec(memory_space=pltpu.HBM)
    shape = lambda i, a: (a.shape[0], N_CHIP * a.shape[1]) if i in cols else (N_CHIP, *a.shape)
    io = dict(specs=[hbm] * ns, out_shape=[jax.ShapeDtypeStruct(shape(i, a), a.dtype) for i, a in enumerate(side)],
              scratch=([pltpu.SemaphoreType.DMA((3 * ns,))] * 4 + [pltpu.SemaphoreType.DMA((ns,))]) if ns else [])

    def wrap(ins, outs, sems):
        if not ns:
            return (lambda: None,) * 3
        ici_send, ici_recv, d2d_send, d2d_recv, local_sems = sems
        x, y, c = lax.axis_index("x"), lax.axis_index("y"), lax.axis_index("c")
        k = 2 * x + y
        chips = [(1 - x, y), (x, 1 - y), (1 - x, 1 - y)]
        half = [r.shape[0] // 2 for r in ins]

        def piece(a, chip, core=None):
            rows = slice(None) if core is None else pl.ds(core * half[a], half[a])
            if a in cols:
                width = ins[a].shape[1]
                return outs[a].at[rows, pl.ds(pl.multiple_of(chip * width, 128), width)]
            return outs[a].at[chip, rows]

        def ici(a, j, chip):
            px, py = chips[j]
            return pltpu.make_async_remote_copy(
                src_ref=ins[a].at[pl.ds(c * half[a], half[a])], dst_ref=piece(a, chip, c),
                send_sem=ici_send.at[3 * a + j], recv_sem=ici_recv.at[3 * a + j],
                device_id=(px, py, c), device_id_type=MESH)

        def d2d(a, j, core):
            px, py = chips[j]
            return pltpu.make_async_remote_copy(
                src_ref=piece(a, 2 * px + py, core), dst_ref=piece(a, 2 * px + py, core),
                send_sem=d2d_send.at[3 * a + j], recv_sem=d2d_recv.at[3 * a + j],
                device_id=(x, y, 1 - c), device_id_type=MESH)

        local = [pltpu.make_async_copy(ins[a], piece(a, k), local_sems.at[a]) for a in range(ns)]
        pairs = [(a, j) for a in range(ns) for j in range(3)]

        def start():
            for cp in local + [ici(a, j, k) for a, j in pairs]:
                cp.start()

        def mid():
            for a, j in pairs:
                px, py = chips[j]
                ici(a, j, 2 * px + py).wait_recv()
                d2d(a, j, c).start()

        def wait():
            for a, j in pairs:
                d2d(a, j, 1 - c).wait_recv()
            for a, j in pairs:
                ici(a, j, k).wait_send()
                d2d(a, j, c).wait_send()
            for cp in local:
                cp.wait()

        ids = [pl.program_id(a) for a in range(len(grid))]
        at = lambda step: functools.reduce(jnp.logical_and, [i == s for i, s in zip(ids, step)])
        return (lambda: pl.when(at((0,) * len(grid)))(start), lambda: pl.when(at(mid_step))(mid),
                lambda: pl.when(at(tuple(g - 1 for g in grid)))(wait))

    return ns, io, wrap


def _attn_fwd(q, k, v, *, side=(), side_cols=(), tq=1024, split=4):
    S = q.shape[0]
    tq = min(tq, S)
    hq = tq // split
    grid = (H_MLA, S // tq)
    ns, io, wrap = _side_gather_halved(side, grid, side_cols, (grid[0] * 5 // 8, 0))

    def body(q_ref, k_ref, v_ref, *rest):
        o_ref, qa_ref = rest[ns:ns + 2]
        side_start, side_mid, side_wait = wrap(rest[:ns], rest[ns + 2:2 * ns + 2], rest[2 * ns + 2:])
        side_start()
        side_mid()
        kv, vv = k_ref[...], v_ref[...]
        lane = lax.broadcasted_iota(jnp.int32, (hq, HP), 1)
        for a in range(split):
            r = pl.ds(a * hq, hq)
            qv = q_ref[r, :]
            s = _dot(qv, kv, "nt")
            m = jnp.max(s, axis=1, keepdims=True)
            acc = _dot(jnp.exp2(s - m), vv, "nn")
            l = acc[:, VD:VD + 1]
            o_ref[r, :] = (acc[:, :VD] / l).astype(o_ref.dtype)
            lse = m + jnp.log2(l)
            hi = lse.astype(_MXU_DTYPE).astype(F32)
            qa = jnp.where(lane == QK, -hi, jnp.where(lane == QK + 1, hi - lse, qv.astype(F32)))
            qa_ref[r, :] = qa.astype(qa_ref.dtype)
        side_wait()

    res = _call(
        body, name="attn_fwd", grid=grid,
        in_specs=[pl.BlockSpec((tq, HP), lambda h, i: (i, h)),
                  pl.BlockSpec((S, HP), lambda h, i: (0, h)),
                  pl.BlockSpec((S, HP), lambda h, i: (0, h))] + io["specs"],
        out_specs=[pl.BlockSpec((tq, VD), lambda h, i: (i, h)),
                   pl.BlockSpec((tq, HP), lambda h, i: (i, h))] + io["specs"],
        out_shape=[jax.ShapeDtypeStruct((S, H_MLA * VD), _MXU_DTYPE),
                   jax.ShapeDtypeStruct((S, H_MLA * HP), _MXU_DTYPE)] + io["out_shape"],
        scratch_shapes=io["scratch"],
        compiler_params=_cparams(("arbitrary", "arbitrary")),
    )(q, k, v, *side)
    return res[0], res[1], list(res[2:])


def _attn_bwd(qa, k, va, doa, *, side=(), side_cols=(), tq=8192, tk=1024, split=8, unroll=1):
    S = qa.shape[0]
    tq, tk = min(tq, S), min(tk, S)
    nq, nkb = S // tq, S // tk
    hq = tq // split
    grid = (H_MLA, nkb)
    ns, io, wrap = _side_exchange(side, False, grid, side_cols)

    def body(q_ref, k_ref, v_ref, do_ref, *rest):
        dq_ref, dk_ref, dv_ref = rest[ns:ns + 3]
        side_start, side_wait = wrap(rest[:ns], rest[ns + 3:2 * ns + 3], rest[2 * ns + 3:])
        side_start()
        j = pl.program_id(1)

        @pl.when(j == 0)
        def _():
            dq_ref[...] = jnp.zeros(dq_ref.shape, F32)

        dk_ref[...] = jnp.zeros(dk_ref.shape, F32)
        dv_ref[...] = jnp.zeros(dv_ref.shape, F32)
        kb, vb = k_ref[...], v_ref[...]

        def step(i, carry):
            for a in range(split):
                r = pl.ds(pl.multiple_of(i * tq + a * hq, hq), hq)
                qg, dog = q_ref[r, :], do_ref[r, :]
                p = jnp.exp2(_dot(qg, kb, "nt"))
                ds = (p * _dot(dog, vb, "nt")).astype(_MXU_DTYPE)
                dq_ref[r, :] += _dot(ds, kb, "nn")
                dv_ref[...] += _dot(p, dog, "tn")
                dk_ref[...] += _dot(ds, qg, "tn")
            return carry

        lax.fori_loop(0, nq, step, 0, unroll=unroll if nq % unroll == 0 else 1)
        dk_ref[...] = dk_ref[...] * (1.0 / _LOG2E)

        @pl.when(j == nkb - 1)
        def _():
            dq_ref[...] = dq_ref[...] * _ATT_SCALE

        side_wait()

    blk = pl.BlockSpec((tk, HP), lambda h, j: (j, h))
    whole = pl.BlockSpec((S, HP), lambda h, j: (0, h))
    res = _call(
        body, name="attn_bwd", grid=grid,
        in_specs=[whole, blk, blk, whole] + io["specs"],
        out_specs=[whole, blk, blk] + io["specs"],
        out_shape=[jax.ShapeDtypeStruct((S, H_MLA * HP), F32)] * 3 + io["out_shape"],
        scratch_shapes=io["scratch"],
        compiler_params=_cparams(("arbitrary", "arbitrary")),
    )(qa, k, va, doa, *side)
    return res[0], res[1], res[2], list(res[3:])


def _mlstm_chunk_terms(g, q, k, v, gates, gates_t, bg_row, C, n_row, m):
    L = LCH
    d = g // HM
    h = g % HM
    i_idx = d * 8 + h
    f_idx = d * 8 + 4 + h
    rr = lax.broadcasted_iota(jnp.int32, (L, L), 0)
    cc = lax.broadcasted_iota(jnp.int32, (L, L), 1)
    order = (rr - cc) * (1 - 2 * d)
    tri = order >= 0
    eye = rr == cc
    lane = lax.broadcasted_iota(jnp.int32, gates.shape, 1)
    sub = lax.broadcasted_iota(jnp.int32, gates_t.shape, 0)
    lane_b = lax.broadcasted_iota(jnp.int32, bg_row.shape, 1)
    pick_c = lambda idx: jnp.sum(jnp.where(lane == idx, gates, 0.0), axis=1, keepdims=True)
    pick_r = lambda idx: jnp.sum(jnp.where(sub == idx, gates_t, 0.0), axis=0, keepdims=True)
    pick_b = lambda idx: jnp.sum(jnp.where(lane_b == idx, bg_row, 0.0), axis=1, keepdims=True)
    i_col, i_row = pick_c(i_idx) + pick_b(i_idx), pick_r(i_idx) + pick_b(i_idx)
    f_col, f_row = pick_c(f_idx) + pick_b(f_idx), pick_r(f_idx) + pick_b(f_idx)
    logsig = lambda x: jnp.minimum(x, 0.0) - jnp.log(1.0 + jnp.exp(-jnp.abs(x)))
    lf_col, lf_row = logsig(f_col), logsig(f_row)
    b_col = jnp.sum(jnp.where(tri, lf_row, 0.0), axis=1, keepdims=True)
    tri_t = order <= 0
    b_row = jnp.sum(jnp.where(tri_t, lf_col, 0.0), axis=0, keepdims=True)
    bL = jnp.sum(lf_row, axis=1, keepdims=True)
    log_inter = b_col + m
    logD = jnp.where(tri, b_col - b_row + i_row, -jnp.inf)
    m_t = jnp.maximum(log_inter, jnp.max(logD, axis=1, keepdims=True))
    Dm = jnp.exp(logD - m_t)
    w_inter = jnp.exp(log_inter - m_t)
    A = _dot(q, k, "nt")
    Sc = A * Dm
    numI = _dot(q, C, "nt")
    qf = q.astype(F32)
    kf = k.astype(F32)
    denI = jnp.sum(qf * n_row, axis=1, keepdims=True)
    num = _dot(Sc, v, "nn") + w_inter * numI
    den = jnp.sum(Sc, axis=1, keepdims=True) + w_inter * denI
    floor = jnp.exp(-m_t)
    Nst = jnp.maximum(jnp.abs(den), floor)
    log_w = bL - b_col + i_col
    m_new = jnp.maximum(bL + m, jnp.max(log_w, axis=0, keepdims=True))
    decay = jnp.exp(bL + m - m_new)
    w_col = jnp.exp(log_w - m_new)
    return dict(tri=tri, eye=eye, f_row=f_row, Dm=Dm, w_inter=w_inter, A=A, Sc=Sc, numI=numI, denI=denI,
                num=num, den=den, floor=floor, Nst=Nst, m_new=m_new, decay=decay, w_col=w_col, qf=qf, kf=kf)


def _mlstm_specs(nc, d, step_of):
    chunk = lambda j: step_of(j) if d == 0 else nc - 1 - step_of(j)
    return chunk, [
        pl.BlockSpec((LCH, DM), lambda h, j: (chunk(j), h)),
        pl.BlockSpec((LCH, DM), lambda h, j: (chunk(j), h)),
        pl.BlockSpec((LCH, DM), lambda h, j: (chunk(j), P_VM // DM + h)),
        pl.BlockSpec((LCH, 128), lambda h, j: (chunk(j), P_G // 128)),
        pl.BlockSpec((NG, LCH), lambda h, j: (0, chunk(j))),
    ]


def _mlstm_fwd(qc, kc, proj, gates_t, bg_row):
    S = qc.shape[0]
    nc = S // LCH
    in_specs, out_specs = [], []
    for d in (0, 1):
        chunk, specs = _mlstm_specs(nc, d, lambda j: j)
        in_specs += specs
        out_specs += [pl.BlockSpec((LCH, DM), lambda h, j, chunk=chunk: (chunk(j), h)),
                      pl.BlockSpec((None, None, DM, DM), lambda h, j, chunk=chunk: (h, chunk(j), 0, 0)),
                      pl.BlockSpec((None, None, 8, DM), lambda h, j, chunk=chunk: (h, chunk(j), 0, 0))]
    in_specs.append(pl.BlockSpec((1, 128), lambda h, j: (0, 0)))

    def body(*refs):
        bg_ref, outs, (C_s, n_s, m_s) = refs[10], refs[11:17], refs[17:]

        @pl.when(pl.program_id(1) == 0)
        def _():
            C_s[...] = jnp.zeros(C_s.shape, F32)
            n_s[...] = jnp.zeros(n_s.shape, F32)
            m_s[...] = jnp.full(m_s.shape, M_INIT, F32)

        for d in (0, 1):
            q_ref, k_ref, v_ref, g_ref, gt_ref = refs[5 * d:5 * d + 5]
            h_ref, cst_ref, nm_ref = outs[3 * d:3 * d + 3]
            g = d * HM + pl.program_id(0)
            C, n_row, m = C_s[d], n_s[d, 0:1, :], m_s[d, 0:1, 0:1]
            cst_ref[...] = C
            nm_ref[0:1, :] = n_row
            nm_ref[1:2, :] = jnp.broadcast_to(m, (1, DM))
            nm_ref[2:8, :] = jnp.zeros((6, DM), F32)
            q, k, v = q_ref[...], k_ref[...], v_ref[...]
            t = _mlstm_chunk_terms(g, q, k, v, g_ref[...], gt_ref[...], bg_ref[...], C, n_row, m)
            h_ref[...] = t["num"] / t["Nst"]
            wv = t["w_col"] * v
            C_s[d] = t["decay"] * C + _dot(wv, k, "tn")
            n_s[d, 0:1, :] = t["decay"] * n_row + _colsum(t["w_col"] * t["kf"])
            m_s[d] = jnp.broadcast_to(t["m_new"], (8, 128))

    res = _call(
        body, name="mlstm_fwd", grid=(HM, nc), in_specs=in_specs, out_specs=out_specs,
        out_shape=[jax.ShapeDtypeStruct((S, MW), F32), jax.ShapeDtypeStruct((HM, nc, DM, DM), F32),
                   jax.ShapeDtypeStruct((HM, nc, 8, DM), F32)] * 2,
        scratch_shapes=[pltpu.VMEM((2, DM, DM), F32), pltpu.VMEM((2, 8, DM), F32), pltpu.VMEM((2, 8, 128), F32)],
        compiler_params=_cparams(("parallel", "arbitrary")),
    )(*([qc, kc, proj, proj, gates_t] * 2), bg_row)
    return (res[0], res[3]), (res[1], res[4]), (res[2], res[5])


def _mlstm_bwd(qc, kc, proj, gates_t, bg_row, dh, cst, nm):
    S = qc.shape[0]
    nc = S // LCH
    in_specs, out_specs = [], []
    for d in (0, 1):
        chunk, specs = _mlstm_specs(nc, d, lambda j: nc - 1 - j)
        in_specs += specs + [pl.BlockSpec((LCH, DM), lambda h, j, chunk=chunk: (chunk(j), h)),
                             pl.BlockSpec((None, None, DM, DM), lambda h, j, chunk=chunk: (h, chunk(j), 0, 0)),
                             pl.BlockSpec((None, None, 8, DM), lambda h, j, chunk=chunk: (h, chunk(j), 0, 0))]
        out_specs += [pl.BlockSpec((LCH, DM), lambda h, j, chunk=chunk: (chunk(j), h))] * 3
        out_specs += [pl.BlockSpec((None, None, 8, LCH), lambda h, j, chunk=chunk: (h, chunk(j), 0, 0))]
    in_specs.append(pl.BlockSpec((1, 128), lambda h, j: (0, 0)))

    def body(*refs):
        bg_ref, outs, (dC_s, dn_s) = refs[16], refs[17:25], refs[25:]

        @pl.when(pl.program_id(1) == 0)
        def _():
            dC_s[...] = jnp.zeros(dC_s.shape, F32)
            dn_s[...] = jnp.zeros(dn_s.shape, F32)

        for d in (0, 1):
            _mlstm_bwd_chain(d, refs[8 * d:8 * d + 8], bg_ref, outs[4 * d:4 * d + 4], dC_s, dn_s)

    res = _call(
        body, name="mlstm_bwd", grid=(HM, nc), in_specs=in_specs, out_specs=out_specs,
        out_shape=([jax.ShapeDtypeStruct((S, MW), F32)] * 3 + [jax.ShapeDtypeStruct((HM, nc, 8, LCH), F32)]) * 2,
        scratch_shapes=[pltpu.VMEM((2, DM, DM), F32), pltpu.VMEM((2, 8, DM), F32)],
        compiler_params=_cparams(("parallel", "arbitrary")),
    )(*[a for d in (0, 1) for a in (qc, kc, proj, proj, gates_t, dh, cst[d], nm[d])], bg_row)
    return (res[0], res[4]), (res[1], res[5]), (res[2], res[6]), (res[3], res[7])


def _mlstm_bwd_chain(d, ins, bg_ref, outs, dC_s, dn_s):
        q_ref, k_ref, v_ref, g_ref, gt_ref, dh_ref, cst_ref, nm_ref = ins
        dq_ref, dk_ref, dv_ref, dg_ref = outs
        g = d * HM + pl.program_id(0)
        C, n_row, m = cst_ref[...], nm_ref[0:1, :], nm_ref[1:2, 0:1]
        q, k, v = q_ref[...], k_ref[...], v_ref[...]
        t = _mlstm_chunk_terms(g, q, k, v, g_ref[...], gt_ref[...], bg_ref[...], C, n_row, m)
        tri, eye, qf, kf = t["tri"], t["eye"], t["qf"], t["kf"]
        w_inter, w_col, decay, Nst = t["w_inter"], t["w_col"], t["decay"], t["Nst"]
        dC, dn = dC_s[d], dn_s[d, 0:1, :]
        dhv = dh_ref[...]
        hval = t["num"] / Nst
        dnum = dhv / Nst
        dNst = -jnp.sum(dhv * hval, axis=1, keepdims=True) / Nst
        dden = jnp.where(jnp.abs(t["den"]) > t["floor"], jnp.sign(t["den"]) * dNst, 0.0)
        dSc = _dot(dnum, v, "nt") + dden
        dA = dSc * t["Dm"]
        G = dSc * t["Sc"]
        KdC = _dot(k, dC, "nt")
        dq = _dot(dA, k, "nn") + w_inter * _dot(dnum, C, "nn") + (w_inter * dden) * n_row
        dk = _dot(dA, q, "tn") + w_col * _dot(v, dC, "nn") + w_col * dn
        dv = _dot(t["Sc"], dnum, "tn") + w_col * KdC
        dq_ref[...] = dq
        dk_ref[...] = dk
        dv_ref[...] = dv
        dlog_inter = w_inter * (jnp.sum(dnum * t["numI"], axis=1, keepdims=True) + dden * t["denI"])
        rowG = jnp.sum(G, axis=1, keepdims=True)
        colG = jnp.sum(G, axis=0, keepdims=True)
        u_col = w_col * (jnp.sum(v * KdC, axis=1, keepdims=True) + jnp.sum(kf * dn, axis=1, keepdims=True))
        colG_c = jnp.sum(jnp.where(eye, colG, 0.0), axis=1, keepdims=True)
        u_row = jnp.sum(jnp.where(eye, u_col, 0.0), axis=0, keepdims=True)
        db_col = rowG + dlog_inter - u_col - colG_c
        dbL = jnp.sum(u_col, axis=0, keepdims=True) + decay * (
            jnp.sum(jnp.sum(dC * C, axis=1, keepdims=True), axis=0, keepdims=True)
            + jnp.sum(dn * n_row, axis=1, keepdims=True))
        dlf_row = jnp.sum(jnp.where(tri, db_col, 0.0), axis=0, keepdims=True) + dbL
        di_row = colG + u_row
        df_row = dlf_row * (1.0 - _sigmoid(t["f_row"]))
        dg_ref[...] = jnp.zeros(dg_ref.shape, F32)
        dg_ref[0:1, :] = di_row
        dg_ref[1:2, :] = df_row
        dC_s[d] = decay * dC + _dot(w_inter * dnum, q, "tn")
        dn_s[d, 0:1, :] = decay * dn + _colsum((w_inter * dden) * qf)


def _pad_w_in(w):
    cq, ckv, kpe, qm, km, vm, om, gt = _split_in(w)
    z = lambda n: jnp.zeros((w.shape[0], n), w.dtype)
    return jnp.concatenate([qm, km, vm, om, cq, ckv, kpe, z(HP - QK), gt, z(128 - NG)], axis=1)


def _split_in(w):
    out, o = [], 0
    for n in IN_SIZES:
        out.append(w[:, o:o + n])
        o += n
    return out


def _unpad_w_in(g):
    return jnp.concatenate([g[:, P_CQ:P_CQ + Q_LORA], g[:, P_CKV:P_CKV + KV_LORA], g[:, P_KPE:P_KPE + ROPE],
                            g[:, 0:4 * MW], g[:, P_G:P_G + NG]], axis=1)


_IN_SHARD = D_IN // 4
_IN_SEGMENTS = ((0, 512, P_CQ), (512, 768, P_CKV), (768, 832, P_KPE), (832, 4928, P_QM), (4928, 4944, P_G))


def _pad_w_in_slabs(slabs):
    def orig(a, b):
        out = []
        for k in range(4):
            lo, hi = max(a, k * _IN_SHARD), min(b, (k + 1) * _IN_SHARD)
            if lo < hi:
                out.append(slabs[k][:, lo - k * _IN_SHARD:hi - k * _IN_SHARD])
        return out

    z = lambda n: jnp.zeros((slabs.shape[1], n), slabs.dtype)
    return jnp.concatenate(orig(832, 4928) + orig(0, 512) + orig(512, 768) + orig(768, 832) + [z(HP - QK)]
                           + orig(4928, 4944) + [z(128 - NG)], axis=1)


def _unpad_w_in_slabs(g):
    slabs = []
    for k in range(4):
        pieces = []
        for a, b, p in _IN_SEGMENTS:
            lo, hi = max(a, k * _IN_SHARD), min(b, (k + 1) * _IN_SHARD)
            if lo < hi:
                pieces.append(g[:, p + lo - a:p + hi - a])
        slabs.append(jnp.concatenate(pieces, axis=1))
    return jnp.stack(slabs)


def _pad_w_uq(w):
    return jnp.pad(w.reshape(Q_LORA, H_MLA, QK), ((0, 0), (0, 0), (0, HP - QK))).reshape(Q_LORA, H_MLA * HP)


def _unpad_w_uq(g):
    return g.reshape(Q_LORA, H_MLA, HP)[:, :, :QK].reshape(Q_LORA, H_MLA * QK)


def _perm_w_ukv(w):
    return w.reshape(KV_LORA, H_MLA, 2, NOPE).transpose(0, 2, 1, 3).reshape(KV_LORA, 2 * H_MLA * NOPE)


def _unperm_w_ukv(g):
    return g.reshape(KV_LORA, 2, H_MLA, NOPE).transpose(0, 2, 1, 3).reshape(KV_LORA, 2 * H_MLA * NOPE)


def _rope_tables(positions):
    half = ROPE // 2
    freqs = ROPE_THETA ** (-jnp.arange(half, dtype=F32) / half)
    ang = positions.astype(F32)[:, None] * freqs
    cos, sin = jnp.cos(ang), jnp.sin(ang)
    z32, z64 = jnp.zeros_like(cos), jnp.zeros((cos.shape[0], 64), F32)
    return (jnp.concatenate([cos, cos, z64], axis=1), jnp.concatenate([z32, sin, z64], axis=1),
            jnp.concatenate([-sin, z32, z64], axis=1))


def _device_step(x, tgt, positions, modv, W, late=None):
    S = x.shape[0]
    MX = _MXU_DTYPE
    cosp, rs1, rs2 = _rope_tables(positions)
    tabs = [(cosp, 128, 0), (rs1, 128, 0), (rs2, 128, 0)]
    cat1 = lambda vs: jnp.concatenate(vs, axis=1)
    hsl = lambda hh, w: slice(hh * w, (hh + 1) * w)

    def ln1(xv, g, mv):
        xhat, _ = _rms(xv, D)
        return [xhat * g * (1.0 + mv[1:2]) + mv[0:1]], []

    (h,) = _rowmap(ln1, [(x, D, 0)], [W["g_mix"], modv], [(D, MX)], tile=512, name="ln1")
    proj = _mm(h, W["w_in"], "nn", name="proj")

    def lora(cq, ckv, gq, gkv):
        return [_rms(cq, Q_LORA)[0] * gq, _rms(ckv, KV_LORA)[0] * gkv], []

    cqn, ckvn = _rowmap(lora, [(proj, Q_LORA, P_CQ // Q_LORA), (proj, KV_LORA, P_CKV // KV_LORA)],
                        [W["g_qlora"], W["g_kvlora"]], [(Q_LORA, MX), (KV_LORA, MX)], tile=512, name="lora_norm")
    q_raw = _mm(cqn, W["w_uq"], "nn", name="q_up")
    kv_raw = _mm(ckvn, W["w_ukv"], "nn", name="kv_up")

    def mla_q(qr, cp, a1, a2, gq):
        outs = []
        for hh in range(H_MLA):
            y = _rms(qr[:, hsl(hh, HP)], QK)[0] * gq
            outs += [y[:, :NOPE], _rope_fwd(y[:, NOPE:], cp, a1, a2)]
        return [cat1(outs) * _Q_PRESCALE], []

    (qh,) = _rowmap(mla_q, [(q_raw, H_MLA * HP, 0)] + tabs, [W["gq"]], [(H_MLA * HP, MX)], tile=512, name="mla_q")

    def mla_k(kvr, kpe, cp, a1, a2, gk):
        lane = lax.broadcasted_iota(jnp.int32, (kvr.shape[0], 128), 1)
        outs, vas = [], []
        for hh in range(H_MLA):
            y = _rms(cat1([kvr[:, hsl(hh, NOPE)], kpe]), QK)[0] * gk
            outs += [y[:, :NOPE], _rope_fwd(y[:, NOPE:], cp, a1, a2) + ((lane == ROPE) | (lane == ROPE + 1)).astype(F32)]
            vas += [kvr[:, H_MLA * NOPE + hh * VD:H_MLA * NOPE + (hh + 1) * VD], (lane < 2).astype(F32)]
        return [cat1(outs), cat1(vas)], []

    kh, va = _rowmap(mla_k, [(kv_raw, 2 * H_MLA * NOPE, 0), (proj, 128, P_KPE // 128)] + tabs, [W["gk"]],
                     [(H_MLA * HP, MX), (H_MLA * HP, MX)], tile=512, name="mla_k")
    attn_o, qa, gathered = _attn_fwd(qh, kh, va, side=late or (), side_cols=(1,))
    if late:
        W = dict(W, w_out=gathered[0].reshape(D, D), w_ff1=gathered[1], w_ff2=gathered[2].reshape(DFF, D))

    qc, kc = _conv_fwd(proj, W["conv_w8"], W["conv_b"])
    gates_t = proj[:, P_G:P_G + NG].T
    (h_f, h_b), cst, nm = _mlstm_fwd(qc, kc, proj, gates_t, W["bg_row"])
    hrows = [(h_f, MW, 0), (h_b, MW, 0), (proj, MW, P_OM // MW)]

    def ml_out(ao, hf, hb, om, gmn):
        outs = [ao.astype(F32)]
        hs = hf + hb
        for hh in range(HM):
            sl = hsl(hh, DM)
            outs.append(_sigmoid(om[:, sl]) * _rms(hs[:, sl], DM)[0] * gmn[:, sl])
        return [cat1(outs)], []

    (cat,) = _rowmap(ml_out, [(attn_o, MW, 0)] + hrows, [W["g_mn"]], [(D, MX)], tile=512, name="ml_out")
    mixed = _mm(cat, W["w_out"], "nn", name="out_proj")

    def res_ln2(xv, mx, g, mv):
        x1 = xv + mv[2:3] * mx
        return [x1, _rms(x1, D)[0] * g * (1.0 + mv[4:5]) + mv[3:4]], []

    x1, h2 = _rowmap(res_ln2, [(x, D, 0), (mixed, D, 0)], [W["g_mlp"], modv], [(D, F32), (D, MX)],
                     tile=512, name="res_ln2")
    a, u = _mm(h2, W["w_ff1"], "nn", name="ff1", out_dtypes=(MX, MX),
               epilogue=lambda r: (jnp.square(jnp.maximum(r, 0.0)), r))
    y = _mm(a, W["w_ff2"], "nn", name="ff2")

    def final(x1v, yv, tv, mv):
        err = x1v + mv[5:6] * yv - tv
        dout = err * (1.0 / D)
        loss = jnp.sum(jnp.sum(0.5 * err * dout, axis=1, keepdims=True), axis=0, keepdims=True)
        return [dout, mv[5:6] * dout], [loss, _colsum(dout * yv)]

    dout, dy, loss, dgate2 = _rowmap(final, [(x1, D, 0), (y, D, 0), (tgt, D, 0)], [modv], [(D, F32), (D, MX)],
                                     [(1, 1), (1, D)], tile=256, name="loss_head")

    du = _mm(dy, W["w_ff2"], "nt", name="ff2_dx", out_dtypes=(MX,), extras=(u,),
             epilogue=lambda r, uv: (r * (2.0 * jnp.maximum(uv.astype(F32), 0.0)),))
    gdt = (MX,)
    g_ff2 = _mm(a, dy, "tn", name="ff2_dw", out_dtypes=gdt)
    dh2 = _mm(du, W["w_ff1"], "nt", name="ff1_dx")
    g_ff1 = _mm(h2, du, "tn", name="ff1_dw", out_dtypes=gdt)

    def ln2_bwd(dh2v, x1v, doutv, mxv, g, mv):
        xhat, r = _rms(x1v, D)
        dn2 = dh2v * (1.0 + mv[4:5])
        dx1 = doutv + _rms_bwd(dn2 * g, xhat, r, D)
        return [dx1, mv[2:3] * dx1], [_colsum(dh2v), _colsum(dh2v * xhat * g), _colsum(dn2 * xhat), _colsum(dx1 * mxv)]

    dx1, dmixed, dshift2, dscale2, dg_mlp, dgate1 = _rowmap(
        ln2_bwd, [(dh2, D, 0), (x1, D, 0), (dout, D, 0), (mixed, D, 0)], [W["g_mlp"], modv],
        [(D, F32), (D, MX)], [(1, D)] * 4, tile=256, name="ln2_bwd")
    dcat = _mm(dmixed, W["w_out"], "nt", name="out_dx")
    g_out = _mm(cat, dmixed, "tn", name="out_dw", out_dtypes=gdt)

    def ml_out_bwd(dml, hf, hb, om, gmn):
        hs = hf + hb
        dhs, dos, dgs = [], [], []
        for hh in range(HM):
            sl = hsl(hh, DM)
            xhat, r = _rms(hs[:, sl], DM)
            g, sg, d = gmn[:, sl], _sigmoid(om[:, sl]), dml[:, sl]
            dos.append(d * xhat * g * sg * (1.0 - sg))
            dhn = d * sg
            dgs.append(_colsum(dhn * xhat))
            dhs.append(_rms_bwd(dhn * g, xhat, r, DM))
        return [cat1(dhs), cat1(dos)], [cat1(dgs)]

    dhs, do_m, dg_mn = _rowmap(ml_out_bwd, [(dcat, MW, 1)] + hrows, [W["g_mn"]], [(MW, F32), (MW, MX)],
                               [(1, MW)], tile=512, name="ml_out_bwd")
    dqd, dkd, dvd, dgates = _mlstm_bwd(qc, kc, proj, gates_t, W["bg_row"], dhs, cst, nm)
    dqk_m, dconv_w8, dconv_b = _conv_bwd(proj, dqd, dkd, W["conv_w8"], W["conv_b"])

    def do_aug(ao, dov):
        lane = lax.broadcasted_iota(jnp.int32, (ao.shape[0], 128), 1)
        outs = []
        for hh in range(H_MLA):
            sl = hsl(hh, VD)
            dl = jnp.sum(ao[:, sl].astype(F32) * dov[:, sl], axis=1, keepdims=True)
            hi = dl.astype(MX).astype(F32)
            outs += [dov[:, sl], jnp.where(lane == 0, -hi, jnp.where(lane == 1, hi - dl, 0.0))]
        return [cat1(outs)], []

    (doa,) = _rowmap(do_aug, [(attn_o, MW, 0), (dcat, MW, 0)], [], [(H_MLA * HP, MX)], tile=512, name="attn_delta")
    side = [g_out.reshape(N_CHIP, D // N_CHIP, D), g_ff1, g_ff2.reshape(N_CHIP, DFF // N_CHIP, D)] if late else ()
    dq_a, dk_a, dv_a, late_got = _attn_bwd(qa, kh, va, doa, side=side, side_cols=(1,))

    def mla_q_bwd(dqv, qr, cp, a1, a2, gq):
        outs, dg = [], 0.0
        for hh in range(H_MLA):
            sl = hsl(hh, HP)
            xhat, r = _rms(qr[:, sl], QK)
            d = dqv[:, sl]
            dyv = cat1([d[:, :NOPE], _rope_bwd(d[:, NOPE:], cp, a1, a2)])
            dg = dg + _colsum(dyv * xhat)
            outs.append(_rms_bwd(dyv * gq, xhat, r, QK))
        return [cat1(outs)], [dg]

    dq_raw, dgq = _rowmap(mla_q_bwd, [(dq_a, H_MLA * HP, 0), (q_raw, H_MLA * HP, 0)] + tabs, [W["gq"]],
                          [(H_MLA * HP, MX)], [(1, HP)], tile=512, name="mla_q_bwd")
    dcqn = _mm(dq_raw, W["w_uq"], "nt", name="q_up_dx")
    g_uq = _mm(cqn, dq_raw, "tn", name="q_up_dw", out_dtypes=gdt)

    def mla_k_bwd(dkv, dvv, kvr, kpe, cp, a1, a2, gk):
        dkn, dg, dkpe = [], 0.0, 0.0
        for hh in range(H_MLA):
            xhat, r = _rms(cat1([kvr[:, hsl(hh, NOPE)], kpe]), QK)
            d = dkv[:, hsl(hh, HP)]
            dyv = cat1([d[:, :NOPE], _rope_bwd(d[:, NOPE:], cp, a1, a2)])
            dg = dg + _colsum(dyv * xhat)
            dxv = _rms_bwd(dyv * gk, xhat, r, QK)
            dkn.append(dxv[:, :NOPE])
            dkpe = dkpe + dxv[:, NOPE:]
        return [cat1(dkn + [dvv[:, hh * HP:hh * HP + VD] for hh in range(H_MLA)]), dkpe], [dg]

    dkv_raw, dkpe, dgk = _rowmap(
        mla_k_bwd, [(dk_a, H_MLA * HP, 0), (dv_a, H_MLA * HP, 0), (kv_raw, 2 * H_MLA * NOPE, 0),
                    (proj, 128, P_KPE // 128)] + tabs, [W["gk"]],
        [(2 * H_MLA * NOPE, MX), (128, MX)], [(1, HP)], tile=256, name="mla_k_bwd")
    dckvn = _mm(dkv_raw, W["w_ukv"], "nt", name="kv_up_dx")
    g_ukv = _mm(ckvn, dkv_raw, "tn", name="kv_up_dw", out_dtypes=gdt)

    def lora_bwd(dcq, dckv, cq, ckv, gq, gkv):
        xq, rq = _rms(cq, Q_LORA)
        xk, rk = _rms(ckv, KV_LORA)
        return ([_rms_bwd(dcq * gq, xq, rq, Q_LORA), _rms_bwd(dckv * gkv, xk, rk, KV_LORA)],
                [_colsum(dcq * xq), _colsum(dckv * xk)])

    dc_q, dc_kv, dg_qlora, dg_kvlora = _rowmap(
        lora_bwd, [(dcqn, Q_LORA, 0), (dckvn, KV_LORA, 0), (proj, Q_LORA, P_CQ // Q_LORA),
                   (proj, KV_LORA, P_CKV // KV_LORA)], [W["g_qlora"], W["g_kvlora"]],
        [(Q_LORA, MX), (KV_LORA, MX)], [(1, Q_LORA), (1, KV_LORA)], tile=512, name="lora_bwd")

    nc = S // LCH
    dg16 = jnp.stack(dgates)[:, :, :, 0:2, :].transpose(2, 4, 0, 3, 1).reshape(S, NG)
    dg128 = jnp.pad(dg16, ((0, 0), (0, 128 - NG)))

    def assemble(dqk, dv0, dv1, dom, dcq, dckv, dkp, dgp):
        f = lambda t: t.astype(F32)
        return [cat1([f(dqk), dv0 + dv1, f(dom), f(dcq), f(dckv), f(dkp), dgp])], [_colsum(dgp)]

    dproj, dbg = _rowmap(
        assemble, [(dqk_m, 2 * MW, 0), (dvd[0], MW, 0), (dvd[1], MW, 0), (do_m, MW, 0), (dc_q, Q_LORA, 0),
                   (dc_kv, KV_LORA, 0), (dkpe, 128, 0), (dg128, 128, 0)], [], [(D_INP, MX)], [(1, 128)],
        tile=256, name="dproj")
    g_in = _mm(h, dproj, "tn", name="proj_dw", out_dtypes=gdt)
    early_got = ()
    if late:
        side = [_unpad_w_in_slabs(g_in).astype(MX), _slabs(_unpad_w_uq(g_uq)).astype(MX),
                _slabs(_unperm_w_ukv(g_ukv)).astype(MX)]
        dh, early_got = _mm(dproj, W["w_in"], "nt", name="proj_dx", side=side)
    else:
        dh = _mm(dproj, W["w_in"], "nt", name="proj_dx")

    def ln1_bwd(dhv, xv, dx1v, g, mv):
        xhat, r = _rms(xv, D)
        dn = dhv * (1.0 + mv[1:2])
        return [dx1v + _rms_bwd(dn * g, xhat, r, D)], [_colsum(dhv), _colsum(dhv * xhat * g), _colsum(dn * xhat)]

    gx, dshift1, dscale1, dg_mix = _rowmap(ln1_bwd, [(dh, D, 0), (x, D, 0), (dx1, D, 0)], [W["g_mix"], modv],
                                           [(D, F32)], [(1, D)] * 3, tile=256, name="ln1_bwd")
    dmodv = jnp.concatenate([dshift1, dscale1, dgate1, dshift2, dscale2, dgate2], axis=0)
    grads = dict(w_in=g_in, w_uq=g_uq, w_ukv=g_ukv, w_out=g_out, w_ff1=g_ff1, w_ff2=g_ff2,
                 norm_mix_g=dg_mix, b_gates=dbg[:, :NG], conv_w=dconv_w8[:CONVW], conv_b=dconv_b,
                 q_lora_g=dg_qlora, kv_lora_g=dg_kvlora, q_norm_g=dgq[:, :QK], k_norm_g=dgk[:, :QK],
                 mlstm_norm_g=dg_mn, norm_mlp_g=dg_mlp)
    grads["got"] = list(early_got) + list(late_got)
    return loss, gx, dmodv, grads


def _cols(g):
    return g.transpose(1, 0, 2).reshape(g.shape[1], N_CHIP * g.shape[2])


def _slabs(gfull):
    return gfull.reshape(gfull.shape[0], N_CHIP, -1).transpose(1, 0, 2)


def _prep_weights(w_in, w_uq, w_ukv, w_out, w_ff1, w_ff2, norm_mix_g, norm_mlp_g, q_lora_g, kv_lora_g,
                  q_norm_g, k_norm_g, mlstm_norm_g, conv_w, conv_b, b_gates):
    MX = _MXU_DTYPE
    padg = lambda g: jnp.pad(g.reshape(1, QK).astype(F32), ((0, 0), (0, HP - QK)))
    return dict(
        w_in=(_pad_w_in_slabs(w_in) if w_in.ndim == 3 else _pad_w_in(w_in)).astype(MX), w_uq=_pad_w_uq(w_uq).astype(MX), w_ukv=_perm_w_ukv(w_ukv).astype(MX),
        w_out=None if w_out is None else w_out.astype(MX), w_ff1=None if w_ff1 is None else w_ff1.astype(MX),
        w_ff2=None if w_ff2 is None else w_ff2.astype(MX),
        g_mix=norm_mix_g.reshape(1, D), g_mlp=norm_mlp_g.reshape(1, D), g_qlora=q_lora_g.reshape(1, Q_LORA),
        g_kvlora=kv_lora_g.reshape(1, KV_LORA), gq=padg(q_norm_g), gk=padg(k_norm_g),
        g_mn=mlstm_norm_g.reshape(1, MW), conv_w8=jnp.pad(conv_w.reshape(CONVW, 2 * MW), ((0, 8 - CONVW), (0, 0))),
        conv_b=conv_b.reshape(1, 2 * MW), bg_row=jnp.pad(b_gates.reshape(1, NG), ((0, 0), (0, 128 - NG))))


MESH = pl.DeviceIdType.MESH
N_DEV = 8
N_CHIP = 4


def _comm_call(body, **kw):
    if _INTERPRET:
        kw["interpret"] = pltpu.InterpretParams()
    return pl.pallas_call(body, **kw)


def _allgather8(blk, *, name):
    m_per, n = blk.shape

    def body(x_ref, out_ref, send_sems, recv_sems, local_sem):
        x, y, c = lax.axis_index("x"), lax.axis_index("y"), lax.axis_index("c")
        me, sibling = (x, y, c), (x, y, 1 - c)
        chips = [(1 - x, y), (x, 1 - y), (1 - x, 1 - y)]

        def rows(px, py, pc):
            return out_ref.at[pl.ds((4 * px + 2 * py + pc) * m_per, m_per), :]

        def copy(k, block, to, src=None):
            return pltpu.make_async_remote_copy(
                src_ref=rows(*block) if src is None else src, dst_ref=rows(*block),
                send_sem=send_sems.at[k], recv_sem=recv_sems.at[k], device_id=to, device_id_type=MESH)

        mine = pltpu.make_async_copy(x_ref, rows(*me), local_sem)
        mine.start()
        first = [copy(0, me, sibling, src=x_ref)]
        first += [copy(1 + j, me, (*chip, c), src=x_ref) for j, chip in enumerate(chips)]
        for cp in first:
            cp.start()
        passed = [copy(4 + j, (*chip, c), sibling) for j, chip in enumerate(chips)]
        for j, chip in enumerate(chips):
            copy(1 + j, (*chip, c), me).wait_recv()
            passed[j].start()
        copy(0, sibling, me).wait_recv()
        for j, chip in enumerate(chips):
            copy(4 + j, (*chip, 1 - c), me).wait_recv()
        for cp in first + passed:
            cp.wait_send()
        mine.wait()

    return _comm_call(
        body, name=name, out_shape=jax.ShapeDtypeStruct((N_DEV * m_per, n), blk.dtype),
        in_specs=[pl.BlockSpec(memory_space=pltpu.VMEM)], out_specs=pl.BlockSpec(memory_space=pltpu.VMEM),
        scratch_shapes=[pltpu.SemaphoreType.DMA((7,)), pltpu.SemaphoreType.DMA((7,)), pltpu.SemaphoreType.DMA],
    )(blk)


def _exchange_io(arrays, gather, cols=()):
    n = len(arrays)

    def out(i, a):
        if gather:
            return (a.shape[0], N_CHIP * a.shape[1]) if i in cols else (N_CHIP, *a.shape)
        return (N_CHIP, a.shape[0], a.shape[1] // N_CHIP) if i in cols else a.shape

    return dict(
        specs=[pl.BlockSpec(memory_space=pltpu.HBM)] * n,
        out_shape=[jax.ShapeDtypeStruct(out(i, a), a.dtype) for i, a in enumerate(arrays)],
        scratch=[pltpu.SemaphoreType.DMA((3 * n,)), pltpu.SemaphoreType.DMA((3 * n,)), pltpu.SemaphoreType.DMA((n,))])


def _exchange_ops(ins, outs, send_sems, recv_sems, local_sems, *, gather, cols=()):
    n = len(ins)
    x, y, c = lax.axis_index("x"), lax.axis_index("y"), lax.axis_index("c")
    k = 2 * x + y
    chips = [(1 - x, y), (x, 1 - y), (1 - x, 1 - y)]

    def piece(ref, a, chip, windowed):
        if not windowed:
            return ref.at[chip]
        width = ref.shape[1] // N_CHIP
        return ref.at[:, pl.ds(pl.multiple_of(chip * width, 128), width)]

    src_of = lambda a, chip: ins[a] if gather else piece(ins[a], a, chip, a in cols)
    dst_of = lambda a, chip: piece(outs[a], a, chip, gather and a in cols)

    def remote(a, j):
        px, py = chips[j]
        return pltpu.make_async_remote_copy(
            src_ref=src_of(a, 2 * px + py), dst_ref=dst_of(a, k), send_sem=send_sems.at[3 * a + j],
            recv_sem=recv_sems.at[3 * a + j], device_id=(px, py, c), device_id_type=MESH)

    def arrival(a, j):
        px, py = chips[j]
        return pltpu.make_async_remote_copy(
            src_ref=src_of(a, k), dst_ref=dst_of(a, 2 * px + py), send_sem=send_sems.at[3 * a + j],
            recv_sem=recv_sems.at[3 * a + j], device_id=(px, py, c), device_id_type=MESH)

    local = [pltpu.make_async_copy(src_of(a, k), dst_of(a, k), local_sems.at[a]) for a in range(n)]
    sent = [remote(a, j) for a in range(n) for j in range(3)]

    def start():
        for cp in local + sent:
            cp.start()

    def wait():
        for a in range(n):
            for j in range(3):
                arrival(a, j).wait_recv()
        for cp in sent:
            cp.wait_send()
        for cp in local:
            cp.wait()

    return start, wait


def _chip_allgather_halved(shards, *, name):
    n = len(shards)
    half_rows = [s.shape[0] // 2 for s in shards]
    assert all(s.shape[0] % 16 == 0 for s in shards)

    def body(*refs):
        ins, outs = refs[:n], refs[n:2 * n]
        ici_send, ici_recv, d2d_send, d2d_recv, local_sems = refs[2 * n:]
        x, y, c = lax.axis_index("x"), lax.axis_index("y"), lax.axis_index("c")
        k = 2 * x + y
        chips = [(1 - x, y), (x, 1 - y), (1 - x, 1 - y)]

        def half(a, slab, core):
            return outs[a].at[slab, pl.ds(core * half_rows[a], half_rows[a])]

        def ici(a, j, slab):
            px, py = chips[j]
            return pltpu.make_async_remote_copy(
                src_ref=ins[a].at[pl.ds(c * half_rows[a], half_rows[a])], dst_ref=half(a, slab, c),
                send_sem=ici_send.at[3 * a + j], recv_sem=ici_recv.at[3 * a + j],
                device_id=(px, py, c), device_id_type=MESH)

        def d2d(a, j, core):
            px, py = chips[j]
            return pltpu.make_async_remote_copy(
                src_ref=half(a, 2 * px + py, core), dst_ref=half(a, 2 * px + py, core),
                send_sem=d2d_send.at[3 * a + j], recv_sem=d2d_recv.at[3 * a + j],
                device_id=(x, y, 1 - c), device_id_type=MESH)

        local = [pltpu.make_async_copy(ins[a], outs[a].at[k], local_sems.at[a]) for a in range(n)]
        sent = [ici(a, j, k) for a in range(n) for j in range(3)]
        for cp in local + sent:
            cp.start()
        passed = []
        for a in range(n):
            for j, (px, py) in enumerate(chips):
                ici(a, j, 2 * px + py).wait_recv()
                passed.append(d2d(a, j, c))
                passed[-1].start()
        for a in range(n):
            for j in range(3):
                d2d(a, j, 1 - c).wait_recv()
        for cp in sent + passed:
            cp.wait_send()
        for cp in local:
            cp.wait()

    hbm = pl.BlockSpec(memory_space=pltpu.HBM)
    return _comm_call(
        body, name=name, out_shape=[jax.ShapeDtypeStruct((N_CHIP, *s.shape), s.dtype) for s in shards],
        in_specs=[hbm] * n, out_specs=[hbm] * n,
        scratch_shapes=[pltpu.SemaphoreType.DMA((3 * n,))] * 4 + [pltpu.SemaphoreType.DMA((n,))],
    )(*shards)


def _sibling_exchange(arrays, *, name):
    n = len(arrays)

    def body(*refs):
        ins, outs = refs[:n], refs[n:2 * n]
        send_sems, recv_sems = refs[2 * n:]
        x, y, c = lax.axis_index("x"), lax.axis_index("y"), lax.axis_index("c")
        cps = [pltpu.make_async_remote_copy(
            src_ref=ins[a], dst_ref=outs[a], send_sem=send_sems.at[a], recv_sem=recv_sems.at[a],
            device_id=(x, y, 1 - c), device_id_type=MESH) for a in range(n)]
        for cp in cps:
            cp.start()
        for cp in cps:
            cp.wait()

    hbm = pl.BlockSpec(memory_space=pltpu.HBM)
    return _comm_call(
        body, name=name, out_shape=[jax.ShapeDtypeStruct(a.shape, a.dtype) for a in arrays],
        in_specs=[hbm] * n, out_specs=[hbm] * n,
        scratch_shapes=[pltpu.SemaphoreType.DMA((n,)), pltpu.SemaphoreType.DMA((n,))],
    )(*arrays)


def _sum_blocks(a, nblk, *, name):
    n = a.shape[1]

    def body(a_ref, o_ref):
        acc = a_ref[pl.ds(0, 8), :]
        for d in range(1, nblk):
            acc = acc + a_ref[pl.ds(8 * d, 8), :]
        o_ref[...] = acc

    return _call(body, name=name, out_shape=jax.ShapeDtypeStruct((8, n), F32))(a)


def _outer8(sct, dm, *, name, tm=256, tn=1024):
    R, N = sct.shape[0], dm.shape[1]
    tm, tn = min(tm, R), min(tn, N)

    def body(s_ref, d_ref, o_ref):
        s, dmv = s_ref[...], d_ref[...]
        acc = s[:, 0:1] * dmv[0:1, :]
        for b in range(1, 8):
            acc = acc + s[:, b:b + 1] * dmv[b:b + 1, :]
        o_ref[...] = acc

    return _call(
        body, name=name, grid=(R // tm, N // tn),
        in_specs=[pl.BlockSpec((tm, 8), lambda i, j: (i, 0)), pl.BlockSpec((8, tn), lambda i, j: (0, j))],
        out_specs=pl.BlockSpec((tm, tn), lambda i, j: (i, j)),
        out_shape=jax.ShapeDtypeStruct((R, N), F32),
        compiler_params=_cparams(("parallel", "parallel")),
    )(sct, dm)


_BC1 = 1.0 - ADAM_B1 ** ADAM_STEP
_BC2 = 1.0 - ADAM_B2 ** ADAM_STEP


def _adamw(w, g_parts, m, v, *, name, tile=128):
    R, C = w.shape
    tile = min(tile, R)
    assert R % tile == 0
    npart = len(g_parts)

    def body(*refs):
        w_ref, m_ref, v_ref = refs[npart:npart + 3]
        g_o, d_o, m_o, v_o = refs[npart + 3:]
        g = refs[0][...].astype(F32)
        for r in refs[1:npart]:
            g = g + r[...].astype(F32)
        mn = ADAM_B1 * m_ref[...] + (1.0 - ADAM_B1) * g
        vn = ADAM_B2 * v_ref[...] + (1.0 - ADAM_B2) * jnp.square(g)
        g_o[...] = g
        m_o[...] = mn
        v_o[...] = vn
        d_o[...] = -ADAM_LR * ((mn / _BC1) / (jnp.sqrt(vn / _BC2) + ADAM_EPS) + ADAM_WD * w_ref[...])

    spec = pl.BlockSpec((tile, C), lambda i: (i, 0))
    return _call(
        body, name=name, grid=(R // tile,), in_specs=[spec] * (npart + 3), out_specs=[spec] * 4,
        out_shape=[jax.ShapeDtypeStruct((R, C), F32)] * 4,
        compiler_params=_cparams(("parallel",)),
    )(*g_parts, w, m, v)


def _pack(vecs, rows8_cols):
    flat = jnp.concatenate([v.reshape(-1).astype(F32) for v in vecs])
    return jnp.pad(flat, (0, 8 * rows8_cols - flat.shape[0])).reshape(8, rows8_cols)


def _unpack(flat, shapes):
    out, o = [], 0
    for s in shapes:
        n = math.prod(s)
        out.append(flat[o:o + n].reshape(s))
        o += n
    return out


_BIG = ("w_in", "w_uq", "w_ukv", "w_out", "w_ff1", "w_ff2")
_SMALL = ("b_ada", "norm_mix_g", "b_gates", "conv_w", "conv_b", "q_lora_g", "kv_lora_g", "q_norm_g", "k_norm_g",
          "mlstm_norm_g", "norm_mlp_g")
_ORDER = ("w_ada", "b_ada", "norm_mix_g", "w_in", "b_gates", "conv_w", "conv_b", "q_lora_g", "w_uq", "kv_lora_g",
          "w_ukv", "q_norm_g", "k_norm_g", "mlstm_norm_g", "w_out", "norm_mlp_g", "w_ff1", "w_ff2")


def kernel(x, c, positions, w_ada, b_ada, norm_mix_g, w_in, b_gates, conv_w, conv_b, q_lora_g, w_uq, kv_lora_g, w_ukv, q_norm_g, k_norm_g, mlstm_norm_g, w_out, norm_mlp_g, w_ff1, w_ff2, loss_target, m_w_ada, m_b_ada, m_norm_mix_g, m_w_in, m_b_gates, m_conv_w, m_conv_b, m_q_lora_g, m_w_uq, m_kv_lora_g, m_w_ukv, m_q_norm_g, m_k_norm_g, m_mlstm_norm_g, m_w_out, m_norm_mlp_g, m_w_ff1, m_w_ff2, v_w_ada, v_b_ada, v_norm_mix_g, v_w_in, v_b_gates, v_conv_w, v_conv_b, v_q_lora_g, v_w_uq, v_kv_lora_g, v_w_ukv, v_q_norm_g, v_k_norm_g, v_mlstm_norm_g, v_w_out, v_norm_mlp_g, v_w_ff1, v_w_ff2):
    args = dict(locals())
    wts = {n: args[n] for n in _ORDER}
    mom = {n: args["m_" + n] for n in _ORDER}
    var = {n: args["v_" + n] for n in _ORDER}
    MX = _MXU_DTYPE
    xi, yi, ci = lax.axis_index("x"), lax.axis_index("y"), lax.axis_index("c")
    chip = 2 * xi + yi
    dev = 2 * chip + ci
    S = x.shape[1]
    CS = 2 * MW // N_CHIP
    GS = DM // N_CHIP

    pk = _pack([c, conv_w, mlstm_norm_g], 1024)
    allpk = _allgather8(pk, name="gather_small").reshape(N_DEV, 8 * 1024)
    c_all = allpk[:, :D]
    per_chip = allpk[0::2]
    conv_w_full = per_chip[:, D:D + CONVW * CS].reshape(N_CHIP, CONVW, CS).transpose(1, 0, 2).reshape(CONVW, 2 * MW)
    o = D + CONVW * CS
    mn_full = per_chip[:, o:o + HM * GS].reshape(N_CHIP, HM, GS).transpose(1, 0, 2).reshape(HM, DM)

    (sc,) = _rowmap(lambda cv: ([cv * _sigmoid(cv)], []), [(c_all, D, 0)], [], [(D, F32)], tile=8, name="silu_c")
    ncol = w_ada.shape[2]
    b_cols = lax.dynamic_slice(b_ada, (0, chip * ncol), (1, ncol))
    modp = _mm(sc, w_ada[0], "nn", name="ada_fwd", tm=8, tn=1024, tk=512, extras=(jnp.broadcast_to(b_cols, (8, ncol)),),
               epilogue=lambda r, b: (r + b,))
    modg = _allgather8(modp, name="gather_mod").reshape(N_CHIP, 2, 8, ncol)[:, 0]
    mod_all = modg.transpose(1, 0, 2).reshape(N_DEV, N_CHIP * ncol)
    modv = jnp.pad(lax.dynamic_slice(mod_all, (dev, 0), (1, 6 * D)).reshape(6, D), ((0, 2), (0, 0)))

    shards = [wts[n][0].astype(MX) for n in _BIG]
    gw_in, gw_uq, gw_ukv = _chip_allgather_halved(shards[:3], name="gather_weights")
    W = _prep_weights(gw_in, _cols(gw_uq), _cols(gw_ukv), None, None, None, norm_mix_g, norm_mlp_g,
                      q_lora_g, kv_lora_g, q_norm_g, k_norm_g, mn_full, conv_w_full, conv_b, b_gates)

    loss, gx, dmodv, g = _device_step(x[0], loss_target[0], positions[0], modv, W, late=shards[3:])

    small_shapes = [(6 * D,), (D,), (NG,), (CONVW, 2 * MW), (2 * MW,), (Q_LORA,), (KV_LORA,), (QK,), (QK,), (MW,), (D,), (1,)]
    pg = _pack([dmodv, g["norm_mix_g"], g["b_gates"], g["conv_w"], g["conv_b"], g["q_lora_g"], g["kv_lora_g"],
                g["q_norm_g"], g["k_norm_g"], g["mlstm_norm_g"], g["norm_mlp_g"], loss], 4096)
    allpg = _allgather8(pg, name="gather_small_grads")
    tot = _unpack(_sum_blocks(allpg, N_DEV, name="sum_small_grads").reshape(-1), small_shapes)
    dmod_all = allpg.reshape(N_DEV, 8 * 4096)[:, :6 * D]
    gsmall = dict(zip(_SMALL, [tot[0].reshape(1, 6 * D), tot[1].reshape(1, D), tot[2].reshape(1, NG),
                               lax.dynamic_slice(tot[3], (0, chip * CS), (CONVW, CS)).reshape(1, CONVW, CS),
                               tot[4].reshape(1, 2 * MW), tot[5].reshape(1, Q_LORA), tot[6].reshape(1, KV_LORA),
                               tot[7].reshape(1, QK), tot[8].reshape(1, QK),
                               lax.dynamic_slice(tot[9].reshape(HM, DM), (0, chip * GS), (HM, GS)).reshape(1, HM, GS),
                               tot[10].reshape(1, D)]))
    loss_tot = tot[11].reshape(())

    got = g["got"]
    part = []
    for nme, r in zip(_BIG, got):
        wd = r.shape[2]
        (p,) = _rowmap(lambda a0, a1, a2, a3: ([(a0.astype(F32) + a1.astype(F32)) + (a2.astype(F32) + a3.astype(F32))], []),
                       [(r, wd, 0, k) for k in range(N_CHIP)], [], [(wd, MX)], tile=256, name="sum_chips_" + nme)
        part.append(p)
    other = _sibling_exchange(part, name="exchange_cores")

    dm_cols = lax.dynamic_slice(dmod_all, (0, chip * ncol), (N_DEV, ncol))
    g_ada = _outer8(sc.T, dm_cols, name="ada_dw")

    res = {}
    for nme, p, q in zip(_BIG, part, other):
        res[nme] = _adamw(wts[nme][0], [p, q], mom[nme][0], var[nme][0], name="adamw_" + nme)
    res["w_ada"] = _adamw(w_ada[0], [g_ada], m_w_ada[0], v_w_ada[0], name="adamw_w_ada")
    sw = _pack([wts[n] for n in _SMALL], 3072)
    sg = _pack([gsmall[n] for n in _SMALL], 3072)
    sm = _pack([mom[n] for n in _SMALL], 3072)
    sv = _pack([var[n] for n in _SMALL], 3072)
    small_res = _adamw(sw, [sg], sm, sv, name="adamw_small", tile=8)
    shapes = [wts[n].shape for n in _SMALL]
    unp = [_unpack(r.reshape(-1), shapes) for r in small_res]
    for i, nme in enumerate(_SMALL):
        res[nme] = tuple(u[i] for u in unp)
    outs = [loss_tot, gx[None]]
    for kind in range(4):
        outs += [res[n][kind].reshape(wts[n].shape) for n in _ORDER]
    return tuple(outs)
```

```python
import functools
import math

import jax
import jax.numpy as jnp
from jax import lax
from jax.experimental import pallas as pl
from jax.experimental.pallas import tpu as pltpu

F32 = jnp.float32
BF16 = jnp.bfloat16
_MXU_DTYPE = jnp.bfloat16
_INTERPRET = False

D = 2048
H_MLA = 8
NOPE = 128
ROPE = 64
QK = NOPE + ROPE
HP = 256
VD = 128
Q_LORA = 512
KV_LORA = 256
HM = 4
DM = 256
MW = HM * DM
LCH = 128
CONVW = 5
NG = 16
DFF = 4 * D
EPS = 1e-6
M_INIT = -1e30
ROPE_THETA = 10000.0
IN_SIZES = (Q_LORA, KV_LORA, ROPE, MW, MW, MW, MW, NG)
D_IN = sum(IN_SIZES)
P_QM, P_KM, P_VM, P_OM, P_CQ, P_CKV, P_KPE, P_G = 0, 1024, 2048, 3072, 4096, 4608, 4864, 4992
D_INP = 5120

ADAM_LR, ADAM_B1, ADAM_B2, ADAM_EPS, ADAM_WD, ADAM_STEP = 0.001, 0.9, 0.999, 1e-08, 0.01, 10

V7X_VMEM_LIMIT = 56 * 1024 * 1024


def _cparams(sem):
    return pltpu.CompilerParams(dimension_semantics=sem, vmem_limit_bytes=V7X_VMEM_LIMIT)


def _call(body, **kw):
    if _INTERPRET:
        kw.pop("compiler_params", None)
        kw["interpret"] = pltpu.InterpretParams()
    return pl.pallas_call(body, **kw)


def _dot(a, b, form):
    dims = {"nn": ((1,), (0,)), "nt": ((1,), (1,)), "tn": ((0,), (0,))}[form]
    return lax.dot_general(a.astype(_MXU_DTYPE), b.astype(_MXU_DTYPE), (dims, ((), ())),
                           preferred_element_type=F32)


def _mm(a, b, form, *, name, out_dtypes=(F32,), epilogue=None, extras=(), tm=1024, tn=1024, tk=2048, side=()):
    if form == "nn":
        (M, K), (K2, N) = a.shape, b.shape
    elif form == "nt":
        (M, K), (N, K2) = a.shape, b.shape
    else:
        (K, M), (K2, N) = a.shape, b.shape
    assert K == K2, (a.shape, b.shape, form)
    tm, tn = min(tm, M), min(tn, N)
    tk = max(d for d in range(128, min(tk, K) + 1, 128) if K % d == 0) if K > 128 else K
    assert M % tm == 0 and N % tn == 0 and K % tk == 0, (M, N, K, tm, tn, tk)
    nk = K // tk
    ne, no = len(extras), len(out_dtypes)
    if form == "tn":
        a_spec = pl.BlockSpec((tk, tm), lambda i, j, k: (k, i))
    else:
        a_spec = pl.BlockSpec((tm, tk), lambda i, j, k: (i, k))
    if form == "nt":
        b_spec = pl.BlockSpec((tn, tk), lambda i, j, k: (j, k))
    else:
        b_spec = pl.BlockSpec((tk, tn), lambda i, j, k: (k, j))
    mn_spec = pl.BlockSpec((tm, tn), lambda i, j, k: (i, j))
    grid = (M // tm, N // tn, nk)
    ns, io, wrap = _side_exchange(side, False, grid)

    def body(a_ref, b_ref, *rest):
        ex, outs = rest[:ne], rest[ne + ns:ne + ns + no]
        scratch = rest[ne + 2 * ns + no:]
        side_start, side_wait = wrap(rest[ne:ne + ns], rest[ne + ns + no:ne + 2 * ns + no], scratch[1:])
        side_start()
        prod = _dot(a_ref[...], b_ref[...], form)

        def finish(r):
            vals = (r,) if epilogue is None else epilogue(r, *[e[...] for e in ex])
            for o, v in zip(outs, vals):
                o[...] = v.astype(o.dtype)

        if nk == 1:
            finish(prod)
        else:
            acc, k = scratch[0], pl.program_id(2)

            @pl.when(k == 0)
            def _():
                acc[...] = prod

            @pl.when(k > 0)
            def _():
                acc[...] += prod

            @pl.when(k == nk - 1)
            def _():
                finish(acc[...])
        side_wait()

    res = _call(
        body, name=name, grid=grid,
        in_specs=[a_spec, b_spec] + [mn_spec] * ne + io["specs"],
        out_specs=[mn_spec] * no + io["specs"],
        out_shape=[jax.ShapeDtypeStruct((M, N), dt) for dt in out_dtypes] + io["out_shape"],
        scratch_shapes=[pltpu.VMEM((tm, tn) if nk > 1 else (8, 128), F32)] + io["scratch"],
        compiler_params=_cparams(("arbitrary",) * 3 if ns else ("parallel", "parallel", "arbitrary")),
    )(a, b, *extras, *side)
    if ns:
        return (res[0] if no == 1 else res[:no]), list(res[no:])
    return res[0] if no == 1 else res


def _rowmap(fn, rows, bcasts, outs, accs=(), *, tile, name):
    rows = [r if len(r) == 4 else (*r, None) for r in rows]
    S = rows[0][0].shape[-2]
    tile = min(tile, S)
    assert S % tile == 0
    nr, nb, no, na = len(rows), len(bcasts), len(outs), len(accs)

    def body(*refs):
        vals = [r[...] for r in refs[:nr + nb]]
        o_refs, a_refs = refs[nr + nb:nr + nb + no], refs[nr + nb + no:]
        o_vals, a_vals = fn(*vals)
        for r, v in zip(o_refs, o_vals):
            r[...] = v.astype(r.dtype)
        if na:
            @pl.when(pl.program_id(0) == 0)
            def _():
                for r in a_refs:
                    r[...] = jnp.zeros(r.shape, r.dtype)
            for r, v in zip(a_refs, a_vals):
                r[...] += v

    in_specs = []
    for (arr, w, cb, lead) in rows:
        if lead is None:
            in_specs.append(pl.BlockSpec((tile, w), lambda i, cb=cb: (i, cb)))
        else:
            in_specs.append(pl.BlockSpec((None, tile, w), lambda i, cb=cb, lead=lead: (lead, i, cb)))
    in_specs += [pl.BlockSpec(b.shape, lambda i: (0, 0)) for b in bcasts]
    out_specs = [pl.BlockSpec((tile, w), lambda i: (i, 0)) for (w, _) in outs]
    out_specs += [pl.BlockSpec(s, lambda i: (0, 0)) for s in accs]
    out_shape = [jax.ShapeDtypeStruct((S, w), dt) for (w, dt) in outs]
    out_shape += [jax.ShapeDtypeStruct(s, F32) for s in accs]
    return _call(
        body, name=name, grid=(S // tile,), in_specs=in_specs, out_specs=out_specs, out_shape=out_shape,
        compiler_params=_cparams(("arbitrary",)),
    )(*[r[0] for r in rows], *bcasts)


def _colsum(v):
    return jnp.sum(v, axis=0, keepdims=True)


def _rms(x, n):
    r = lax.rsqrt(jnp.sum(x * x, axis=-1, keepdims=True) * (1.0 / n) + EPS)
    return x * r, r


def _rms_bwd(dxhat, xhat, r, n):
    return r * (dxhat - xhat * (jnp.sum(dxhat * xhat, axis=-1, keepdims=True) * (1.0 / n)))


def _rope_fwd(r, cosp, s1, s2):
    return r * cosp + pltpu.roll(r, 32, 1) * s1 + pltpu.roll(r, 96, 1) * s2


def _rope_bwd(d, cosp, s1, s2):
    return d * cosp + pltpu.roll(d * s1, 96, 1) + pltpu.roll(d * s2, 32, 1)


def _sigmoid(x):
    return 1.0 / (1.0 + jnp.exp(-x))


def _halo_specs(tile, halo, width, cb, S, lead=None):
    nh = tile // halo
    last = S // halo - 1
    if lead is None:
        return [
            pl.BlockSpec((tile, width), lambda i: (i, cb)),
            pl.BlockSpec((halo, width), lambda i: (jnp.maximum(i * nh - 1, 0), cb)),
            pl.BlockSpec((halo, width), lambda i: (jnp.minimum((i + 1) * nh, last), cb)),
        ]
    return [
        pl.BlockSpec((None, tile, width), lambda i: (lead, i, cb)),
        pl.BlockSpec((None, halo, width), lambda i: (lead, jnp.maximum(i * nh - 1, 0), cb)),
        pl.BlockSpec((None, halo, width), lambda i: (lead, jnp.minimum((i + 1) * nh, last), cb)),
    ]


def _conv_fwd(proj, conv_w8, conv_b, *, tile=256):
    S = proj.shape[0]
    T = min(tile, S)
    n = S // T
    W = 2 * MW

    def body(x_ref, xp_ref, xn_ref, w_ref, b_ref, q_ref, k_ref, ext):
        i = pl.program_id(0)
        ext[pl.ds(0, 8), :] = xp_ref[...] * (i > 0).astype(F32)
        ext[pl.ds(8, T), :] = x_ref[...]
        ext[pl.ds(8 + T, 8), :] = xn_ref[...] * (i < n - 1).astype(F32)
        w = w_ref[...]
        y = b_ref[...] + w[0:1, :] * ext[pl.ds(6, T), :]
        for o in range(1, CONVW):
            y = y + w[o:o + 1, :] * ext[pl.ds(6 + o, T), :]
        y = y * _sigmoid(y)
        q_ref[...] = y[:, :MW].astype(q_ref.dtype)
        k_ref[...] = (y[:, MW:] * (DM ** -0.5)).astype(k_ref.dtype)

    return _call(
        body, name="conv_fwd", grid=(n,),
        in_specs=_halo_specs(T, 8, W, 0, S) + [pl.BlockSpec((8, W), lambda i: (0, 0)),
                                                 pl.BlockSpec((1, W), lambda i: (0, 0))],
        out_specs=[pl.BlockSpec((T, MW), lambda i: (i, 0))] * 2,
        out_shape=[jax.ShapeDtypeStruct((S, MW), _MXU_DTYPE)] * 2,
        scratch_shapes=[pltpu.VMEM((T + 16, W), F32)],
        compiler_params=_cparams(("arbitrary",)),
    )(proj, proj, proj, conv_w8, conv_b)


def _conv_bwd(proj, dqd, dkd, conv_w8, conv_b, *, tile=256):
    S = proj.shape[0]
    T = min(tile, S)
    n = S // T
    W = 2 * MW

    def body(x_ref, xp_ref, xn_ref, *rest):
        g = rest[:12]
        w_ref, b_ref, dx_ref, dw_ref, db_ref, ext, edp = rest[12:]
        i = pl.program_id(0)
        mp = (i > 0).astype(F32)
        mn = (i < n - 1).astype(F32)
        ext[pl.ds(0, 16), :] = xp_ref[...] * mp
        ext[pl.ds(16, T), :] = x_ref[...]
        ext[pl.ds(16 + T, 16), :] = xn_ref[...] * mn
        w = w_ref[...]
        pre = b_ref[...] + w[0:1, :] * ext[pl.ds(6, T + 16), :]
        for o in range(1, CONVW):
            pre = pre + w[o:o + 1, :] * ext[pl.ds(6 + o, T + 16), :]
        sg = _sigmoid(pre)
        dsilu = sg * (1.0 + pre * (1.0 - sg))
        for half, (a0, a1) in enumerate(((g[0:3], g[3:6]), (g[6:9], g[9:12]))):
            sc = 1.0 if half == 0 else DM ** -0.5
            cols = pl.ds(half * MW, MW)
            edp[pl.ds(0, 8), cols] = (a0[1][...] + a1[1][...]) * (mp * sc)
            edp[pl.ds(8, T), cols] = (a0[0][...] + a1[0][...]) * sc
            edp[pl.ds(8 + T, 8), cols] = (a0[2][...] + a1[2][...]) * (mn * sc)
        edp[...] = edp[...] * dsilu
        @pl.when(i == 0)
        def _():
            dw_ref[...] = jnp.zeros(dw_ref.shape, F32)
            db_ref[...] = jnp.zeros(db_ref.shape, F32)

        x_main = ext[pl.ds(16, T), :]
        dx = None
        for o in range(CONVW):
            view = edp[pl.ds(10 - o, T), :]
            dx = w[o:o + 1, :] * view if dx is None else dx + w[o:o + 1, :] * view
            dw_ref[pl.ds(o, 1), :] += _colsum(x_main * view)
        dx_ref[...] = dx.astype(dx_ref.dtype)
        db_ref[...] += _colsum(edp[pl.ds(8, T), :])

    gspecs = _halo_specs(T, 8, MW, 0, S) * 4
    return _call(
        body, name="conv_bwd", grid=(n,),
        in_specs=_halo_specs(T, 16, W, 0, S) + gspecs + [pl.BlockSpec((8, W), lambda i: (0, 0)),
                                                          pl.BlockSpec((1, W), lambda i: (0, 0))],
        out_specs=[pl.BlockSpec((T, W), lambda i: (i, 0)), pl.BlockSpec((8, W), lambda i: (0, 0)),
                   pl.BlockSpec((1, W), lambda i: (0, 0))],
        out_shape=[jax.ShapeDtypeStruct((S, W), _MXU_DTYPE), jax.ShapeDtypeStruct((8, W), F32),
                   jax.ShapeDtypeStruct((1, W), F32)],
        scratch_shapes=[pltpu.VMEM((T + 32, W), F32), pltpu.VMEM((T + 16, W), F32)],
        compiler_params=_cparams(("arbitrary",)),
    )(proj, proj, proj, *([dqd[0]] * 3), *([dqd[1]] * 3), *([dkd[0]] * 3), *([dkd[1]] * 3), conv_w8, conv_b)


_ATT_SCALE = QK ** -0.5
_LOG2E = math.log2(math.e)
_Q_PRESCALE = _ATT_SCALE * _LOG2E


def _side_exchange(side, gather, grid, cols=()):
    ns = len(side)
    io = _exchange_io(side, gather, cols) if ns else dict(specs=[], out_shape=[], scratch=[])

    def wrap(refs_in, refs_out, sems):
        if not ns:
            return (lambda: None), (lambda: None)
        start, wait = _exchange_ops(refs_in, refs_out, *sems, gather=gather, cols=cols)
        ids = [pl.program_id(a) for a in range(len(grid))]
        first = functools.reduce(jnp.logical_and, [i == 0 for i in ids])
        last = functools.reduce(jnp.logical_and, [i == g - 1 for i, g in zip(ids, grid)])
        return (lambda: pl.when(first)(start)), (lambda: pl.when(last)(wait))

    return ns, io, wrap


def _side_gather_halved(side, grid, cols, mid_step):
    ns = len(side)
    hbm = pl.BlockSpec(memory_space=pltpu.HBM)
    shape = lambda i, a: (a.shape[0], N_CHIP * a.shape[1]) if i in cols else (N_CHIP, *a.shape)
    io = dict(specs=[hbm] * ns, out_shape=[jax.ShapeDtypeStruct(shape(i, a), a.dtype) for i, a in enumerate(side)],
              scratch=([pltpu.SemaphoreType.DMA((3 * ns,))] * 4 + [pltpu.SemaphoreType.DMA((ns,))]) if ns else [])

    def wrap(ins, outs, sems):
        if not ns:
            return (lambda: None,) * 3
        ici_send, ici_recv, d2d_send, d2d_recv, local_sems = sems
        x, y, c = lax.axis_index("x"), lax.axis_index("y"), lax.axis_index("c")
        k = 2 * x + y
        chips = [(1 - x, y), (x, 1 - y), (1 - x, 1 - y)]
        half = [r.shape[0] // 2 for r in ins]

        def piece(a, chip, core=None):
            rows = slice(None) if core is None else pl.ds(core * half[a], half[a])
            if a in cols:
                width = ins[a].shape[1]
                return outs[a].at[rows, pl.ds(pl.multiple_of(chip * width, 128), width)]
            return outs[a].at[chip, rows]

        def ici(a, j, chip):
            px, py = chips[j]
            return pltpu.make_async_remote_copy(
                src_ref=ins[a].at[pl.ds(c * half[a], half[a])], dst_ref=piece(a, chip, c),
                send_sem=ici_send.at[3 * a + j], recv_sem=ici_recv.at[3 * a + j],
                device_id=(px, py, c), device_id_type=MESH)

        def d2d(a, j, core):
            px, py = chips[j]
            return pltpu.make_async_remote_copy(
                src_ref=piece(a, 2 * px + py, core), dst_ref=piece(a, 2 * px + py, core),
                send_sem=d2d_send.at[3 * a + j], recv_sem=d2d_recv.at[3 * a + j],
                device_id=(x, y, 1 - c), device_id_type=MESH)

        local = [pltpu.make_async_copy(ins[a], piece(a, k), local_sems.at[a]) for a in range(ns)]
        pairs = [(a, j) for a in range(ns) for j in range(3)]

        def start():
            for cp in local + [ici(a, j, k) for a, j in pairs]:
                cp.start()

        def mid():
            for a, j in pairs:
                px, py = chips[j]
                ici(a, j, 2 * px + py).wait_recv()
                d2d(a, j, c).start()

        def wait():
            for a, j in pairs:
                d2d(a, j, 1 - c).wait_recv()
            for a, j in pairs:
                ici(a, j, k).wait_send()
                d2d(a, j, c).wait_send()
            for cp in local:
                cp.wait()

        ids = [pl.program_id(a) for a in range(len(grid))]
        at = lambda step: functools.reduce(jnp.logical_and, [i == s for i, s in zip(ids, step)])
        return (lambda: pl.when(at((0,) * len(grid)))(start), lambda: pl.when(at(mid_step))(mid),
                lambda: pl.when(at(tuple(g - 1 for g in grid)))(wait))

    return ns, io, wrap


def _attn_fwd(q, k, v, *, side=(), side_cols=(), tq=1024, split=4):
    S = q.shape[0]
    tq = min(tq, S)
    hq = tq // split
    grid = (H_MLA, S // tq)
    ns, io, wrap = _side_gather_halved(side, grid, side_cols, (grid[0] * 5 // 8, 0))

    def body(q_ref, k_ref, v_ref, *rest):
        o_ref, qa_ref = rest[ns:ns + 2]
        side_start, side_mid, side_wait = wrap(rest[:ns], rest[ns + 2:2 * ns + 2], rest[2 * ns + 2:])
        side_start()
        side_mid()
        kv, vv = k_ref[...], v_ref[...]
        lane = lax.broadcasted_iota(jnp.int32, (hq, HP), 1)
        for a in range(split):
            r = pl.ds(a * hq, hq)
            qv = q_ref[r, :]
            s = _dot(qv, kv, "nt")
            m = jnp.max(s, axis=1, keepdims=True)
            acc = _dot(jnp.exp2(s - m), vv, "nn")
            l = acc[:, VD:VD + 1]
            o_ref[r, :] = (acc[:, :VD] / l).astype(o_ref.dtype)
            lse = m + jnp.log2(l)
            hi = lse.astype(_MXU_DTYPE).astype(F32)
            qa = jnp.where(lane == QK, -hi, jnp.where(lane == QK + 1, hi - lse, qv.astype(F32)))
            qa_ref[r, :] = qa.astype(qa_ref.dtype)
        side_wait()

    res = _call(
        body, name="attn_fwd", grid=grid,
        in_specs=[pl.BlockSpec((tq, HP), lambda h, i: (i, h)),
                  pl.BlockSpec((S, HP), lambda h, i: (0, h)),
                  pl.BlockSpec((S, HP), lambda h, i: (0, h))] + io["specs"],
        out_specs=[pl.BlockSpec((tq, VD), lambda h, i: (i, h)),
                   pl.BlockSpec((tq, HP), lambda h, i: (i, h))] + io["specs"],
        out_shape=[jax.ShapeDtypeStruct((S, H_MLA * VD), _MXU_DTYPE),
                   jax.ShapeDtypeStruct((S, H_MLA * HP), _MXU_DTYPE)] + io["out_shape"],
        scratch_shapes=io["scratch"],
        compiler_params=_cparams(("arbitrary", "arbitrary")),
    )(q, k, v, *side)
    return res[0], res[1], list(res[2:])


def _attn_bwd(qa, k, va, doa, *, side=(), side_cols=(), tq=8192, tk=1024, split=8, unroll=1):
    S = qa.shape[0]
    tq, tk = min(tq, S), min(tk, S)
    nq, nkb = S // tq, S // tk
    hq = tq // split
    grid = (H_MLA, nkb)
    ns, io, wrap = _side_exchange(side, False, grid, side_cols)

    def body(q_ref, k_ref, v_ref, do_ref, *rest):
        dq_ref, dk_ref, dv_ref = rest[ns:ns + 3]
        side_start, side_wait = wrap(rest[:ns], rest[ns + 3:2 * ns + 3], rest[2 * ns + 3:])
        side_start()
        j = pl.program_id(1)

        @pl.when(j == 0)
        def _():
            dq_ref[...] = jnp.zeros(dq_ref.shape, F32)

        dk_ref[...] = jnp.zeros(dk_ref.shape, F32)
        dv_ref[...] = jnp.zeros(dv_ref.shape, F32)
        kb, vb = k_ref[...], v_ref[...]

        def step(i, carry):
            for a in range(split):
                r = pl.ds(pl.multiple_of(i * tq + a * hq, hq), hq)
                qg, dog = q_ref[r, :], do_ref[r, :]
                p = jnp.exp2(_dot(qg, kb, "nt"))
                ds = (p * _dot(dog, vb, "nt")).astype(_MXU_DTYPE)
                dq_ref[r, :] += _dot(ds, kb, "nn")
                dv_ref[...] += _dot(p, dog, "tn")
                dk_ref[...] += _dot(ds, qg, "tn")
            return carry

        lax.fori_loop(0, nq, step, 0, unroll=unroll if nq % unroll == 0 else 1)
        dk_ref[...] = dk_ref[...] * (1.0 / _LOG2E)

        @pl.when(j == nkb - 1)
        def _():
            dq_ref[...] = dq_ref[...] * _ATT_SCALE

        side_wait()

    blk = pl.BlockSpec((tk, HP), lambda h, j: (j, h))
    whole = pl.BlockSpec((S, HP), lambda h, j: (0, h))
    res = _call(
        body, name="attn_bwd", grid=grid,
        in_specs=[whole, blk, blk, whole] + io["specs"],
        out_specs=[whole, blk, blk] + io["specs"],
        out_shape=[jax.ShapeDtypeStruct((S, H_MLA * HP), F32)] * 3 + io["out_shape"],
        scratch_shapes=io["scratch"],
        compiler_params=_cparams(("arbitrary", "arbitrary")),
    )(qa, k, va, doa, *side)
    return res[0], res[1], res[2], list(res[3:])


def _mlstm_chunk_terms(g, q, k, v, gates, gates_t, bg_row, C, n_row, m):
    L = LCH
    d = g // HM
    h = g % HM
    i_idx = d * 8 + h
    f_idx = d * 8 + 4 + h
    rr = lax.broadcasted_iota(jnp.int32, (L, L), 0)
    cc = lax.broadcasted_iota(jnp.int32, (L, L), 1)
    order = (rr - cc) * (1 - 2 * d)
    tri = order >= 0
    eye = rr == cc
    lane = lax.broadcasted_iota(jnp.int32, gates.shape, 1)
    sub = lax.broadcasted_iota(jnp.int32, gates_t.shape, 0)
    lane_b = lax.broadcasted_iota(jnp.int32, bg_row.shape, 1)
    pick_c = lambda idx: jnp.sum(jnp.where(lane == idx, gates, 0.0), axis=1, keepdims=True)
    pick_r = lambda idx: jnp.sum(jnp.where(sub == idx, gates_t, 0.0), axis=0, keepdims=True)
    pick_b = lambda idx: jnp.sum(jnp.where(lane_b == idx, bg_row, 0.0), axis=1, keepdims=True)
    i_col, i_row = pick_c(i_idx) + pick_b(i_idx), pick_r(i_idx) + pick_b(i_idx)
    f_col, f_row = pick_c(f_idx) + pick_b(f_idx), pick_r(f_idx) + pick_b(f_idx)
    logsig = lambda x: jnp.minimum(x, 0.0) - jnp.log(1.0 + jnp.exp(-jnp.abs(x)))
    lf_col, lf_row = logsig(f_col), logsig(f_row)
    b_col = jnp.sum(jnp.where(tri, lf_row, 0.0), axis=1, keepdims=True)
    tri_t = order <= 0
    b_row = jnp.sum(jnp.where(tri_t, lf_col, 0.0), axis=0, keepdims=True)
    bL = jnp.sum(lf_row, axis=1, keepdims=True)
    log_inter = b_col + m
    logD = jnp.where(tri, b_col - b_row + i_row, -jnp.inf)
    m_t = jnp.maximum(log_inter, jnp.max(logD, axis=1, keepdims=True))
    Dm = jnp.exp(logD - m_t)
    w_inter = jnp.exp(log_inter - m_t)
    A = _dot(q, k, "nt")
    Sc = A * Dm
    numI = _dot(q, C, "nt")
    qf = q.astype(F32)
    kf = k.astype(F32)
    denI = jnp.sum(qf * n_row, axis=1, keepdims=True)
    num = _dot(Sc, v, "nn") + w_inter * numI
    den = jnp.sum(Sc, axis=1, keepdims=True) + w_inter * denI
    floor = jnp.exp(-m_t)
    Nst = jnp.maximum(jnp.abs(den), floor)
    log_w = bL - b_col + i_col
    m_new = jnp.maximum(bL + m, jnp.max(log_w, axis=0, keepdims=True))
    decay = jnp.exp(bL + m - m_new)
    w_col = jnp.exp(log_w - m_new)
    return dict(tri=tri, eye=eye, f_row=f_row, Dm=Dm, w_inter=w_inter, A=A, Sc=Sc, numI=numI, denI=denI,
                num=num, den=den, floor=floor, Nst=Nst, m_new=m_new, decay=decay, w_col=w_col, qf=qf, kf=kf)


def _mlstm_specs(nc, d, step_of):
    chunk = lambda j: step_of(j) if d == 0 else nc - 1 - step_of(j)
    return chunk, [
        pl.BlockSpec((LCH, DM), lambda h, j: (chunk(j), h)),
        pl.BlockSpec((LCH, DM), lambda h, j: (chunk(j), h)),
        pl.BlockSpec((LCH, DM), lambda h, j: (chunk(j), P_VM // DM + h)),
        pl.BlockSpec((LCH, 128), lambda h, j: (chunk(j), P_G // 128)),
        pl.BlockSpec((NG, LCH), lambda h, j: (0, chunk(j))),
    ]


def _mlstm_fwd(qc, kc, proj, gates_t, bg_row):
    S = qc.shape[0]
    nc = S // LCH
    in_specs, out_specs = [], []
    for d in (0, 1):
        chunk, specs = _mlstm_specs(nc, d, lambda j: j)
        in_specs += specs
        out_specs += [pl.BlockSpec((LCH, DM), lambda h, j, chunk=chunk: (chunk(j), h)),
                      pl.BlockSpec((None, None, DM, DM), lambda h, j, chunk=chunk: (h, chunk(j), 0, 0)),
                      pl.BlockSpec((None, None, 8, DM), lambda h, j, chunk=chunk: (h, chunk(j), 0, 0))]
    in_specs.append(pl.BlockSpec((1, 128), lambda h, j: (0, 0)))

    def body(*refs):
        bg_ref, outs, (C_s, n_s, m_s) = refs[10], refs[11:17], refs[17:]

        @pl.when(pl.program_id(1) == 0)
        def _():
            C_s[...] = jnp.zeros(C_s.shape, F32)
            n_s[...] = jnp.zeros(n_s.shape, F32)
            m_s[...] = jnp.full(m_s.shape, M_INIT, F32)

        for d in (0, 1):
            q_ref, k_ref, v_ref, g_ref, gt_ref = refs[5 * d:5 * d + 5]
            h_ref, cst_ref, nm_ref = outs[3 * d:3 * d + 3]
            g = d * HM + pl.program_id(0)
            C, n_row, m = C_s[d], n_s[d, 0:1, :], m_s[d, 0:1, 0:1]
            cst_ref[...] = C
            nm_ref[0:1, :] = n_row
            nm_ref[1:2, :] = jnp.broadcast_to(m, (1, DM))
            nm_ref[2:8, :] = jnp.zeros((6, DM), F32)
            q, k, v = q_ref[...], k_ref[...], v_ref[...]
            t = _mlstm_chunk_terms(g, q, k, v, g_ref[...], gt_ref[...], bg_ref[...], C, n_row, m)
            h_ref[...] = t["num"] / t["Nst"]
            wv = t["w_col"] * v
            C_s[d] = t["decay"] * C + _dot(wv, k, "tn")
            n_s[d, 0:1, :] = t["decay"] * n_row + _colsum(t["w_col"] * t["kf"])
            m_s[d] = jnp.broadcast_to(t["m_new"], (8, 128))

    res = _call(
        body, name="mlstm_fwd", grid=(HM, nc), in_specs=in_specs, out_specs=out_specs,
        out_shape=[jax.ShapeDtypeStruct((S, MW), F32), jax.ShapeDtypeStruct((HM, nc, DM, DM), F32),
                   jax.ShapeDtypeStruct((HM, nc, 8, DM), F32)] * 2,
        scratch_shapes=[pltpu.VMEM((2, DM, DM), F32), pltpu.VMEM((2, 8, DM), F32), pltpu.VMEM((2, 8, 128), F32)],
        compiler_params=_cparams(("parallel", "arbitrary")),
    )(*([qc, kc, proj, proj, gates_t] * 2), bg_row)
    return (res[0], res[3]), (res[1], res[4]), (res[2], res[5])


def _mlstm_bwd(qc, kc, proj, gates_t, bg_row, dh, cst, nm):
    S = qc.shape[0]
    nc = S // LCH
    in_specs, out_specs = [], []
    for d in (0, 1):
        chunk, specs = _mlstm_specs(nc, d, lambda j: nc - 1 - j)
        in_specs += specs + [pl.BlockSpec((LCH, DM), lambda h, j, chunk=chunk: (chunk(j), h)),
                             pl.BlockSpec((None, None, DM, DM), lambda h, j, chunk=chunk: (h, chunk(j), 0, 0)),
                             pl.BlockSpec((None, None, 8, DM), lambda h, j, chunk=chunk: (h, chunk(j), 0, 0))]
        out_specs += [pl.BlockSpec((LCH, DM), lambda h, j, chunk=chunk: (chunk(j), h))] * 3
        out_specs += [pl.BlockSpec((None, None, 8, LCH), lambda h, j, chunk=chunk: (h, chunk(j), 0, 0))]
    in_specs.append(pl.BlockSpec((1, 128), lambda h, j: (0, 0)))

    def body(*refs):
        bg_ref, outs, (dC_s, dn_s) = refs[16], refs[17:25], refs[25:]

        @pl.when(pl.program_id(1) == 0)
        def _():
            dC_s[...] = jnp.zeros(dC_s.shape, F32)
            dn_s[...] = jnp.zeros(dn_s.shape, F32)

        for d in (0, 1):
            _mlstm_bwd_chain(d, refs[8 * d:8 * d + 8], bg_ref, outs[4 * d:4 * d + 4], dC_s, dn_s)

    res = _call(
        body, name="mlstm_bwd", grid=(HM, nc), in_specs=in_specs, out_specs=out_specs,
        out_shape=([jax.ShapeDtypeStruct((S, MW), F32)] * 3 + [jax.ShapeDtypeStruct((HM, nc, 8, LCH), F32)]) * 2,
        scratch_shapes=[pltpu.VMEM((2, DM, DM), F32), pltpu.VMEM((2, 8, DM), F32)],
        compiler_params=_cparams(("parallel", "arbitrary")),
    )(*[a for d in (0, 1) for a in (qc, kc, proj, proj, gates_t, dh, cst[d], nm[d])], bg_row)
    return (res[0], res[4]), (res[1], res[5]), (res[2], res[6]), (res[3], res[7])


def _mlstm_bwd_chain(d, ins, bg_ref, outs, dC_s, dn_s):
        q_ref, k_ref, v_ref, g_ref, gt_ref, dh_ref, cst_ref, nm_ref = ins
        dq_ref, dk_ref, dv_ref, dg_ref = outs
        g = d * HM + pl.program_id(0)
        C, n_row, m = cst_ref[...], nm_ref[0:1, :], nm_ref[1:2, 0:1]
        q, k, v = q_ref[...], k_ref[...], v_ref[...]
        t = _mlstm_chunk_terms(g, q, k, v, g_ref[...], gt_ref[...], bg_ref[...], C, n_row, m)
        tri, eye, qf, kf = t["tri"], t["eye"], t["qf"], t["kf"]
        w_inter, w_col, decay, Nst = t["w_inter"], t["w_col"], t["decay"], t["Nst"]
        dC, dn = dC_s[d], dn_s[d, 0:1, :]
        dhv = dh_ref[...]
        hval = t["num"] / Nst
        dnum = dhv / Nst
        dNst = -jnp.sum(dhv * hval, axis=1, keepdims=True) / Nst
        dden = jnp.where(jnp.abs(t["den"]) > t["floor"], jnp.sign(t["den"]) * dNst, 0.0)
        dSc = _dot(dnum, v, "nt") + dden
        dA = dSc * t["Dm"]
        G = dSc * t["Sc"]
        KdC = _dot(k, dC, "nt")
        dq = _dot(dA, k, "nn") + w_inter * _dot(dnum, C, "nn") + (w_inter * dden) * n_row
        dk = _dot(dA, q, "tn") + w_col * _dot(v, dC, "nn") + w_col * dn
        dv = _dot(t["Sc"], dnum, "tn") + w_col * KdC
        dq_ref[...] = dq
        dk_ref[...] = dk
        dv_ref[...] = dv
        dlog_inter = w_inter * (jnp.sum(dnum * t["numI"], axis=1, keepdims=True) + dden * t["denI"])
        rowG = jnp.sum(G, axis=1, keepdims=True)
        colG = jnp.sum(G, axis=0, keepdims=True)
        u_col = w_col * (jnp.sum(v * KdC, axis=1, keepdims=True) + jnp.sum(kf * dn, axis=1, keepdims=True))
        colG_c = jnp.sum(jnp.where(eye, colG, 0.0), axis=1, keepdims=True)
        u_row = jnp.sum(jnp.where(eye, u_col, 0.0), axis=0, keepdims=True)
        db_col = rowG + dlog_inter - u_col - colG_c
        dbL = jnp.sum(u_col, axis=0, keepdims=True) + decay * (
            jnp.sum(jnp.sum(dC * C, axis=1, keepdims=True), axis=0, keepdims=True)
            + jnp.sum(dn * n_row, axis=1, keepdims=True))
        dlf_row = jnp.sum(jnp.where(tri, db_col, 0.0), axis=0, keepdims=True) + dbL
        di_row = colG + u_row
        df_row = dlf_row * (1.0 - _sigmoid(t["f_row"]))
        dg_ref[...] = jnp.zeros(dg_ref.shape, F32)
        dg_ref[0:1, :] = di_row
        dg_ref[1:2, :] = df_row
        dC_s[d] = decay * dC + _dot(w_inter * dnum, q, "tn")
        dn_s[d, 0:1, :] = decay * dn + _colsum((w_inter * dden) * qf)


def _pad_w_in(w):
    cq, ckv, kpe, qm, km, vm, om, gt = _split_in(w)
    z = lambda n: jnp.zeros((w.shape[0], n), w.dtype)
    return jnp.concatenate([qm, km, vm, om, cq, ckv, kpe, z(HP - QK), gt, z(128 - NG)], axis=1)


def _split_in(w):
    out, o = [], 0
    for n in IN_SIZES:
        out.append(w[:, o:o + n])
        o += n
    return out


def _unpad_w_in(g):
    return jnp.concatenate([g[:, P_CQ:P_CQ + Q_LORA], g[:, P_CKV:P_CKV + KV_LORA], g[:, P_KPE:P_KPE + ROPE],
                            g[:, 0:4 * MW], g[:, P_G:P_G + NG]], axis=1)


_IN_SHARD = D_IN // 4
_IN_SEGMENTS = ((0, 512, P_CQ), (512, 768, P_CKV), (768, 832, P_KPE), (832, 4928, P_QM), (4928, 4944, P_G))


def _pad_w_in_slabs(slabs):
    def orig(a, b):
        out = []
        for k in range(4):
            lo, hi = max(a, k * _IN_SHARD), min(b, (k + 1) * _IN_SHARD)
            if lo < hi:
                out.append(slabs[k][:, lo - k * _IN_SHARD:hi - k * _IN_SHARD])
        return out

    z = lambda n: jnp.zeros((slabs.shape[1], n), slabs.dtype)
    return jnp.concatenate(orig(832, 4928) + orig(0, 512) + orig(512, 768) + orig(768, 832) + [z(HP - QK)]
                           + orig(4928, 4944) + [z(128 - NG)], axis=1)


def _unpad_w_in_slabs(g):
    slabs = []
    for k in range(4):
        pieces = []
        for a, b, p in _IN_SEGMENTS:
            lo, hi = max(a, k * _IN_SHARD), min(b, (k + 1) * _IN_SHARD)
            if lo < hi:
                pieces.append(g[:, p + lo - a:p + hi - a])
        slabs.append(jnp.concatenate(pieces, axis=1))
    return jnp.stack(slabs)


def _pad_w_uq(w):
    return jnp.pad(w.reshape(Q_LORA, H_MLA, QK), ((0, 0), (0, 0), (0, HP - QK))).reshape(Q_LORA, H_MLA * HP)


def _unpad_w_uq(g):
    return g.reshape(Q_LORA, H_MLA, HP)[:, :, :QK].reshape(Q_LORA, H_MLA * QK)


def _perm_w_ukv(w):
    return w.reshape(KV_LORA, H_MLA, 2, NOPE).transpose(0, 2, 1, 3).reshape(KV_LORA, 2 * H_MLA * NOPE)


def _unperm_w_ukv(g):
    return g.reshape(KV_LORA, 2, H_MLA, NOPE).transpose(0, 2, 1, 3).reshape(KV_LORA, 2 * H_MLA * NOPE)


def _rope_tables(positions):
    half = ROPE // 2
    freqs = ROPE_THETA ** (-jnp.arange(half, dtype=F32) / half)
    ang = positions.astype(F32)[:, None] * freqs
    cos, sin = jnp.cos(ang), jnp.sin(ang)
    z32, z64 = jnp.zeros_like(cos), jnp.zeros((cos.shape[0], 64), F32)
    return (jnp.concatenate([cos, cos, z64], axis=1), jnp.concatenate([z32, sin, z64], axis=1),
            jnp.concatenate([-sin, z32, z64], axis=1))


def _device_step(x, tgt, positions, modv, W, late=None):
    S = x.shape[0]
    MX = _MXU_DTYPE
    cosp, rs1, rs2 = _rope_tables(positions)
    tabs = [(cosp, 128, 0), (rs1, 128, 0), (rs2, 128, 0)]
    cat1 = lambda vs: jnp.concatenate(vs, axis=1)
    hsl = lambda hh, w: slice(hh * w, (hh + 1) * w)

    def ln1(xv, g, mv):
        xhat, _ = _rms(xv, D)
        return [xhat * g * (1.0 + mv[1:2]) + mv[0:1]], []

    (h,) = _rowmap(ln1, [(x, D, 0)], [W["g_mix"], modv], [(D, MX)], tile=512, name="ln1")
    proj = _mm(h, W["w_in"], "nn", name="proj")

    def lora(cq, ckv, gq, gkv):
        return [_rms(cq, Q_LORA)[0] * gq, _rms(ckv, KV_LORA)[0] * gkv], []

    cqn, ckvn = _rowmap(lora, [(proj, Q_LORA, P_CQ // Q_LORA), (proj, KV_LORA, P_CKV // KV_LORA)],
                        [W["g_qlora"], W["g_kvlora"]], [(Q_LORA, MX), (KV_LORA, MX)], tile=512, name="lora_norm")
    q_raw = _mm(cqn, W["w_uq"], "nn", name="q_up")
    kv_raw = _mm(ckvn, W["w_ukv"], "nn", name="kv_up")

    def mla_q(qr, cp, a1, a2, gq):
        outs = []
        for hh in range(H_MLA):
            y = _rms(qr[:, hsl(hh, HP)], QK)[0] * gq
            outs += [y[:, :NOPE], _rope_fwd(y[:, NOPE:], cp, a1, a2)]
        return [cat1(outs) * _Q_PRESCALE], []

    (qh,) = _rowmap(mla_q, [(q_raw, H_MLA * HP, 0)] + tabs, [W["gq"]], [(H_MLA * HP, MX)], tile=512, name="mla_q")

    def mla_k(kvr, kpe, cp, a1, a2, gk):
        lane = lax.broadcasted_iota(jnp.int32, (kvr.shape[0], 128), 1)
        outs, vas = [], []
        for hh in range(H_MLA):
            y = _rms(cat1([kvr[:, hsl(hh, NOPE)], kpe]), QK)[0] * gk
            outs += [y[:, :NOPE], _rope_fwd(y[:, NOPE:], cp, a1, a2) + ((lane == ROPE) | (lane == ROPE + 1)).astype(F32)]
            vas += [kvr[:, H_MLA * NOPE + hh * VD:H_MLA * NOPE + (hh + 1) * VD], (lane < 2).astype(F32)]
        return [cat1(outs), cat1(vas)], []

    kh, va = _rowmap(mla_k, [(kv_raw, 2 * H_MLA * NOPE, 0), (proj, 128, P_KPE // 128)] + tabs, [W["gk"]],
                     [(H_MLA * HP, MX), (H_MLA * HP, MX)], tile=512, name="mla_k")
    attn_o, qa, gathered = _attn_fwd(qh, kh, va, side=late or (), side_cols=(1,))
    if late:
        W = dict(W, w_out=gathered[0].reshape(D, D), w_ff1=gathered[1], w_ff2=gathered[2].reshape(DFF, D))

    qc, kc = _conv_fwd(proj, W["conv_w8"], W["conv_b"])
    gates_t = proj[:, P_G:P_G + NG].T
    (h_f, h_b), cst, nm = _mlstm_fwd(qc, kc, proj, gates_t, W["bg_row"])
    hrows = [(h_f, MW, 0), (h_b, MW, 0), (proj, MW, P_OM // MW)]

    def ml_out(ao, hf, hb, om, gmn):
        outs = [ao.astype(F32)]
        hs = hf + hb
        for hh in range(HM):
            sl = hsl(hh, DM)
            outs.append(_sigmoid(om[:, sl]) * _rms(hs[:, sl], DM)[0] * gmn[:, sl])
        return [cat1(outs)], []

    (cat,) = _rowmap(ml_out, [(attn_o, MW, 0)] + hrows, [W["g_mn"]], [(D, MX)], tile=512, name="ml_out")
    mixed = _mm(cat, W["w_out"], "nn", name="out_proj")

    def res_ln2(xv, mx, g, mv):
        x1 = xv + mv[2:3] * mx
        return [x1, _rms(x1, D)[0] * g * (1.0 + mv[4:5]) + mv[3:4]], []

    x1, h2 = _rowmap(res_ln2, [(x, D, 0), (mixed, D, 0)], [W["g_mlp"], modv], [(D, F32), (D, MX)],
                     tile=512, name="res_ln2")
    a, u = _mm(h2, W["w_ff1"], "nn", name="ff1", out_dtypes=(MX, MX),
               epilogue=lambda r: (jnp.square(jnp.maximum(r, 0.0)), r))
    y = _mm(a, W["w_ff2"], "nn", name="ff2")

    def final(x1v, yv, tv, mv):
        err = x1v + mv[5:6] * yv - tv
        dout = err * (1.0 / D)
        loss = jnp.sum(jnp.sum(0.5 * err * dout, axis=1, keepdims=True), axis=0, keepdims=True)
        return [dout, mv[5:6] * dout], [loss, _colsum(dout * yv)]

    dout, dy, loss, dgate2 = _rowmap(final, [(x1, D, 0), (y, D, 0), (tgt, D, 0)], [modv], [(D, F32), (D, MX)],
                                     [(1, 1), (1, D)], tile=256, name="loss_head")

    du = _mm(dy, W["w_ff2"], "nt", name="ff2_dx", out_dtypes=(MX,), extras=(u,),
             epilogue=lambda r, uv: (r * (2.0 * jnp.maximum(uv.astype(F32), 0.0)),))
    gdt = (MX,)
    g_ff2 = _mm(a, dy, "tn", name="ff2_dw", out_dtypes=gdt)
    dh2 = _mm(du, W["w_ff1"], "nt", name="ff1_dx")
    g_ff1 = _mm(h2, du, "tn", name="ff1_dw", out_dtypes=gdt)

    def ln2_bwd(dh2v, x1v, doutv, mxv, g, mv):
        xhat, r = _rms(x1v, D)
        dn2 = dh2v * (1.0 + mv[4:5])
        dx1 = doutv + _rms_bwd(dn2 * g, xhat, r, D)
        return [dx1, mv[2:3] * dx1], [_colsum(dh2v), _colsum(dh2v * xhat * g), _colsum(dn2 * xhat), _colsum(dx1 * mxv)]

    dx1, dmixed, dshift2, dscale2, dg_mlp, dgate1 = _rowmap(
        ln2_bwd, [(dh2, D, 0), (x1, D, 0), (dout, D, 0), (mixed, D, 0)], [W["g_mlp"], modv],
        [(D, F32), (D, MX)], [(1, D)] * 4, tile=256, name="ln2_bwd")
    dcat = _mm(dmixed, W["w_out"], "nt", name="out_dx")
    g_out = _mm(cat, dmixed, "tn", name="out_dw", out_dtypes=gdt)

    def ml_out_bwd(dml, hf, hb, om, gmn):
        hs = hf + hb
        dhs, dos, dgs = [], [], []
        for hh in range(HM):
            sl = hsl(hh, DM)
            xhat, r = _rms(hs[:, sl], DM)
            g, sg, d = gmn[:, sl], _sigmoid(om[:, sl]), dml[:, sl]
            dos.append(d * xhat * g * sg * (1.0 - sg))
            dhn = d * sg
            dgs.append(_colsum(dhn * xhat))
            dhs.append(_rms_bwd(dhn * g, xhat, r, DM))
        return [cat1(dhs), cat1(dos)], [cat1(dgs)]

    dhs, do_m, dg_mn = _rowmap(ml_out_bwd, [(dcat, MW, 1)] + hrows, [W["g_mn"]], [(MW, F32), (MW, MX)],
                               [(1, MW)], tile=512, name="ml_out_bwd")
    dqd, dkd, dvd, dgates = _mlstm_bwd(qc, kc, proj, gates_t, W["bg_row"], dhs, cst, nm)
    dqk_m, dconv_w8, dconv_b = _conv_bwd(proj, dqd, dkd, W["conv_w8"], W["conv_b"])

    def do_aug(ao, dov):
        lane = lax.broadcasted_iota(jnp.int32, (ao.shape[0], 128), 1)
        outs = []
        for hh in range(H_MLA):
            sl = hsl(hh, VD)
            dl = jnp.sum(ao[:, sl].astype(F32) * dov[:, sl], axis=1, keepdims=True)
            hi = dl.astype(MX).astype(F32)
            outs += [dov[:, sl], jnp.where(lane == 0, -hi, jnp.where(lane == 1, hi - dl, 0.0))]
        return [cat1(outs)], []

    (doa,) = _rowmap(do_aug, [(attn_o, MW, 0), (dcat, MW, 0)], [], [(H_MLA * HP, MX)], tile=512, name="attn_delta")
    side = [g_out.reshape(N_CHIP, D // N_CHIP, D), g_ff1, g_ff2.reshape(N_CHIP, DFF // N_CHIP, D)] if late else ()
    dq_a, dk_a, dv_a, late_got = _attn_bwd(qa, kh, va, doa, side=side, side_cols=(1,))

    def mla_q_bwd(dqv, qr, cp, a1, a2, gq):
        outs, dg = [], 0.0
        for hh in range(H_MLA):
            sl = hsl(hh, HP)
            xhat, r = _rms(qr[:, sl], QK)
            d = dqv[:, sl]
            dyv = cat1([d[:, :NOPE], _rope_bwd(d[:, NOPE:], cp, a1, a2)])
            dg = dg + _colsum(dyv * xhat)
            outs.append(_rms_bwd(dyv * gq, xhat, r, QK))
        return [cat1(outs)], [dg]

    dq_raw, dgq = _rowmap(mla_q_bwd, [(dq_a, H_MLA * HP, 0), (q_raw, H_MLA * HP, 0)] + tabs, [W["gq"]],
                          [(H_MLA * HP, MX)], [(1, HP)], tile=512, name="mla_q_bwd")
    dcqn = _mm(dq_raw, W["w_uq"], "nt", name="q_up_dx")
    g_uq = _mm(cqn, dq_raw, "tn", name="q_up_dw", out_dtypes=gdt)

    def mla_k_bwd(dkv, dvv, kvr, kpe, cp, a1, a2, gk):
        dkn, dg, dkpe = [], 0.0, 0.0
        for hh in range(H_MLA):
            xhat, r = _rms(cat1([kvr[:, hsl(hh, NOPE)], kpe]), QK)
            d = dkv[:, hsl(hh, HP)]
            dyv = cat1([d[:, :NOPE], _rope_bwd(d[:, NOPE:], cp, a1, a2)])
            dg = dg + _colsum(dyv * xhat)
            dxv = _rms_bwd(dyv * gk, xhat, r, QK)
            dkn.append(dxv[:, :NOPE])
            dkpe = dkpe + dxv[:, NOPE:]
        return [cat1(dkn + [dvv[:, hh * HP:hh * HP + VD] for hh in range(H_MLA)]), dkpe], [dg]

    dkv_raw, dkpe, dgk = _rowmap(
        mla_k_bwd, [(dk_a, H_MLA * HP, 0), (dv_a, H_MLA * HP, 0), (kv_raw, 2 * H_MLA * NOPE, 0),
                    (proj, 128, P_KPE // 128)] + tabs, [W["gk"]],
        [(2 * H_MLA * NOPE, MX), (128, MX)], [(1, HP)], tile=256, name="mla_k_bwd")
    dckvn = _mm(dkv_raw, W["w_ukv"], "nt", name="kv_up_dx")
    g_ukv = _mm(ckvn, dkv_raw, "tn", name="kv_up_dw", out_dtypes=gdt)

    def lora_bwd(dcq, dckv, cq, ckv, gq, gkv):
        xq, rq = _rms(cq, Q_LORA)
        xk, rk = _rms(ckv, KV_LORA)
        return ([_rms_bwd(dcq * gq, xq, rq, Q_LORA), _rms_bwd(dckv * gkv, xk, rk, KV_LORA)],
                [_colsum(dcq * xq), _colsum(dckv * xk)])

    dc_q, dc_kv, dg_qlora, dg_kvlora = _rowmap(
        lora_bwd, [(dcqn, Q_LORA, 0), (dckvn, KV_LORA, 0), (proj, Q_LORA, P_CQ // Q_LORA),
                   (proj, KV_LORA, P_CKV // KV_LORA)], [W["g_qlora"], W["g_kvlora"]],
        [(Q_LORA, MX), (KV_LORA, MX)], [(1, Q_LORA), (1, KV_LORA)], tile=512, name="lora_bwd")

    nc = S // LCH
    dg16 = jnp.stack(dgates)[:, :, :, 0:2, :].transpose(2, 4, 0, 3, 1).reshape(S, NG)
    dg128 = jnp.pad(dg16, ((0, 0), (0, 128 - NG)))

    def assemble(dqk, dv0, dv1, dom, dcq, dckv, dkp, dgp):
        f = lambda t: t.astype(F32)
        return [cat1([f(dqk), dv0 + dv1, f(dom), f(dcq), f(dckv), f(dkp), dgp])], [_colsum(dgp)]

    dproj, dbg = _rowmap(
        assemble, [(dqk_m, 2 * MW, 0), (dvd[0], MW, 0), (dvd[1], MW, 0), (do_m, MW, 0), (dc_q, Q_LORA, 0),
                   (dc_kv, KV_LORA, 0), (dkpe, 128, 0), (dg128, 128, 0)], [], [(D_INP, MX)], [(1, 128)],
        tile=256, name="dproj")
    g_in = _mm(h, dproj, "tn", name="proj_dw", out_dtypes=gdt)
    early_got = ()
    if late:
        side = [_unpad_w_in_slabs(g_in).astype(MX), _slabs(_unpad_w_uq(g_uq)).astype(MX),
                _slabs(_unperm_w_ukv(g_ukv)).astype(MX)]
        dh, early_got = _mm(dproj, W["w_in"], "nt", name="proj_dx", side=side)
    else:
        dh = _mm(dproj, W["w_in"], "nt", name="proj_dx")

    def ln1_bwd(dhv, xv, dx1v, g, mv):
        xhat, r = _rms(xv, D)
        dn = dhv * (1.0 + mv[1:2])
        return [dx1v + _rms_bwd(dn * g, xhat, r, D)], [_colsum(dhv), _colsum(dhv * xhat * g), _colsum(dn * xhat)]

    gx, dshift1, dscale1, dg_mix = _rowmap(ln1_bwd, [(dh, D, 0), (x, D, 0), (dx1, D, 0)], [W["g_mix"], modv],
                                           [(D, F32)], [(1, D)] * 3, tile=256, name="ln1_bwd")
    dmodv = jnp.concatenate([dshift1, dscale1, dgate1, dshift2, dscale2, dgate2], axis=0)
    grads = dict(w_in=g_in, w_uq=g_uq, w_ukv=g_ukv, w_out=g_out, w_ff1=g_ff1, w_ff2=g_ff2,
                 norm_mix_g=dg_mix, b_gates=dbg[:, :NG], conv_w=dconv_w8[:CONVW], conv_b=dconv_b,
                 q_lora_g=dg_qlora, kv_lora_g=dg_kvlora, q_norm_g=dgq[:, :QK], k_norm_g=dgk[:, :QK],
                 mlstm_norm_g=dg_mn, norm_mlp_g=dg_mlp)
    grads["got"] = list(early_got) + list(late_got)
    return loss, gx, dmodv, grads


def _cols(g):
    return g.transpose(1, 0, 2).reshape(g.shape[1], N_CHIP * g.shape[2])


def _slabs(gfull):
    return gfull.reshape(gfull.shape[0], N_CHIP, -1).transpose(1, 0, 2)


def _prep_weights(w_in, w_uq, w_ukv, w_out, w_ff1, w_ff2, norm_mix_g, norm_mlp_g, q_lora_g, kv_lora_g,
                  q_norm_g, k_norm_g, mlstm_norm_g, conv_w, conv_b, b_gates):
    MX = _MXU_DTYPE
    padg = lambda g: jnp.pad(g.reshape(1, QK).astype(F32), ((0, 0), (0, HP - QK)))
    return dict(
        w_in=(_pad_w_in_slabs(w_in) if w_in.ndim == 3 else _pad_w_in(w_in)).astype(MX), w_uq=_pad_w_uq(w_uq).astype(MX), w_ukv=_perm_w_ukv(w_ukv).astype(MX),
        w_out=None if w_out is None else w_out.astype(MX), w_ff1=None if w_ff1 is None else w_ff1.astype(MX),
        w_ff2=None if w_ff2 is None else w_ff2.astype(MX),
        g_mix=norm_mix_g.reshape(1, D), g_mlp=norm_mlp_g.reshape(1, D), g_qlora=q_lora_g.reshape(1, Q_LORA),
        g_kvlora=kv_lora_g.reshape(1, KV_LORA), gq=padg(q_norm_g), gk=padg(k_norm_g),
        g_mn=mlstm_norm_g.reshape(1, MW), conv_w8=jnp.pad(conv_w.reshape(CONVW, 2 * MW), ((0, 8 - CONVW), (0, 0))),
        conv_b=conv_b.reshape(1, 2 * MW), bg_row=jnp.pad(b_gates.reshape(1, NG), ((0, 0), (0, 128 - NG))))


MESH = pl.DeviceIdType.MESH
N_DEV = 8
N_CHIP = 4


def _comm_call(body, **kw):
    if _INTERPRET:
        kw["interpret"] = pltpu.InterpretParams()
    return pl.pallas_call(body, **kw)


def _allgather8(blk, *, name):
    m_per, n = blk.shape

    def body(x_ref, out_ref, send_sems, recv_sems, local_sem):
        x, y, c = lax.axis_index("x"), lax.axis_index("y"), lax.axis_index("c")
        me, sibling = (x, y, c), (x, y, 1 - c)
        chips = [(1 - x, y), (x, 1 - y), (1 - x, 1 - y)]

        def rows(px, py, pc):
            return out_ref.at[pl.ds((4 * px + 2 * py + pc) * m_per, m_per), :]

        def copy(k, block, to, src=None):
            return pltpu.make_async_remote_copy(
                src_ref=rows(*block) if src is None else src, dst_ref=rows(*block),
                send_sem=send_sems.at[k], recv_sem=recv_sems.at[k], device_id=to, device_id_type=MESH)

        mine = pltpu.make_async_copy(x_ref, rows(*me), local_sem)
        mine.start()
        first = [copy(0, me, sibling, src=x_ref)]
        first += [copy(1 + j, me, (*chip, c), src=x_ref) for j, chip in enumerate(chips)]
        for cp in first:
            cp.start()
        passed = [copy(4 + j, (*chip, c), sibling) for j, chip in enumerate(chips)]
        for j, chip in enumerate(chips):
            copy(1 + j, (*chip, c), me).wait_recv()
            passed[j].start()
        copy(0, sibling, me).wait_recv()
        for j, chip in enumerate(chips):
            copy(4 + j, (*chip, 1 - c), me).wait_recv()
        for cp in first + passed:
            cp.wait_send()
        mine.wait()

    return _comm_call(
        body, name=name, out_shape=jax.ShapeDtypeStruct((N_DEV * m_per, n), blk.dtype),
        in_specs=[pl.BlockSpec(memory_space=pltpu.VMEM)], out_specs=pl.BlockSpec(memory_space=pltpu.VMEM),
        scratch_shapes=[pltpu.SemaphoreType.DMA((7,)), pltpu.SemaphoreType.DMA((7,)), pltpu.SemaphoreType.DMA],
    )(blk)


def _exchange_io(arrays, gather, cols=()):
    n = len(arrays)

    def out(i, a):
        if gather:
            return (a.shape[0], N_CHIP * a.shape[1]) if i in cols else (N_CHIP, *a.shape)
        return (N_CHIP, a.shape[0], a.shape[1] // N_CHIP) if i in cols else a.shape

    return dict(
        specs=[pl.BlockSpec(memory_space=pltpu.HBM)] * n,
        out_shape=[jax.ShapeDtypeStruct(out(i, a), a.dtype) for i, a in enumerate(arrays)],
        scratch=[pltpu.SemaphoreType.DMA((3 * n,)), pltpu.SemaphoreType.DMA((3 * n,)), pltpu.SemaphoreType.DMA((n,))])


def _exchange_ops(ins, outs, send_sems, recv_sems, local_sems, *, gather, cols=()):
    n = len(ins)
    x, y, c = lax.axis_index("x"), lax.axis_index("y"), lax.axis_index("c")
    k = 2 * x + y
    chips = [(1 - x, y), (x, 1 - y), (1 - x, 1 - y)]

    def piece(ref, a, chip, windowed):
        if not windowed:
            return ref.at[chip]
        width = ref.shape[1] // N_CHIP
        return ref.at[:, pl.ds(pl.multiple_of(chip * width, 128), width)]

    src_of = lambda a, chip: ins[a] if gather else piece(ins[a], a, chip, a in cols)
    dst_of = lambda a, chip: piece(outs[a], a, chip, gather and a in cols)

    def remote(a, j):
        px, py = chips[j]
        return pltpu.make_async_remote_copy(
            src_ref=src_of(a, 2 * px + py), dst_ref=dst_of(a, k), send_sem=send_sems.at[3 * a + j],
            recv_sem=recv_sems.at[3 * a + j], device_id=(px, py, c), device_id_type=MESH)

    def arrival(a, j):
        px, py = chips[j]
        return pltpu.make_async_remote_copy(
            src_ref=src_of(a, k), dst_ref=dst_of(a, 2 * px + py), send_sem=send_sems.at[3 * a + j],
            recv_sem=recv_sems.at[3 * a + j], device_id=(px, py, c), device_id_type=MESH)

    local = [pltpu.make_async_copy(src_of(a, k), dst_of(a, k), local_sems.at[a]) for a in range(n)]
    sent = [remote(a, j) for a in range(n) for j in range(3)]

    def start():
        for cp in local + sent:
            cp.start()

    def wait():
        for a in range(n):
            for j in range(3):
                arrival(a, j).wait_recv()
        for cp in sent:
            cp.wait_send()
        for cp in local:
            cp.wait()

    return start, wait


def _chip_allgather_halved(shards, *, name):
    n = len(shards)
    half_rows = [s.shape[0] // 2 for s in shards]
    assert all(s.shape[0] % 16 == 0 for s in shards)

    def body(*refs):
        ins, outs = refs[:n], refs[n:2 * n]
        ici_send, ici_recv, d2d_send, d2d_recv, local_sems = refs[2 * n:]
        x, y, c = lax.axis_index("x"), lax.axis_index("y"), lax.axis_index("c")
        k = 2 * x + y
        chips = [(1 - x, y), (x, 1 - y), (1 - x, 1 - y)]

        def half(a, slab, core):
            return outs[a].at[slab, pl.ds(core * half_rows[a], half_rows[a])]

        def ici(a, j, slab):
            px, py = chips[j]
            return pltpu.make_async_remote_copy(
                src_ref=ins[a].at[pl.ds(c * half_rows[a], half_rows[a])], dst_ref=half(a, slab, c),
                send_sem=ici_send.at[3 * a + j], recv_sem=ici_recv.at[3 * a + j],
                device_id=(px, py, c), device_id_type=MESH)

        def d2d(a, j, core):
            px, py = chips[j]
            return pltpu.make_async_remote_copy(
                src_ref=half(a, 2 * px + py, core), dst_ref=half(a, 2 * px + py, core),
                send_sem=d2d_send.at[3 * a + j], recv_sem=d2d_recv.at[3 * a + j],
                device_id=(x, y, 1 - c), device_id_type=MESH)

        local = [pltpu.make_async_copy(ins[a], outs[a].at[k], local_sems.at[a]) for a in range(n)]
        sent = [ici(a, j, k) for a in range(n) for j in range(3)]
        for cp in local + sent:
            cp.start()
        passed = []
        for a in range(n):
            for j, (px, py) in enumerate(chips):
                ici(a, j, 2 * px + py).wait_recv()
                passed.append(d2d(a, j, c))
                passed[-1].start()
        for a in range(n):
            for j in range(3):
                d2d(a, j, 1 - c).wait_recv()
        for cp in sent + passed:
            cp.wait_send()
        for cp in local:
            cp.wait()

    hbm = pl.BlockSpec(memory_space=pltpu.HBM)
    return _comm_call(
        body, name=name, out_shape=[jax.ShapeDtypeStruct((N_CHIP, *s.shape), s.dtype) for s in shards],
        in_specs=[hbm] * n, out_specs=[hbm] * n,
        scratch_shapes=[pltpu.SemaphoreType.DMA((3 * n,))] * 4 + [pltpu.SemaphoreType.DMA((n,))],
    )(*shards)


def _sum_blocks(a, nblk, *, name):
    n = a.shape[1]

    def body(a_ref, o_ref):
        acc = a_ref[pl.ds(0, 8), :]
        for d in range(1, nblk):
            acc = acc + a_ref[pl.ds(8 * d, 8), :]
        o_ref[...] = acc

    return _call(body, name=name, out_shape=jax.ShapeDtypeStruct((8, n), F32))(a)


def _outer8(sct, dm, *, name, swap=(), tm=256, tn=1024):
    R, N = sct.shape[0], dm.shape[1]
    tm, tn = min(tm, R), min(tn, N)
    grid = (R // tm, N // tn)
    ns = len(swap)

    def body(s_ref, d_ref, *rest):
        o_ref = rest[ns]
        x, y, c = lax.axis_index("x"), lax.axis_index("y"), lax.axis_index("c")
        swaps = [pltpu.make_async_remote_copy(
            src_ref=rest[a], dst_ref=rest[ns + 1 + a], send_sem=rest[2 * ns + 1].at[a], recv_sem=rest[2 * ns + 2].at[a],
            device_id=(x, y, 1 - c), device_id_type=MESH) for a in range(ns)]
        i, j = pl.program_id(0), pl.program_id(1)

        @pl.when(jnp.logical_and(i == 0, j == 0))
        def _():
            for cp in swaps:
                cp.start()

        s, dmv = s_ref[...], d_ref[...]
        acc = s[:, 0:1] * dmv[0:1, :]
        for b in range(1, 8):
            acc = acc + s[:, b:b + 1] * dmv[b:b + 1, :]
        o_ref[...] = acc

        @pl.when(jnp.logical_and(i == grid[0] - 1, j == grid[1] - 1))
        def _():
            for cp in swaps:
                cp.wait()

    hbm = pl.BlockSpec(memory_space=pltpu.HBM)
    res = _call(
        body, name=name, grid=grid,
        in_specs=[pl.BlockSpec((tm, 8), lambda i, j: (i, 0)), pl.BlockSpec((8, tn), lambda i, j: (0, j))] + [hbm] * ns,
        out_specs=[pl.BlockSpec((tm, tn), lambda i, j: (i, j))] + [hbm] * ns,
        out_shape=[jax.ShapeDtypeStruct((R, N), F32)] + [jax.ShapeDtypeStruct(a.shape, a.dtype) for a in swap],
        scratch_shapes=[pltpu.SemaphoreType.DMA((ns,)), pltpu.SemaphoreType.DMA((ns,))] if ns else [],
        compiler_params=_cparams(("arbitrary", "arbitrary")),
    )(sct, dm, *swap)
    return res[0], list(res[1:])


_BC1 = 1.0 - ADAM_B1 ** ADAM_STEP
_BC2 = 1.0 - ADAM_B2 ** ADAM_STEP


def _adamw(w, g_parts, m, v, *, name, tile=128):
    R, C = w.shape
    tile = min(tile, R)
    assert R % tile == 0
    npart = len(g_parts)

    def body(*refs):
        w_ref, m_ref, v_ref = refs[npart:npart + 3]
        g_o, d_o, m_o, v_o = refs[npart + 3:]
        g = refs[0][...].astype(F32)
        for r in refs[1:npart]:
            g = g + r[...].astype(F32)
        mn = ADAM_B1 * m_ref[...] + (1.0 - ADAM_B1) * g
        vn = ADAM_B2 * v_ref[...] + (1.0 - ADAM_B2) * jnp.square(g)
        g_o[...] = g
        m_o[...] = mn
        v_o[...] = vn
        d_o[...] = -ADAM_LR * ((mn / _BC1) / (jnp.sqrt(vn / _BC2) + ADAM_EPS) + ADAM_WD * w_ref[...])

    spec = pl.BlockSpec((tile, C), lambda i: (i, 0))
    return _call(
        body, name=name, grid=(R // tile,), in_specs=[spec] * (npart + 3), out_specs=[spec] * 4,
        out_shape=[jax.ShapeDtypeStruct((R, C), F32)] * 4,
        compiler_params=_cparams(("parallel",)),
    )(*g_parts, w, m, v)


def _pack(vecs, rows8_cols):
    flat = jnp.concatenate([v.reshape(-1).astype(F32) for v in vecs])
    return jnp.pad(flat, (0, 8 * rows8_cols - flat.shape[0])).reshape(8, rows8_cols)


def _unpack(flat, shapes):
    out, o = [], 0
    for s in shapes:
        n = math.prod(s)
        out.append(flat[o:o + n].reshape(s))
        o += n
    return out


_BIG = ("w_in", "w_uq", "w_ukv", "w_out", "w_ff1", "w_ff2")
_SMALL = ("b_ada", "norm_mix_g", "b_gates", "conv_w", "conv_b", "q_lora_g", "kv_lora_g", "q_norm_g", "k_norm_g",
          "mlstm_norm_g", "norm_mlp_g")
_ORDER = ("w_ada", "b_ada", "norm_mix_g", "w_in", "b_gates", "conv_w", "conv_b", "q_lora_g", "w_uq", "kv_lora_g",
          "w_ukv", "q_norm_g", "k_norm_g", "mlstm_norm_g", "w_out", "norm_mlp_g", "w_ff1", "w_ff2")


def kernel(x, c, positions, w_ada, b_ada, norm_mix_g, w_in, b_gates, conv_w, conv_b, q_lora_g, w_uq, kv_lora_g, w_ukv, q_norm_g, k_norm_g, mlstm_norm_g, w_out, norm_mlp_g, w_ff1, w_ff2, loss_target, m_w_ada, m_b_ada, m_norm_mix_g, m_w_in, m_b_gates, m_conv_w, m_conv_b, m_q_lora_g, m_w_uq, m_kv_lora_g, m_w_ukv, m_q_norm_g, m_k_norm_g, m_mlstm_norm_g, m_w_out, m_norm_mlp_g, m_w_ff1, m_w_ff2, v_w_ada, v_b_ada, v_norm_mix_g, v_w_in, v_b_gates, v_conv_w, v_conv_b, v_q_lora_g, v_w_uq, v_kv_lora_g, v_w_ukv, v_q_norm_g, v_k_norm_g, v_mlstm_norm_g, v_w_out, v_norm_mlp_g, v_w_ff1, v_w_ff2):
    args = dict(locals())
    wts = {n: args[n] for n in _ORDER}
    mom = {n: args["m_" + n] for n in _ORDER}
    var = {n: args["v_" + n] for n in _ORDER}
    MX = _MXU_DTYPE
    xi, yi, ci = lax.axis_index("x"), lax.axis_index("y"), lax.axis_index("c")
    chip = 2 * xi + yi
    dev = 2 * chip + ci
    S = x.shape[1]
    CS = 2 * MW // N_CHIP
    GS = DM // N_CHIP

    pk = _pack([c, conv_w, mlstm_norm_g], 1024)
    allpk = _allgather8(pk, name="gather_small").reshape(N_DEV, 8 * 1024)
    c_all = allpk[:, :D]
    per_chip = allpk[0::2]
    conv_w_full = per_chip[:, D:D + CONVW * CS].reshape(N_CHIP, CONVW, CS).transpose(1, 0, 2).reshape(CONVW, 2 * MW)
    o = D + CONVW * CS
    mn_full = per_chip[:, o:o + HM * GS].reshape(N_CHIP, HM, GS).transpose(1, 0, 2).reshape(HM, DM)

    (sc,) = _rowmap(lambda cv: ([cv * _sigmoid(cv)], []), [(c_all, D, 0)], [], [(D, F32)], tile=8, name="silu_c")
    ncol = w_ada.shape[2]
    b_cols = lax.dynamic_slice(b_ada, (0, chip * ncol), (1, ncol))
    modp = _mm(sc, w_ada[0], "nn", name="ada_fwd", tm=8, tn=1024, tk=512, extras=(jnp.broadcast_to(b_cols, (8, ncol)),),
               epilogue=lambda r, b: (r + b,))
    modg = _allgather8(modp, name="gather_mod").reshape(N_CHIP, 2, 8, ncol)[:, 0]
    mod_all = modg.transpose(1, 0, 2).reshape(N_DEV, N_CHIP * ncol)
    modv = jnp.pad(lax.dynamic_slice(mod_all, (dev, 0), (1, 6 * D)).reshape(6, D), ((0, 2), (0, 0)))

    shards = [wts[n][0].astype(MX) for n in _BIG]
    gw_in, gw_uq, gw_ukv = _chip_allgather_halved(shards[:3], name="gather_weights")
    W = _prep_weights(gw_in, _cols(gw_uq), _cols(gw_ukv), None, None, None, norm_mix_g, norm_mlp_g,
                      q_lora_g, kv_lora_g, q_norm_g, k_norm_g, mn_full, conv_w_full, conv_b, b_gates)

    loss, gx, dmodv, g = _device_step(x[0], loss_target[0], positions[0], modv, W, late=shards[3:])

    small_shapes = [(6 * D,), (D,), (NG,), (CONVW, 2 * MW), (2 * MW,), (Q_LORA,), (KV_LORA,), (QK,), (QK,), (MW,), (D,), (1,)]
    pg = _pack([dmodv, g["norm_mix_g"], g["b_gates"], g["conv_w"], g["conv_b"], g["q_lora_g"], g["kv_lora_g"],
                g["q_norm_g"], g["k_norm_g"], g["mlstm_norm_g"], g["norm_mlp_g"], loss], 4096)
    allpg = _allgather8(pg, name="gather_small_grads")
    tot = _unpack(_sum_blocks(allpg, N_DEV, name="sum_small_grads").reshape(-1), small_shapes)
    dmod_all = allpg.reshape(N_DEV, 8 * 4096)[:, :6 * D]
    gsmall = dict(zip(_SMALL, [tot[0].reshape(1, 6 * D), tot[1].reshape(1, D), tot[2].reshape(1, NG),
                               lax.dynamic_slice(tot[3], (0, chip * CS), (CONVW, CS)).reshape(1, CONVW, CS),
                               tot[4].reshape(1, 2 * MW), tot[5].reshape(1, Q_LORA), tot[6].reshape(1, KV_LORA),
                               tot[7].reshape(1, QK), tot[8].reshape(1, QK),
                               lax.dynamic_slice(tot[9].reshape(HM, DM), (0, chip * GS), (HM, GS)).reshape(1, HM, GS),
                               tot[10].reshape(1, D)]))
    loss_tot = tot[11].reshape(())

    got = g["got"]
    part = []
    for nme, r in zip(_BIG, got):
        wd = r.shape[2]
        (p,) = _rowmap(lambda a0, a1, a2, a3: ([(a0.astype(F32) + a1.astype(F32)) + (a2.astype(F32) + a3.astype(F32))], []),
                       [(r, wd, 0, k) for k in range(N_CHIP)], [], [(wd, MX)], tile=256, name="sum_chips_" + nme)
        part.append(p)

    dm_cols = lax.dynamic_slice(dmod_all, (0, chip * ncol), (N_DEV, ncol))
    g_ada, other = _outer8(sc.T, dm_cols, name="ada_dw", swap=part)

    res = {}
    for nme, p, q in zip(_BIG, part, other):
        res[nme] = _adamw(wts[nme][0], [p, q], mom[nme][0], var[nme][0], name="adamw_" + nme)
    res["w_ada"] = _adamw(w_ada[0], [g_ada], m_w_ada[0], v_w_ada[0], name="adamw_w_ada")
    sw = _pack([wts[n] for n in _SMALL], 3072)
    sg = _pack([gsmall[n] for n in _SMALL], 3072)
    sm = _pack([mom[n] for n in _SMALL], 3072)
    sv = _pack([var[n] for n in _SMALL], 3072)
    small_res = _adamw(sw, [sg], sm, sv, name="adamw_small", tile=8)
    shapes = [wts[n].shape for n in _SMALL]
    unp = [_unpack(r.reshape(-1), shapes) for r in small_res]
    for i, nme in enumerate(_SMALL):
        res[nme] = tuple(u[i] for u in unp)
    outs = [loss_tot, gx[None]]
    for kind in range(4):
        outs += [res[n][kind].reshape(wts[n].shape) for n in _ORDER]
    return tuple(outs)
```

```python
import functools
import math

import jax
import jax.numpy as jnp
from jax import lax
from jax.experimental import pallas as pl
from jax.experimental.pallas import tpu as pltpu

F32 = jnp.float32
BF16 = jnp.bfloat16
_MXU_DTYPE = jnp.bfloat16
_INTERPRET = False

D = 2048
H_MLA = 8
NOPE = 128
ROPE = 64
QK = NOPE + ROPE
HP = 256
VD = 128
Q_LORA = 512
KV_LORA = 256
HM = 4
DM = 256
MW = HM * DM
LCH = 128
CONVW = 5
NG = 16
DFF = 4 * D
EPS = 1e-6
M_INIT = -1e30
ROPE_THETA = 10000.0
IN_SIZES = (Q_LORA, KV_LORA, ROPE, MW, MW, MW, MW, NG)
D_IN = sum(IN_SIZES)
P_QM, P_KM, P_VM, P_OM, P_CQ, P_CKV, P_KPE, P_G = 0, 1024, 2048, 3072, 4096, 4608, 4864, 4992
D_INP = 5120

ADAM_LR, ADAM_B1, ADAM_B2, ADAM_EPS, ADAM_WD, ADAM_STEP = 0.001, 0.9, 0.999, 1e-08, 0.01, 10

V7X_VMEM_LIMIT = 56 * 1024 * 1024


def _cparams(sem):
    return pltpu.CompilerParams(dimension_semantics=sem, vmem_limit_bytes=V7X_VMEM_LIMIT)


def _call(body, **kw):
    if _INTERPRET:
        kw.pop("compiler_params", None)
        kw["interpret"] = pltpu.InterpretParams()
    return pl.pallas_call(body, **kw)


def _dot(a, b, form):
    dims = {"nn": ((1,), (0,)), "nt": ((1,), (1,)), "tn": ((0,), (0,))}[form]
    return lax.dot_general(a.astype(_MXU_DTYPE), b.astype(_MXU_DTYPE), (dims, ((), ())),
                           preferred_element_type=F32)


def _mm(a, b, form, *, name, out_dtypes=(F32,), epilogue=None, extras=(), tm=1024, tn=1024, tk=2048, side=()):
    if form == "nn":
        (M, K), (K2, N) = a.shape, b.shape
    elif form == "nt":
        (M, K), (N, K2) = a.shape, b.shape
    else:
        (K, M), (K2, N) = a.shape, b.shape
    assert K == K2, (a.shape, b.shape, form)
    tm, tn = min(tm, M), min(tn, N)
    tk = max(d for d in range(128, min(tk, K) + 1, 128) if K % d == 0) if K > 128 else K
    assert M % tm == 0 and N % tn == 0 and K % tk == 0, (M, N, K, tm, tn, tk)
    nk = K // tk
    ne, no = len(extras), len(out_dtypes)
    if form == "tn":
        a_spec = pl.BlockSpec((tk, tm), lambda i, j, k: (k, i))
    else:
        a_spec = pl.BlockSpec((tm, tk), lambda i, j, k: (i, k))
    if form == "nt":
        b_spec = pl.BlockSpec((tn, tk), lambda i, j, k: (j, k))
    else:
        b_spec = pl.BlockSpec((tk, tn), lambda i, j, k: (k, j))
    mn_spec = pl.BlockSpec((tm, tn), lambda i, j, k: (i, j))
    grid = (M // tm, N // tn, nk)
    ns, io, wrap = _side_exchange(side, False, grid)

    def body(a_ref, b_ref, *rest):
        ex, outs = rest[:ne], rest[ne + ns:ne + ns + no]
        scratch = rest[ne + 2 * ns + no:]
        side_start, side_wait = wrap(rest[ne:ne + ns], rest[ne + ns + no:ne + 2 * ns + no], scratch[1:])
        side_start()
        prod = _dot(a_ref[...], b_ref[...], form)

        def finish(r):
            vals = (r,) if epilogue is None else epilogue(r, *[e[...] for e in ex])
            for o, v in zip(outs, vals):
                o[...] = v.astype(o.dtype)

        if nk == 1:
            finish(prod)
        else:
            acc, k = scratch[0], pl.program_id(2)

            @pl.when(k == 0)
            def _():
                acc[...] = prod

            @pl.when(k > 0)
            def _():
                acc[...] += prod

            @pl.when(k == nk - 1)
            def _():
                finish(acc[...])
        side_wait()

    res = _call(
        body, name=name, grid=grid,
        in_specs=[a_spec, b_spec] + [mn_spec] * ne + io["specs"],
        out_specs=[mn_spec] * no + io["specs"],
        out_shape=[jax.ShapeDtypeStruct((M, N), dt) for dt in out_dtypes] + io["out_shape"],
        scratch_shapes=[pltpu.VMEM((tm, tn) if nk > 1 else (8, 128), F32)] + io["scratch"],
        compiler_params=_cparams(("arbitrary",) * 3 if ns else ("parallel", "parallel", "arbitrary")),
    )(a, b, *extras, *side)
    if ns:
        return (res[0] if no == 1 else res[:no]), list(res[no:])
    return res[0] if no == 1 else res


def _rowmap(fn, rows, bcasts, outs, accs=(), *, tile, name):
    rows = [r if len(r) == 4 else (*r, None) for r in rows]
    S = rows[0][0].shape[-2]
    tile = min(tile, S)
    assert S % tile == 0
    nr, nb, no, na = len(rows), len(bcasts), len(outs), len(accs)

    def body(*refs):
        vals = [r[...] for r in refs[:nr + nb]]
        o_refs, a_refs = refs[nr + nb:nr + nb + no], refs[nr + nb + no:]
        o_vals, a_vals = fn(*vals)
        for r, v in zip(o_refs, o_vals):
            r[...] = v.astype(r.dtype)
        if na:
            @pl.when(pl.program_id(0) == 0)
            def _():
                for r in a_refs:
                    r[...] = jnp.zeros(r.shape, r.dtype)
            for r, v in zip(a_refs, a_vals):
                r[...] += v

    in_specs = []
    for (arr, w, cb, lead) in rows:
        if lead is None:
            in_specs.append(pl.BlockSpec((tile, w), lambda i, cb=cb: (i, cb)))
        else:
            in_specs.append(pl.BlockSpec((None, tile, w), lambda i, cb=cb, lead=lead: (lead, i, cb)))
    in_specs += [pl.BlockSpec(b.shape, lambda i: (0, 0)) for b in bcasts]
    out_specs = [pl.BlockSpec((tile, w), lambda i: (i, 0)) for (w, _) in outs]
    out_specs += [pl.BlockSpec(s, lambda i: (0, 0)) for s in accs]
    out_shape = [jax.ShapeDtypeStruct((S, w), dt) for (w, dt) in outs]
    out_shape += [jax.ShapeDtypeStruct(s, F32) for s in accs]
    return _call(
        body, name=name, grid=(S // tile,), in_specs=in_specs, out_specs=out_specs, out_shape=out_shape,
        compiler_params=_cparams(("arbitrary",)),
    )(*[r[0] for r in rows], *bcasts)


def _colsum(v):
    return jnp.sum(v, axis=0, keepdims=True)


def _rms(x, n):
    r = lax.rsqrt(jnp.sum(x * x, axis=-1, keepdims=True) * (1.0 / n) + EPS)
    return x * r, r


def _rms_bwd(dxhat, xhat, r, n):
    return r * (dxhat - xhat * (jnp.sum(dxhat * xhat, axis=-1, keepdims=True) * (1.0 / n)))


def _rope_fwd(r, cosp, s1, s2):
    return r * cosp + pltpu.roll(r, 32, 1) * s1 + pltpu.roll(r, 96, 1) * s2


def _rope_bwd(d, cosp, s1, s2):
    return d * cosp + pltpu.roll(d * s1, 96, 1) + pltpu.roll(d * s2, 32, 1)


def _sigmoid(x):
    return 1.0 / (1.0 + jnp.exp(-x))


def _halo_specs(tile, halo, width, cb, S, lead=None):
    nh = tile // halo
    last = S // halo - 1
    if lead is None:
        return [
            pl.BlockSpec((tile, width), lambda i: (i, cb)),
            pl.BlockSpec((halo, width), lambda i: (jnp.maximum(i * nh - 1, 0), cb)),
            pl.BlockSpec((halo, width), lambda i: (jnp.minimum((i + 1) * nh, last), cb)),
        ]
    return [
        pl.BlockSpec((None, tile, width), lambda i: (lead, i, cb)),
        pl.BlockSpec((None, halo, width), lambda i: (lead, jnp.maximum(i * nh - 1, 0), cb)),
        pl.BlockSpec((None, halo, width), lambda i: (lead, jnp.minimum((i + 1) * nh, last), cb)),
    ]


def _conv_fwd(proj, conv_w8, conv_b, *, tile=256):
    S = proj.shape[0]
    T = min(tile, S)
    n = S // T
    W = 2 * MW

    def body(x_ref, xp_ref, xn_ref, w_ref, b_ref, q_ref, k_ref, ext):
        i = pl.program_id(0)
        ext[pl.ds(0, 8), :] = xp_ref[...] * (i > 0).astype(F32)
        ext[pl.ds(8, T), :] = x_ref[...]
        ext[pl.ds(8 + T, 8), :] = xn_ref[...] * (i < n - 1).astype(F32)
        w = w_ref[...]
        y = b_ref[...] + w[0:1, :] * ext[pl.ds(6, T), :]
        for o in range(1, CONVW):
            y = y + w[o:o + 1, :] * ext[pl.ds(6 + o, T), :]
        y = y * _sigmoid(y)
        q_ref[...] = y[:, :MW].astype(q_ref.dtype)
        k_ref[...] = (y[:, MW:] * (DM ** -0.5)).astype(k_ref.dtype)

    return _call(
        body, name="conv_fwd", grid=(n,),
        in_specs=_halo_specs(T, 8, W, 0, S) + [pl.BlockSpec((8, W), lambda i: (0, 0)),
                                                 pl.BlockSpec((1, W), lambda i: (0, 0))],
        out_specs=[pl.BlockSpec((T, MW), lambda i: (i, 0))] * 2,
        out_shape=[jax.ShapeDtypeStruct((S, MW), _MXU_DTYPE)] * 2,
        scratch_shapes=[pltpu.VMEM((T + 16, W), F32)],
        compiler_params=_cparams(("arbitrary",)),
    )(proj, proj, proj, conv_w8, conv_b)


def _conv_bwd(proj, dqd, dkd, conv_w8, conv_b, *, tile=256):
    S = proj.shape[0]
    T = min(tile, S)
    n = S // T
    W = 2 * MW

    def body(x_ref, xp_ref, xn_ref, *rest):
        g = rest[:12]
        w_ref, b_ref, dx_ref, dw_ref, db_ref, ext, edp = rest[12:]
        i = pl.program_id(0)
        mp = (i > 0).astype(F32)
        mn = (i < n - 1).astype(F32)
        ext[pl.ds(0, 16), :] = xp_ref[...] * mp
        ext[pl.ds(16, T), :] = x_ref[...]
        ext[pl.ds(16 + T, 16), :] = xn_ref[...] * mn
        w = w_ref[...]
        pre = b_ref[...] + w[0:1, :] * ext[pl.ds(6, T + 16), :]
        for o in range(1, CONVW):
            pre = pre + w[o:o + 1, :] * ext[pl.ds(6 + o, T + 16), :]
        sg = _sigmoid(pre)
        dsilu = sg * (1.0 + pre * (1.0 - sg))
        for half, (a0, a1) in enumerate(((g[0:3], g[3:6]), (g[6:9], g[9:12]))):
            sc = 1.0 if half == 0 else DM ** -0.5
            cols = pl.ds(half * MW, MW)
            edp[pl.ds(0, 8), cols] = (a0[1][...] + a1[1][...]) * (mp * sc)
            edp[pl.ds(8, T), cols] = (a0[0][...] + a1[0][...]) * sc
            edp[pl.ds(8 + T, 8), cols] = (a0[2][...] + a1[2][...]) * (mn * sc)
        edp[...] = edp[...] * dsilu
        @pl.when(i == 0)
        def _():
            dw_ref[...] = jnp.zeros(dw_ref.shape, F32)
            db_ref[...] = jnp.zeros(db_ref.shape, F32)

        x_main = ext[pl.ds(16, T), :]
        dx = None
        for o in range(CONVW):
            view = edp[pl.ds(10 - o, T), :]
            dx = w[o:o + 1, :] * view if dx is None else dx + w[o:o + 1, :] * view
            dw_ref[pl.ds(o, 1), :] += _colsum(x_main * view)
        dx_ref[...] = dx.astype(dx_ref.dtype)
        db_ref[...] += _colsum(edp[pl.ds(8, T), :])

    gspecs = _halo_specs(T, 8, MW, 0, S) * 4
    return _call(
        body, name="conv_bwd", grid=(n,),
        in_specs=_halo_specs(T, 16, W, 0, S) + gspecs + [pl.BlockSpec((8, W), lambda i: (0, 0)),
                                                          pl.BlockSpec((1, W), lambda i: (0, 0))],
        out_specs=[pl.BlockSpec((T, W), lambda i: (i, 0)), pl.BlockSpec((8, W), lambda i: (0, 0)),
                   pl.BlockSpec((1, W), lambda i: (0, 0))],
        out_shape=[jax.ShapeDtypeStruct((S, W), _MXU_DTYPE), jax.ShapeDtypeStruct((8, W), F32),
                   jax.ShapeDtypeStruct((1, W), F32)],
        scratch_shapes=[pltpu.VMEM((T + 32, W), F32), pltpu.VMEM((T + 16, W), F32)],
        compiler_params=_cparams(("arbitrary",)),
    )(proj, proj, proj, *([dqd[0]] * 3), *([dqd[1]] * 3), *([dkd[0]] * 3), *([dkd[1]] * 3), conv_w8, conv_b)


_ATT_SCALE = QK ** -0.5
_LOG2E = math.log2(math.e)
_Q_PRESCALE = _ATT_SCALE * _LOG2E


def _side_exchange(side, gather, grid, cols=()):
    ns = len(side)
    io = _exchange_io(side, gather, cols) if ns else dict(specs=[], out_shape=[], scratch=[])

    def wrap(refs_in, refs_out, sems):
        if not ns:
            return (lambda: None), (lambda: None)
        start, wait = _exchange_ops(refs_in, refs_out, *sems, gather=gather, cols=cols)
        ids = [pl.program_id(a) for a in range(len(grid))]
        first = functools.reduce(jnp.logical_and, [i == 0 for i in ids])
        last = functools.reduce(jnp.logical_and, [i == g - 1 for i, g in zip(ids, grid)])
        return (lambda: pl.when(first)(start)), (lambda: pl.when(last)(wait))

    return ns, io, wrap


def _side_gather_halved(side, grid, cols, mid_step):
    ns = len(side)
    hbm = pl.BlockSpec(memory_space=pltpu.HBM)
    shape = lambda i, a: (a.shape[0], N_CHIP * a.shape[1]) if i in cols else (N_CHIP, *a.shape)
    io = dict(specs=[hbm] * ns, out_shape=[jax.ShapeDtypeStruct(shape(i, a), a.dtype) for i, a in enumerate(side)],
              scratch=([pltpu.SemaphoreType.DMA((3 * ns,))] * 4 + [pltpu.SemaphoreType.DMA((ns,))]) if ns else [])

    def wrap(ins, outs, sems):
        if not ns:
            return (lambda: None,) * 3
        ici_send, ici_recv, d2d_send, d2d_recv, local_sems = sems
        x, y, c = lax.axis_index("x"), lax.axis_index("y"), lax.axis_index("c")
        k = 2 * x + y
        chips = [(1 - x, y), (x, 1 - y), (1 - x, 1 - y)]
        half = [r.shape[0] // 2 for r in ins]

        def piece(a, chip, core=None):
            rows = slice(None) if core is None else pl.ds(core * half[a], half[a])
            if a in cols:
                width = ins[a].shape[1]
                return outs[a].at[rows, pl.ds(pl.multiple_of(chip * width, 128), width)]
            return outs[a].at[chip, rows]

        def ici(a, j, chip):
            px, py = chips[j]
            return pltpu.make_async_remote_copy(
                src_ref=ins[a].at[pl.ds(c * half[a], half[a])], dst_ref=piece(a, chip, c),
                send_sem=ici_send.at[3 * a + j], recv_sem=ici_recv.at[3 * a + j],
                device_id=(px, py, c), device_id_type=MESH)

        def d2d(a, j, core):
            px, py = chips[j]
            return pltpu.make_async_remote_copy(
                src_ref=piece(a, 2 * px + py, core), dst_ref=piece(a, 2 * px + py, core),
                send_sem=d2d_send.at[3 * a + j], recv_sem=d2d_recv.at[3 * a + j],
                device_id=(x, y, 1 - c), device_id_type=MESH)

        local = [pltpu.make_async_copy(ins[a], piece(a, k), local_sems.at[a]) for a in range(ns)]
        pairs = [(a, j) for a in range(ns) for j in range(3)]

        def start():
            for cp in local + [ici(a, j, k) for a, j in pairs]:
                cp.start()

        def mid():
            for a, j in pairs:
                px, py = chips[j]
                ici(a, j, 2 * px + py).wait_recv()
                d2d(a, j, c).start()

        def wait():
            for a, j in pairs:
                d2d(a, j, 1 - c).wait_recv()
            for a, j in pairs:
                ici(a, j, k).wait_send()
                d2d(a, j, c).wait_send()
            for cp in local:
                cp.wait()

        ids = [pl.program_id(a) for a in range(len(grid))]
        at = lambda step: functools.reduce(jnp.logical_and, [i == s for i, s in zip(ids, step)])
        return (lambda: pl.when(at((0,) * len(grid)))(start), lambda: pl.when(at(mid_step))(mid),
                lambda: pl.when(at(tuple(g - 1 for g in grid)))(wait))

    return ns, io, wrap


def _attn_fwd(q, k, v, *, side=(), side_cols=(), tq=2048, split=8):
    S = q.shape[0]
    tq = min(tq, S)
    hq = tq // split
    grid = (H_MLA, S // tq)
    ns, io, wrap = _side_gather_halved(side, grid, side_cols, (grid[0] * 5 // 8, 0))

    def body(q_ref, k_ref, v_ref, *rest):
        o_ref, qa_ref = rest[ns:ns + 2]
        side_start, side_mid, side_wait = wrap(rest[:ns], rest[ns + 2:2 * ns + 2], rest[2 * ns + 2:])
        side_start()
        side_mid()
        kv, vv = k_ref[...], v_ref[...]
        lane = lax.broadcasted_iota(jnp.int32, (hq, HP), 1)
        for a in range(split):
            r = pl.ds(a * hq, hq)
            qv = q_ref[r, :]
            s = _dot(qv, kv, "nt")
            m = jnp.max(s, axis=1, keepdims=True)
            acc = _dot(jnp.exp2(s - m), vv, "nn")
            l = acc[:, VD:VD + 1]
            o_ref[r, :] = (acc[:, :VD] / l).astype(o_ref.dtype)
            lse = m + jnp.log2(l)
            hi = lse.astype(_MXU_DTYPE).astype(F32)
            qa = jnp.where(lane == QK, -hi, jnp.where(lane == QK + 1, hi - lse, qv.astype(F32)))
            qa_ref[r, :] = qa.astype(qa_ref.dtype)
        side_wait()

    res = _call(
        body, name="attn_fwd", grid=grid,
        in_specs=[pl.BlockSpec((tq, HP), lambda h, i: (i, h)),
                  pl.BlockSpec((S, HP), lambda h, i: (0, h)),
                  pl.BlockSpec((S, HP), lambda h, i: (0, h))] + io["specs"],
        out_specs=[pl.BlockSpec((tq, VD), lambda h, i: (i, h)),
                   pl.BlockSpec((tq, HP), lambda h, i: (i, h))] + io["specs"],
        out_shape=[jax.ShapeDtypeStruct((S, H_MLA * VD), _MXU_DTYPE),
                   jax.ShapeDtypeStruct((S, H_MLA * HP), _MXU_DTYPE)] + io["out_shape"],
        scratch_shapes=io["scratch"],
        compiler_params=_cparams(("arbitrary", "arbitrary")),
    )(q, k, v, *side)
    return res[0], res[1], list(res[2:])


def _attn_bwd(qa, k, va, doa, *, side=(), side_cols=(), tq=8192, tk=1024, split=8, unroll=1):
    S = qa.shape[0]
    tq, tk = min(tq, S), min(tk, S)
    nq, nkb = S // tq, S // tk
    hq = tq // split
    grid = (H_MLA, nkb)
    ns, io, wrap = _side_exchange(side, False, grid, side_cols)

    def body(q_ref, k_ref, v_ref, do_ref, *rest):
        dq_ref, dk_ref, dv_ref = rest[ns:ns + 3]
        side_start, side_wait = wrap(rest[:ns], rest[ns + 3:2 * ns + 3], rest[2 * ns + 3:])
        side_start()
        j = pl.program_id(1)

        @pl.when(j == 0)
        def _():
            dq_ref[...] = jnp.zeros(dq_ref.shape, F32)

        dk_ref[...] = jnp.zeros(dk_ref.shape, F32)
        dv_ref[...] = jnp.zeros(dv_ref.shape, F32)
        kb, vb = k_ref[...], v_ref[...]

        def step(i, carry):
            for a in range(split):
                r = pl.ds(pl.multiple_of(i * tq + a * hq, hq), hq)
                qg, dog = q_ref[r, :], do_ref[r, :]
                p = jnp.exp2(_dot(qg, kb, "nt"))
                ds = (p * _dot(dog, vb, "nt")).astype(_MXU_DTYPE)
                dq_ref[r, :] += _dot(ds, kb, "nn")
                dv_ref[...] += _dot(p, dog, "tn")
                dk_ref[...] += _dot(ds, qg, "tn")
            return carry

        lax.fori_loop(0, nq, step, 0, unroll=unroll if nq % unroll == 0 else 1)
        dk_ref[...] = dk_ref[...] * (1.0 / _LOG2E)

        @pl.when(j == nkb - 1)
        def _():
            dq_ref[...] = dq_ref[...] * _ATT_SCALE

        side_wait()

    blk = pl.BlockSpec((tk, HP), lambda h, j: (j, h))
    whole = pl.BlockSpec((S, HP), lambda h, j: (0, h))
    res = _call(
        body, name="attn_bwd", grid=grid,
        in_specs=[whole, blk, blk, whole] + io["specs"],
        out_specs=[whole, blk, blk] + io["specs"],
        out_shape=[jax.ShapeDtypeStruct((S, H_MLA * HP), F32)] * 3 + io["out_shape"],
        scratch_shapes=io["scratch"],
        compiler_params=_cparams(("arbitrary", "arbitrary")),
    )(qa, k, va, doa, *side)
    return res[0], res[1], res[2], list(res[3:])


def _mlstm_chunk_terms(g, q, k, v, gates, gates_t, bg_row, C, n_row, m):
    L = LCH
    d = g // HM
    h = g % HM
    i_idx = d * 8 + h
    f_idx = d * 8 + 4 + h
    rr = lax.broadcasted_iota(jnp.int32, (L, L), 0)
    cc = lax.broadcasted_iota(jnp.int32, (L, L), 1)
    order = (rr - cc) * (1 - 2 * d)
    tri = order >= 0
    eye = rr == cc
    lane = lax.broadcasted_iota(jnp.int32, gates.shape, 1)
    sub = lax.broadcasted_iota(jnp.int32, gates_t.shape, 0)
    lane_b = lax.broadcasted_iota(jnp.int32, bg_row.shape, 1)
    pick_c = lambda idx: jnp.sum(jnp.where(lane == idx, gates, 0.0), axis=1, keepdims=True)
    pick_r = lambda idx: jnp.sum(jnp.where(sub == idx, gates_t, 0.0), axis=0, keepdims=True)
    pick_b = lambda idx: jnp.sum(jnp.where(lane_b == idx, bg_row, 0.0), axis=1, keepdims=True)
    i_col, i_row = pick_c(i_idx) + pick_b(i_idx), pick_r(i_idx) + pick_b(i_idx)
    f_col, f_row = pick_c(f_idx) + pick_b(f_idx), pick_r(f_idx) + pick_b(f_idx)
    logsig = lambda x: jnp.minimum(x, 0.0) - jnp.log(1.0 + jnp.exp(-jnp.abs(x)))
    lf_col, lf_row = logsig(f_col), logsig(f_row)
    b_col = jnp.sum(jnp.where(tri, lf_row, 0.0), axis=1, keepdims=True)
    tri_t = order <= 0
    b_row = jnp.sum(jnp.where(tri_t, lf_col, 0.0), axis=0, keepdims=True)
    bL = jnp.sum(lf_row, axis=1, keepdims=True)
    log_inter = b_col + m
    logD = jnp.where(tri, b_col - b_row + i_row, -jnp.inf)
    m_t = jnp.maximum(log_inter, jnp.max(logD, axis=1, keepdims=True))
    Dm = jnp.exp(logD - m_t)
    w_inter = jnp.exp(log_inter - m_t)
    A = _dot(q, k, "nt")
    Sc = A * Dm
    numI = _dot(q, C, "nt")
    qf = q.astype(F32)
    kf = k.astype(F32)
    denI = jnp.sum(qf * n_row, axis=1, keepdims=True)
    num = _dot(Sc, v, "nn") + w_inter * numI
    den = jnp.sum(Sc, axis=1, keepdims=True) + w_inter * denI
    floor = jnp.exp(-m_t)
    Nst = jnp.maximum(jnp.abs(den), floor)
    log_w = bL - b_col + i_col
    m_new = jnp.maximum(bL + m, jnp.max(log_w, axis=0, keepdims=True))
    decay = jnp.exp(bL + m - m_new)
    w_col = jnp.exp(log_w - m_new)
    return dict(tri=tri, eye=eye, f_row=f_row, Dm=Dm, w_inter=w_inter, A=A, Sc=Sc, numI=numI, denI=denI,
                num=num, den=den, floor=floor, Nst=Nst, m_new=m_new, decay=decay, w_col=w_col, qf=qf, kf=kf)


def _mlstm_specs(nc, d, step_of):
    chunk = lambda j: step_of(j) if d == 0 else nc - 1 - step_of(j)
    return chunk, [
        pl.BlockSpec((LCH, DM), lambda h, j: (chunk(j), h)),
        pl.BlockSpec((LCH, DM), lambda h, j: (chunk(j), h)),
        pl.BlockSpec((LCH, DM), lambda h, j: (chunk(j), P_VM // DM + h)),
        pl.BlockSpec((LCH, 128), lambda h, j: (chunk(j), P_G // 128)),
        pl.BlockSpec((NG, LCH), lambda h, j: (0, chunk(j))),
    ]


def _mlstm_fwd(qc, kc, proj, gates_t, bg_row):
    S = qc.shape[0]
    nc = S // LCH
    in_specs, out_specs = [], []
    for d in (0, 1):
        chunk, specs = _mlstm_specs(nc, d, lambda j: j)
        in_specs += specs
        out_specs += [pl.BlockSpec((LCH, DM), lambda h, j, chunk=chunk: (chunk(j), h)),
                      pl.BlockSpec((None, None, DM, DM), lambda h, j, chunk=chunk: (h, chunk(j), 0, 0)),
                      pl.BlockSpec((None, None, 8, DM), lambda h, j, chunk=chunk: (h, chunk(j), 0, 0))]
    in_specs.append(pl.BlockSpec((1, 128), lambda h, j: (0, 0)))

    def body(*refs):
        bg_ref, outs, (C_s, n_s, m_s) = refs[10], refs[11:17], refs[17:]

        @pl.when(pl.program_id(1) == 0)
        def _():
            C_s[...] = jnp.zeros(C_s.shape, F32)
            n_s[...] = jnp.zeros(n_s.shape, F32)
            m_s[...] = jnp.full(m_s.shape, M_INIT, F32)

        for d in (0, 1):
            q_ref, k_ref, v_ref, g_ref, gt_ref = refs[5 * d:5 * d + 5]
            h_ref, cst_ref, nm_ref = outs[3 * d:3 * d + 3]
            g = d * HM + pl.program_id(0)
            C, n_row, m = C_s[d], n_s[d, 0:1, :], m_s[d, 0:1, 0:1]
            cst_ref[...] = C
            nm_ref[0:1, :] = n_row
            nm_ref[1:2, :] = jnp.broadcast_to(m, (1, DM))
            nm_ref[2:8, :] = jnp.zeros((6, DM), F32)
            q, k, v = q_ref[...], k_ref[...], v_ref[...]
            t = _mlstm_chunk_terms(g, q, k, v, g_ref[...], gt_ref[...], bg_ref[...], C, n_row, m)
            h_ref[...] = t["num"] / t["Nst"]
            wv = t["w_col"] * v
            C_s[d] = t["decay"] * C + _dot(wv, k, "tn")
            n_s[d, 0:1, :] = t["decay"] * n_row + _colsum(t["w_col"] * t["kf"])
            m_s[d] = jnp.broadcast_to(t["m_new"], (8, 128))

    res = _call(
        body, name="mlstm_fwd", grid=(HM, nc), in_specs=in_specs, out_specs=out_specs,
        out_shape=[jax.ShapeDtypeStruct((S, MW), F32), jax.ShapeDtypeStruct((HM, nc, DM, DM), F32),
                   jax.ShapeDtypeStruct((HM, nc, 8, DM), F32)] * 2,
        scratch_shapes=[pltpu.VMEM((2, DM, DM), F32), pltpu.VMEM((2, 8, DM), F32), pltpu.VMEM((2, 8, 128), F32)],
        compiler_params=_cparams(("parallel", "arbitrary")),
    )(*([qc, kc, proj, proj, gates_t] * 2), bg_row)
    return (res[0], res[3]), (res[1], res[4]), (res[2], res[5])


def _mlstm_bwd(qc, kc, proj, gates_t, bg_row, dh, cst, nm):
    S = qc.shape[0]
    nc = S // LCH
    in_specs, out_specs = [], []
    for d in (0, 1):
        chunk, specs = _mlstm_specs(nc, d, lambda j: nc - 1 - j)
        in_specs += specs + [pl.BlockSpec((LCH, DM), lambda h, j, chunk=chunk: (chunk(j), h)),
                             pl.BlockSpec((None, None, DM, DM), lambda h, j, chunk=chunk: (h, chunk(j), 0, 0)),
                             pl.BlockSpec((None, None, 8, DM), lambda h, j, chunk=chunk: (h, chunk(j), 0, 0))]
        out_specs += [pl.BlockSpec((LCH, DM), lambda h, j, chunk=chunk: (chunk(j), h))] * 3
        out_specs += [pl.BlockSpec((None, None, 8, LCH), lambda h, j, chunk=chunk: (h, chunk(j), 0, 0))]
    in_specs.append(pl.BlockSpec((1, 128), lambda h, j: (0, 0)))

    def body(*refs):
        bg_ref, outs, (dC_s, dn_s) = refs[16], refs[17:25], refs[25:]

        @pl.when(pl.program_id(1) == 0)
        def _():
            dC_s[...] = jnp.zeros(dC_s.shape, F32)
            dn_s[...] = jnp.zeros(dn_s.shape, F32)

        for d in (0, 1):
            _mlstm_bwd_chain(d, refs[8 * d:8 * d + 8], bg_ref, outs[4 * d:4 * d + 4], dC_s, dn_s)

    res = _call(
        body, name="mlstm_bwd", grid=(HM, nc), in_specs=in_specs, out_specs=out_specs,
        out_shape=([jax.ShapeDtypeStruct((S, MW), F32)] * 3 + [jax.ShapeDtypeStruct((HM, nc, 8, LCH), F32)]) * 2,
        scratch_shapes=[pltpu.VMEM((2, DM, DM), F32), pltpu.VMEM((2, 8, DM), F32)],
        compiler_params=_cparams(("parallel", "arbitrary")),
    )(*[a for d in (0, 1) for a in (qc, kc, proj, proj, gates_t, dh, cst[d], nm[d])], bg_row)
    return (res[0], res[4]), (res[1], res[5]), (res[2], res[6]), (res[3], res[7])


def _mlstm_bwd_chain(d, ins, bg_ref, outs, dC_s, dn_s):
        q_ref, k_ref, v_ref, g_ref, gt_ref, dh_ref, cst_ref, nm_ref = ins
        dq_ref, dk_ref, dv_ref, dg_ref = outs
        g = d * HM + pl.program_id(0)
        C, n_row, m = cst_ref[...], nm_ref[0:1, :], nm_ref[1:2, 0:1]
        q, k, v = q_ref[...], k_ref[...], v_ref[...]
        t = _mlstm_chunk_terms(g, q, k, v, g_ref[...], gt_ref[...], bg_ref[...], C, n_row, m)
        tri, eye, qf, kf = t["tri"], t["eye"], t["qf"], t["kf"]
        w_inter, w_col, decay, Nst = t["w_inter"], t["w_col"], t["decay"], t["Nst"]
        dC, dn = dC_s[d], dn_s[d, 0:1, :]
        dhv = dh_ref[...]
        hval = t["num"] / Nst
        dnum = dhv / Nst
        dNst = -jnp.sum(dhv * hval, axis=1, keepdims=True) / Nst
        dden = jnp.where(jnp.abs(t["den"]) > t["floor"], jnp.sign(t["den"]) * dNst, 0.0)
        dSc = _dot(dnum, v, "nt") + dden
        dA = dSc * t["Dm"]
        G = dSc * t["Sc"]
        KdC = _dot(k, dC, "nt")
        dq = _dot(dA, k, "nn") + w_inter * _dot(dnum, C, "nn") + (w_inter * dden) * n_row
        dk = _dot(dA, q, "tn") + w_col * _dot(v, dC, "nn") + w_col * dn
        dv = _dot(t["Sc"], dnum, "tn") + w_col * KdC
        dq_ref[...] = dq
        dk_ref[...] = dk
        dv_ref[...] = dv
        dlog_inter = w_inter * (jnp.sum(dnum * t["numI"], axis=1, keepdims=True) + dden * t["denI"])
        rowG = jnp.sum(G, axis=1, keepdims=True)
        colG = jnp.sum(G, axis=0, keepdims=True)
        u_col = w_col * (jnp.sum(v * KdC, axis=1, keepdims=True) + jnp.sum(kf * dn, axis=1, keepdims=True))
        colG_c = jnp.sum(jnp.where(eye, colG, 0.0), axis=1, keepdims=True)
        u_row = jnp.sum(jnp.where(eye, u_col, 0.0), axis=0, keepdims=True)
        db_col = rowG + dlog_inter - u_col - colG_c
        dbL = jnp.sum(u_col, axis=0, keepdims=True) + decay * (
            jnp.sum(jnp.sum(dC * C, axis=1, keepdims=True), axis=0, keepdims=True)
            + jnp.sum(dn * n_row, axis=1, keepdims=True))
        dlf_row = jnp.sum(jnp.where(tri, db_col, 0.0), axis=0, keepdims=True) + dbL
        di_row = colG + u_row
        df_row = dlf_row * (1.0 - _sigmoid(t["f_row"]))
        dg_ref[...] = jnp.zeros(dg_ref.shape, F32)
        dg_ref[0:1, :] = di_row
        dg_ref[1:2, :] = df_row
        dC_s[d] = decay * dC + _dot(w_inter * dnum, q, "tn")
        dn_s[d, 0:1, :] = decay * dn + _colsum((w_inter * dden) * qf)


def _pad_w_in(w):
    cq, ckv, kpe, qm, km, vm, om, gt = _split_in(w)
    z = lambda n: jnp.zeros((w.shape[0], n), w.dtype)
    return jnp.concatenate([qm, km, vm, om, cq, ckv, kpe, z(HP - QK), gt, z(128 - NG)], axis=1)


def _split_in(w):
    out, o = [], 0
    for n in IN_SIZES:
        out.append(w[:, o:o + n])
        o += n
    return out


def _unpad_w_in(g):
    return jnp.concatenate([g[:, P_CQ:P_CQ + Q_LORA], g[:, P_CKV:P_CKV + KV_LORA], g[:, P_KPE:P_KPE + ROPE],
                            g[:, 0:4 * MW], g[:, P_G:P_G + NG]], axis=1)


_IN_SHARD = D_IN // 4
_IN_SEGMENTS = ((0, 512, P_CQ), (512, 768, P_CKV), (768, 832, P_KPE), (832, 4928, P_QM), (4928, 4944, P_G))


def _pad_w_in_slabs(slabs):
    def orig(a, b):
        out = []
        for k in range(4):
            lo, hi = max(a, k * _IN_SHARD), min(b, (k + 1) * _IN_SHARD)
            if lo < hi:
                out.append(slabs[k][:, lo - k * _IN_SHARD:hi - k * _IN_SHARD])
        return out

    z = lambda n: jnp.zeros((slabs.shape[1], n), slabs.dtype)
    return jnp.concatenate(orig(832, 4928) + orig(0, 512) + orig(512, 768) + orig(768, 832) + [z(HP - QK)]
                           + orig(4928, 4944) + [z(128 - NG)], axis=1)


def _unpad_w_in_slabs(g):
    slabs = []
    for k in range(4):
        pieces = []
        for a, b, p in _IN_SEGMENTS:
            lo, hi = max(a, k * _IN_SHARD), min(b, (k + 1) * _IN_SHARD)
            if lo < hi:
                pieces.append(g[:, p + lo - a:p + hi - a])
        slabs.append(jnp.concatenate(pieces, axis=1))
    return jnp.stack(slabs)


def _pad_w_uq(w):
    return jnp.pad(w.reshape(Q_LORA, H_MLA, QK), ((0, 0), (0, 0), (0, HP - QK))).reshape(Q_LORA, H_MLA * HP)


def _unpad_w_uq(g):
    return g.reshape(Q_LORA, H_MLA, HP)[:, :, :QK].reshape(Q_LORA, H_MLA * QK)


def _perm_w_ukv(w):
    return w.reshape(KV_LORA, H_MLA, 2, NOPE).transpose(0, 2, 1, 3).reshape(KV_LORA, 2 * H_MLA * NOPE)


def _unperm_w_ukv(g):
    return g.reshape(KV_LORA, 2, H_MLA, NOPE).transpose(0, 2, 1, 3).reshape(KV_LORA, 2 * H_MLA * NOPE)


def _rope_tables(positions):
    half = ROPE // 2
    freqs = ROPE_THETA ** (-jnp.arange(half, dtype=F32) / half)
    ang = positions.astype(F32)[:, None] * freqs
    cos, sin = jnp.cos(ang), jnp.sin(ang)
    z32, z64 = jnp.zeros_like(cos), jnp.zeros((cos.shape[0], 64), F32)
    return (jnp.concatenate([cos, cos, z64], axis=1), jnp.concatenate([z32, sin, z64], axis=1),
            jnp.concatenate([-sin, z32, z64], axis=1))


def _device_step(x, tgt, positions, modv, W, late=None):
    S = x.shape[0]
    MX = _MXU_DTYPE
    cosp, rs1, rs2 = _rope_tables(positions)
    tabs = [(cosp, 128, 0), (rs1, 128, 0), (rs2, 128, 0)]
    cat1 = lambda vs: jnp.concatenate(vs, axis=1)
    hsl = lambda hh, w: slice(hh * w, (hh + 1) * w)

    def ln1(xv, g, mv):
        xhat, _ = _rms(xv, D)
        return [xhat * g * (1.0 + mv[1:2]) + mv[0:1]], []

    (h,) = _rowmap(ln1, [(x, D, 0)], [W["g_mix"], modv], [(D, MX)], tile=512, name="ln1")
    proj = _mm(h, W["w_in"], "nn", name="proj")

    def lora(cq, ckv, gq, gkv):
        return [_rms(cq, Q_LORA)[0] * gq, _rms(ckv, KV_LORA)[0] * gkv], []

    cqn, ckvn = _rowmap(lora, [(proj, Q_LORA, P_CQ // Q_LORA), (proj, KV_LORA, P_CKV // KV_LORA)],
                        [W["g_qlora"], W["g_kvlora"]], [(Q_LORA, MX), (KV_LORA, MX)], tile=512, name="lora_norm")
    q_raw = _mm(cqn, W["w_uq"], "nn", name="q_up")
    kv_raw = _mm(ckvn, W["w_ukv"], "nn", name="kv_up")

    def mla_q(qr, cp, a1, a2, gq):
        outs = []
        for hh in range(H_MLA):
            y = _rms(qr[:, hsl(hh, HP)], QK)[0] * gq
            outs += [y[:, :NOPE], _rope_fwd(y[:, NOPE:], cp, a1, a2)]
        return [cat1(outs) * _Q_PRESCALE], []

    (qh,) = _rowmap(mla_q, [(q_raw, H_MLA * HP, 0)] + tabs, [W["gq"]], [(H_MLA * HP, MX)], tile=512, name="mla_q")

    def mla_k(kvr, kpe, cp, a1, a2, gk):
        lane = lax.broadcasted_iota(jnp.int32, (kvr.shape[0], 128), 1)
        outs, vas = [], []
        for hh in range(H_MLA):
            y = _rms(cat1([kvr[:, hsl(hh, NOPE)], kpe]), QK)[0] * gk
            outs += [y[:, :NOPE], _rope_fwd(y[:, NOPE:], cp, a1, a2) + ((lane == ROPE) | (lane == ROPE + 1)).astype(F32)]
            vas += [kvr[:, H_MLA * NOPE + hh * VD:H_MLA * NOPE + (hh + 1) * VD], (lane < 2).astype(F32)]
        return [cat1(outs), cat1(vas)], []

    kh, va = _rowmap(mla_k, [(kv_raw, 2 * H_MLA * NOPE, 0), (proj, 128, P_KPE // 128)] + tabs, [W["gk"]],
                     [(H_MLA * HP, MX), (H_MLA * HP, MX)], tile=512, name="mla_k")
    attn_o, qa, gathered = _attn_fwd(qh, kh, va, side=late or (), side_cols=(1,))
    if late:
        W = dict(W, w_out=gathered[0].reshape(D, D), w_ff1=gathered[1], w_ff2=gathered[2].reshape(DFF, D))

    qc, kc = _conv_fwd(proj, W["conv_w8"], W["conv_b"])
    gates_t = proj[:, P_G:P_G + NG].T
    (h_f, h_b), cst, nm = _mlstm_fwd(qc, kc, proj, gates_t, W["bg_row"])
    hrows = [(h_f, MW, 0), (h_b, MW, 0), (proj, MW, P_OM // MW)]

    def ml_out(ao, hf, hb, om, gmn):
        outs = [ao.astype(F32)]
        hs = hf + hb
        for hh in range(HM):
            sl = hsl(hh, DM)
            outs.append(_sigmoid(om[:, sl]) * _rms(hs[:, sl], DM)[0] * gmn[:, sl])
        return [cat1(outs)], []

    (cat,) = _rowmap(ml_out, [(attn_o, MW, 0)] + hrows, [W["g_mn"]], [(D, MX)], tile=512, name="ml_out")
    mixed = _mm(cat, W["w_out"], "nn", name="out_proj")

    def res_ln2(xv, mx, g, mv):
        x1 = xv + mv[2:3] * mx
        return [x1, _rms(x1, D)[0] * g * (1.0 + mv[4:5]) + mv[3:4]], []

    x1, h2 = _rowmap(res_ln2, [(x, D, 0), (mixed, D, 0)], [W["g_mlp"], modv], [(D, F32), (D, MX)],
                     tile=512, name="res_ln2")
    a, u = _mm(h2, W["w_ff1"], "nn", name="ff1", out_dtypes=(MX, MX),
               epilogue=lambda r: (jnp.square(jnp.maximum(r, 0.0)), r))
    y = _mm(a, W["w_ff2"], "nn", name="ff2")

    def final(x1v, yv, tv, mv):
        err = x1v + mv[5:6] * yv - tv
        dout = err * (1.0 / D)
        loss = jnp.sum(jnp.sum(0.5 * err * dout, axis=1, keepdims=True), axis=0, keepdims=True)
        return [dout, mv[5:6] * dout], [loss, _colsum(dout * yv)]

    dout, dy, loss, dgate2 = _rowmap(final, [(x1, D, 0), (y, D, 0), (tgt, D, 0)], [modv], [(D, F32), (D, MX)],
                                     [(1, 1), (1, D)], tile=256, name="loss_head")

    du = _mm(dy, W["w_ff2"], "nt", name="ff2_dx", out_dtypes=(MX,), extras=(u,),
             epilogue=lambda r, uv: (r * (2.0 * jnp.maximum(uv.astype(F32), 0.0)),))
    gdt = (MX,)
    g_ff2 = _mm(a, dy, "tn", name="ff2_dw", out_dtypes=gdt)
    dh2 = _mm(du, W["w_ff1"], "nt", name="ff1_dx")
    g_ff1 = _mm(h2, du, "tn", name="ff1_dw", out_dtypes=gdt)

    def ln2_bwd(dh2v, x1v, doutv, mxv, g, mv):
        xhat, r = _rms(x1v, D)
        dn2 = dh2v * (1.0 + mv[4:5])
        dx1 = doutv + _rms_bwd(dn2 * g, xhat, r, D)
        return [dx1, mv[2:3] * dx1], [_colsum(dh2v), _colsum(dh2v * xhat * g), _colsum(dn2 * xhat), _colsum(dx1 * mxv)]

    dx1, dmixed, dshift2, dscale2, dg_mlp, dgate1 = _rowmap(
        ln2_bwd, [(dh2, D, 0), (x1, D, 0), (dout, D, 0), (mixed, D, 0)], [W["g_mlp"], modv],
        [(D, F32), (D, MX)], [(1, D)] * 4, tile=256, name="ln2_bwd")
    dcat = _mm(dmixed, W["w_out"], "nt", name="out_dx")
    g_out = _mm(cat, dmixed, "tn", name="out_dw", out_dtypes=gdt)

    def ml_out_bwd(dml, hf, hb, om, gmn):
        hs = hf + hb
        dhs, dos, dgs = [], [], []
        for hh in range(HM):
            sl = hsl(hh, DM)
            xhat, r = _rms(hs[:, sl], DM)
            g, sg, d = gmn[:, sl], _sigmoid(om[:, sl]), dml[:, sl]
            dos.append(d * xhat * g * sg * (1.0 - sg))
            dhn = d * sg
            dgs.append(_colsum(dhn * xhat))
            dhs.append(_rms_bwd(dhn * g, xhat, r, DM))
        return [cat1(dhs), cat1(dos)], [cat1(dgs)]

    dhs, do_m, dg_mn = _rowmap(ml_out_bwd, [(dcat, MW, 1)] + hrows, [W["g_mn"]], [(MW, F32), (MW, MX)],
                               [(1, MW)], tile=512, name="ml_out_bwd")
    dqd, dkd, dvd, dgates = _mlstm_bwd(qc, kc, proj, gates_t, W["bg_row"], dhs, cst, nm)
    dqk_m, dconv_w8, dconv_b = _conv_bwd(proj, dqd, dkd, W["conv_w8"], W["conv_b"])

    def do_aug(ao, dov):
        lane = lax.broadcasted_iota(jnp.int32, (ao.shape[0], 128), 1)
        outs = []
        for hh in range(H_MLA):
            sl = hsl(hh, VD)
            dl = jnp.sum(ao[:, sl].astype(F32) * dov[:, sl], axis=1, keepdims=True)
            hi = dl.astype(MX).astype(F32)
            outs += [dov[:, sl], jnp.where(lane == 0, -hi, jnp.where(lane == 1, hi - dl, 0.0))]
        return [cat1(outs)], []

    (doa,) = _rowmap(do_aug, [(attn_o, MW, 0), (dcat, MW, 0)], [], [(H_MLA * HP, MX)], tile=512, name="attn_delta")
    side = [g_out.reshape(N_CHIP, D // N_CHIP, D), g_ff1, g_ff2.reshape(N_CHIP, DFF // N_CHIP, D)] if late else ()
    dq_a, dk_a, dv_a, late_got = _attn_bwd(qa, kh, va, doa, side=side, side_cols=(1,))

    def mla_q_bwd(dqv, qr, cp, a1, a2, gq):
        outs, dg = [], 0.0
        for hh in range(H_MLA):
            sl = hsl(hh, HP)
            xhat, r = _rms(qr[:, sl], QK)
            d = dqv[:, sl]
            dyv = cat1([d[:, :NOPE], _rope_bwd(d[:, NOPE:], cp, a1, a2)])
            dg = dg + _colsum(dyv * xhat)
            outs.append(_rms_bwd(dyv * gq, xhat, r, QK))
        return [cat1(outs)], [dg]

    dq_raw, dgq = _rowmap(mla_q_bwd, [(dq_a, H_MLA * HP, 0), (q_raw, H_MLA * HP, 0)] + tabs, [W["gq"]],
                          [(H_MLA * HP, MX)], [(1, HP)], tile=512, name="mla_q_bwd")
    dcqn = _mm(dq_raw, W["w_uq"], "nt", name="q_up_dx")
    g_uq = _mm(cqn, dq_raw, "tn", name="q_up_dw", out_dtypes=gdt)

    def mla_k_bwd(dkv, dvv, kvr, kpe, cp, a1, a2, gk):
        dkn, dg, dkpe = [], 0.0, 0.0
        for hh in range(H_MLA):
            xhat, r = _rms(cat1([kvr[:, hsl(hh, NOPE)], kpe]), QK)
            d = dkv[:, hsl(hh, HP)]
            dyv = cat1([d[:, :NOPE], _rope_bwd(d[:, NOPE:], cp, a1, a2)])
            dg = dg + _colsum(dyv * xhat)
            dxv = _rms_bwd(dyv * gk, xhat, r, QK)
            dkn.append(dxv[:, :NOPE])
            dkpe = dkpe + dxv[:, NOPE:]
        return [cat1(dkn + [dvv[:, hh * HP:hh * HP + VD] for hh in range(H_MLA)]), dkpe], [dg]

    dkv_raw, dkpe, dgk = _rowmap(
        mla_k_bwd, [(dk_a, H_MLA * HP, 0), (dv_a, H_MLA * HP, 0), (kv_raw, 2 * H_MLA * NOPE, 0),
                    (proj, 128, P_KPE // 128)] + tabs, [W["gk"]],
        [(2 * H_MLA * NOPE, MX), (128, MX)], [(1, HP)], tile=256, name="mla_k_bwd")
    dckvn = _mm(dkv_raw, W["w_ukv"], "nt", name="kv_up_dx")
    g_ukv = _mm(ckvn, dkv_raw, "tn", name="kv_up_dw", out_dtypes=gdt)

    def lora_bwd(dcq, dckv, cq, ckv, gq, gkv):
        xq, rq = _rms(cq, Q_LORA)
        xk, rk = _rms(ckv, KV_LORA)
        return ([_rms_bwd(dcq * gq, xq, rq, Q_LORA), _rms_bwd(dckv * gkv, xk, rk, KV_LORA)],
                [_colsum(dcq * xq), _colsum(dckv * xk)])

    dc_q, dc_kv, dg_qlora, dg_kvlora = _rowmap(
        lora_bwd, [(dcqn, Q_LORA, 0), (dckvn, KV_LORA, 0), (proj, Q_LORA, P_CQ // Q_LORA),
                   (proj, KV_LORA, P_CKV // KV_LORA)], [W["g_qlora"], W["g_kvlora"]],
        [(Q_LORA, MX), (KV_LORA, MX)], [(1, Q_LORA), (1, KV_LORA)], tile=512, name="lora_bwd")

    nc = S // LCH
    dg16 = jnp.stack(dgates)[:, :, :, 0:2, :].transpose(2, 4, 0, 3, 1).reshape(S, NG)
    dg128 = jnp.pad(dg16, ((0, 0), (0, 128 - NG)))

    def assemble(dqk, dv0, dv1, dom, dcq, dckv, dkp, dgp):
        f = lambda t: t.astype(F32)
        return [cat1([f(dqk), dv0 + dv1, f(dom), f(dcq), f(dckv), f(dkp), dgp])], [_colsum(dgp)]

    dproj, dbg = _rowmap(
        assemble, [(dqk_m, 2 * MW, 0), (dvd[0], MW, 0), (dvd[1], MW, 0), (do_m, MW, 0), (dc_q, Q_LORA, 0),
                   (dc_kv, KV_LORA, 0), (dkpe, 128, 0), (dg128, 128, 0)], [], [(D_INP, MX)], [(1, 128)],
        tile=256, name="dproj")
    g_in = _mm(h, dproj, "tn", name="proj_dw", out_dtypes=gdt)
    early_got = ()
    if late:
        side = [_unpad_w_in_slabs(g_in).astype(MX), _slabs(_unpad_w_uq(g_uq)).astype(MX),
                _slabs(_unperm_w_ukv(g_ukv)).astype(MX)]
        dh, early_got = _mm(dproj, W["w_in"], "nt", name="proj_dx", side=side)
    else:
        dh = _mm(dproj, W["w_in"], "nt", name="proj_dx")

    def ln1_bwd(dhv, xv, dx1v, g, mv):
        xhat, r = _rms(xv, D)
        dn = dhv * (1.0 + mv[1:2])
        return [dx1v + _rms_bwd(dn * g, xhat, r, D)], [_colsum(dhv), _colsum(dhv * xhat * g), _colsum(dn * xhat)]

    gx, dshift1, dscale1, dg_mix = _rowmap(ln1_bwd, [(dh, D, 0), (x, D, 0), (dx1, D, 0)], [W["g_mix"], modv],
                                           [(D, F32)], [(1, D)] * 3, tile=256, name="ln1_bwd")
    dmodv = jnp.concatenate([dshift1, dscale1, dgate1, dshift2, dscale2, dgate2], axis=0)
    grads = dict(w_in=g_in, w_uq=g_uq, w_ukv=g_ukv, w_out=g_out, w_ff1=g_ff1, w_ff2=g_ff2,
                 norm_mix_g=dg_mix, b_gates=dbg[:, :NG], conv_w=dconv_w8[:CONVW], conv_b=dconv_b,
                 q_lora_g=dg_qlora, kv_lora_g=dg_kvlora, q_norm_g=dgq[:, :QK], k_norm_g=dgk[:, :QK],
                 mlstm_norm_g=dg_mn, norm_mlp_g=dg_mlp)
    grads["got"] = list(early_got) + list(late_got)
    return loss, gx, dmodv, grads


def _cols(g):
    return g.transpose(1, 0, 2).reshape(g.shape[1], N_CHIP * g.shape[2])


def _slabs(gfull):
    return gfull.reshape(gfull.shape[0], N_CHIP, -1).transpose(1, 0, 2)


def _prep_weights(w_in, w_uq, w_ukv, w_out, w_ff1, w_ff2, norm_mix_g, norm_mlp_g, q_lora_g, kv_lora_g,
                  q_norm_g, k_norm_g, mlstm_norm_g, conv_w, conv_b, b_gates):
    MX = _MXU_DTYPE
    padg = lambda g: jnp.pad(g.reshape(1, QK).astype(F32), ((0, 0), (0, HP - QK)))
    return dict(
        w_in=(_pad_w_in_slabs(w_in) if w_in.ndim == 3 else _pad_w_in(w_in)).astype(MX), w_uq=_pad_w_uq(w_uq).astype(MX), w_ukv=_perm_w_ukv(w_ukv).astype(MX),
        w_out=None if w_out is None else w_out.astype(MX), w_ff1=None if w_ff1 is None else w_ff1.astype(MX),
        w_ff2=None if w_ff2 is None else w_ff2.astype(MX),
        g_mix=norm_mix_g.reshape(1, D), g_mlp=norm_mlp_g.reshape(1, D), g_qlora=q_lora_g.reshape(1, Q_LORA),
        g_kvlora=kv_lora_g.reshape(1, KV_LORA), gq=padg(q_norm_g), gk=padg(k_norm_g),
        g_mn=mlstm_norm_g.reshape(1, MW), conv_w8=jnp.pad(conv_w.reshape(CONVW, 2 * MW), ((0, 8 - CONVW), (0, 0))),
        conv_b=conv_b.reshape(1, 2 * MW), bg_row=jnp.pad(b_gates.reshape(1, NG), ((0, 0), (0, 128 - NG))))


MESH = pl.DeviceIdType.MESH
N_DEV = 8
N_CHIP = 4


def _comm_call(body, **kw):
    if _INTERPRET:
        kw["interpret"] = pltpu.InterpretParams()
    return pl.pallas_call(body, **kw)


def _allgather8(blk, *, name):
    m_per, n = blk.shape

    def body(x_ref, out_ref, send_sems, recv_sems, local_sem):
        x, y, c = lax.axis_index("x"), lax.axis_index("y"), lax.axis_index("c")
        me, sibling = (x, y, c), (x, y, 1 - c)
        chips = [(1 - x, y), (x, 1 - y), (1 - x, 1 - y)]

        def rows(px, py, pc):
            return out_ref.at[pl.ds((4 * px + 2 * py + pc) * m_per, m_per), :]

        def copy(k, block, to, src=None):
            return pltpu.make_async_remote_copy(
                src_ref=rows(*block) if src is None else src, dst_ref=rows(*block),
                send_sem=send_sems.at[k], recv_sem=recv_sems.at[k], device_id=to, device_id_type=MESH)

        mine = pltpu.make_async_copy(x_ref, rows(*me), local_sem)
        mine.start()
        first = [copy(0, me, sibling, src=x_ref)]
        first += [copy(1 + j, me, (*chip, c), src=x_ref) for j, chip in enumerate(chips)]
        for cp in first:
            cp.start()
        passed = [copy(4 + j, (*chip, c), sibling) for j, chip in enumerate(chips)]
        for j, chip in enumerate(chips):
            copy(1 + j, (*chip, c), me).wait_recv()
            passed[j].start()
        copy(0, sibling, me).wait_recv()
        for j, chip in enumerate(chips):
            copy(4 + j, (*chip, 1 - c), me).wait_recv()
        for cp in first + passed:
            cp.wait_send()
        mine.wait()

    return _comm_call(
        body, name=name, out_shape=jax.ShapeDtypeStruct((N_DEV * m_per, n), blk.dtype),
        in_specs=[pl.BlockSpec(memory_space=pltpu.VMEM)], out_specs=pl.BlockSpec(memory_space=pltpu.VMEM),
        scratch_shapes=[pltpu.SemaphoreType.DMA((7,)), pltpu.SemaphoreType.DMA((7,)), pltpu.SemaphoreType.DMA],
    )(blk)


def _exchange_io(arrays, gather, cols=()):
    n = len(arrays)

    def out(i, a):
        if gather:
            return (a.shape[0], N_CHIP * a.shape[1]) if i in cols else (N_CHIP, *a.shape)
        return (N_CHIP, a.shape[0], a.shape[1] // N_CHIP) if i in cols else a.shape

    return dict(
        specs=[pl.BlockSpec(memory_space=pltpu.HBM)] * n,
        out_shape=[jax.ShapeDtypeStruct(out(i, a), a.dtype) for i, a in enumerate(arrays)],
        scratch=[pltpu.SemaphoreType.DMA((3 * n,)), pltpu.SemaphoreType.DMA((3 * n,)), pltpu.SemaphoreType.DMA((n,))])


def _exchange_ops(ins, outs, send_sems, recv_sems, local_sems, *, gather, cols=()):
    n = len(ins)
    x, y, c = lax.axis_index("x"), lax.axis_index("y"), lax.axis_index("c")
    k = 2 * x + y
    chips = [(1 - x, y), (x, 1 - y), (1 - x, 1 - y)]

    def piece(ref, a, chip, windowed):
        if not windowed:
            return ref.at[chip]
        width = ref.shape[1] // N_CHIP
        return ref.at[:, pl.ds(pl.multiple_of(chip * width, 128), width)]

    src_of = lambda a, chip: ins[a] if gather else piece(ins[a], a, chip, a in cols)
    dst_of = lambda a, chip: piece(outs[a], a, chip, gather and a in cols)

    def remote(a, j):
        px, py = chips[j]
        return pltpu.make_async_remote_copy(
            src_ref=src_of(a, 2 * px + py), dst_ref=dst_of(a, k), send_sem=send_sems.at[3 * a + j],
            recv_sem=recv_sems.at[3 * a + j], device_id=(px, py, c), device_id_type=MESH)

    def arrival(a, j):
        px, py = chips[j]
        return pltpu.make_async_remote_copy(
            src_ref=src_of(a, k), dst_ref=dst_of(a, 2 * px + py), send_sem=send_sems.at[3 * a + j],
            recv_sem=recv_sems.at[3 * a + j], device_id=(px, py, c), device_id_type=MESH)

    local = [pltpu.make_async_copy(src_of(a, k), dst_of(a, k), local_sems.at[a]) for a in range(n)]
    sent = [remote(a, j) for a in range(n) for j in range(3)]

    def start():
        for cp in local + sent:
            cp.start()

    def wait():
        for a in range(n):
            for j in range(3):
                arrival(a, j).wait_recv()
        for cp in sent:
            cp.wait_send()
        for cp in local:
            cp.wait()

    return start, wait


def _chip_allgather_halved(shards, *, name):
    n = len(shards)
    half_rows = [s.shape[0] // 2 for s in shards]
    assert all(s.shape[0] % 16 == 0 for s in shards)

    def body(*refs):
        ins, outs = refs[:n], refs[n:2 * n]
        ici_send, ici_recv, d2d_send, d2d_recv, local_sems = refs[2 * n:]
        x, y, c = lax.axis_index("x"), lax.axis_index("y"), lax.axis_index("c")
        k = 2 * x + y
        chips = [(1 - x, y), (x, 1 - y), (1 - x, 1 - y)]

        def half(a, slab, core):
            return outs[a].at[slab, pl.ds(core * half_rows[a], half_rows[a])]

        def ici(a, j, slab):
            px, py = chips[j]
            return pltpu.make_async_remote_copy(
                src_ref=ins[a].at[pl.ds(c * half_rows[a], half_rows[a])], dst_ref=half(a, slab, c),
                send_sem=ici_send.at[3 * a + j], recv_sem=ici_recv.at[3 * a + j],
                device_id=(px, py, c), device_id_type=MESH)

        def d2d(a, j, core):
            px, py = chips[j]
            return pltpu.make_async_remote_copy(
                src_ref=half(a, 2 * px + py, core), dst_ref=half(a, 2 * px + py, core),
                send_sem=d2d_send.at[3 * a + j], recv_sem=d2d_recv.at[3 * a + j],
                device_id=(x, y, 1 - c), device_id_type=MESH)

        local = [pltpu.make_async_copy(ins[a], outs[a].at[k], local_sems.at[a]) for a in range(n)]
        sent = [ici(a, j, k) for a in range(n) for j in range(3)]
        for cp in local + sent:
            cp.start()
        passed = []
        for a in range(n):
            for j, (px, py) in enumerate(chips):
                ici(a, j, 2 * px + py).wait_recv()
                passed.append(d2d(a, j, c))
                passed[-1].start()
        for a in range(n):
            for j in range(3):
                d2d(a, j, 1 - c).wait_recv()
        for cp in sent + passed:
            cp.wait_send()
        for cp in local:
            cp.wait()

    hbm = pl.BlockSpec(memory_space=pltpu.HBM)
    return _comm_call(
        body, name=name, out_shape=[jax.ShapeDtypeStruct((N_CHIP, *s.shape), s.dtype) for s in shards],
        in_specs=[hbm] * n, out_specs=[hbm] * n,
        scratch_shapes=[pltpu.SemaphoreType.DMA((3 * n,))] * 4 + [pltpu.SemaphoreType.DMA((n,))],
    )(*shards)


def _sum_blocks(a, nblk, *, name):
    n = a.shape[1]

    def body(a_ref, o_ref):
        acc = a_ref[pl.ds(0, 8), :]
        for d in range(1, nblk):
            acc = acc + a_ref[pl.ds(8 * d, 8), :]
        o_ref[...] = acc

    return _call(body, name=name, out_shape=jax.ShapeDtypeStruct((8, n), F32))(a)


def _outer8(sct, dm, *, name, swap=(), tm=256, tn=1024):
    R, N = sct.shape[0], dm.shape[1]
    tm, tn = min(tm, R), min(tn, N)
    grid = (R // tm, N // tn)
    ns = len(swap)

    def body(s_ref, d_ref, *rest):
        o_ref = rest[ns]
        x, y, c = lax.axis_index("x"), lax.axis_index("y"), lax.axis_index("c")
        swaps = [pltpu.make_async_remote_copy(
            src_ref=rest[a], dst_ref=rest[ns + 1 + a], send_sem=rest[2 * ns + 1].at[a], recv_sem=rest[2 * ns + 2].at[a],
            device_id=(x, y, 1 - c), device_id_type=MESH) for a in range(ns)]
        i, j = pl.program_id(0), pl.program_id(1)

        @pl.when(jnp.logical_and(i == 0, j == 0))
        def _():
            for cp in swaps:
                cp.start()

        s, dmv = s_ref[...], d_ref[...]
        acc = s[:, 0:1] * dmv[0:1, :]
        for b in range(1, 8):
            acc = acc + s[:, b:b + 1] * dmv[b:b + 1, :]
        o_ref[...] = acc

        @pl.when(jnp.logical_and(i == grid[0] - 1, j == grid[1] - 1))
        def _():
            for cp in swaps:
                cp.wait()

    hbm = pl.BlockSpec(memory_space=pltpu.HBM)
    res = _call(
        body, name=name, grid=grid,
        in_specs=[pl.BlockSpec((tm, 8), lambda i, j: (i, 0)), pl.BlockSpec((8, tn), lambda i, j: (0, j))] + [hbm] * ns,
        out_specs=[pl.BlockSpec((tm, tn), lambda i, j: (i, j))] + [hbm] * ns,
        out_shape=[jax.ShapeDtypeStruct((R, N), F32)] + [jax.ShapeDtypeStruct(a.shape, a.dtype) for a in swap],
        scratch_shapes=[pltpu.SemaphoreType.DMA((ns,)), pltpu.SemaphoreType.DMA((ns,))] if ns else [],
        compiler_params=_cparams(("arbitrary", "arbitrary")),
    )(sct, dm, *swap)
    return res[0], list(res[1:])


_BC1 = 1.0 - ADAM_B1 ** ADAM_STEP
_BC2 = 1.0 - ADAM_B2 ** ADAM_STEP


def _adamw(w, g_parts, m, v, *, name, tile=128):
    R, C = w.shape
    tile = min(tile, R)
    assert R % tile == 0
    npart = len(g_parts)

    def body(*refs):
        w_ref, m_ref, v_ref = refs[npart:npart + 3]
        g_o, d_o, m_o, v_o = refs[npart + 3:]
        g = refs[0][...].astype(F32)
        for r in refs[1:npart]:
            g = g + r[...].astype(F32)
        mn = ADAM_B1 * m_ref[...] + (1.0 - ADAM_B1) * g
        vn = ADAM_B2 * v_ref[...] + (1.0 - ADAM_B2) * jnp.square(g)
        g_o[...] = g
        m_o[...] = mn
        v_o[...] = vn
        d_o[...] = -ADAM_LR * ((mn / _BC1) / (jnp.sqrt(vn / _BC2) + ADAM_EPS) + ADAM_WD * w_ref[...])

    spec = pl.BlockSpec((tile, C), lambda i: (i, 0))
    return _call(
        body, name=name, grid=(R // tile,), in_specs=[spec] * (npart + 3), out_specs=[spec] * 4,
        out_shape=[jax.ShapeDtypeStruct((R, C), F32)] * 4,
        compiler_params=_cparams(("parallel",)),
    )(*g_parts, w, m, v)


def _pack(vecs, rows8_cols):
    flat = jnp.concatenate([v.reshape(-1).astype(F32) for v in vecs])
    return jnp.pad(flat, (0, 8 * rows8_cols - flat.shape[0])).reshape(8, rows8_cols)


def _unpack(flat, shapes):
    out, o = [], 0
    for s in shapes:
        n = math.prod(s)
        out.append(flat[o:o + n].reshape(s))
        o += n
    return out


_BIG = ("w_in", "w_uq", "w_ukv", "w_out", "w_ff1", "w_ff2")
_SMALL = ("b_ada", "norm_mix_g", "b_gates", "conv_w", "conv_b", "q_lora_g", "kv_lora_g", "q_norm_g", "k_norm_g",
          "mlstm_norm_g", "norm_mlp_g")
_ORDER = ("w_ada", "b_ada", "norm_mix_g", "w_in", "b_gates", "conv_w", "conv_b", "q_lora_g", "w_uq", "kv_lora_g",
          "w_ukv", "q_norm_g", "k_norm_g", "mlstm_norm_g", "w_out", "norm_mlp_g", "w_ff1", "w_ff2")


def kernel(x, c, positions, w_ada, b_ada, norm_mix_g, w_in, b_gates, conv_w, conv_b, q_lora_g, w_uq, kv_lora_g, w_ukv, q_norm_g, k_norm_g, mlstm_norm_g, w_out, norm_mlp_g, w_ff1, w_ff2, loss_target, m_w_ada, m_b_ada, m_norm_mix_g, m_w_in, m_b_gates, m_conv_w, m_conv_b, m_q_lora_g, m_w_uq, m_kv_lora_g, m_w_ukv, m_q_norm_g, m_k_norm_g, m_mlstm_norm_g, m_w_out, m_norm_mlp_g, m_w_ff1, m_w_ff2, v_w_ada, v_b_ada, v_norm_mix_g, v_w_in, v_b_gates, v_conv_w, v_conv_b, v_q_lora_g, v_w_uq, v_kv_lora_g, v_w_ukv, v_q_norm_g, v_k_norm_g, v_mlstm_norm_g, v_w_out, v_norm_mlp_g, v_w_ff1, v_w_ff2):
    args = dict(locals())
    wts = {n: args[n] for n in _ORDER}
    mom = {n: args["m_" + n] for n in _ORDER}
    var = {n: args["v_" + n] for n in _ORDER}
    MX = _MXU_DTYPE
    xi, yi, ci = lax.axis_index("x"), lax.axis_index("y"), lax.axis_index("c")
    chip = 2 * xi + yi
    dev = 2 * chip + ci
    S = x.shape[1]
    CS = 2 * MW // N_CHIP
    GS = DM // N_CHIP

    pk = _pack([c, conv_w, mlstm_norm_g], 1024)
    allpk = _allgather8(pk, name="gather_small").reshape(N_DEV, 8 * 1024)
    c_all = allpk[:, :D]
    per_chip = allpk[0::2]
    conv_w_full = per_chip[:, D:D + CONVW * CS].reshape(N_CHIP, CONVW, CS).transpose(1, 0, 2).reshape(CONVW, 2 * MW)
    o = D + CONVW * CS
    mn_full = per_chip[:, o:o + HM * GS].reshape(N_CHIP, HM, GS).transpose(1, 0, 2).reshape(HM, DM)

    (sc,) = _rowmap(lambda cv: ([cv * _sigmoid(cv)], []), [(c_all, D, 0)], [], [(D, F32)], tile=8, name="silu_c")
    ncol = w_ada.shape[2]
    b_cols = lax.dynamic_slice(b_ada, (0, chip * ncol), (1, ncol))
    modp = _mm(sc, w_ada[0], "nn", name="ada_fwd", tm=8, tn=1024, tk=512, extras=(jnp.broadcast_to(b_cols, (8, ncol)),),
               epilogue=lambda r, b: (r + b,))
    modg = _allgather8(modp, name="gather_mod").reshape(N_CHIP, 2, 8, ncol)[:, 0]
    mod_all = modg.transpose(1, 0, 2).reshape(N_DEV, N_CHIP * ncol)
    modv = jnp.pad(lax.dynamic_slice(mod_all, (dev, 0), (1, 6 * D)).reshape(6, D), ((0, 2), (0, 0)))

    shards = [wts[n][0].astype(MX) for n in _BIG]
    gw_in, gw_uq, gw_ukv = _chip_allgather_halved(shards[:3], name="gather_weights")
    W = _prep_weights(gw_in, _cols(gw_uq), _cols(gw_ukv), None, None, None, norm_mix_g, norm_mlp_g,
                      q_lora_g, kv_lora_g, q_norm_g, k_norm_g, mn_full, conv_w_full, conv_b, b_gates)

    loss, gx, dmodv, g = _device_step(x[0], loss_target[0], positions[0], modv, W, late=shards[3:])

    small_shapes = [(6 * D,), (D,), (NG,), (CONVW, 2 * MW), (2 * MW,), (Q_LORA,), (KV_LORA,), (QK,), (QK,), (MW,), (D,), (1,)]
    pg = _pack([dmodv, g["norm_mix_g"], g["b_gates"], g["conv_w"], g["conv_b"], g["q_lora_g"], g["kv_lora_g"],
                g["q_norm_g"], g["k_norm_g"], g["mlstm_norm_g"], g["norm_mlp_g"], loss], 4096)
    allpg = _allgather8(pg, name="gather_small_grads")
    tot = _unpack(_sum_blocks(allpg, N_DEV, name="sum_small_grads").reshape(-1), small_shapes)
    dmod_all = allpg.reshape(N_DEV, 8 * 4096)[:, :6 * D]
    gsmall = dict(zip(_SMALL, [tot[0].reshape(1, 6 * D), tot[1].reshape(1, D), tot[2].reshape(1, NG),
                               lax.dynamic_slice(tot[3], (0, chip * CS), (CONVW, CS)).reshape(1, CONVW, CS),
                               tot[4].reshape(1, 2 * MW), tot[5].reshape(1, Q_LORA), tot[6].reshape(1, KV_LORA),
                               tot[7].reshape(1, QK), tot[8].reshape(1, QK),
                               lax.dynamic_slice(tot[9].reshape(HM, DM), (0, chip * GS), (HM, GS)).reshape(1, HM, GS),
                               tot[10].reshape(1, D)]))
    loss_tot = tot[11].reshape(())

    got = g["got"]
    part = []
    for nme, r in zip(_BIG, got):
        wd = r.shape[2]
        (p,) = _rowmap(lambda a0, a1, a2, a3: ([(a0.astype(F32) + a1.astype(F32)) + (a2.astype(F32) + a3.astype(F32))], []),
                       [(r, wd, 0, k) for k in range(N_CHIP)], [], [(wd, MX)], tile=256, name="sum_chips_" + nme)
        part.append(p)

    dm_cols = lax.dynamic_slice(dmod_all, (0, chip * ncol), (N_DEV, ncol))
    g_ada, other = _outer8(sc.T, dm_cols, name="ada_dw", swap=part)

    res = {}
    for nme, p, q in zip(_BIG, part, other):
        res[nme] = _adamw(wts[nme][0], [p, q], mom[nme][0], var[nme][0], name="adamw_" + nme)
    res["w_ada"] = _adamw(w_ada[0], [g_ada], m_w_ada[0], v_w_ada[0], name="adamw_w_ada")
    sw = _pack([wts[n] for n in _SMALL], 3072)
    sg = _pack([gsmall[n] for n in _SMALL], 3072)
    sm = _pack([mom[n] for n in _SMALL], 3072)
    sv = _pack([var[n] for n in _SMALL], 3072)
    small_res = _adamw(sw, [sg], sm, sv, name="adamw_small", tile=8)
    shapes = [wts[n].shape for n in _SMALL]
    unp = [_unpack(r.reshape(-1), shapes) for r in small_res]
    for i, nme in enumerate(_SMALL):
        res[nme] = tuple(u[i] for u in unp)
    outs = [loss_tot, gx[None]]
    for kind in range(4):
        outs += [res[n][kind].reshape(wts[n].shape) for n in _ORDER]
    return tuple(outs)
```
